```python
import jax, jax.numpy as jnp
from jax import lax
import numpy as np

D_MODEL = 1024
BATCH = 32
SEQ = 2048
DEPTH = 4

CHUNK = 64
N_MIXERS = 2
N_SB = (DEPTH + 1) // 2
N_SGU = DEPTH // 2
SB_HEADS = 16
SB_HEAD_DIM = D_MODEL // SB_HEADS
Q_BLOCK = 128
SGU_CHUNK = 2 * CHUNK
SGU_FFN = 2 * D_MODEL
SGU_GROUPS = 8
SGU_GROUP_W = SGU_FFN // SGU_GROUPS
MLP_HIDDEN = 4 * D_MODEL
EPS = 1e-6

kernel_name = "hybrid_stickbreak_sgu_encoder"


def rmsnorm(x, gain):
    x32 = x.astype(jnp.float32)
    y = x32 * lax.rsqrt(jnp.mean(x32 * x32, axis=-1, keepdims=True) + EPS)
    return (y * gain.astype(jnp.float32)).astype(x.dtype)


def stick_breaking_attention(q, k, v):
    seq = q.shape[2]
    scale = SB_HEAD_DIM ** -0.5
    outs = []
    for blk in range(seq // Q_BLOCK):
        t0 = blk * Q_BLOCK
        t1 = t0 + Q_BLOCK
        qb = q[:, :, t0:t1].astype(jnp.float32)
        kb = k[:, :, :t1].astype(jnp.float32)
        vb = v[:, :, :t1].astype(jnp.float32)
        z = jnp.einsum('bhtd,bhsd->bhts', qb, kb) * scale
        t_idx = t0 + jnp.arange(Q_BLOCK)[:, None]
        s_idx = jnp.arange(t1)[None, :]
        past = s_idx < t_idx
        log_beta = jax.nn.log_sigmoid(z)
        log_one_minus = jnp.where(past, log_beta - z, 0.0)
        suffix = lax.cumsum(log_one_minus, axis=3, reverse=True) - log_one_minus
        a = jnp.where(past, jnp.exp(log_beta + suffix), 0.0)
        outs.append(jnp.einsum('bhts,bhsd->bhtd', a, vb))
    return jnp.concatenate(outs, axis=2).astype(q.dtype)


def stick_breaking_mixer(h, w_qkv, w_o):
    b, s, _ = h.shape
    qkv = (h @ w_qkv).reshape(b, s, 3, SB_HEADS, SB_HEAD_DIM)
    q = jnp.transpose(qkv[:, :, 0], (0, 2, 1, 3))
    k = jnp.transpose(qkv[:, :, 1], (0, 2, 1, 3))
    v = jnp.transpose(qkv[:, :, 2], (0, 2, 1, 3))
    o = stick_breaking_attention(q, k, v)
    o = jnp.transpose(o, (0, 2, 1, 3)).reshape(b, s, D_MODEL)
    return o @ w_o


def spatial_gating_mixer(h, w_in, gain, w_s, b_s, w_out):
    b, s, _ = h.shape
    uv = jax.nn.gelu(h @ w_in)
    u, v = uv[..., :SGU_FFN], uv[..., SGU_FFN:]
    v = rmsnorm(v, gain)
    vc = v.reshape(b, s // SGU_CHUNK, SGU_CHUNK, SGU_GROUPS, SGU_GROUP_W)
    causal = jnp.tril(jnp.ones((SGU_CHUNK, SGU_CHUNK), dtype=bool))
    ws = jnp.where(causal[None], w_s, 0.0).astype(v.dtype)
    mixed = jnp.einsum('gts,bnsgc->bntgc', ws, vc) + jnp.transpose(b_s)[None, None, :, :, None]
    y = u * mixed.reshape(b, s, SGU_FFN)
    return y @ w_out


def squared_relu_mlp(h, w1, w2):
    return jnp.square(jax.nn.relu(h @ w1)) @ w2


def _fwd_setup_inputs(seed: int = 0) -> dict:
    key = jax.random.key(seed)
    ks = jax.random.split(key, 14)
    f32 = jnp.float32
    nrm = lambda k, shape, scale: jax.random.normal(k, shape, f32) * scale
    return {
        "x": jax.random.normal(ks[0], (BATCH, SEQ, D_MODEL), f32),
        "norm_mix": 1.0 + nrm(ks[1], (DEPTH, D_MODEL), 0.02),
        "norm_mlp": 1.0 + nrm(ks[2], (DEPTH, D_MODEL), 0.02),
        "sb_wqkv": nrm(ks[3], (N_SB, D_MODEL, 3 * D_MODEL), D_MODEL ** -0.5),
        "sb_wo": nrm(ks[4], (N_SB, D_MODEL, D_MODEL), D_MODEL ** -0.5),
        "sgu_win": nrm(ks[5], (N_SGU, D_MODEL, 2 * SGU_FFN), D_MODEL ** -0.5),
        "sgu_gain": 1.0 + nrm(ks[6], (N_SGU, SGU_FFN), 0.02),
        "sgu_ws": nrm(ks[7], (N_SGU, SGU_GROUPS, SGU_CHUNK, SGU_CHUNK), SGU_CHUNK ** -0.5),
        "sgu_bs": 1.0 + nrm(ks[8], (N_SGU, SGU_GROUPS, SGU_CHUNK), 0.02),
        "sgu_wout": nrm(ks[9], (N_SGU, SGU_FFN, D_MODEL), SGU_FFN ** -0.5),
        "mlp_w1": nrm(ks[10], (DEPTH, D_MODEL, MLP_HIDDEN), D_MODEL ** -0.5),
        "mlp_w2": nrm(ks[11], (DEPTH, MLP_HIDDEN, D_MODEL), 0.5 * MLP_HIDDEN ** -0.5),
        "final_norm": 1.0 + nrm(ks[12], (D_MODEL,), 0.02),
    }


def _fwd_reference(x, norm_mix, norm_mlp, sb_wqkv, sb_wo, sgu_win, sgu_gain, sgu_ws,
              sgu_bs, sgu_wout, mlp_w1, mlp_w2, final_norm):
    for i in range(DEPTH):
        h = rmsnorm(x, norm_mix[i])
        j = i // N_MIXERS
        if i % N_MIXERS == 0:
            x = x + stick_breaking_mixer(h, sb_wqkv[j], sb_wo[j])
        else:
            x = x + spatial_gating_mixer(h, sgu_win[j], sgu_gain[j], sgu_ws[j],
                                         sgu_bs[j], sgu_wout[j])
        x = x + squared_relu_mlp(rmsnorm(x, norm_mlp[i]), mlp_w1[i], mlp_w2[i])
    return rmsnorm(x, final_norm)


import jax as _jax
import jax.numpy as _jnp

TWIN_FORMAT = 'train_step'
FWD_PARAMS = ['x', 'norm_mix', 'norm_mlp', 'sb_wqkv', 'sb_wo', 'sgu_win', 'sgu_gain', 'sgu_ws', 'sgu_bs', 'sgu_wout', 'mlp_w1', 'mlp_w2', 'final_norm']
TWIN_WEIGHTS = ['norm_mix', 'norm_mlp', 'sb_wqkv', 'sb_wo', 'sgu_win', 'sgu_gain', 'sgu_ws', 'sgu_bs', 'sgu_wout', 'mlp_w1', 'mlp_w2', 'final_norm']
TWIN_DIFF_INPUT = 'x'
TWIN_INPUTS = ['x', 'norm_mix', 'norm_mlp', 'sb_wqkv', 'sb_wo', 'sgu_win', 'sgu_gain', 'sgu_ws', 'sgu_bs', 'sgu_wout', 'mlp_w1', 'mlp_w2', 'final_norm', 'loss_target', 'm_norm_mix', 'm_norm_mlp', 'm_sb_wqkv', 'm_sb_wo', 'm_sgu_win', 'm_sgu_gain', 'm_sgu_ws', 'm_sgu_bs', 'm_sgu_wout', 'm_mlp_w1', 'm_mlp_w2', 'm_final_norm', 'v_norm_mix', 'v_norm_mlp', 'v_sb_wqkv', 'v_sb_wo', 'v_sgu_win', 'v_sgu_gain', 'v_sgu_ws', 'v_sgu_bs', 'v_sgu_wout', 'v_mlp_w1', 'v_mlp_w2', 'v_final_norm']
TWIN_OUTPUTS = ['loss', 'grad_x', 'grad_norm_mix', 'grad_norm_mlp', 'grad_sb_wqkv', 'grad_sb_wo', 'grad_sgu_win', 'grad_sgu_gain', 'grad_sgu_ws', 'grad_sgu_bs', 'grad_sgu_wout', 'grad_mlp_w1', 'grad_mlp_w2', 'grad_final_norm', 'delta_norm_mix', 'delta_norm_mlp', 'delta_sb_wqkv', 'delta_sb_wo', 'delta_sgu_win', 'delta_sgu_gain', 'delta_sgu_ws', 'delta_sgu_bs', 'delta_sgu_wout', 'delta_mlp_w1', 'delta_mlp_w2', 'delta_final_norm', 'new_m_norm_mix', 'new_m_norm_mlp', 'new_m_sb_wqkv', 'new_m_sb_wo', 'new_m_sgu_win', 'new_m_sgu_gain', 'new_m_sgu_ws', 'new_m_sgu_bs', 'new_m_sgu_wout', 'new_m_mlp_w1', 'new_m_mlp_w2', 'new_m_final_norm', 'new_v_norm_mix', 'new_v_norm_mlp', 'new_v_sb_wqkv', 'new_v_sb_wo', 'new_v_sgu_win', 'new_v_sgu_gain', 'new_v_sgu_ws', 'new_v_sgu_bs', 'new_v_sgu_wout', 'new_v_mlp_w1', 'new_v_mlp_w2', 'new_v_final_norm']
TWIN_LEAF_KINDS = {'loss': 'loss', 'grad_x': 'grad_x', 'grad_norm_mix': 'grad_w', 'grad_norm_mlp': 'grad_w', 'grad_sb_wqkv': 'grad_w', 'grad_sb_wo': 'grad_w', 'grad_sgu_win': 'grad_w', 'grad_sgu_gain': 'grad_w', 'grad_sgu_ws': 'grad_w', 'grad_sgu_bs': 'grad_w', 'grad_sgu_wout': 'grad_w', 'grad_mlp_w1': 'grad_w', 'grad_mlp_w2': 'grad_w', 'grad_final_norm': 'grad_w', 'delta_norm_mix': 'delta_w', 'delta_norm_mlp': 'delta_w', 'delta_sb_wqkv': 'delta_w', 'delta_sb_wo': 'delta_w', 'delta_sgu_win': 'delta_w', 'delta_sgu_gain': 'delta_w', 'delta_sgu_ws': 'delta_w', 'delta_sgu_bs': 'delta_w', 'delta_sgu_wout': 'delta_w', 'delta_mlp_w1': 'delta_w', 'delta_mlp_w2': 'delta_w', 'delta_final_norm': 'delta_w', 'new_m_norm_mix': 'new_m', 'new_m_norm_mlp': 'new_m', 'new_m_sb_wqkv': 'new_m', 'new_m_sb_wo': 'new_m', 'new_m_sgu_win': 'new_m', 'new_m_sgu_gain': 'new_m', 'new_m_sgu_ws': 'new_m', 'new_m_sgu_bs': 'new_m', 'new_m_sgu_wout': 'new_m', 'new_m_mlp_w1': 'new_m', 'new_m_mlp_w2': 'new_m', 'new_m_final_norm': 'new_m', 'new_v_norm_mix': 'new_v', 'new_v_norm_mlp': 'new_v', 'new_v_sb_wqkv': 'new_v', 'new_v_sb_wo': 'new_v', 'new_v_sgu_win': 'new_v', 'new_v_sgu_gain': 'new_v', 'new_v_sgu_ws': 'new_v', 'new_v_sgu_bs': 'new_v', 'new_v_sgu_wout': 'new_v', 'new_v_mlp_w1': 'new_v', 'new_v_mlp_w2': 'new_v', 'new_v_final_norm': 'new_v'}


def _forward(args):
    return _fwd_reference(*[args[k] for k in FWD_PARAMS])


def _output_shape():
    out = _jax.eval_shape(lambda: _forward(_fwd_setup_inputs(0)))
    return out.shape, out.dtype

N_MICROBATCH = 1
ADAM_LR = 0.001
ADAM_B1 = 0.9
ADAM_B2 = 0.999
ADAM_EPS = 1e-08
ADAM_WD = 0.01
ADAM_STEP = 10
PER_EXAMPLE_BATCH_AXIS = {'x': 0, 'loss_target': 0}
SHARED_INPUTS = []
_WEIGHT_DTYPES = {'norm_mix': _jnp.float32, 'norm_mlp': _jnp.float32, 'sb_wqkv': _jnp.float32, 'sb_wo': _jnp.float32, 'sgu_win': _jnp.float32, 'sgu_gain': _jnp.float32, 'sgu_ws': _jnp.float32, 'sgu_bs': _jnp.float32, 'sgu_wout': _jnp.float32, 'mlp_w1': _jnp.float32, 'mlp_w2': _jnp.float32, 'final_norm': _jnp.float32}
MOMENT_SCALE = {'norm_mix': 1.583398e-01, 'norm_mlp': 1.172890e-01, 'sb_wqkv': 9.646440e-02, 'sb_wo': 1.435789e-01, 'sgu_win': 7.354959e-02, 'sgu_gain': 5.174918e-02, 'sgu_ws': 6.962811e-02, 'sgu_bs': 1.007825e-01, 'sgu_wout': 1.338340e-01, 'mlp_w1': 5.824656e-02, 'mlp_w2': 2.250639e-01, 'final_norm': 6.479949e+01}


def _to_microbatches(a, axis):
    t = _jnp.moveaxis(a, axis, 0)
    t = t.reshape((N_MICROBATCH, t.shape[0] // N_MICROBATCH) + t.shape[1:])
    return _jnp.moveaxis(t, 1, axis + 1)


def setup_inputs(seed: int = 0) -> dict:
    inp = _fwd_setup_inputs(seed)
    key = _jax.random.fold_in(_jax.random.key(seed), 7919)
    shape, _ = _output_shape()
    out = dict(inp)
    out["loss_target"] = _jax.random.normal(_jax.random.fold_in(key, 0), shape, _jnp.float32)
    for i, name in enumerate(TWIN_WEIGHTS):
        w = inp[name].astype(_jnp.float32)
        if MOMENT_SCALE is None:
            s = _jnp.sqrt(_jnp.mean(_jnp.square(w)) + 1e-30)
        else:
            s = MOMENT_SCALE[name]
        km, kv = _jax.random.split(_jax.random.fold_in(key, i + 1))
        out[name] = w
        out["m_" + name] = s * _jax.random.normal(km, w.shape, _jnp.float32)
        out["v_" + name] = (s * s) * _jax.random.uniform(kv, w.shape, _jnp.float32, 0.5, 1.5)
    if N_MICROBATCH > 1:
        for name, axis in PER_EXAMPLE_BATCH_AXIS.items():
            out[name] = _to_microbatches(out[name], axis)
    return {'x': out['x'], 'norm_mix': out['norm_mix'], 'norm_mlp': out['norm_mlp'], 'sb_wqkv': out['sb_wqkv'], 'sb_wo': out['sb_wo'], 'sgu_win': out['sgu_win'], 'sgu_gain': out['sgu_gain'], 'sgu_ws': out['sgu_ws'], 'sgu_bs': out['sgu_bs'], 'sgu_wout': out['sgu_wout'], 'mlp_w1': out['mlp_w1'], 'mlp_w2': out['mlp_w2'], 'final_norm': out['final_norm'], 'loss_target': out['loss_target'], 'm_norm_mix': out['m_norm_mix'], 'm_norm_mlp': out['m_norm_mlp'], 'm_sb_wqkv': out['m_sb_wqkv'], 'm_sb_wo': out['m_sb_wo'], 'm_sgu_win': out['m_sgu_win'], 'm_sgu_gain': out['m_sgu_gain'], 'm_sgu_ws': out['m_sgu_ws'], 'm_sgu_bs': out['m_sgu_bs'], 'm_sgu_wout': out['m_sgu_wout'], 'm_mlp_w1': out['m_mlp_w1'], 'm_mlp_w2': out['m_mlp_w2'], 'm_final_norm': out['m_final_norm'], 'v_norm_mix': out['v_norm_mix'], 'v_norm_mlp': out['v_norm_mlp'], 'v_sb_wqkv': out['v_sb_wqkv'], 'v_sb_wo': out['v_sb_wo'], 'v_sgu_win': out['v_sgu_win'], 'v_sgu_gain': out['v_sgu_gain'], 'v_sgu_ws': out['v_sgu_ws'], 'v_sgu_bs': out['v_sgu_bs'], 'v_sgu_wout': out['v_sgu_wout'], 'v_mlp_w1': out['v_mlp_w1'], 'v_mlp_w2': out['v_mlp_w2'], 'v_final_norm': out['v_final_norm']}


def _loss(weights, diff, rest, loss_target):
    with _jax.named_scope("forward"):
        args = {**rest, TWIN_DIFF_INPUT: diff, **{k: w.astype(_WEIGHT_DTYPES[k]) for k, w in weights.items()}}
        y = _forward(args)
    with _jax.named_scope("loss_head"):
        err = _jnp.square(y.astype(_jnp.float32) - loss_target)
        return 0.5 * _jnp.sum(_jnp.mean(err, axis=-1)) if err.ndim else 0.5 * err


def _adamw(w, g, m, v):
    m = ADAM_B1 * m + (1.0 - ADAM_B1) * g
    v = ADAM_B2 * v + (1.0 - ADAM_B2) * _jnp.square(g)
    m_hat = m / (1.0 - ADAM_B1 ** ADAM_STEP)
    v_hat = v / (1.0 - ADAM_B2 ** ADAM_STEP)
    delta = -ADAM_LR * (m_hat / (_jnp.sqrt(v_hat) + ADAM_EPS) + ADAM_WD * w)
    return delta, m, v


def reference(x, norm_mix, norm_mlp, sb_wqkv, sb_wo, sgu_win, sgu_gain, sgu_ws, sgu_bs, sgu_wout, mlp_w1, mlp_w2, final_norm, loss_target, m_norm_mix, m_norm_mlp, m_sb_wqkv, m_sb_wo, m_sgu_win, m_sgu_gain, m_sgu_ws, m_sgu_bs, m_sgu_wout, m_mlp_w1, m_mlp_w2, m_final_norm, v_norm_mix, v_norm_mlp, v_sb_wqkv, v_sb_wo, v_sgu_win, v_sgu_gain, v_sgu_ws, v_sgu_bs, v_sgu_wout, v_mlp_w1, v_mlp_w2, v_final_norm):
    given = dict(x=x, norm_mix=norm_mix, norm_mlp=norm_mlp, sb_wqkv=sb_wqkv, sb_wo=sb_wo, sgu_win=sgu_win, sgu_gain=sgu_gain, sgu_ws=sgu_ws, sgu_bs=sgu_bs, sgu_wout=sgu_wout, mlp_w1=mlp_w1, mlp_w2=mlp_w2, final_norm=final_norm, loss_target=loss_target, m_norm_mix=m_norm_mix, m_norm_mlp=m_norm_mlp, m_sb_wqkv=m_sb_wqkv, m_sb_wo=m_sb_wo, m_sgu_win=m_sgu_win, m_sgu_gain=m_sgu_gain, m_sgu_ws=m_sgu_ws, m_sgu_bs=m_sgu_bs, m_sgu_wout=m_sgu_wout, m_mlp_w1=m_mlp_w1, m_mlp_w2=m_mlp_w2, m_final_norm=m_final_norm, v_norm_mix=v_norm_mix, v_norm_mlp=v_norm_mlp, v_sb_wqkv=v_sb_wqkv, v_sb_wo=v_sb_wo, v_sgu_win=v_sgu_win, v_sgu_gain=v_sgu_gain, v_sgu_ws=v_sgu_ws, v_sgu_bs=v_sgu_bs, v_sgu_wout=v_sgu_wout, v_mlp_w1=v_mlp_w1, v_mlp_w2=v_mlp_w2, v_final_norm=v_final_norm)
    weights = {n: given[n] for n in TWIN_WEIGHTS}
    shared = {n: given[n] for n in SHARED_INPUTS}
    per_example = {n: given[n] for n in ['x']}
    grad_fn = _jax.value_and_grad(_loss, argnums=(0, 1))

    def one_microbatch(ex, loss_target):
        ex = dict(ex)
        diff = ex.pop(TWIN_DIFF_INPUT)
        return grad_fn(weights, diff, {**shared, **ex}, loss_target)

    if N_MICROBATCH == 1:
        loss, (grad_w, grad_x) = one_microbatch(per_example, given["loss_target"])
    else:
        def body(carry, xs):
            loss_sum, grad_sum = carry
            l_k, (gw_k, gx_k) = one_microbatch(xs[0], xs[1])
            with _jax.named_scope("update"):
                return (loss_sum + l_k, _jax.tree.map(_jnp.add, grad_sum, gw_k)), gx_k

        init = (_jnp.zeros((), _jnp.float32), _jax.tree.map(_jnp.zeros_like, weights))
        (loss, grad_w), grad_x = _jax.lax.scan(body, init, (per_example, given["loss_target"]))
    with _jax.named_scope("update"):
        delta_w, new_m, new_v = {}, {}, {}
        for n in TWIN_WEIGHTS:
            delta_w[n], new_m[n], new_v[n] = _adamw(weights[n], grad_w[n], given["m_" + n], given["v_" + n])
    return (loss, grad_x, *[grad_w[n] for n in TWIN_WEIGHTS], *[delta_w[n] for n in TWIN_WEIGHTS],
            *[new_m[n] for n in TWIN_WEIGHTS], *[new_v[n] for n in TWIN_WEIGHTS])
```

```python
import functools

import jax
import jax.numpy as jnp
from jax import lax
from jax.experimental import pallas as pl
from jax.experimental.pallas import tpu as pltpu

F32 = jnp.float32
BF16 = jnp.bfloat16

N_DEV = 8
D_MODEL = 1024
SEQ = 2048
DEPTH = 4
SB_HEAD_DIM = 64
SGU_CHUNK = 128
SGU_FFN = 2 * D_MODEL
SGU_GROUPS = 8
SGU_GROUP_W = SGU_FFN // SGU_GROUPS
EPS = 1e-6

ADAM_LR = 0.001
ADAM_B1 = 0.9
ADAM_B2 = 0.999
ADAM_EPS = 1e-08
ADAM_WD = 0.01
ADAM_STEP = 10

LANES = 128
SUBLANES = 8
VMEM_LIMIT = 48 * 1024 * 1024
EXP_ZERO_BELOW = -104.0

MESH = pl.DeviceIdType.MESH
ANY = pl.BlockSpec(memory_space=pl.ANY)


def _params(*sem):
    return pltpu.CompilerParams(dimension_semantics=sem, vmem_limit_bytes=VMEM_LIMIT)


def _dot(a, b):
    return jnp.dot(a, b, preferred_element_type=F32)


def _dot_nt(a, b):
    return lax.dot_general(a, b, (((1,), (1,)), ((), ())), preferred_element_type=F32)


def _dot_tn(a, b):
    return lax.dot_general(a, b, (((0,), (0,)), ((), ())), preferred_element_type=F32)


def _split_bf16(x):
    hi = x.astype(BF16)
    lo = (x - hi.astype(F32)).astype(BF16)
    return hi, lo


def _acc_steps(part, acc, k, nk, finish):
    if nk == 1:
        finish(part)
        return

    @pl.when(k == 0)
    def _():
        acc[...] = part

    @pl.when(k > 0)
    def _():
        acc[...] += part

    @pl.when(k == nk - 1)
    def _():
        finish(acc[...])


def _mm_nn(a, w, l, *, out_dtype, name, res=None, a_act=None, tm=1024, tn=512, tk=1024):
    M, K = a.shape
    _, P, K2, n = w.shape
    assert K2 == K
    N = P * n
    tm, tn, tk = min(tm, M), min(tn, n), min(tk, K)
    assert M % tm == 0 and n % tn == 0 and K % tk == 0
    jn, nk = n // tn, K // tk

    def body(*refs):
        if res is not None:
            a_ref, w_ref, r_ref, o_ref, acc = refs
        else:
            a_ref, w_ref, o_ref, acc = refs
        av = a_ref[...]
        if a_act == "relu2":
            t = jnp.maximum(av.astype(F32), 0.0)
            av = t * t

        def finish(r):
            if res is not None:
                r = r + r_ref[...]
            o_ref[...] = r.astype(out_dtype)

        _acc_steps(_dot(av.astype(BF16), w_ref[0, 0]), acc, pl.program_id(2), nk, finish)

    in_specs = [pl.BlockSpec((tm, tk), lambda i, j, k: (i, k)),
                pl.BlockSpec((1, 1, tk, tn), lambda i, j, k: (l, j // jn, k, j % jn))]
    args = [a, w]
    if res is not None:
        in_specs.append(pl.BlockSpec((tm, tn), lambda i, j, k: (i, j)))
        args.append(res)
    return pl.pallas_call(
        body, grid=(M // tm, N // tn, nk), in_specs=in_specs,
        out_specs=pl.BlockSpec((tm, tn), lambda i, j, k: (i, j)),
        out_shape=jax.ShapeDtypeStruct((M, N), out_dtype),
        scratch_shapes=[pltpu.VMEM((tm, tn), F32)],
        compiler_params=_params("parallel", "parallel", "arbitrary"), name=name)(*args)


def _mm_nt(a, w, l, *, out_dtype, name, act_src=None, tm=1024, tn=512, tk=1024):
    M, K = a.shape
    _, P, Nout, kc = w.shape
    assert K == P * kc
    tm, tn, tk = min(tm, M), min(tn, Nout), min(tk, kc)
    assert M % tm == 0 and Nout % tn == 0 and kc % tk == 0
    kn, nk = kc // tk, K // tk

    def body(*refs):
        if act_src is not None:
            a_ref, w_ref, s_ref, o_ref, acc = refs
        else:
            a_ref, w_ref, o_ref, acc = refs

        def finish(r):
            if act_src is not None:
                r = r * (2.0 * jnp.maximum(s_ref[...].astype(F32), 0.0))
            o_ref[...] = r.astype(out_dtype)

        _acc_steps(_dot_nt(a_ref[...].astype(BF16), w_ref[0, 0]), acc, pl.program_id(2), nk, finish)

    in_specs = [pl.BlockSpec((tm, tk), lambda i, j, k: (i, k)),
                pl.BlockSpec((1, 1, tn, tk), lambda i, j, k: (l, k // kn, j, k % kn))]
    args = [a, w]
    if act_src is not None:
        in_specs.append(pl.BlockSpec((tm, tn), lambda i, j, k: (i, j)))
        args.append(act_src)
    return pl.pallas_call(
        body, grid=(M // tm, Nout // tn, nk), in_specs=in_specs,
        out_specs=pl.BlockSpec((tm, tn), lambda i, j, k: (i, j)),
        out_shape=jax.ShapeDtypeStruct((M, Nout), out_dtype),
        scratch_shapes=[pltpu.VMEM((tm, tn), F32)],
        compiler_params=_params("parallel", "parallel", "arbitrary"), name=name)(*args)


def _mm_tn(a, b, stack, l, *, layers, shards, name, a_act=None, tm=512, tk=1024, tn=512):
    M, K = a.shape
    M2, N = b.shape
    assert M2 == M
    n = N // shards
    tm, tk, tn = min(tm, M), min(tk, K), min(tn, n)
    assert M % tm == 0 and K % tk == 0 and n % tn == 0
    jn, nm = n // tn, M // tm

    def body(*refs):
        a_ref, b_ref = refs[:2]
        o_ref, acc = refs[-2:]
        av = a_ref[...]
        if a_act == "relu2":
            t = jnp.maximum(av.astype(F32), 0.0)
            av = t * t

        def finish(r):
            o_ref[0, 0] = r.astype(BF16)

        _acc_steps(_dot_tn(av.astype(BF16), b_ref[...].astype(BF16)), acc, pl.program_id(2), nm, finish)

    in_specs = [pl.BlockSpec((tm, tk), lambda i, j, m: (m, i)),
                pl.BlockSpec((tm, tn), lambda i, j, m: (m, j))]
    args = [a, b]
    aliases = {}
    if stack is not None:
        in_specs.append(ANY)
        args.append(stack)
        aliases = {2: 0}
    return pl.pallas_call(
        body, grid=(K // tk, N // tn, nm), in_specs=in_specs,
        out_specs=pl.BlockSpec((1, 1, tk, tn), lambda i, j, m: (l, j // jn, i, j % jn)),
        out_shape=jax.ShapeDtypeStruct((layers, shards, K, n), BF16),
        scratch_shapes=[pltpu.VMEM((tk, tn), F32)], input_output_aliases=aliases,
        compiler_params=_params("parallel", "parallel", "arbitrary"), name=name)(*args)


def _rms_fwd(x, gain, *, name, tr=512):
    T, D = x.shape

    def body(x_ref, g_ref, o_ref):
        xv = x_ref[...]
        r = lax.rsqrt(jnp.mean(xv * xv, axis=-1, keepdims=True) + EPS)
        o_ref[...] = (xv * r * g_ref[...]).astype(BF16)

    return pl.pallas_call(
        body, grid=(T // tr,),
        in_specs=[pl.BlockSpec((tr, D), lambda i: (i, 0)), pl.BlockSpec((1, D), lambda i: (0, 0))],
        out_specs=pl.BlockSpec((tr, D), lambda i: (i, 0)),
        out_shape=jax.ShapeDtypeStruct((T, D), BF16),
        compiler_params=_params("parallel"), name=name)(x, gain)


def _rms_bwd(dh, x, gain, dres, *, name, tr=512):
    T, D = x.shape
    nr = T // tr

    def body(dh_ref, x_ref, g_ref, dres_ref, dx_ref, dg_ref, acc):
        i = pl.program_id(0)
        xv = x_ref[...]
        r = lax.rsqrt(jnp.mean(xv * xv, axis=-1, keepdims=True) + EPS)
        xhat = xv * r
        dhv = dh_ref[...].astype(F32)
        dxhat = dhv * g_ref[...]
        dx_ref[...] = dres_ref[...] + r * (dxhat - xhat * jnp.mean(dxhat * xhat, axis=-1, keepdims=True))
        part = jnp.sum((dhv * xhat).reshape(tr // SUBLANES, SUBLANES, D), axis=0)

        @pl.when(i == 0)
        def _():
            acc[...] = part

        @pl.when(i > 0)
        def _():
            acc[...] += part

        @pl.when(i == nr - 1)
        def _():
            dg_ref[...] = jnp.sum(acc[...], axis=0, keepdims=True)

    row = pl.BlockSpec((tr, D), lambda i: (i, 0))
    vec = pl.BlockSpec((1, D), lambda i: (0, 0))
    return pl.pallas_call(
        body, grid=(nr,), in_specs=[row, row, vec, row], out_specs=[row, vec],
        out_shape=[jax.ShapeDtypeStruct((T, D), F32), jax.ShapeDtypeStruct((1, D), F32)],
        scratch_shapes=[pltpu.VMEM((SUBLANES, D), F32)],
        compiler_params=_params("arbitrary"), name=name)(dh, x, gain, dres)


def _loss_head(x, gain, target, *, name, tr=512):
    T, D = x.shape
    nr = T // tr

    def body(x_ref, g_ref, t_ref, dx_ref, sq_ref, dg_ref, sq_acc, dg_acc):
        i = pl.program_id(0)
        xv = x_ref[...]
        g = g_ref[...]
        r = lax.rsqrt(jnp.mean(xv * xv, axis=-1, keepdims=True) + EPS)
        xhat = xv * r
        err = xhat * g - t_ref[...]
        dy = err * (1.0 / D)
        dxhat = dy * g
        dx_ref[...] = r * (dxhat - xhat * jnp.mean(dxhat * xhat, axis=-1, keepdims=True))
        sq = jnp.sum((err * err).reshape(tr // SUBLANES, SUBLANES, D), axis=0)
        dg = jnp.sum((dy * xhat).reshape(tr // SUBLANES, SUBLANES, D), axis=0)

        @pl.when(i == 0)
        def _():
            sq_acc[...] = sq
            dg_acc[...] = dg

        @pl.when(i > 0)
        def _():
            sq_acc[...] += sq
            dg_acc[...] += dg

        @pl.when(i == nr - 1)
        def _():
            sq_ref[...] = sq_acc[...]
            dg_ref[...] = jnp.sum(dg_acc[...], axis=0, keepdims=True)

    row = pl.BlockSpec((tr, D), lambda i: (i, 0))
    vec = pl.BlockSpec((1, D), lambda i: (0, 0))
    part = pl.BlockSpec((SUBLANES, D), lambda i: (0, 0))
    return pl.pallas_call(
        body, grid=(nr,), in_specs=[row, vec, row], out_specs=[row, part, vec],
        out_shape=[jax.ShapeDtypeStruct((T, D), F32), jax.ShapeDtypeStruct((SUBLANES, D), F32),
                   jax.ShapeDtypeStruct((1, D), F32)],
        scratch_shapes=[pltpu.VMEM((SUBLANES, D), F32), pltpu.VMEM((SUBLANES, D), F32)],
        compiler_params=_params("arbitrary"), name=name)(x, gain, target)


def _sb_block(z, past):
    e = jnp.exp(-jnp.abs(z))
    l = -(jnp.maximum(z, 0.0) + jnp.log(1.0 + e))
    if past is not None:
        l = jnp.where(past, l, 0.0)
    return l, e


def _suffix_matrix(tk):
    j = lax.broadcasted_iota(jnp.int32, (tk, tk), 0)
    s = lax.broadcasted_iota(jnp.int32, (tk, tk), 1)
    return (j > s).astype(BF16)


def _suffix_sum(x, u):
    hi, lo = _split_bf16(x)
    return _dot(hi, u) + _dot(lo, u)


def _sb_fwd(qkv, *, batch, seq, name, tq=256):
    T, D3 = qkv.shape
    D = D3 // 3
    nhp = D // LANES
    tk = tq
    nq = seq // tq
    scale = SB_HEAD_DIM ** -0.5

    def body(q_ref, k_ref, v_ref, o_ref, acc):
        qi = pl.program_id(2)
        q = q_ref[...]
        lane = lax.broadcasted_iota(jnp.int32, (1, LANES), 1)
        row = lax.broadcasted_iota(jnp.int32, (tq, tk), 0)
        col = lax.broadcasted_iota(jnp.int32, (tq, tk), 1)
        past_diag = col < row
        u = _suffix_matrix(tk)
        acc[...] = jnp.zeros_like(acc)
        for h in range(LANES // SB_HEAD_DIM):
            hm = (lane // SB_HEAD_DIM) == h
            qh = jnp.where(hm, q, jnp.zeros_like(q))

            def block(kb, c, past):
                ks = pl.multiple_of(kb * tk, tk)
                kblk = k_ref[pl.ds(ks, tk), :]
                vblk = v_ref[pl.ds(ks, tk), :]
                z = _dot_nt(qh, kblk) * scale
                l, _ = _sb_block(z, past)
                a = jnp.exp(z + l + _suffix_sum(l, u) + c)
                if past is not None:
                    a = jnp.where(past, a, 0.0)
                a_hi, a_lo = _split_bf16(a)
                vh = jnp.where(hm, vblk, jnp.zeros_like(vblk))
                acc[...] += _dot(a_hi, vh) + _dot(a_lo, vh)
                return c + jnp.sum(l, axis=1, keepdims=True)

            c0 = block(qi, jnp.zeros((tq, 1), F32), past_diag)

            def cond(st):
                kb, c = st
                return jnp.logical_and(kb >= 0, jnp.max(c) > EXP_ZERO_BELOW)

            def step(st):
                kb, c = st
                return kb - 1, block(kb, c, None)

            lax.while_loop(cond, step, (qi - 1, c0))
        o_ref[...] = acc[...]

    return pl.pallas_call(
        body, grid=(batch, nhp, nq),
        in_specs=[pl.BlockSpec((tq, LANES), lambda b, p, i: (b * nq + i, p)),
                  pl.BlockSpec((seq, LANES), lambda b, p, i: (b, nhp + p)),
                  pl.BlockSpec((seq, LANES), lambda b, p, i: (b, 2 * nhp + p))],
        out_specs=pl.BlockSpec((tq, LANES), lambda b, p, i: (b * nq + i, p)),
        out_shape=jax.ShapeDtypeStruct((T, D), F32),
        scratch_shapes=[pltpu.VMEM((tq, LANES), F32)],
        compiler_params=_params("parallel", "parallel", "arbitrary"), name=name)(qkv, qkv, qkv)


def _sb_bwd(qkv, o, do, *, batch, seq, name, tq=256):
    T, D3 = qkv.shape
    D = D3 // 3
    nhp = D // LANES
    tk = tq
    nq = seq // tq
    scale = SB_HEAD_DIM ** -0.5

    def body(q_ref, k_ref, v_ref, o_ref, do_ref, dq_ref, dk_ref, dv_ref, dq_acc, dk_acc, dv_acc):
        qi = pl.program_id(2)

        @pl.when(qi == 0)
        def _():
            dk_acc[...] = jnp.zeros_like(dk_acc)
            dv_acc[...] = jnp.zeros_like(dv_acc)

        q = q_ref[...]
        dov = do_ref[...]
        dod = dov.astype(F32) * o_ref[...]
        lane = lax.broadcasted_iota(jnp.int32, (1, LANES), 1)
        row = lax.broadcasted_iota(jnp.int32, (tq, tk), 0)
        col = lax.broadcasted_iota(jnp.int32, (tq, tk), 1)
        past_diag = col < row
        u = _suffix_matrix(tk)
        dq_acc[...] = jnp.zeros_like(dq_acc)
        for h in range(LANES // SB_HEAD_DIM):
            hm = (lane // SB_HEAD_DIM) == h
            qh = jnp.where(hm, q, jnp.zeros_like(q))
            doh = jnp.where(hm, dov, jnp.zeros_like(dov))
            dsum = jnp.sum(jnp.where(hm, dod, 0.0), axis=1, keepdims=True)

            def block(kb, c, gc, past):
                ks = pl.multiple_of(kb * tk, tk)
                kblk = k_ref[pl.ds(ks, tk), :]
                vblk = v_ref[pl.ds(ks, tk), :]
                z = _dot_nt(qh, kblk) * scale
                l, e = _sb_block(z, past)
                a = jnp.exp(z + l + _suffix_sum(l, u) + c)
                if past is not None:
                    a = jnp.where(past, a, 0.0)
                r = 1.0 / (1.0 + e)
                beta = jnp.where(z >= 0.0, r, e * r)
                g = a * _dot_nt(doh, vblk)
                p = dsum - (_suffix_sum(g, u) + gc)
                dz = g - beta * p
                if past is not None:
                    dz = jnp.where(past, dz, 0.0)
                dzs = (dz * scale).astype(BF16)
                kh = jnp.where(hm, kblk, jnp.zeros_like(kblk))
                dq_acc[...] += _dot(dzs, kh)
                dk_acc[pl.ds(ks, tk), :] += _dot_tn(dzs, qh)
                dv_acc[pl.ds(ks, tk), :] += _dot_tn(a.astype(BF16), doh)
                return c + jnp.sum(l, axis=1, keepdims=True), gc + jnp.sum(g, axis=1, keepdims=True)

            zero = jnp.zeros((tq, 1), F32)
            c0, g0 = block(qi, zero, zero, past_diag)

            def cond(st):
                kb, c, gc = st
                return jnp.logical_and(kb >= 0, jnp.max(c) > EXP_ZERO_BELOW)

            def step(st):
                kb, c, gc = st
                c, gc = block(kb, c, gc, None)
                return kb - 1, c, gc

            lax.while_loop(cond, step, (qi - 1, c0, g0))
        dq_ref[...] = dq_acc[...].astype(BF16)

        @pl.when(qi == nq - 1)
        def _():
            dk_ref[...] = dk_acc[...].astype(BF16)
            dv_ref[...] = dv_acc[...].astype(BF16)

    qspec = pl.BlockSpec((tq, LANES), lambda b, p, i: (b * nq + i, p))
    sspec = pl.BlockSpec((seq, LANES), lambda b, p, i: (b, p))
    out = jax.ShapeDtypeStruct((T, D), BF16)
    return pl.pallas_call(
        body, grid=(batch, nhp, nq),
        in_specs=[qspec,
                  pl.BlockSpec((seq, LANES), lambda b, p, i: (b, nhp + p)),
                  pl.BlockSpec((seq, LANES), lambda b, p, i: (b, 2 * nhp + p)),
                  qspec, qspec],
        out_specs=[qspec, sspec, sspec], out_shape=[out, out, out],
        scratch_shapes=[pltpu.VMEM((tq, LANES), F32), pltpu.VMEM((seq, LANES), F32),
                        pltpu.VMEM((seq, LANES), F32)],
        compiler_params=_params("parallel", "parallel", "arbitrary"), name=name)(qkv, qkv, qkv, o, do)


_GELU_C = 0.7978845608028654
_GELU_A = 0.044715


def _gelu(x):
    t = jnp.tanh(_GELU_C * (x + _GELU_A * x * x * x))
    return 0.5 * x * (1.0 + t), t


def _gelu_grad(x, t):
    return 0.5 * (1.0 + t) + 0.5 * x * (1.0 - t * t) * (_GELU_C * (1.0 + 3.0 * _GELU_A * x * x))


def _causal_ws(ws_ref, g):
    t = lax.broadcasted_iota(jnp.int32, (SGU_CHUNK, SGU_CHUNK), 0)
    s = lax.broadcasted_iota(jnp.int32, (SGU_CHUNK, SGU_CHUNK), 1)
    return jnp.where(s <= t, ws_ref[g], 0.0)


def _sgu_fwd(uvp, gain, ws, bst, *, name):
    T, F2 = uvp.shape
    F = F2 // 2
    C, G, W = SGU_CHUNK, SGU_GROUPS, SGU_GROUP_W

    def body(uv_ref, g_ref, ws_ref, bs_ref, y_ref):
        uvf = uv_ref[...].astype(F32)
        uv, _ = _gelu(uvf)
        u, v = uv[:, :F], uv[:, F:]
        r = lax.rsqrt(jnp.mean(v * v, axis=-1, keepdims=True) + EPS)
        vn = (v * r * g_ref[...]).astype(BF16)
        for g in range(G):
            sl = slice(g * W, (g + 1) * W)
            mixed = _dot(_causal_ws(ws_ref, g).astype(BF16), vn[:, sl]) + bs_ref[:, g:g + 1]
            y_ref[:, sl] = (u[:, sl] * mixed).astype(BF16)

    return pl.pallas_call(
        body, grid=(T // C,),
        in_specs=[pl.BlockSpec((C, F2), lambda i: (i, 0)), pl.BlockSpec((1, F), lambda i: (0, 0)),
                  pl.BlockSpec((G, C, C), lambda i: (0, 0, 0)), pl.BlockSpec((C, G), lambda i: (0, 0))],
        out_specs=pl.BlockSpec((C, F), lambda i: (i, 0)),
        out_shape=jax.ShapeDtypeStruct((T, F), BF16),
        compiler_params=_params("parallel"), name=name)(uvp, gain, ws, bst)


def _sgu_bwd(uvp, dy, gain, ws, bst, *, name):
    T, F2 = uvp.shape
    F = F2 // 2
    C, G, W = SGU_CHUNK, SGU_GROUPS, SGU_GROUP_W
    nc = T // C

    def body(uv_ref, dy_ref, g_ref, ws_ref, bs_ref, duv_ref, dg_ref, dws_ref, dbs_ref,
             dg_acc, dws_acc, dbs_acc):
        i = pl.program_id(0)

        @pl.when(i == 0)
        def _():
            dg_acc[...] = jnp.zeros_like(dg_acc)
            dws_acc[...] = jnp.zeros_like(dws_acc)
            dbs_acc[...] = jnp.zeros_like(dbs_acc)

        uvf = uv_ref[...].astype(F32)
        uv, th = _gelu(uvf)
        dgelu = _gelu_grad(uvf, th)
        u, v = uv[:, :F], uv[:, F:]
        r = lax.rsqrt(jnp.mean(v * v, axis=-1, keepdims=True) + EPS)
        vhat = v * r
        gain_v = g_ref[...]
        vn = (vhat * gain_v).astype(BF16)
        dyv = dy_ref[...].astype(F32)
        lane8 = lax.broadcasted_iota(jnp.int32, (1, G), 1)
        dvn_parts = []
        dbs_new = jnp.zeros((C, G), F32)
        for g in range(G):
            sl = slice(g * W, (g + 1) * W)
            wsg = _causal_ws(ws_ref, g)
            mixed = _dot(wsg.astype(BF16), vn[:, sl]) + bs_ref[:, g:g + 1]
            duv_ref[:, sl] = (dyv[:, sl] * mixed * dgelu[:, sl]).astype(BF16)
            dmix = dyv[:, sl] * u[:, sl]
            dbs_new = dbs_new + jnp.where(lane8 == g, jnp.sum(dmix, axis=1, keepdims=True), 0.0)
            dmix_b = dmix.astype(BF16)
            dws_acc[g] += _dot_nt(dmix_b, vn[:, sl])
            dvn_parts.append(_dot(wsg.T.astype(BF16), dmix_b))
        dbs_acc[...] += dbs_new
        dvn = jnp.concatenate(dvn_parts, axis=1)
        dg_acc[...] += jnp.sum((dvn * vhat).reshape(C // SUBLANES, SUBLANES, F), axis=0)
        dvhat = dvn * gain_v
        dv = r * (dvhat - vhat * jnp.mean(dvhat * vhat, axis=-1, keepdims=True))
        duv_ref[:, F:] = (dv * dgelu[:, F:]).astype(BF16)

        @pl.when(i == nc - 1)
        def _():
            dg_ref[...] = jnp.sum(dg_acc[...], axis=0, keepdims=True)
            t = lax.broadcasted_iota(jnp.int32, (G, C, C), 1)
            s = lax.broadcasted_iota(jnp.int32, (G, C, C), 2)
            dws_ref[...] = jnp.where(s <= t, dws_acc[...], 0.0)
            dbs_ref[...] = dbs_acc[...]

    return pl.pallas_call(
        body, grid=(nc,),
        in_specs=[pl.BlockSpec((C, F2), lambda i: (i, 0)), pl.BlockSpec((C, F), lambda i: (i, 0)),
                  pl.BlockSpec((1, F), lambda i: (0, 0)), pl.BlockSpec((G, C, C), lambda i: (0, 0, 0)),
                  pl.BlockSpec((C, G), lambda i: (0, 0))],
        out_specs=[pl.BlockSpec((C, F2), lambda i: (i, 0)), pl.BlockSpec((1, F), lambda i: (0, 0)),
                   pl.BlockSpec((G, C, C), lambda i: (0, 0, 0)), pl.BlockSpec((C, G), lambda i: (0, 0))],
        out_shape=[jax.ShapeDtypeStruct((T, F2), BF16), jax.ShapeDtypeStruct((1, F), F32),
                   jax.ShapeDtypeStruct((G, C, C), F32), jax.ShapeDtypeStruct((C, G), F32)],
        scratch_shapes=[pltpu.VMEM((SUBLANES, F), F32), pltpu.VMEM((G, C, C), F32), pltpu.VMEM((C, G), F32)],
        compiler_params=_params("arbitrary"), name=name)(uvp, dy, gain, ws, bst)


def _my_place():
    return lax.axis_index("x"), lax.axis_index("y"), lax.axis_index("c")


def _all_gather(shards, *, name):
    nf = len(shards)

    def body(*refs):
        ins, outs = refs[:nf], refs[nf:2 * nf]
        send_sems, recv_sems, local_sems = refs[2 * nf:]
        x, y, c = _my_place()
        me, sibling = (x, y, c), (x, y, 1 - c)
        chips = [(1 - x, y), (x, 1 - y), (1 - x, 1 - y)]

        def copy(f, k, block, to, src=None):
            dst = outs[f].at[:, 4 * block[0] + 2 * block[1] + block[2]]
            return pltpu.make_async_remote_copy(
                src_ref=dst if src is None else src, dst_ref=dst,
                send_sem=send_sems.at[f, k], recv_sem=recv_sems.at[f, k],
                device_id=to, device_id_type=MESH)

        mine = [pltpu.make_async_copy(ins[f], outs[f].at[:, 4 * x + 2 * y + c], local_sems.at[f])
                for f in range(nf)]
        for cp in mine:
            cp.start()
        first = []
        for f in range(nf):
            first.append(copy(f, 0, me, sibling, src=ins[f]))
            first += [copy(f, 1 + j, me, (*chip, c), src=ins[f]) for j, chip in enumerate(chips)]
        for cp in first:
            cp.start()
        passed = []
        for j, chip in enumerate(chips):
            for f in range(nf):
                copy(f, 1 + j, (*chip, c), me).wait_recv()
                cp = copy(f, 4 + j, (*chip, c), sibling)
                cp.start()
                passed.append(cp)
        for f in range(nf):
            copy(f, 0, sibling, me).wait_recv()
        for j, chip in enumerate(chips):
            for f in range(nf):
                copy(f, 4 + j, (*chip, 1 - c), me).wait_recv()
        for cp in first + passed:
            cp.wait_send()
        for cp in mine:
            cp.wait()

    return pl.pallas_call(
        body, in_specs=[ANY] * nf, out_specs=[ANY] * nf,
        out_shape=[jax.ShapeDtypeStruct((s.shape[0], N_DEV) + s.shape[1:], s.dtype) for s in shards],
        scratch_shapes=[pltpu.SemaphoreType.DMA((nf, 7)), pltpu.SemaphoreType.DMA((nf, 7)),
                        pltpu.SemaphoreType.DMA((nf,))],
        name=name)(*shards)


def _exchange(parts, *, name):
    nf = len(parts)

    def body(*refs):
        ins, outs = refs[:nf], refs[nf:2 * nf]
        send_sems, recv_sems, local_sems = refs[2 * nf:]
        x, y, c = _my_place()
        me = 4 * x + 2 * y + c
        copies = []
        for f in range(nf):
            cp = pltpu.make_async_copy(ins[f].at[:, me], outs[f].at[:, me], local_sems.at[f])
            cp.start()
            copies.append(cp)
        remote = []
        for k in range(1, N_DEV):
            px, py, pc = x ^ (k >> 2), y ^ ((k >> 1) & 1), c ^ (k & 1)
            peer = 4 * px + 2 * py + pc
            for f in range(nf):
                cp = pltpu.make_async_remote_copy(
                    src_ref=ins[f].at[:, peer], dst_ref=outs[f].at[:, me],
                    send_sem=send_sems.at[f, k - 1], recv_sem=recv_sems.at[f, k - 1],
                    device_id=(px, py, pc), device_id_type=MESH)
                cp.start()
                remote.append((cp, f, k, peer))
        for cp, f, k, peer in remote:
            pltpu.make_async_remote_copy(
                src_ref=ins[f].at[:, peer], dst_ref=outs[f].at[:, peer],
                send_sem=send_sems.at[f, k - 1], recv_sem=recv_sems.at[f, k - 1],
                device_id=(x, y, c), device_id_type=MESH).wait_recv()
        for cp, f, k, peer in remote:
            cp.wait_send()
        for cp in copies:
            cp.wait()

    return pl.pallas_call(
        body, in_specs=[ANY] * nf, out_specs=[ANY] * nf,
        out_shape=[jax.ShapeDtypeStruct(p.shape, p.dtype) for p in parts],
        scratch_shapes=[pltpu.SemaphoreType.DMA((nf, 7)), pltpu.SemaphoreType.DMA((nf, 7)),
                        pltpu.SemaphoreType.DMA((nf,))],
        name=name)(*parts)


def _adam_math(g, w, m, v):
    m = ADAM_B1 * m + (1.0 - ADAM_B1) * g
    v = ADAM_B2 * v + (1.0 - ADAM_B2) * (g * g)
    m_hat = m / (1.0 - ADAM_B1 ** ADAM_STEP)
    v_hat = v / (1.0 - ADAM_B2 ** ADAM_STEP)
    delta = -ADAM_LR * (m_hat / (jnp.sqrt(v_hat) + ADAM_EPS) + ADAM_WD * w)
    return delta, m, v


def _sum_adamw(parts, w, m, v, *, name, tr=256):
    L, nd, R, C = parts.shape
    tr = min(tr, R)
    assert R % tr == 0

    def body(p_ref, w_ref, m_ref, v_ref, g_ref, d_ref, nm_ref, nv_ref):
        g = p_ref[0, 0].astype(F32)
        for q in range(1, nd):
            g = g + p_ref[0, q].astype(F32)
        d, nm, nv = _adam_math(g, w_ref[0], m_ref[0], v_ref[0])
        g_ref[0] = g
        d_ref[0] = d
        nm_ref[0] = nm
        nv_ref[0] = nv

    blk = pl.BlockSpec((1, tr, C), lambda l, i: (l, i, 0))
    out = jax.ShapeDtypeStruct((L, R, C), F32)
    return pl.pallas_call(
        body, grid=(L, R // tr),
        in_specs=[pl.BlockSpec((1, nd, tr, C), lambda l, i: (l, 0, i, 0)), blk, blk, blk],
        out_specs=[blk] * 4, out_shape=[out] * 4,
        compiler_params=_params("parallel", "parallel"), name=name)(parts, w, m, v)


def _sum_parts(parts, *, name):
    nd, R, C = parts.shape

    def body(p_ref, o_ref):
        g = p_ref[0]
        for q in range(1, nd):
            g = g + p_ref[q]
        o_ref[...] = g

    return pl.pallas_call(
        body, out_shape=jax.ShapeDtypeStruct((R, C), F32),
        in_specs=[pl.BlockSpec(memory_space=pltpu.VMEM)],
        out_specs=pl.BlockSpec(memory_space=pltpu.VMEM), name=name)(parts)


def _adamw_small(g, w, m, v, *, name):
    def body(g_ref, w_ref, m_ref, v_ref, d_ref, nm_ref, nv_ref):
        d, nm, nv = _adam_math(g_ref[...], w_ref[...], m_ref[...], v_ref[...])
        d_ref[...] = d
        nm_ref[...] = nm
        nv_ref[...] = nv

    vm = pl.BlockSpec(memory_space=pltpu.VMEM)
    out = jax.ShapeDtypeStruct(g.shape, F32)
    return pl.pallas_call(body, out_shape=[out] * 3, in_specs=[vm] * 4, out_specs=[vm] * 3,
                          name=name)(g, w, m, v)


def kernel(x, norm_mix, norm_mlp, sb_wqkv, sb_wo, sgu_win, sgu_gain, sgu_ws, sgu_bs, sgu_wout, mlp_w1, mlp_w2, final_norm, loss_target, m_norm_mix, m_norm_mlp, m_sb_wqkv, m_sb_wo, m_sgu_win, m_sgu_gain, m_sgu_ws, m_sgu_bs, m_sgu_wout, m_mlp_w1, m_mlp_w2, m_final_norm, v_norm_mix, v_norm_mlp, v_sb_wqkv, v_sb_wo, v_sgu_win, v_sgu_gain, v_sgu_ws, v_sgu_bs, v_sgu_wout, v_mlp_w1, v_mlp_w2, v_final_norm):
    batch, seq, D = x.shape
    T = batch * seq
    x0 = x.reshape(T, D)
    target = loss_target.reshape(T, D)

    big = [sb_wqkv, sb_wo, sgu_win, sgu_wout, mlp_w1, mlp_w2]
    gathered = _all_gather([w.astype(BF16) for w in big] + [sgu_gain[:, None, :]], name="gather_weights")
    g_wqkv, g_wo, g_win, g_wout, g_w1, g_w2, g_gain = gathered
    g_wo, g_wout, g_w2 = [g.reshape(g.shape[0], 1, N_DEV * g.shape[2], g.shape[3]) for g in (g_wo, g_wout, g_w2)]
    gain_sgu = g_gain.reshape(g_gain.shape[0], 1, SGU_FFN)

    saved = []
    xs = x0
    for i in range(DEPTH):
        j = i // 2
        h = _rms_fwd(xs, norm_mix[i:i + 1], name=f"norm_mix{i}")
        if i % 2 == 0:
            qkv = _mm_nn(h, g_wqkv, j, out_dtype=BF16, name=f"qkv{i}", tn=384)
            o = _sb_fwd(qkv, batch=batch, seq=seq, name=f"sb_fwd{i}")
            x_mid = _mm_nn(o, g_wo, j, out_dtype=F32, res=xs, name=f"wo{i}")
            mix = (h, qkv, o)
        else:
            gain_j = gain_sgu[j]
            bst = sgu_bs[j].T
            uvp = _mm_nn(h, g_win, j, out_dtype=BF16, name=f"win{i}")
            yv = _sgu_fwd(uvp, gain_j, sgu_ws[j], bst, name=f"sgu_fwd{i}")
            x_mid = _mm_nn(yv, g_wout, j, out_dtype=F32, res=xs, name=f"wout{i}")
            mix = (h, uvp, yv, gain_j, bst)
        h2 = _rms_fwd(x_mid, norm_mlp[i:i + 1], name=f"norm_mlp{i}")
        a = _mm_nn(h2, g_w1, i, out_dtype=BF16, name=f"w1_{i}")
        x_out = _mm_nn(a, g_w2, i, out_dtype=F32, res=x_mid, a_act="relu2", name=f"w2_{i}")
        saved.append((xs, mix, x_mid, h2, a))
        xs = x_out

    dx, sq, d_final = _loss_head(xs, final_norm.reshape(1, D), target, name="loss_head")
    loss = lax.psum(0.5 * jnp.sum(sq) / D, ("x", "y", "c"))

    p_wqkv = p_wo = p_win = p_wout = p_w1 = p_w2 = None
    d_norm_mix, d_norm_mlp = [None] * DEPTH, [None] * DEPTH
    d_gain, d_ws, d_bs = [None] * 2, [None] * 2, [None] * 2
    for i in reversed(range(DEPTH)):
        j = i // 2
        xs, mix, x_mid, h2, a = saved[i]
        da = _mm_nt(dx, g_w2, i, out_dtype=BF16, act_src=a, name=f"d_a{i}")
        p_w2 = _mm_tn(a, dx, p_w2, i, layers=DEPTH, shards=1, a_act="relu2", name=f"d_w2_{i}")
        p_w1 = _mm_tn(h2, da, p_w1, i, layers=DEPTH, shards=N_DEV, name=f"d_w1_{i}")
        dh2 = _mm_nt(da, g_w1, i, out_dtype=F32, name=f"d_h2_{i}")
        dx, d_norm_mlp[i] = _rms_bwd(dh2, x_mid, norm_mlp[i:i + 1], dx, name=f"d_norm_mlp{i}")
        if i % 2 == 0:
            h, qkv, o = mix
            do = _mm_nt(dx, g_wo, j, out_dtype=BF16, name=f"d_o{i}")
            p_wo = _mm_tn(o, dx, p_wo, j, layers=2, shards=1, name=f"d_wo{i}")
            dq, dk, dv = _sb_bwd(qkv, o, do, batch=batch, seq=seq, name=f"sb_bwd{i}")
            dqkv = jnp.concatenate([dq, dk, dv], axis=1)
            p_wqkv = _mm_tn(h, dqkv, p_wqkv, j, layers=2, shards=N_DEV, name=f"d_wqkv{i}", tn=384)
            dh = _mm_nt(dqkv, g_wqkv, j, out_dtype=F32, name=f"d_h_sb{i}", tk=384)
        else:
            h, uvp, yv, gain_j, bst = mix
            dy = _mm_nt(dx, g_wout, j, out_dtype=BF16, name=f"d_y{i}")
            p_wout = _mm_tn(yv, dx, p_wout, j, layers=2, shards=1, name=f"d_wout{i}")
            duv, d_gain[j], d_ws[j], dbst = _sgu_bwd(uvp, dy, gain_j, sgu_ws[j], bst, name=f"sgu_bwd{i}")
            d_bs[j] = dbst.T
            p_win = _mm_tn(h, duv, p_win, j, layers=2, shards=N_DEV, name=f"d_win{i}")
            dh = _mm_nt(duv, g_win, j, out_dtype=F32, name=f"d_h_sgu{i}")
        dx, d_norm_mix[i] = _rms_bwd(dh, xs, norm_mix[i:i + 1], dx, name=f"d_norm_mix{i}")
    grad_x = dx.reshape(batch, seq, D)

    def row_shards(p):
        return p.reshape(p.shape[0], N_DEV, p.shape[2] // N_DEV, p.shape[3])

    small = [jnp.concatenate(d_norm_mix, 0), jnp.concatenate(d_norm_mlp, 0), d_final,
             jnp.concatenate(d_gain, 0), jnp.stack(d_bs, 0), jnp.stack(d_ws, 0)]
    small_flat = jnp.concatenate([s.reshape(-1) for s in small])
    n_small = small_flat.shape[0]
    small_rows = -(-n_small // (N_DEV * SUBLANES * LANES)) * SUBLANES
    small_flat = jnp.pad(small_flat, (0, N_DEV * small_rows * LANES - n_small))
    parts = [p_wqkv, row_shards(p_wo), p_win, row_shards(p_wout), p_w1, row_shards(p_w2),
             small_flat.reshape(1, N_DEV, small_rows, LANES)]
    r_wqkv, r_wo, r_win, r_wout, r_w1, r_w2, r_small = _exchange(parts, name="exchange_grads")

    u_wqkv = _sum_adamw(r_wqkv, sb_wqkv, m_sb_wqkv, v_sb_wqkv, name="adamw_wqkv")
    u_wo = _sum_adamw(r_wo, sb_wo, m_sb_wo, v_sb_wo, name="adamw_wo")
    u_win = _sum_adamw(r_win, sgu_win, m_sgu_win, v_sgu_win, name="adamw_win")
    u_wout = _sum_adamw(r_wout, sgu_wout, m_sgu_wout, v_sgu_wout, name="adamw_wout")
    u_w1 = _sum_adamw(r_w1, mlp_w1, m_mlp_w1, v_mlp_w1, name="adamw_w1")
    u_w2 = _sum_adamw(r_w2, mlp_w2, m_mlp_w2, v_mlp_w2, name="adamw_w2")

    small_sum = _sum_parts(r_small[0], name="sum_small")
    g_small = _all_gather([small_sum[None]], name="gather_small")[0].reshape(-1)[:n_small]

    shapes = [s.shape for s in small]
    sizes = [s.size for s in small]
    offs = [sum(sizes[:k]) for k in range(len(sizes))]
    me = 4 * lax.axis_index("x") + 2 * lax.axis_index("y") + lax.axis_index("c")
    shard_w = SGU_FFN // N_DEV

    def pack(arrs):
        flat = jnp.concatenate([a_.reshape(-1) for a_ in arrs])
        return jnp.pad(flat, (0, N_DEV * small_rows * LANES - n_small)).reshape(-1, LANES)

    def full_gain(gshard):
        return lax.dynamic_update_slice(jnp.zeros((2, SGU_FFN), F32), gshard, (0, me * shard_w))

    w_small = pack([norm_mix, norm_mlp, final_norm, full_gain(sgu_gain), sgu_bs, sgu_ws])
    m_small = pack([m_norm_mix, m_norm_mlp, m_final_norm, full_gain(m_sgu_gain), m_sgu_bs, m_sgu_ws])
    v_small = pack([v_norm_mix, v_norm_mlp, v_final_norm, full_gain(v_sgu_gain), v_sgu_bs, v_sgu_ws])
    g_pack = jnp.pad(g_small, (0, N_DEV * small_rows * LANES - n_small)).reshape(-1, LANES)
    sm = [g_pack] + list(_adamw_small(g_pack, w_small, m_small, v_small, name="adamw_small"))

    def unpack(flat2d):
        flat = flat2d.reshape(-1)
        out = [flat[offs[k]:offs[k] + sizes[k]].reshape(shapes[k]) for k in range(len(sizes))]
        out[2] = out[2].reshape(D)
        out[3] = lax.dynamic_slice(out[3], (0, me * shard_w), (2, shard_w))
        return out

    outs = []
    for k, big_u in enumerate(zip(u_wqkv, u_wo, u_win, u_wout, u_w1, u_w2)):
        s_nm, s_nl, s_fn, s_gain, s_bs, s_ws = unpack(sm[k])
        b_wqkv, b_wo, b_win, b_wout, b_w1, b_w2 = big_u
        outs += [s_nm, s_nl, b_wqkv, b_wo, b_win, s_gain, s_ws, s_bs, b_wout, b_w1, b_w2, s_fn]
    return (loss, grad_x, *outs)
```

```python
import functools

import jax
import jax.numpy as jnp
from jax import lax
from jax.experimental import pallas as pl
from jax.experimental.pallas import tpu as pltpu

F32 = jnp.float32
BF16 = jnp.bfloat16

N_DEV = 8
D_MODEL = 1024
SEQ = 2048
DEPTH = 4
SB_HEAD_DIM = 64
SGU_CHUNK = 128
SGU_FFN = 2 * D_MODEL
SGU_GROUPS = 8
SGU_GROUP_W = SGU_FFN // SGU_GROUPS
EPS = 1e-6

ADAM_LR = 0.001
ADAM_B1 = 0.9
ADAM_B2 = 0.999
ADAM_EPS = 1e-08
ADAM_WD = 0.01
ADAM_STEP = 10

LANES = 128
SUBLANES = 8
VMEM_LIMIT = 48 * 1024 * 1024
EXP_ZERO_BELOW = -104.0

MESH = pl.DeviceIdType.MESH
ANY = pl.BlockSpec(memory_space=pl.ANY)


def _params(*sem):
    return pltpu.CompilerParams(dimension_semantics=sem, vmem_limit_bytes=VMEM_LIMIT)


def _dot(a, b):
    return jnp.dot(a, b, preferred_element_type=F32)


def _dot_nt(a, b):
    return lax.dot_general(a, b, (((1,), (1,)), ((), ())), preferred_element_type=F32)


def _dot_tn(a, b):
    return lax.dot_general(a, b, (((0,), (0,)), ((), ())), preferred_element_type=F32)


def _split_bf16(x):
    hi = x.astype(BF16)
    lo = (x - hi.astype(F32)).astype(BF16)
    return hi, lo


def _relu2(av):
    t = jnp.maximum(av, jnp.zeros_like(av))
    return t * t


def _mm_nn(a, w, l, *, out_dtype, name, res=None, a_act=None, tm=1024, tn=None, pb=1):
    M, K = a.shape
    _, P, K2, n = w.shape
    assert K2 == K
    tn = n if tn is None else tn
    assert n % tn == 0 and (tn == n or pb == 1) and P % pb == 0
    tm = min(tm, M)
    assert M % tm == 0
    jn, width = n // tn, pb * tn

    def body(*refs):
        a_ref, w_ref, o_ref = refs[0], refs[1], refs[-1]
        av = a_ref[...]
        if a_act == "relu2":
            av = _relu2(av)
        av = av.astype(BF16)
        for p in range(pb):
            sl = slice(p * tn, (p + 1) * tn)
            r = _dot(av, w_ref[0, p])
            if res is not None:
                r = r + refs[2][:, sl]
            o_ref[:, sl] = r.astype(out_dtype)

    in_specs = [pl.BlockSpec((tm, K), lambda i, j: (i, 0)),
                pl.BlockSpec((1, pb, K, tn), lambda i, j: (l, j // jn, 0, j % jn))]
    args = [a, w]
    if res is not None:
        in_specs.append(pl.BlockSpec((tm, width), lambda i, j: (i, j)))
        args.append(res)
    return pl.pallas_call(
        body, grid=(M // tm, P * n // width), in_specs=in_specs,
        out_specs=pl.BlockSpec((tm, width), lambda i, j: (i, j)),
        out_shape=jax.ShapeDtypeStruct((M, P * n), out_dtype),
        compiler_params=_params("parallel", "parallel"), name=name)(*args)


def _mm_nt(a, w, l, *, out_dtype, name, act_src=None, tm=1024, tn=512):
    M, K = a.shape
    _, P, Nout, kc = w.shape
    assert K == P * kc
    tm, tn = min(tm, M), min(tn, Nout)
    assert M % tm == 0 and Nout % tn == 0

    def body(*refs):
        a_ref, w_ref, o_ref = refs[0], refs[1], refs[-1]
        r = None
        for p in range(P):
            d = _dot_nt(a_ref[:, p * kc:(p + 1) * kc].astype(BF16), w_ref[0, p])
            r = d if r is None else r + d
        if act_src is not None:
            r = r * (2.0 * jnp.maximum(refs[2][...].astype(F32), 0.0))
        o_ref[...] = r.astype(out_dtype)

    in_specs = [pl.BlockSpec((tm, K), lambda i, j: (i, 0)),
                pl.BlockSpec((1, P, tn, kc), lambda i, j: (l, 0, j, 0))]
    args = [a, w]
    if act_src is not None:
        in_specs.append(pl.BlockSpec((tm, tn), lambda i, j: (i, j)))
        args.append(act_src)
    return pl.pallas_call(
        body, grid=(M // tm, Nout // tn), in_specs=in_specs,
        out_specs=pl.BlockSpec((tm, tn), lambda i, j: (i, j)),
        out_shape=jax.ShapeDtypeStruct((M, Nout), out_dtype),
        compiler_params=_params("parallel", "parallel"), name=name)(*args)


def _mm_tn(a, b, stack, l, *, layers, shards, name, a_act=None, tm=2048, tk=1024, pb=1):
    M, K = a.shape
    M2, N = b.shape
    assert M2 == M
    n = N // shards
    tm, tk = min(tm, M), min(tk, K)
    assert M % tm == 0 and K % tk == 0 and shards % pb == 0
    width, nm = pb * n, M // tm

    def body(*refs):
        a_ref, b_ref = refs[:2]
        o_ref, acc = refs[-2:]
        m = pl.program_id(2)
        av = a_ref[...]
        if a_act == "relu2":
            av = _relu2(av)

        @pl.when(m == 0)
        def _():
            acc[...] = jnp.zeros_like(acc)

        acc[...] += _dot_tn(av.astype(BF16), b_ref[...].astype(BF16))

        @pl.when(m == nm - 1)
        def _():
            for p in range(pb):
                o_ref[0, p] = acc[:, p * n:(p + 1) * n].astype(BF16)

    in_specs = [pl.BlockSpec((tm, tk), lambda i, j, m: (m, i)),
                pl.BlockSpec((tm, width), lambda i, j, m: (m, j))]
    args = [a, b]
    aliases = {}
    if stack is not None:
        in_specs.append(ANY)
        args.append(stack)
        aliases = {2: 0}
    return pl.pallas_call(
        body, grid=(K // tk, N // width, nm), in_specs=in_specs,
        out_specs=pl.BlockSpec((1, pb, tk, n), lambda i, j, m: (l, j, i, 0)),
        out_shape=jax.ShapeDtypeStruct((layers, shards, K, n), BF16),
        scratch_shapes=[pltpu.VMEM((tk, width), F32)], input_output_aliases=aliases,
        compiler_params=_params("parallel", "parallel", "arbitrary"), name=name)(*args)


def _rms_fwd(x, gain, *, name, tr=512):
    T, D = x.shape

    def body(x_ref, g_ref, o_ref):
        xv = x_ref[...]
        r = lax.rsqrt(jnp.mean(xv * xv, axis=-1, keepdims=True) + EPS)
        o_ref[...] = (xv * r * g_ref[...]).astype(BF16)

    return pl.pallas_call(
        body, grid=(T // tr,),
        in_specs=[pl.BlockSpec((tr, D), lambda i: (i, 0)), pl.BlockSpec((1, D), lambda i: (0, 0))],
        out_specs=pl.BlockSpec((tr, D), lambda i: (i, 0)),
        out_shape=jax.ShapeDtypeStruct((T, D), BF16),
        compiler_params=_params("parallel"), name=name)(x, gain)


def _rms_bwd(dh, x, gain, dres, *, name, tr=512):
    T, D = x.shape
    nr = T // tr

    def body(dh_ref, x_ref, g_ref, dres_ref, dx_ref, dxb_ref, dg_ref, acc):
        i = pl.program_id(0)
        xv = x_ref[...]
        r = lax.rsqrt(jnp.mean(xv * xv, axis=-1, keepdims=True) + EPS)
        xhat = xv * r
        dhv = dh_ref[...].astype(F32)
        dxhat = dhv * g_ref[...]
        dx = dres_ref[...] + r * (dxhat - xhat * jnp.mean(dxhat * xhat, axis=-1, keepdims=True))
        dx_ref[...] = dx
        dxb_ref[...] = dx.astype(BF16)
        part = jnp.sum((dhv * xhat).reshape(tr // SUBLANES, SUBLANES, D), axis=0)

        @pl.when(i == 0)
        def _():
            acc[...] = part

        @pl.when(i > 0)
        def _():
            acc[...] += part

        @pl.when(i == nr - 1)
        def _():
            dg_ref[...] = jnp.sum(acc[...], axis=0, keepdims=True)

    row = pl.BlockSpec((tr, D), lambda i: (i, 0))
    vec = pl.BlockSpec((1, D), lambda i: (0, 0))
    return pl.pallas_call(
        body, grid=(nr,), in_specs=[row, row, vec, row], out_specs=[row, row, vec],
        out_shape=[jax.ShapeDtypeStruct((T, D), F32), jax.ShapeDtypeStruct((T, D), BF16),
                   jax.ShapeDtypeStruct((1, D), F32)],
        scratch_shapes=[pltpu.VMEM((SUBLANES, D), F32)],
        compiler_params=_params("arbitrary"), name=name)(dh, x, gain, dres)


def _loss_head(x, gain, target, *, name, tr=512):
    T, D = x.shape
    nr = T // tr

    def body(x_ref, g_ref, t_ref, dx_ref, dxb_ref, sq_ref, dg_ref, sq_acc, dg_acc):
        i = pl.program_id(0)
        xv = x_ref[...]
        g = g_ref[...]
        r = lax.rsqrt(jnp.mean(xv * xv, axis=-1, keepdims=True) + EPS)
        xhat = xv * r
        err = xhat * g - t_ref[...]
        dy = err * (1.0 / D)
        dxhat = dy * g
        dx = r * (dxhat - xhat * jnp.mean(dxhat * xhat, axis=-1, keepdims=True))
        dx_ref[...] = dx
        dxb_ref[...] = dx.astype(BF16)
        sq = jnp.sum((err * err).reshape(tr // SUBLANES, SUBLANES, D), axis=0)
        dg = jnp.sum((dy * xhat).reshape(tr // SUBLANES, SUBLANES, D), axis=0)

        @pl.when(i == 0)
        def _():
            sq_acc[...] = sq
            dg_acc[...] = dg

        @pl.when(i > 0)
        def _():
            sq_acc[...] += sq
            dg_acc[...] += dg

        @pl.when(i == nr - 1)
        def _():
            sq_ref[...] = sq_acc[...]
            dg_ref[...] = jnp.sum(dg_acc[...], axis=0, keepdims=True)

    row = pl.BlockSpec((tr, D), lambda i: (i, 0))
    vec = pl.BlockSpec((1, D), lambda i: (0, 0))
    part = pl.BlockSpec((SUBLANES, D), lambda i: (0, 0))
    return pl.pallas_call(
        body, grid=(nr,), in_specs=[row, vec, row], out_specs=[row, row, part, vec],
        out_shape=[jax.ShapeDtypeStruct((T, D), F32), jax.ShapeDtypeStruct((T, D), BF16),
                   jax.ShapeDtypeStruct((SUBLANES, D), F32), jax.ShapeDtypeStruct((1, D), F32)],
        scratch_shapes=[pltpu.VMEM((SUBLANES, D), F32), pltpu.VMEM((SUBLANES, D), F32)],
        compiler_params=_params("arbitrary"), name=name)(x, gain, target)


def _head0_lanes():
    return lax.broadcasted_iota(jnp.int32, (1, LANES), 1) < SB_HEAD_DIM


def _stack_heads(x):
    zero = jnp.zeros_like(x)
    h0 = _head0_lanes()
    return jnp.concatenate([jnp.where(h0, x, zero), jnp.where(h0, zero, x)], axis=0)


def _unstack_heads(y, tq):
    return jnp.where(_head0_lanes(), y[:tq], y[tq:])


def _past_mask(tq, tk):
    row = lax.broadcasted_iota(jnp.int32, (2 * tq, tk), 0) & (tq - 1)
    col = lax.broadcasted_iota(jnp.int32, (2 * tq, tk), 1)
    return col < row


def _sb_block(z, past):
    e = jnp.exp(-jnp.abs(z))
    l = -(jnp.maximum(z, 0.0) + jnp.log(1.0 + e))
    if past is not None:
        l = jnp.where(past, l, 0.0)
    return l, e


def _suffix_matrix(tk):
    j = lax.broadcasted_iota(jnp.int32, (tk, tk), 0)
    s = lax.broadcasted_iota(jnp.int32, (tk, tk), 1)
    return (j > s).astype(BF16)


def _suffix_sum(x, u):
    hi, lo = _split_bf16(x)
    return _dot(hi, u) + _dot(lo, u)


def _sb_fwd(qkv, *, batch, seq, name, tq=256):
    T, D3 = qkv.shape
    D = D3 // 3
    nhp = D // LANES
    tk = tq
    nq = seq // tq
    scale = SB_HEAD_DIM ** -0.5

    def body(q_ref, k_ref, v_ref, o_ref, acc):
        qi = pl.program_id(2)
        qs = _stack_heads(q_ref[...]) * scale
        past = _past_mask(tq, tk)
        u = _suffix_matrix(tk)

        def block(kb, c, diag):
            ks = pl.multiple_of(kb * tk, tk)
            z = _dot_nt(qs, k_ref[pl.ds(ks, tk), :])
            l, _ = _sb_block(z, past if diag else None)
            arg = z + l + _suffix_sum(l, u)
            a = jnp.exp(arg if c is None else arg + c)
            if diag:
                a = jnp.where(past, a, 0.0)
            a_hi, a_lo = _split_bf16(a)
            vblk = v_ref[pl.ds(ks, tk), :]
            return _dot(a_hi, vblk) + _dot(a_lo, vblk), jnp.sum(l, axis=1, keepdims=True)

        @pl.when(qi == 0)
        def _():
            o_d, _ = block(0, None, True)
            o_ref[...] = _unstack_heads(o_d, tq)

        @pl.when(qi > 0)
        def _():
            o_d, c_d = block(qi, None, True)
            o_p, c_p = block(qi - 1, c_d, False)
            acc[...] = o_d + o_p

            def cond(st):
                kb, c = st
                return jnp.logical_and(kb >= 0, jnp.max(c) > EXP_ZERO_BELOW)

            def step(st):
                kb, c = st
                o_n, c_n = block(kb, c, False)
                acc[...] += o_n
                return kb - 1, c + c_n

            lax.while_loop(cond, step, (qi - 2, c_d + c_p))
            o_ref[...] = _unstack_heads(acc[...], tq)

    return pl.pallas_call(
        body, grid=(batch, nhp, nq),
        in_specs=[pl.BlockSpec((tq, LANES), lambda b, p, i: (b * nq + i, p)),
                  pl.BlockSpec((seq, LANES), lambda b, p, i: (b, nhp + p)),
                  pl.BlockSpec((seq, LANES), lambda b, p, i: (b, 2 * nhp + p))],
        out_specs=pl.BlockSpec((tq, LANES), lambda b, p, i: (b * nq + i, p)),
        out_shape=jax.ShapeDtypeStruct((T, D), F32),
        scratch_shapes=[pltpu.VMEM((2 * tq, LANES), F32)],
        compiler_params=_params("parallel", "parallel", "arbitrary"), name=name)(qkv, qkv, qkv)


def _sb_bwd(qkv, o, do, *, batch, seq, name, tq=256):
    T, D3 = qkv.shape
    D = D3 // 3
    nhp = D // LANES
    tk = tq
    nq = seq // tq
    scale = SB_HEAD_DIM ** -0.5

    def body(q_ref, k_ref, v_ref, o_ref, do_ref, dq_ref, dk_ref, dv_ref, dq_acc, dk_acc, dv_acc):
        qi = pl.program_id(2)

        @pl.when(qi == 0)
        def _():
            dk_acc[...] = jnp.zeros_like(dk_acc)
            dv_acc[...] = jnp.zeros_like(dv_acc)

        qs = _stack_heads(q_ref[...]) * scale
        dov = do_ref[...]
        dos = _stack_heads(dov)
        dsum = jnp.sum(_stack_heads(dov.astype(F32) * o_ref[...]), axis=1, keepdims=True)
        past = _past_mask(tq, tk)
        u = _suffix_matrix(tk)

        def block(kb, c, gc, diag):
            ks = pl.multiple_of(kb * tk, tk)
            kblk = k_ref[pl.ds(ks, tk), :]
            vblk = v_ref[pl.ds(ks, tk), :]
            z = _dot_nt(qs, kblk)
            l, _ = _sb_block(z, past if diag else None)
            arg = z + l + _suffix_sum(l, u)
            a = jnp.exp(arg if c is None else arg + c)
            if diag:
                a = jnp.where(past, a, 0.0)
            beta = 1.0 - jnp.exp(l)
            g = a * _dot_nt(dos, vblk)
            gs = _suffix_sum(g, u)
            dz = g - beta * (dsum - (gs if gc is None else gs + gc))
            if diag:
                dz = jnp.where(past, dz, 0.0)
            dzb = dz.astype(BF16)
            dk_acc[pl.ds(ks, tk), :] += _dot_tn(dzb, qs)
            dv_acc[pl.ds(ks, tk), :] += _dot_tn(a.astype(BF16), dos)
            return (_dot(dzb, kblk), jnp.sum(l, axis=1, keepdims=True),
                    jnp.sum(g, axis=1, keepdims=True))

        def finish(dq2):
            dq_ref[...] = (_unstack_heads(dq2, tq) * scale).astype(BF16)

        @pl.when(qi == 0)
        def _():
            dq_d, _, _ = block(0, None, None, True)
            finish(dq_d)

        @pl.when(qi > 0)
        def _():
            dq_d, c_d, g_d = block(qi, None, None, True)
            dq_p, c_p, g_p = block(qi - 1, c_d, g_d, False)
            dq_acc[...] = dq_d + dq_p

            def cond(st):
                kb, c, gc = st
                return jnp.logical_and(kb >= 0, jnp.max(c) > EXP_ZERO_BELOW)

            def step(st):
                kb, c, gc = st
                dq_n, c_n, g_n = block(kb, c, gc, False)
                dq_acc[...] += dq_n
                return kb - 1, c + c_n, gc + g_n

            lax.while_loop(cond, step, (qi - 2, c_d + c_p, g_d + g_p))
            finish(dq_acc[...])

        @pl.when(qi == nq - 1)
        def _():
            dk_ref[...] = dk_acc[...].astype(BF16)
            dv_ref[...] = dv_acc[...].astype(BF16)

    qspec = pl.BlockSpec((tq, LANES), lambda b, p, i: (b * nq + i, p))
    sspec = pl.BlockSpec((seq, LANES), lambda b, p, i: (b, p))
    out = jax.ShapeDtypeStruct((T, D), BF16)
    return pl.pallas_call(
        body, grid=(batch, nhp, nq),
        in_specs=[qspec,
                  pl.BlockSpec((seq, LANES), lambda b, p, i: (b, nhp + p)),
                  pl.BlockSpec((seq, LANES), lambda b, p, i: (b, 2 * nhp + p)),
                  qspec, qspec],
        out_specs=[qspec, sspec, sspec], out_shape=[out, out, out],
        scratch_shapes=[pltpu.VMEM((2 * tq, LANES), F32), pltpu.VMEM((seq, LANES), F32),
                        pltpu.VMEM((seq, LANES), F32)],
        compiler_params=_params("parallel", "parallel", "arbitrary"), name=name)(qkv, qkv, qkv, o, do)


_GELU_C = 0.7978845608028654
_GELU_A = 0.044715


def _gelu(x):
    t = jnp.tanh(_GELU_C * (x + _GELU_A * x * x * x))
    return 0.5 * x * (1.0 + t), t


def _gelu_grad(x, t):
    return 0.5 * (1.0 + t) + 0.5 * x * (1.0 - t * t) * (_GELU_C * (1.0 + 3.0 * _GELU_A * x * x))


def _causal_ws(ws_ref, g):
    t = lax.broadcasted_iota(jnp.int32, (SGU_CHUNK, SGU_CHUNK), 0)
    s = lax.broadcasted_iota(jnp.int32, (SGU_CHUNK, SGU_CHUNK), 1)
    return jnp.where(s <= t, ws_ref[g], 0.0)


def _sgu_fwd(uvp, gain, ws, bst, *, name):
    T, F2 = uvp.shape
    F = F2 // 2
    C, G, W = SGU_CHUNK, SGU_GROUPS, SGU_GROUP_W

    def body(uv_ref, g_ref, ws_ref, bs_ref, y_ref):
        uvf = uv_ref[...].astype(F32)
        uv, _ = _gelu(uvf)
        u, v = uv[:, :F], uv[:, F:]
        r = lax.rsqrt(jnp.mean(v * v, axis=-1, keepdims=True) + EPS)
        vn = (v * r * g_ref[...]).astype(BF16)
        for g in range(G):
            sl = slice(g * W, (g + 1) * W)
            mixed = _dot(_causal_ws(ws_ref, g).astype(BF16), vn[:, sl]) + bs_ref[:, g:g + 1]
            y_ref[:, sl] = (u[:, sl] * mixed).astype(BF16)

    return pl.pallas_call(
        body, grid=(T // C,),
        in_specs=[pl.BlockSpec((C, F2), lambda i: (i, 0)), pl.BlockSpec((1, F), lambda i: (0, 0)),
                  pl.BlockSpec((G, C, C), lambda i: (0, 0, 0)), pl.BlockSpec((C, G), lambda i: (0, 0))],
        out_specs=pl.BlockSpec((C, F), lambda i: (i, 0)),
        out_shape=jax.ShapeDtypeStruct((T, F), BF16),
        compiler_params=_params("parallel"), name=name)(uvp, gain, ws, bst)


def _sgu_bwd(uvp, dy, gain, ws, bst, *, name):
    T, F2 = uvp.shape
    F = F2 // 2
    C, G, W = SGU_CHUNK, SGU_GROUPS, SGU_GROUP_W
    nc = T // C

    def body(uv_ref, dy_ref, g_ref, ws_ref, bs_ref, duv_ref, dg_ref, dws_ref, dbs_ref,
             dg_acc, dws_acc, dbs_acc):
        i = pl.program_id(0)

        @pl.when(i == 0)
        def _():
            dg_acc[...] = jnp.zeros_like(dg_acc)
            dws_acc[...] = jnp.zeros_like(dws_acc)
            dbs_acc[...] = jnp.zeros_like(dbs_acc)

        uvf = uv_ref[...].astype(F32)
        uv, th = _gelu(uvf)
        dgelu = _gelu_grad(uvf, th)
        u, v = uv[:, :F], uv[:, F:]
        r = lax.rsqrt(jnp.mean(v * v, axis=-1, keepdims=True) + EPS)
        vhat = v * r
        gain_v = g_ref[...]
        vn = (vhat * gain_v).astype(BF16)
        dyv = dy_ref[...].astype(F32)
        lane8 = lax.broadcasted_iota(jnp.int32, (1, G), 1)
        dvn_parts = []
        dbs_new = jnp.zeros((C, G), F32)
        for g in range(G):
            sl = slice(g * W, (g + 1) * W)
            wsg = _causal_ws(ws_ref, g)
            mixed = _dot(wsg.astype(BF16), vn[:, sl]) + bs_ref[:, g:g + 1]
            duv_ref[:, sl] = (dyv[:, sl] * mixed * dgelu[:, sl]).astype(BF16)
            dmix = dyv[:, sl] * u[:, sl]
            dbs_new = dbs_new + jnp.where(lane8 == g, jnp.sum(dmix, axis=1, keepdims=True), 0.0)
            dmix_b = dmix.astype(BF16)
            dws_acc[g] += _dot_nt(dmix_b, vn[:, sl])
            dvn_parts.append(_dot(wsg.T.astype(BF16), dmix_b))
        dbs_acc[...] += dbs_new
        dvn = jnp.concatenate(dvn_parts, axis=1)
        dg_acc[...] += jnp.sum((dvn * vhat).reshape(C // SUBLANES, SUBLANES, F), axis=0)
        dvhat = dvn * gain_v
        dv = r * (dvhat - vhat * jnp.mean(dvhat * vhat, axis=-1, keepdims=True))
        duv_ref[:, F:] = (dv * dgelu[:, F:]).astype(BF16)

        @pl.when(i == nc - 1)
        def _():
            dg_ref[...] = jnp.sum(dg_acc[...], axis=0, keepdims=True)
            t = lax.broadcasted_iota(jnp.int32, (G, C, C), 1)
            s = lax.broadcasted_iota(jnp.int32, (G, C, C), 2)
            dws_ref[...] = jnp.where(s <= t, dws_acc[...], 0.0)
            dbs_ref[...] = dbs_acc[...]

    return pl.pallas_call(
        body, grid=(nc,),
        in_specs=[pl.BlockSpec((C, F2), lambda i: (i, 0)), pl.BlockSpec((C, F), lambda i: (i, 0)),
                  pl.BlockSpec((1, F), lambda i: (0, 0)), pl.BlockSpec((G, C, C), lambda i: (0, 0, 0)),
                  pl.BlockSpec((C, G), lambda i: (0, 0))],
        out_specs=[pl.BlockSpec((C, F2), lambda i: (i, 0)), pl.BlockSpec((1, F), lambda i: (0, 0)),
                   pl.BlockSpec((G, C, C), lambda i: (0, 0, 0)), pl.BlockSpec((C, G), lambda i: (0, 0))],
        out_shape=[jax.ShapeDtypeStruct((T, F2), BF16), jax.ShapeDtypeStruct((1, F), F32),
                   jax.ShapeDtypeStruct((G, C, C), F32), jax.ShapeDtypeStruct((C, G), F32)],
        scratch_shapes=[pltpu.VMEM((SUBLANES, F), F32), pltpu.VMEM((G, C, C), F32), pltpu.VMEM((C, G), F32)],
        compiler_params=_params("arbitrary"), name=name)(uvp, dy, gain, ws, bst)


def _my_place():
    return lax.axis_index("x"), lax.axis_index("y"), lax.axis_index("c")


def _all_gather(shards, *, name):
    nf = len(shards)

    def body(*refs):
        ins, outs = refs[:nf], refs[nf:2 * nf]
        send_sems, recv_sems, local_sems = refs[2 * nf:]
        x, y, c = _my_place()
        me, sibling = (x, y, c), (x, y, 1 - c)
        chips = [(1 - x, y), (x, 1 - y), (1 - x, 1 - y)]

        def copy(f, k, block, to, src=None):
            dst = outs[f].at[:, 4 * block[0] + 2 * block[1] + block[2]]
            return pltpu.make_async_remote_copy(
                src_ref=dst if src is None else src, dst_ref=dst,
                send_sem=send_sems.at[f, k], recv_sem=recv_sems.at[f, k],
                device_id=to, device_id_type=MESH)

        mine = [pltpu.make_async_copy(ins[f], outs[f].at[:, 4 * x + 2 * y + c], local_sems.at[f])
                for f in range(nf)]
        for cp in mine:
            cp.start()
        first = []
        for f in range(nf):
            first.append(copy(f, 0, me, sibling, src=ins[f]))
            first += [copy(f, 1 + j, me, (*chip, c), src=ins[f]) for j, chip in enumerate(chips)]
        for cp in first:
            cp.start()
        passed = []
        for j, chip in enumerate(chips):
            for f in range(nf):
                copy(f, 1 + j, (*chip, c), me).wait_recv()
                cp = copy(f, 4 + j, (*chip, c), sibling)
                cp.start()
                passed.append(cp)
        for f in range(nf):
            copy(f, 0, sibling, me).wait_recv()
        for j, chip in enumerate(chips):
            for f in range(nf):
                copy(f, 4 + j, (*chip, 1 - c), me).wait_recv()
        for cp in first + passed:
            cp.wait_send()
        for cp in mine:
            cp.wait()

    return pl.pallas_call(
        body, in_specs=[ANY] * nf, out_specs=[ANY] * nf,
        out_shape=[jax.ShapeDtypeStruct((s.shape[0], N_DEV) + s.shape[1:], s.dtype) for s in shards],
        scratch_shapes=[pltpu.SemaphoreType.DMA((nf, 7)), pltpu.SemaphoreType.DMA((nf, 7)),
                        pltpu.SemaphoreType.DMA((nf,))],
        name=name)(*shards)


def _exchange(parts, *, name):
    nf = len(parts)

    def body(*refs):
        ins, outs = refs[:nf], refs[nf:2 * nf]
        send_sems, recv_sems, local_sems = refs[2 * nf:]
        x, y, c = _my_place()
        me = 4 * x + 2 * y + c
        copies = []
        for f in range(nf):
            cp = pltpu.make_async_copy(ins[f].at[:, me], outs[f].at[:, me], local_sems.at[f])
            cp.start()
            copies.append(cp)
        remote = []
        for k in range(1, N_DEV):
            px, py, pc = x ^ (k >> 2), y ^ ((k >> 1) & 1), c ^ (k & 1)
            peer = 4 * px + 2 * py + pc
            for f in range(nf):
                cp = pltpu.make_async_remote_copy(
                    src_ref=ins[f].at[:, peer], dst_ref=outs[f].at[:, me],
                    send_sem=send_sems.at[f, k - 1], recv_sem=recv_sems.at[f, k - 1],
                    device_id=(px, py, pc), device_id_type=MESH)
                cp.start()
                remote.append((cp, f, k, peer))
        for cp, f, k, peer in remote:
            pltpu.make_async_remote_copy(
                src_ref=ins[f].at[:, peer], dst_ref=outs[f].at[:, peer],
                send_sem=send_sems.at[f, k - 1], recv_sem=recv_sems.at[f, k - 1],
                device_id=(x, y, c), device_id_type=MESH).wait_recv()
        for cp, f, k, peer in remote:
            cp.wait_send()
        for cp in copies:
            cp.wait()

    return pl.pallas_call(
        body, in_specs=[ANY] * nf, out_specs=[ANY] * nf,
        out_shape=[jax.ShapeDtypeStruct(p.shape, p.dtype) for p in parts],
        scratch_shapes=[pltpu.SemaphoreType.DMA((nf, 7)), pltpu.SemaphoreType.DMA((nf, 7)),
                        pltpu.SemaphoreType.DMA((nf,))],
        name=name)(*parts)


def _adam_math(g, w, m, v):
    m = ADAM_B1 * m + (1.0 - ADAM_B1) * g
    v = ADAM_B2 * v + (1.0 - ADAM_B2) * (g * g)
    m_hat = m / (1.0 - ADAM_B1 ** ADAM_STEP)
    v_hat = v / (1.0 - ADAM_B2 ** ADAM_STEP)
    delta = -ADAM_LR * (m_hat / (jnp.sqrt(v_hat) + ADAM_EPS) + ADAM_WD * w)
    return delta, m, v


def _sum_adamw(parts, w, m, v, *, name, tr=256):
    L, nd, R, C = parts.shape
    tr = min(tr, R)
    assert R % tr == 0

    def body(p_ref, w_ref, m_ref, v_ref, g_ref, d_ref, nm_ref, nv_ref):
        g = p_ref[0, 0].astype(F32)
        for q in range(1, nd):
            g = g + p_ref[0, q].astype(F32)
        d, nm, nv = _adam_math(g, w_ref[0], m_ref[0], v_ref[0])
        g_ref[0] = g
        d_ref[0] = d
        nm_ref[0] = nm
        nv_ref[0] = nv

    blk = pl.BlockSpec((1, tr, C), lambda l, i: (l, i, 0))
    out = jax.ShapeDtypeStruct((L, R, C), F32)
    return pl.pallas_call(
        body, grid=(L, R // tr),
        in_specs=[pl.BlockSpec((1, nd, tr, C), lambda l, i: (l, 0, i, 0)), blk, blk, blk],
        out_specs=[blk] * 4, out_shape=[out] * 4,
        compiler_params=_params("parallel", "parallel"), name=name)(parts, w, m, v)


def _sum_parts(parts, *, name):
    nd, R, C = parts.shape

    def body(p_ref, o_ref):
        g = p_ref[0]
        for q in range(1, nd):
            g = g + p_ref[q]
        o_ref[...] = g

    return pl.pallas_call(
        body, out_shape=jax.ShapeDtypeStruct((R, C), F32),
        in_specs=[pl.BlockSpec(memory_space=pltpu.VMEM)],
        out_specs=pl.BlockSpec(memory_space=pltpu.VMEM), name=name)(parts)


def _adamw_small(g, w, m, v, *, name):
    def body(g_ref, w_ref, m_ref, v_ref, d_ref, nm_ref, nv_ref):
        d, nm, nv = _adam_math(g_ref[...], w_ref[...], m_ref[...], v_ref[...])
        d_ref[...] = d
        nm_ref[...] = nm
        nv_ref[...] = nv

    vm = pl.BlockSpec(memory_space=pltpu.VMEM)
    out = jax.ShapeDtypeStruct(g.shape, F32)
    return pl.pallas_call(body, out_shape=[out] * 3, in_specs=[vm] * 4, out_specs=[vm] * 3,
                          name=name)(g, w, m, v)


def kernel(x, norm_mix, norm_mlp, sb_wqkv, sb_wo, sgu_win, sgu_gain, sgu_ws, sgu_bs, sgu_wout, mlp_w1, mlp_w2, final_norm, loss_target, m_norm_mix, m_norm_mlp, m_sb_wqkv, m_sb_wo, m_sgu_win, m_sgu_gain, m_sgu_ws, m_sgu_bs, m_sgu_wout, m_mlp_w1, m_mlp_w2, m_final_norm, v_norm_mix, v_norm_mlp, v_sb_wqkv, v_sb_wo, v_sgu_win, v_sgu_gain, v_sgu_ws, v_sgu_bs, v_sgu_wout, v_mlp_w1, v_mlp_w2, v_final_norm):
    batch, seq, D = x.shape
    T = batch * seq
    x0 = x.reshape(T, D)
    target = loss_target.reshape(T, D)

    big = [sb_wqkv, sb_wo, sgu_win, sgu_wout, mlp_w1, mlp_w2]
    gathered = _all_gather([w.astype(BF16) for w in big] + [sgu_gain[:, None, :]], name="gather_weights")
    g_wqkv, g_wo, g_win, g_wout, g_w1, g_w2, g_gain = gathered
    g_wo, g_wout, g_w2 = [g.reshape(g.shape[0], 1, N_DEV * g.shape[2], g.shape[3]) for g in (g_wo, g_wout, g_w2)]
    gain_sgu = g_gain.reshape(g_gain.shape[0], 1, SGU_FFN)

    saved = []
    xs = x0
    for i in range(DEPTH):
        j = i // 2
        h = _rms_fwd(xs, norm_mix[i:i + 1], name=f"norm_mix{i}")
        if i % 2 == 0:
            qkv = _mm_nn(h, g_wqkv, j, out_dtype=BF16, name=f"qkv{i}", pb=2)
            o = _sb_fwd(qkv, batch=batch, seq=seq, name=f"sb_fwd{i}")
            x_mid = _mm_nn(o, g_wo, j, out_dtype=F32, res=xs, name=f"wo{i}")
            mix = (h, qkv, o)
        else:
            gain_j = gain_sgu[j]
            bst = sgu_bs[j].T
            uvp = _mm_nn(h, g_win, j, out_dtype=BF16, name=f"win{i}", pb=2)
            yv = _sgu_fwd(uvp, gain_j, sgu_ws[j], bst, name=f"sgu_fwd{i}")
            x_mid = _mm_nn(yv, g_wout, j, out_dtype=F32, res=xs, name=f"wout{i}")
            mix = (h, uvp, yv, gain_j, bst)
        h2 = _rms_fwd(x_mid, norm_mlp[i:i + 1], name=f"norm_mlp{i}")
        a = _mm_nn(h2, g_w1, i, out_dtype=BF16, name=f"w1_{i}", pb=2)
        x_out = _mm_nn(a, g_w2, i, out_dtype=F32, res=x_mid, a_act="relu2", name=f"w2_{i}", tm=512, tn=512)
        saved.append((xs, mix, x_mid, h2, a))
        xs = x_out

    dx, dxb, sq, d_final = _loss_head(xs, final_norm.reshape(1, D), target, name="loss_head")
    loss = lax.psum(0.5 * jnp.sum(sq) / D, ("x", "y", "c"))

    p_wqkv = p_wo = p_win = p_wout = p_w1 = p_w2 = None
    d_norm_mix, d_norm_mlp = [None] * DEPTH, [None] * DEPTH
    d_gain, d_ws, d_bs = [None] * 2, [None] * 2, [None] * 2
    for i in reversed(range(DEPTH)):
        j = i // 2
        xs, mix, x_mid, h2, a = saved[i]
        da = _mm_nt(dxb, g_w2, i, out_dtype=BF16, act_src=a, name=f"d_a{i}", tn=1024)
        p_w2 = _mm_tn(a, dxb, p_w2, i, layers=DEPTH, shards=1, a_act="relu2", name=f"d_w2_{i}")
        p_w1 = _mm_tn(h2, da, p_w1, i, layers=DEPTH, shards=N_DEV, name=f"d_w1_{i}", pb=2)
        dh2 = _mm_nt(da, g_w1, i, out_dtype=F32, name=f"d_h2_{i}")
        dx, dxb, d_norm_mlp[i] = _rms_bwd(dh2, x_mid, norm_mlp[i:i + 1], dx, name=f"d_norm_mlp{i}")
        if i % 2 == 0:
            h, qkv, o = mix
            do = _mm_nt(dxb, g_wo, j, out_dtype=BF16, name=f"d_o{i}", tn=1024)
            p_wo = _mm_tn(o, dxb, p_wo, j, layers=2, shards=1, name=f"d_wo{i}")
            dq, dk, dv = _sb_bwd(qkv, o, do, batch=batch, seq=seq, name=f"sb_bwd{i}")
            dqkv = jnp.concatenate([dq, dk, dv], axis=1)
            p_wqkv = _mm_tn(h, dqkv, p_wqkv, j, layers=2, shards=N_DEV, name=f"d_wqkv{i}", pb=2)
            dh = _mm_nt(dqkv, g_wqkv, j, out_dtype=F32, name=f"d_h_sb{i}")
        else:
            h, uvp, yv, gain_j, bst = mix
            dy = _mm_nt(dxb, g_wout, j, out_dtype=BF16, name=f"d_y{i}", tn=1024)
            p_wout = _mm_tn(yv, dxb, p_wout, j, layers=2, shards=1, name=f"d_wout{i}")
            duv, d_gain[j], d_ws[j], dbst = _sgu_bwd(uvp, dy, gain_j, sgu_ws[j], bst, name=f"sgu_bwd{i}")
            d_bs[j] = dbst.T
            p_win = _mm_tn(h, duv, p_win, j, layers=2, shards=N_DEV, name=f"d_win{i}", pb=2)
            dh = _mm_nt(duv, g_win, j, out_dtype=F32, name=f"d_h_sgu{i}")
        dx, dxb, d_norm_mix[i] = _rms_bwd(dh, xs, norm_mix[i:i + 1], dx, name=f"d_norm_mix{i}")
    grad_x = dx.reshape(batch, seq, D)

    def row_shards(p):
        return p.reshape(p.shape[0], N_DEV, p.shape[2] // N_DEV, p.shape[3])

    small = [jnp.concatenate(d_norm_mix, 0), jnp.concatenate(d_norm_mlp, 0), d_final,
             jnp.concatenate(d_gain, 0), jnp.stack(d_bs, 0), jnp.stack(d_ws, 0)]
    small_flat = jnp.concatenate([s.reshape(-1) for s in small])
    n_small = small_flat.shape[0]
    small_rows = -(-n_small // (N_DEV * SUBLANES * LANES)) * SUBLANES
    small_flat = jnp.pad(small_flat, (0, N_DEV * small_rows * LANES - n_small))
    parts = [p_wqkv, row_shards(p_wo), p_win, row_shards(p_wout), p_w1, row_shards(p_w2),
             small_flat.reshape(1, N_DEV, small_rows, LANES)]
    r_wqkv, r_wo, r_win, r_wout, r_w1, r_w2, r_small = _exchange(parts, name="exchange_grads")

    u_wqkv = _sum_adamw(r_wqkv, sb_wqkv, m_sb_wqkv, v_sb_wqkv, name="adamw_wqkv")
    u_wo = _sum_adamw(r_wo, sb_wo, m_sb_wo, v_sb_wo, name="adamw_wo")
    u_win = _sum_adamw(r_win, sgu_win, m_sgu_win, v_sgu_win, name="adamw_win")
    u_wout = _sum_adamw(r_wout, sgu_wout, m_sgu_wout, v_sgu_wout, name="adamw_wout")
    u_w1 = _sum_adamw(r_w1, mlp_w1, m_mlp_w1, v_mlp_w1, name="adamw_w1")
    u_w2 = _sum_adamw(r_w2, mlp_w2, m_mlp_w2, v_mlp_w2, name="adamw_w2")

    small_sum = _sum_parts(r_small[0], name="sum_small")
    g_small = _all_gather([small_sum[None]], name="gather_small")[0].reshape(-1)[:n_small]

    shapes = [s.shape for s in small]
    sizes = [s.size for s in small]
    offs = [sum(sizes[:k]) for k in range(len(sizes))]
    me = 4 * lax.axis_index("x") + 2 * lax.axis_index("y") + lax.axis_index("c")
    shard_w = SGU_FFN // N_DEV

    def pack(arrs):
        flat = jnp.concatenate([a_.reshape(-1) for a_ in arrs])
        return jnp.pad(flat, (0, N_DEV * small_rows * LANES - n_small)).reshape(-1, LANES)

    def full_gain(gshard):
        return lax.dynamic_update_slice(jnp.zeros((2, SGU_FFN), F32), gshard, (0, me * shard_w))

    w_small = pack([norm_mix, norm_mlp, final_norm, full_gain(sgu_gain), sgu_bs, sgu_ws])
    m_small = pack([m_norm_mix, m_norm_mlp, m_final_norm, full_gain(m_sgu_gain), m_sgu_bs, m_sgu_ws])
    v_small = pack([v_norm_mix, v_norm_mlp, v_final_norm, full_gain(v_sgu_gain), v_sgu_bs, v_sgu_ws])
    g_pack = jnp.pad(g_small, (0, N_DEV * small_rows * LANES - n_small)).reshape(-1, LANES)
    sm = [g_pack] + list(_adamw_small(g_pack, w_small, m_small, v_small, name="adamw_small"))

    def unpack(flat2d):
        flat = flat2d.reshape(-1)
        out = [flat[offs[k]:offs[k] + sizes[k]].reshape(shapes[k]) for k in range(len(sizes))]
        out[2] = out[2].reshape(D)
        out[3] = lax.dynamic_slice(out[3], (0, me * shard_w), (2, shard_w))
        return out

    outs = []
    for k, big_u in enumerate(zip(u_wqkv, u_wo, u_win, u_wout, u_w1, u_w2)):
        s_nm, s_nl, s_fn, s_gain, s_bs, s_ws = unpack(sm[k])
        b_wqkv, b_wo, b_win, b_wout, b_w1, b_w2 = big_u
        outs += [s_nm, s_nl, b_wqkv, b_wo, b_win, s_gain, s_ws, s_bs, b_wout, b_w1, b_w2, s_fn]
    return (loss, grad_x, *outs)
```

```python
import functools

import jax
import jax.numpy as jnp
from jax import lax
from jax.experimental import pallas as pl
from jax.experimental.pallas import tpu as pltpu

F32 = jnp.float32
BF16 = jnp.bfloat16

N_DEV = 8
D_MODEL = 1024
SEQ = 2048
DEPTH = 4
SB_HEAD_DIM = 64
SGU_CHUNK = 128
SGU_FFN = 2 * D_MODEL
SGU_GROUPS = 8
SGU_GROUP_W = SGU_FFN // SGU_GROUPS
EPS = 1e-6

ADAM_LR = 0.001
ADAM_B1 = 0.9
ADAM_B2 = 0.999
ADAM_EPS = 1e-08
ADAM_WD = 0.01
ADAM_STEP = 10

LANES = 128
SUBLANES = 8
VMEM_LIMIT = 48 * 1024 * 1024
EXP_ZERO_BELOW = -104.0

MESH = pl.DeviceIdType.MESH
ANY = pl.BlockSpec(memory_space=pl.ANY)


def _params(*sem):
    return pltpu.CompilerParams(dimension_semantics=sem, vmem_limit_bytes=VMEM_LIMIT)


def _dot(a, b):
    return jnp.dot(a, b, preferred_element_type=F32)


def _dot_nt(a, b):
    return lax.dot_general(a, b, (((1,), (1,)), ((), ())), preferred_element_type=F32)


def _dot_tn(a, b):
    return lax.dot_general(a, b, (((0,), (0,)), ((), ())), preferred_element_type=F32)


def _split_bf16(x):
    hi = x.astype(BF16)
    lo = (x - hi.astype(F32)).astype(BF16)
    return hi, lo


def _relu2(av):
    t = jnp.maximum(av, jnp.zeros_like(av))
    return t * t


def _mm_nn(a, w, l, *, out_dtype, name, res=None, a_act=None, tm=1024, tn=None, pb=1):
    M, K = a.shape
    _, P, K2, n = w.shape
    assert K2 == K
    tn = n if tn is None else tn
    assert n % tn == 0 and (tn == n or pb == 1) and P % pb == 0
    tm = min(tm, M)
    assert M % tm == 0
    jn, width = n // tn, pb * tn

    def body(*refs):
        a_ref, w_ref, o_ref = refs[0], refs[1], refs[-1]
        av = a_ref[...]
        if a_act == "relu2":
            av = _relu2(av)
        av = av.astype(BF16)
        for p in range(pb):
            sl = slice(p * tn, (p + 1) * tn)
            r = _dot(av, w_ref[0, p])
            if res is not None:
                r = r + refs[2][:, sl]
            o_ref[:, sl] = r.astype(out_dtype)

    in_specs = [pl.BlockSpec((tm, K), lambda i, j: (i, 0)),
                pl.BlockSpec((1, pb, K, tn), lambda i, j: (l, j // jn, 0, j % jn))]
    args = [a, w]
    if res is not None:
        in_specs.append(pl.BlockSpec((tm, width), lambda i, j: (i, j)))
        args.append(res)
    return pl.pallas_call(
        body, grid=(M // tm, P * n // width), in_specs=in_specs,
        out_specs=pl.BlockSpec((tm, width), lambda i, j: (i, j)),
        out_shape=jax.ShapeDtypeStruct((M, P * n), out_dtype),
        compiler_params=_params("parallel", "parallel"), name=name)(*args)


def _mm_nt(a, w, l, *, out_dtype, name, act_src=None, tm=1024, tn=512):
    M, K = a.shape
    _, P, Nout, kc = w.shape
    assert K == P * kc
    tm, tn = min(tm, M), min(tn, Nout)
    assert M % tm == 0 and Nout % tn == 0

    def body(*refs):
        a_ref, w_ref, o_ref = refs[0], refs[1], refs[-1]
        r = None
        for p in range(P):
            d = _dot_nt(a_ref[:, p * kc:(p + 1) * kc].astype(BF16), w_ref[0, p])
            r = d if r is None else r + d
        if act_src is not None:
            r = r * (2.0 * jnp.maximum(refs[2][...].astype(F32), 0.0))
        o_ref[...] = r.astype(out_dtype)

    in_specs = [pl.BlockSpec((tm, K), lambda i, j: (i, 0)),
                pl.BlockSpec((1, P, tn, kc), lambda i, j: (l, 0, j, 0))]
    args = [a, w]
    if act_src is not None:
        in_specs.append(pl.BlockSpec((tm, tn), lambda i, j: (i, j)))
        args.append(act_src)
    return pl.pallas_call(
        body, grid=(M // tm, Nout // tn), in_specs=in_specs,
        out_specs=pl.BlockSpec((tm, tn), lambda i, j: (i, j)),
        out_shape=jax.ShapeDtypeStruct((M, Nout), out_dtype),
        compiler_params=_params("parallel", "parallel"), name=name)(*args)


def _mm_tn(a, b, *, shards, name, a_act=None, tm=2048, tk=1024, pb=1):
    M, K = a.shape
    M2, N = b.shape
    assert M2 == M
    n = N // shards
    tm, tk = min(tm, M), min(tk, K)
    assert M % tm == 0 and K % tk == 0 and shards % pb == 0
    width, nm = pb * n, M // tm

    def body(a_ref, b_ref, o_ref, acc):
        m = pl.program_id(2)
        av = a_ref[...]
        if a_act == "relu2":
            av = _relu2(av)

        @pl.when(m == 0)
        def _():
            acc[...] = jnp.zeros_like(acc)

        acc[...] += _dot_tn(av.astype(BF16), b_ref[...].astype(BF16))

        @pl.when(m == nm - 1)
        def _():
            for p in range(pb):
                o_ref[p] = acc[:, p * n:(p + 1) * n].astype(BF16)

    return pl.pallas_call(
        body, grid=(K // tk, N // width, nm),
        in_specs=[pl.BlockSpec((tm, tk), lambda i, j, m: (m, i)),
                  pl.BlockSpec((tm, width), lambda i, j, m: (m, j))],
        out_specs=pl.BlockSpec((pb, tk, n), lambda i, j, m: (j, i, 0)),
        out_shape=jax.ShapeDtypeStruct((shards, K, n), BF16),
        scratch_shapes=[pltpu.VMEM((tk, width), F32)],
        compiler_params=_params("parallel", "parallel", "arbitrary"), name=name)(a, b)


def _rms_fwd(x, gain, *, name, tr=512):
    T, D = x.shape

    def body(x_ref, g_ref, o_ref):
        xv = x_ref[...]
        r = lax.rsqrt(jnp.mean(xv * xv, axis=-1, keepdims=True) + EPS)
        o_ref[...] = (xv * r * g_ref[...]).astype(BF16)

    return pl.pallas_call(
        body, grid=(T // tr,),
        in_specs=[pl.BlockSpec((tr, D), lambda i: (i, 0)), pl.BlockSpec((1, D), lambda i: (0, 0))],
        out_specs=pl.BlockSpec((tr, D), lambda i: (i, 0)),
        out_shape=jax.ShapeDtypeStruct((T, D), BF16),
        compiler_params=_params("parallel"), name=name)(x, gain)


def _rms_bwd(dh, x, gain, dres, *, name, tr=512):
    T, D = x.shape
    nr = T // tr

    def body(dh_ref, x_ref, g_ref, dres_ref, dx_ref, dxb_ref, dg_ref, acc):
        i = pl.program_id(0)
        xv = x_ref[...]
        r = lax.rsqrt(jnp.mean(xv * xv, axis=-1, keepdims=True) + EPS)
        xhat = xv * r
        dhv = dh_ref[...].astype(F32)
        dxhat = dhv * g_ref[...]
        dx = dres_ref[...] + r * (dxhat - xhat * jnp.mean(dxhat * xhat, axis=-1, keepdims=True))
        dx_ref[...] = dx
        dxb_ref[...] = dx.astype(BF16)
        part = jnp.sum((dhv * xhat).reshape(tr // SUBLANES, SUBLANES, D), axis=0)

        @pl.when(i == 0)
        def _():
            acc[...] = part

        @pl.when(i > 0)
        def _():
            acc[...] += part

        @pl.when(i == nr - 1)
        def _():
            dg_ref[...] = jnp.sum(acc[...], axis=0, keepdims=True)

    row = pl.BlockSpec((tr, D), lambda i: (i, 0))
    vec = pl.BlockSpec((1, D), lambda i: (0, 0))
    return pl.pallas_call(
        body, grid=(nr,), in_specs=[row, row, vec, row], out_specs=[row, row, vec],
        out_shape=[jax.ShapeDtypeStruct((T, D), F32), jax.ShapeDtypeStruct((T, D), BF16),
                   jax.ShapeDtypeStruct((1, D), F32)],
        scratch_shapes=[pltpu.VMEM((SUBLANES, D), F32)],
        compiler_params=_params("arbitrary"), name=name)(dh, x, gain, dres)


def _loss_head(x, gain, target, *, name, tr=512):
    T, D = x.shape
    nr = T // tr

    def body(x_ref, g_ref, t_ref, dx_ref, dxb_ref, sq_ref, dg_ref, sq_acc, dg_acc):
        i = pl.program_id(0)
        xv = x_ref[...]
        g = g_ref[...]
        r = lax.rsqrt(jnp.mean(xv * xv, axis=-1, keepdims=True) + EPS)
        xhat = xv * r
        err = xhat * g - t_ref[...]
        dy = err * (1.0 / D)
        dxhat = dy * g
        dx = r * (dxhat - xhat * jnp.mean(dxhat * xhat, axis=-1, keepdims=True))
        dx_ref[...] = dx
        dxb_ref[...] = dx.astype(BF16)
        sq = jnp.sum((err * err).reshape(tr // SUBLANES, SUBLANES, D), axis=0)
        dg = jnp.sum((dy * xhat).reshape(tr // SUBLANES, SUBLANES, D), axis=0)

        @pl.when(i == 0)
        def _():
            sq_acc[...] = sq
            dg_acc[...] = dg

        @pl.when(i > 0)
        def _():
            sq_acc[...] += sq
            dg_acc[...] += dg

        @pl.when(i == nr - 1)
        def _():
            sq_ref[...] = sq_acc[...]
            dg_ref[...] = jnp.sum(dg_acc[...], axis=0, keepdims=True)

    row = pl.BlockSpec((tr, D), lambda i: (i, 0))
    vec = pl.BlockSpec((1, D), lambda i: (0, 0))
    part = pl.BlockSpec((SUBLANES, D), lambda i: (0, 0))
    return pl.pallas_call(
        body, grid=(nr,), in_specs=[row, vec, row], out_specs=[row, row, part, vec],
        out_shape=[jax.ShapeDtypeStruct((T, D), F32), jax.ShapeDtypeStruct((T, D), BF16),
                   jax.ShapeDtypeStruct((SUBLANES, D), F32), jax.ShapeDtypeStruct((1, D), F32)],
        scratch_shapes=[pltpu.VMEM((SUBLANES, D), F32), pltpu.VMEM((SUBLANES, D), F32)],
        compiler_params=_params("arbitrary"), name=name)(x, gain, target)


def _head0_lanes():
    return lax.broadcasted_iota(jnp.int32, (1, LANES), 1) < SB_HEAD_DIM


def _stack_heads(x):
    zero = jnp.zeros_like(x)
    h0 = _head0_lanes()
    return jnp.concatenate([jnp.where(h0, x, zero), jnp.where(h0, zero, x)], axis=0)


def _unstack_heads(y, tq):
    return jnp.where(_head0_lanes(), y[:tq], y[tq:])


def _past_mask(tq, tk):
    row = lax.broadcasted_iota(jnp.int32, (2 * tq, tk), 0) & (tq - 1)
    col = lax.broadcasted_iota(jnp.int32, (2 * tq, tk), 1)
    return col < row


def _sb_block(z, past):
    e = jnp.exp(-jnp.abs(z))
    l = -(jnp.maximum(z, 0.0) + jnp.log(1.0 + e))
    if past is not None:
        l = jnp.where(past, l, 0.0)
    return l, e


def _suffix_matrix(tk):
    j = lax.broadcasted_iota(jnp.int32, (tk, tk), 0)
    s = lax.broadcasted_iota(jnp.int32, (tk, tk), 1)
    return (j > s).astype(BF16)


def _suffix_sum(x, u):
    hi, lo = _split_bf16(x)
    return _dot(hi, u) + _dot(lo, u)


class _Carried:
    def __init__(self, comm, grid):
        self.comm, self.grid = comm, grid
        self.n_op = len(comm.operands) if comm else 0
        self.n_tgt = len(comm.targets) if comm else 0

    def split(self, refs, n_in, n_out, n_scratch):
        self.n_in, self.n_out, self.n_scratch = n_in, n_out, n_scratch
        a = n_in + self.n_op
        b = a + n_out + self.n_tgt
        return refs[:n_in], refs[a:a + n_out], refs[b:b + n_scratch]

    def emit(self, refs):
        if self.comm is None:
            return
        step, n_steps = 0, 1
        for d, size in enumerate(self.grid):
            step = step * size + pl.program_id(d)
            n_steps *= size
        a = self.n_in + self.n_op + self.n_out
        self.comm.emit(step, n_steps, refs[self.n_in:self.n_in + len(self.comm.sources)],
                       refs[a:a + self.n_tgt], refs[a + self.n_tgt + self.n_scratch:])

    def pallas_call(self, body, *, in_specs, out_specs, out_shape, scratch_shapes, args, name):
        comm = self.comm
        n_in, n_out = len(in_specs), len(out_specs)
        aliases = {}
        if comm is not None:
            in_specs = in_specs + [ANY] * self.n_op
            out_specs = out_specs + [ANY] * self.n_tgt
            out_shape = out_shape + comm.out_shapes
            scratch_shapes = scratch_shapes + comm.sems
            args = args + comm.operands
            aliases = comm.aliases(n_in, n_out)
        sem = ("parallel",) * (len(self.grid) - 1) + ("arbitrary",) if comm is None else ("arbitrary",) * len(self.grid)
        results = pl.pallas_call(
            body, grid=self.grid, in_specs=in_specs, out_specs=out_specs, out_shape=out_shape,
            scratch_shapes=scratch_shapes, input_output_aliases=aliases,
            compiler_params=_params(*sem), name=name)(*args)
        return results[:n_out], results[n_out:]


def _sb_fwd(qkv, *, batch, seq, name, comm=None, tq=256):
    T, D3 = qkv.shape
    D = D3 // 3
    nhp = D // LANES
    tk = tq
    nq = seq // tq
    scale = SB_HEAD_DIM ** -0.5

    def body(*refs):
        (q_ref, k_ref, v_ref), (o_ref,), (acc,) = carried.split(refs, 3, 1, 1)
        qi = pl.program_id(2)
        carried.emit(refs)
        qs = _stack_heads(q_ref[...]) * scale
        past = _past_mask(tq, tk)
        u = _suffix_matrix(tk)

        def block(kb, c, diag):
            ks = pl.multiple_of(kb * tk, tk)
            z = _dot_nt(qs, k_ref[pl.ds(ks, tk), :])
            l, _ = _sb_block(z, past if diag else None)
            arg = z + l + _suffix_sum(l, u)
            a = jnp.exp(arg if c is None else arg + c)
            if diag:
                a = jnp.where(past, a, 0.0)
            a_hi, a_lo = _split_bf16(a)
            vblk = v_ref[pl.ds(ks, tk), :]
            return _dot(a_hi, vblk) + _dot(a_lo, vblk), jnp.sum(l, axis=1, keepdims=True)

        @pl.when(qi == 0)
        def _():
            o_d, _ = block(0, None, True)
            o_ref[...] = _unstack_heads(o_d, tq)

        @pl.when(qi > 0)
        def _():
            o_d, c_d = block(qi, None, True)
            o_p, c_p = block(qi - 1, c_d, False)
            acc[...] = o_d + o_p

            def cond(st):
                kb, c = st
                return jnp.logical_and(kb >= 0, jnp.max(c) > EXP_ZERO_BELOW)

            def step(st):
                kb, c = st
                o_n, c_n = block(kb, c, False)
                acc[...] += o_n
                return kb - 1, c + c_n

            lax.while_loop(cond, step, (qi - 2, c_d + c_p))
            o_ref[...] = _unstack_heads(acc[...], tq)

    carried = _Carried(comm, (batch, nhp, nq))
    (o,), comm_results = carried.pallas_call(
        body,
        in_specs=[pl.BlockSpec((tq, LANES), lambda b, p, i: (b * nq + i, p)),
                  pl.BlockSpec((seq, LANES), lambda b, p, i: (b, nhp + p)),
                  pl.BlockSpec((seq, LANES), lambda b, p, i: (b, 2 * nhp + p))],
        out_specs=[pl.BlockSpec((tq, LANES), lambda b, p, i: (b * nq + i, p))],
        out_shape=[jax.ShapeDtypeStruct((T, D), F32)],
        scratch_shapes=[pltpu.VMEM((2 * tq, LANES), F32)],
        args=[qkv, qkv, qkv], name=name)
    return o, comm_results


def _sb_bwd(qkv, o, do, *, batch, seq, name, comm=None, tq=256):
    T, D3 = qkv.shape
    D = D3 // 3
    nhp = D // LANES
    tk = tq
    nq = seq // tq
    scale = SB_HEAD_DIM ** -0.5

    def body(*refs):
        ins, outs, scratch = carried.split(refs, 5, 3, 3)
        q_ref, k_ref, v_ref, o_ref, do_ref = ins
        dq_ref, dk_ref, dv_ref = outs
        dq_acc, dk_acc, dv_acc = scratch
        qi = pl.program_id(2)
        carried.emit(refs)

        @pl.when(qi == 0)
        def _():
            dk_acc[...] = jnp.zeros_like(dk_acc)
            dv_acc[...] = jnp.zeros_like(dv_acc)

        qs = _stack_heads(q_ref[...]) * scale
        dov = do_ref[...]
        dos = _stack_heads(dov)
        dsum = jnp.sum(_stack_heads(dov.astype(F32) * o_ref[...]), axis=1, keepdims=True)
        past = _past_mask(tq, tk)
        u = _suffix_matrix(tk)

        def block(kb, c, gc, diag):
            ks = pl.multiple_of(kb * tk, tk)
            kblk = k_ref[pl.ds(ks, tk), :]
            vblk = v_ref[pl.ds(ks, tk), :]
            z = _dot_nt(qs, kblk)
            l, _ = _sb_block(z, past if diag else None)
            arg = z + l + _suffix_sum(l, u)
            a = jnp.exp(arg if c is None else arg + c)
            if diag:
                a = jnp.where(past, a, 0.0)
            beta = 1.0 - jnp.exp(l)
            g = a * _dot_nt(dos, vblk)
            gs = _suffix_sum(g, u)
            dz = g - beta * (dsum - (gs if gc is None else gs + gc))
            if diag:
                dz = jnp.where(past, dz, 0.0)
            dzb = dz.astype(BF16)
            dk_acc[pl.ds(ks, tk), :] += _dot_tn(dzb, qs)
            dv_acc[pl.ds(ks, tk), :] += _dot_tn(a.astype(BF16), dos)
            return (_dot(dzb, kblk), jnp.sum(l, axis=1, keepdims=True),
                    jnp.sum(g, axis=1, keepdims=True))

        def finish(dq2):
            dq_ref[...] = (_unstack_heads(dq2, tq) * scale).astype(BF16)

        @pl.when(qi == 0)
        def _():
            dq_d, _, _ = block(0, None, None, True)
            finish(dq_d)

        @pl.when(qi > 0)
        def _():
            dq_d, c_d, g_d = block(qi, None, None, True)
            dq_p, c_p, g_p = block(qi - 1, c_d, g_d, False)
            dq_acc[...] = dq_d + dq_p

            def cond(st):
                kb, c, gc = st
                return jnp.logical_and(kb >= 0, jnp.max(c) > EXP_ZERO_BELOW)

            def step(st):
                kb, c, gc = st
                dq_n, c_n, g_n = block(kb, c, gc, False)
                dq_acc[...] += dq_n
                return kb - 1, c + c_n, gc + g_n

            lax.while_loop(cond, step, (qi - 2, c_d + c_p, g_d + g_p))
            finish(dq_acc[...])

        @pl.when(qi == nq - 1)
        def _():
            dk_ref[...] = dk_acc[...].astype(BF16)
            dv_ref[...] = dv_acc[...].astype(BF16)

    qspec = pl.BlockSpec((tq, LANES), lambda b, p, i: (b * nq + i, p))
    sspec = pl.BlockSpec((seq, LANES), lambda b, p, i: (b, p))
    out = jax.ShapeDtypeStruct((T, D), BF16)
    carried = _Carried(comm, (batch, nhp, nq))
    return carried.pallas_call(
        body,
        in_specs=[qspec,
                  pl.BlockSpec((seq, LANES), lambda b, p, i: (b, nhp + p)),
                  pl.BlockSpec((seq, LANES), lambda b, p, i: (b, 2 * nhp + p)),
                  qspec, qspec],
        out_specs=[qspec, sspec, sspec], out_shape=[out, out, out],
        scratch_shapes=[pltpu.VMEM((2 * tq, LANES), F32), pltpu.VMEM((seq, LANES), F32),
                        pltpu.VMEM((seq, LANES), F32)],
        args=[qkv, qkv, qkv, o, do], name=name)


_GELU_C = 0.7978845608028654
_GELU_A = 0.044715


def _gelu(x):
    t = jnp.tanh(_GELU_C * (x + _GELU_A * x * x * x))
    return 0.5 * x * (1.0 + t), t


def _gelu_grad(x, t):
    return 0.5 * (1.0 + t) + 0.5 * x * (1.0 - t * t) * (_GELU_C * (1.0 + 3.0 * _GELU_A * x * x))


def _causal_ws(ws_ref, g):
    t = lax.broadcasted_iota(jnp.int32, (SGU_CHUNK, SGU_CHUNK), 0)
    s = lax.broadcasted_iota(jnp.int32, (SGU_CHUNK, SGU_CHUNK), 1)
    return jnp.where(s <= t, ws_ref[g], 0.0)


def _sgu_fwd(uvp, gain, ws, bst, *, name):
    T, F2 = uvp.shape
    F = F2 // 2
    C, G, W = SGU_CHUNK, SGU_GROUPS, SGU_GROUP_W

    def body(uv_ref, g_ref, ws_ref, bs_ref, y_ref):
        uvf = uv_ref[...].astype(F32)
        uv, _ = _gelu(uvf)
        u, v = uv[:, :F], uv[:, F:]
        r = lax.rsqrt(jnp.mean(v * v, axis=-1, keepdims=True) + EPS)
        vn = (v * r * g_ref[...]).astype(BF16)
        for g in range(G):
            sl = slice(g * W, (g + 1) * W)
            mixed = _dot(_causal_ws(ws_ref, g).astype(BF16), vn[:, sl]) + bs_ref[:, g:g + 1]
            y_ref[:, sl] = (u[:, sl] * mixed).astype(BF16)

    return pl.pallas_call(
        body, grid=(T // C,),
        in_specs=[pl.BlockSpec((C, F2), lambda i: (i, 0)), pl.BlockSpec((1, F), lambda i: (0, 0)),
                  pl.BlockSpec((G, C, C), lambda i: (0, 0, 0)), pl.BlockSpec((C, G), lambda i: (0, 0))],
        out_specs=pl.BlockSpec((C, F), lambda i: (i, 0)),
        out_shape=jax.ShapeDtypeStruct((T, F), BF16),
        compiler_params=_params("parallel"), name=name)(uvp, gain, ws, bst)


def _sgu_bwd(uvp, dy, gain, ws, bst, *, name):
    T, F2 = uvp.shape
    F = F2 // 2
    C, G, W = SGU_CHUNK, SGU_GROUPS, SGU_GROUP_W
    nc = T // C

    def body(uv_ref, dy_ref, g_ref, ws_ref, bs_ref, duv_ref, dg_ref, dws_ref, dbs_ref,
             dg_acc, dws_acc, dbs_acc):
        i = pl.program_id(0)

        @pl.when(i == 0)
        def _():
            dg_acc[...] = jnp.zeros_like(dg_acc)
            dws_acc[...] = jnp.zeros_like(dws_acc)
            dbs_acc[...] = jnp.zeros_like(dbs_acc)

        uvf = uv_ref[...].astype(F32)
        uv, th = _gelu(uvf)
        dgelu = _gelu_grad(uvf, th)
        u, v = uv[:, :F], uv[:, F:]
        r = lax.rsqrt(jnp.mean(v * v, axis=-1, keepdims=True) + EPS)
        vhat = v * r
        gain_v = g_ref[...]
        vn = (vhat * gain_v).astype(BF16)
        dyv = dy_ref[...].astype(F32)
        lane8 = lax.broadcasted_iota(jnp.int32, (1, G), 1)
        dvn_parts = []
        dbs_new = jnp.zeros((C, G), F32)
        for g in range(G):
            sl = slice(g * W, (g + 1) * W)
            wsg = _causal_ws(ws_ref, g)
            mixed = _dot(wsg.astype(BF16), vn[:, sl]) + bs_ref[:, g:g + 1]
            duv_ref[:, sl] = (dyv[:, sl] * mixed * dgelu[:, sl]).astype(BF16)
            dmix = dyv[:, sl] * u[:, sl]
            dbs_new = dbs_new + jnp.where(lane8 == g, jnp.sum(dmix, axis=1, keepdims=True), 0.0)
            dmix_b = dmix.astype(BF16)
            dws_acc[g] += _dot_nt(dmix_b, vn[:, sl])
            dvn_parts.append(_dot(wsg.T.astype(BF16), dmix_b))
        dbs_acc[...] += dbs_new
        dvn = jnp.concatenate(dvn_parts, axis=1)
        dg_acc[...] += jnp.sum((dvn * vhat).reshape(C // SUBLANES, SUBLANES, F), axis=0)
        dvhat = dvn * gain_v
        dv = r * (dvhat - vhat * jnp.mean(dvhat * vhat, axis=-1, keepdims=True))
        duv_ref[:, F:] = (dv * dgelu[:, F:]).astype(BF16)

        @pl.when(i == nc - 1)
        def _():
            dg_ref[...] = jnp.sum(dg_acc[...], axis=0, keepdims=True)
            t = lax.broadcasted_iota(jnp.int32, (G, C, C), 1)
            s = lax.broadcasted_iota(jnp.int32, (G, C, C), 2)
            dws_ref[...] = jnp.where(s <= t, dws_acc[...], 0.0)
            dbs_ref[...] = dbs_acc[...]

    return pl.pallas_call(
        body, grid=(nc,),
        in_specs=[pl.BlockSpec((C, F2), lambda i: (i, 0)), pl.BlockSpec((C, F), lambda i: (i, 0)),
                  pl.BlockSpec((1, F), lambda i: (0, 0)), pl.BlockSpec((G, C, C), lambda i: (0, 0, 0)),
                  pl.BlockSpec((C, G), lambda i: (0, 0))],
        out_specs=[pl.BlockSpec((C, F2), lambda i: (i, 0)), pl.BlockSpec((1, F), lambda i: (0, 0)),
                   pl.BlockSpec((G, C, C), lambda i: (0, 0, 0)), pl.BlockSpec((C, G), lambda i: (0, 0))],
        out_shape=[jax.ShapeDtypeStruct((T, F2), BF16), jax.ShapeDtypeStruct((1, F), F32),
                   jax.ShapeDtypeStruct((G, C, C), F32), jax.ShapeDtypeStruct((C, G), F32)],
        scratch_shapes=[pltpu.VMEM((SUBLANES, F), F32), pltpu.VMEM((G, C, C), F32), pltpu.VMEM((C, G), F32)],
        compiler_params=_params("arbitrary"), name=name)(uvp, dy, gain, ws, bst)


def _my_place():
    return lax.axis_index("x"), lax.axis_index("y"), lax.axis_index("c")


def _all_gather(shards, *, name):
    nf = len(shards)
    items = [(lambda ins, f=f: ins[f], lambda outs, p, f=f: outs[f].at[:, p]) for f in range(nf)]
    targets = [jax.ShapeDtypeStruct((s.shape[0], N_DEV) + s.shape[1:], s.dtype) for s in shards]
    return _run_comm(_Comm(shards, targets, len(items), _gather_emit(items, [list(range(nf))])), name=name)


class _Comm:
    def __init__(self, sources, targets, n_items, emit):
        self.sources, self.targets, self.n_items, self.emit = list(sources), list(targets), n_items, emit
        self.filled = [t for t in self.targets if not isinstance(t, jax.ShapeDtypeStruct)]
        self.operands = self.sources + self.filled
        self.out_shapes = [jax.ShapeDtypeStruct(t.shape, t.dtype) for t in self.targets]
        self.sems = [pltpu.SemaphoreType.DMA((n_items, 7)), pltpu.SemaphoreType.DMA((n_items, 7)),
                     pltpu.SemaphoreType.DMA((n_items,))]

    def aliases(self, first_operand, first_result):
        pos = {id(t): k for k, t in enumerate(self.targets)}
        return {first_operand + len(self.sources) + a: first_result + pos[id(t)]
                for a, t in enumerate(self.filled)}


def _run_comm(comm, *, name):
    n_op, n_out = len(comm.operands), len(comm.targets)

    def body(*refs):
        comm.emit(0, 1, refs[:len(comm.sources)], refs[n_op:n_op + n_out], refs[n_op + n_out:])

    return pl.pallas_call(
        body, in_specs=[ANY] * n_op, out_specs=[ANY] * n_out, out_shape=comm.out_shapes,
        scratch_shapes=comm.sems, input_output_aliases=comm.aliases(0, 0), name=name)(*comm.operands)


def _at_steps(step, n_steps, phases):
    if n_steps == 1:
        for _, fn in phases:
            fn()
        return
    marks = {}
    for frac, fn in phases:
        marks.setdefault(min(int(frac * n_steps), n_steps - 1), []).append(fn)
    for mark, fns in sorted(marks.items()):
        @pl.when(step == mark)
        def _(fns=fns):
            for fn in fns:
                fn()


def _gather_emit(items, groups):
    def emit(step, n_steps, ins, outs, sems):
        send_sems, recv_sems, local_sems = sems
        x, y, c = _my_place()
        me, sibling = (x, y, c), (x, y, 1 - c)
        chips = [(1 - x, y), (x, 1 - y), (1 - x, 1 - y)]

        def copy(i, k, block, to, own=False):
            src_of, dst_of = items[i]
            dst = dst_of(outs, 4 * block[0] + 2 * block[1] + block[2])
            return pltpu.make_async_remote_copy(
                src_ref=src_of(ins) if own else dst, dst_ref=dst,
                send_sem=send_sems.at[i, k], recv_sem=recv_sems.at[i, k],
                device_id=to, device_id_type=MESH)

        def local(i):
            src_of, dst_of = items[i]
            return pltpu.make_async_copy(src_of(ins), dst_of(outs, 4 * x + 2 * y + c), local_sems.at[i])

        def first(i):
            return [copy(i, 0, me, sibling, own=True)] + [
                copy(i, 1 + j, me, (*chip, c), own=True) for j, chip in enumerate(chips)]

        def start():
            for i in range(len(items)):
                local(i).start()
                for cp in first(i):
                    cp.start()

        def forward(group):
            for j, chip in enumerate(chips):
                for i in group:
                    copy(i, 1 + j, (*chip, c), me).wait_recv()
                    copy(i, 4 + j, (*chip, c), sibling).start()

        def finish():
            for i in range(len(items)):
                copy(i, 0, sibling, me).wait_recv()
                for j, chip in enumerate(chips):
                    copy(i, 4 + j, (*chip, 1 - c), me).wait_recv()
            for i in range(len(items)):
                for cp in first(i) + [copy(i, 4 + j, (*chip, c), sibling) for j, chip in enumerate(chips)]:
                    cp.wait_send()
                local(i).wait()

        phases = [(0.0, start)]
        for g, group in enumerate(groups):
            phases.append(((g + 1) / len(groups), functools.partial(forward, group)))
        phases.append((1.0, finish))
        _at_steps(step, n_steps, phases)

    return emit


def _exchange_emit(items):
    def emit(step, n_steps, ins, outs, sems):
        send_sems, recv_sems, local_sems = sems
        x, y, c = _my_place()
        me = 4 * x + 2 * y + c

        def peer_of(k):
            return x ^ (k >> 2), y ^ ((k >> 1) & 1), c ^ (k & 1)

        def copy(i, k):
            src_of, dst_of = items[i]
            px, py, pc = peer_of(k)
            return pltpu.make_async_remote_copy(
                src_ref=src_of(ins, 4 * px + 2 * py + pc), dst_ref=dst_of(outs, me),
                send_sem=send_sems.at[i, k - 1], recv_sem=recv_sems.at[i, k - 1],
                device_id=(px, py, pc), device_id_type=MESH)

        def arrival(i, k):
            src_of, dst_of = items[i]
            px, py, pc = peer_of(k)
            peer = 4 * px + 2 * py + pc
            return pltpu.make_async_remote_copy(
                src_ref=src_of(ins, peer), dst_ref=dst_of(outs, peer),
                send_sem=send_sems.at[i, k - 1], recv_sem=recv_sems.at[i, k - 1],
                device_id=(x, y, c), device_id_type=MESH)

        def local(i):
            src_of, dst_of = items[i]
            return pltpu.make_async_copy(src_of(ins, me), dst_of(outs, me), local_sems.at[i])

        def start():
            for i in range(len(items)):
                local(i).start()
            for k in range(1, N_DEV):
                for i in range(len(items)):
                    copy(i, k).start()

        def finish():
            for k in range(1, N_DEV):
                for i in range(len(items)):
                    arrival(i, k).wait_recv()
            for k in range(1, N_DEV):
                for i in range(len(items)):
                    copy(i, k).wait_send()
            for i in range(len(items)):
                local(i).wait()

        _at_steps(step, n_steps, [(0.0, start), (1.0, finish)])

    return emit


def _adam_math(g, w, m, v):
    m = ADAM_B1 * m + (1.0 - ADAM_B1) * g
    v = ADAM_B2 * v + (1.0 - ADAM_B2) * (g * g)
    m_hat = m / (1.0 - ADAM_B1 ** ADAM_STEP)
    v_hat = v / (1.0 - ADAM_B2 ** ADAM_STEP)
    delta = -ADAM_LR * (m_hat / (jnp.sqrt(v_hat) + ADAM_EPS) + ADAM_WD * w)
    return delta, m, v


def _sum_adamw(parts, w, m, v, *, name, tr=256):
    L, nd, R, C = parts.shape
    tr = min(tr, R)
    assert R % tr == 0

    def body(p_ref, w_ref, m_ref, v_ref, g_ref, d_ref, nm_ref, nv_ref):
        g = p_ref[0, 0].astype(F32)
        for q in range(1, nd):
            g = g + p_ref[0, q].astype(F32)
        d, nm, nv = _adam_math(g, w_ref[0], m_ref[0], v_ref[0])
        g_ref[0] = g
        d_ref[0] = d
        nm_ref[0] = nm
        nv_ref[0] = nv

    blk = pl.BlockSpec((1, tr, C), lambda l, i: (l, i, 0))
    out = jax.ShapeDtypeStruct((L, R, C), F32)
    return pl.pallas_call(
        body, grid=(L, R // tr),
        in_specs=[pl.BlockSpec((1, nd, tr, C), lambda l, i: (l, 0, i, 0)), blk, blk, blk],
        out_specs=[blk] * 4, out_shape=[out] * 4,
        compiler_params=_params("parallel", "parallel"), name=name)(parts, w, m, v)


def _sum_parts(parts, *, name):
    nd, R, C = parts.shape

    def body(p_ref, o_ref):
        g = p_ref[0]
        for q in range(1, nd):
            g = g + p_ref[q]
        o_ref[...] = g

    return pl.pallas_call(
        body, out_shape=jax.ShapeDtypeStruct((R, C), F32),
        in_specs=[pl.BlockSpec(memory_space=pltpu.VMEM)],
        out_specs=pl.BlockSpec(memory_space=pltpu.VMEM), name=name)(parts)


def _adamw_small(g, w, m, v, *, name):
    def body(g_ref, w_ref, m_ref, v_ref, d_ref, nm_ref, nv_ref):
        d, nm, nv = _adam_math(g_ref[...], w_ref[...], m_ref[...], v_ref[...])
        d_ref[...] = d
        nm_ref[...] = nm
        nv_ref[...] = nv

    vm = pl.BlockSpec(memory_space=pltpu.VMEM)
    out = jax.ShapeDtypeStruct(g.shape, F32)
    return pl.pallas_call(body, out_shape=[out] * 3, in_specs=[vm] * 4, out_specs=[vm] * 3,
                          name=name)(g, w, m, v)


def kernel(x, norm_mix, norm_mlp, sb_wqkv, sb_wo, sgu_win, sgu_gain, sgu_ws, sgu_bs, sgu_wout, mlp_w1, mlp_w2, final_norm, loss_target, m_norm_mix, m_norm_mlp, m_sb_wqkv, m_sb_wo, m_sgu_win, m_sgu_gain, m_sgu_ws, m_sgu_bs, m_sgu_wout, m_mlp_w1, m_mlp_w2, m_final_norm, v_norm_mix, v_norm_mlp, v_sb_wqkv, v_sb_wo, v_sgu_win, v_sgu_gain, v_sgu_ws, v_sgu_bs, v_sgu_wout, v_mlp_w1, v_mlp_w2, v_final_norm):
    batch, seq, D = x.shape
    T = batch * seq
    x0 = x.reshape(T, D)
    target = loss_target.reshape(T, D)

    WQKV, WO, WIN, WOUT, W1, W2, GAIN = range(7)
    big = [sb_wqkv, sb_wo, sgu_win, sgu_wout, mlp_w1, mlp_w2]
    shards = [w.astype(BF16) for w in big] + [sgu_gain[:, None, :]]
    layer_weights = [[(WQKV, 0), (WO, 0), (W1, 0), (W2, 0)], [(WIN, 0), (WOUT, 0), (W1, 1), (W2, 1)],
                     [(WQKV, 1), (WO, 1), (W1, 2), (W2, 2)], [(WIN, 1), (WOUT, 1), (W1, 3), (W2, 3)]]

    def weight_item(f, l):
        return (lambda ins: ins[f].at[l], lambda outs, p: outs[f].at[l, p])

    first = [weight_item(f, l) for f, l in layer_weights[0]]
    first.append((lambda ins: ins[GAIN], lambda outs, p: outs[GAIN].at[:, p]))
    targets = [jax.ShapeDtypeStruct((s.shape[0], N_DEV) + s.shape[1:], s.dtype) for s in shards]
    gathered0 = _run_comm(_Comm(shards, targets, len(first), _gather_emit(first, [list(range(len(first)))])),
                          name="gather_layer0")
    rest = [weight_item(f, l) for lw in layer_weights[1:] for f, l in lw]
    gather_rest = _Comm(shards[:GAIN], list(gathered0[:GAIN]), len(rest),
                        _gather_emit(rest, [[0, 1, 2, 3], [4, 5, 6, 7], [8, 9, 10, 11]]))
    gain_sgu = gathered0[GAIN].reshape(-1, 1, SGU_FFN)

    def rows_joined(g):
        return g.reshape(g.shape[0], 1, N_DEV * g.shape[2], g.shape[3])

    saved = []
    xs = x0
    for i in range(DEPTH):
        j = i // 2
        h = _rms_fwd(xs, norm_mix[i:i + 1], name=f"norm_mix{i}")
        if i == 0:
            qkv = _mm_nn(h, gathered0[WQKV], j, out_dtype=BF16, name=f"qkv{i}", pb=2)
            o, gathered = _sb_fwd(qkv, batch=batch, seq=seq, name=f"sb_fwd{i}", comm=gather_rest)
            g_wqkv, g_wo, g_win, g_wout, g_w1, g_w2 = gathered
            g_wo, g_wout, g_w2 = rows_joined(g_wo), rows_joined(g_wout), rows_joined(g_w2)
            x_mid = _mm_nn(o, g_wo, j, out_dtype=F32, res=xs, name=f"wo{i}")
            mix = (h, qkv, o)
        elif i % 2 == 0:
            qkv = _mm_nn(h, g_wqkv, j, out_dtype=BF16, name=f"qkv{i}", pb=2)
            o, _ = _sb_fwd(qkv, batch=batch, seq=seq, name=f"sb_fwd{i}")
            x_mid = _mm_nn(o, g_wo, j, out_dtype=F32, res=xs, name=f"wo{i}")
            mix = (h, qkv, o)
        else:
            gain_j = gain_sgu[j]
            bst = sgu_bs[j].T
            uvp = _mm_nn(h, g_win, j, out_dtype=BF16, name=f"win{i}", pb=2)
            yv = _sgu_fwd(uvp, gain_j, sgu_ws[j], bst, name=f"sgu_fwd{i}")
            x_mid = _mm_nn(yv, g_wout, j, out_dtype=F32, res=xs, name=f"wout{i}")
            mix = (h, uvp, yv, gain_j, bst)
        h2 = _rms_fwd(x_mid, norm_mlp[i:i + 1], name=f"norm_mlp{i}")
        a = _mm_nn(h2, g_w1, i, out_dtype=BF16, name=f"w1_{i}", pb=2)
        x_out = _mm_nn(a, g_w2, i, out_dtype=F32, res=x_mid, a_act="relu2", name=f"w2_{i}", tm=512, tn=512)
        saved.append((xs, mix, x_mid, h2, a))
        xs = x_out

    dx, dxb, sq, d_final = _loss_head(xs, final_norm.reshape(1, D), target, name="loss_head")
    loss = lax.psum(0.5 * jnp.sum(sq) / D, ("x", "y", "c"))

    SMALL = GAIN
    stacks = {f: jax.ShapeDtypeStruct((w.shape[0], N_DEV) + w.shape[1:], BF16) for f, w in enumerate(big)}
    pending = []

    def row_shards(p):
        return p.reshape(N_DEV, p.shape[1] // N_DEV, p.shape[2])

    def exchange_pending():
        fams = sorted({f for _, f, _ in pending})
        items = [(lambda ins, p, a=a: ins[a].at[p], lambda outs, q, t=fams.index(f), l=l: outs[t].at[l, q])
                 for a, (_, f, l) in enumerate(pending)]
        comm = _Comm([part for part, _, _ in pending], [stacks[f] for f in fams], len(items), _exchange_emit(items))
        pending.clear()
        return comm, fams

    d_norm_mix, d_norm_mlp = [None] * DEPTH, [None] * DEPTH
    d_gain, d_ws, d_bs = [None] * 2, [None] * 2, [None] * 2
    for i in reversed(range(DEPTH)):
        j = i // 2
        xs, mix, x_mid, h2, a = saved[i]
        da = _mm_nt(dxb, g_w2, i, out_dtype=BF16, act_src=a, name=f"d_a{i}", tn=1024)
        pending.append((row_shards(_mm_tn(a, dxb, shards=1, a_act="relu2", name=f"d_w2_{i}")), W2, i))
        pending.append((_mm_tn(h2, da, shards=N_DEV, name=f"d_w1_{i}", pb=2), W1, i))
        dh2 = _mm_nt(da, g_w1, i, out_dtype=F32, name=f"d_h2_{i}")
        dx, dxb, d_norm_mlp[i] = _rms_bwd(dh2, x_mid, norm_mlp[i:i + 1], dx, name=f"d_norm_mlp{i}")
        if i % 2 == 0:
            h, qkv, o = mix
            do = _mm_nt(dxb, g_wo, j, out_dtype=BF16, name=f"d_o{i}", tn=1024)
            pending.append((row_shards(_mm_tn(o, dxb, shards=1, name=f"d_wo{i}")), WO, j))
            comm, fams = exchange_pending()
            (dq, dk, dv), results = _sb_bwd(qkv, o, do, batch=batch, seq=seq, name=f"sb_bwd{i}", comm=comm)
            stacks.update(zip(fams, results))
            dqkv = jnp.concatenate([dq, dk, dv], axis=1)
            pending.append((_mm_tn(h, dqkv, shards=N_DEV, name=f"d_wqkv{i}", pb=2), WQKV, j))
            dh = _mm_nt(dqkv, g_wqkv, j, out_dtype=F32, name=f"d_h_sb{i}")
        else:
            h, uvp, yv, gain_j, bst = mix
            dy = _mm_nt(dxb, g_wout, j, out_dtype=BF16, name=f"d_y{i}", tn=1024)
            pending.append((row_shards(_mm_tn(yv, dxb, shards=1, name=f"d_wout{i}")), WOUT, j))
            duv, d_gain[j], d_ws[j], dbst = _sgu_bwd(uvp, dy, gain_j, sgu_ws[j], bst, name=f"sgu_bwd{i}")
            d_bs[j] = dbst.T
            pending.append((_mm_tn(h, duv, shards=N_DEV, name=f"d_win{i}", pb=2), WIN, j))
            dh = _mm_nt(duv, g_win, j, out_dtype=F32, name=f"d_h_sgu{i}")
        dx, dxb, d_norm_mix[i] = _rms_bwd(dh, xs, norm_mix[i:i + 1], dx, name=f"d_norm_mix{i}")
    grad_x = dx.reshape(batch, seq, D)

    small = [jnp.concatenate(d_norm_mix, 0), jnp.concatenate(d_norm_mlp, 0), d_final,
             jnp.concatenate(d_gain, 0), jnp.stack(d_bs, 0), jnp.stack(d_ws, 0)]
    small_flat = jnp.concatenate([s.reshape(-1) for s in small])
    n_small = small_flat.shape[0]
    small_rows = -(-n_small // (N_DEV * SUBLANES * LANES)) * SUBLANES
    small_flat = jnp.pad(small_flat, (0, N_DEV * small_rows * LANES - n_small))
    stacks[SMALL] = jax.ShapeDtypeStruct((1, N_DEV, small_rows, LANES), F32)
    pending.append((small_flat.reshape(N_DEV, small_rows, LANES), SMALL, 0))
    comm, fams = exchange_pending()
    stacks.update(zip(fams, _run_comm(comm, name="exchange_last")))
    r_wqkv, r_wo, r_win, r_wout, r_w1, r_w2, r_small = [stacks[f] for f in range(SMALL + 1)]

    u_wqkv = _sum_adamw(r_wqkv, sb_wqkv, m_sb_wqkv, v_sb_wqkv, name="adamw_wqkv")
    u_wo = _sum_adamw(r_wo, sb_wo, m_sb_wo, v_sb_wo, name="adamw_wo")
    u_win = _sum_adamw(r_win, sgu_win, m_sgu_win, v_sgu_win, name="adamw_win")
    u_wout = _sum_adamw(r_wout, sgu_wout, m_sgu_wout, v_sgu_wout, name="adamw_wout")
    u_w1 = _sum_adamw(r_w1, mlp_w1, m_mlp_w1, v_mlp_w1, name="adamw_w1")
    u_w2 = _sum_adamw(r_w2, mlp_w2, m_mlp_w2, v_mlp_w2, name="adamw_w2")

    small_sum = _sum_parts(r_small[0], name="sum_small")
    g_small = _all_gather([small_sum[None]], name="gather_small")[0].reshape(-1)[:n_small]

    shapes = [s.shape for s in small]
    sizes = [s.size for s in small]
    offs = [sum(sizes[:k]) for k in range(len(sizes))]
    me = 4 * lax.axis_index("x") + 2 * lax.axis_index("y") + lax.axis_index("c")
    shard_w = SGU_FFN // N_DEV

    def pack(arrs):
        flat = jnp.concatenate([a_.reshape(-1) for a_ in arrs])
        return jnp.pad(flat, (0, N_DEV * small_rows * LANES - n_small)).reshape(-1, LANES)

    def full_gain(gshard):
        return lax.dynamic_update_slice(jnp.zeros((2, SGU_FFN), F32), gshard, (0, me * shard_w))

    w_small = pack([norm_mix, norm_mlp, final_norm, full_gain(sgu_gain), sgu_bs, sgu_ws])
    m_small = pack([m_norm_mix, m_norm_mlp, m_final_norm, full_gain(m_sgu_gain), m_sgu_bs, m_sgu_ws])
    v_small = pack([v_norm_mix, v_norm_mlp, v_final_norm, full_gain(v_sgu_gain), v_sgu_bs, v_sgu_ws])
    g_pack = jnp.pad(g_small, (0, N_DEV * small_rows * LANES - n_small)).reshape(-1, LANES)
    sm = [g_pack] + list(_adamw_small(g_pack, w_small, m_small, v_small, name="adamw_small"))

    def unpack(flat2d):
        flat = flat2d.reshape(-1)
        out = [flat[offs[k]:offs[k] + sizes[k]].reshape(shapes[k]) for k in range(len(sizes))]
        out[2] = out[2].reshape(D)
        out[3] = lax.dynamic_slice(out[3], (0, me * shard_w), (2, shard_w))
        return out

    outs = []
    for k, big_u in enumerate(zip(u_wqkv, u_wo, u_win, u_wout, u_w1, u_w2)):
        s_nm, s_nl, s_fn, s_gain, s_bs, s_ws = unpack(sm[k])
        b_wqkv, b_wo, b_win, b_wout, b_w1, b_w2 = big_u
        outs += [s_nm, s_nl, b_wqkv, b_wo, b_win, s_gain, s_ws, s_bs, b_wout, b_w1, b_w2, s_fn]
    return (loss, grad_x, *outs)
```

```python
import functools

import jax
import jax.numpy as jnp
from jax import lax
from jax.experimental import pallas as pl
from jax.experimental.pallas import tpu as pltpu

F32 = jnp.float32
BF16 = jnp.bfloat16

N_DEV = 8
D_MODEL = 1024
SEQ = 2048
DEPTH = 4
SB_HEAD_DIM = 64
SGU_CHUNK = 128
SGU_FFN = 2 * D_MODEL
SGU_GROUPS = 8
SGU_GROUP_W = SGU_FFN // SGU_GROUPS
EPS = 1e-6

ADAM_LR = 0.001
ADAM_B1 = 0.9
ADAM_B2 = 0.999
ADAM_EPS = 1e-08
ADAM_WD = 0.01
ADAM_STEP = 10

LANES = 128
SUBLANES = 8
VMEM_LIMIT = 48 * 1024 * 1024
EXP_ZERO_BELOW = -104.0

MESH = pl.DeviceIdType.MESH
ANY = pl.BlockSpec(memory_space=pl.ANY)


def _params(*sem):
    return pltpu.CompilerParams(dimension_semantics=sem, vmem_limit_bytes=VMEM_LIMIT)


def _dot(a, b):
    return jnp.dot(a, b, preferred_element_type=F32)


def _dot_nt(a, b):
    return lax.dot_general(a, b, (((1,), (1,)), ((), ())), preferred_element_type=F32)


def _dot_tn(a, b):
    return lax.dot_general(a, b, (((0,), (0,)), ((), ())), preferred_element_type=F32)


def _split_bf16(x):
    hi = x.astype(BF16)
    lo = (x - hi.astype(F32)).astype(BF16)
    return hi, lo


def _relu2(av):
    t = jnp.maximum(av, jnp.zeros_like(av))
    return t * t


def _mm_nn(a, w, l, *, out_dtype, name, res=None, a_act=None, tm=1024, tn=None, pb=1):
    M, K = a.shape
    _, P, K2, n = w.shape
    assert K2 == K
    tn = n if tn is None else tn
    assert n % tn == 0 and (tn == n or pb == 1) and P % pb == 0
    tm = min(tm, M)
    assert M % tm == 0
    jn, width = n // tn, pb * tn

    def body(*refs):
        a_ref, w_ref, o_ref = refs[0], refs[1], refs[-1]
        av = a_ref[...]
        if a_act == "relu2":
            av = _relu2(av)
        av = av.astype(BF16)
        for p in range(pb):
            sl = slice(p * tn, (p + 1) * tn)
            r = _dot(av, w_ref[0, p])
            if res is not None:
                r = r + refs[2][:, sl]
            o_ref[:, sl] = r.astype(out_dtype)

    in_specs = [pl.BlockSpec((tm, K), lambda i, j: (i, 0)),
                pl.BlockSpec((1, pb, K, tn), lambda i, j: (l, j // jn, 0, j % jn))]
    args = [a, w]
    if res is not None:
        in_specs.append(pl.BlockSpec((tm, width), lambda i, j: (i, j)))
        args.append(res)
    return pl.pallas_call(
        body, grid=(M // tm, P * n // width), in_specs=in_specs,
        out_specs=pl.BlockSpec((tm, width), lambda i, j: (i, j)),
        out_shape=jax.ShapeDtypeStruct((M, P * n), out_dtype),
        compiler_params=_params("parallel", "parallel"), name=name)(*args)


def _mm_nt(a, w, l, *, out_dtype, name, act_src=None, tm=1024, tn=512):
    M, K = a.shape
    _, P, Nout, kc = w.shape
    assert K == P * kc
    tm, tn = min(tm, M), min(tn, Nout)
    assert M % tm == 0 and Nout % tn == 0

    def body(*refs):
        a_ref, w_ref, o_ref = refs[0], refs[1], refs[-1]
        r = None
        for p in range(P):
            d = _dot_nt(a_ref[:, p * kc:(p + 1) * kc].astype(BF16), w_ref[0, p])
            r = d if r is None else r + d
        if act_src is not None:
            r = r * (2.0 * jnp.maximum(refs[2][...].astype(F32), 0.0))
        o_ref[...] = r.astype(out_dtype)

    in_specs = [pl.BlockSpec((tm, K), lambda i, j: (i, 0)),
                pl.BlockSpec((1, P, tn, kc), lambda i, j: (l, 0, j, 0))]
    args = [a, w]
    if act_src is not None:
        in_specs.append(pl.BlockSpec((tm, tn), lambda i, j: (i, j)))
        args.append(act_src)
    return pl.pallas_call(
        body, grid=(M // tm, Nout // tn), in_specs=in_specs,
        out_specs=pl.BlockSpec((tm, tn), lambda i, j: (i, j)),
        out_shape=jax.ShapeDtypeStruct((M, Nout), out_dtype),
        compiler_params=_params("parallel", "parallel"), name=name)(*args)


def _mm_tn(a, b, *, shards, name, a_act=None, tm=2048, tk=1024, pb=1):
    M, K = a.shape
    M2, N = b.shape
    assert M2 == M
    n = N // shards
    tm, tk = min(tm, M), min(tk, K)
    assert M % tm == 0 and K % tk == 0 and shards % pb == 0
    width, nm = pb * n, M // tm

    def body(a_ref, b_ref, o_ref, acc):
        m = pl.program_id(2)
        av = a_ref[...]
        if a_act == "relu2":
            av = _relu2(av)

        @pl.when(m == 0)
        def _():
            acc[...] = jnp.zeros_like(acc)

        acc[...] += _dot_tn(av.astype(BF16), b_ref[...].astype(BF16))

        @pl.when(m == nm - 1)
        def _():
            for p in range(pb):
                o_ref[p] = acc[:, p * n:(p + 1) * n].astype(BF16)

    return pl.pallas_call(
        body, grid=(K // tk, N // width, nm),
        in_specs=[pl.BlockSpec((tm, tk), lambda i, j, m: (m, i)),
                  pl.BlockSpec((tm, width), lambda i, j, m: (m, j))],
        out_specs=pl.BlockSpec((pb, tk, n), lambda i, j, m: (j, i, 0)),
        out_shape=jax.ShapeDtypeStruct((shards, K, n), BF16),
        scratch_shapes=[pltpu.VMEM((tk, width), F32)],
        compiler_params=_params("parallel", "parallel", "arbitrary"), name=name)(a, b)


def _rms_fwd(x, gain, *, name, tr=512):
    T, D = x.shape

    def body(x_ref, g_ref, o_ref):
        xv = x_ref[...]
        r = lax.rsqrt(jnp.mean(xv * xv, axis=-1, keepdims=True) + EPS)
        o_ref[...] = (xv * r * g_ref[...]).astype(BF16)

    return pl.pallas_call(
        body, grid=(T // tr,),
        in_specs=[pl.BlockSpec((tr, D), lambda i: (i, 0)), pl.BlockSpec((1, D), lambda i: (0, 0))],
        out_specs=pl.BlockSpec((tr, D), lambda i: (i, 0)),
        out_shape=jax.ShapeDtypeStruct((T, D), BF16),
        compiler_params=_params("parallel"), name=name)(x, gain)


def _mm_nt_rms_bwd(a, w, l, x, gain, dres, *, name, tm=512):
    M, K = a.shape
    _, P, D, kc = w.shape
    assert K == P * kc and x.shape == (M, D)
    tm = min(tm, M)
    nr = M // tm

    def body(a_ref, w_ref, x_ref, g_ref, dres_ref, dx_ref, dxb_ref, dg_ref, acc):
        i = pl.program_id(0)
        dhv = None
        for p in range(P):
            d = _dot_nt(a_ref[:, p * kc:(p + 1) * kc], w_ref[0, p])
            dhv = d if dhv is None else dhv + d
        xv = x_ref[...]
        r = lax.rsqrt(jnp.mean(xv * xv, axis=-1, keepdims=True) + EPS)
        xhat = xv * r
        dxhat = dhv * g_ref[...]
        dx = dres_ref[...] + r * (dxhat - xhat * jnp.mean(dxhat * xhat, axis=-1, keepdims=True))
        dx_ref[...] = dx
        dxb_ref[...] = dx.astype(BF16)
        part = jnp.sum((dhv * xhat).reshape(tm // SUBLANES, SUBLANES, D), axis=0)

        @pl.when(i == 0)
        def _():
            acc[...] = jnp.zeros_like(acc)

        acc[...] += part

        @pl.when(i == nr - 1)
        def _():
            dg_ref[...] = jnp.sum(acc[...], axis=0, keepdims=True)

    row = pl.BlockSpec((tm, D), lambda i: (i, 0))
    vec = pl.BlockSpec((1, D), lambda i: (0, 0))
    return pl.pallas_call(
        body, grid=(nr,),
        in_specs=[pl.BlockSpec((tm, K), lambda i: (i, 0)),
                  pl.BlockSpec((1, P, D, kc), lambda i: (l, 0, 0, 0), pipeline_mode=pl.Buffered(1)),
                  row, vec, row],
        out_specs=[row, row, vec],
        out_shape=[jax.ShapeDtypeStruct((M, D), F32), jax.ShapeDtypeStruct((M, D), BF16),
                   jax.ShapeDtypeStruct((1, D), F32)],
        scratch_shapes=[pltpu.VMEM((SUBLANES, D), F32)],
        compiler_params=_params("arbitrary"), name=name)(a, w, x, gain, dres)


def _loss_head(x, gain, target, *, name, tr=512):
    T, D = x.shape
    nr = T // tr

    def body(x_ref, g_ref, t_ref, dx_ref, dxb_ref, sq_ref, dg_ref, sq_acc, dg_acc):
        i = pl.program_id(0)
        xv = x_ref[...]
        g = g_ref[...]
        r = lax.rsqrt(jnp.mean(xv * xv, axis=-1, keepdims=True) + EPS)
        xhat = xv * r
        err = xhat * g - t_ref[...]
        dy = err * (1.0 / D)
        dxhat = dy * g
        dx = r * (dxhat - xhat * jnp.mean(dxhat * xhat, axis=-1, keepdims=True))
        dx_ref[...] = dx
        dxb_ref[...] = dx.astype(BF16)
        sq = jnp.sum((err * err).reshape(tr // SUBLANES, SUBLANES, D), axis=0)
        dg = jnp.sum((dy * xhat).reshape(tr // SUBLANES, SUBLANES, D), axis=0)

        @pl.when(i == 0)
        def _():
            sq_acc[...] = sq
            dg_acc[...] = dg

        @pl.when(i > 0)
        def _():
            sq_acc[...] += sq
            dg_acc[...] += dg

        @pl.when(i == nr - 1)
        def _():
            sq_ref[...] = sq_acc[...]
            dg_ref[...] = jnp.sum(dg_acc[...], axis=0, keepdims=True)

    row = pl.BlockSpec((tr, D), lambda i: (i, 0))
    vec = pl.BlockSpec((1, D), lambda i: (0, 0))
    part = pl.BlockSpec((SUBLANES, D), lambda i: (0, 0))
    return pl.pallas_call(
        body, grid=(nr,), in_specs=[row, vec, row], out_specs=[row, row, part, vec],
        out_shape=[jax.ShapeDtypeStruct((T, D), F32), jax.ShapeDtypeStruct((T, D), BF16),
                   jax.ShapeDtypeStruct((SUBLANES, D), F32), jax.ShapeDtypeStruct((1, D), F32)],
        scratch_shapes=[pltpu.VMEM((SUBLANES, D), F32), pltpu.VMEM((SUBLANES, D), F32)],
        compiler_params=_params("arbitrary"), name=name)(x, gain, target)


def _head0_lanes():
    return lax.broadcasted_iota(jnp.int32, (1, LANES), 1) < SB_HEAD_DIM


def _stack_heads(x):
    zero = jnp.zeros_like(x)
    h0 = _head0_lanes()
    return jnp.concatenate([jnp.where(h0, x, zero), jnp.where(h0, zero, x)], axis=0)


def _unstack_heads(y, tq):
    return jnp.where(_head0_lanes(), y[:tq], y[tq:])


def _past_mask(tq, tk):
    row = lax.broadcasted_iota(jnp.int32, (2 * tq, tk), 0) & (tq - 1)
    col = lax.broadcasted_iota(jnp.int32, (2 * tq, tk), 1)
    return col < row


def _sb_logs(z, past):
    minus_abs = lax.bitcast_convert_type(
        lax.bitcast_convert_type(z, jnp.uint32) | jnp.uint32(0x80000000), F32)
    log_beta = jnp.minimum(z, 0.0) - jnp.log(1.0 + jnp.exp(minus_abs))
    l = log_beta - z
    if past is not None:
        l = jnp.where(past, l, 0.0)
    return log_beta, l


def _suffix_matrix(tk):
    j = lax.broadcasted_iota(jnp.int32, (2 * tk, tk), 0) & (tk - 1)
    s = lax.broadcasted_iota(jnp.int32, (2 * tk, tk), 1)
    return (j > s).astype(BF16)


def _suffix_sum(x, u2):
    hi, lo = _split_bf16(x)
    return _dot(jnp.concatenate([hi, lo], axis=1), u2)


class _Carried:
    def __init__(self, comm, grid):
        self.comm, self.grid = comm, grid
        self.n_op = len(comm.operands) if comm else 0
        self.n_tgt = len(comm.targets) if comm else 0

    def split(self, refs, n_in, n_out, n_scratch):
        self.n_in, self.n_out, self.n_scratch = n_in, n_out, n_scratch
        a = n_in + self.n_op
        b = a + n_out + self.n_tgt
        return refs[:n_in], refs[a:a + n_out], refs[b:b + n_scratch]

    def emit(self, refs):
        if self.comm is None:
            return
        step, n_steps = 0, 1
        for d, size in enumerate(self.grid):
            step = step * size + pl.program_id(d)
            n_steps *= size
        a = self.n_in + self.n_op + self.n_out
        self.comm.emit(step, n_steps, refs[self.n_in:self.n_in + len(self.comm.sources)],
                       refs[a:a + self.n_tgt], refs[a + self.n_tgt + self.n_scratch:])

    def pallas_call(self, body, *, in_specs, out_specs, out_shape, scratch_shapes, args, name):
        comm = self.comm
        n_in, n_out = len(in_specs), len(out_specs)
        aliases = {}
        if comm is not None:
            in_specs = in_specs + [ANY] * self.n_op
            out_specs = out_specs + [ANY] * self.n_tgt
            out_shape = out_shape + comm.out_shapes
            scratch_shapes = scratch_shapes + comm.sems
            args = args + comm.operands
            aliases = comm.aliases(n_in, n_out)
        sem = ("parallel",) * (len(self.grid) - 1) + ("arbitrary",) if comm is None else ("arbitrary",) * len(self.grid)
        results = pl.pallas_call(
            body, grid=self.grid, in_specs=in_specs, out_specs=out_specs, out_shape=out_shape,
            scratch_shapes=scratch_shapes, input_output_aliases=aliases,
            compiler_params=_params(*sem), name=name)(*args)
        return results[:n_out], results[n_out:]


def _sb_fwd(qkv, *, batch, seq, name, comm=None, tq=256):
    T, D3 = qkv.shape
    D = D3 // 3
    nhp = D // LANES
    tk = tq
    nq = seq // tq
    scale = SB_HEAD_DIM ** -0.5

    def body(*refs):
        (q_ref, k_ref, v_ref), (o_ref,), (acc,) = carried.split(refs, 3, 1, 1)
        qi = pl.program_id(2)
        carried.emit(refs)
        qs = _stack_heads(q_ref[...]) * scale
        past = _past_mask(tq, tk)
        u = _suffix_matrix(tk)

        def block(kb, c, diag):
            ks = pl.multiple_of(kb * tk, tk)
            z = _dot_nt(qs, k_ref[pl.ds(ks, tk), :])
            log_beta, l = _sb_logs(z, past if diag else None)
            arg = log_beta + _suffix_sum(l, u)
            a = jnp.exp(arg if c is None else arg + c)
            if diag:
                a = jnp.where(past, a, 0.0)
            return _dot(a.astype(BF16), v_ref[pl.ds(ks, tk), :]), jnp.sum(l, axis=1, keepdims=True)

        @pl.when(qi == 0)
        def _():
            o_d, _ = block(0, None, True)
            o_ref[...] = _unstack_heads(o_d, tq)

        @pl.when(qi > 0)
        def _():
            o_d, c_d = block(qi, None, True)
            o_p, c_p = block(qi - 1, c_d, False)
            acc[...] = o_d + o_p

            def cond(st):
                kb, c = st
                return jnp.logical_and(kb >= 0, jnp.max(c) > EXP_ZERO_BELOW)

            def step(st):
                kb, c = st
                o_n, c_n = block(kb, c, False)
                acc[...] += o_n
                return kb - 1, c + c_n

            lax.while_loop(cond, step, (qi - 2, c_d + c_p))
            o_ref[...] = _unstack_heads(acc[...], tq)

    carried = _Carried(comm, (batch, nhp, nq))
    (o,), comm_results = carried.pallas_call(
        body,
        in_specs=[pl.BlockSpec((tq, LANES), lambda b, p, i: (b * nq + i, p)),
                  pl.BlockSpec((seq, LANES), lambda b, p, i: (b, nhp + p)),
                  pl.BlockSpec((seq, LANES), lambda b, p, i: (b, 2 * nhp + p))],
        out_specs=[pl.BlockSpec((tq, LANES), lambda b, p, i: (b * nq + i, p))],
        out_shape=[jax.ShapeDtypeStruct((T, D), F32)],
        scratch_shapes=[pltpu.VMEM((2 * tq, LANES), F32)],
        args=[qkv, qkv, qkv], name=name)
    return o, comm_results


def _sb_bwd(qkv, o, do, *, batch, seq, name, comm=None, tq=256):
    T, D3 = qkv.shape
    D = D3 // 3
    nhp = D // LANES
    tk = tq
    nq = seq // tq
    scale = SB_HEAD_DIM ** -0.5

    def body(*refs):
        ins, outs, scratch = carried.split(refs, 5, 3, 3)
        q_ref, k_ref, v_ref, o_ref, do_ref = ins
        dq_ref, dk_ref, dv_ref = outs
        dq_acc, dk_acc, dv_acc = scratch
        qi = pl.program_id(2)
        carried.emit(refs)

        @pl.when(qi == 0)
        def _():
            dk_acc[...] = jnp.zeros_like(dk_acc)
            dv_acc[...] = jnp.zeros_like(dv_acc)

        qs = _stack_heads(q_ref[...]) * scale
        dov = do_ref[...]
        dos = _stack_heads(dov)
        dsum = jnp.sum(_stack_heads(dov.astype(F32) * o_ref[...]), axis=1, keepdims=True)
        past = _past_mask(tq, tk)
        u = _suffix_matrix(tk)

        def block(kb, c, gc, diag):
            ks = pl.multiple_of(kb * tk, tk)
            kblk = k_ref[pl.ds(ks, tk), :]
            vblk = v_ref[pl.ds(ks, tk), :]
            z = _dot_nt(qs, kblk)
            log_beta, l = _sb_logs(z, past if diag else None)
            arg = log_beta + _suffix_sum(l, u)
            a = jnp.exp(arg if c is None else arg + c)
            if diag:
                a = jnp.where(past, a, 0.0)
            a = a.astype(BF16)
            beta = 1.0 - jnp.exp(l)
            g = a.astype(F32) * _dot_nt(dos, vblk)
            gs = _suffix_sum(g, u)
            dz = g - beta * (dsum - (gs if gc is None else gs + gc))
            if diag:
                dz = jnp.where(past, dz, 0.0)
            dzb = dz.astype(BF16)
            dk_acc[pl.ds(ks, tk), :] += _dot_tn(dzb, qs)
            dv_acc[pl.ds(ks, tk), :] += _dot_tn(a, dos)
            return (_dot(dzb, kblk), jnp.sum(l, axis=1, keepdims=True),
                    jnp.sum(g, axis=1, keepdims=True))

        def finish(dq2):
            dq_ref[...] = (_unstack_heads(dq2, tq) * scale).astype(BF16)

        @pl.when(qi == 0)
        def _():
            dq_d, _, _ = block(0, None, None, True)
            finish(dq_d)

        @pl.when(qi > 0)
        def _():
            dq_d, c_d, g_d = block(qi, None, None, True)
            dq_p, c_p, g_p = block(qi - 1, c_d, g_d, False)
            dq_acc[...] = dq_d + dq_p

            def cond(st):
                kb, c, gc = st
                return jnp.logical_and(kb >= 0, jnp.max(c) > EXP_ZERO_BELOW)

            def step(st):
                kb, c, gc = st
                dq_n, c_n, g_n = block(kb, c, gc, False)
                dq_acc[...] += dq_n
                return kb - 1, c + c_n, gc + g_n

            lax.while_loop(cond, step, (qi - 2, c_d + c_p, g_d + g_p))
            finish(dq_acc[...])

        @pl.when(qi == nq - 1)
        def _():
            dk_ref[...] = dk_acc[...].astype(BF16)
            dv_ref[...] = dv_acc[...].astype(BF16)

    qspec = pl.BlockSpec((tq, LANES), lambda b, p, i: (b * nq + i, p))
    sspec = pl.BlockSpec((seq, LANES), lambda b, p, i: (b, p))
    out = jax.ShapeDtypeStruct((T, D), BF16)
    carried = _Carried(comm, (batch, nhp, nq))
    return carried.pallas_call(
        body,
        in_specs=[qspec,
                  pl.BlockSpec((seq, LANES), lambda b, p, i: (b, nhp + p)),
                  pl.BlockSpec((seq, LANES), lambda b, p, i: (b, 2 * nhp + p)),
                  qspec, qspec],
        out_specs=[qspec, sspec, sspec], out_shape=[out, out, out],
        scratch_shapes=[pltpu.VMEM((2 * tq, LANES), F32), pltpu.VMEM((seq, LANES), F32),
                        pltpu.VMEM((seq, LANES), F32)],
        args=[qkv, qkv, qkv, o, do], name=name)


_GELU_C = 0.7978845608028654
_GELU_A = 0.044715


def _gelu(x, with_grad=False):
    xx = x * x
    t = jnp.tanh(x * (_GELU_C + (_GELU_C * _GELU_A) * xx))
    hx = 0.5 * x
    y = hx + hx * t
    if not with_grad:
        return y
    grad = (0.5 + 0.5 * t) + (hx * (1.0 - t * t)) * (_GELU_C + (3.0 * _GELU_C * _GELU_A) * xx)
    return y, grad


def _causal_ws(ws_ref, g):
    t = lax.broadcasted_iota(jnp.int32, (SGU_CHUNK, SGU_CHUNK), 0)
    s = lax.broadcasted_iota(jnp.int32, (SGU_CHUNK, SGU_CHUNK), 1)
    return jnp.where(s <= t, ws_ref[g], 0.0)


def _sgu_fwd(uvp, gain, ws, bst, *, name):
    T, F2 = uvp.shape
    F = F2 // 2
    C, G, W = SGU_CHUNK, SGU_GROUPS, SGU_GROUP_W

    def body(uv_ref, g_ref, ws_ref, bs_ref, y_ref):
        uvf = uv_ref[...].astype(F32)
        uv = _gelu(uvf)
        u, v = uv[:, :F], uv[:, F:]
        r = lax.rsqrt(jnp.mean(v * v, axis=-1, keepdims=True) + EPS)
        vn = (v * r * g_ref[...]).astype(BF16)
        for g in range(G):
            sl = slice(g * W, (g + 1) * W)
            mixed = _dot(_causal_ws(ws_ref, g).astype(BF16), vn[:, sl]) + bs_ref[:, g:g + 1]
            y_ref[:, sl] = (u[:, sl] * mixed).astype(BF16)

    return pl.pallas_call(
        body, grid=(T // C,),
        in_specs=[pl.BlockSpec((C, F2), lambda i: (i, 0)), pl.BlockSpec((1, F), lambda i: (0, 0)),
                  pl.BlockSpec((G, C, C), lambda i: (0, 0, 0)), pl.BlockSpec((C, G), lambda i: (0, 0))],
        out_specs=pl.BlockSpec((C, F), lambda i: (i, 0)),
        out_shape=jax.ShapeDtypeStruct((T, F), BF16),
        compiler_params=_params("parallel"), name=name)(uvp, gain, ws, bst)


def _sgu_bwd(uvp, dy, gain, ws, bst, *, name):
    T, F2 = uvp.shape
    F = F2 // 2
    C, G, W = SGU_CHUNK, SGU_GROUPS, SGU_GROUP_W
    nc = T // C

    def body(uv_ref, dy_ref, g_ref, ws_ref, bs_ref, duv_ref, dg_ref, dws_ref, dbs_ref,
             dg_acc, dws_acc, dbs_acc):
        i = pl.program_id(0)

        @pl.when(i == 0)
        def _():
            dg_acc[...] = jnp.zeros_like(dg_acc)
            dws_acc[...] = jnp.zeros_like(dws_acc)
            dbs_acc[...] = jnp.zeros_like(dbs_acc)

        uvf = uv_ref[...].astype(F32)
        uv, dgelu = _gelu(uvf, with_grad=True)
        u, v = uv[:, :F], uv[:, F:]
        r = lax.rsqrt(jnp.mean(v * v, axis=-1, keepdims=True) + EPS)
        vhat = v * r
        gain_v = g_ref[...]
        vn = (vhat * gain_v).astype(BF16)
        dyv = dy_ref[...].astype(F32)
        lane8 = lax.broadcasted_iota(jnp.int32, (1, G), 1)
        dvn_parts = []
        dbs_new = jnp.zeros((C, G), F32)
        for g in range(G):
            sl = slice(g * W, (g + 1) * W)
            wsg = _causal_ws(ws_ref, g)
            mixed = _dot(wsg.astype(BF16), vn[:, sl]) + bs_ref[:, g:g + 1]
            duv_ref[:, sl] = (dyv[:, sl] * mixed * dgelu[:, sl]).astype(BF16)
            dmix = dyv[:, sl] * u[:, sl]
            dbs_new = dbs_new + jnp.where(lane8 == g, jnp.sum(dmix, axis=1, keepdims=True), 0.0)
            dmix_b = dmix.astype(BF16)
            dws_acc[g] += _dot_nt(dmix_b, vn[:, sl])
            dvn_parts.append(_dot(wsg.T.astype(BF16), dmix_b))
        dbs_acc[...] += dbs_new
        dvn = jnp.concatenate(dvn_parts, axis=1)
        dg_acc[...] += jnp.sum((dvn * vhat).reshape(C // SUBLANES, SUBLANES, F), axis=0)
        dvhat = dvn * gain_v
        dv = r * (dvhat - vhat * jnp.mean(dvhat * vhat, axis=-1, keepdims=True))
        duv_ref[:, F:] = (dv * dgelu[:, F:]).astype(BF16)

        @pl.when(i == nc - 1)
        def _():
            dg_ref[...] = jnp.sum(dg_acc[...], axis=0, keepdims=True)
            t = lax.broadcasted_iota(jnp.int32, (G, C, C), 1)
            s = lax.broadcasted_iota(jnp.int32, (G, C, C), 2)
            dws_ref[...] = jnp.where(s <= t, dws_acc[...], 0.0)
            dbs_ref[...] = dbs_acc[...]

    return pl.pallas_call(
        body, grid=(nc,),
        in_specs=[pl.BlockSpec((C, F2), lambda i: (i, 0)), pl.BlockSpec((C, F), lambda i: (i, 0)),
                  pl.BlockSpec((1, F), lambda i: (0, 0)), pl.BlockSpec((G, C, C), lambda i: (0, 0, 0)),
                  pl.BlockSpec((C, G), lambda i: (0, 0))],
        out_specs=[pl.BlockSpec((C, F2), lambda i: (i, 0)), pl.BlockSpec((1, F), lambda i: (0, 0)),
                   pl.BlockSpec((G, C, C), lambda i: (0, 0, 0)), pl.BlockSpec((C, G), lambda i: (0, 0))],
        out_shape=[jax.ShapeDtypeStruct((T, F2), BF16), jax.ShapeDtypeStruct((1, F), F32),
                   jax.ShapeDtypeStruct((G, C, C), F32), jax.ShapeDtypeStruct((C, G), F32)],
        scratch_shapes=[pltpu.VMEM((SUBLANES, F), F32), pltpu.VMEM((G, C, C), F32), pltpu.VMEM((C, G), F32)],
        compiler_params=_params("arbitrary"), name=name)(uvp, dy, gain, ws, bst)


def _my_place():
    return lax.axis_index("x"), lax.axis_index("y"), lax.axis_index("c")


def _all_gather(shards, *, name):
    nf = len(shards)
    items = [(lambda ins, f=f: ins[f], lambda outs, p, f=f: outs[f].at[:, p]) for f in range(nf)]
    targets = [jax.ShapeDtypeStruct((s.shape[0], N_DEV) + s.shape[1:], s.dtype) for s in shards]
    return _run_comm(_Comm(shards, targets, len(items), _gather_emit(items, [list(range(nf))])), name=name)


class _Comm:
    def __init__(self, sources, targets, n_items, emit):
        self.sources, self.targets, self.n_items, self.emit = list(sources), list(targets), n_items, emit
        self.filled = [t for t in self.targets if not isinstance(t, jax.ShapeDtypeStruct)]
        self.operands = self.sources + self.filled
        self.out_shapes = [jax.ShapeDtypeStruct(t.shape, t.dtype) for t in self.targets]
        self.sems = [pltpu.SemaphoreType.DMA((n_items, 7)), pltpu.SemaphoreType.DMA((n_items, 7)),
                     pltpu.SemaphoreType.DMA((n_items,))]

    def aliases(self, first_operand, first_result):
        pos = {id(t): k for k, t in enumerate(self.targets)}
        return {first_operand + len(self.sources) + a: first_result + pos[id(t)]
                for a, t in enumerate(self.filled)}


def _run_comm(comm, *, name):
    n_op, n_out = len(comm.operands), len(comm.targets)

    def body(*refs):
        comm.emit(0, 1, refs[:len(comm.sources)], refs[n_op:n_op + n_out], refs[n_op + n_out:])

    return pl.pallas_call(
        body, in_specs=[ANY] * n_op, out_specs=[ANY] * n_out, out_shape=comm.out_shapes,
        scratch_shapes=comm.sems, input_output_aliases=comm.aliases(0, 0), name=name)(*comm.operands)


def _at_steps(step, n_steps, phases):
    if n_steps == 1:
        for _, fn in phases:
            fn()
        return
    marks = {}
    for frac, fn in phases:
        marks.setdefault(min(int(frac * n_steps), n_steps - 1), []).append(fn)
    for mark, fns in sorted(marks.items()):
        @pl.when(step == mark)
        def _(fns=fns):
            for fn in fns:
                fn()


def _gather_emit(items, groups):
    def emit(step, n_steps, ins, outs, sems):
        send_sems, recv_sems, local_sems = sems
        x, y, c = _my_place()
        me, sibling = (x, y, c), (x, y, 1 - c)
        chips = [(1 - x, y), (x, 1 - y), (1 - x, 1 - y)]

        def copy(i, k, block, to, own=False):
            src_of, dst_of = items[i]
            dst = dst_of(outs, 4 * block[0] + 2 * block[1] + block[2])
            return pltpu.make_async_remote_copy(
                src_ref=src_of(ins) if own else dst, dst_ref=dst,
                send_sem=send_sems.at[i, k], recv_sem=recv_sems.at[i, k],
                device_id=to, device_id_type=MESH)

        def local(i):
            src_of, dst_of = items[i]
            return pltpu.make_async_copy(src_of(ins), dst_of(outs, 4 * x + 2 * y + c), local_sems.at[i])

        def first(i):
            return [copy(i, 0, me, sibling, own=True)] + [
                copy(i, 1 + j, me, (*chip, c), own=True) for j, chip in enumerate(chips)]

        def start():
            for i in range(len(items)):
                local(i).start()
                for cp in first(i):
                    cp.start()

        def forward(group):
            for j, chip in enumerate(chips):
                for i in group:
                    copy(i, 1 + j, (*chip, c), me).wait_recv()
                    copy(i, 4 + j, (*chip, c), sibling).start()

        def finish():
            for i in range(len(items)):
                copy(i, 0, sibling, me).wait_recv()
                for j, chip in enumerate(chips):
                    copy(i, 4 + j, (*chip, 1 - c), me).wait_recv()
            for i in range(len(items)):
                for cp in first(i) + [copy(i, 4 + j, (*chip, c), sibling) for j, chip in enumerate(chips)]:
                    cp.wait_send()
                local(i).wait()

        phases = [(0.0, start)]
        for g, group in enumerate(groups):
            phases.append(((g + 1) / len(groups), functools.partial(forward, group)))
        phases.append((1.0, finish))
        _at_steps(step, n_steps, phases)

    return emit


def _exchange_emit(items):
    def emit(step, n_steps, ins, outs, sems):
        send_sems, recv_sems, local_sems = sems
        x, y, c = _my_place()
        me = 4 * x + 2 * y + c

        def peer_of(k):
            return x ^ (k >> 2), y ^ ((k >> 1) & 1), c ^ (k & 1)

        def copy(i, k):
            src_of, dst_of = items[i]
            px, py, pc = peer_of(k)
            return pltpu.make_async_remote_copy(
                src_ref=src_of(ins, 4 * px + 2 * py + pc), dst_ref=dst_of(outs, me),
                send_sem=send_sems.at[i, k - 1], recv_sem=recv_sems.at[i, k - 1],
                device_id=(px, py, pc), device_id_type=MESH)

        def arrival(i, k):
            src_of, dst_of = items[i]
            px, py, pc = peer_of(k)
            peer = 4 * px + 2 * py + pc
            return pltpu.make_async_remote_copy(
                src_ref=src_of(ins, peer), dst_ref=dst_of(outs, peer),
                send_sem=send_sems.at[i, k - 1], recv_sem=recv_sems.at[i, k - 1],
                device_id=(x, y, c), device_id_type=MESH)

        def local(i):
            src_of, dst_of = items[i]
            return pltpu.make_async_copy(src_of(ins, me), dst_of(outs, me), local_sems.at[i])

        def start():
            for i in range(len(items)):
                local(i).start()
            for k in range(1, N_DEV):
                for i in range(len(items)):
                    copy(i, k).start()

        def finish():
            for k in range(1, N_DEV):
                for i in range(len(items)):
                    arrival(i, k).wait_recv()
            for k in range(1, N_DEV):
                for i in range(len(items)):
                    copy(i, k).wait_send()
            for i in range(len(items)):
                local(i).wait()

        _at_steps(step, n_steps, [(0.0, start), (1.0, finish)])

    return emit


def _adam_math(g, w, m, v):
    m = ADAM_B1 * m + (1.0 - ADAM_B1) * g
    v = ADAM_B2 * v + (1.0 - ADAM_B2) * (g * g)
    m_hat = m / (1.0 - ADAM_B1 ** ADAM_STEP)
    v_hat = v / (1.0 - ADAM_B2 ** ADAM_STEP)
    delta = -ADAM_LR * (m_hat / (jnp.sqrt(v_hat) + ADAM_EPS) + ADAM_WD * w)
    return delta, m, v


def _sum_adamw(parts, w, m, v, *, name, tr=256):
    L, nd, R, C = parts.shape
    tr = min(tr, R)
    assert R % tr == 0

    def body(p_ref, w_ref, m_ref, v_ref, g_ref, d_ref, nm_ref, nv_ref):
        g = p_ref[0, 0].astype(F32)
        for q in range(1, nd):
            g = g + p_ref[0, q].astype(F32)
        d, nm, nv = _adam_math(g, w_ref[0], m_ref[0], v_ref[0])
        g_ref[0] = g
        d_ref[0] = d
        nm_ref[0] = nm
        nv_ref[0] = nv

    blk = pl.BlockSpec((1, tr, C), lambda l, i: (l, i, 0))
    out = jax.ShapeDtypeStruct((L, R, C), F32)
    return pl.pallas_call(
        body, grid=(L, R // tr),
        in_specs=[pl.BlockSpec((1, nd, tr, C), lambda l, i: (l, 0, i, 0)), blk, blk, blk],
        out_specs=[blk] * 4, out_shape=[out] * 4,
        compiler_params=_params("parallel", "parallel"), name=name)(parts, w, m, v)


def _sum_parts(parts, *, name):
    nd, R, C = parts.shape

    def body(p_ref, o_ref):
        g = p_ref[0]
        for q in range(1, nd):
            g = g + p_ref[q]
        o_ref[...] = g

    return pl.pallas_call(
        body, out_shape=jax.ShapeDtypeStruct((R, C), F32),
        in_specs=[pl.BlockSpec(memory_space=pltpu.VMEM)],
        out_specs=pl.BlockSpec(memory_space=pltpu.VMEM), name=name)(parts)


def _adamw_small(g, w, m, v, *, name):
    def body(g_ref, w_ref, m_ref, v_ref, d_ref, nm_ref, nv_ref):
        d, nm, nv = _adam_math(g_ref[...], w_ref[...], m_ref[...], v_ref[...])
        d_ref[...] = d
        nm_ref[...] = nm
        nv_ref[...] = nv

    vm = pl.BlockSpec(memory_space=pltpu.VMEM)
    out = jax.ShapeDtypeStruct(g.shape, F32)
    return pl.pallas_call(body, out_shape=[out] * 3, in_specs=[vm] * 4, out_specs=[vm] * 3,
                          name=name)(g, w, m, v)


def kernel(x, norm_mix, norm_mlp, sb_wqkv, sb_wo, sgu_win, sgu_gain, sgu_ws, sgu_bs, sgu_wout, mlp_w1, mlp_w2, final_norm, loss_target, m_norm_mix, m_norm_mlp, m_sb_wqkv, m_sb_wo, m_sgu_win, m_sgu_gain, m_sgu_ws, m_sgu_bs, m_sgu_wout, m_mlp_w1, m_mlp_w2, m_final_norm, v_norm_mix, v_norm_mlp, v_sb_wqkv, v_sb_wo, v_sgu_win, v_sgu_gain, v_sgu_ws, v_sgu_bs, v_sgu_wout, v_mlp_w1, v_mlp_w2, v_final_norm):
    batch, seq, D = x.shape
    T = batch * seq
    x0 = x.reshape(T, D)
    target = loss_target.reshape(T, D)

    WQKV, WO, WIN, WOUT, W1, W2, GAIN = range(7)
    big = [sb_wqkv, sb_wo, sgu_win, sgu_wout, mlp_w1, mlp_w2]
    shards = [w.astype(BF16) for w in big] + [sgu_gain[:, None, :]]
    layer_weights = [[(WQKV, 0), (WO, 0), (W1, 0), (W2, 0)], [(WIN, 0), (WOUT, 0), (W1, 1), (W2, 1)],
                     [(WQKV, 1), (WO, 1), (W1, 2), (W2, 2)], [(WIN, 1), (WOUT, 1), (W1, 3), (W2, 3)]]

    def weight_item(f, l):
        return (lambda ins: ins[f].at[l], lambda outs, p: outs[f].at[l, p])

    first = [weight_item(f, l) for f, l in layer_weights[0]]
    first.append((lambda ins: ins[GAIN], lambda outs, p: outs[GAIN].at[:, p]))
    targets = [jax.ShapeDtypeStruct((s.shape[0], N_DEV) + s.shape[1:], s.dtype) for s in shards]
    gathered0 = _run_comm(_Comm(shards, targets, len(first), _gather_emit(first, [list(range(len(first)))])),
                          name="gather_layer0")
    rest = [weight_item(f, l) for lw in layer_weights[1:] for f, l in lw]
    gather_rest = _Comm(shards[:GAIN], list(gathered0[:GAIN]), len(rest),
                        _gather_emit(rest, [[0, 1, 2, 3], [4, 5, 6, 7], [8, 9, 10, 11]]))
    gain_sgu = gathered0[GAIN].reshape(-1, 1, SGU_FFN)

    def rows_joined(g):
        return g.reshape(g.shape[0], 1, N_DEV * g.shape[2], g.shape[3])

    saved = []
    xs = x0
    for i in range(DEPTH):
        j = i // 2
        h = _rms_fwd(xs, norm_mix[i:i + 1], name=f"norm_mix{i}")
        if i == 0:
            qkv = _mm_nn(h, gathered0[WQKV], j, out_dtype=BF16, name=f"qkv{i}", pb=2)
            o, gathered = _sb_fwd(qkv, batch=batch, seq=seq, name=f"sb_fwd{i}", comm=gather_rest)
            g_wqkv, g_wo, g_win, g_wout, g_w1, g_w2 = gathered
            g_wo, g_wout, g_w2 = rows_joined(g_wo), rows_joined(g_wout), rows_joined(g_w2)
            x_mid = _mm_nn(o, g_wo, j, out_dtype=F32, res=xs, name=f"wo{i}")
            mix = (h, qkv, o)
        elif i % 2 == 0:
            qkv = _mm_nn(h, g_wqkv, j, out_dtype=BF16, name=f"qkv{i}", pb=2)
            o, _ = _sb_fwd(qkv, batch=batch, seq=seq, name=f"sb_fwd{i}")
            x_mid = _mm_nn(o, g_wo, j, out_dtype=F32, res=xs, name=f"wo{i}")
            mix = (h, qkv, o)
        else:
            gain_j = gain_sgu[j]
            bst = sgu_bs[j].T
            uvp = _mm_nn(h, g_win, j, out_dtype=BF16, name=f"win{i}", pb=2)
            yv = _sgu_fwd(uvp, gain_j, sgu_ws[j], bst, name=f"sgu_fwd{i}")
            x_mid = _mm_nn(yv, g_wout, j, out_dtype=F32, res=xs, name=f"wout{i}")
            mix = (h, uvp, yv, gain_j, bst)
        h2 = _rms_fwd(x_mid, norm_mlp[i:i + 1], name=f"norm_mlp{i}")
        a = _mm_nn(h2, g_w1, i, out_dtype=BF16, name=f"w1_{i}", pb=2)
        x_out = _mm_nn(a, g_w2, i, out_dtype=F32, res=x_mid, a_act="relu2", name=f"w2_{i}", tm=512, tn=512)
        saved.append((xs, mix, x_mid, h2, a))
        xs = x_out

    dx, dxb, sq, d_final = _loss_head(xs, final_norm.reshape(1, D), target, name="loss_head")
    loss = lax.psum(0.5 * jnp.sum(sq) / D, ("x", "y", "c"))

    SMALL = GAIN
    stacks = {f: jax.ShapeDtypeStruct((w.shape[0], N_DEV) + w.shape[1:], BF16) for f, w in enumerate(big)}
    pending = []

    def row_shards(p):
        return p.reshape(N_DEV, p.shape[1] // N_DEV, p.shape[2])

    def exchange_pending():
        fams = sorted({f for _, f, _ in pending})
        items = [(lambda ins, p, a=a: ins[a].at[p], lambda outs, q, t=fams.index(f), l=l: outs[t].at[l, q])
                 for a, (_, f, l) in enumerate(pending)]
        comm = _Comm([part for part, _, _ in pending], [stacks[f] for f in fams], len(items), _exchange_emit(items))
        pending.clear()
        return comm, fams

    d_norm_mix, d_norm_mlp = [None] * DEPTH, [None] * DEPTH
    d_gain, d_ws, d_bs = [None] * 2, [None] * 2, [None] * 2
    for i in reversed(range(DEPTH)):
        j = i // 2
        xs, mix, x_mid, h2, a = saved[i]
        da = _mm_nt(dxb, g_w2, i, out_dtype=BF16, act_src=a, name=f"d_a{i}", tn=1024)
        pending.append((row_shards(_mm_tn(a, dxb, shards=1, a_act="relu2", name=f"d_w2_{i}")), W2, i))
        pending.append((_mm_tn(h2, da, shards=N_DEV, name=f"d_w1_{i}", pb=2), W1, i))
        dx, dxb, d_norm_mlp[i] = _mm_nt_rms_bwd(da, g_w1, i, x_mid, norm_mlp[i:i + 1], dx, name=f"d_h2_{i}")
        if i % 2 == 0:
            h, qkv, o = mix
            do = _mm_nt(dxb, g_wo, j, out_dtype=BF16, name=f"d_o{i}", tn=1024)
            pending.append((row_shards(_mm_tn(o, dxb, shards=1, name=f"d_wo{i}")), WO, j))
            comm, fams = exchange_pending()
            (dq, dk, dv), results = _sb_bwd(qkv, o, do, batch=batch, seq=seq, name=f"sb_bwd{i}", comm=comm)
            stacks.update(zip(fams, results))
            dqkv = jnp.concatenate([dq, dk, dv], axis=1)
            pending.append((_mm_tn(h, dqkv, shards=N_DEV, name=f"d_wqkv{i}", pb=2), WQKV, j))
            mixer_in, w_in = dqkv, g_wqkv
        else:
            h, uvp, yv, gain_j, bst = mix
            dy = _mm_nt(dxb, g_wout, j, out_dtype=BF16, name=f"d_y{i}", tn=1024)
            pending.append((row_shards(_mm_tn(yv, dxb, shards=1, name=f"d_wout{i}")), WOUT, j))
            duv, d_gain[j], d_ws[j], dbst = _sgu_bwd(uvp, dy, gain_j, sgu_ws[j], bst, name=f"sgu_bwd{i}")
            d_bs[j] = dbst.T
            pending.append((_mm_tn(h, duv, shards=N_DEV, name=f"d_win{i}", pb=2), WIN, j))
            mixer_in, w_in = duv, g_win
        dx, dxb, d_norm_mix[i] = _mm_nt_rms_bwd(mixer_in, w_in, j, xs, norm_mix[i:i + 1], dx, name=f"d_h_mix{i}")
    grad_x = dx.reshape(batch, seq, D)

    small = [jnp.concatenate(d_norm_mix, 0), jnp.concatenate(d_norm_mlp, 0), d_final,
             jnp.concatenate(d_gain, 0), jnp.stack(d_bs, 0), jnp.stack(d_ws, 0)]
    small_flat = jnp.concatenate([s.reshape(-1) for s in small])
    n_small = small_flat.shape[0]
    small_rows = -(-n_small // (N_DEV * SUBLANES * LANES)) * SUBLANES
    small_flat = jnp.pad(small_flat, (0, N_DEV * small_rows * LANES - n_small))
    stacks[SMALL] = jax.ShapeDtypeStruct((1, N_DEV, small_rows, LANES), F32)
    pending.append((small_flat.reshape(N_DEV, small_rows, LANES), SMALL, 0))
    comm, fams = exchange_pending()
    stacks.update(zip(fams, _run_comm(comm, name="exchange_last")))
    r_wqkv, r_wo, r_win, r_wout, r_w1, r_w2, r_small = [stacks[f] for f in range(SMALL + 1)]

    u_wqkv = _sum_adamw(r_wqkv, sb_wqkv, m_sb_wqkv, v_sb_wqkv, name="adamw_wqkv")
    u_wo = _sum_adamw(r_wo, sb_wo, m_sb_wo, v_sb_wo, name="adamw_wo")
    u_win = _sum_adamw(r_win, sgu_win, m_sgu_win, v_sgu_win, name="adamw_win")
    u_wout = _sum_adamw(r_wout, sgu_wout, m_sgu_wout, v_sgu_wout, name="adamw_wout")
    u_w1 = _sum_adamw(r_w1, mlp_w1, m_mlp_w1, v_mlp_w1, name="adamw_w1")
    u_w2 = _sum_adamw(r_w2, mlp_w2, m_mlp_w2, v_mlp_w2, name="adamw_w2")

    small_sum = _sum_parts(r_small[0], name="sum_small")
    g_small = _all_gather([small_sum[None]], name="gather_small")[0].reshape(-1)[:n_small]

    shapes = [s.shape for s in small]
    sizes = [s.size for s in small]
    offs = [sum(sizes[:k]) for k in range(len(sizes))]
    me = 4 * lax.axis_index("x") + 2 * lax.axis_index("y") + lax.axis_index("c")
    shard_w = SGU_FFN // N_DEV

    def pack(arrs):
        flat = jnp.concatenate([a_.reshape(-1) for a_ in arrs])
        return jnp.pad(flat, (0, N_DEV * small_rows * LANES - n_small)).reshape(-1, LANES)

    def full_gain(gshard):
        return lax.dynamic_update_slice(jnp.zeros((2, SGU_FFN), F32), gshard, (0, me * shard_w))

    w_small = pack([norm_mix, norm_mlp, final_norm, full_gain(sgu_gain), sgu_bs, sgu_ws])
    m_small = pack([m_norm_mix, m_norm_mlp, m_final_norm, full_gain(m_sgu_gain), m_sgu_bs, m_sgu_ws])
    v_small = pack([v_norm_mix, v_norm_mlp, v_final_norm, full_gain(v_sgu_gain), v_sgu_bs, v_sgu_ws])
    g_pack = jnp.pad(g_small, (0, N_DEV * small_rows * LANES - n_small)).reshape(-1, LANES)
    sm = [g_pack] + list(_adamw_small(g_pack, w_small, m_small, v_small, name="adamw_small"))

    def unpack(flat2d):
        flat = flat2d.reshape(-1)
        out = [flat[offs[k]:offs[k] + sizes[k]].reshape(shapes[k]) for k in range(len(sizes))]
        out[2] = out[2].reshape(D)
        out[3] = lax.dynamic_slice(out[3], (0, me * shard_w), (2, shard_w))
        return out

    outs = []
    for k, big_u in enumerate(zip(u_wqkv, u_wo, u_win, u_wout, u_w1, u_w2)):
        s_nm, s_nl, s_fn, s_gain, s_bs, s_ws = unpack(sm[k])
        b_wqkv, b_wo, b_win, b_wout, b_w1, b_w2 = big_u
        outs += [s_nm, s_nl, b_wqkv, b_wo, b_win, s_gain, s_ws, s_bs, b_wout, b_w1, b_w2, s_fn]
    return (loss, grad_x, *outs)
```

```python
import functools

import jax
import jax.numpy as jnp
from jax import lax
from jax.experimental import pallas as pl
from jax.experimental.pallas import tpu as pltpu

F32 = jnp.float32
BF16 = jnp.bfloat16

N_DEV = 8
D_MODEL = 1024
SEQ = 2048
DEPTH = 4
SB_HEAD_DIM = 64
SGU_CHUNK = 128
SGU_FFN = 2 * D_MODEL
SGU_GROUPS = 8
SGU_GROUP_W = SGU_FFN // SGU_GROUPS
EPS = 1e-6

ADAM_LR = 0.001
ADAM_B1 = 0.9
ADAM_B2 = 0.999
ADAM_EPS = 1e-08
ADAM_WD = 0.01
ADAM_STEP = 10

LANES = 128
SUBLANES = 8
VMEM_LIMIT = 48 * 1024 * 1024
EXP_ZERO_BELOW = -104.0
SB_TILE = 256
SB_STRAIGHT = 2
SB_PAIRS = 2

MESH = pl.DeviceIdType.MESH
ANY = pl.BlockSpec(memory_space=pl.ANY)


def _params(*sem):
    return pltpu.CompilerParams(dimension_semantics=sem, vmem_limit_bytes=VMEM_LIMIT)


def _dot(a, b):
    return jnp.dot(a, b, preferred_element_type=F32)


def _dot_nt(a, b):
    return lax.dot_general(a, b, (((1,), (1,)), ((), ())), preferred_element_type=F32)


def _dot_tn(a, b):
    return lax.dot_general(a, b, (((0,), (0,)), ((), ())), preferred_element_type=F32)


def _split_bf16(x):
    hi = x.astype(BF16)
    lo = (x - hi.astype(F32)).astype(BF16)
    return hi, lo


def _relu2(av):
    t = jnp.maximum(av, jnp.zeros_like(av))
    return t * t


def _mm_nn(a, w, l, *, out_dtype, name, res=None, a_act=None, tm=1024, tn=None, pb=1):
    M, K = a.shape
    _, P, K2, n = w.shape
    assert K2 == K
    tn = n if tn is None else tn
    assert n % tn == 0 and (tn == n or pb == 1) and P % pb == 0
    tm = min(tm, M)
    assert M % tm == 0
    jn, width = n // tn, pb * tn

    def body(*refs):
        a_ref, w_ref, o_ref = refs[0], refs[1], refs[-1]
        av = a_ref[...]
        if a_act == "relu2":
            av = _relu2(av)
        av = av.astype(BF16)
        for p in range(pb):
            sl = slice(p * tn, (p + 1) * tn)
            r = _dot(av, w_ref[0, p])
            if res is not None:
                r = r + refs[2][:, sl]
            o_ref[:, sl] = r.astype(out_dtype)

    in_specs = [pl.BlockSpec((tm, K), lambda i, j: (i, 0)),
                pl.BlockSpec((1, pb, K, tn), lambda i, j: (l, j // jn, 0, j % jn))]
    args = [a, w]
    if res is not None:
        in_specs.append(pl.BlockSpec((tm, width), lambda i, j: (i, j)))
        args.append(res)
    return pl.pallas_call(
        body, grid=(M // tm, P * n // width), in_specs=in_specs,
        out_specs=pl.BlockSpec((tm, width), lambda i, j: (i, j)),
        out_shape=jax.ShapeDtypeStruct((M, P * n), out_dtype),
        compiler_params=_params("parallel", "parallel"), name=name)(*args)


def _mm_nt(a, w, l, *, out_dtype, name, act_src=None, tm=1024, tn=512):
    M, K = a.shape
    _, P, Nout, kc = w.shape
    assert K == P * kc
    tm, tn = min(tm, M), min(tn, Nout)
    assert M % tm == 0 and Nout % tn == 0

    def body(*refs):
        a_ref, w_ref, o_ref = refs[0], refs[1], refs[-1]
        r = None
        for p in range(P):
            d = _dot_nt(a_ref[:, p * kc:(p + 1) * kc].astype(BF16), w_ref[0, p])
            r = d if r is None else r + d
        if act_src is not None:
            r = r * (2.0 * jnp.maximum(refs[2][...].astype(F32), 0.0))
        o_ref[...] = r.astype(out_dtype)

    in_specs = [pl.BlockSpec((tm, K), lambda i, j: (i, 0)),
                pl.BlockSpec((1, P, tn, kc), lambda i, j: (l, 0, j, 0))]
    args = [a, w]
    if act_src is not None:
        in_specs.append(pl.BlockSpec((tm, tn), lambda i, j: (i, j)))
        args.append(act_src)
    return pl.pallas_call(
        body, grid=(M // tm, Nout // tn), in_specs=in_specs,
        out_specs=pl.BlockSpec((tm, tn), lambda i, j: (i, j)),
        out_shape=jax.ShapeDtypeStruct((M, Nout), out_dtype),
        compiler_params=_params("parallel", "parallel"), name=name)(*args)


def _mm_tn(a, b, *, shards, name, a_act=None, tm=2048, tk=1024, pb=1):
    M, K = a.shape
    M2, N = b.shape
    assert M2 == M
    n = N // shards
    tm, tk = min(tm, M), min(tk, K)
    assert M % tm == 0 and K % tk == 0 and shards % pb == 0
    width, nm = pb * n, M // tm

    def body(a_ref, b_ref, o_ref, acc):
        m = pl.program_id(2)
        av = a_ref[...]
        if a_act == "relu2":
            av = _relu2(av)

        @pl.when(m == 0)
        def _():
            acc[...] = jnp.zeros_like(acc)

        acc[...] += _dot_tn(av.astype(BF16), b_ref[...].astype(BF16))

        @pl.when(m == nm - 1)
        def _():
            for p in range(pb):
                o_ref[p] = acc[:, p * n:(p + 1) * n].astype(BF16)

    return pl.pallas_call(
        body, grid=(K // tk, N // width, nm),
        in_specs=[pl.BlockSpec((tm, tk), lambda i, j, m: (m, i)),
                  pl.BlockSpec((tm, width), lambda i, j, m: (m, j))],
        out_specs=pl.BlockSpec((pb, tk, n), lambda i, j, m: (j, i, 0)),
        out_shape=jax.ShapeDtypeStruct((shards, K, n), BF16),
        scratch_shapes=[pltpu.VMEM((tk, width), F32)],
        compiler_params=_params("parallel", "parallel", "arbitrary"), name=name)(a, b)


def _rms_fwd(x, gain, *, name, tr=512):
    T, D = x.shape

    def body(x_ref, g_ref, o_ref):
        xv = x_ref[...]
        r = lax.rsqrt(jnp.mean(xv * xv, axis=-1, keepdims=True) + EPS)
        o_ref[...] = (xv * r * g_ref[...]).astype(BF16)

    return pl.pallas_call(
        body, grid=(T // tr,),
        in_specs=[pl.BlockSpec((tr, D), lambda i: (i, 0)), pl.BlockSpec((1, D), lambda i: (0, 0))],
        out_specs=pl.BlockSpec((tr, D), lambda i: (i, 0)),
        out_shape=jax.ShapeDtypeStruct((T, D), BF16),
        compiler_params=_params("parallel"), name=name)(x, gain)


def _mm_nt_rms_bwd(a, w, l, x, gain, dres, *, name, comm=None, tm=512):
    M, K = a.shape
    _, P, D, kc = w.shape
    assert K == P * kc and x.shape == (M, D)
    tm = min(tm, M)
    nr = M // tm

    def body(*refs):
        (a_ref, w_ref, x_ref, g_ref, dres_ref), (dx_ref, dxb_ref, dg_ref), (acc,) = carried.split(refs, 5, 3, 1)
        i = pl.program_id(0)
        carried.emit(refs)
        dhv = None
        for p in range(P):
            d = _dot_nt(a_ref[:, p * kc:(p + 1) * kc], w_ref[0, p])
            dhv = d if dhv is None else dhv + d
        xv = x_ref[...]
        r = lax.rsqrt(jnp.mean(xv * xv, axis=-1, keepdims=True) + EPS)
        xhat = xv * r
        dxhat = dhv * g_ref[...]
        dx = dres_ref[...] + r * (dxhat - xhat * jnp.mean(dxhat * xhat, axis=-1, keepdims=True))
        dx_ref[...] = dx
        dxb_ref[...] = dx.astype(BF16)
        part = jnp.sum((dhv * xhat).reshape(tm // SUBLANES, SUBLANES, D), axis=0)

        @pl.when(i == 0)
        def _():
            acc[...] = jnp.zeros_like(acc)

        acc[...] += part

        @pl.when(i == nr - 1)
        def _():
            dg_ref[...] = jnp.sum(acc[...], axis=0, keepdims=True)

    row = pl.BlockSpec((tm, D), lambda i: (i, 0))
    vec = pl.BlockSpec((1, D), lambda i: (0, 0))
    carried = _Carried(comm, (nr,))
    return carried.pallas_call(
        body,
        in_specs=[pl.BlockSpec((tm, K), lambda i: (i, 0)),
                  pl.BlockSpec((1, P, D, kc), lambda i: (l, 0, 0, 0), pipeline_mode=pl.Buffered(1)),
                  row, vec, row],
        out_specs=[row, row, vec],
        out_shape=[jax.ShapeDtypeStruct((M, D), F32), jax.ShapeDtypeStruct((M, D), BF16),
                   jax.ShapeDtypeStruct((1, D), F32)],
        scratch_shapes=[pltpu.VMEM((SUBLANES, D), F32)],
        args=[a, w, x, gain, dres], name=name)


def _loss_head(x, gain, target, *, name, tr=512):
    T, D = x.shape
    nr = T // tr

    def body(x_ref, g_ref, t_ref, dx_ref, dxb_ref, sq_ref, dg_ref, sq_acc, dg_acc):
        i = pl.program_id(0)
        xv = x_ref[...]
        g = g_ref[...]
        r = lax.rsqrt(jnp.mean(xv * xv, axis=-1, keepdims=True) + EPS)
        xhat = xv * r
        err = xhat * g - t_ref[...]
        dy = err * (1.0 / D)
        dxhat = dy * g
        dx = r * (dxhat - xhat * jnp.mean(dxhat * xhat, axis=-1, keepdims=True))
        dx_ref[...] = dx
        dxb_ref[...] = dx.astype(BF16)
        sq = jnp.sum((err * err).reshape(tr // SUBLANES, SUBLANES, D), axis=0)
        dg = jnp.sum((dy * xhat).reshape(tr // SUBLANES, SUBLANES, D), axis=0)

        @pl.when(i == 0)
        def _():
            sq_acc[...] = sq
            dg_acc[...] = dg

        @pl.when(i > 0)
        def _():
            sq_acc[...] += sq
            dg_acc[...] += dg

        @pl.when(i == nr - 1)
        def _():
            sq_ref[...] = sq_acc[...]
            dg_ref[...] = jnp.sum(dg_acc[...], axis=0, keepdims=True)

    row = pl.BlockSpec((tr, D), lambda i: (i, 0))
    vec = pl.BlockSpec((1, D), lambda i: (0, 0))
    part = pl.BlockSpec((SUBLANES, D), lambda i: (0, 0))
    return pl.pallas_call(
        body, grid=(nr,), in_specs=[row, vec, row], out_specs=[row, row, part, vec],
        out_shape=[jax.ShapeDtypeStruct((T, D), F32), jax.ShapeDtypeStruct((T, D), BF16),
                   jax.ShapeDtypeStruct((SUBLANES, D), F32), jax.ShapeDtypeStruct((1, D), F32)],
        scratch_shapes=[pltpu.VMEM((SUBLANES, D), F32), pltpu.VMEM((SUBLANES, D), F32)],
        compiler_params=_params("arbitrary"), name=name)(x, gain, target)


def _head0_lanes():
    return lax.broadcasted_iota(jnp.int32, (1, LANES), 1) < SB_HEAD_DIM


def _stack_heads(x):
    zero = jnp.zeros_like(x)
    h0 = _head0_lanes()
    return jnp.concatenate([jnp.where(h0, x, zero), jnp.where(h0, zero, x)], axis=0)


def _unstack_heads(y, tq):
    return jnp.where(_head0_lanes(), y[:tq], y[tq:])


def _past_mask(tq, tk):
    row = lax.broadcasted_iota(jnp.int32, (2 * tq, tk), 0) & (tq - 1)
    col = lax.broadcasted_iota(jnp.int32, (2 * tq, tk), 1)
    return col < row


def _max_of(arrays):
    return functools.reduce(jnp.maximum, [jnp.max(a) for a in arrays])


def _sb_logs(z, past):
    minus_abs = lax.bitcast_convert_type(
        lax.bitcast_convert_type(z, jnp.uint32) | jnp.uint32(0x80000000), F32)
    log_beta = jnp.minimum(z, 0.0) - jnp.log(1.0 + jnp.exp(minus_abs))
    l = log_beta - z
    if past is not None:
        l = jnp.where(past, l, 0.0)
    return log_beta, l


def _suffix_matrix(tk):
    j = lax.broadcasted_iota(jnp.int32, (2 * tk, tk), 0) & (tk - 1)
    s = lax.broadcasted_iota(jnp.int32, (2 * tk, tk), 1)
    return (j > s).astype(BF16)


def _suffix_sum(x, u2, exact=True):
    if not exact:
        return _dot(x.astype(BF16), u2[:x.shape[1]])
    hi, lo = _split_bf16(x)
    return _dot(jnp.concatenate([hi, lo], axis=1), u2)


class _Carried:
    def __init__(self, comm, grid):
        self.comm, self.grid = comm, grid
        self.n_op = len(comm.operands) if comm else 0
        self.n_tgt = len(comm.targets) if comm else 0

    def split(self, refs, n_in, n_out, n_scratch):
        self.n_in, self.n_out, self.n_scratch = n_in, n_out, n_scratch
        a = n_in + self.n_op
        b = a + n_out + self.n_tgt
        return refs[:n_in], refs[a:a + n_out], refs[b:b + n_scratch]

    def emit(self, refs):
        if self.comm is None:
            return
        step, n_steps = 0, 1
        for d, size in enumerate(self.grid):
            step = step * size + pl.program_id(d)
            n_steps *= size
        a = self.n_in + self.n_op + self.n_out
        self.comm.emit(step, n_steps, refs[self.n_in:self.n_in + len(self.comm.sources)],
                       refs[a:a + self.n_tgt], refs[a + self.n_tgt + self.n_scratch:])

    def pallas_call(self, body, *, in_specs, out_specs, out_shape, scratch_shapes, args, name):
        comm = self.comm
        n_in, n_out = len(in_specs), len(out_specs)
        aliases = {}
        if comm is not None:
            in_specs = in_specs + [ANY] * self.n_op
            out_specs = out_specs + [ANY] * self.n_tgt
            out_shape = out_shape + comm.out_shapes
            scratch_shapes = scratch_shapes + comm.sems
            args = args + comm.operands
            aliases = comm.aliases(n_in, n_out)
        sem = ("parallel",) * (len(self.grid) - 1) + ("arbitrary",) if comm is None else ("arbitrary",) * len(self.grid)
        results = pl.pallas_call(
            body, grid=self.grid, in_specs=in_specs, out_specs=out_specs, out_shape=out_shape,
            scratch_shapes=scratch_shapes, input_output_aliases=aliases,
            compiler_params=_params(*sem), name=name)(*args)
        return results[:n_out], results[n_out:]


def _sb_fwd(qkv, *, batch, seq, name, comm=None, tq=SB_TILE, n_pre=SB_STRAIGHT, pairs=SB_PAIRS):
    T, D3 = qkv.shape
    D = D3 // 3
    nhp = D // LANES
    tk = tq
    nq = seq // tq
    scale = SB_HEAD_DIM ** -0.5
    assert 1 <= n_pre <= nq and nhp % pairs == 0

    def body(*refs):
        (q_ref, k_ref, v_ref), (o_ref,), (acc,) = carried.split(refs, 3, 1, 1)
        qi = pl.program_id(2)
        carried.emit(refs)
        qs = [_stack_heads(q_ref[:, lanes]) * scale for lanes in cols]
        past = _past_mask(tq, tk)
        u = _suffix_matrix(tk)

        def block(kb, g, c, diag):
            ks = pl.multiple_of(kb * tk, tk)
            z = _dot_nt(qs[g], k_ref[pl.ds(ks, tk), cols[g]])
            log_beta, l = _sb_logs(z, past if diag else None)
            arg = log_beta + _suffix_sum(l, u, exact=False)
            a = jnp.exp(arg if c is None else arg + c)
            if diag:
                a = jnp.where(past, a, 0.0)
            return _dot(a.astype(BF16), v_ref[pl.ds(ks, tk), cols[g]]), jnp.sum(l, axis=1, keepdims=True)

        def straight(n):
            o_sum, c = [None] * pairs, [None] * pairs
            for b in range(n):
                for g in range(pairs):
                    o_b, c_b = block(qi - b, g, c[g], b == 0)
                    o_sum[g] = o_b if b == 0 else o_sum[g] + o_b
                    c[g] = c_b if b == 0 else c[g] + c_b
            return o_sum, c

        def finish(o_sum):
            for g in range(pairs):
                o_ref[:, cols[g]] = _unstack_heads(o_sum[g], tq)

        for n in range(1, n_pre):
            @pl.when(qi == n - 1)
            def _(n=n):
                finish(straight(n)[0])

        @pl.when(qi >= n_pre - 1)
        def _():
            o_sum, c = straight(n_pre)
            for g in range(pairs):
                acc[g] = o_sum[g]

            def cond(st):
                kb, c = st
                return jnp.logical_and(kb >= 0, _max_of(c) > EXP_ZERO_BELOW)

            def step(st):
                kb, c = st
                new_c = []
                for g in range(pairs):
                    o_n, c_n = block(kb, g, c[g], False)
                    acc[g] += o_n
                    new_c.append(c[g] + c_n)
                return kb - 1, tuple(new_c)

            lax.while_loop(cond, step, (qi - n_pre, tuple(c)))
            finish([acc[g] for g in range(pairs)])

    cols = [slice(g * LANES, (g + 1) * LANES) for g in range(pairs)]
    width = pairs * LANES
    carried = _Carried(comm, (batch, nhp // pairs, nq))
    (o,), comm_results = carried.pallas_call(
        body,
        in_specs=[pl.BlockSpec((tq, width), lambda b, p, i: (b * nq + i, p)),
                  pl.BlockSpec((seq, width), lambda b, p, i: (b, nhp // pairs + p)),
                  pl.BlockSpec((seq, width), lambda b, p, i: (b, 2 * (nhp // pairs) + p))],
        out_specs=[pl.BlockSpec((tq, width), lambda b, p, i: (b * nq + i, p))],
        out_shape=[jax.ShapeDtypeStruct((T, D), F32)],
        scratch_shapes=[pltpu.VMEM((pairs, 2 * tq, LANES), F32)],
        args=[qkv, qkv, qkv], name=name)
    return o, comm_results


def _sb_bwd(qkv, o, do, *, batch, seq, name, comm=None, tq=SB_TILE, n_pre=SB_STRAIGHT, pairs=SB_PAIRS):
    T, D3 = qkv.shape
    D = D3 // 3
    nhp = D // LANES
    tk = tq
    nq = seq // tq
    scale = SB_HEAD_DIM ** -0.5

    def body(*refs):
        ins, outs, scratch = carried.split(refs, 5, 3, 3)
        q_ref, k_ref, v_ref, o_ref, do_ref = ins
        dq_ref, dk_ref, dv_ref = outs
        dq_acc, dk_acc, dv_acc = scratch
        qi = pl.program_id(2)
        carried.emit(refs)

        @pl.when(qi == 0)
        def _():
            dk_acc[...] = jnp.zeros_like(dk_acc)
            dv_acc[...] = jnp.zeros_like(dv_acc)

        qs = [_stack_heads(q_ref[:, lanes]) * scale for lanes in cols]
        dos = [_stack_heads(do_ref[:, lanes]) for lanes in cols]
        dsum = [jnp.sum(_stack_heads(do_ref[:, lanes].astype(F32) * o_ref[:, lanes]), axis=1, keepdims=True)
                for lanes in cols]
        past = _past_mask(tq, tk)
        u = _suffix_matrix(tk)

        def block(kb, p, c, gc, diag):
            ks = pl.multiple_of(kb * tk, tk)
            kblk = k_ref[pl.ds(ks, tk), cols[p]]
            vblk = v_ref[pl.ds(ks, tk), cols[p]]
            z = _dot_nt(qs[p], kblk)
            log_beta, l = _sb_logs(z, past if diag else None)
            arg = log_beta + _suffix_sum(l, u, exact=False)
            a = jnp.exp(arg if c is None else arg + c)
            if diag:
                a = jnp.where(past, a, 0.0)
            a = a.astype(BF16)
            beta = 1.0 - jnp.exp(l)
            g = a.astype(F32) * _dot_nt(dos[p], vblk)
            gs = _suffix_sum(g, u)
            dz = g - beta * (dsum[p] - (gs if gc is None else gs + gc))
            if diag:
                dz = jnp.where(past, dz, 0.0)
            dzb = dz.astype(BF16)
            dk_acc[pl.ds(ks, tk), cols[p]] += _dot_tn(dzb, qs[p])
            dv_acc[pl.ds(ks, tk), cols[p]] += _dot_tn(a, dos[p])
            return (_dot(dzb, kblk), jnp.sum(l, axis=1, keepdims=True),
                    jnp.sum(g, axis=1, keepdims=True))

        def finish(dq_sum):
            for p in range(pairs):
                dq_ref[:, cols[p]] = (_unstack_heads(dq_sum[p], tq) * scale).astype(BF16)

        def straight(n):
            dq_sum, c, gc = [None] * pairs, [None] * pairs, [None] * pairs
            for b in range(n):
                for p in range(pairs):
                    dq_b, c_b, g_b = block(qi - b, p, c[p], gc[p], b == 0)
                    dq_sum[p] = dq_b if b == 0 else dq_sum[p] + dq_b
                    c[p] = c_b if b == 0 else c[p] + c_b
                    gc[p] = g_b if b == 0 else gc[p] + g_b
            return dq_sum, c, gc

        for n in range(1, n_pre):
            @pl.when(qi == n - 1)
            def _(n=n):
                finish(straight(n)[0])

        @pl.when(qi >= n_pre - 1)
        def _():
            dq_sum, c, gc = straight(n_pre)
            for p in range(pairs):
                dq_acc[p] = dq_sum[p]

            def cond(st):
                kb, c, gc = st
                return jnp.logical_and(kb >= 0, _max_of(c) > EXP_ZERO_BELOW)

            def step(st):
                kb, c, gc = st
                new_c, new_gc = [], []
                for p in range(pairs):
                    dq_n, c_n, g_n = block(kb, p, c[p], gc[p], False)
                    dq_acc[p] += dq_n
                    new_c.append(c[p] + c_n)
                    new_gc.append(gc[p] + g_n)
                return kb - 1, tuple(new_c), tuple(new_gc)

            lax.while_loop(cond, step, (qi - n_pre, tuple(c), tuple(gc)))
            finish([dq_acc[p] for p in range(pairs)])

        @pl.when(qi == nq - 1)
        def _():
            dk_ref[...] = dk_acc[...].astype(BF16)
            dv_ref[...] = dv_acc[...].astype(BF16)

    cols = [slice(p * LANES, (p + 1) * LANES) for p in range(pairs)]
    width, ncb = pairs * LANES, nhp // pairs
    qspec = pl.BlockSpec((tq, width), lambda b, p, i: (b * nq + i, p))
    sspec = pl.BlockSpec((seq, width), lambda b, p, i: (b, p))
    out = jax.ShapeDtypeStruct((T, D), BF16)
    carried = _Carried(comm, (batch, ncb, nq))
    return carried.pallas_call(
        body,
        in_specs=[qspec,
                  pl.BlockSpec((seq, width), lambda b, p, i: (b, ncb + p)),
                  pl.BlockSpec((seq, width), lambda b, p, i: (b, 2 * ncb + p)),
                  qspec, qspec],
        out_specs=[qspec, sspec, sspec], out_shape=[out, out, out],
        scratch_shapes=[pltpu.VMEM((pairs, 2 * tq, LANES), F32), pltpu.VMEM((seq, width), F32),
                        pltpu.VMEM((seq, width), F32)],
        args=[qkv, qkv, qkv, o, do], name=name)


_GELU_C = 0.7978845608028654
_GELU_A = 0.044715


def _gelu(x, with_grad=False):
    xx = x * x
    t = jnp.tanh(x * (_GELU_C + (_GELU_C * _GELU_A) * xx))
    hx = 0.5 * x
    y = hx + hx * t
    if not with_grad:
        return y
    grad = (0.5 + 0.5 * t) + (hx * (1.0 - t * t)) * (_GELU_C + (3.0 * _GELU_C * _GELU_A) * xx)
    return y, grad


def _causal_ws(ws_ref, g):
    t = lax.broadcasted_iota(jnp.int32, (SGU_CHUNK, SGU_CHUNK), 0)
    s = lax.broadcasted_iota(jnp.int32, (SGU_CHUNK, SGU_CHUNK), 1)
    return jnp.where(s <= t, ws_ref[g], 0.0)


def _sgu_fwd(uvp, gain, ws, bst, *, name):
    T, F2 = uvp.shape
    F = F2 // 2
    C, G, W = SGU_CHUNK, SGU_GROUPS, SGU_GROUP_W

    def body(uv_ref, g_ref, ws_ref, bs_ref, y_ref):
        uvf = uv_ref[...].astype(F32)
        uv = _gelu(uvf)
        u, v = uv[:, :F], uv[:, F:]
        r = lax.rsqrt(jnp.mean(v * v, axis=-1, keepdims=True) + EPS)
        vn = (v * r * g_ref[...]).astype(BF16)
        for g in range(G):
            sl = slice(g * W, (g + 1) * W)
            mixed = _dot(_causal_ws(ws_ref, g).astype(BF16), vn[:, sl]) + bs_ref[:, g:g + 1]
            y_ref[:, sl] = (u[:, sl] * mixed).astype(BF16)

    return pl.pallas_call(
        body, grid=(T // C,),
        in_specs=[pl.BlockSpec((C, F2), lambda i: (i, 0)), pl.BlockSpec((1, F), lambda i: (0, 0)),
                  pl.BlockSpec((G, C, C), lambda i: (0, 0, 0)), pl.BlockSpec((C, G), lambda i: (0, 0))],
        out_specs=pl.BlockSpec((C, F), lambda i: (i, 0)),
        out_shape=jax.ShapeDtypeStruct((T, F), BF16),
        compiler_params=_params("parallel"), name=name)(uvp, gain, ws, bst)


def _sgu_bwd(uvp, dy, gain, ws, bst, *, name):
    T, F2 = uvp.shape
    F = F2 // 2
    C, G, W = SGU_CHUNK, SGU_GROUPS, SGU_GROUP_W
    nc = T // C

    def body(uv_ref, dy_ref, g_ref, ws_ref, bs_ref, duv_ref, dg_ref, dws_ref, dbs_ref,
             dg_acc, dws_acc, dbs_acc):
        i = pl.program_id(0)

        @pl.when(i == 0)
        def _():
            dg_acc[...] = jnp.zeros_like(dg_acc)
            dws_acc[...] = jnp.zeros_like(dws_acc)
            dbs_acc[...] = jnp.zeros_like(dbs_acc)

        uvf = uv_ref[...].astype(F32)
        uv, dgelu = _gelu(uvf, with_grad=True)
        u, v = uv[:, :F], uv[:, F:]
        r = lax.rsqrt(jnp.mean(v * v, axis=-1, keepdims=True) + EPS)
        vhat = v * r
        gain_v = g_ref[...]
        vn = (vhat * gain_v).astype(BF16)
        dyv = dy_ref[...].astype(F32)
        lane8 = lax.broadcasted_iota(jnp.int32, (1, G), 1)
        dvn_parts = []
        dbs_new = jnp.zeros((C, G), F32)
        for g in range(G):
            sl = slice(g * W, (g + 1) * W)
            wsg = _causal_ws(ws_ref, g)
            mixed = _dot(wsg.astype(BF16), vn[:, sl]) + bs_ref[:, g:g + 1]
            duv_ref[:, sl] = (dyv[:, sl] * mixed * dgelu[:, sl]).astype(BF16)
            dmix = dyv[:, sl] * u[:, sl]
            dbs_new = dbs_new + jnp.where(lane8 == g, jnp.sum(dmix, axis=1, keepdims=True), 0.0)
            dmix_b = dmix.astype(BF16)
            dws_acc[g] += _dot_nt(dmix_b, vn[:, sl])
            dvn_parts.append(_dot(wsg.T.astype(BF16), dmix_b))
        dbs_acc[...] += dbs_new
        dvn = jnp.concatenate(dvn_parts, axis=1)
        dg_acc[...] += jnp.sum((dvn * vhat).reshape(C // SUBLANES, SUBLANES, F), axis=0)
        dvhat = dvn * gain_v
        dv = r * (dvhat - vhat * jnp.mean(dvhat * vhat, axis=-1, keepdims=True))
        duv_ref[:, F:] = (dv * dgelu[:, F:]).astype(BF16)

        @pl.when(i == nc - 1)
        def _():
            dg_ref[...] = jnp.sum(dg_acc[...], axis=0, keepdims=True)
            t = lax.broadcasted_iota(jnp.int32, (G, C, C), 1)
            s = lax.broadcasted_iota(jnp.int32, (G, C, C), 2)
            dws_ref[...] = jnp.where(s <= t, dws_acc[...], 0.0)
            dbs_ref[...] = dbs_acc[...]

    return pl.pallas_call(
        body, grid=(nc,),
        in_specs=[pl.BlockSpec((C, F2), lambda i: (i, 0)), pl.BlockSpec((C, F), lambda i: (i, 0)),
                  pl.BlockSpec((1, F), lambda i: (0, 0)), pl.BlockSpec((G, C, C), lambda i: (0, 0, 0)),
                  pl.BlockSpec((C, G), lambda i: (0, 0))],
        out_specs=[pl.BlockSpec((C, F2), lambda i: (i, 0)), pl.BlockSpec((1, F), lambda i: (0, 0)),
                   pl.BlockSpec((G, C, C), lambda i: (0, 0, 0)), pl.BlockSpec((C, G), lambda i: (0, 0))],
        out_shape=[jax.ShapeDtypeStruct((T, F2), BF16), jax.ShapeDtypeStruct((1, F), F32),
                   jax.ShapeDtypeStruct((G, C, C), F32), jax.ShapeDtypeStruct((C, G), F32)],
        scratch_shapes=[pltpu.VMEM((SUBLANES, F), F32), pltpu.VMEM((G, C, C), F32), pltpu.VMEM((C, G), F32)],
        compiler_params=_params("arbitrary"), name=name)(uvp, dy, gain, ws, bst)


def _my_place():
    return lax.axis_index("x"), lax.axis_index("y"), lax.axis_index("c")


def _all_gather(shards, *, name):
    nf = len(shards)
    items = [(lambda ins, f=f: ins[f], lambda outs, p, f=f: outs[f].at[:, p]) for f in range(nf)]
    targets = [jax.ShapeDtypeStruct((s.shape[0], N_DEV) + s.shape[1:], s.dtype) for s in shards]
    return _run_comm(_Comm(shards, targets, len(items), _gather_emit(items, [list(range(nf))])), name=name)


class _Comm:
    def __init__(self, sources, targets, n_items, emit):
        self.sources, self.targets, self.n_items, self.emit = list(sources), list(targets), n_items, emit
        self.filled = [t for t in self.targets if not isinstance(t, jax.ShapeDtypeStruct)]
        self.operands = self.sources + self.filled
        self.out_shapes = [jax.ShapeDtypeStruct(t.shape, t.dtype) for t in self.targets]
        self.sems = [pltpu.SemaphoreType.DMA((n_items, 7)), pltpu.SemaphoreType.DMA((n_items, 7)),
                     pltpu.SemaphoreType.DMA((n_items,))]

    def aliases(self, first_operand, first_result):
        pos = {id(t): k for k, t in enumerate(self.targets)}
        return {first_operand + len(self.sources) + a: first_result + pos[id(t)]
                for a, t in enumerate(self.filled)}


def _run_comm(comm, *, name):
    n_op, n_out = len(comm.operands), len(comm.targets)

    def body(*refs):
        comm.emit(0, 1, refs[:len(comm.sources)], refs[n_op:n_op + n_out], refs[n_op + n_out:])

    return pl.pallas_call(
        body, in_specs=[ANY] * n_op, out_specs=[ANY] * n_out, out_shape=comm.out_shapes,
        scratch_shapes=comm.sems, input_output_aliases=comm.aliases(0, 0), name=name)(*comm.operands)


def _at_steps(step, n_steps, phases):
    if n_steps == 1:
        for _, fn in phases:
            fn()
        return
    marks = {}
    for frac, fn in phases:
        marks.setdefault(min(int(frac * n_steps), n_steps - 1), []).append(fn)
    for mark, fns in sorted(marks.items()):
        @pl.when(step == mark)
        def _(fns=fns):
            for fn in fns:
                fn()


def _gather_emit(items, groups):
    def emit(step, n_steps, ins, outs, sems):
        send_sems, recv_sems, local_sems = sems
        x, y, c = _my_place()
        me, sibling = (x, y, c), (x, y, 1 - c)
        chips = [(1 - x, y), (x, 1 - y), (1 - x, 1 - y)]

        def copy(i, k, block, to, own=False):
            src_of, dst_of = items[i]
            dst = dst_of(outs, 4 * block[0] + 2 * block[1] + block[2])
            return pltpu.make_async_remote_copy(
                src_ref=src_of(ins) if own else dst, dst_ref=dst,
                send_sem=send_sems.at[i, k], recv_sem=recv_sems.at[i, k],
                device_id=to, device_id_type=MESH)

        def local(i):
            src_of, dst_of = items[i]
            return pltpu.make_async_copy(src_of(ins), dst_of(outs, 4 * x + 2 * y + c), local_sems.at[i])

        def first(i):
            return [copy(i, 0, me, sibling, own=True)] + [
                copy(i, 1 + j, me, (*chip, c), own=True) for j, chip in enumerate(chips)]

        def start():
            for i in range(len(items)):
                local(i).start()
                for cp in first(i):
                    cp.start()

        def forward(group):
            for j, chip in enumerate(chips):
                for i in group:
                    copy(i, 1 + j, (*chip, c), me).wait_recv()
                    copy(i, 4 + j, (*chip, c), sibling).start()

        def finish():
            for i in range(len(items)):
                copy(i, 0, sibling, me).wait_recv()
                for j, chip in enumerate(chips):
                    copy(i, 4 + j, (*chip, 1 - c), me).wait_recv()
            for i in range(len(items)):
                for cp in first(i) + [copy(i, 4 + j, (*chip, c), sibling) for j, chip in enumerate(chips)]:
                    cp.wait_send()
                local(i).wait()

        phases = [(0.0, start)]
        for g, group in enumerate(groups):
            phases.append(((g + 1) / len(groups), functools.partial(forward, group)))
        phases.append((1.0, finish))
        _at_steps(step, n_steps, phases)

    return emit


def _exchange_emit(items):
    def emit(step, n_steps, ins, outs, sems):
        send_sems, recv_sems, local_sems = sems
        x, y, c = _my_place()
        me = 4 * x + 2 * y + c

        def peer_of(k):
            return x ^ (k >> 2), y ^ ((k >> 1) & 1), c ^ (k & 1)

        def copy(i, k):
            src_of, dst_of = items[i]
            px, py, pc = peer_of(k)
            return pltpu.make_async_remote_copy(
                src_ref=src_of(ins, 4 * px + 2 * py + pc), dst_ref=dst_of(outs, me),
                send_sem=send_sems.at[i, k - 1], recv_sem=recv_sems.at[i, k - 1],
                device_id=(px, py, pc), device_id_type=MESH)

        def arrival(i, k):
            src_of, dst_of = items[i]
            px, py, pc = peer_of(k)
            peer = 4 * px + 2 * py + pc
            return pltpu.make_async_remote_copy(
                src_ref=src_of(ins, peer), dst_ref=dst_of(outs, peer),
                send_sem=send_sems.at[i, k - 1], recv_sem=recv_sems.at[i, k - 1],
                device_id=(x, y, c), device_id_type=MESH)

        def local(i):
            src_of, dst_of = items[i]
            return pltpu.make_async_copy(src_of(ins, me), dst_of(outs, me), local_sems.at[i])

        def start():
            for i in range(len(items)):
                local(i).start()
            for k in range(1, N_DEV):
                for i in range(len(items)):
                    copy(i, k).start()

        def finish():
            for k in range(1, N_DEV):
                for i in range(len(items)):
                    arrival(i, k).wait_recv()
            for k in range(1, N_DEV):
                for i in range(len(items)):
                    copy(i, k).wait_send()
            for i in range(len(items)):
                local(i).wait()

        _at_steps(step, n_steps, [(0.0, start), (1.0, finish)])

    return emit


def _adam_math(g, w, m, v):
    m = ADAM_B1 * m + (1.0 - ADAM_B1) * g
    v = ADAM_B2 * v + (1.0 - ADAM_B2) * (g * g)
    m_hat = m / (1.0 - ADAM_B1 ** ADAM_STEP)
    v_hat = v / (1.0 - ADAM_B2 ** ADAM_STEP)
    delta = -ADAM_LR * (m_hat / (jnp.sqrt(v_hat) + ADAM_EPS) + ADAM_WD * w)
    return delta, m, v


def _sum_adamw(parts, w, m, v, *, name, tr=256):
    L, nd, R, C = parts.shape
    tr = min(tr, R)
    assert R % tr == 0

    def body(p_ref, w_ref, m_ref, v_ref, g_ref, d_ref, nm_ref, nv_ref):
        g = p_ref[0, 0].astype(F32)
        for q in range(1, nd):
            g = g + p_ref[0, q].astype(F32)
        d, nm, nv = _adam_math(g, w_ref[0], m_ref[0], v_ref[0])
        g_ref[0] = g
        d_ref[0] = d
        nm_ref[0] = nm
        nv_ref[0] = nv

    blk = pl.BlockSpec((1, tr, C), lambda l, i: (l, i, 0))
    out = jax.ShapeDtypeStruct((L, R, C), F32)
    return pl.pallas_call(
        body, grid=(L, R // tr),
        in_specs=[pl.BlockSpec((1, nd, tr, C), lambda l, i: (l, 0, i, 0)), blk, blk, blk],
        out_specs=[blk] * 4, out_shape=[out] * 4,
        compiler_params=_params("parallel", "parallel"), name=name)(parts, w, m, v)


def _sum_parts(parts, *, name):
    nd, R, C = parts.shape

    def body(p_ref, o_ref):
        g = p_ref[0]
        for q in range(1, nd):
            g = g + p_ref[q]
        o_ref[...] = g

    return pl.pallas_call(
        body, out_shape=jax.ShapeDtypeStruct((R, C), F32),
        in_specs=[pl.BlockSpec(memory_space=pltpu.VMEM)],
        out_specs=pl.BlockSpec(memory_space=pltpu.VMEM), name=name)(parts)


def _adamw_small(g, w, m, v, *, name):
    def body(g_ref, w_ref, m_ref, v_ref, d_ref, nm_ref, nv_ref):
        d, nm, nv = _adam_math(g_ref[...], w_ref[...], m_ref[...], v_ref[...])
        d_ref[...] = d
        nm_ref[...] = nm
        nv_ref[...] = nv

    vm = pl.BlockSpec(memory_space=pltpu.VMEM)
    out = jax.ShapeDtypeStruct(g.shape, F32)
    return pl.pallas_call(body, out_shape=[out] * 3, in_specs=[vm] * 4, out_specs=[vm] * 3,
                          name=name)(g, w, m, v)


def kernel(x, norm_mix, norm_mlp, sb_wqkv, sb_wo, sgu_win, sgu_gain, sgu_ws, sgu_bs, sgu_wout, mlp_w1, mlp_w2, final_norm, loss_target, m_norm_mix, m_norm_mlp, m_sb_wqkv, m_sb_wo, m_sgu_win, m_sgu_gain, m_sgu_ws, m_sgu_bs, m_sgu_wout, m_mlp_w1, m_mlp_w2, m_final_norm, v_norm_mix, v_norm_mlp, v_sb_wqkv, v_sb_wo, v_sgu_win, v_sgu_gain, v_sgu_ws, v_sgu_bs, v_sgu_wout, v_mlp_w1, v_mlp_w2, v_final_norm):
    batch, seq, D = x.shape
    T = batch * seq
    x0 = x.reshape(T, D)
    target = loss_target.reshape(T, D)

    WQKV, WO, WIN, WOUT, W1, W2, GAIN = range(7)
    big = [sb_wqkv, sb_wo, sgu_win, sgu_wout, mlp_w1, mlp_w2]
    shards = [w.astype(BF16) for w in big] + [sgu_gain[:, None, :]]
    layer_weights = [[(WQKV, 0), (WO, 0), (W1, 0), (W2, 0)], [(WIN, 0), (WOUT, 0), (W1, 1), (W2, 1)],
                     [(WQKV, 1), (WO, 1), (W1, 2), (W2, 2)], [(WIN, 1), (WOUT, 1), (W1, 3), (W2, 3)]]

    def weight_item(f, l):
        return (lambda ins: ins[f].at[l], lambda outs, p: outs[f].at[l, p])

    first = [weight_item(f, l) for f, l in layer_weights[0]]
    first.append((lambda ins: ins[GAIN], lambda outs, p: outs[GAIN].at[:, p]))
    targets = [jax.ShapeDtypeStruct((s.shape[0], N_DEV) + s.shape[1:], s.dtype) for s in shards]
    gathered0 = _run_comm(_Comm(shards, targets, len(first), _gather_emit(first, [list(range(len(first)))])),
                          name="gather_layer0")
    gain_sgu = gathered0[GAIN].reshape(-1, 1, SGU_FFN)
    gw = dict(enumerate(gathered0[:GAIN]))

    def gather_layers(layers):
        fams = sorted({f for lw in layers for f, _ in layer_weights[lw]})
        items = [(lambda ins, f=f, l=l: ins[f].at[l], lambda outs, p, t=fams.index(f), l=l: outs[t].at[l, p])
                 for lw in layers for f, l in layer_weights[lw]]
        groups = [list(range(4 * g, 4 * g + 4)) for g in range(len(layers))]
        return _Comm(shards[:GAIN], [gw[f] for f in fams], len(items), _gather_emit(items, groups)), fams

    def weight(f):
        g = gw[f]
        return g if f in (WQKV, WIN, W1) else g.reshape(g.shape[0], 1, N_DEV * g.shape[2], g.shape[3])

    saved = []
    xs = x0
    for i in range(DEPTH):
        j = i // 2
        h = _rms_fwd(xs, norm_mix[i:i + 1], name=f"norm_mix{i}")
        if i % 2 == 0:
            qkv = _mm_nn(h, weight(WQKV), j, out_dtype=BF16, name=f"qkv{i}", pb=2)
            comm, fams = gather_layers([1, 2] if i == 0 else [3])
            o, results = _sb_fwd(qkv, batch=batch, seq=seq, name=f"sb_fwd{i}", comm=comm)
            gw.update(zip(fams, results))
            x_mid = _mm_nn(o, weight(WO), j, out_dtype=F32, res=xs, name=f"wo{i}")
            mix = (h, qkv, o)
        else:
            gain_j = gain_sgu[j]
            bst = sgu_bs[j].T
            uvp = _mm_nn(h, weight(WIN), j, out_dtype=BF16, name=f"win{i}", pb=2)
            yv = _sgu_fwd(uvp, gain_j, sgu_ws[j], bst, name=f"sgu_fwd{i}")
            x_mid = _mm_nn(yv, weight(WOUT), j, out_dtype=F32, res=xs, name=f"wout{i}")
            mix = (h, uvp, yv, gain_j, bst)
        h2 = _rms_fwd(x_mid, norm_mlp[i:i + 1], name=f"norm_mlp{i}")
        a = _mm_nn(h2, weight(W1), i, out_dtype=BF16, name=f"w1_{i}", pb=2)
        x_out = _mm_nn(a, weight(W2), i, out_dtype=F32, res=x_mid, a_act="relu2", name=f"w2_{i}", tm=512, tn=512)
        saved.append((xs, mix, x_mid, h2, a))
        xs = x_out
    g_wqkv, g_wo, g_win, g_wout, g_w1, g_w2 = [weight(f) for f in range(GAIN)]

    dx, dxb, sq, d_final = _loss_head(xs, final_norm.reshape(1, D), target, name="loss_head")
    loss = lax.psum(0.5 * jnp.sum(sq) / D, ("x", "y", "c"))

    SMALL = GAIN
    stacks = {f: jax.ShapeDtypeStruct((w.shape[0], N_DEV) + w.shape[1:], BF16) for f, w in enumerate(big)}
    pending = []

    def row_shards(p):
        return p.reshape(N_DEV, p.shape[1] // N_DEV, p.shape[2])

    def exchange_pending():
        fams = sorted({f for _, f, _ in pending})
        items = [(lambda ins, p, a=a: ins[a].at[p], lambda outs, q, t=fams.index(f), l=l: outs[t].at[l, q])
                 for a, (_, f, l) in enumerate(pending)]
        comm = _Comm([part for part, _, _ in pending], [stacks[f] for f in fams], len(items), _exchange_emit(items))
        pending.clear()
        return comm, fams

    d_norm_mix, d_norm_mlp = [None] * DEPTH, [None] * DEPTH
    d_gain, d_ws, d_bs = [None] * 2, [None] * 2, [None] * 2
    for i in reversed(range(DEPTH)):
        j = i // 2
        xs, mix, x_mid, h2, a = saved[i]
        da = _mm_nt(dxb, g_w2, i, out_dtype=BF16, act_src=a, name=f"d_a{i}", tn=1024)
        pending.append((row_shards(_mm_tn(a, dxb, shards=1, a_act="relu2", name=f"d_w2_{i}")), W2, i))
        pending.append((_mm_tn(h2, da, shards=N_DEV, name=f"d_w1_{i}", pb=2), W1, i))
        (dx, dxb, d_norm_mlp[i]), _ = _mm_nt_rms_bwd(da, g_w1, i, x_mid, norm_mlp[i:i + 1], dx, name=f"d_h2_{i}")
        if i % 2 == 0:
            h, qkv, o = mix
            do = _mm_nt(dxb, g_wo, j, out_dtype=BF16, name=f"d_o{i}", tn=1024)
            pending.append((row_shards(_mm_tn(o, dxb, shards=1, name=f"d_wo{i}")), WO, j))
            comm, fams = exchange_pending()
            (dq, dk, dv), results = _sb_bwd(qkv, o, do, batch=batch, seq=seq, name=f"sb_bwd{i}", comm=comm)
            stacks.update(zip(fams, results))
            dqkv = jnp.concatenate([dq, dk, dv], axis=1)
            pending.append((_mm_tn(h, dqkv, shards=N_DEV, name=f"d_wqkv{i}", pb=2), WQKV, j))
            mixer_in, w_in = dqkv, g_wqkv
        else:
            h, uvp, yv, gain_j, bst = mix
            dy = _mm_nt(dxb, g_wout, j, out_dtype=BF16, name=f"d_y{i}", tn=1024)
            pending.append((row_shards(_mm_tn(yv, dxb, shards=1, name=f"d_wout{i}")), WOUT, j))
            duv, d_gain[j], d_ws[j], dbst = _sgu_bwd(uvp, dy, gain_j, sgu_ws[j], bst, name=f"sgu_bwd{i}")
            d_bs[j] = dbst.T
            pending.append((_mm_tn(h, duv, shards=N_DEV, name=f"d_win{i}", pb=2), WIN, j))
            mixer_in, w_in = duv, g_win
        comm, fams = exchange_pending() if i == 0 else (None, [])
        (dx, dxb, d_norm_mix[i]), results = _mm_nt_rms_bwd(
            mixer_in, w_in, j, xs, norm_mix[i:i + 1], dx, name=f"d_h_mix{i}", comm=comm)
        stacks.update(zip(fams, results))
    grad_x = dx.reshape(batch, seq, D)

    small = [jnp.concatenate(d_norm_mix, 0), jnp.concatenate(d_norm_mlp, 0), d_final,
             jnp.concatenate(d_gain, 0), jnp.stack(d_bs, 0), jnp.stack(d_ws, 0)]
    small_flat = jnp.concatenate([s.reshape(-1) for s in small])
    n_small = small_flat.shape[0]
    small_rows = -(-n_small // (N_DEV * SUBLANES * LANES)) * SUBLANES
    small_flat = jnp.pad(small_flat, (0, N_DEV * small_rows * LANES - n_small))
    stacks[SMALL] = jax.ShapeDtypeStruct((1, N_DEV, small_rows, LANES), F32)
    pending.append((small_flat.reshape(N_DEV, small_rows, LANES), SMALL, 0))
    comm, fams = exchange_pending()
    stacks.update(zip(fams, _run_comm(comm, name="exchange_last")))
    r_wqkv, r_wo, r_win, r_wout, r_w1, r_w2, r_small = [stacks[f] for f in range(SMALL + 1)]

    u_wqkv = _sum_adamw(r_wqkv, sb_wqkv, m_sb_wqkv, v_sb_wqkv, name="adamw_wqkv")
    u_wo = _sum_adamw(r_wo, sb_wo, m_sb_wo, v_sb_wo, name="adamw_wo")
    u_win = _sum_adamw(r_win, sgu_win, m_sgu_win, v_sgu_win, name="adamw_win")
    u_wout = _sum_adamw(r_wout, sgu_wout, m_sgu_wout, v_sgu_wout, name="adamw_wout")
    u_w1 = _sum_adamw(r_w1, mlp_w1, m_mlp_w1, v_mlp_w1, name="adamw_w1")
    u_w2 = _sum_adamw(r_w2, mlp_w2, m_mlp_w2, v_mlp_w2, name="adamw_w2")

    small_sum = _sum_parts(r_small[0], name="sum_small")
    g_small = _all_gather([small_sum[None]], name="gather_small")[0].reshape(-1)[:n_small]

    shapes = [s.shape for s in small]
    sizes = [s.size for s in small]
    offs = [sum(sizes[:k]) for k in range(len(sizes))]
    me = 4 * lax.axis_index("x") + 2 * lax.axis_index("y") + lax.axis_index("c")
    shard_w = SGU_FFN // N_DEV

    def pack(arrs):
        flat = jnp.concatenate([a_.reshape(-1) for a_ in arrs])
        return jnp.pad(flat, (0, N_DEV * small_rows * LANES - n_small)).reshape(-1, LANES)

    def full_gain(gshard):
        return lax.dynamic_update_slice(jnp.zeros((2, SGU_FFN), F32), gshard, (0, me * shard_w))

    w_small = pack([norm_mix, norm_mlp, final_norm, full_gain(sgu_gain), sgu_bs, sgu_ws])
    m_small = pack([m_norm_mix, m_norm_mlp, m_final_norm, full_gain(m_sgu_gain), m_sgu_bs, m_sgu_ws])
    v_small = pack([v_norm_mix, v_norm_mlp, v_final_norm, full_gain(v_sgu_gain), v_sgu_bs, v_sgu_ws])
    g_pack = jnp.pad(g_small, (0, N_DEV * small_rows * LANES - n_small)).reshape(-1, LANES)
    sm = [g_pack] + list(_adamw_small(g_pack, w_small, m_small, v_small, name="adamw_small"))

    def unpack(flat2d):
        flat = flat2d.reshape(-1)
        out = [flat[offs[k]:offs[k] + sizes[k]].reshape(shapes[k]) for k in range(len(sizes))]
        out[2] = out[2].reshape(D)
        out[3] = lax.dynamic_slice(out[3], (0, me * shard_w), (2, shard_w))
        return out

    outs = []
    for k, big_u in enumerate(zip(u_wqkv, u_wo, u_win, u_wout, u_w1, u_w2)):
        s_nm, s_nl, s_fn, s_gain, s_bs, s_ws = unpack(sm[k])
        b_wqkv, b_wo, b_win, b_wout, b_w1, b_w2 = big_u
        outs += [s_nm, s_nl, b_wqkv, b_wo, b_win, s_gain, s_ws, s_bs, b_wout, b_w1, b_w2, s_fn]
    return (loss, grad_x, *outs)
```

```python
import functools

import jax
import jax.numpy as jnp
from jax import lax
from jax.experimental import pallas as pl
from jax.experimental.pallas import tpu as pltpu

F32 = jnp.float32
BF16 = jnp.bfloat16

N_DEV = 8
D_MODEL = 1024
SEQ = 2048
DEPTH = 4
SB_HEAD_DIM = 64
SGU_CHUNK = 128
SGU_FFN = 2 * D_MODEL
SGU_GROUPS = 8
SGU_GROUP_W = SGU_FFN // SGU_GROUPS
EPS = 1e-6

ADAM_LR = 0.001
ADAM_B1 = 0.9
ADAM_B2 = 0.999
ADAM_EPS = 1e-08
ADAM_WD = 0.01
ADAM_STEP = 10

LANES = 128
SUBLANES = 8
VMEM_LIMIT = 56 * 1024 * 1024
EXP_ZERO_BELOW = -104.0
SB_TILE = 256
SB_STRAIGHT = 2
SB_PAIRS = 2

MESH = pl.DeviceIdType.MESH
ANY = pl.BlockSpec(memory_space=pl.ANY)


def _params(*sem):
    return pltpu.CompilerParams(dimension_semantics=sem, vmem_limit_bytes=VMEM_LIMIT)


def _dot(a, b):
    return jnp.dot(a, b, preferred_element_type=F32)


def _dot_nt(a, b):
    return lax.dot_general(a, b, (((1,), (1,)), ((), ())), preferred_element_type=F32)


def _dot_tn(a, b):
    return lax.dot_general(a, b, (((0,), (0,)), ((), ())), preferred_element_type=F32)


def _split_bf16(x):
    hi = x.astype(BF16)
    lo = (x - hi.astype(F32)).astype(BF16)
    return hi, lo


def _relu2(av):
    t = jnp.maximum(av, jnp.zeros_like(av))
    return t * t


def _mm_nn(a, w, l, *, out_dtype, name, res=None, a_act=None, gain=None, tm=512, kc=1024):
    M, K = a.shape
    _, P, K2, n = w.shape
    assert K2 == K
    tm, kc = min(tm, M), min(kc, K)
    assert M % tm == 0 and K % kc == 0

    n_in = 2 + (gain is not None) + (res is not None)

    def body(*refs):
        a_ref, w_ref, o_ref = refs[0], refs[1], refs[n_in]
        if gain is not None:
            xv = a_ref[...]
            r = lax.rsqrt(jnp.mean(xv * xv, axis=-1, keepdims=True) + EPS)
            h = (xv * r * refs[2][...]).astype(BF16)
            refs[n_in + 1][...] = h
        for p in range(P):
            sl = slice(p * n, (p + 1) * n)
            acc = None
            for k0 in range(0, K, kc):
                if gain is not None:
                    av = h[:, k0:k0 + kc]
                else:
                    av = a_ref[:, k0:k0 + kc]
                    av = _relu2(av) if a_act == "relu2" else av.astype(BF16)
                d = _dot(av, w_ref[0, p, k0:k0 + kc, :])
                acc = d if acc is None else acc + d
            if res is not None:
                acc = acc + refs[n_in - 1][:, sl]
            o_ref[:, sl] = acc.astype(out_dtype)

    row = lambda width: pl.BlockSpec((tm, width), lambda i: (i, 0))
    in_specs = [row(K), pl.BlockSpec((1, P, K, n), lambda i: (l, 0, 0, 0), pipeline_mode=pl.Buffered(1))]
    args = [a, w]
    if gain is not None:
        in_specs.append(pl.BlockSpec((1, K), lambda i: (0, 0)))
        args.append(gain)
    if res is not None:
        in_specs.append(row(P * n))
        args.append(res)
    out_specs, out_shape = [row(P * n)], [jax.ShapeDtypeStruct((M, P * n), out_dtype)]
    if gain is not None:
        out_specs.append(row(K))
        out_shape.append(jax.ShapeDtypeStruct((M, K), BF16))
    outs = pl.pallas_call(body, grid=(M // tm,), in_specs=in_specs, out_specs=out_specs,
                          out_shape=out_shape, compiler_params=_params("parallel"), name=name)(*args)
    return outs if gain is not None else outs[0]


def _mm_nt(a, w, l, *, out_dtype, name, act_src=None, tm=512, tn=1024):
    M, K = a.shape
    _, P, Nout, kc = w.shape
    assert P == 1 and K == kc
    tm, tn = min(tm, M), min(tn, Nout)
    assert M % tm == 0 and Nout % tn == 0

    def body(*refs):
        a_ref, w_ref, o_ref = refs[0], refs[1], refs[-1]
        av = a_ref[...].astype(BF16)
        for n0 in range(0, Nout, tn):
            r = _dot_nt(av, w_ref[0, 0, n0:n0 + tn, :])
            if act_src is not None:
                r = r * (2.0 * jnp.maximum(refs[2][:, n0:n0 + tn].astype(F32), 0.0))
            o_ref[:, n0:n0 + tn] = r.astype(out_dtype)

    row = lambda width: pl.BlockSpec((tm, width), lambda i: (i, 0))
    in_specs = [row(K), pl.BlockSpec((1, 1, Nout, K), lambda i: (l, 0, 0, 0), pipeline_mode=pl.Buffered(1))]
    args = [a, w]
    if act_src is not None:
        in_specs.append(row(Nout))
        args.append(act_src)
    return pl.pallas_call(
        body, grid=(M // tm,), in_specs=in_specs, out_specs=row(Nout),
        out_shape=jax.ShapeDtypeStruct((M, Nout), out_dtype),
        compiler_params=_params("parallel"), name=name)(*args)


def _mm_tn(a, b, *, shards, name, a_act=None, tm=2048, tk=1024, pb=1):
    M, K = a.shape
    M2, N = b.shape
    assert M2 == M
    n = N // shards
    tm, tk = min(tm, M), min(tk, K)
    assert M % tm == 0 and K % tk == 0 and shards % pb == 0
    width, nm = pb * n, M // tm

    def body(a_ref, b_ref, o_ref, acc):
        m = pl.program_id(2)
        av = a_ref[...]
        if a_act == "relu2":
            av = _relu2(av)

        @pl.when(m == 0)
        def _():
            acc[...] = jnp.zeros_like(acc)

        acc[...] += _dot_tn(av.astype(BF16), b_ref[...].astype(BF16))

        @pl.when(m == nm - 1)
        def _():
            for p in range(pb):
                o_ref[p] = acc[:, p * n:(p + 1) * n].astype(BF16)

    return pl.pallas_call(
        body, grid=(K // tk, N // width, nm),
        in_specs=[pl.BlockSpec((tm, tk), lambda i, j, m: (m, i)),
                  pl.BlockSpec((tm, width), lambda i, j, m: (m, j))],
        out_specs=pl.BlockSpec((pb, tk, n), lambda i, j, m: (j, i, 0)),
        out_shape=jax.ShapeDtypeStruct((shards, K, n), BF16),
        scratch_shapes=[pltpu.VMEM((tk, width), F32)],
        compiler_params=_params("parallel", "parallel", "arbitrary"), name=name)(a, b)


def _mm_nt_rms_bwd(a, w, l, x, gain, dres, *, name, comm=None, tm=512):
    M, K = a.shape
    _, P, D, kc = w.shape
    assert K == P * kc and x.shape == (M, D)
    tm = min(tm, M)
    nr = M // tm

    def body(*refs):
        (a_ref, w_ref, x_ref, g_ref, dres_ref), (dx_ref, dxb_ref, dg_ref), (acc,) = carried.split(refs, 5, 3, 1)
        i = pl.program_id(0)
        carried.emit(refs)
        dhv = None
        for p in range(P):
            d = _dot_nt(a_ref[:, p * kc:(p + 1) * kc], w_ref[0, p])
            dhv = d if dhv is None else dhv + d
        xv = x_ref[...]
        r = lax.rsqrt(jnp.mean(xv * xv, axis=-1, keepdims=True) + EPS)
        xhat = xv * r
        dxhat = dhv * g_ref[...]
        dx = dres_ref[...] + r * (dxhat - xhat * jnp.mean(dxhat * xhat, axis=-1, keepdims=True))
        dx_ref[...] = dx
        dxb_ref[...] = dx.astype(BF16)
        part = jnp.sum((dhv * xhat).reshape(tm // SUBLANES, SUBLANES, D), axis=0)

        @pl.when(i == 0)
        def _():
            acc[...] = jnp.zeros_like(acc)

        acc[...] += part

        @pl.when(i == nr - 1)
        def _():
            dg_ref[...] = jnp.sum(acc[...], axis=0, keepdims=True)

    row = pl.BlockSpec((tm, D), lambda i: (i, 0))
    vec = pl.BlockSpec((1, D), lambda i: (0, 0))
    carried = _Carried(comm, (nr,))
    return carried.pallas_call(
        body,
        in_specs=[pl.BlockSpec((tm, K), lambda i: (i, 0)),
                  pl.BlockSpec((1, P, D, kc), lambda i: (l, 0, 0, 0), pipeline_mode=pl.Buffered(1)),
                  row, vec, row],
        out_specs=[row, row, vec],
        out_shape=[jax.ShapeDtypeStruct((M, D), F32), jax.ShapeDtypeStruct((M, D), BF16),
                   jax.ShapeDtypeStruct((1, D), F32)],
        scratch_shapes=[pltpu.VMEM((SUBLANES, D), F32)],
        args=[a, w, x, gain, dres], name=name)


def _loss_head(x, gain, target, *, name, tr=512):
    T, D = x.shape
    nr = T // tr

    def body(x_ref, g_ref, t_ref, dx_ref, dxb_ref, sq_ref, dg_ref, sq_acc, dg_acc):
        i = pl.program_id(0)
        xv = x_ref[...]
        g = g_ref[...]
        r = lax.rsqrt(jnp.mean(xv * xv, axis=-1, keepdims=True) + EPS)
        xhat = xv * r
        err = xhat * g - t_ref[...]
        dy = err * (1.0 / D)
        dxhat = dy * g
        dx = r * (dxhat - xhat * jnp.mean(dxhat * xhat, axis=-1, keepdims=True))
        dx_ref[...] = dx
        dxb_ref[...] = dx.astype(BF16)
        sq = jnp.sum((err * err).reshape(tr // SUBLANES, SUBLANES, D), axis=0)
        dg = jnp.sum((dy * xhat).reshape(tr // SUBLANES, SUBLANES, D), axis=0)

        @pl.when(i == 0)
        def _():
            sq_acc[...] = sq
            dg_acc[...] = dg

        @pl.when(i > 0)
        def _():
            sq_acc[...] += sq
            dg_acc[...] += dg

        @pl.when(i == nr - 1)
        def _():
            sq_ref[...] = sq_acc[...]
            dg_ref[...] = jnp.sum(dg_acc[...], axis=0, keepdims=True)

    row = pl.BlockSpec((tr, D), lambda i: (i, 0))
    vec = pl.BlockSpec((1, D), lambda i: (0, 0))
    part = pl.BlockSpec((SUBLANES, D), lambda i: (0, 0))
    return pl.pallas_call(
        body, grid=(nr,), in_specs=[row, vec, row], out_specs=[row, row, part, vec],
        out_shape=[jax.ShapeDtypeStruct((T, D), F32), jax.ShapeDtypeStruct((T, D), BF16),
                   jax.ShapeDtypeStruct((SUBLANES, D), F32), jax.ShapeDtypeStruct((1, D), F32)],
        scratch_shapes=[pltpu.VMEM((SUBLANES, D), F32), pltpu.VMEM((SUBLANES, D), F32)],
        compiler_params=_params("arbitrary"), name=name)(x, gain, target)


def _head0_lanes():
    return lax.broadcasted_iota(jnp.int32, (1, LANES), 1) < SB_HEAD_DIM


def _stack_heads(x):
    zero = jnp.zeros_like(x)
    h0 = _head0_lanes()
    return jnp.concatenate([jnp.where(h0, x, zero), jnp.where(h0, zero, x)], axis=0)


def _unstack_heads(y, tq):
    return jnp.where(_head0_lanes(), y[:tq], y[tq:])


def _past_mask(tq, tk):
    row = lax.broadcasted_iota(jnp.int32, (2 * tq, tk), 0) & (tq - 1)
    col = lax.broadcasted_iota(jnp.int32, (2 * tq, tk), 1)
    return col < row


def _max_of(arrays):
    return functools.reduce(jnp.maximum, [jnp.max(a) for a in arrays])


def _sb_logs(z, past):
    minus_abs = lax.bitcast_convert_type(
        lax.bitcast_convert_type(z, jnp.uint32) | jnp.uint32(0x80000000), F32)
    log_beta = jnp.minimum(z, 0.0) - jnp.log(1.0 + jnp.exp(minus_abs))
    l = log_beta - z
    if past is not None:
        l = jnp.where(past, l, 0.0)
    return log_beta, l


def _suffix_matrix(tk):
    j = lax.broadcasted_iota(jnp.int32, (2 * tk, tk), 0) & (tk - 1)
    s = lax.broadcasted_iota(jnp.int32, (2 * tk, tk), 1)
    return (j > s).astype(BF16)


def _suffix_sum(x, u2, exact=True):
    if not exact:
        return _dot(x.astype(BF16), u2[:x.shape[1]])
    hi, lo = _split_bf16(x)
    return _dot(jnp.concatenate([hi, lo], axis=1), u2)


class _Carried:
    def __init__(self, comm, grid):
        self.comm, self.grid = comm, grid
        self.n_op = len(comm.operands) if comm else 0
        self.n_tgt = len(comm.targets) if comm else 0

    def split(self, refs, n_in, n_out, n_scratch):
        self.n_in, self.n_out, self.n_scratch = n_in, n_out, n_scratch
        a = n_in + self.n_op
        b = a + n_out + self.n_tgt
        return refs[:n_in], refs[a:a + n_out], refs[b:b + n_scratch]

    def emit(self, refs):
        if self.comm is None:
            return
        step, n_steps = 0, 1
        for d, size in enumerate(self.grid):
            step = step * size + pl.program_id(d)
            n_steps *= size
        a = self.n_in + self.n_op + self.n_out
        self.comm.emit(step, n_steps, refs[self.n_in:self.n_in + len(self.comm.sources)],
                       refs[a:a + self.n_tgt], refs[a + self.n_tgt + self.n_scratch:])

    def pallas_call(self, body, *, in_specs, out_specs, out_shape, scratch_shapes, args, name):
        comm = self.comm
        n_in, n_out = len(in_specs), len(out_specs)
        aliases = {}
        if comm is not None:
            in_specs = in_specs + [ANY] * self.n_op
            out_specs = out_specs + [ANY] * self.n_tgt
            out_shape = out_shape + comm.out_shapes
            scratch_shapes = scratch_shapes + comm.sems
            args = args + comm.operands
            aliases = comm.aliases(n_in, n_out)
        sem = ("parallel",) * (len(self.grid) - 1) + ("arbitrary",) if comm is None else ("arbitrary",) * len(self.grid)
        results = pl.pallas_call(
            body, grid=self.grid, in_specs=in_specs, out_specs=out_specs, out_shape=out_shape,
            scratch_shapes=scratch_shapes, input_output_aliases=aliases,
            compiler_params=_params(*sem), name=name)(*args)
        return results[:n_out], results[n_out:]


def _sb_fwd(qkv, *, batch, seq, name, comm=None, tq=SB_TILE, n_pre=SB_STRAIGHT, pairs=SB_PAIRS):
    T, D3 = qkv.shape
    D = D3 // 3
    nhp = D // LANES
    tk = tq
    nq = seq // tq
    scale = SB_HEAD_DIM ** -0.5
    assert 1 <= n_pre <= nq and nhp % pairs == 0

    def body(*refs):
        (q_ref, k_ref, v_ref), (o_ref,), (acc,) = carried.split(refs, 3, 1, 1)
        qi = pl.program_id(2)
        carried.emit(refs)
        qs = [_stack_heads(q_ref[:, lanes]) * scale for lanes in cols]
        past = _past_mask(tq, tk)
        u = _suffix_matrix(tk)

        def block(kb, g, c, diag):
            ks = pl.multiple_of(kb * tk, tk)
            z = _dot_nt(qs[g], k_ref[pl.ds(ks, tk), cols[g]])
            log_beta, l = _sb_logs(z, past if diag else None)
            arg = log_beta + _suffix_sum(l, u, exact=False)
            a = jnp.exp(arg if c is None else arg + c)
            if diag:
                a = jnp.where(past, a, 0.0)
            return _dot(a.astype(BF16), v_ref[pl.ds(ks, tk), cols[g]]), jnp.sum(l, axis=1, keepdims=True)

        def straight(n):
            o_sum, c = [None] * pairs, [None] * pairs
            for b in range(n):
                for g in range(pairs):
                    o_b, c_b = block(qi - b, g, c[g], b == 0)
                    o_sum[g] = o_b if b == 0 else o_sum[g] + o_b
                    c[g] = c_b if b == 0 else c[g] + c_b
            return o_sum, c

        def finish(o_sum):
            for g in range(pairs):
                o_ref[:, cols[g]] = _unstack_heads(o_sum[g], tq)

        for n in range(1, n_pre):
            @pl.when(qi == n - 1)
            def _(n=n):
                finish(straight(n)[0])

        @pl.when(qi >= n_pre - 1)
        def _():
            o_sum, c = straight(n_pre)
            for g in range(pairs):
                acc[g] = o_sum[g]

            def cond(st):
                kb, c = st
                return jnp.logical_and(kb >= 0, _max_of(c) > EXP_ZERO_BELOW)

            def step(st):
                kb, c = st
                new_c = []
                for g in range(pairs):
                    o_n, c_n = block(kb, g, c[g], False)
                    acc[g] += o_n
                    new_c.append(c[g] + c_n)
                return kb - 1, tuple(new_c)

            lax.while_loop(cond, step, (qi - n_pre, tuple(c)))
            finish([acc[g] for g in range(pairs)])

    cols = [slice(g * LANES, (g + 1) * LANES) for g in range(pairs)]
    width = pairs * LANES
    carried = _Carried(comm, (batch, nhp // pairs, nq))
    (o,), comm_results = carried.pallas_call(
        body,
        in_specs=[pl.BlockSpec((tq, width), lambda b, p, i: (b * nq + i, p)),
                  pl.BlockSpec((seq, width), lambda b, p, i: (b, nhp // pairs + p)),
                  pl.BlockSpec((seq, width), lambda b, p, i: (b, 2 * (nhp // pairs) + p))],
        out_specs=[pl.BlockSpec((tq, width), lambda b, p, i: (b * nq + i, p))],
        out_shape=[jax.ShapeDtypeStruct((T, D), F32)],
        scratch_shapes=[pltpu.VMEM((pairs, 2 * tq, LANES), F32)],
        args=[qkv, qkv, qkv], name=name)
    return o, comm_results


def _sb_bwd(qkv, o, do, *, batch, seq, name, comm=None, tq=SB_TILE, n_pre=SB_STRAIGHT, pairs=SB_PAIRS):
    T, D3 = qkv.shape
    D = D3 // 3
    nhp = D // LANES
    tk = tq
    nq = seq // tq
    scale = SB_HEAD_DIM ** -0.5

    def body(*refs):
        ins, outs, scratch = carried.split(refs, 5, 3, 3)
        q_ref, k_ref, v_ref, o_ref, do_ref = ins
        dq_ref, dk_ref, dv_ref = outs
        dq_acc, dk_acc, dv_acc = scratch
        qi = pl.program_id(2)
        carried.emit(refs)

        @pl.when(qi == 0)
        def _():
            dk_acc[...] = jnp.zeros_like(dk_acc)
            dv_acc[...] = jnp.zeros_like(dv_acc)

        qs = [_stack_heads(q_ref[:, lanes]) * scale for lanes in cols]
        dos = [_stack_heads(do_ref[:, lanes]) for lanes in cols]
        dsum = [jnp.sum(_stack_heads(do_ref[:, lanes].astype(F32) * o_ref[:, lanes]), axis=1, keepdims=True)
                for lanes in cols]
        past = _past_mask(tq, tk)
        u = _suffix_matrix(tk)

        def block(kb, p, c, gc, diag):
            ks = pl.multiple_of(kb * tk, tk)
            kblk = k_ref[pl.ds(ks, tk), cols[p]]
            vblk = v_ref[pl.ds(ks, tk), cols[p]]
            z = _dot_nt(qs[p], kblk)
            log_beta, l = _sb_logs(z, past if diag else None)
            arg = log_beta + _suffix_sum(l, u, exact=False)
            a = jnp.exp(arg if c is None else arg + c)
            if diag:
                a = jnp.where(past, a, 0.0)
            a = a.astype(BF16)
            beta = 1.0 - jnp.exp(l)
            g = a.astype(F32) * _dot_nt(dos[p], vblk)
            gs = _suffix_sum(g, u)
            dz = g - beta * (dsum[p] - (gs if gc is None else gs + gc))
            if diag:
                dz = jnp.where(past, dz, 0.0)
            dzb = dz.astype(BF16)
            dk_acc[pl.ds(ks, tk), cols[p]] += _dot_tn(dzb, qs[p])
            dv_acc[pl.ds(ks, tk), cols[p]] += _dot_tn(a, dos[p])
            return (_dot(dzb, kblk), jnp.sum(l, axis=1, keepdims=True),
                    jnp.sum(g, axis=1, keepdims=True))

        def finish(dq_sum):
            for p in range(pairs):
                dq_ref[:, cols[p]] = (_unstack_heads(dq_sum[p], tq) * scale).astype(BF16)

        def straight(n):
            dq_sum, c, gc = [None] * pairs, [None] * pairs, [None] * pairs
            for b in range(n):
                for p in range(pairs):
                    dq_b, c_b, g_b = block(qi - b, p, c[p], gc[p], b == 0)
                    dq_sum[p] = dq_b if b == 0 else dq_sum[p] + dq_b
                    c[p] = c_b if b == 0 else c[p] + c_b
                    gc[p] = g_b if b == 0 else gc[p] + g_b
            return dq_sum, c, gc

        for n in range(1, n_pre):
            @pl.when(qi == n - 1)
            def _(n=n):
                finish(straight(n)[0])

        @pl.when(qi >= n_pre - 1)
        def _():
            dq_sum, c, gc = straight(n_pre)
            for p in range(pairs):
                dq_acc[p] = dq_sum[p]

            def cond(st):
                kb, c, gc = st
                return jnp.logical_and(kb >= 0, _max_of(c) > EXP_ZERO_BELOW)

            def step(st):
                kb, c, gc = st
                new_c, new_gc = [], []
                for p in range(pairs):
                    dq_n, c_n, g_n = block(kb, p, c[p], gc[p], False)
                    dq_acc[p] += dq_n
                    new_c.append(c[p] + c_n)
                    new_gc.append(gc[p] + g_n)
                return kb - 1, tuple(new_c), tuple(new_gc)

            lax.while_loop(cond, step, (qi - n_pre, tuple(c), tuple(gc)))
            finish([dq_acc[p] for p in range(pairs)])

        @pl.when(qi == nq - 1)
        def _():
            dk_ref[...] = dk_acc[...].astype(BF16)
            dv_ref[...] = dv_acc[...].astype(BF16)

    cols = [slice(p * LANES, (p + 1) * LANES) for p in range(pairs)]
    width, ncb = pairs * LANES, nhp // pairs
    qspec = pl.BlockSpec((tq, width), lambda b, p, i: (b * nq + i, p))
    sspec = pl.BlockSpec((seq, width), lambda b, p, i: (b, p))
    out = jax.ShapeDtypeStruct((T, D), BF16)
    carried = _Carried(comm, (batch, ncb, nq))
    return carried.pallas_call(
        body,
        in_specs=[qspec,
                  pl.BlockSpec((seq, width), lambda b, p, i: (b, ncb + p)),
                  pl.BlockSpec((seq, width), lambda b, p, i: (b, 2 * ncb + p)),
                  qspec, qspec],
        out_specs=[qspec, sspec, sspec], out_shape=[out, out, out],
        scratch_shapes=[pltpu.VMEM((pairs, 2 * tq, LANES), F32), pltpu.VMEM((seq, width), F32),
                        pltpu.VMEM((seq, width), F32)],
        args=[qkv, qkv, qkv, o, do], name=name)


_GELU_C = 0.7978845608028654
_GELU_A = 0.044715


def _gelu(x, with_grad=False):
    xx = x * x
    t = jnp.tanh(x * (_GELU_C + (_GELU_C * _GELU_A) * xx))
    hx = 0.5 * x
    y = hx + hx * t
    if not with_grad:
        return y
    grad = (0.5 + 0.5 * t) + (hx * (1.0 - t * t)) * (_GELU_C + (3.0 * _GELU_C * _GELU_A) * xx)
    return y, grad


def _causal_ws(ws_ref, g):
    t = lax.broadcasted_iota(jnp.int32, (SGU_CHUNK, SGU_CHUNK), 0)
    s = lax.broadcasted_iota(jnp.int32, (SGU_CHUNK, SGU_CHUNK), 1)
    return jnp.where(s <= t, ws_ref[g], 0.0)


def _sgu_fwd(uvp, gain, ws, bst, *, name):
    T, F2 = uvp.shape
    F = F2 // 2
    C, G, W = SGU_CHUNK, SGU_GROUPS, SGU_GROUP_W

    def body(uv_ref, g_ref, ws_ref, bs_ref, y_ref):
        uvf = uv_ref[...].astype(F32)
        uv = _gelu(uvf)
        u, v = uv[:, :F], uv[:, F:]
        r = lax.rsqrt(jnp.mean(v * v, axis=-1, keepdims=True) + EPS)
        vn = (v * r * g_ref[...]).astype(BF16)
        for g in range(G):
            sl = slice(g * W, (g + 1) * W)
            mixed = _dot(_causal_ws(ws_ref, g).astype(BF16), vn[:, sl]) + bs_ref[:, g:g + 1]
            y_ref[:, sl] = (u[:, sl] * mixed).astype(BF16)

    return pl.pallas_call(
        body, grid=(T // C,),
        in_specs=[pl.BlockSpec((C, F2), lambda i: (i, 0)), pl.BlockSpec((1, F), lambda i: (0, 0)),
                  pl.BlockSpec((G, C, C), lambda i: (0, 0, 0)), pl.BlockSpec((C, G), lambda i: (0, 0))],
        out_specs=pl.BlockSpec((C, F), lambda i: (i, 0)),
        out_shape=jax.ShapeDtypeStruct((T, F), BF16),
        compiler_params=_params("parallel"), name=name)(uvp, gain, ws, bst)


def _sgu_bwd(uvp, dy, gain, ws, bst, *, name):
    T, F2 = uvp.shape
    F = F2 // 2
    C, G, W = SGU_CHUNK, SGU_GROUPS, SGU_GROUP_W
    nc = T // C

    def body(uv_ref, dy_ref, g_ref, ws_ref, bs_ref, duv_ref, dg_ref, dws_ref, dbs_ref,
             dg_acc, dws_acc, dbs_acc):
        i = pl.program_id(0)

        @pl.when(i == 0)
        def _():
            dg_acc[...] = jnp.zeros_like(dg_acc)
            dws_acc[...] = jnp.zeros_like(dws_acc)
            dbs_acc[...] = jnp.zeros_like(dbs_acc)

        uvf = uv_ref[...].astype(F32)
        uv, dgelu = _gelu(uvf, with_grad=True)
        u, v = uv[:, :F], uv[:, F:]
        r = lax.rsqrt(jnp.mean(v * v, axis=-1, keepdims=True) + EPS)
        vhat = v * r
        gain_v = g_ref[...]
        vn = (vhat * gain_v).astype(BF16)
        dyv = dy_ref[...].astype(F32)
        lane8 = lax.broadcasted_iota(jnp.int32, (1, G), 1)
        dvn_parts = []
        dbs_new = jnp.zeros((C, G), F32)
        for g in range(G):
            sl = slice(g * W, (g + 1) * W)
            wsg = _causal_ws(ws_ref, g)
            mixed = _dot(wsg.astype(BF16), vn[:, sl]) + bs_ref[:, g:g + 1]
            duv_ref[:, sl] = (dyv[:, sl] * mixed * dgelu[:, sl]).astype(BF16)
            dmix = dyv[:, sl] * u[:, sl]
            dbs_new = dbs_new + jnp.where(lane8 == g, jnp.sum(dmix, axis=1, keepdims=True), 0.0)
            dmix_b = dmix.astype(BF16)
            dws_acc[g] += _dot_nt(dmix_b, vn[:, sl])
            dvn_parts.append(_dot(wsg.T.astype(BF16), dmix_b))
        dbs_acc[...] += dbs_new
        dvn = jnp.concatenate(dvn_parts, axis=1)
        dg_acc[...] += jnp.sum((dvn * vhat).reshape(C // SUBLANES, SUBLANES, F), axis=0)
        dvhat = dvn * gain_v
        dv = r * (dvhat - vhat * jnp.mean(dvhat * vhat, axis=-1, keepdims=True))
        duv_ref[:, F:] = (dv * dgelu[:, F:]).astype(BF16)

        @pl.when(i == nc - 1)
        def _():
            dg_ref[...] = jnp.sum(dg_acc[...], axis=0, keepdims=True)
            t = lax.broadcasted_iota(jnp.int32, (G, C, C), 1)
            s = lax.broadcasted_iota(jnp.int32, (G, C, C), 2)
            dws_ref[...] = jnp.where(s <= t, dws_acc[...], 0.0)
            dbs_ref[...] = dbs_acc[...]

    return pl.pallas_call(
        body, grid=(nc,),
        in_specs=[pl.BlockSpec((C, F2), lambda i: (i, 0)), pl.BlockSpec((C, F), lambda i: (i, 0)),
                  pl.BlockSpec((1, F), lambda i: (0, 0)), pl.BlockSpec((G, C, C), lambda i: (0, 0, 0)),
                  pl.BlockSpec((C, G), lambda i: (0, 0))],
        out_specs=[pl.BlockSpec((C, F2), lambda i: (i, 0)), pl.BlockSpec((1, F), lambda i: (0, 0)),
                   pl.BlockSpec((G, C, C), lambda i: (0, 0, 0)), pl.BlockSpec((C, G), lambda i: (0, 0))],
        out_shape=[jax.ShapeDtypeStruct((T, F2), BF16), jax.ShapeDtypeStruct((1, F), F32),
                   jax.ShapeDtypeStruct((G, C, C), F32), jax.ShapeDtypeStruct((C, G), F32)],
        scratch_shapes=[pltpu.VMEM((SUBLANES, F), F32), pltpu.VMEM((G, C, C), F32), pltpu.VMEM((C, G), F32)],
        compiler_params=_params("arbitrary"), name=name)(uvp, dy, gain, ws, bst)


def _my_place():
    return lax.axis_index("x"), lax.axis_index("y"), lax.axis_index("c")


def _all_gather(shards, *, name):
    nf = len(shards)
    items = [(lambda ins, f=f: ins[f], lambda outs, p, f=f: outs[f].at[:, p]) for f in range(nf)]
    targets = [jax.ShapeDtypeStruct((s.shape[0], N_DEV) + s.shape[1:], s.dtype) for s in shards]
    return _run_comm(_Comm(shards, targets, len(items), _gather_emit(items, [list(range(nf))])), name=name)


class _Comm:
    def __init__(self, sources, targets, n_items, emit):
        self.sources, self.targets, self.n_items, self.emit = list(sources), list(targets), n_items, emit
        self.filled = [t for t in self.targets if not isinstance(t, jax.ShapeDtypeStruct)]
        self.operands = self.sources + self.filled
        self.out_shapes = [jax.ShapeDtypeStruct(t.shape, t.dtype) for t in self.targets]
        self.sems = [pltpu.SemaphoreType.DMA((n_items, 7)), pltpu.SemaphoreType.DMA((n_items, 7)),
                     pltpu.SemaphoreType.DMA((n_items,))]

    def aliases(self, first_operand, first_result):
        pos = {id(t): k for k, t in enumerate(self.targets)}
        return {first_operand + len(self.sources) + a: first_result + pos[id(t)]
                for a, t in enumerate(self.filled)}


def _run_comm(comm, *, name):
    n_op, n_out = len(comm.operands), len(comm.targets)

    def body(*refs):
        comm.emit(0, 1, refs[:len(comm.sources)], refs[n_op:n_op + n_out], refs[n_op + n_out:])

    return pl.pallas_call(
        body, in_specs=[ANY] * n_op, out_specs=[ANY] * n_out, out_shape=comm.out_shapes,
        scratch_shapes=comm.sems, input_output_aliases=comm.aliases(0, 0), name=name)(*comm.operands)


def _at_steps(step, n_steps, phases):
    if n_steps == 1:
        for _, fn in phases:
            fn()
        return
    marks = {}
    for frac, fn in phases:
        marks.setdefault(min(int(frac * n_steps), n_steps - 1), []).append(fn)
    for mark, fns in sorted(marks.items()):
        @pl.when(step == mark)
        def _(fns=fns):
            for fn in fns:
                fn()


def _gather_emit(items, groups):
    def emit(step, n_steps, ins, outs, sems):
        send_sems, recv_sems, local_sems = sems
        x, y, c = _my_place()
        me, sibling = (x, y, c), (x, y, 1 - c)
        chips = [(1 - x, y), (x, 1 - y), (1 - x, 1 - y)]

        def copy(i, k, block, to, own=False):
            src_of, dst_of = items[i]
            dst = dst_of(outs, 4 * block[0] + 2 * block[1] + block[2])
            return pltpu.make_async_remote_copy(
                src_ref=src_of(ins) if own else dst, dst_ref=dst,
                send_sem=send_sems.at[i, k], recv_sem=recv_sems.at[i, k],
                device_id=to, device_id_type=MESH)

        def local(i):
            src_of, dst_of = items[i]
            return pltpu.make_async_copy(src_of(ins), dst_of(outs, 4 * x + 2 * y + c), local_sems.at[i])

        def first(i):
            return [copy(i, 0, me, sibling, own=True)] + [
                copy(i, 1 + j, me, (*chip, c), own=True) for j, chip in enumerate(chips)]

        def start():
            for i in range(len(items)):
                local(i).start()
                for cp in first(i):
                    cp.start()

        def forward(group):
            for j, chip in enumerate(chips):
                for i in group:
                    copy(i, 1 + j, (*chip, c), me).wait_recv()
                    copy(i, 4 + j, (*chip, c), sibling).start()

        def finish():
            for i in range(len(items)):
                copy(i, 0, sibling, me).wait_recv()
                for j, chip in enumerate(chips):
                    copy(i, 4 + j, (*chip, 1 - c), me).wait_recv()
            for i in range(len(items)):
                for cp in first(i) + [copy(i, 4 + j, (*chip, c), sibling) for j, chip in enumerate(chips)]:
                    cp.wait_send()
                local(i).wait()

        phases = [(0.0, start)]
        for g, group in enumerate(groups):
            phases.append(((g + 1) / len(groups), functools.partial(forward, group)))
        phases.append((1.0, finish))
        _at_steps(step, n_steps, phases)

    return emit


def _exchange_emit(items):
    def emit(step, n_steps, ins, outs, sems):
        send_sems, recv_sems, local_sems = sems
        x, y, c = _my_place()
        me = 4 * x + 2 * y + c

        def peer_of(k):
            return x ^ (k >> 2), y ^ ((k >> 1) & 1), c ^ (k & 1)

        def copy(i, k):
            src_of, dst_of = items[i]
            px, py, pc = peer_of(k)
            return pltpu.make_async_remote_copy(
                src_ref=src_of(ins, 4 * px + 2 * py + pc), dst_ref=dst_of(outs, me),
                send_sem=send_sems.at[i, k - 1], recv_sem=recv_sems.at[i, k - 1],
                device_id=(px, py, pc), device_id_type=MESH)

        def arrival(i, k):
            src_of, dst_of = items[i]
            px, py, pc = peer_of(k)
            peer = 4 * px + 2 * py + pc
            return pltpu.make_async_remote_copy(
                src_ref=src_of(ins, peer), dst_ref=dst_of(outs, peer),
                send_sem=send_sems.at[i, k - 1], recv_sem=recv_sems.at[i, k - 1],
                device_id=(x, y, c), device_id_type=MESH)

        def local(i):
            src_of, dst_of = items[i]
            return pltpu.make_async_copy(src_of(ins, me), dst_of(outs, me), local_sems.at[i])

        def start():
            for i in range(len(items)):
                local(i).start()
            for k in range(1, N_DEV):
                for i in range(len(items)):
                    copy(i, k).start()

        def finish():
            for k in range(1, N_DEV):
                for i in range(len(items)):
                    arrival(i, k).wait_recv()
            for k in range(1, N_DEV):
                for i in range(len(items)):
                    copy(i, k).wait_send()
            for i in range(len(items)):
                local(i).wait()

        _at_steps(step, n_steps, [(0.0, start), (1.0, finish)])

    return emit


def _adam_math(g, w, m, v):
    m = ADAM_B1 * m + (1.0 - ADAM_B1) * g
    v = ADAM_B2 * v + (1.0 - ADAM_B2) * (g * g)
    m_hat = m / (1.0 - ADAM_B1 ** ADAM_STEP)
    v_hat = v / (1.0 - ADAM_B2 ** ADAM_STEP)
    delta = -ADAM_LR * (m_hat / (jnp.sqrt(v_hat) + ADAM_EPS) + ADAM_WD * w)
    return delta, m, v


def _sum_adamw(parts, w, m, v, *, name, tr=256):
    L, nd, R, C = parts.shape
    tr = min(tr, R)
    assert R % tr == 0

    def body(p_ref, w_ref, m_ref, v_ref, g_ref, d_ref, nm_ref, nv_ref):
        g = p_ref[0, 0].astype(F32)
        for q in range(1, nd):
            g = g + p_ref[0, q].astype(F32)
        d, nm, nv = _adam_math(g, w_ref[0], m_ref[0], v_ref[0])
        g_ref[0] = g
        d_ref[0] = d
        nm_ref[0] = nm
        nv_ref[0] = nv

    blk = pl.BlockSpec((1, tr, C), lambda l, i: (l, i, 0))
    out = jax.ShapeDtypeStruct((L, R, C), F32)
    return pl.pallas_call(
        body, grid=(L, R // tr),
        in_specs=[pl.BlockSpec((1, nd, tr, C), lambda l, i: (l, 0, i, 0)), blk, blk, blk],
        out_specs=[blk] * 4, out_shape=[out] * 4,
        compiler_params=_params("parallel", "parallel"), name=name)(parts, w, m, v)


def _sum_parts(parts, *, name):
    nd, R, C = parts.shape

    def body(p_ref, o_ref):
        g = p_ref[0]
        for q in range(1, nd):
            g = g + p_ref[q]
        o_ref[...] = g

    return pl.pallas_call(
        body, out_shape=jax.ShapeDtypeStruct((R, C), F32),
        in_specs=[pl.BlockSpec(memory_space=pltpu.VMEM)],
        out_specs=pl.BlockSpec(memory_space=pltpu.VMEM), name=name)(parts)


def _adamw_small(g, w, m, v, *, name):
    def body(g_ref, w_ref, m_ref, v_ref, d_ref, nm_ref, nv_ref):
        d, nm, nv = _adam_math(g_ref[...], w_ref[...], m_ref[...], v_ref[...])
        d_ref[...] = d
        nm_ref[...] = nm
        nv_ref[...] = nv

    vm = pl.BlockSpec(memory_space=pltpu.VMEM)
    out = jax.ShapeDtypeStruct(g.shape, F32)
    return pl.pallas_call(body, out_shape=[out] * 3, in_specs=[vm] * 4, out_specs=[vm] * 3,
                          name=name)(g, w, m, v)


def kernel(x, norm_mix, norm_mlp, sb_wqkv, sb_wo, sgu_win, sgu_gain, sgu_ws, sgu_bs, sgu_wout, mlp_w1, mlp_w2, final_norm, loss_target, m_norm_mix, m_norm_mlp, m_sb_wqkv, m_sb_wo, m_sgu_win, m_sgu_gain, m_sgu_ws, m_sgu_bs, m_sgu_wout, m_mlp_w1, m_mlp_w2, m_final_norm, v_norm_mix, v_norm_mlp, v_sb_wqkv, v_sb_wo, v_sgu_win, v_sgu_gain, v_sgu_ws, v_sgu_bs, v_sgu_wout, v_mlp_w1, v_mlp_w2, v_final_norm):
    batch, seq, D = x.shape
    T = batch * seq
    x0 = x.reshape(T, D)
    target = loss_target.reshape(T, D)

    WQKV, WO, WIN, WOUT, W1, W2, GAIN = range(7)
    big = [sb_wqkv, sb_wo, sgu_win, sgu_wout, mlp_w1, mlp_w2]
    shards = [w.astype(BF16) for w in big] + [sgu_gain[:, None, :]]
    layer_weights = [[(WQKV, 0), (WO, 0), (W1, 0), (W2, 0)], [(WIN, 0), (WOUT, 0), (W1, 1), (W2, 1)],
                     [(WQKV, 1), (WO, 1), (W1, 2), (W2, 2)], [(WIN, 1), (WOUT, 1), (W1, 3), (W2, 3)]]

    def weight_item(f, l):
        return (lambda ins: ins[f].at[l], lambda outs, p: outs[f].at[l, p])

    first = [weight_item(f, l) for f, l in layer_weights[0]]
    first.append((lambda ins: ins[GAIN], lambda outs, p: outs[GAIN].at[:, p]))
    targets = [jax.ShapeDtypeStruct((s.shape[0], N_DEV) + s.shape[1:], s.dtype) for s in shards]
    gathered0 = _run_comm(_Comm(shards, targets, len(first), _gather_emit(first, [list(range(len(first)))])),
                          name="gather_layer0")
    gain_sgu = gathered0[GAIN].reshape(-1, 1, SGU_FFN)
    gw = dict(enumerate(gathered0[:GAIN]))

    def gather_layers(layers):
        fams = sorted({f for lw in layers for f, _ in layer_weights[lw]})
        items = [(lambda ins, f=f, l=l: ins[f].at[l], lambda outs, p, t=fams.index(f), l=l: outs[t].at[l, p])
                 for lw in layers for f, l in layer_weights[lw]]
        groups = [list(range(4 * g, 4 * g + 4)) for g in range(len(layers))]
        return _Comm(shards[:GAIN], [gw[f] for f in fams], len(items), _gather_emit(items, groups)), fams

    def weight(f):
        g = gw[f]
        return g if f in (WQKV, WIN, W1) else g.reshape(g.shape[0], 1, N_DEV * g.shape[2], g.shape[3])

    saved = []
    xs = x0
    for i in range(DEPTH):
        j = i // 2
        if i % 2 == 0:
            qkv, h = _mm_nn(xs, weight(WQKV), j, out_dtype=BF16, gain=norm_mix[i:i + 1], name=f"qkv{i}", tm=1024)
            comm, fams = gather_layers([1, 2] if i == 0 else [3])
            o, results = _sb_fwd(qkv, batch=batch, seq=seq, name=f"sb_fwd{i}", comm=comm)
            gw.update(zip(fams, results))
            x_mid = _mm_nn(o, weight(WO), j, out_dtype=F32, res=xs, name=f"wo{i}", tm=1024)
            mix = (h, qkv, o)
        else:
            gain_j = gain_sgu[j]
            bst = sgu_bs[j].T
            uvp, h = _mm_nn(xs, weight(WIN), j, out_dtype=BF16, gain=norm_mix[i:i + 1], name=f"win{i}", tm=1024)
            yv = _sgu_fwd(uvp, gain_j, sgu_ws[j], bst, name=f"sgu_fwd{i}")
            x_mid = _mm_nn(yv, weight(WOUT), j, out_dtype=F32, res=xs, name=f"wout{i}", tm=1024)
            mix = (h, uvp, yv, gain_j, bst)
        a, h2 = _mm_nn(x_mid, weight(W1), i, out_dtype=BF16, gain=norm_mlp[i:i + 1], name=f"w1_{i}", tm=1024)
        x_out = _mm_nn(a, weight(W2), i, out_dtype=F32, res=x_mid, a_act="relu2", name=f"w2_{i}")
        saved.append((xs, mix, x_mid, h2, a))
        xs = x_out
    g_wqkv, g_wo, g_win, g_wout, g_w1, g_w2 = [weight(f) for f in range(GAIN)]

    dx, dxb, sq, d_final = _loss_head(xs, final_norm.reshape(1, D), target, name="loss_head")
    loss = lax.psum(0.5 * jnp.sum(sq) / D, ("x", "y", "c"))

    SMALL = GAIN
    stacks = {f: jax.ShapeDtypeStruct((w.shape[0], N_DEV) + w.shape[1:], BF16) for f, w in enumerate(big)}
    pending = []

    def row_shards(p):
        return p.reshape(N_DEV, p.shape[1] // N_DEV, p.shape[2])

    def exchange_pending():
        fams = sorted({f for _, f, _ in pending})
        items = [(lambda ins, p, a=a: ins[a].at[p], lambda outs, q, t=fams.index(f), l=l: outs[t].at[l, q])
                 for a, (_, f, l) in enumerate(pending)]
        comm = _Comm([part for part, _, _ in pending], [stacks[f] for f in fams], len(items), _exchange_emit(items))
        pending.clear()
        return comm, fams

    d_norm_mix, d_norm_mlp = [None] * DEPTH, [None] * DEPTH
    d_gain, d_ws, d_bs = [None] * 2, [None] * 2, [None] * 2
    for i in reversed(range(DEPTH)):
        j = i // 2
        xs, mix, x_mid, h2, a = saved[i]
        da = _mm_nt(dxb, g_w2, i, out_dtype=BF16, act_src=a, name=f"d_a{i}")
        pending.append((row_shards(_mm_tn(a, dxb, shards=1, a_act="relu2", name=f"d_w2_{i}")), W2, i))
        pending.append((_mm_tn(h2, da, shards=N_DEV, name=f"d_w1_{i}", pb=4, tm=1024), W1, i))
        (dx, dxb, d_norm_mlp[i]), _ = _mm_nt_rms_bwd(da, g_w1, i, x_mid, norm_mlp[i:i + 1], dx, name=f"d_h2_{i}")
        if i % 2 == 0:
            h, qkv, o = mix
            do = _mm_nt(dxb, g_wo, j, out_dtype=BF16, name=f"d_o{i}", tm=1024)
            pending.append((row_shards(_mm_tn(o, dxb, shards=1, name=f"d_wo{i}")), WO, j))
            comm, fams = exchange_pending()
            (dq, dk, dv), results = _sb_bwd(qkv, o, do, batch=batch, seq=seq, name=f"sb_bwd{i}", comm=comm)
            stacks.update(zip(fams, results))
            dqkv = jnp.concatenate([dq, dk, dv], axis=1)
            pending.append((_mm_tn(h, dqkv, shards=N_DEV, name=f"d_wqkv{i}", pb=4, tm=1024), WQKV, j))
            mixer_in, w_in = dqkv, g_wqkv
        else:
            h, uvp, yv, gain_j, bst = mix
            dy = _mm_nt(dxb, g_wout, j, out_dtype=BF16, name=f"d_y{i}", tm=1024)
            pending.append((row_shards(_mm_tn(yv, dxb, shards=1, name=f"d_wout{i}")), WOUT, j))
            duv, d_gain[j], d_ws[j], dbst = _sgu_bwd(uvp, dy, gain_j, sgu_ws[j], bst, name=f"sgu_bwd{i}")
            d_bs[j] = dbst.T
            pending.append((_mm_tn(h, duv, shards=N_DEV, name=f"d_win{i}", pb=4, tm=1024), WIN, j))
            mixer_in, w_in = duv, g_win
        comm, fams = exchange_pending() if i == 0 else (None, [])
        (dx, dxb, d_norm_mix[i]), results = _mm_nt_rms_bwd(
            mixer_in, w_in, j, xs, norm_mix[i:i + 1], dx, name=f"d_h_mix{i}", comm=comm)
        stacks.update(zip(fams, results))
    grad_x = dx.reshape(batch, seq, D)

    small = [jnp.concatenate(d_norm_mix, 0), jnp.concatenate(d_norm_mlp, 0), d_final,
             jnp.concatenate(d_gain, 0), jnp.stack(d_bs, 0), jnp.stack(d_ws, 0)]
    small_flat = jnp.concatenate([s.reshape(-1) for s in small])
    n_small = small_flat.shape[0]
    small_rows = -(-n_small // (N_DEV * SUBLANES * LANES)) * SUBLANES
    small_flat = jnp.pad(small_flat, (0, N_DEV * small_rows * LANES - n_small))
    stacks[SMALL] = jax.ShapeDtypeStruct((1, N_DEV, small_rows, LANES), F32)
    pending.append((small_flat.reshape(N_DEV, small_rows, LANES), SMALL, 0))
    comm, fams = exchange_pending()
    stacks.update(zip(fams, _run_comm(comm, name="exchange_last")))
    r_wqkv, r_wo, r_win, r_wout, r_w1, r_w2, r_small = [stacks[f] for f in range(SMALL + 1)]

    u_wqkv = _sum_adamw(r_wqkv, sb_wqkv, m_sb_wqkv, v_sb_wqkv, name="adamw_wqkv")
    u_wo = _sum_adamw(r_wo, sb_wo, m_sb_wo, v_sb_wo, name="adamw_wo")
    u_win = _sum_adamw(r_win, sgu_win, m_sgu_win, v_sgu_win, name="adamw_win")
    u_wout = _sum_adamw(r_wout, sgu_wout, m_sgu_wout, v_sgu_wout, name="adamw_wout")
    u_w1 = _sum_adamw(r_w1, mlp_w1, m_mlp_w1, v_mlp_w1, name="adamw_w1")
    u_w2 = _sum_adamw(r_w2, mlp_w2, m_mlp_w2, v_mlp_w2, name="adamw_w2")

    small_sum = _sum_parts(r_small[0], name="sum_small")
    g_small = _all_gather([small_sum[None]], name="gather_small")[0].reshape(-1)[:n_small]

    shapes = [s.shape for s in small]
    sizes = [s.size for s in small]
    offs = [sum(sizes[:k]) for k in range(len(sizes))]
    me = 4 * lax.axis_index("x") + 2 * lax.axis_index("y") + lax.axis_index("c")
    shard_w = SGU_FFN // N_DEV

    def pack(arrs):
        flat = jnp.concatenate([a_.reshape(-1) for a_ in arrs])
        return jnp.pad(flat, (0, N_DEV * small_rows * LANES - n_small)).reshape(-1, LANES)

    def full_gain(gshard):
        return lax.dynamic_update_slice(jnp.zeros((2, SGU_FFN), F32), gshard, (0, me * shard_w))

    w_small = pack([norm_mix, norm_mlp, final_norm, full_gain(sgu_gain), sgu_bs, sgu_ws])
    m_small = pack([m_norm_mix, m_norm_mlp, m_final_norm, full_gain(m_sgu_gain), m_sgu_bs, m_sgu_ws])
    v_small = pack([v_norm_mix, v_norm_mlp, v_final_norm, full_gain(v_sgu_gain), v_sgu_bs, v_sgu_ws])
    g_pack = jnp.pad(g_small, (0, N_DEV * small_rows * LANES - n_small)).reshape(-1, LANES)
    sm = [g_pack] + list(_adamw_small(g_pack, w_small, m_small, v_small, name="adamw_small"))

    def unpack(flat2d):
        flat = flat2d.reshape(-1)
        out = [flat[offs[k]:offs[k] + sizes[k]].reshape(shapes[k]) for k in range(len(sizes))]
        out[2] = out[2].reshape(D)
        out[3] = lax.dynamic_slice(out[3], (0, me * shard_w), (2, shard_w))
        return out

    outs = []
    for k, big_u in enumerate(zip(u_wqkv, u_wo, u_win, u_wout, u_w1, u_w2)):
        s_nm, s_nl, s_fn, s_gain, s_bs, s_ws = unpack(sm[k])
        b_wqkv, b_wo, b_win, b_wout, b_w1, b_w2 = big_u
        outs += [s_nm, s_nl, b_wqkv, b_wo, b_win, s_gain, s_ws, s_bs, b_wout, b_w1, b_w2, s_fn]
    return (loss, grad_x, *outs)
```

```python
import functools

import jax
import jax.numpy as jnp
from jax import lax
from jax.experimental import pallas as pl
from jax.experimental.pallas import tpu as pltpu

F32 = jnp.float32
BF16 = jnp.bfloat16

N_DEV = 8
D_MODEL = 1024
SEQ = 2048
DEPTH = 4
SB_HEAD_DIM = 64
SGU_CHUNK = 128
SGU_FFN = 2 * D_MODEL
SGU_GROUPS = 8
SGU_GROUP_W = SGU_FFN // SGU_GROUPS
EPS = 1e-6

ADAM_LR = 0.001
ADAM_B1 = 0.9
ADAM_B2 = 0.999
ADAM_EPS = 1e-08
ADAM_WD = 0.01
ADAM_STEP = 10

LANES = 128
SUBLANES = 8
VMEM_LIMIT = 56 * 1024 * 1024
EXP_ZERO_BELOW = -104.0
SB_TILE = 256
SB_STRAIGHT = 2
SB_PAIRS = 2

MESH = pl.DeviceIdType.MESH
ANY = pl.BlockSpec(memory_space=pl.ANY)


def _params(*sem):
    return pltpu.CompilerParams(dimension_semantics=sem, vmem_limit_bytes=VMEM_LIMIT)


def _dot(a, b):
    return jnp.dot(a, b, preferred_element_type=F32)


def _dot_nt(a, b):
    return lax.dot_general(a, b, (((1,), (1,)), ((), ())), preferred_element_type=F32)


def _dot_tn(a, b):
    return lax.dot_general(a, b, (((0,), (0,)), ((), ())), preferred_element_type=F32)


def _split_bf16(x):
    hi = x.astype(BF16)
    lo = (x - hi.astype(F32)).astype(BF16)
    return hi, lo


def _relu2(av):
    t = jnp.maximum(av, jnp.zeros_like(av))
    return t * t


def _mm_nn(a, w, l, *, out_dtype, name, res=None, a_act=None, gain=None, gelu=False, comm=None,
           tm=512, kc=1024):
    M, K = a.shape
    _, P, K2, n = w.shape
    assert K2 == K
    tm, kc = min(tm, M), min(kc, K)
    assert M % tm == 0 and K % kc == 0

    n_in = 2 + (gain is not None) + (res is not None)
    n_out = 1 + (gain is not None) + (gelu is True)

    def body(*all_refs):
        ins, outs, _ = carried.split(all_refs, n_in, n_out, 0)
        carried.emit(all_refs)
        refs = list(ins) + list(outs)
        a_ref, w_ref, o_ref = refs[0], refs[1], refs[n_in]
        if gain is not None:
            xv = a_ref[...]
            r = lax.rsqrt(jnp.mean(xv * xv, axis=-1, keepdims=True) + EPS)
            h = (xv * r * refs[2][...]).astype(BF16)
            refs[n_in + 1][...] = h
        for p in range(P):
            sl = slice(p * n, (p + 1) * n)
            acc = None
            for k0 in range(0, K, kc):
                if gain is not None:
                    av = h[:, k0:k0 + kc]
                else:
                    av = a_ref[:, k0:k0 + kc]
                    av = _relu2(av) if a_act == "relu2" else av.astype(BF16)
                d = _dot(av, w_ref[0, p, k0:k0 + kc, :])
                acc = d if acc is None else acc + d
            if res is not None:
                acc = acc + refs[n_in - 1][:, sl]
            if gelu:
                acc, dact = _gelu(acc)
                refs[-1][:, sl] = dact.astype(BF16)
            o_ref[:, sl] = acc.astype(out_dtype)

    row = lambda width: pl.BlockSpec((tm, width), lambda i: (i, 0))
    in_specs = [row(K), pl.BlockSpec((1, P, K, n), lambda i: (l, 0, 0, 0), pipeline_mode=pl.Buffered(1))]
    args = [a, w]
    if gain is not None:
        in_specs.append(pl.BlockSpec((1, K), lambda i: (0, 0)))
        args.append(gain)
    if res is not None:
        in_specs.append(row(P * n))
        args.append(res)
    out_specs, out_shape = [row(P * n)], [jax.ShapeDtypeStruct((M, P * n), out_dtype)]
    if gain is not None:
        out_specs.append(row(K))
        out_shape.append(jax.ShapeDtypeStruct((M, K), BF16))
    if gelu:
        out_specs.append(row(P * n))
        out_shape.append(jax.ShapeDtypeStruct((M, P * n), BF16))
    carried = _Carried(comm, (M // tm,), last="parallel")
    outs, comm_results = carried.pallas_call(
        body, in_specs=in_specs, out_specs=out_specs, out_shape=out_shape, scratch_shapes=[],
        args=args, name=name)
    outs = tuple(outs) if n_out > 1 else outs[0]
    return outs if comm is None else (outs, comm_results)


def _mm_nt(a, w, l, *, out_dtype, name, act_src=None, tm=512, tn=1024):
    M, K = a.shape
    _, P, Nout, kc = w.shape
    assert P == 1 and K == kc
    tm, tn = min(tm, M), min(tn, Nout)
    assert M % tm == 0 and Nout % tn == 0

    def body(*refs):
        a_ref, w_ref, o_ref = refs[0], refs[1], refs[-1]
        av = a_ref[...].astype(BF16)
        for n0 in range(0, Nout, tn):
            r = _dot_nt(av, w_ref[0, 0, n0:n0 + tn, :])
            if act_src is not None:
                r = r * (2.0 * jnp.maximum(refs[2][:, n0:n0 + tn].astype(F32), 0.0))
            o_ref[:, n0:n0 + tn] = r.astype(out_dtype)

    row = lambda width: pl.BlockSpec((tm, width), lambda i: (i, 0))
    in_specs = [row(K), pl.BlockSpec((1, 1, Nout, K), lambda i: (l, 0, 0, 0), pipeline_mode=pl.Buffered(1))]
    args = [a, w]
    if act_src is not None:
        in_specs.append(row(Nout))
        args.append(act_src)
    return pl.pallas_call(
        body, grid=(M // tm,), in_specs=in_specs, out_specs=row(Nout),
        out_shape=jax.ShapeDtypeStruct((M, Nout), out_dtype),
        compiler_params=_params("parallel"), name=name)(*args)


def _mm_tn(a, b, *, shards, name, a_act=None, tm=2048, tk=1024, pb=1):
    M, K = a.shape
    M2, N = b.shape
    assert M2 == M
    n = N // shards
    tm, tk = min(tm, M), min(tk, K)
    assert M % tm == 0 and K % tk == 0 and shards % pb == 0
    width, nm = pb * n, M // tm

    def body(a_ref, b_ref, o_ref, acc):
        m = pl.program_id(2)
        av = a_ref[...]
        if a_act == "relu2":
            av = _relu2(av)

        @pl.when(m == 0)
        def _():
            acc[...] = jnp.zeros_like(acc)

        acc[...] += _dot_tn(av.astype(BF16), b_ref[...].astype(BF16))

        @pl.when(m == nm - 1)
        def _():
            for p in range(pb):
                o_ref[p] = acc[:, p * n:(p + 1) * n].astype(BF16)

    return pl.pallas_call(
        body, grid=(K // tk, N // width, nm),
        in_specs=[pl.BlockSpec((tm, tk), lambda i, j, m: (m, i)),
                  pl.BlockSpec((tm, width), lambda i, j, m: (m, j))],
        out_specs=pl.BlockSpec((pb, tk, n), lambda i, j, m: (j, i, 0)),
        out_shape=jax.ShapeDtypeStruct((shards, K, n), BF16),
        scratch_shapes=[pltpu.VMEM((tk, width), F32)],
        compiler_params=_params("parallel", "parallel", "arbitrary"), name=name)(a, b)


def _mm_nt_rms_bwd(a, w, l, x, gain, dres, *, name, comm=None, tm=512):
    M, K = a.shape
    _, P, D, kc = w.shape
    assert K == P * kc and x.shape == (M, D)
    tm = min(tm, M)
    nr = M // tm

    def body(*refs):
        (a_ref, w_ref, x_ref, g_ref, dres_ref), (dx_ref, dxb_ref, dg_ref), (acc,) = carried.split(refs, 5, 3, 1)
        i = pl.program_id(0)
        carried.emit(refs)
        dhv = None
        for p in range(P):
            d = _dot_nt(a_ref[:, p * kc:(p + 1) * kc], w_ref[0, p])
            dhv = d if dhv is None else dhv + d
        xv = x_ref[...]
        r = lax.rsqrt(jnp.mean(xv * xv, axis=-1, keepdims=True) + EPS)
        xhat = xv * r
        dxhat = dhv * g_ref[...]
        dx = dres_ref[...] + r * (dxhat - xhat * jnp.mean(dxhat * xhat, axis=-1, keepdims=True))
        dx_ref[...] = dx
        dxb_ref[...] = dx.astype(BF16)
        part = jnp.sum((dhv * xhat).reshape(tm // SUBLANES, SUBLANES, D), axis=0)

        @pl.when(i == 0)
        def _():
            acc[...] = jnp.zeros_like(acc)

        acc[...] += part

        @pl.when(i == nr - 1)
        def _():
            dg_ref[...] = jnp.sum(acc[...], axis=0, keepdims=True)

    row = pl.BlockSpec((tm, D), lambda i: (i, 0))
    vec = pl.BlockSpec((1, D), lambda i: (0, 0))
    carried = _Carried(comm, (nr,))
    return carried.pallas_call(
        body,
        in_specs=[pl.BlockSpec((tm, K), lambda i: (i, 0)),
                  pl.BlockSpec((1, P, D, kc), lambda i: (l, 0, 0, 0), pipeline_mode=pl.Buffered(1)),
                  row, vec, row],
        out_specs=[row, row, vec],
        out_shape=[jax.ShapeDtypeStruct((M, D), F32), jax.ShapeDtypeStruct((M, D), BF16),
                   jax.ShapeDtypeStruct((1, D), F32)],
        scratch_shapes=[pltpu.VMEM((SUBLANES, D), F32)],
        args=[a, w, x, gain, dres], name=name)


def _loss_head(x, gain, target, *, name, tr=512):
    T, D = x.shape
    nr = T // tr

    def body(x_ref, g_ref, t_ref, dx_ref, dxb_ref, sq_ref, dg_ref, sq_acc, dg_acc):
        i = pl.program_id(0)
        xv = x_ref[...]
        g = g_ref[...]
        r = lax.rsqrt(jnp.mean(xv * xv, axis=-1, keepdims=True) + EPS)
        xhat = xv * r
        err = xhat * g - t_ref[...]
        dy = err * (1.0 / D)
        dxhat = dy * g
        dx = r * (dxhat - xhat * jnp.mean(dxhat * xhat, axis=-1, keepdims=True))
        dx_ref[...] = dx
        dxb_ref[...] = dx.astype(BF16)
        sq = jnp.sum((err * err).reshape(tr // SUBLANES, SUBLANES, D), axis=0)
        dg = jnp.sum((dy * xhat).reshape(tr // SUBLANES, SUBLANES, D), axis=0)

        @pl.when(i == 0)
        def _():
            sq_acc[...] = sq
            dg_acc[...] = dg

        @pl.when(i > 0)
        def _():
            sq_acc[...] += sq
            dg_acc[...] += dg

        @pl.when(i == nr - 1)
        def _():
            sq_ref[...] = sq_acc[...]
            dg_ref[...] = jnp.sum(dg_acc[...], axis=0, keepdims=True)

    row = pl.BlockSpec((tr, D), lambda i: (i, 0))
    vec = pl.BlockSpec((1, D), lambda i: (0, 0))
    part = pl.BlockSpec((SUBLANES, D), lambda i: (0, 0))
    return pl.pallas_call(
        body, grid=(nr,), in_specs=[row, vec, row], out_specs=[row, row, part, vec],
        out_shape=[jax.ShapeDtypeStruct((T, D), F32), jax.ShapeDtypeStruct((T, D), BF16),
                   jax.ShapeDtypeStruct((SUBLANES, D), F32), jax.ShapeDtypeStruct((1, D), F32)],
        scratch_shapes=[pltpu.VMEM((SUBLANES, D), F32), pltpu.VMEM((SUBLANES, D), F32)],
        compiler_params=_params("arbitrary"), name=name)(x, gain, target)


def _head0_lanes():
    return lax.broadcasted_iota(jnp.int32, (1, LANES), 1) < SB_HEAD_DIM


def _stack_heads(x):
    zero = jnp.zeros_like(x)
    h0 = _head0_lanes()
    return jnp.concatenate([jnp.where(h0, x, zero), jnp.where(h0, zero, x)], axis=0)


def _unstack_heads(y, tq):
    return jnp.where(_head0_lanes(), y[:tq], y[tq:])


def _past_mask(tq, tk):
    row = lax.broadcasted_iota(jnp.int32, (2 * tq, tk), 0) & (tq - 1)
    col = lax.broadcasted_iota(jnp.int32, (2 * tq, tk), 1)
    return col < row


def _max_of(arrays):
    return functools.reduce(jnp.maximum, [jnp.max(a) for a in arrays])


def _sb_logs(z, past):
    minus_abs = lax.bitcast_convert_type(
        lax.bitcast_convert_type(z, jnp.uint32) | jnp.uint32(0x80000000), F32)
    log_beta = jnp.minimum(z, 0.0) - jnp.log(1.0 + jnp.exp(minus_abs))
    l = log_beta - z
    if past is not None:
        l = jnp.where(past, l, 0.0)
    return log_beta, l


def _suffix_matrix(tk):
    j = lax.broadcasted_iota(jnp.int32, (2 * tk, tk), 0) & (tk - 1)
    s = lax.broadcasted_iota(jnp.int32, (2 * tk, tk), 1)
    return (j > s).astype(BF16)


def _suffix_sum(x, u2, exact=True):
    if not exact:
        return _dot(x.astype(BF16), u2[:x.shape[1]])
    hi, lo = _split_bf16(x)
    return _dot(jnp.concatenate([hi, lo], axis=1), u2)


class _Carried:
    def __init__(self, comm, grid, last="arbitrary"):
        self.comm, self.grid, self.last = comm, grid, last
        self.n_op = len(comm.operands) if comm else 0
        self.n_tgt = len(comm.targets) if comm else 0

    def split(self, refs, n_in, n_out, n_scratch):
        self.n_in, self.n_out, self.n_scratch = n_in, n_out, n_scratch
        a = n_in + self.n_op
        b = a + n_out + self.n_tgt
        return refs[:n_in], refs[a:a + n_out], refs[b:b + n_scratch]

    def emit(self, refs):
        if self.comm is None:
            return
        step, n_steps = 0, 1
        for d, size in enumerate(self.grid):
            step = step * size + pl.program_id(d)
            n_steps *= size
        a = self.n_in + self.n_op + self.n_out
        self.comm.emit(step, n_steps, refs[self.n_in:self.n_in + len(self.comm.sources)],
                       refs[a:a + self.n_tgt], refs[a + self.n_tgt + self.n_scratch:])

    def pallas_call(self, body, *, in_specs, out_specs, out_shape, scratch_shapes, args, name):
        comm = self.comm
        n_in, n_out = len(in_specs), len(out_specs)
        aliases = {}
        if comm is not None:
            in_specs = in_specs + [ANY] * self.n_op
            out_specs = out_specs + [ANY] * self.n_tgt
            out_shape = out_shape + comm.out_shapes
            scratch_shapes = scratch_shapes + comm.sems
            args = args + comm.operands
            aliases = comm.aliases(n_in, n_out)
        sem = ("parallel",) * (len(self.grid) - 1) + (self.last,) if comm is None else ("arbitrary",) * len(self.grid)
        results = pl.pallas_call(
            body, grid=self.grid, in_specs=in_specs, out_specs=out_specs, out_shape=out_shape,
            scratch_shapes=scratch_shapes, input_output_aliases=aliases,
            compiler_params=_params(*sem), name=name)(*args)
        return results[:n_out], results[n_out:]


def _sb_fwd(qkv, *, batch, seq, name, comm=None, tq=SB_TILE, n_pre=SB_STRAIGHT, pairs=SB_PAIRS):
    T, D3 = qkv.shape
    D = D3 // 3
    nhp = D // LANES
    tk = tq
    nq = seq // tq
    scale = SB_HEAD_DIM ** -0.5
    assert 1 <= n_pre <= nq and nhp % pairs == 0

    def body(*refs):
        (q_ref, k_ref, v_ref), (o_ref,), (acc,) = carried.split(refs, 3, 1, 1)
        qi = pl.program_id(2)
        carried.emit(refs)
        qs = [_stack_heads(q_ref[:, lanes]) * scale for lanes in cols]
        past = _past_mask(tq, tk)
        u = _suffix_matrix(tk)

        def block(kb, g, c, diag):
            ks = pl.multiple_of(kb * tk, tk)
            z = _dot_nt(qs[g], k_ref[pl.ds(ks, tk), cols[g]])
            log_beta, l = _sb_logs(z, past if diag else None)
            arg = log_beta + _suffix_sum(l, u, exact=False)
            a = jnp.exp(arg if c is None else arg + c)
            if diag:
                a = jnp.where(past, a, 0.0)
            return _dot(a.astype(BF16), v_ref[pl.ds(ks, tk), cols[g]]), jnp.sum(l, axis=1, keepdims=True)

        def straight(n):
            o_sum, c = [None] * pairs, [None] * pairs
            for b in range(n):
                for g in range(pairs):
                    o_b, c_b = block(qi - b, g, c[g], b == 0)
                    o_sum[g] = o_b if b == 0 else o_sum[g] + o_b
                    c[g] = c_b if b == 0 else c[g] + c_b
            return o_sum, c

        def finish(o_sum):
            for g in range(pairs):
                o_ref[:, cols[g]] = _unstack_heads(o_sum[g], tq)

        for n in range(1, n_pre):
            @pl.when(qi == n - 1)
            def _(n=n):
                finish(straight(n)[0])

        @pl.when(qi >= n_pre - 1)
        def _():
            o_sum, c = straight(n_pre)
            for g in range(pairs):
                acc[g] = o_sum[g]

            def cond(st):
                kb, c = st
                return jnp.logical_and(kb >= 0, _max_of(c) > EXP_ZERO_BELOW)

            def step(st):
                kb, c = st
                new_c = []
                for g in range(pairs):
                    o_n, c_n = block(kb, g, c[g], False)
                    acc[g] += o_n
                    new_c.append(c[g] + c_n)
                return kb - 1, tuple(new_c)

            lax.while_loop(cond, step, (qi - n_pre, tuple(c)))
            finish([acc[g] for g in range(pairs)])

    cols = [slice(g * LANES, (g + 1) * LANES) for g in range(pairs)]
    width = pairs * LANES
    carried = _Carried(comm, (batch, nhp // pairs, nq))
    (o,), comm_results = carried.pallas_call(
        body,
        in_specs=[pl.BlockSpec((tq, width), lambda b, p, i: (b * nq + i, p)),
                  pl.BlockSpec((seq, width), lambda b, p, i: (b, nhp // pairs + p)),
                  pl.BlockSpec((seq, width), lambda b, p, i: (b, 2 * (nhp // pairs) + p))],
        out_specs=[pl.BlockSpec((tq, width), lambda b, p, i: (b * nq + i, p))],
        out_shape=[jax.ShapeDtypeStruct((T, D), F32)],
        scratch_shapes=[pltpu.VMEM((pairs, 2 * tq, LANES), F32)],
        args=[qkv, qkv, qkv], name=name)
    return o, comm_results


def _sb_bwd(qkv, o, do, *, batch, seq, name, comm=None, tq=SB_TILE, n_pre=SB_STRAIGHT, pairs=SB_PAIRS):
    T, D3 = qkv.shape
    D = D3 // 3
    nhp = D // LANES
    tk = tq
    nq = seq // tq
    scale = SB_HEAD_DIM ** -0.5

    def body(*refs):
        ins, outs, scratch = carried.split(refs, 5, 3, 3)
        q_ref, k_ref, v_ref, o_ref, do_ref = ins
        dq_ref, dk_ref, dv_ref = outs
        dq_acc, dk_acc, dv_acc = scratch
        qi = pl.program_id(2)
        carried.emit(refs)

        @pl.when(qi == 0)
        def _():
            dk_acc[...] = jnp.zeros_like(dk_acc)
            dv_acc[...] = jnp.zeros_like(dv_acc)

        qs = [_stack_heads(q_ref[:, lanes]) * scale for lanes in cols]
        dos = [_stack_heads(do_ref[:, lanes]) for lanes in cols]
        dsum = [jnp.sum(_stack_heads(do_ref[:, lanes].astype(F32) * o_ref[:, lanes]), axis=1, keepdims=True)
                for lanes in cols]
        past = _past_mask(tq, tk)
        u = _suffix_matrix(tk)

        def block(kb, p, c, gc, diag):
            ks = pl.multiple_of(kb * tk, tk)
            kblk = k_ref[pl.ds(ks, tk), cols[p]]
            vblk = v_ref[pl.ds(ks, tk), cols[p]]
            z = _dot_nt(qs[p], kblk)
            log_beta, l = _sb_logs(z, past if diag else None)
            arg = log_beta + _suffix_sum(l, u, exact=False)
            a = jnp.exp(arg if c is None else arg + c)
            if diag:
                a = jnp.where(past, a, 0.0)
            a = a.astype(BF16)
            beta = 1.0 - jnp.exp(l)
            g = a.astype(F32) * _dot_nt(dos[p], vblk)
            gs = _suffix_sum(g, u)
            dz = g - beta * (dsum[p] - (gs if gc is None else gs + gc))
            if diag:
                dz = jnp.where(past, dz, 0.0)
            dzb = dz.astype(BF16)
            dk_acc[pl.ds(ks, tk), cols[p]] += _dot_tn(dzb, qs[p])
            dv_acc[pl.ds(ks, tk), cols[p]] += _dot_tn(a, dos[p])
            return (_dot(dzb, kblk), jnp.sum(l, axis=1, keepdims=True),
                    jnp.sum(g, axis=1, keepdims=True))

        def finish(dq_sum):
            for p in range(pairs):
                dq_ref[:, cols[p]] = (_unstack_heads(dq_sum[p], tq) * scale).astype(BF16)

        def straight(n):
            dq_sum, c, gc = [None] * pairs, [None] * pairs, [None] * pairs
            for b in range(n):
                for p in range(pairs):
                    dq_b, c_b, g_b = block(qi - b, p, c[p], gc[p], b == 0)
                    dq_sum[p] = dq_b if b == 0 else dq_sum[p] + dq_b
                    c[p] = c_b if b == 0 else c[p] + c_b
                    gc[p] = g_b if b == 0 else gc[p] + g_b
            return dq_sum, c, gc

        for n in range(1, n_pre):
            @pl.when(qi == n - 1)
            def _(n=n):
                finish(straight(n)[0])

        @pl.when(qi >= n_pre - 1)
        def _():
            dq_sum, c, gc = straight(n_pre)
            for p in range(pairs):
                dq_acc[p] = dq_sum[p]

            def cond(st):
                kb, c, gc = st
                return jnp.logical_and(kb >= 0, _max_of(c) > EXP_ZERO_BELOW)

            def step(st):
                kb, c, gc = st
                new_c, new_gc = [], []
                for p in range(pairs):
                    dq_n, c_n, g_n = block(kb, p, c[p], gc[p], False)
                    dq_acc[p] += dq_n
                    new_c.append(c[p] + c_n)
                    new_gc.append(gc[p] + g_n)
                return kb - 1, tuple(new_c), tuple(new_gc)

            lax.while_loop(cond, step, (qi - n_pre, tuple(c), tuple(gc)))
            finish([dq_acc[p] for p in range(pairs)])

        @pl.when(qi == nq - 1)
        def _():
            dk_ref[...] = dk_acc[...].astype(BF16)
            dv_ref[...] = dv_acc[...].astype(BF16)

    cols = [slice(p * LANES, (p + 1) * LANES) for p in range(pairs)]
    width, ncb = pairs * LANES, nhp // pairs
    qspec = pl.BlockSpec((tq, width), lambda b, p, i: (b * nq + i, p))
    sspec = pl.BlockSpec((seq, width), lambda b, p, i: (b, p))
    out = jax.ShapeDtypeStruct((T, D), BF16)
    carried = _Carried(comm, (batch, ncb, nq))
    return carried.pallas_call(
        body,
        in_specs=[qspec,
                  pl.BlockSpec((seq, width), lambda b, p, i: (b, ncb + p)),
                  pl.BlockSpec((seq, width), lambda b, p, i: (b, 2 * ncb + p)),
                  qspec, qspec],
        out_specs=[qspec, sspec, sspec], out_shape=[out, out, out],
        scratch_shapes=[pltpu.VMEM((pairs, 2 * tq, LANES), F32), pltpu.VMEM((seq, width), F32),
                        pltpu.VMEM((seq, width), F32)],
        args=[qkv, qkv, qkv, o, do], name=name)


_GELU_C = 0.7978845608028654
_GELU_A = 0.044715


def _gelu(x):
    xx = x * x
    t = jnp.tanh(x * (_GELU_C + (_GELU_C * _GELU_A) * xx))
    hx = 0.5 * x
    grad = (0.5 + 0.5 * t) + (hx * (1.0 - t * t)) * (_GELU_C + (3.0 * _GELU_C * _GELU_A) * xx)
    return hx + hx * t, grad


def _causal_ws(ws_ref, g):
    t = lax.broadcasted_iota(jnp.int32, (SGU_CHUNK, SGU_CHUNK), 0)
    s = lax.broadcasted_iota(jnp.int32, (SGU_CHUNK, SGU_CHUNK), 1)
    return jnp.where(s <= t, ws_ref[g], 0.0)


def _sgu_fwd(uv, gain, ws, bst, *, name):
    T, F2 = uv.shape
    F = F2 // 2
    C, G, W = SGU_CHUNK, SGU_GROUPS, SGU_GROUP_W

    def body(uv_ref, g_ref, ws_ref, bs_ref, y_ref):
        u, v = uv_ref[:, :F].astype(F32), uv_ref[:, F:].astype(F32)
        r = lax.rsqrt(jnp.mean(v * v, axis=-1, keepdims=True) + EPS)
        vn = (v * r * g_ref[...]).astype(BF16)
        for g in range(G):
            sl = slice(g * W, (g + 1) * W)
            mixed = _dot(_causal_ws(ws_ref, g).astype(BF16), vn[:, sl]) + bs_ref[:, g:g + 1]
            y_ref[:, sl] = (u[:, sl] * mixed).astype(BF16)

    return pl.pallas_call(
        body, grid=(T // C,),
        in_specs=[pl.BlockSpec((C, F2), lambda i: (i, 0)), pl.BlockSpec((1, F), lambda i: (0, 0)),
                  pl.BlockSpec((G, C, C), lambda i: (0, 0, 0)), pl.BlockSpec((C, G), lambda i: (0, 0))],
        out_specs=pl.BlockSpec((C, F), lambda i: (i, 0)),
        out_shape=jax.ShapeDtypeStruct((T, F), BF16),
        compiler_params=_params("parallel"), name=name)(uv, gain, ws, bst)


def _sgu_bwd(uv, dgelu, dy, gain, ws, bst, *, name):
    T, F2 = uv.shape
    F = F2 // 2
    C, G, W = SGU_CHUNK, SGU_GROUPS, SGU_GROUP_W
    nc = T // C

    def body(uv_ref, dgelu_ref, dy_ref, g_ref, ws_ref, bs_ref, duv_ref, dg_ref, dws_ref, dbs_ref,
             dg_acc, dws_acc, dbs_acc):
        i = pl.program_id(0)

        @pl.when(i == 0)
        def _():
            dg_acc[...] = jnp.zeros_like(dg_acc)
            dws_acc[...] = jnp.zeros_like(dws_acc)
            dbs_acc[...] = jnp.zeros_like(dbs_acc)

        u, v = uv_ref[:, :F].astype(F32), uv_ref[:, F:].astype(F32)
        dgelu = dgelu_ref[...].astype(F32)
        r = lax.rsqrt(jnp.mean(v * v, axis=-1, keepdims=True) + EPS)
        vhat = v * r
        gain_v = g_ref[...]
        vn = (vhat * gain_v).astype(BF16)
        dyv = dy_ref[...].astype(F32)
        lane8 = lax.broadcasted_iota(jnp.int32, (1, G), 1)
        dvn_parts = []
        dbs_new = jnp.zeros((C, G), F32)
        for g in range(G):
            sl = slice(g * W, (g + 1) * W)
            wsg = _causal_ws(ws_ref, g)
            mixed = _dot(wsg.astype(BF16), vn[:, sl]) + bs_ref[:, g:g + 1]
            duv_ref[:, sl] = (dyv[:, sl] * mixed * dgelu[:, sl]).astype(BF16)
            dmix = dyv[:, sl] * u[:, sl]
            dbs_new = dbs_new + jnp.where(lane8 == g, jnp.sum(dmix, axis=1, keepdims=True), 0.0)
            dmix_b = dmix.astype(BF16)
            dws_acc[g] += _dot_nt(dmix_b, vn[:, sl])
            dvn_parts.append(_dot(wsg.T.astype(BF16), dmix_b))
        dbs_acc[...] += dbs_new
        dvn = jnp.concatenate(dvn_parts, axis=1)
        dg_acc[...] += jnp.sum((dvn * vhat).reshape(C // SUBLANES, SUBLANES, F), axis=0)
        dvhat = dvn * gain_v
        dv = r * (dvhat - vhat * jnp.mean(dvhat * vhat, axis=-1, keepdims=True))
        duv_ref[:, F:] = (dv * dgelu[:, F:]).astype(BF16)

        @pl.when(i == nc - 1)
        def _():
            dg_ref[...] = jnp.sum(dg_acc[...], axis=0, keepdims=True)
            t = lax.broadcasted_iota(jnp.int32, (G, C, C), 1)
            s = lax.broadcasted_iota(jnp.int32, (G, C, C), 2)
            dws_ref[...] = jnp.where(s <= t, dws_acc[...], 0.0)
            dbs_ref[...] = dbs_acc[...]

    return pl.pallas_call(
        body, grid=(nc,),
        in_specs=[pl.BlockSpec((C, F2), lambda i: (i, 0)), pl.BlockSpec((C, F2), lambda i: (i, 0)),
                  pl.BlockSpec((C, F), lambda i: (i, 0)),
                  pl.BlockSpec((1, F), lambda i: (0, 0)), pl.BlockSpec((G, C, C), lambda i: (0, 0, 0)),
                  pl.BlockSpec((C, G), lambda i: (0, 0))],
        out_specs=[pl.BlockSpec((C, F2), lambda i: (i, 0)), pl.BlockSpec((1, F), lambda i: (0, 0)),
                   pl.BlockSpec((G, C, C), lambda i: (0, 0, 0)), pl.BlockSpec((C, G), lambda i: (0, 0))],
        out_shape=[jax.ShapeDtypeStruct((T, F2), BF16), jax.ShapeDtypeStruct((1, F), F32),
                   jax.ShapeDtypeStruct((G, C, C), F32), jax.ShapeDtypeStruct((C, G), F32)],
        scratch_shapes=[pltpu.VMEM((SUBLANES, F), F32), pltpu.VMEM((G, C, C), F32), pltpu.VMEM((C, G), F32)],
        compiler_params=_params("arbitrary"), name=name)(uv, dgelu, dy, gain, ws, bst)


def _my_place():
    return lax.axis_index("x"), lax.axis_index("y"), lax.axis_index("c")


def _all_gather(shards, *, name):
    nf = len(shards)
    items = [(lambda ins, f=f: ins[f], lambda outs, p, f=f: outs[f].at[:, p]) for f in range(nf)]
    targets = [jax.ShapeDtypeStruct((s.shape[0], N_DEV) + s.shape[1:], s.dtype) for s in shards]
    return _run_comm(_Comm(shards, targets, len(items), _gather_emit(items, [list(range(nf))])), name=name)


class _Comm:
    def __init__(self, sources, targets, n_items, emit):
        self.sources, self.targets, self.n_items, self.emit = list(sources), list(targets), n_items, emit
        self.filled = [t for t in self.targets if not isinstance(t, jax.ShapeDtypeStruct)]
        self.operands = self.sources + self.filled
        self.out_shapes = [jax.ShapeDtypeStruct(t.shape, t.dtype) for t in self.targets]
        self.sems = [pltpu.SemaphoreType.DMA((n_items, 7)), pltpu.SemaphoreType.DMA((n_items, 7)),
                     pltpu.SemaphoreType.DMA((n_items,))]

    def aliases(self, first_operand, first_result):
        pos = {id(t): k for k, t in enumerate(self.targets)}
        return {first_operand + len(self.sources) + a: first_result + pos[id(t)]
                for a, t in enumerate(self.filled)}


def _run_comm(comm, *, name):
    n_op, n_out = len(comm.operands), len(comm.targets)

    def body(*refs):
        comm.emit(0, 1, refs[:len(comm.sources)], refs[n_op:n_op + n_out], refs[n_op + n_out:])

    return pl.pallas_call(
        body, in_specs=[ANY] * n_op, out_specs=[ANY] * n_out, out_shape=comm.out_shapes,
        scratch_shapes=comm.sems, input_output_aliases=comm.aliases(0, 0), name=name)(*comm.operands)


def _at_steps(step, n_steps, phases):
    if n_steps == 1:
        for _, fn in phases:
            fn()
        return
    marks = {}
    for frac, fn in phases:
        marks.setdefault(min(int(frac * n_steps), n_steps - 1), []).append(fn)
    for mark, fns in sorted(marks.items()):
        @pl.when(step == mark)
        def _(fns=fns):
            for fn in fns:
                fn()


def _gather_emit(items, groups):
    def emit(step, n_steps, ins, outs, sems):
        send_sems, recv_sems, local_sems = sems
        x, y, c = _my_place()
        me, sibling = (x, y, c), (x, y, 1 - c)
        chips = [(1 - x, y), (x, 1 - y), (1 - x, 1 - y)]

        def copy(i, k, block, to, own=False):
            src_of, dst_of = items[i]
            dst = dst_of(outs, 4 * block[0] + 2 * block[1] + block[2])
            return pltpu.make_async_remote_copy(
                src_ref=src_of(ins) if own else dst, dst_ref=dst,
                send_sem=send_sems.at[i, k], recv_sem=recv_sems.at[i, k],
                device_id=to, device_id_type=MESH)

        def local(i):
            src_of, dst_of = items[i]
            return pltpu.make_async_copy(src_of(ins), dst_of(outs, 4 * x + 2 * y + c), local_sems.at[i])

        def first(i):
            return [copy(i, 0, me, sibling, own=True)] + [
                copy(i, 1 + j, me, (*chip, c), own=True) for j, chip in enumerate(chips)]

        def start():
            for i in range(len(items)):
                local(i).start()
                for cp in first(i):
                    cp.start()

        def forward(group):
            for j, chip in enumerate(chips):
                for i in group:
                    copy(i, 1 + j, (*chip, c), me).wait_recv()
                    copy(i, 4 + j, (*chip, c), sibling).start()

        def finish():
            for i in range(len(items)):
                copy(i, 0, sibling, me).wait_recv()
                for j, chip in enumerate(chips):
                    copy(i, 4 + j, (*chip, 1 - c), me).wait_recv()
            for i in range(len(items)):
                for cp in first(i) + [copy(i, 4 + j, (*chip, c), sibling) for j, chip in enumerate(chips)]:
                    cp.wait_send()
                local(i).wait()

        phases = [(0.0, start)]
        for g, group in enumerate(groups):
            phases.append(((g + 1) / len(groups), functools.partial(forward, group)))
        phases.append((1.0, finish))
        _at_steps(step, n_steps, phases)

    return emit


def _exchange_emit(items):
    def emit(step, n_steps, ins, outs, sems):
        send_sems, recv_sems, local_sems = sems
        x, y, c = _my_place()
        me = 4 * x + 2 * y + c

        def peer_of(k):
            return x ^ (k >> 2), y ^ ((k >> 1) & 1), c ^ (k & 1)

        def copy(i, k):
            src_of, dst_of = items[i]
            px, py, pc = peer_of(k)
            return pltpu.make_async_remote_copy(
                src_ref=src_of(ins, 4 * px + 2 * py + pc), dst_ref=dst_of(outs, me),
                send_sem=send_sems.at[i, k - 1], recv_sem=recv_sems.at[i, k - 1],
                device_id=(px, py, pc), device_id_type=MESH)

        def arrival(i, k):
            src_of, dst_of = items[i]
            px, py, pc = peer_of(k)
            peer = 4 * px + 2 * py + pc
            return pltpu.make_async_remote_copy(
                src_ref=src_of(ins, peer), dst_ref=dst_of(outs, peer),
                send_sem=send_sems.at[i, k - 1], recv_sem=recv_sems.at[i, k - 1],
                device_id=(x, y, c), device_id_type=MESH)

        def local(i):
            src_of, dst_of = items[i]
            return pltpu.make_async_copy(src_of(ins, me), dst_of(outs, me), local_sems.at[i])

        def start():
            for i in range(len(items)):
                local(i).start()
            for k in range(1, N_DEV):
                for i in range(len(items)):
                    copy(i, k).start()

        def finish():
            for k in range(1, N_DEV):
                for i in range(len(items)):
                    arrival(i, k).wait_recv()
            for k in range(1, N_DEV):
                for i in range(len(items)):
                    copy(i, k).wait_send()
            for i in range(len(items)):
                local(i).wait()

        _at_steps(step, n_steps, [(0.0, start), (1.0, finish)])

    return emit


def _adam_math(g, w, m, v):
    m = ADAM_B1 * m + (1.0 - ADAM_B1) * g
    v = ADAM_B2 * v + (1.0 - ADAM_B2) * (g * g)
    m_hat = m / (1.0 - ADAM_B1 ** ADAM_STEP)
    v_hat = v / (1.0 - ADAM_B2 ** ADAM_STEP)
    delta = -ADAM_LR * (m_hat / (jnp.sqrt(v_hat) + ADAM_EPS) + ADAM_WD * w)
    return delta, m, v


def _sum_adamw(parts, w, m, v, *, name, tr=256):
    L, nd, R, C = parts.shape
    tr = min(tr, R)
    assert R % tr == 0

    def body(p_ref, w_ref, m_ref, v_ref, g_ref, d_ref, nm_ref, nv_ref):
        g = p_ref[0, 0].astype(F32)
        for q in range(1, nd):
            g = g + p_ref[0, q].astype(F32)
        d, nm, nv = _adam_math(g, w_ref[0], m_ref[0], v_ref[0])
        g_ref[0] = g
        d_ref[0] = d
        nm_ref[0] = nm
        nv_ref[0] = nv

    blk = pl.BlockSpec((1, tr, C), lambda l, i: (l, i, 0))
    out = jax.ShapeDtypeStruct((L, R, C), F32)
    return pl.pallas_call(
        body, grid=(L, R // tr),
        in_specs=[pl.BlockSpec((1, nd, tr, C), lambda l, i: (l, 0, i, 0)), blk, blk, blk],
        out_specs=[blk] * 4, out_shape=[out] * 4,
        compiler_params=_params("parallel", "parallel"), name=name)(parts, w, m, v)


def _sum_parts(parts, *, name):
    nd, R, C = parts.shape

    def body(p_ref, o_ref):
        g = p_ref[0]
        for q in range(1, nd):
            g = g + p_ref[q]
        o_ref[...] = g

    return pl.pallas_call(
        body, out_shape=jax.ShapeDtypeStruct((R, C), F32),
        in_specs=[pl.BlockSpec(memory_space=pltpu.VMEM)],
        out_specs=pl.BlockSpec(memory_space=pltpu.VMEM), name=name)(parts)


def _adamw_small(g, w, m, v, *, name):
    def body(g_ref, w_ref, m_ref, v_ref, d_ref, nm_ref, nv_ref):
        d, nm, nv = _adam_math(g_ref[...], w_ref[...], m_ref[...], v_ref[...])
        d_ref[...] = d
        nm_ref[...] = nm
        nv_ref[...] = nv

    vm = pl.BlockSpec(memory_space=pltpu.VMEM)
    out = jax.ShapeDtypeStruct(g.shape, F32)
    return pl.pallas_call(body, out_shape=[out] * 3, in_specs=[vm] * 4, out_specs=[vm] * 3,
                          name=name)(g, w, m, v)


def kernel(x, norm_mix, norm_mlp, sb_wqkv, sb_wo, sgu_win, sgu_gain, sgu_ws, sgu_bs, sgu_wout, mlp_w1, mlp_w2, final_norm, loss_target, m_norm_mix, m_norm_mlp, m_sb_wqkv, m_sb_wo, m_sgu_win, m_sgu_gain, m_sgu_ws, m_sgu_bs, m_sgu_wout, m_mlp_w1, m_mlp_w2, m_final_norm, v_norm_mix, v_norm_mlp, v_sb_wqkv, v_sb_wo, v_sgu_win, v_sgu_gain, v_sgu_ws, v_sgu_bs, v_sgu_wout, v_mlp_w1, v_mlp_w2, v_final_norm):
    batch, seq, D = x.shape
    T = batch * seq
    x0 = x.reshape(T, D)
    target = loss_target.reshape(T, D)

    WQKV, WO, WIN, WOUT, W1, W2, GAIN = range(7)
    big = [sb_wqkv, sb_wo, sgu_win, sgu_wout, mlp_w1, mlp_w2]
    shards = [w.astype(BF16) for w in big] + [sgu_gain[:, None, :]]
    rides = {
        "qkv0": [[(WO, 0), (W1, 0)]],
        "sb_fwd0": [[(W2, 0)], [(WIN, 0), (WOUT, 0), (W1, 1), (W2, 1)], [(WQKV, 1)]],
        "sb_fwd2": [[(WO, 1), (W1, 2), (W2, 2)], [(WIN, 1), (WOUT, 1), (W1, 3), (W2, 3)]],
    }

    first = [(lambda ins: ins[WQKV].at[0], lambda outs, p: outs[WQKV].at[0, p]),
             (lambda ins: ins[GAIN], lambda outs, p: outs[GAIN].at[:, p])]
    targets = [jax.ShapeDtypeStruct((s.shape[0], N_DEV) + s.shape[1:], s.dtype) for s in shards]
    gathered0 = _run_comm(_Comm(shards, targets, len(first), _gather_emit(first, [[0, 1]])), name="gather_first")
    gain_sgu = gathered0[GAIN].reshape(-1, 1, SGU_FFN)
    gw = dict(enumerate(gathered0[:GAIN]))

    def gather_ride(call):
        wanted = [fl for group in rides[call] for fl in group]
        fams = sorted({f for f, _ in wanted})
        items = [(lambda ins, f=f, l=l: ins[f].at[l], lambda outs, p, t=fams.index(f), l=l: outs[t].at[l, p])
                 for f, l in wanted]
        groups, k = [], 0
        for group in rides[call]:
            groups.append(list(range(k, k + len(group))))
            k += len(group)
        return _Comm(shards[:GAIN], [gw[f] for f in fams], len(items), _gather_emit(items, groups)), fams

    def weight(f):
        g = gw[f]
        return g if f in (WQKV, WIN, W1) else g.reshape(g.shape[0], 1, N_DEV * g.shape[2], g.shape[3])

    saved = []
    xs = x0
    for i in range(DEPTH):
        j = i // 2
        if i % 2 == 0:
            if i == 0:
                comm, fams = gather_ride("qkv0")
                (qkv, h), results = _mm_nn(xs, weight(WQKV), j, out_dtype=BF16, gain=norm_mix[i:i + 1],
                                           name=f"qkv{i}", comm=comm, tm=1024)
                gw.update(zip(fams, results))
            else:
                qkv, h = _mm_nn(xs, weight(WQKV), j, out_dtype=BF16, gain=norm_mix[i:i + 1], name=f"qkv{i}", tm=1024)
            comm, fams = gather_ride(f"sb_fwd{i}")
            o, results = _sb_fwd(qkv, batch=batch, seq=seq, name=f"sb_fwd{i}", comm=comm)
            gw.update(zip(fams, results))
            x_mid = _mm_nn(o, weight(WO), j, out_dtype=F32, res=xs, name=f"wo{i}", tm=1024)
            mix = (h, qkv, o)
        else:
            gain_j = gain_sgu[j]
            bst = sgu_bs[j].T
            uv, h, dgelu = _mm_nn(xs, weight(WIN), j, out_dtype=BF16, gain=norm_mix[i:i + 1], gelu=True,
                                  name=f"win{i}")
            yv = _sgu_fwd(uv, gain_j, sgu_ws[j], bst, name=f"sgu_fwd{i}")
            x_mid = _mm_nn(yv, weight(WOUT), j, out_dtype=F32, res=xs, name=f"wout{i}", tm=1024)
            mix = (h, uv, dgelu, yv, gain_j, bst)
        a, h2 = _mm_nn(x_mid, weight(W1), i, out_dtype=BF16, gain=norm_mlp[i:i + 1], name=f"w1_{i}", tm=1024)
        x_out = _mm_nn(a, weight(W2), i, out_dtype=F32, res=x_mid, a_act="relu2", name=f"w2_{i}")
        saved.append((xs, mix, x_mid, h2, a))
        xs = x_out
    g_wqkv, g_wo, g_win, g_wout, g_w1, g_w2 = [weight(f) for f in range(GAIN)]

    dx, dxb, sq, d_final = _loss_head(xs, final_norm.reshape(1, D), target, name="loss_head")
    loss = lax.psum(0.5 * jnp.sum(sq) / D, ("x", "y", "c"))

    SMALL = GAIN
    stacks = {f: jax.ShapeDtypeStruct((w.shape[0], N_DEV) + w.shape[1:], BF16) for f, w in enumerate(big)}
    pending = []

    def row_shards(p):
        return p.reshape(N_DEV, p.shape[1] // N_DEV, p.shape[2])

    def exchange_pending():
        fams = sorted({f for _, f, _ in pending})
        items = [(lambda ins, p, a=a: ins[a].at[p], lambda outs, q, t=fams.index(f), l=l: outs[t].at[l, q])
                 for a, (_, f, l) in enumerate(pending)]
        comm = _Comm([part for part, _, _ in pending], [stacks[f] for f in fams], len(items), _exchange_emit(items))
        pending.clear()
        return comm, fams

    d_norm_mix, d_norm_mlp = [None] * DEPTH, [None] * DEPTH
    d_gain, d_ws, d_bs = [None] * 2, [None] * 2, [None] * 2
    for i in reversed(range(DEPTH)):
        j = i // 2
        xs, mix, x_mid, h2, a = saved[i]
        da = _mm_nt(dxb, g_w2, i, out_dtype=BF16, act_src=a, name=f"d_a{i}")
        pending.append((row_shards(_mm_tn(a, dxb, shards=1, a_act="relu2", name=f"d_w2_{i}")), W2, i))
        pending.append((_mm_tn(h2, da, shards=N_DEV, name=f"d_w1_{i}", pb=4, tm=1024), W1, i))
        (dx, dxb, d_norm_mlp[i]), _ = _mm_nt_rms_bwd(da, g_w1, i, x_mid, norm_mlp[i:i + 1], dx, name=f"d_h2_{i}")
        if i % 2 == 0:
            h, qkv, o = mix
            do = _mm_nt(dxb, g_wo, j, out_dtype=BF16, name=f"d_o{i}", tm=1024)
            pending.append((row_shards(_mm_tn(o, dxb, shards=1, name=f"d_wo{i}")), WO, j))
            comm, fams = exchange_pending()
            (dq, dk, dv), results = _sb_bwd(qkv, o, do, batch=batch, seq=seq, name=f"sb_bwd{i}", comm=comm)
            stacks.update(zip(fams, results))
            dqkv = jnp.concatenate([dq, dk, dv], axis=1)
            pending.append((_mm_tn(h, dqkv, shards=N_DEV, name=f"d_wqkv{i}", pb=4, tm=1024), WQKV, j))
            mixer_in, w_in = dqkv, g_wqkv
        else:
            h, uv, dgelu, yv, gain_j, bst = mix
            dy = _mm_nt(dxb, g_wout, j, out_dtype=BF16, name=f"d_y{i}", tm=1024)
            pending.append((row_shards(_mm_tn(yv, dxb, shards=1, name=f"d_wout{i}")), WOUT, j))
            duv, d_gain[j], d_ws[j], dbst = _sgu_bwd(uv, dgelu, dy, gain_j, sgu_ws[j], bst, name=f"sgu_bwd{i}")
            d_bs[j] = dbst.T
            pending.append((_mm_tn(h, duv, shards=N_DEV, name=f"d_win{i}", pb=4, tm=1024), WIN, j))
            mixer_in, w_in = duv, g_win
        comm, fams = exchange_pending() if i == 0 else (None, [])
        (dx, dxb, d_norm_mix[i]), results = _mm_nt_rms_bwd(
            mixer_in, w_in, j, xs, norm_mix[i:i + 1], dx, name=f"d_h_mix{i}", comm=comm)
        stacks.update(zip(fams, results))
    grad_x = dx.reshape(batch, seq, D)

    small = [jnp.concatenate(d_norm_mix, 0), jnp.concatenate(d_norm_mlp, 0), d_final,
             jnp.concatenate(d_gain, 0), jnp.stack(d_bs, 0), jnp.stack(d_ws, 0)]
    small_flat = jnp.concatenate([s.reshape(-1) for s in small])
    n_small = small_flat.shape[0]
    small_rows = -(-n_small // (N_DEV * SUBLANES * LANES)) * SUBLANES
    small_flat = jnp.pad(small_flat, (0, N_DEV * small_rows * LANES - n_small))
    stacks[SMALL] = jax.ShapeDtypeStruct((1, N_DEV, small_rows, LANES), F32)
    pending.append((small_flat.reshape(N_DEV, small_rows, LANES), SMALL, 0))
    comm, fams = exchange_pending()
    stacks.update(zip(fams, _run_comm(comm, name="exchange_last")))
    r_wqkv, r_wo, r_win, r_wout, r_w1, r_w2, r_small = [stacks[f] for f in range(SMALL + 1)]

    u_wqkv = _sum_adamw(r_wqkv, sb_wqkv, m_sb_wqkv, v_sb_wqkv, name="adamw_wqkv")
    u_wo = _sum_adamw(r_wo, sb_wo, m_sb_wo, v_sb_wo, name="adamw_wo")
    u_win = _sum_adamw(r_win, sgu_win, m_sgu_win, v_sgu_win, name="adamw_win")
    u_wout = _sum_adamw(r_wout, sgu_wout, m_sgu_wout, v_sgu_wout, name="adamw_wout")
    u_w1 = _sum_adamw(r_w1, mlp_w1, m_mlp_w1, v_mlp_w1, name="adamw_w1")
    u_w2 = _sum_adamw(r_w2, mlp_w2, m_mlp_w2, v_mlp_w2, name="adamw_w2")

    small_sum = _sum_parts(r_small[0], name="sum_small")
    g_small = _all_gather([small_sum[None]], name="gather_small")[0].reshape(-1)[:n_small]

    shapes = [s.shape for s in small]
    sizes = [s.size for s in small]
    offs = [sum(sizes[:k]) for k in range(len(sizes))]
    me = 4 * lax.axis_index("x") + 2 * lax.axis_index("y") + lax.axis_index("c")
    shard_w = SGU_FFN // N_DEV

    def pack(arrs):
        flat = jnp.concatenate([a_.reshape(-1) for a_ in arrs])
        return jnp.pad(flat, (0, N_DEV * small_rows * LANES - n_small)).reshape(-1, LANES)

    def full_gain(gshard):
        return lax.dynamic_update_slice(jnp.zeros((2, SGU_FFN), F32), gshard, (0, me * shard_w))

    w_small = pack([norm_mix, norm_mlp, final_norm, full_gain(sgu_gain), sgu_bs, sgu_ws])
    m_small = pack([m_norm_mix, m_norm_mlp, m_final_norm, full_gain(m_sgu_gain), m_sgu_bs, m_sgu_ws])
    v_small = pack([v_norm_mix, v_norm_mlp, v_final_norm, full_gain(v_sgu_gain), v_sgu_bs, v_sgu_ws])
    g_pack = jnp.pad(g_small, (0, N_DEV * small_rows * LANES - n_small)).reshape(-1, LANES)
    sm = [g_pack] + list(_adamw_small(g_pack, w_small, m_small, v_small, name="adamw_small"))

    def unpack(flat2d):
        flat = flat2d.reshape(-1)
        out = [flat[offs[k]:offs[k] + sizes[k]].reshape(shapes[k]) for k in range(len(sizes))]
        out[2] = out[2].reshape(D)
        out[3] = lax.dynamic_slice(out[3], (0, me * shard_w), (2, shard_w))
        return out

    outs = []
    for k, big_u in enumerate(zip(u_wqkv, u_wo, u_win, u_wout, u_w1, u_w2)):
        s_nm, s_nl, s_fn, s_gain, s_bs, s_ws = unpack(sm[k])
        b_wqkv, b_wo, b_win, b_wout, b_w1, b_w2 = big_u
        outs += [s_nm, s_nl, b_wqkv, b_wo, b_win, s_gain, s_ws, s_bs, b_wout, b_w1, b_w2, s_fn]
    return (loss, grad_x, *outs)
```

```python
import functools

import jax
import jax.numpy as jnp
from jax import lax
from jax.experimental import pallas as pl
from jax.experimental.pallas import tpu as pltpu

F32 = jnp.float32
BF16 = jnp.bfloat16

N_DEV = 8
D_MODEL = 1024
SEQ = 2048
DEPTH = 4
SB_HEAD_DIM = 64
SGU_CHUNK = 128
SGU_FFN = 2 * D_MODEL
SGU_GROUPS = 8
SGU_GROUP_W = SGU_FFN // SGU_GROUPS
EPS = 1e-6

ADAM_LR = 0.001
ADAM_B1 = 0.9
ADAM_B2 = 0.999
ADAM_EPS = 1e-08
ADAM_WD = 0.01
ADAM_STEP = 10

MXU_TILE = 256
LANES = 128
SUBLANES = 8
VMEM_LIMIT = 56 * 1024 * 1024
EXP_ZERO_BELOW = -104.0
SB_TILE = 256
SB_STRAIGHT = 2
SB_PAIRS = 4

MESH = pl.DeviceIdType.MESH
ANY = pl.BlockSpec(memory_space=pl.ANY)


def _params(*sem):
    return pltpu.CompilerParams(dimension_semantics=sem, vmem_limit_bytes=VMEM_LIMIT)


def _dot(a, b):
    return jnp.dot(a, b, preferred_element_type=F32)


def _dot_nt(a, b):
    return lax.dot_general(a, b, (((1,), (1,)), ((), ())), preferred_element_type=F32)


def _dot_tn(a, b):
    return lax.dot_general(a, b, (((0,), (0,)), ((), ())), preferred_element_type=F32)


def _split_bf16(x):
    hi = x.astype(BF16)
    lo = (x - hi.astype(F32)).astype(BF16)
    return hi, lo


def _relu2(av):
    t = jnp.maximum(av, jnp.zeros_like(av))
    return t * t


def _mm_nn(a, w, l, *, out_dtype, name, res=None, a_act=None, gain=None, gelu=False, comm=None,
           tm=512, kc=1024):
    M, K = a.shape
    _, P, K2, n = w.shape
    assert K2 == K
    tm, kc = min(tm, M), min(kc, K)
    assert M % tm == 0 and K % kc == 0
    join = 2 if n % MXU_TILE and P % 2 == 0 else 1

    n_in = 2 + (gain is not None) + (res is not None)
    n_out = 1 + (gain is not None) + (gelu is True)

    def body(*all_refs):
        ins, outs, _ = carried.split(all_refs, n_in, n_out, 0)
        carried.emit(all_refs)
        refs = list(ins) + list(outs)
        a_ref, w_ref, o_ref = refs[0], refs[1], refs[n_in]
        if gain is not None:
            xv = a_ref[...]
            r = lax.rsqrt(jnp.mean(xv * xv, axis=-1, keepdims=True) + EPS)
            h = (xv * r * refs[2][...]).astype(BF16)
            refs[n_in + 1][...] = h
        for p in range(0, P, join):
            sl = slice(p * n, (p + join) * n)
            acc = None
            for k0 in range(0, K, kc):
                if gain is not None:
                    av = h[:, k0:k0 + kc]
                else:
                    av = a_ref[:, k0:k0 + kc]
                    av = _relu2(av) if a_act == "relu2" else av.astype(BF16)
                wv = [w_ref[0, p + s, k0:k0 + kc, :] for s in range(join)]
                d = _dot(av, wv[0] if join == 1 else jnp.concatenate(wv, axis=1))
                acc = d if acc is None else acc + d
            if res is not None:
                acc = acc + refs[n_in - 1][:, sl]
            if gelu:
                acc, dact = _gelu(acc)
                refs[-1][:, sl] = dact.astype(BF16)
            o_ref[:, sl] = acc.astype(out_dtype)

    row = lambda width: pl.BlockSpec((tm, width), lambda i: (i, 0))
    in_specs = [row(K), pl.BlockSpec((1, P, K, n), lambda i: (l, 0, 0, 0), pipeline_mode=pl.Buffered(1))]
    args = [a, w]
    if gain is not None:
        in_specs.append(pl.BlockSpec((1, K), lambda i: (0, 0)))
        args.append(gain)
    if res is not None:
        in_specs.append(row(P * n))
        args.append(res)
    out_specs, out_shape = [row(P * n)], [jax.ShapeDtypeStruct((M, P * n), out_dtype)]
    if gain is not None:
        out_specs.append(row(K))
        out_shape.append(jax.ShapeDtypeStruct((M, K), BF16))
    if gelu:
        out_specs.append(row(P * n))
        out_shape.append(jax.ShapeDtypeStruct((M, P * n), BF16))
    carried = _Carried(comm, (M // tm,), last="parallel")
    outs, comm_results = carried.pallas_call(
        body, in_specs=in_specs, out_specs=out_specs, out_shape=out_shape, scratch_shapes=[],
        args=args, name=name)
    outs = tuple(outs) if n_out > 1 else outs[0]
    return outs if comm is None else (outs, comm_results)


def _mm_nt(a, w, l, *, out_dtype, name, act_src=None, tm=512, tn=1024):
    M, K = a.shape
    _, P, Nout, kc = w.shape
    assert P == 1 and K == kc
    tm, tn = min(tm, M), min(tn, Nout)
    assert M % tm == 0 and Nout % tn == 0

    def body(*refs):
        a_ref, w_ref, o_ref = refs[0], refs[1], refs[-1]
        av = a_ref[...].astype(BF16)
        for n0 in range(0, Nout, tn):
            r = _dot_nt(av, w_ref[0, 0, n0:n0 + tn, :])
            if act_src is not None:
                r = r * (2.0 * jnp.maximum(refs[2][:, n0:n0 + tn].astype(F32), 0.0))
            o_ref[:, n0:n0 + tn] = r.astype(out_dtype)

    row = lambda width: pl.BlockSpec((tm, width), lambda i: (i, 0))
    in_specs = [row(K), pl.BlockSpec((1, 1, Nout, K), lambda i: (l, 0, 0, 0), pipeline_mode=pl.Buffered(1))]
    args = [a, w]
    if act_src is not None:
        in_specs.append(row(Nout))
        args.append(act_src)
    return pl.pallas_call(
        body, grid=(M // tm,), in_specs=in_specs, out_specs=row(Nout),
        out_shape=jax.ShapeDtypeStruct((M, Nout), out_dtype),
        compiler_params=_params("parallel"), name=name)(*args)


def _mm_tn(a, b, *, shards, name, a_act=None, tm=2048, tk=1024, pb=1):
    M, K = a.shape
    M2, N = b.shape
    assert M2 == M
    n = N // shards
    tm, tk = min(tm, M), min(tk, K)
    assert M % tm == 0 and K % tk == 0 and shards % pb == 0
    width, nm = pb * n, M // tm

    def body(a_ref, b_ref, o_ref, acc):
        m = pl.program_id(2)
        av = a_ref[...]
        if a_act == "relu2":
            av = _relu2(av)

        @pl.when(m == 0)
        def _():
            acc[...] = jnp.zeros_like(acc)

        acc[...] += _dot_tn(av.astype(BF16), b_ref[...].astype(BF16))

        @pl.when(m == nm - 1)
        def _():
            for p in range(pb):
                o_ref[p] = acc[:, p * n:(p + 1) * n].astype(BF16)

    return pl.pallas_call(
        body, grid=(K // tk, N // width, nm),
        in_specs=[pl.BlockSpec((tm, tk), lambda i, j, m: (m, i)),
                  pl.BlockSpec((tm, width), lambda i, j, m: (m, j))],
        out_specs=pl.BlockSpec((pb, tk, n), lambda i, j, m: (j, i, 0)),
        out_shape=jax.ShapeDtypeStruct((shards, K, n), BF16),
        scratch_shapes=[pltpu.VMEM((tk, width), F32)],
        compiler_params=_params("parallel", "parallel", "arbitrary"), name=name)(a, b)


def _mm_nt_rms_bwd(a, w, l, x, gain, dres, *, name, comm=None, tm=512):
    M, K = a.shape
    _, P, D, kc = w.shape
    assert K == P * kc and x.shape == (M, D)
    tm = min(tm, M)
    nr = M // tm
    join = 2 if kc % MXU_TILE and P % 2 == 0 else 1

    def body(*refs):
        (a_ref, w_ref, x_ref, g_ref, dres_ref), (dx_ref, dxb_ref, dg_ref), (acc,) = carried.split(refs, 5, 3, 1)
        i = pl.program_id(0)
        carried.emit(refs)
        dhv = None
        for p in range(0, P, join):
            wv = [w_ref[0, p + s] for s in range(join)]
            d = _dot_nt(a_ref[:, p * kc:(p + join) * kc], wv[0] if join == 1 else jnp.concatenate(wv, axis=1))
            dhv = d if dhv is None else dhv + d
        xv = x_ref[...]
        r = lax.rsqrt(jnp.mean(xv * xv, axis=-1, keepdims=True) + EPS)
        xhat = xv * r
        dxhat = dhv * g_ref[...]
        dx = dres_ref[...] + r * (dxhat - xhat * jnp.mean(dxhat * xhat, axis=-1, keepdims=True))
        dx_ref[...] = dx
        dxb_ref[...] = dx.astype(BF16)
        part = jnp.sum((dhv * xhat).reshape(tm // SUBLANES, SUBLANES, D), axis=0)

        @pl.when(i == 0)
        def _():
            acc[...] = jnp.zeros_like(acc)

        acc[...] += part

        @pl.when(i == nr - 1)
        def _():
            dg_ref[...] = jnp.sum(acc[...], axis=0, keepdims=True)

    row = pl.BlockSpec((tm, D), lambda i: (i, 0))
    vec = pl.BlockSpec((1, D), lambda i: (0, 0))
    carried = _Carried(comm, (nr,))
    return carried.pallas_call(
        body,
        in_specs=[pl.BlockSpec((tm, K), lambda i: (i, 0)),
                  pl.BlockSpec((1, P, D, kc), lambda i: (l, 0, 0, 0), pipeline_mode=pl.Buffered(1)),
                  row, vec, row],
        out_specs=[row, row, vec],
        out_shape=[jax.ShapeDtypeStruct((M, D), F32), jax.ShapeDtypeStruct((M, D), BF16),
                   jax.ShapeDtypeStruct((1, D), F32)],
        scratch_shapes=[pltpu.VMEM((SUBLANES, D), F32)],
        args=[a, w, x, gain, dres], name=name)


def _loss_head(x, gain, target, *, name, tr=512):
    T, D = x.shape
    nr = T // tr

    def body(x_ref, g_ref, t_ref, dx_ref, dxb_ref, sq_ref, dg_ref, sq_acc, dg_acc):
        i = pl.program_id(0)
        xv = x_ref[...]
        g = g_ref[...]
        r = lax.rsqrt(jnp.mean(xv * xv, axis=-1, keepdims=True) + EPS)
        xhat = xv * r
        err = xhat * g - t_ref[...]
        dy = err * (1.0 / D)
        dxhat = dy * g
        dx = r * (dxhat - xhat * jnp.mean(dxhat * xhat, axis=-1, keepdims=True))
        dx_ref[...] = dx
        dxb_ref[...] = dx.astype(BF16)
        sq = jnp.sum((err * err).reshape(tr // SUBLANES, SUBLANES, D), axis=0)
        dg = jnp.sum((dy * xhat).reshape(tr // SUBLANES, SUBLANES, D), axis=0)

        @pl.when(i == 0)
        def _():
            sq_acc[...] = sq
            dg_acc[...] = dg

        @pl.when(i > 0)
        def _():
            sq_acc[...] += sq
            dg_acc[...] += dg

        @pl.when(i == nr - 1)
        def _():
            sq_ref[...] = sq_acc[...]
            dg_ref[...] = jnp.sum(dg_acc[...], axis=0, keepdims=True)

    row = pl.BlockSpec((tr, D), lambda i: (i, 0))
    vec = pl.BlockSpec((1, D), lambda i: (0, 0))
    part = pl.BlockSpec((SUBLANES, D), lambda i: (0, 0))
    return pl.pallas_call(
        body, grid=(nr,), in_specs=[row, vec, row], out_specs=[row, row, part, vec],
        out_shape=[jax.ShapeDtypeStruct((T, D), F32), jax.ShapeDtypeStruct((T, D), BF16),
                   jax.ShapeDtypeStruct((SUBLANES, D), F32), jax.ShapeDtypeStruct((1, D), F32)],
        scratch_shapes=[pltpu.VMEM((SUBLANES, D), F32), pltpu.VMEM((SUBLANES, D), F32)],
        compiler_params=_params("arbitrary"), name=name)(x, gain, target)


def _head0_lanes():
    return lax.broadcasted_iota(jnp.int32, (1, LANES), 1) < SB_HEAD_DIM


def _stack_heads(x):
    zero = jnp.zeros_like(x)
    h0 = _head0_lanes()
    return jnp.concatenate([jnp.where(h0, x, zero), jnp.where(h0, zero, x)], axis=0)


def _unstack_heads(y, tq):
    return jnp.where(_head0_lanes(), y[:tq], y[tq:])


def _past_mask(tq, tk):
    row = lax.broadcasted_iota(jnp.int32, (2 * tq, tk), 0) & (tq - 1)
    col = lax.broadcasted_iota(jnp.int32, (2 * tq, tk), 1)
    return col < row


def _max_of(arrays):
    return functools.reduce(jnp.maximum, [jnp.max(a) for a in arrays])


def _sb_logs(z, past):
    minus_abs = lax.bitcast_convert_type(
        lax.bitcast_convert_type(z, jnp.uint32) | jnp.uint32(0x80000000), F32)
    log_beta = jnp.minimum(z, 0.0) - jnp.log(1.0 + jnp.exp(minus_abs))
    l = log_beta - z
    if past is not None:
        l = jnp.where(past, l, 0.0)
    return log_beta, l


def _suffix_matrix(tk):
    j = lax.broadcasted_iota(jnp.int32, (2 * tk, tk), 0) & (tk - 1)
    s = lax.broadcasted_iota(jnp.int32, (2 * tk, tk), 1)
    return (j > s).astype(BF16)


def _suffix_sum(x, u2, exact=True):
    if not exact:
        return _dot(x.astype(BF16), u2[:x.shape[1]])
    hi, lo = _split_bf16(x)
    return _dot(jnp.concatenate([hi, lo], axis=1), u2)


class _Carried:
    def __init__(self, comm, grid, last="arbitrary"):
        self.comm, self.grid, self.last = comm, grid, last
        self.n_op = len(comm.operands) if comm else 0
        self.n_tgt = len(comm.targets) if comm else 0

    def split(self, refs, n_in, n_out, n_scratch):
        self.n_in, self.n_out, self.n_scratch = n_in, n_out, n_scratch
        a = n_in + self.n_op
        b = a + n_out + self.n_tgt
        return refs[:n_in], refs[a:a + n_out], refs[b:b + n_scratch]

    def emit(self, refs):
        if self.comm is None:
            return
        step, n_steps = 0, 1
        for d, size in enumerate(self.grid):
            step = step * size + pl.program_id(d)
            n_steps *= size
        a = self.n_in + self.n_op + self.n_out
        self.comm.emit(step, n_steps, refs[self.n_in:self.n_in + len(self.comm.sources)],
                       refs[a:a + self.n_tgt], refs[a + self.n_tgt + self.n_scratch:])

    def pallas_call(self, body, *, in_specs, out_specs, out_shape, scratch_shapes, args, name):
        comm = self.comm
        n_in, n_out = len(in_specs), len(out_specs)
        aliases = {}
        if comm is not None:
            in_specs = in_specs + [ANY] * self.n_op
            out_specs = out_specs + [ANY] * self.n_tgt
            out_shape = out_shape + comm.out_shapes
            scratch_shapes = scratch_shapes + comm.sems
            args = args + comm.operands
            aliases = comm.aliases(n_in, n_out)
        sem = ("parallel",) * (len(self.grid) - 1) + (self.last,) if comm is None else ("arbitrary",) * len(self.grid)
        results = pl.pallas_call(
            body, grid=self.grid, in_specs=in_specs, out_specs=out_specs, out_shape=out_shape,
            scratch_shapes=scratch_shapes, input_output_aliases=aliases,
            compiler_params=_params(*sem), name=name)(*args)
        return results[:n_out], results[n_out:]


def _sb_fwd(qkv, *, batch, seq, name, comm=None, tq=SB_TILE, n_pre=SB_STRAIGHT, pairs=SB_PAIRS):
    T, D3 = qkv.shape
    D = D3 // 3
    nhp = D // LANES
    tk = tq
    nq = seq // tq
    scale = SB_HEAD_DIM ** -0.5
    assert 1 <= n_pre <= nq and nhp % pairs == 0

    def body(*refs):
        (q_ref, k_ref, v_ref), (o_ref,), (acc,) = carried.split(refs, 3, 1, 1)
        qi = pl.program_id(2)
        carried.emit(refs)
        qs = [_stack_heads(q_ref[:, lanes]) * scale for lanes in cols]
        past = _past_mask(tq, tk)
        u = _suffix_matrix(tk)

        def block(kb, g, c, diag):
            ks = pl.multiple_of(kb * tk, tk)
            z = _dot_nt(qs[g], k_ref[pl.ds(ks, tk), cols[g]])
            log_beta, l = _sb_logs(z, past if diag else None)
            arg = log_beta + _suffix_sum(l, u, exact=False)
            a = jnp.exp(arg if c is None else arg + c)
            if diag:
                a = jnp.where(past, a, 0.0)
            return _dot(a.astype(BF16), v_ref[pl.ds(ks, tk), cols[g]]), jnp.sum(l, axis=1, keepdims=True)

        def straight(n):
            o_sum, c = [None] * pairs, [None] * pairs
            for b in range(n):
                for g in range(pairs):
                    o_b, c_b = block(qi - b, g, c[g], b == 0)
                    o_sum[g] = o_b if b == 0 else o_sum[g] + o_b
                    c[g] = c_b if b == 0 else c[g] + c_b
            return o_sum, c

        def finish(o_sum):
            for g in range(pairs):
                o_ref[:, cols[g]] = _unstack_heads(o_sum[g], tq)

        for n in range(1, n_pre):
            @pl.when(qi == n - 1)
            def _(n=n):
                finish(straight(n)[0])

        @pl.when(qi >= n_pre - 1)
        def _():
            o_sum, c = straight(n_pre)
            for g in range(pairs):
                acc[g] = o_sum[g]

            def cond(st):
                kb, c = st
                return jnp.logical_and(kb >= 0, _max_of(c) > EXP_ZERO_BELOW)

            def step(st):
                kb, c = st
                new_c = []
                for g in range(pairs):
                    o_n, c_n = block(kb, g, c[g], False)
                    acc[g] += o_n
                    new_c.append(c[g] + c_n)
                return kb - 1, tuple(new_c)

            lax.while_loop(cond, step, (qi - n_pre, tuple(c)))
            finish([acc[g] for g in range(pairs)])

    cols = [slice(g * LANES, (g + 1) * LANES) for g in range(pairs)]
    width = pairs * LANES
    carried = _Carried(comm, (batch, nhp // pairs, nq))
    (o,), comm_results = carried.pallas_call(
        body,
        in_specs=[pl.BlockSpec((tq, width), lambda b, p, i: (b * nq + i, p)),
                  pl.BlockSpec((seq, width), lambda b, p, i: (b, nhp // pairs + p)),
                  pl.BlockSpec((seq, width), lambda b, p, i: (b, 2 * (nhp // pairs) + p))],
        out_specs=[pl.BlockSpec((tq, width), lambda b, p, i: (b * nq + i, p))],
        out_shape=[jax.ShapeDtypeStruct((T, D), F32)],
        scratch_shapes=[pltpu.VMEM((pairs, 2 * tq, LANES), F32)],
        args=[qkv, qkv, qkv], name=name)
    return o, comm_results


def _sb_bwd(qkv, o, do, *, batch, seq, name, comm=None, tq=SB_TILE, n_pre=SB_STRAIGHT, pairs=SB_PAIRS):
    T, D3 = qkv.shape
    D = D3 // 3
    nhp = D // LANES
    tk = tq
    nq = seq // tq
    scale = SB_HEAD_DIM ** -0.5

    def body(*refs):
        ins, outs, scratch = carried.split(refs, 5, 3, 3)
        q_ref, k_ref, v_ref, o_ref, do_ref = ins
        dq_ref, dk_ref, dv_ref = outs
        dq_acc, dk_acc, dv_acc = scratch
        qi = pl.program_id(2)
        carried.emit(refs)

        @pl.when(qi == 0)
        def _():
            dk_acc[...] = jnp.zeros_like(dk_acc)
            dv_acc[...] = jnp.zeros_like(dv_acc)

        qs = [_stack_heads(q_ref[:, lanes]) * scale for lanes in cols]
        dos = [_stack_heads(do_ref[:, lanes]) for lanes in cols]
        dsum = [jnp.sum(_stack_heads(do_ref[:, lanes].astype(F32) * o_ref[:, lanes]), axis=1, keepdims=True)
                for lanes in cols]
        past = _past_mask(tq, tk)
        u = _suffix_matrix(tk)

        def block(kb, p, c, gc, diag):
            ks = pl.multiple_of(kb * tk, tk)
            kblk = k_ref[pl.ds(ks, tk), cols[p]]
            vblk = v_ref[pl.ds(ks, tk), cols[p]]
            z = _dot_nt(qs[p], kblk)
            log_beta, l = _sb_logs(z, past if diag else None)
            arg = log_beta + _suffix_sum(l, u, exact=False)
            a = jnp.exp(arg if c is None else arg + c)
            if diag:
                a = jnp.where(past, a, 0.0)
            a = a.astype(BF16)
            beta = 1.0 - jnp.exp(l)
            g = a.astype(F32) * _dot_nt(dos[p], vblk)
            gs = _suffix_sum(g, u)
            dz = g - beta * (dsum[p] - (gs if gc is None else gs + gc))
            if diag:
                dz = jnp.where(past, dz, 0.0)
            dzb = dz.astype(BF16)
            dk_acc[pl.ds(ks, tk), cols[p]] += _dot_tn(dzb, qs[p])
            dv_acc[pl.ds(ks, tk), cols[p]] += _dot_tn(a, dos[p])
            return (_dot(dzb, kblk), jnp.sum(l, axis=1, keepdims=True),
                    jnp.sum(g, axis=1, keepdims=True))

        def finish(dq_sum):
            for p in range(pairs):
                dq_ref[:, cols[p]] = (_unstack_heads(dq_sum[p], tq) * scale).astype(BF16)

        def straight(n):
            dq_sum, c, gc = [None] * pairs, [None] * pairs, [None] * pairs
            for b in range(n):
                for p in range(pairs):
                    dq_b, c_b, g_b = block(qi - b, p, c[p], gc[p], b == 0)
                    dq_sum[p] = dq_b if b == 0 else dq_sum[p] + dq_b
                    c[p] = c_b if b == 0 else c[p] + c_b
                    gc[p] = g_b if b == 0 else gc[p] + g_b
            return dq_sum, c, gc

        for n in range(1, n_pre):
            @pl.when(qi == n - 1)
            def _(n=n):
                finish(straight(n)[0])

        @pl.when(qi >= n_pre - 1)
        def _():
            dq_sum, c, gc = straight(n_pre)
            for p in range(pairs):
                dq_acc[p] = dq_sum[p]

            def cond(st):
                kb, c, gc = st
                return jnp.logical_and(kb >= 0, _max_of(c) > EXP_ZERO_BELOW)

            def step(st):
                kb, c, gc = st
                new_c, new_gc = [], []
                for p in range(pairs):
                    dq_n, c_n, g_n = block(kb, p, c[p], gc[p], False)
                    dq_acc[p] += dq_n
                    new_c.append(c[p] + c_n)
                    new_gc.append(gc[p] + g_n)
                return kb - 1, tuple(new_c), tuple(new_gc)

            lax.while_loop(cond, step, (qi - n_pre, tuple(c), tuple(gc)))
            finish([dq_acc[p] for p in range(pairs)])

        @pl.when(qi == nq - 1)
        def _():
            dk_ref[...] = dk_acc[...].astype(BF16)
            dv_ref[...] = dv_acc[...].astype(BF16)

    cols = [slice(p * LANES, (p + 1) * LANES) for p in range(pairs)]
    width, ncb = pairs * LANES, nhp // pairs
    qspec = pl.BlockSpec((tq, width), lambda b, p, i: (b * nq + i, p))
    sspec = pl.BlockSpec((seq, width), lambda b, p, i: (b, p))
    out = jax.ShapeDtypeStruct((T, D), BF16)
    carried = _Carried(comm, (batch, ncb, nq))
    return carried.pallas_call(
        body,
        in_specs=[qspec,
                  pl.BlockSpec((seq, width), lambda b, p, i: (b, ncb + p)),
                  pl.BlockSpec((seq, width), lambda b, p, i: (b, 2 * ncb + p)),
                  qspec, qspec],
        out_specs=[qspec, sspec, sspec], out_shape=[out, out, out],
        scratch_shapes=[pltpu.VMEM((pairs, 2 * tq, LANES), F32), pltpu.VMEM((seq, width), F32),
                        pltpu.VMEM((seq, width), F32)],
        args=[qkv, qkv, qkv, o, do], name=name)


_GELU_C = 0.7978845608028654
_GELU_A = 0.044715


def _gelu(x):
    xx = x * x
    a1 = 1.0 + jnp.tanh(x * (_GELU_C + (_GELU_C * _GELU_A) * xx))
    hx = 0.5 * x
    grad = a1 * (0.5 + (hx * (2.0 - a1)) * (_GELU_C + (3.0 * _GELU_C * _GELU_A) * xx))
    return hx * a1, grad


def _causal_ws(ws_ref, g):
    t = lax.broadcasted_iota(jnp.int32, (SGU_CHUNK, SGU_CHUNK), 0)
    s = lax.broadcasted_iota(jnp.int32, (SGU_CHUNK, SGU_CHUNK), 1)
    return jnp.where(s <= t, ws_ref[g], 0.0)


def _sgu_fwd(uv, gain, ws, bst, *, name):
    T, F2 = uv.shape
    F = F2 // 2
    C, G, W = SGU_CHUNK, SGU_GROUPS, SGU_GROUP_W

    def body(uv_ref, g_ref, ws_ref, bs_ref, y_ref):
        u, v = uv_ref[:, :F].astype(F32), uv_ref[:, F:].astype(F32)
        r = lax.rsqrt(jnp.mean(v * v, axis=-1, keepdims=True) + EPS)
        vn = (v * r * g_ref[...]).astype(BF16)
        for g in range(G):
            sl = slice(g * W, (g + 1) * W)
            mixed = _dot(_causal_ws(ws_ref, g).astype(BF16), vn[:, sl]) + bs_ref[:, g:g + 1]
            y_ref[:, sl] = (u[:, sl] * mixed).astype(BF16)

    return pl.pallas_call(
        body, grid=(T // C,),
        in_specs=[pl.BlockSpec((C, F2), lambda i: (i, 0)), pl.BlockSpec((1, F), lambda i: (0, 0)),
                  pl.BlockSpec((G, C, C), lambda i: (0, 0, 0)), pl.BlockSpec((C, G), lambda i: (0, 0))],
        out_specs=pl.BlockSpec((C, F), lambda i: (i, 0)),
        out_shape=jax.ShapeDtypeStruct((T, F), BF16),
        compiler_params=_params("parallel"), name=name)(uv, gain, ws, bst)


def _sgu_bwd(uv, dgelu, dy, gain, ws, bst, *, name):
    T, F2 = uv.shape
    F = F2 // 2
    C, G, W = SGU_CHUNK, SGU_GROUPS, SGU_GROUP_W
    nc = T // C

    def body(uv_ref, dgelu_ref, dy_ref, g_ref, ws_ref, bs_ref, duv_ref, dg_ref, dws_ref, dbs_ref,
             dg_acc, dws_acc, dbs_acc):
        i = pl.program_id(0)

        @pl.when(i == 0)
        def _():
            dg_acc[...] = jnp.zeros_like(dg_acc)
            dws_acc[...] = jnp.zeros_like(dws_acc)
            dbs_acc[...] = jnp.zeros_like(dbs_acc)

        u, v = uv_ref[:, :F].astype(F32), uv_ref[:, F:].astype(F32)
        dgelu = dgelu_ref[...].astype(F32)
        r = lax.rsqrt(jnp.mean(v * v, axis=-1, keepdims=True) + EPS)
        vhat = v * r
        gain_v = g_ref[...]
        vn = (vhat * gain_v).astype(BF16)
        dyv = dy_ref[...].astype(F32)
        lane8 = lax.broadcasted_iota(jnp.int32, (1, G), 1)
        dvn_parts = []
        dbs_new = jnp.zeros((C, G), F32)
        for g in range(G):
            sl = slice(g * W, (g + 1) * W)
            wsg = _causal_ws(ws_ref, g)
            mixed = _dot(wsg.astype(BF16), vn[:, sl]) + bs_ref[:, g:g + 1]
            duv_ref[:, sl] = (dyv[:, sl] * mixed * dgelu[:, sl]).astype(BF16)
            dmix = dyv[:, sl] * u[:, sl]
            dbs_new = dbs_new + jnp.where(lane8 == g, jnp.sum(dmix, axis=1, keepdims=True), 0.0)
            dmix_b = dmix.astype(BF16)
            dws_acc[g] += _dot_nt(dmix_b, vn[:, sl])
            dvn_parts.append(_dot(wsg.T.astype(BF16), dmix_b))
        dbs_acc[...] += dbs_new
        dvn = jnp.concatenate(dvn_parts, axis=1)
        dg_acc[...] += jnp.sum((dvn * vhat).reshape(C // SUBLANES, SUBLANES, F), axis=0)
        dvhat = dvn * gain_v
        dv = r * (dvhat - vhat * jnp.mean(dvhat * vhat, axis=-1, keepdims=True))
        duv_ref[:, F:] = (dv * dgelu[:, F:]).astype(BF16)

        @pl.when(i == nc - 1)
        def _():
            dg_ref[...] = jnp.sum(dg_acc[...], axis=0, keepdims=True)
            t = lax.broadcasted_iota(jnp.int32, (G, C, C), 1)
            s = lax.broadcasted_iota(jnp.int32, (G, C, C), 2)
            dws_ref[...] = jnp.where(s <= t, dws_acc[...], 0.0)
            dbs_ref[...] = dbs_acc[...]

    return pl.pallas_call(
        body, grid=(nc,),
        in_specs=[pl.BlockSpec((C, F2), lambda i: (i, 0)), pl.BlockSpec((C, F2), lambda i: (i, 0)),
                  pl.BlockSpec((C, F), lambda i: (i, 0)),
                  pl.BlockSpec((1, F), lambda i: (0, 0)), pl.BlockSpec((G, C, C), lambda i: (0, 0, 0)),
                  pl.BlockSpec((C, G), lambda i: (0, 0))],
        out_specs=[pl.BlockSpec((C, F2), lambda i: (i, 0)), pl.BlockSpec((1, F), lambda i: (0, 0)),
                   pl.BlockSpec((G, C, C), lambda i: (0, 0, 0)), pl.BlockSpec((C, G), lambda i: (0, 0))],
        out_shape=[jax.ShapeDtypeStruct((T, F2), BF16), jax.ShapeDtypeStruct((1, F), F32),
                   jax.ShapeDtypeStruct((G, C, C), F32), jax.ShapeDtypeStruct((C, G), F32)],
        scratch_shapes=[pltpu.VMEM((SUBLANES, F), F32), pltpu.VMEM((G, C, C), F32), pltpu.VMEM((C, G), F32)],
        compiler_params=_params("arbitrary"), name=name)(uv, dgelu, dy, gain, ws, bst)


def _my_place():
    return lax.axis_index("x"), lax.axis_index("y"), lax.axis_index("c")


def _all_gather(shards, *, name):
    nf = len(shards)
    items = [(lambda ins, f=f: ins[f], lambda outs, p, f=f: outs[f].at[:, p]) for f in range(nf)]
    targets = [jax.ShapeDtypeStruct((s.shape[0], N_DEV) + s.shape[1:], s.dtype) for s in shards]
    return _run_comm(_Comm(shards, targets, len(items), _gather_emit(items, [list(range(nf))])), name=name)


class _Comm:
    def __init__(self, sources, targets, n_items, emit):
        self.sources, self.targets, self.n_items, self.emit = list(sources), list(targets), n_items, emit
        self.filled = [t for t in self.targets if not isinstance(t, jax.ShapeDtypeStruct)]
        self.operands = self.sources + self.filled
        self.out_shapes = [jax.ShapeDtypeStruct(t.shape, t.dtype) for t in self.targets]
        self.sems = [pltpu.SemaphoreType.DMA((n_items, 7)), pltpu.SemaphoreType.DMA((n_items, 7)),
                     pltpu.SemaphoreType.DMA((n_items,))]

    def aliases(self, first_operand, first_result):
        pos = {id(t): k for k, t in enumerate(self.targets)}
        return {first_operand + len(self.sources) + a: first_result + pos[id(t)]
                for a, t in enumerate(self.filled)}


def _run_comm(comm, *, name):
    n_op, n_out = len(comm.operands), len(comm.targets)

    def body(*refs):
        comm.emit(0, 1, refs[:len(comm.sources)], refs[n_op:n_op + n_out], refs[n_op + n_out:])

    return pl.pallas_call(
        body, in_specs=[ANY] * n_op, out_specs=[ANY] * n_out, out_shape=comm.out_shapes,
        scratch_shapes=comm.sems, input_output_aliases=comm.aliases(0, 0), name=name)(*comm.operands)


def _at_steps(step, n_steps, phases):
    if n_steps == 1:
        for _, fn in phases:
            fn()
        return
    marks = {}
    for frac, fn in phases:
        marks.setdefault(min(int(frac * n_steps), n_steps - 1), []).append(fn)
    for mark, fns in sorted(marks.items()):
        @pl.when(step == mark)
        def _(fns=fns):
            for fn in fns:
                fn()


def _gather_emit(items, groups, fractions=None):
    if fractions is None:
        fractions = [(g + 1) / len(groups) for g in range(len(groups))]
    def emit(step, n_steps, ins, outs, sems):
        send_sems, recv_sems, local_sems = sems
        x, y, c = _my_place()
        me, sibling = (x, y, c), (x, y, 1 - c)
        chips = [(1 - x, y), (x, 1 - y), (1 - x, 1 - y)]

        def copy(i, k, block, to, own=False):
            src_of, dst_of = items[i]
            dst = dst_of(outs, 4 * block[0] + 2 * block[1] + block[2])
            return pltpu.make_async_remote_copy(
                src_ref=src_of(ins) if own else dst, dst_ref=dst,
                send_sem=send_sems.at[i, k], recv_sem=recv_sems.at[i, k],
                device_id=to, device_id_type=MESH)

        def local(i):
            src_of, dst_of = items[i]
            return pltpu.make_async_copy(src_of(ins), dst_of(outs, 4 * x + 2 * y + c), local_sems.at[i])

        def first(i):
            return [copy(i, 0, me, sibling, own=True)] + [
                copy(i, 1 + j, me, (*chip, c), own=True) for j, chip in enumerate(chips)]

        def start():
            for i in range(len(items)):
                local(i).start()
                for cp in first(i):
                    cp.start()

        def forward(group):
            for j, chip in enumerate(chips):
                for i in group:
                    copy(i, 1 + j, (*chip, c), me).wait_recv()
                    copy(i, 4 + j, (*chip, c), sibling).start()

        def finish():
            for i in range(len(items)):
                copy(i, 0, sibling, me).wait_recv()
                for j, chip in enumerate(chips):
                    copy(i, 4 + j, (*chip, 1 - c), me).wait_recv()
            for i in range(len(items)):
                for cp in first(i) + [copy(i, 4 + j, (*chip, c), sibling) for j, chip in enumerate(chips)]:
                    cp.wait_send()
                local(i).wait()

        phases = [(0.0, start)]
        for frac, group in zip(fractions, groups):
            phases.append((frac, functools.partial(forward, group)))
        phases.append((1.0, finish))
        _at_steps(step, n_steps, phases)

    return emit


def _exchange_emit(items):
    def emit(step, n_steps, ins, outs, sems):
        send_sems, recv_sems, local_sems = sems
        x, y, c = _my_place()
        me = 4 * x + 2 * y + c

        def peer_of(k):
            return x ^ (k >> 2), y ^ ((k >> 1) & 1), c ^ (k & 1)

        def copy(i, k):
            src_of, dst_of = items[i]
            px, py, pc = peer_of(k)
            return pltpu.make_async_remote_copy(
                src_ref=src_of(ins, 4 * px + 2 * py + pc), dst_ref=dst_of(outs, me),
                send_sem=send_sems.at[i, k - 1], recv_sem=recv_sems.at[i, k - 1],
                device_id=(px, py, pc), device_id_type=MESH)

        def arrival(i, k):
            src_of, dst_of = items[i]
            px, py, pc = peer_of(k)
            peer = 4 * px + 2 * py + pc
            return pltpu.make_async_remote_copy(
                src_ref=src_of(ins, peer), dst_ref=dst_of(outs, peer),
                send_sem=send_sems.at[i, k - 1], recv_sem=recv_sems.at[i, k - 1],
                device_id=(x, y, c), device_id_type=MESH)

        def local(i):
            src_of, dst_of = items[i]
            return pltpu.make_async_copy(src_of(ins, me), dst_of(outs, me), local_sems.at[i])

        def start():
            for i in range(len(items)):
                local(i).start()
            for k in range(1, N_DEV):
                for i in range(len(items)):
                    copy(i, k).start()

        def finish():
            for k in range(1, N_DEV):
                for i in range(len(items)):
                    arrival(i, k).wait_recv()
            for k in range(1, N_DEV):
                for i in range(len(items)):
                    copy(i, k).wait_send()
            for i in range(len(items)):
                local(i).wait()

        _at_steps(step, n_steps, [(0.0, start), (1.0, finish)])

    return emit


def _adam_math(g, w, m, v):
    m = ADAM_B1 * m + (1.0 - ADAM_B1) * g
    v = ADAM_B2 * v + (1.0 - ADAM_B2) * (g * g)
    m_hat = m / (1.0 - ADAM_B1 ** ADAM_STEP)
    v_hat = v / (1.0 - ADAM_B2 ** ADAM_STEP)
    delta = -ADAM_LR * (m_hat / (jnp.sqrt(v_hat) + ADAM_EPS) + ADAM_WD * w)
    return delta, m, v


def _sum_adamw(parts, w, m, v, *, name, tr=256):
    L, nd, R, C = parts.shape
    tr = min(tr, R)
    assert R % tr == 0

    def body(p_ref, w_ref, m_ref, v_ref, g_ref, d_ref, nm_ref, nv_ref):
        g = p_ref[0, 0].astype(F32)
        for q in range(1, nd):
            g = g + p_ref[0, q].astype(F32)
        d, nm, nv = _adam_math(g, w_ref[0], m_ref[0], v_ref[0])
        g_ref[0] = g
        d_ref[0] = d
        nm_ref[0] = nm
        nv_ref[0] = nv

    blk = pl.BlockSpec((1, tr, C), lambda l, i: (l, i, 0))
    out = jax.ShapeDtypeStruct((L, R, C), F32)
    return pl.pallas_call(
        body, grid=(L, R // tr),
        in_specs=[pl.BlockSpec((1, nd, tr, C), lambda l, i: (l, 0, i, 0)), blk, blk, blk],
        out_specs=[blk] * 4, out_shape=[out] * 4,
        compiler_params=_params("parallel", "parallel"), name=name)(parts, w, m, v)


def _sum_parts(parts, *, name):
    nd, R, C = parts.shape

    def body(p_ref, o_ref):
        g = p_ref[0]
        for q in range(1, nd):
            g = g + p_ref[q]
        o_ref[...] = g

    return pl.pallas_call(
        body, out_shape=jax.ShapeDtypeStruct((R, C), F32),
        in_specs=[pl.BlockSpec(memory_space=pltpu.VMEM)],
        out_specs=pl.BlockSpec(memory_space=pltpu.VMEM), name=name)(parts)


def _adamw_small(g, w, m, v, *, name):
    def body(g_ref, w_ref, m_ref, v_ref, d_ref, nm_ref, nv_ref):
        d, nm, nv = _adam_math(g_ref[...], w_ref[...], m_ref[...], v_ref[...])
        d_ref[...] = d
        nm_ref[...] = nm
        nv_ref[...] = nv

    vm = pl.BlockSpec(memory_space=pltpu.VMEM)
    out = jax.ShapeDtypeStruct(g.shape, F32)
    return pl.pallas_call(body, out_shape=[out] * 3, in_specs=[vm] * 4, out_specs=[vm] * 3,
                          name=name)(g, w, m, v)


def kernel(x, norm_mix, norm_mlp, sb_wqkv, sb_wo, sgu_win, sgu_gain, sgu_ws, sgu_bs, sgu_wout, mlp_w1, mlp_w2, final_norm, loss_target, m_norm_mix, m_norm_mlp, m_sb_wqkv, m_sb_wo, m_sgu_win, m_sgu_gain, m_sgu_ws, m_sgu_bs, m_sgu_wout, m_mlp_w1, m_mlp_w2, m_final_norm, v_norm_mix, v_norm_mlp, v_sb_wqkv, v_sb_wo, v_sgu_win, v_sgu_gain, v_sgu_ws, v_sgu_bs, v_sgu_wout, v_mlp_w1, v_mlp_w2, v_final_norm):
    batch, seq, D = x.shape
    T = batch * seq
    x0 = x.reshape(T, D)
    target = loss_target.reshape(T, D)

    WQKV, WO, WIN, WOUT, W1, W2, GAIN = range(7)
    big = [sb_wqkv, sb_wo, sgu_win, sgu_wout, mlp_w1, mlp_w2]
    shards = [w.astype(BF16) for w in big] + [sgu_gain[:, None, :]]
    rides = {
        "qkv0": [[(WO, 0), (W1, 0)]],
        "sb_fwd0": [[(W2, 0)], [(WIN, 0), (WOUT, 0), (W1, 1), (W2, 1)], [(WQKV, 1)]],
        "sb_fwd2": [[(WO, 1), (W1, 2), (W2, 2)], [(WIN, 1), (WOUT, 1), (W1, 3), (W2, 3)]],
    }

    first = [(lambda ins: ins[WQKV].at[0], lambda outs, p: outs[WQKV].at[0, p]),
             (lambda ins: ins[GAIN], lambda outs, p: outs[GAIN].at[:, p])]
    targets = [jax.ShapeDtypeStruct((s.shape[0], N_DEV) + s.shape[1:], s.dtype) for s in shards]
    gathered0 = _run_comm(_Comm(shards, targets, len(first), _gather_emit(first, [[0, 1]])), name="gather_first")
    gain_sgu = gathered0[GAIN].reshape(-1, 1, SGU_FFN)
    gw = dict(enumerate(gathered0[:GAIN]))

    def gather_ride(call):
        wanted = [fl for group in rides[call] for fl in group]
        fams = sorted({f for f, _ in wanted})
        items = [(lambda ins, f=f, l=l: ins[f].at[l], lambda outs, p, t=fams.index(f), l=l: outs[t].at[l, p])
                 for f, l in wanted]
        groups, sent, k = [], [], 0
        for group in rides[call]:
            groups.append(list(range(k, k + len(group))))
            sent.append(sum(shards[f].shape[1] * shards[f].shape[2] for f, _ in group) + (sent[-1] if sent else 0))
            k += len(group)
        fractions = [s / sent[-1] for s in sent]
        return _Comm(shards[:GAIN], [gw[f] for f in fams], len(items), _gather_emit(items, groups, fractions)), fams

    def weight(f):
        g = gw[f]
        return g if f in (WQKV, WIN, W1) else g.reshape(g.shape[0], 1, N_DEV * g.shape[2], g.shape[3])

    saved = []
    xs = x0
    for i in range(DEPTH):
        j = i // 2
        if i % 2 == 0:
            if i == 0:
                comm, fams = gather_ride("qkv0")
                (qkv, h), results = _mm_nn(xs, weight(WQKV), j, out_dtype=BF16, gain=norm_mix[i:i + 1],
                                           name=f"qkv{i}", comm=comm, tm=1024)
                gw.update(zip(fams, results))
            else:
                qkv, h = _mm_nn(xs, weight(WQKV), j, out_dtype=BF16, gain=norm_mix[i:i + 1], name=f"qkv{i}", tm=1024)
            comm, fams = gather_ride(f"sb_fwd{i}")
            o, results = _sb_fwd(qkv, batch=batch, seq=seq, name=f"sb_fwd{i}", comm=comm)
            gw.update(zip(fams, results))
            x_mid = _mm_nn(o, weight(WO), j, out_dtype=F32, res=xs, name=f"wo{i}", tm=1024)
            mix = (h, qkv, o)
        else:
            gain_j = gain_sgu[j]
            bst = sgu_bs[j].T
            uv, h, dgelu = _mm_nn(xs, weight(WIN), j, out_dtype=BF16, gain=norm_mix[i:i + 1], gelu=True,
                                  name=f"win{i}")
            yv = _sgu_fwd(uv, gain_j, sgu_ws[j], bst, name=f"sgu_fwd{i}")
            x_mid = _mm_nn(yv, weight(WOUT), j, out_dtype=F32, res=xs, name=f"wout{i}", tm=1024)
            mix = (h, uv, dgelu, yv, gain_j, bst)
        a, h2 = _mm_nn(x_mid, weight(W1), i, out_dtype=BF16, gain=norm_mlp[i:i + 1], name=f"w1_{i}", tm=1024)
        x_out = _mm_nn(a, weight(W2), i, out_dtype=F32, res=x_mid, a_act="relu2", name=f"w2_{i}")
        saved.append((xs, mix, x_mid, h2, a))
        xs = x_out
    g_wqkv, g_wo, g_win, g_wout, g_w1, g_w2 = [weight(f) for f in range(GAIN)]

    dx, dxb, sq, d_final = _loss_head(xs, final_norm.reshape(1, D), target, name="loss_head")
    loss = lax.psum(0.5 * jnp.sum(sq) / D, ("x", "y", "c"))

    SMALL = GAIN
    stacks = {f: jax.ShapeDtypeStruct((w.shape[0], N_DEV) + w.shape[1:], BF16) for f, w in enumerate(big)}
    pending = []

    def row_shards(p):
        return p.reshape(N_DEV, p.shape[1] // N_DEV, p.shape[2])

    def exchange_pending():
        fams = sorted({f for _, f, _ in pending})
        items = [(lambda ins, p, a=a: ins[a].at[p], lambda outs, q, t=fams.index(f), l=l: outs[t].at[l, q])
                 for a, (_, f, l) in enumerate(pending)]
        comm = _Comm([part for part, _, _ in pending], [stacks[f] for f in fams], len(items), _exchange_emit(items))
        pending.clear()
        return comm, fams

    d_norm_mix, d_norm_mlp = [None] * DEPTH, [None] * DEPTH
    d_gain, d_ws, d_bs = [None] * 2, [None] * 2, [None] * 2
    for i in reversed(range(DEPTH)):
        j = i // 2
        xs, mix, x_mid, h2, a = saved[i]
        da = _mm_nt(dxb, g_w2, i, out_dtype=BF16, act_src=a, name=f"d_a{i}")
        pending.append((row_shards(_mm_tn(a, dxb, shards=1, a_act="relu2", name=f"d_w2_{i}")), W2, i))
        pending.append((_mm_tn(h2, da, shards=N_DEV, name=f"d_w1_{i}", pb=4, tm=1024), W1, i))
        (dx, dxb, d_norm_mlp[i]), _ = _mm_nt_rms_bwd(da, g_w1, i, x_mid, norm_mlp[i:i + 1], dx, name=f"d_h2_{i}")
        if i % 2 == 0:
            h, qkv, o = mix
            do = _mm_nt(dxb, g_wo, j, out_dtype=BF16, name=f"d_o{i}", tm=1024)
            pending.append((row_shards(_mm_tn(o, dxb, shards=1, name=f"d_wo{i}")), WO, j))
            comm, fams = exchange_pending()
            (dq, dk, dv), results = _sb_bwd(qkv, o, do, batch=batch, seq=seq, name=f"sb_bwd{i}", comm=comm)
            stacks.update(zip(fams, results))
            dqkv = jnp.concatenate([dq, dk, dv], axis=1)
            pending.append((_mm_tn(h, dqkv, shards=N_DEV, name=f"d_wqkv{i}", pb=4, tm=1024), WQKV, j))
            mixer_in, w_in = dqkv, g_wqkv
        else:
            h, uv, dgelu, yv, gain_j, bst = mix
            dy = _mm_nt(dxb, g_wout, j, out_dtype=BF16, name=f"d_y{i}", tm=1024)
            pending.append((row_shards(_mm_tn(yv, dxb, shards=1, name=f"d_wout{i}")), WOUT, j))
            duv, d_gain[j], d_ws[j], dbst = _sgu_bwd(uv, dgelu, dy, gain_j, sgu_ws[j], bst, name=f"sgu_bwd{i}")
            d_bs[j] = dbst.T
            pending.append((_mm_tn(h, duv, shards=N_DEV, name=f"d_win{i}", pb=4, tm=1024), WIN, j))
            mixer_in, w_in = duv, g_win
        comm, fams = exchange_pending() if i == 0 else (None, [])
        (dx, dxb, d_norm_mix[i]), results = _mm_nt_rms_bwd(
            mixer_in, w_in, j, xs, norm_mix[i:i + 1], dx, name=f"d_h_mix{i}", comm=comm)
        stacks.update(zip(fams, results))
    grad_x = dx.reshape(batch, seq, D)

    small = [jnp.concatenate(d_norm_mix, 0), jnp.concatenate(d_norm_mlp, 0), d_final,
             jnp.concatenate(d_gain, 0), jnp.stack(d_bs, 0), jnp.stack(d_ws, 0)]
    small_flat = jnp.concatenate([s.reshape(-1) for s in small])
    n_small = small_flat.shape[0]
    small_rows = -(-n_small // (N_DEV * SUBLANES * LANES)) * SUBLANES
    small_flat = jnp.pad(small_flat, (0, N_DEV * small_rows * LANES - n_small))
    stacks[SMALL] = jax.ShapeDtypeStruct((1, N_DEV, small_rows, LANES), F32)
    pending.append((small_flat.reshape(N_DEV, small_rows, LANES), SMALL, 0))
    comm, fams = exchange_pending()
    stacks.update(zip(fams, _run_comm(comm, name="exchange_last")))
    r_wqkv, r_wo, r_win, r_wout, r_w1, r_w2, r_small = [stacks[f] for f in range(SMALL + 1)]

    u_wqkv = _sum_adamw(r_wqkv, sb_wqkv, m_sb_wqkv, v_sb_wqkv, name="adamw_wqkv")
    u_wo = _sum_adamw(r_wo, sb_wo, m_sb_wo, v_sb_wo, name="adamw_wo")
    u_win = _sum_adamw(r_win, sgu_win, m_sgu_win, v_sgu_win, name="adamw_win")
    u_wout = _sum_adamw(r_wout, sgu_wout, m_sgu_wout, v_sgu_wout, name="adamw_wout")
    u_w1 = _sum_adamw(r_w1, mlp_w1, m_mlp_w1, v_mlp_w1, name="adamw_w1")
    u_w2 = _sum_adamw(r_w2, mlp_w2, m_mlp_w2, v_mlp_w2, name="adamw_w2")

    small_sum = _sum_parts(r_small[0], name="sum_small")
    g_small = _all_gather([small_sum[None]], name="gather_small")[0].reshape(-1)[:n_small]

    shapes = [s.shape for s in small]
    sizes = [s.size for s in small]
    offs = [sum(sizes[:k]) for k in range(len(sizes))]
    me = 4 * lax.axis_index("x") + 2 * lax.axis_index("y") + lax.axis_index("c")
    shard_w = SGU_FFN // N_DEV

    def pack(arrs):
        flat = jnp.concatenate([a_.reshape(-1) for a_ in arrs])
        return jnp.pad(flat, (0, N_DEV * small_rows * LANES - n_small)).reshape(-1, LANES)

    def full_gain(gshard):
        return lax.dynamic_update_slice(jnp.zeros((2, SGU_FFN), F32), gshard, (0, me * shard_w))

    w_small = pack([norm_mix, norm_mlp, final_norm, full_gain(sgu_gain), sgu_bs, sgu_ws])
    m_small = pack([m_norm_mix, m_norm_mlp, m_final_norm, full_gain(m_sgu_gain), m_sgu_bs, m_sgu_ws])
    v_small = pack([v_norm_mix, v_norm_mlp, v_final_norm, full_gain(v_sgu_gain), v_sgu_bs, v_sgu_ws])
    g_pack = jnp.pad(g_small, (0, N_DEV * small_rows * LANES - n_small)).reshape(-1, LANES)
    sm = [g_pack] + list(_adamw_small(g_pack, w_small, m_small, v_small, name="adamw_small"))

    def unpack(flat2d):
        flat = flat2d.reshape(-1)
        out = [flat[offs[k]:offs[k] + sizes[k]].reshape(shapes[k]) for k in range(len(sizes))]
        out[2] = out[2].reshape(D)
        out[3] = lax.dynamic_slice(out[3], (0, me * shard_w), (2, shard_w))
        return out

    outs = []
    for k, big_u in enumerate(zip(u_wqkv, u_wo, u_win, u_wout, u_w1, u_w2)):
        s_nm, s_nl, s_fn, s_gain, s_bs, s_ws = unpack(sm[k])
        b_wqkv, b_wo, b_win, b_wout, b_w1, b_w2 = big_u
        outs += [s_nm, s_nl, b_wqkv, b_wo, b_win, s_gain, s_ws, s_bs, b_wout, b_w1, b_w2, s_fn]
    return (loss, grad_x, *outs)
```

```python
import functools

import jax
import jax.numpy as jnp
from jax import lax
from jax.experimental import pallas as pl
from jax.experimental.pallas import tpu as pltpu

F32 = jnp.float32
BF16 = jnp.bfloat16

N_DEV = 8
D_MODEL = 1024
SEQ = 2048
DEPTH = 4
SB_HEAD_DIM = 64
SGU_CHUNK = 128
SGU_FFN = 2 * D_MODEL
SGU_GROUPS = 8
SGU_GROUP_W = SGU_FFN // SGU_GROUPS
EPS = 1e-6

ADAM_LR = 0.001
ADAM_B1 = 0.9
ADAM_B2 = 0.999
ADAM_EPS = 1e-08
ADAM_WD = 0.01
ADAM_STEP = 10

MXU_TILE = 256
LANES = 128
SUBLANES = 8
VMEM_LIMIT = 56 * 1024 * 1024
EXP_ZERO_BELOW = -104.0
SB_TILE = 256
SB_STRAIGHT = 2
SB_PAIRS = 4

MESH = pl.DeviceIdType.MESH
ANY = pl.BlockSpec(memory_space=pl.ANY)


def _params(*sem):
    return pltpu.CompilerParams(dimension_semantics=sem, vmem_limit_bytes=VMEM_LIMIT)


def _dot(a, b):
    return jnp.dot(a, b, preferred_element_type=F32)


def _dot_nt(a, b):
    return lax.dot_general(a, b, (((1,), (1,)), ((), ())), preferred_element_type=F32)


def _dot_tn(a, b):
    return lax.dot_general(a, b, (((0,), (0,)), ((), ())), preferred_element_type=F32)


def _split_bf16(x):
    hi = x.astype(BF16)
    lo = (x - hi.astype(F32)).astype(BF16)
    return hi, lo


def _relu2(av):
    t = jnp.maximum(av, jnp.zeros_like(av))
    return t * t


def _mm_nn(a, w, l, *, out_dtype, name, res=None, a_act=None, gain=None, gelu=False, comm=None,
           tm=512, kc=1024):
    M, K = a.shape
    _, P, K2, n = w.shape
    assert K2 == K
    tm, kc = min(tm, M), min(kc, K)
    assert M % tm == 0 and K % kc == 0
    join = 2 if n % MXU_TILE and P % 2 == 0 else 1

    n_in = 2 + (gain is not None) + (res is not None)
    n_out = 1 + (gain is not None) + (gelu is True)

    def body(*all_refs):
        ins, outs, _ = carried.split(all_refs, n_in, n_out, 0)
        carried.emit(all_refs)
        refs = list(ins) + list(outs)
        a_ref, w_ref, o_ref = refs[0], refs[1], refs[n_in]
        if gain is not None:
            xv = a_ref[...]
            r = lax.rsqrt(jnp.mean(xv * xv, axis=-1, keepdims=True) + EPS)
            h = (xv * r * refs[2][...]).astype(BF16)
            refs[n_in + 1][...] = h
        for p in range(0, P, join):
            sl = slice(p * n, (p + join) * n)
            acc = None
            for k0 in range(0, K, kc):
                if gain is not None:
                    av = h[:, k0:k0 + kc]
                else:
                    av = a_ref[:, k0:k0 + kc]
                    av = _relu2(av) if a_act == "relu2" else av.astype(BF16)
                wv = [w_ref[0, p + s, k0:k0 + kc, :] for s in range(join)]
                d = _dot(av, wv[0] if join == 1 else jnp.concatenate(wv, axis=1))
                acc = d if acc is None else acc + d
            if res is not None:
                acc = acc + refs[n_in - 1][:, sl]
            if gelu:
                acc, dact = _gelu(acc)
                refs[-1][:, sl] = dact.astype(BF16)
            o_ref[:, sl] = acc.astype(out_dtype)

    row = lambda width: pl.BlockSpec((tm, width), lambda i: (i, 0))
    in_specs = [row(K), pl.BlockSpec((1, P, K, n), lambda i: (l, 0, 0, 0), pipeline_mode=pl.Buffered(1))]
    args = [a, w]
    if gain is not None:
        in_specs.append(pl.BlockSpec((1, K), lambda i: (0, 0)))
        args.append(gain)
    if res is not None:
        in_specs.append(row(P * n))
        args.append(res)
    out_specs, out_shape = [row(P * n)], [jax.ShapeDtypeStruct((M, P * n), out_dtype)]
    if gain is not None:
        out_specs.append(row(K))
        out_shape.append(jax.ShapeDtypeStruct((M, K), BF16))
    if gelu:
        out_specs.append(row(P * n))
        out_shape.append(jax.ShapeDtypeStruct((M, P * n), BF16))
    carried = _Carried(comm, (M // tm,), last="parallel")
    outs, comm_results = carried.pallas_call(
        body, in_specs=in_specs, out_specs=out_specs, out_shape=out_shape, scratch_shapes=[],
        args=args, name=name)
    outs = tuple(outs) if n_out > 1 else outs[0]
    return outs if comm is None else (outs, comm_results)


def _mm_nt(a, w, l, *, out_dtype, name, act_src=None, tm=512, tn=1024):
    M, K = a.shape
    _, P, Nout, kc = w.shape
    assert P == 1 and K == kc
    tm, tn = min(tm, M), min(tn, Nout)
    assert M % tm == 0 and Nout % tn == 0

    def body(*refs):
        a_ref, w_ref, o_ref = refs[0], refs[1], refs[-1]
        av = a_ref[...].astype(BF16)
        for n0 in range(0, Nout, tn):
            r = _dot_nt(av, w_ref[0, 0, n0:n0 + tn, :])
            if act_src is not None:
                r = r * (2.0 * jnp.maximum(refs[2][:, n0:n0 + tn].astype(F32), 0.0))
            o_ref[:, n0:n0 + tn] = r.astype(out_dtype)

    row = lambda width: pl.BlockSpec((tm, width), lambda i: (i, 0))
    in_specs = [row(K), pl.BlockSpec((1, 1, Nout, K), lambda i: (l, 0, 0, 0), pipeline_mode=pl.Buffered(1))]
    args = [a, w]
    if act_src is not None:
        in_specs.append(row(Nout))
        args.append(act_src)
    return pl.pallas_call(
        body, grid=(M // tm,), in_specs=in_specs, out_specs=row(Nout),
        out_shape=jax.ShapeDtypeStruct((M, Nout), out_dtype),
        compiler_params=_params("parallel"), name=name)(*args)


def _mm_tn(a, b, *, shards, name, a_act=None, tm=2048, tk=1024, pb=1):
    M, K = a.shape
    M2, N = b.shape
    assert M2 == M
    n = N // shards
    tm, tk = min(tm, M), min(tk, K)
    assert M % tm == 0 and K % tk == 0 and shards % pb == 0
    width, nm = pb * n, M // tm

    def body(a_ref, b_ref, o_ref, acc):
        m = pl.program_id(2)
        av = a_ref[...]
        if a_act == "relu2":
            av = _relu2(av)

        @pl.when(m == 0)
        def _():
            acc[...] = jnp.zeros_like(acc)

        acc[...] += _dot_tn(av.astype(BF16), b_ref[...].astype(BF16))

        @pl.when(m == nm - 1)
        def _():
            for p in range(pb):
                o_ref[p] = acc[:, p * n:(p + 1) * n].astype(BF16)

    return pl.pallas_call(
        body, grid=(K // tk, N // width, nm),
        in_specs=[pl.BlockSpec((tm, tk), lambda i, j, m: (m, i)),
                  pl.BlockSpec((tm, width), lambda i, j, m: (m, j))],
        out_specs=pl.BlockSpec((pb, tk, n), lambda i, j, m: (j, i, 0)),
        out_shape=jax.ShapeDtypeStruct((shards, K, n), BF16),
        scratch_shapes=[pltpu.VMEM((tk, width), F32)],
        compiler_params=_params("parallel", "parallel", "arbitrary"), name=name)(a, b)


def _mm_nt_rms_bwd(a, w, l, x, gain, dres, *, name, comm=None, tm=512):
    M, K = a.shape
    _, P, D, kc = w.shape
    assert K == P * kc and x.shape == (M, D)
    tm = min(tm, M)
    nr = M // tm
    join = 2 if kc % MXU_TILE and P % 2 == 0 else 1

    def body(*refs):
        (a_ref, w_ref, x_ref, g_ref, dres_ref), (dx_ref, dxb_ref, dg_ref), (acc,) = carried.split(refs, 5, 3, 1)
        i = pl.program_id(0)
        carried.emit(refs)
        dhv = None
        for p in range(0, P, join):
            wv = [w_ref[0, p + s] for s in range(join)]
            d = _dot_nt(a_ref[:, p * kc:(p + join) * kc], wv[0] if join == 1 else jnp.concatenate(wv, axis=1))
            dhv = d if dhv is None else dhv + d
        xv = x_ref[...]
        r = lax.rsqrt(jnp.mean(xv * xv, axis=-1, keepdims=True) + EPS)
        xhat = xv * r
        dxhat = dhv * g_ref[...]
        dx = dres_ref[...] + r * (dxhat - xhat * jnp.mean(dxhat * xhat, axis=-1, keepdims=True))
        dx_ref[...] = dx
        dxb_ref[...] = dx.astype(BF16)
        part = jnp.sum((dhv * xhat).reshape(tm // SUBLANES, SUBLANES, D), axis=0)

        @pl.when(i == 0)
        def _():
            acc[...] = jnp.zeros_like(acc)

        acc[...] += part

        @pl.when(i == nr - 1)
        def _():
            dg_ref[...] = jnp.sum(acc[...], axis=0, keepdims=True)

    row = pl.BlockSpec((tm, D), lambda i: (i, 0))
    vec = pl.BlockSpec((1, D), lambda i: (0, 0))
    carried = _Carried(comm, (nr,))
    return carried.pallas_call(
        body,
        in_specs=[pl.BlockSpec((tm, K), lambda i: (i, 0)),
                  pl.BlockSpec((1, P, D, kc), lambda i: (l, 0, 0, 0), pipeline_mode=pl.Buffered(1)),
                  row, vec, row],
        out_specs=[row, row, vec],
        out_shape=[jax.ShapeDtypeStruct((M, D), F32), jax.ShapeDtypeStruct((M, D), BF16),
                   jax.ShapeDtypeStruct((1, D), F32)],
        scratch_shapes=[pltpu.VMEM((SUBLANES, D), F32)],
        args=[a, w, x, gain, dres], name=name)


def _loss_head(x, gain, target, *, name, tr=512):
    T, D = x.shape
    nr = T // tr

    def body(x_ref, g_ref, t_ref, dx_ref, dxb_ref, sq_ref, dg_ref, sq_acc, dg_acc):
        i = pl.program_id(0)
        xv = x_ref[...]
        g = g_ref[...]
        r = lax.rsqrt(jnp.mean(xv * xv, axis=-1, keepdims=True) + EPS)
        xhat = xv * r
        err = xhat * g - t_ref[...]
        dy = err * (1.0 / D)
        dxhat = dy * g
        dx = r * (dxhat - xhat * jnp.mean(dxhat * xhat, axis=-1, keepdims=True))
        dx_ref[...] = dx
        dxb_ref[...] = dx.astype(BF16)
        sq = jnp.sum((err * err).reshape(tr // SUBLANES, SUBLANES, D), axis=0)
        dg = jnp.sum((dy * xhat).reshape(tr // SUBLANES, SUBLANES, D), axis=0)

        @pl.when(i == 0)
        def _():
            sq_acc[...] = sq
            dg_acc[...] = dg

        @pl.when(i > 0)
        def _():
            sq_acc[...] += sq
            dg_acc[...] += dg

        @pl.when(i == nr - 1)
        def _():
            sq_ref[...] = sq_acc[...]
            dg_ref[...] = jnp.sum(dg_acc[...], axis=0, keepdims=True)

    row = pl.BlockSpec((tr, D), lambda i: (i, 0))
    vec = pl.BlockSpec((1, D), lambda i: (0, 0))
    part = pl.BlockSpec((SUBLANES, D), lambda i: (0, 0))
    return pl.pallas_call(
        body, grid=(nr,), in_specs=[row, vec, row], out_specs=[row, row, part, vec],
        out_shape=[jax.ShapeDtypeStruct((T, D), F32), jax.ShapeDtypeStruct((T, D), BF16),
                   jax.ShapeDtypeStruct((SUBLANES, D), F32), jax.ShapeDtypeStruct((1, D), F32)],
        scratch_shapes=[pltpu.VMEM((SUBLANES, D), F32), pltpu.VMEM((SUBLANES, D), F32)],
        compiler_params=_params("arbitrary"), name=name)(x, gain, target)


def _head0_lanes():
    return lax.broadcasted_iota(jnp.int32, (1, LANES), 1) < SB_HEAD_DIM


def _stack_heads(x):
    zero = jnp.zeros_like(x)
    h0 = _head0_lanes()
    return jnp.concatenate([jnp.where(h0, x, zero), jnp.where(h0, zero, x)], axis=0)


def _unstack_heads(y, tq):
    return jnp.where(_head0_lanes(), y[:tq], y[tq:])


def _past_mask(tq, tk):
    row = lax.broadcasted_iota(jnp.int32, (2 * tq, tk), 0) & (tq - 1)
    col = lax.broadcasted_iota(jnp.int32, (2 * tq, tk), 1)
    return col < row


def _max_of(arrays):
    return functools.reduce(jnp.maximum, [jnp.max(a) for a in arrays])


def _sb_logs(z, past):
    minus_abs = lax.bitcast_convert_type(
        lax.bitcast_convert_type(z, jnp.uint32) | jnp.uint32(0x80000000), F32)
    log_beta = jnp.minimum(z, 0.0) - jnp.log(1.0 + jnp.exp(minus_abs))
    l = log_beta - z
    if past is not None:
        l = jnp.where(past, l, 0.0)
    return log_beta, l


def _suffix_matrix(tk):
    j = lax.broadcasted_iota(jnp.int32, (2 * tk, tk), 0) & (tk - 1)
    s = lax.broadcasted_iota(jnp.int32, (2 * tk, tk), 1)
    return (j > s).astype(BF16)


def _suffix_sum(x, u2, exact=True):
    if not exact:
        return _dot(x.astype(BF16), u2[:x.shape[1]])
    hi, lo = _split_bf16(x)
    return _dot(jnp.concatenate([hi, lo], axis=1), u2)


class _Carried:
    def __init__(self, comm, grid, last="arbitrary"):
        self.comm, self.grid, self.last = comm, grid, last
        self.n_op = len(comm.operands) if comm else 0
        self.n_tgt = len(comm.targets) if comm else 0

    def split(self, refs, n_in, n_out, n_scratch):
        self.n_in, self.n_out, self.n_scratch = n_in, n_out, n_scratch
        a = n_in + self.n_op
        b = a + n_out + self.n_tgt
        return refs[:n_in], refs[a:a + n_out], refs[b:b + n_scratch]

    def emit(self, refs):
        if self.comm is None:
            return
        step, n_steps = 0, 1
        for d, size in enumerate(self.grid):
            step = step * size + pl.program_id(d)
            n_steps *= size
        a = self.n_in + self.n_op + self.n_out
        self.comm.emit(step, n_steps, refs[self.n_in:self.n_in + len(self.comm.sources)],
                       refs[a:a + self.n_tgt], refs[a + self.n_tgt + self.n_scratch:])

    def pallas_call(self, body, *, in_specs, out_specs, out_shape, scratch_shapes, args, name):
        comm = self.comm
        n_in, n_out = len(in_specs), len(out_specs)
        aliases = {}
        if comm is not None:
            in_specs = in_specs + [ANY] * self.n_op
            out_specs = out_specs + [ANY] * self.n_tgt
            out_shape = out_shape + comm.out_shapes
            scratch_shapes = scratch_shapes + comm.sems
            args = args + comm.operands
            aliases = comm.aliases(n_in, n_out)
        sem = ("parallel",) * (len(self.grid) - 1) + (self.last,) if comm is None else ("arbitrary",) * len(self.grid)
        results = pl.pallas_call(
            body, grid=self.grid, in_specs=in_specs, out_specs=out_specs, out_shape=out_shape,
            scratch_shapes=scratch_shapes, input_output_aliases=aliases,
            compiler_params=_params(*sem), name=name)(*args)
        return results[:n_out], results[n_out:]


def _sb_fwd(qkv, *, batch, seq, name, comm=None, tq=SB_TILE, n_pre=SB_STRAIGHT, pairs=SB_PAIRS):
    T, D3 = qkv.shape
    D = D3 // 3
    nhp = D // LANES
    tk = tq
    nq = seq // tq
    scale = SB_HEAD_DIM ** -0.5
    assert 1 <= n_pre <= nq and nhp % pairs == 0

    def body(*refs):
        (q_ref, k_ref, v_ref), (o_ref,), (acc,) = carried.split(refs, 3, 1, 1)
        qi = pl.program_id(2)
        carried.emit(refs)
        qs = [_stack_heads(q_ref[:, lanes]) * scale for lanes in cols]
        past = _past_mask(tq, tk)
        u = _suffix_matrix(tk)

        def block(kb, g, c, diag):
            ks = pl.multiple_of(kb * tk, tk)
            z = _dot_nt(qs[g], k_ref[pl.ds(ks, tk), cols[g]])
            log_beta, l = _sb_logs(z, past if diag else None)
            arg = log_beta + _suffix_sum(l, u, exact=False)
            a = jnp.exp(arg if c is None else arg + c)
            if diag:
                a = jnp.where(past, a, 0.0)
            return _dot(a.astype(BF16), v_ref[pl.ds(ks, tk), cols[g]]), jnp.sum(l, axis=1, keepdims=True)

        def straight(n):
            o_sum, c = [None] * pairs, [None] * pairs
            for b in range(n):
                for g in range(pairs):
                    o_b, c_b = block(qi - b, g, c[g], b == 0)
                    o_sum[g] = o_b if b == 0 else o_sum[g] + o_b
                    c[g] = c_b if b == 0 else c[g] + c_b
            return o_sum, c

        def finish(o_sum):
            for g in range(pairs):
                o_ref[:, cols[g]] = _unstack_heads(o_sum[g], tq)

        for n in range(1, n_pre):
            @pl.when(qi == n - 1)
            def _(n=n):
                finish(straight(n)[0])

        @pl.when(qi >= n_pre - 1)
        def _():
            o_sum, c = straight(n_pre)
            for g in range(pairs):
                acc[g] = o_sum[g]

            def cond(st):
                kb, c = st
                return jnp.logical_and(kb >= 0, _max_of(c) > EXP_ZERO_BELOW)

            def step(st):
                kb, c = st
                new_c = []
                for g in range(pairs):
                    o_n, c_n = block(kb, g, c[g], False)
                    acc[g] += o_n
                    new_c.append(c[g] + c_n)
                return kb - 1, tuple(new_c)

            lax.while_loop(cond, step, (qi - n_pre, tuple(c)))
            finish([acc[g] for g in range(pairs)])

    cols = [slice(g * LANES, (g + 1) * LANES) for g in range(pairs)]
    width = pairs * LANES
    carried = _Carried(comm, (batch, nhp // pairs, nq))
    (o,), comm_results = carried.pallas_call(
        body,
        in_specs=[pl.BlockSpec((tq, width), lambda b, p, i: (b * nq + i, p)),
                  pl.BlockSpec((seq, width), lambda b, p, i: (b, nhp // pairs + p)),
                  pl.BlockSpec((seq, width), lambda b, p, i: (b, 2 * (nhp // pairs) + p))],
        out_specs=[pl.BlockSpec((tq, width), lambda b, p, i: (b * nq + i, p))],
        out_shape=[jax.ShapeDtypeStruct((T, D), F32)],
        scratch_shapes=[pltpu.VMEM((pairs, 2 * tq, LANES), F32)],
        args=[qkv, qkv, qkv], name=name)
    return o, comm_results


def _sb_bwd(qkv, o, do, *, batch, seq, name, comm=None, tq=SB_TILE, n_pre=SB_STRAIGHT, pairs=SB_PAIRS):
    T, D3 = qkv.shape
    D = D3 // 3
    nhp = D // LANES
    tk = tq
    nq = seq // tq
    scale = SB_HEAD_DIM ** -0.5

    def body(*refs):
        ins, outs, scratch = carried.split(refs, 5, 3, 3)
        q_ref, k_ref, v_ref, o_ref, do_ref = ins
        dq_ref, dk_ref, dv_ref = outs
        dq_acc, dk_acc, dv_acc = scratch
        qi = pl.program_id(2)
        carried.emit(refs)

        @pl.when(qi == 0)
        def _():
            dk_acc[...] = jnp.zeros_like(dk_acc)
            dv_acc[...] = jnp.zeros_like(dv_acc)

        qs = [_stack_heads(q_ref[:, lanes]) * scale for lanes in cols]
        dos = [_stack_heads(do_ref[:, lanes]) for lanes in cols]
        dsum = [jnp.sum(_stack_heads(do_ref[:, lanes].astype(F32) * o_ref[:, lanes]), axis=1, keepdims=True)
                for lanes in cols]
        past = _past_mask(tq, tk)
        u = _suffix_matrix(tk)

        def block(kb, p, c, gc, diag):
            ks = pl.multiple_of(kb * tk, tk)
            kblk = k_ref[pl.ds(ks, tk), cols[p]]
            vblk = v_ref[pl.ds(ks, tk), cols[p]]
            z = _dot_nt(qs[p], kblk)
            log_beta, l = _sb_logs(z, past if diag else None)
            arg = log_beta + _suffix_sum(l, u, exact=False)
            a = jnp.exp(arg if c is None else arg + c)
            if diag:
                a = jnp.where(past, a, 0.0)
            a = a.astype(BF16)
            beta = 1.0 - jnp.exp(l)
            g = a.astype(F32) * _dot_nt(dos[p], vblk)
            gs = _suffix_sum(g, u)
            dz = g - beta * (dsum[p] - (gs if gc is None else gs + gc))
            if diag:
                dz = jnp.where(past, dz, 0.0)
            dzb = dz.astype(BF16)
            dk_acc[pl.ds(ks, tk), cols[p]] += _dot_tn(dzb, qs[p])
            dv_acc[pl.ds(ks, tk), cols[p]] += _dot_tn(a, dos[p])
            return (_dot(dzb, kblk), jnp.sum(l, axis=1, keepdims=True),
                    jnp.sum(g, axis=1, keepdims=True))

        def finish(dq_sum):
            for p in range(pairs):
                dq_ref[:, cols[p]] = (_unstack_heads(dq_sum[p], tq) * scale).astype(BF16)

        def straight(n):
            dq_sum, c, gc = [None] * pairs, [None] * pairs, [None] * pairs
            for b in range(n):
                for p in range(pairs):
                    dq_b, c_b, g_b = block(qi - b, p, c[p], gc[p], b == 0)
                    dq_sum[p] = dq_b if b == 0 else dq_sum[p] + dq_b
                    c[p] = c_b if b == 0 else c[p] + c_b
                    gc[p] = g_b if b == 0 else gc[p] + g_b
            return dq_sum, c, gc

        for n in range(1, n_pre):
            @pl.when(qi == n - 1)
            def _(n=n):
                finish(straight(n)[0])

        @pl.when(qi >= n_pre - 1)
        def _():
            dq_sum, c, gc = straight(n_pre)
            for p in range(pairs):
                dq_acc[p] = dq_sum[p]

            def cond(st):
                kb, c, gc = st
                return jnp.logical_and(kb >= 0, _max_of(c) > EXP_ZERO_BELOW)

            def step(st):
                kb, c, gc = st
                new_c, new_gc = [], []
                for p in range(pairs):
                    dq_n, c_n, g_n = block(kb, p, c[p], gc[p], False)
                    dq_acc[p] += dq_n
                    new_c.append(c[p] + c_n)
                    new_gc.append(gc[p] + g_n)
                return kb - 1, tuple(new_c), tuple(new_gc)

            lax.while_loop(cond, step, (qi - n_pre, tuple(c), tuple(gc)))
            finish([dq_acc[p] for p in range(pairs)])

        @pl.when(qi == nq - 1)
        def _():
            dk_ref[...] = dk_acc[...].astype(BF16)
            dv_ref[...] = dv_acc[...].astype(BF16)

    cols = [slice(p * LANES, (p + 1) * LANES) for p in range(pairs)]
    width, ncb = pairs * LANES, nhp // pairs
    qspec = pl.BlockSpec((tq, width), lambda b, p, i: (b * nq + i, p))
    sspec = pl.BlockSpec((seq, width), lambda b, p, i: (b, p))
    out = jax.ShapeDtypeStruct((T, D), BF16)
    carried = _Carried(comm, (batch, ncb, nq))
    return carried.pallas_call(
        body,
        in_specs=[qspec,
                  pl.BlockSpec((seq, width), lambda b, p, i: (b, ncb + p)),
                  pl.BlockSpec((seq, width), lambda b, p, i: (b, 2 * ncb + p)),
                  qspec, qspec],
        out_specs=[qspec, sspec, sspec], out_shape=[out, out, out],
        scratch_shapes=[pltpu.VMEM((pairs, 2 * tq, LANES), F32), pltpu.VMEM((seq, width), F32),
                        pltpu.VMEM((seq, width), F32)],
        args=[qkv, qkv, qkv, o, do], name=name)


_GELU_C = 0.7978845608028654
_GELU_A = 0.044715


def _gelu(x):
    xx = x * x
    a1 = 1.0 + jnp.tanh(x * (_GELU_C + (_GELU_C * _GELU_A) * xx))
    hx = 0.5 * x
    grad = a1 * (0.5 + (hx * (2.0 - a1)) * (_GELU_C + (3.0 * _GELU_C * _GELU_A) * xx))
    return hx * a1, grad


def _causal_ws(ws_ref, g):
    t = lax.broadcasted_iota(jnp.int32, (SGU_CHUNK, SGU_CHUNK), 0)
    s = lax.broadcasted_iota(jnp.int32, (SGU_CHUNK, SGU_CHUNK), 1)
    return jnp.where(s <= t, ws_ref[g], 0.0)


def _sgu_fwd(uv, gain, ws, bst, *, name):
    T, F2 = uv.shape
    F = F2 // 2
    C, G, W = SGU_CHUNK, SGU_GROUPS, SGU_GROUP_W

    def body(uv_ref, g_ref, ws_ref, bs_ref, y_ref):
        u, v = uv_ref[:, :F].astype(F32), uv_ref[:, F:].astype(F32)
        r = lax.rsqrt(jnp.mean(v * v, axis=-1, keepdims=True) + EPS)
        vn = (v * r * g_ref[...]).astype(BF16)
        for g in range(G):
            sl = slice(g * W, (g + 1) * W)
            mixed = _dot(_causal_ws(ws_ref, g).astype(BF16), vn[:, sl]) + bs_ref[:, g:g + 1]
            y_ref[:, sl] = (u[:, sl] * mixed).astype(BF16)

    return pl.pallas_call(
        body, grid=(T // C,),
        in_specs=[pl.BlockSpec((C, F2), lambda i: (i, 0)), pl.BlockSpec((1, F), lambda i: (0, 0)),
                  pl.BlockSpec((G, C, C), lambda i: (0, 0, 0)), pl.BlockSpec((C, G), lambda i: (0, 0))],
        out_specs=pl.BlockSpec((C, F), lambda i: (i, 0)),
        out_shape=jax.ShapeDtypeStruct((T, F), BF16),
        compiler_params=_params("parallel"), name=name)(uv, gain, ws, bst)


def _sgu_bwd(uv, dgelu, dy, gain, ws, bst, *, name):
    T, F2 = uv.shape
    F = F2 // 2
    C, G, W = SGU_CHUNK, SGU_GROUPS, SGU_GROUP_W
    nc = T // C

    def body(uv_ref, dgelu_ref, dy_ref, g_ref, ws_ref, bs_ref, duv_ref, dg_ref, dws_ref, dbs_ref,
             dg_acc, dws_acc, dbs_acc):
        i = pl.program_id(0)

        @pl.when(i == 0)
        def _():
            dg_acc[...] = jnp.zeros_like(dg_acc)
            dws_acc[...] = jnp.zeros_like(dws_acc)
            dbs_acc[...] = jnp.zeros_like(dbs_acc)

        u, v = uv_ref[:, :F].astype(F32), uv_ref[:, F:].astype(F32)
        dgelu = dgelu_ref[...].astype(F32)
        r = lax.rsqrt(jnp.mean(v * v, axis=-1, keepdims=True) + EPS)
        vhat = v * r
        gain_v = g_ref[...]
        vn = (vhat * gain_v).astype(BF16)
        dyv = dy_ref[...].astype(F32)
        lane8 = lax.broadcasted_iota(jnp.int32, (1, G), 1)
        dvn_parts = []
        dbs_new = jnp.zeros((C, G), F32)
        for g in range(G):
            sl = slice(g * W, (g + 1) * W)
            wsg = _causal_ws(ws_ref, g)
            mixed = _dot(wsg.astype(BF16), vn[:, sl]) + bs_ref[:, g:g + 1]
            duv_ref[:, sl] = (dyv[:, sl] * mixed * dgelu[:, sl]).astype(BF16)
            dmix = dyv[:, sl] * u[:, sl]
            dbs_new = dbs_new + jnp.where(lane8 == g, jnp.sum(dmix, axis=1, keepdims=True), 0.0)
            dmix_b = dmix.astype(BF16)
            dws_acc[g] += _dot_nt(dmix_b, vn[:, sl])
            dvn_parts.append(_dot(wsg.T.astype(BF16), dmix_b))
        dbs_acc[...] += dbs_new
        dvn = jnp.concatenate(dvn_parts, axis=1)
        dg_acc[...] += jnp.sum((dvn * vhat).reshape(C // SUBLANES, SUBLANES, F), axis=0)
        dvhat = dvn * gain_v
        dv = r * (dvhat - vhat * jnp.mean(dvhat * vhat, axis=-1, keepdims=True))
        duv_ref[:, F:] = (dv * dgelu[:, F:]).astype(BF16)

        @pl.when(i == nc - 1)
        def _():
            dg_ref[...] = jnp.sum(dg_acc[...], axis=0, keepdims=True)
            t = lax.broadcasted_iota(jnp.int32, (G, C, C), 1)
            s = lax.broadcasted_iota(jnp.int32, (G, C, C), 2)
            dws_ref[...] = jnp.where(s <= t, dws_acc[...], 0.0)
            dbs_ref[...] = dbs_acc[...]

    return pl.pallas_call(
        body, grid=(nc,),
        in_specs=[pl.BlockSpec((C, F2), lambda i: (i, 0)), pl.BlockSpec((C, F2), lambda i: (i, 0)),
                  pl.BlockSpec((C, F), lambda i: (i, 0)),
                  pl.BlockSpec((1, F), lambda i: (0, 0)), pl.BlockSpec((G, C, C), lambda i: (0, 0, 0)),
                  pl.BlockSpec((C, G), lambda i: (0, 0))],
        out_specs=[pl.BlockSpec((C, F2), lambda i: (i, 0)), pl.BlockSpec((1, F), lambda i: (0, 0)),
                   pl.BlockSpec((G, C, C), lambda i: (0, 0, 0)), pl.BlockSpec((C, G), lambda i: (0, 0))],
        out_shape=[jax.ShapeDtypeStruct((T, F2), BF16), jax.ShapeDtypeStruct((1, F), F32),
                   jax.ShapeDtypeStruct((G, C, C), F32), jax.ShapeDtypeStruct((C, G), F32)],
        scratch_shapes=[pltpu.VMEM((SUBLANES, F), F32), pltpu.VMEM((G, C, C), F32), pltpu.VMEM((C, G), F32)],
        compiler_params=_params("arbitrary"), name=name)(uv, dgelu, dy, gain, ws, bst)


def _my_place():
    return lax.axis_index("x"), lax.axis_index("y"), lax.axis_index("c")


def _all_gather(shards, *, name):
    nf = len(shards)
    items = [(lambda ins, f=f: ins[f], lambda outs, p, f=f: outs[f].at[:, p]) for f in range(nf)]
    targets = [jax.ShapeDtypeStruct((s.shape[0], N_DEV) + s.shape[1:], s.dtype) for s in shards]
    return _run_comm(_Comm(shards, targets, len(items), _gather_emit(items, [list(range(nf))])), name=name)


class _Comm:
    def __init__(self, sources, targets, n_items, emit):
        self.sources, self.targets, self.n_items, self.emit = list(sources), list(targets), n_items, emit
        self.filled = [t for t in self.targets if not isinstance(t, jax.ShapeDtypeStruct)]
        self.operands = self.sources + self.filled
        self.out_shapes = [jax.ShapeDtypeStruct(t.shape, t.dtype) for t in self.targets]
        self.sems = [pltpu.SemaphoreType.DMA((n_items, 7)), pltpu.SemaphoreType.DMA((n_items, 7)),
                     pltpu.SemaphoreType.DMA((n_items,))]

    def aliases(self, first_operand, first_result):
        pos = {id(t): k for k, t in enumerate(self.targets)}
        return {first_operand + len(self.sources) + a: first_result + pos[id(t)]
                for a, t in enumerate(self.filled)}


def _run_comm(comm, *, name):
    n_op, n_out = len(comm.operands), len(comm.targets)

    def body(*refs):
        comm.emit(0, 1, refs[:len(comm.sources)], refs[n_op:n_op + n_out], refs[n_op + n_out:])

    return pl.pallas_call(
        body, in_specs=[ANY] * n_op, out_specs=[ANY] * n_out, out_shape=comm.out_shapes,
        scratch_shapes=comm.sems, input_output_aliases=comm.aliases(0, 0), name=name)(*comm.operands)


def _at_steps(step, n_steps, phases):
    if n_steps == 1:
        for _, fn in phases:
            fn()
        return
    marks = {}
    for frac, fn in phases:
        marks.setdefault(min(int(frac * n_steps), n_steps - 1), []).append(fn)
    for mark, fns in sorted(marks.items()):
        @pl.when(step == mark)
        def _(fns=fns):
            for fn in fns:
                fn()


def _gather_emit(items, groups, fractions=None):
    if fractions is None:
        fractions = [(g + 1) / len(groups) for g in range(len(groups))]
    def emit(step, n_steps, ins, outs, sems):
        send_sems, recv_sems, local_sems = sems
        x, y, c = _my_place()
        me, sibling = (x, y, c), (x, y, 1 - c)
        chips = [(1 - x, y), (x, 1 - y), (1 - x, 1 - y)]

        def copy(i, k, block, to, own=False):
            src_of, dst_of = items[i]
            dst = dst_of(outs, 4 * block[0] + 2 * block[1] + block[2])
            return pltpu.make_async_remote_copy(
                src_ref=src_of(ins) if own else dst, dst_ref=dst,
                send_sem=send_sems.at[i, k], recv_sem=recv_sems.at[i, k],
                device_id=to, device_id_type=MESH)

        def local(i):
            src_of, dst_of = items[i]
            return pltpu.make_async_copy(src_of(ins), dst_of(outs, 4 * x + 2 * y + c), local_sems.at[i])

        def first(i):
            return [copy(i, 0, me, sibling, own=True)] + [
                copy(i, 1 + j, me, (*chip, c), own=True) for j, chip in enumerate(chips)]

        def start():
            for i in range(len(items)):
                local(i).start()
                for cp in first(i):
                    cp.start()

        def forward(group):
            for j, chip in enumerate(chips):
                for i in group:
                    copy(i, 1 + j, (*chip, c), me).wait_recv()
                    copy(i, 4 + j, (*chip, c), sibling).start()

        def finish():
            for i in range(len(items)):
                copy(i, 0, sibling, me).wait_recv()
                for j, chip in enumerate(chips):
                    copy(i, 4 + j, (*chip, 1 - c), me).wait_recv()
            for i in range(len(items)):
                for cp in first(i) + [copy(i, 4 + j, (*chip, c), sibling) for j, chip in enumerate(chips)]:
                    cp.wait_send()
                local(i).wait()

        phases = [(0.0, start)]
        for frac, group in zip(fractions, groups):
            phases.append((frac, functools.partial(forward, group)))
        phases.append((1.0, finish))
        _at_steps(step, n_steps, phases)

    return emit


def _exchange_emit(items):
    def emit(step, n_steps, ins, outs, sems):
        send_sems, recv_sems, local_sems = sems
        x, y, c = _my_place()
        me = 4 * x + 2 * y + c

        def peer_of(k):
            return x ^ (k >> 2), y ^ ((k >> 1) & 1), c ^ (k & 1)

        def copy(i, k):
            src_of, dst_of = items[i]
            px, py, pc = peer_of(k)
            return pltpu.make_async_remote_copy(
                src_ref=src_of(ins, 4 * px + 2 * py + pc), dst_ref=dst_of(outs, me),
                send_sem=send_sems.at[i, k - 1], recv_sem=recv_sems.at[i, k - 1],
                device_id=(px, py, pc), device_id_type=MESH)

        def arrival(i, k):
            src_of, dst_of = items[i]
            px, py, pc = peer_of(k)
            peer = 4 * px + 2 * py + pc
            return pltpu.make_async_remote_copy(
                src_ref=src_of(ins, peer), dst_ref=dst_of(outs, peer),
                send_sem=send_sems.at[i, k - 1], recv_sem=recv_sems.at[i, k - 1],
                device_id=(x, y, c), device_id_type=MESH)

        def local(i):
            src_of, dst_of = items[i]
            return pltpu.make_async_copy(src_of(ins, me), dst_of(outs, me), local_sems.at[i])

        def start():
            for i in range(len(items)):
                local(i).start()
            for k in range(1, N_DEV):
                for i in range(len(items)):
                    copy(i, k).start()

        def finish():
            for k in range(1, N_DEV):
                for i in range(len(items)):
                    arrival(i, k).wait_recv()
            for k in range(1, N_DEV):
                for i in range(len(items)):
                    copy(i, k).wait_send()
            for i in range(len(items)):
                local(i).wait()

        _at_steps(step, n_steps, [(0.0, start), (1.0, finish)])

    return emit


def _adam_math(g, w, m, v):
    m = ADAM_B1 * m + (1.0 - ADAM_B1) * g
    v = ADAM_B2 * v + (1.0 - ADAM_B2) * (g * g)
    m_hat = m / (1.0 - ADAM_B1 ** ADAM_STEP)
    v_hat = v / (1.0 - ADAM_B2 ** ADAM_STEP)
    delta = -ADAM_LR * (m_hat / (jnp.sqrt(v_hat) + ADAM_EPS) + ADAM_WD * w)
    return delta, m, v


def _sum_adamw(parts, w, m, v, *, name, tr=256):
    L, nd, R, C = parts.shape
    tr = min(tr, R)
    assert R % tr == 0

    def body(p_ref, w_ref, m_ref, v_ref, g_ref, d_ref, nm_ref, nv_ref):
        g = p_ref[0, 0].astype(F32)
        for q in range(1, nd):
            g = g + p_ref[0, q].astype(F32)
        d, nm, nv = _adam_math(g, w_ref[0], m_ref[0], v_ref[0])
        g_ref[0] = g
        d_ref[0] = d
        nm_ref[0] = nm
        nv_ref[0] = nv

    blk = pl.BlockSpec((1, tr, C), lambda l, i: (l, i, 0))
    out = jax.ShapeDtypeStruct((L, R, C), F32)
    return pl.pallas_call(
        body, grid=(L, R // tr),
        in_specs=[pl.BlockSpec((1, nd, tr, C), lambda l, i: (l, 0, i, 0)), blk, blk, blk],
        out_specs=[blk] * 4, out_shape=[out] * 4,
        compiler_params=_params("parallel", "parallel"), name=name)(parts, w, m, v)


def _sum_parts(parts, *, name):
    nd, R, C = parts.shape

    def body(p_ref, o_ref):
        g = p_ref[0]
        for q in range(1, nd):
            g = g + p_ref[q]
        o_ref[...] = g

    return pl.pallas_call(
        body, out_shape=jax.ShapeDtypeStruct((R, C), F32),
        in_specs=[pl.BlockSpec(memory_space=pltpu.VMEM)],
        out_specs=pl.BlockSpec(memory_space=pltpu.VMEM), name=name)(parts)


def _adamw_small(g, w, m, v, *, name):
    def body(g_ref, w_ref, m_ref, v_ref, d_ref, nm_ref, nv_ref):
        d, nm, nv = _adam_math(g_ref[...], w_ref[...], m_ref[...], v_ref[...])
        d_ref[...] = d
        nm_ref[...] = nm
        nv_ref[...] = nv

    vm = pl.BlockSpec(memory_space=pltpu.VMEM)
    out = jax.ShapeDtypeStruct(g.shape, F32)
    return pl.pallas_call(body, out_shape=[out] * 3, in_specs=[vm] * 4, out_specs=[vm] * 3,
                          name=name)(g, w, m, v)


def kernel(x, norm_mix, norm_mlp, sb_wqkv, sb_wo, sgu_win, sgu_gain, sgu_ws, sgu_bs, sgu_wout, mlp_w1, mlp_w2, final_norm, loss_target, m_norm_mix, m_norm_mlp, m_sb_wqkv, m_sb_wo, m_sgu_win, m_sgu_gain, m_sgu_ws, m_sgu_bs, m_sgu_wout, m_mlp_w1, m_mlp_w2, m_final_norm, v_norm_mix, v_norm_mlp, v_sb_wqkv, v_sb_wo, v_sgu_win, v_sgu_gain, v_sgu_ws, v_sgu_bs, v_sgu_wout, v_mlp_w1, v_mlp_w2, v_final_norm):
    batch, seq, D = x.shape
    T = batch * seq
    x0 = x.reshape(T, D)
    target = loss_target.reshape(T, D)

    WQKV, WO, WIN, WOUT, W1, W2, GAIN = range(7)
    big = [sb_wqkv, sb_wo, sgu_win, sgu_wout, mlp_w1, mlp_w2]
    shards = [w.astype(BF16) for w in big] + [sgu_gain[:, None, :]]
    rides = {
        "qkv0": [[(WO, 0), (W1, 0)]],
        "sb_fwd0": [[(W2, 0)], [(WIN, 0), (WOUT, 0), (W1, 1), (W2, 1)], [(WQKV, 1)]],
        "qkv2": [[(WO, 1), (W1, 2)]],
        "sb_fwd2": [[(W2, 2)], [(WIN, 1), (WOUT, 1), (W1, 3), (W2, 3)]],
    }

    first = [(lambda ins: ins[WQKV].at[0], lambda outs, p: outs[WQKV].at[0, p]),
             (lambda ins: ins[GAIN], lambda outs, p: outs[GAIN].at[:, p])]
    targets = [jax.ShapeDtypeStruct((s.shape[0], N_DEV) + s.shape[1:], s.dtype) for s in shards]
    gathered0 = _run_comm(_Comm(shards, targets, len(first), _gather_emit(first, [[0, 1]])), name="gather_first")
    gain_sgu = gathered0[GAIN].reshape(-1, 1, SGU_FFN)
    gw = dict(enumerate(gathered0[:GAIN]))

    def gather_ride(call):
        wanted = [fl for group in rides[call] for fl in group]
        fams = sorted({f for f, _ in wanted})
        items = [(lambda ins, f=f, l=l: ins[f].at[l], lambda outs, p, t=fams.index(f), l=l: outs[t].at[l, p])
                 for f, l in wanted]
        groups, sent, k = [], [], 0
        for group in rides[call]:
            groups.append(list(range(k, k + len(group))))
            sent.append(sum(shards[f].shape[1] * shards[f].shape[2] for f, _ in group) + (sent[-1] if sent else 0))
            k += len(group)
        fractions = [s / sent[-1] for s in sent]
        return _Comm(shards[:GAIN], [gw[f] for f in fams], len(items), _gather_emit(items, groups, fractions)), fams

    def weight(f):
        g = gw[f]
        return g if f in (WQKV, WIN, W1) else g.reshape(g.shape[0], 1, N_DEV * g.shape[2], g.shape[3])

    saved = []
    xs = x0
    for i in range(DEPTH):
        j = i // 2
        if i % 2 == 0:
            comm, fams = gather_ride(f"qkv{i}")
            (qkv, h), results = _mm_nn(xs, weight(WQKV), j, out_dtype=BF16, gain=norm_mix[i:i + 1],
                                       name=f"qkv{i}", comm=comm, tm=1024)
            gw.update(zip(fams, results))
            comm, fams = gather_ride(f"sb_fwd{i}")
            o, results = _sb_fwd(qkv, batch=batch, seq=seq, name=f"sb_fwd{i}", comm=comm)
            gw.update(zip(fams, results))
            x_mid = _mm_nn(o, weight(WO), j, out_dtype=F32, res=xs, name=f"wo{i}", tm=1024)
            mix = (h, qkv, o)
        else:
            gain_j = gain_sgu[j]
            bst = sgu_bs[j].T
            uv, h, dgelu = _mm_nn(xs, weight(WIN), j, out_dtype=BF16, gain=norm_mix[i:i + 1], gelu=True,
                                  name=f"win{i}")
            yv = _sgu_fwd(uv, gain_j, sgu_ws[j], bst, name=f"sgu_fwd{i}")
            x_mid = _mm_nn(yv, weight(WOUT), j, out_dtype=F32, res=xs, name=f"wout{i}", tm=1024)
            mix = (h, uv, dgelu, yv, gain_j, bst)
        a, h2 = _mm_nn(x_mid, weight(W1), i, out_dtype=BF16, gain=norm_mlp[i:i + 1], name=f"w1_{i}", tm=1024)
        x_out = _mm_nn(a, weight(W2), i, out_dtype=F32, res=x_mid, a_act="relu2", name=f"w2_{i}")
        saved.append((xs, mix, x_mid, h2, a))
        xs = x_out
    g_wqkv, g_wo, g_win, g_wout, g_w1, g_w2 = [weight(f) for f in range(GAIN)]

    dx, dxb, sq, d_final = _loss_head(xs, final_norm.reshape(1, D), target, name="loss_head")
    loss = lax.psum(0.5 * jnp.sum(sq) / D, ("x", "y", "c"))

    SMALL = GAIN
    stacks = {f: jax.ShapeDtypeStruct((w.shape[0], N_DEV) + w.shape[1:], BF16) for f, w in enumerate(big)}
    pending = []

    def row_shards(p):
        return p.reshape(N_DEV, p.shape[1] // N_DEV, p.shape[2])

    def exchange_pending(newest_only=False):
        going = [pending.pop()] if newest_only else [pending.pop(0) for _ in range(len(pending))]
        fams = sorted({f for _, f, _ in going})
        items = [(lambda ins, p, a=a: ins[a].at[p], lambda outs, q, t=fams.index(f), l=l: outs[t].at[l, q])
                 for a, (_, f, l) in enumerate(going)]
        return _Comm([part for part, _, _ in going], [stacks[f] for f in fams], len(items), _exchange_emit(items)), fams

    d_norm_mix, d_norm_mlp = [None] * DEPTH, [None] * DEPTH
    d_gain, d_ws, d_bs = [None] * 2, [None] * 2, [None] * 2
    for i in reversed(range(DEPTH)):
        j = i // 2
        xs, mix, x_mid, h2, a = saved[i]
        da = _mm_nt(dxb, g_w2, i, out_dtype=BF16, act_src=a, name=f"d_a{i}")
        pending.append((row_shards(_mm_tn(a, dxb, shards=1, a_act="relu2", name=f"d_w2_{i}")), W2, i))
        pending.append((_mm_tn(h2, da, shards=N_DEV, name=f"d_w1_{i}", pb=4, tm=1024), W1, i))
        (dx, dxb, d_norm_mlp[i]), _ = _mm_nt_rms_bwd(da, g_w1, i, x_mid, norm_mlp[i:i + 1], dx, name=f"d_h2_{i}")
        if i % 2 == 0:
            h, qkv, o = mix
            do = _mm_nt(dxb, g_wo, j, out_dtype=BF16, name=f"d_o{i}", tm=1024)
            pending.append((row_shards(_mm_tn(o, dxb, shards=1, name=f"d_wo{i}")), WO, j))
            comm, fams = exchange_pending()
            (dq, dk, dv), results = _sb_bwd(qkv, o, do, batch=batch, seq=seq, name=f"sb_bwd{i}", comm=comm)
            stacks.update(zip(fams, results))
            dqkv = jnp.concatenate([dq, dk, dv], axis=1)
            pending.append((_mm_tn(h, dqkv, shards=N_DEV, name=f"d_wqkv{i}", pb=4, tm=1024), WQKV, j))
            mixer_in, w_in = dqkv, g_wqkv
        else:
            h, uv, dgelu, yv, gain_j, bst = mix
            dy = _mm_nt(dxb, g_wout, j, out_dtype=BF16, name=f"d_y{i}", tm=1024)
            pending.append((row_shards(_mm_tn(yv, dxb, shards=1, name=f"d_wout{i}")), WOUT, j))
            duv, d_gain[j], d_ws[j], dbst = _sgu_bwd(uv, dgelu, dy, gain_j, sgu_ws[j], bst, name=f"sgu_bwd{i}")
            d_bs[j] = dbst.T
            pending.append((_mm_tn(h, duv, shards=N_DEV, name=f"d_win{i}", pb=4, tm=1024), WIN, j))
            mixer_in, w_in = duv, g_win
        comm, fams = exchange_pending(newest_only=True)
        (dx, dxb, d_norm_mix[i]), results = _mm_nt_rms_bwd(
            mixer_in, w_in, j, xs, norm_mix[i:i + 1], dx, name=f"d_h_mix{i}", comm=comm)
        stacks.update(zip(fams, results))
    grad_x = dx.reshape(batch, seq, D)

    small = [jnp.concatenate(d_norm_mix, 0), jnp.concatenate(d_norm_mlp, 0), d_final,
             jnp.concatenate(d_gain, 0), jnp.stack(d_bs, 0), jnp.stack(d_ws, 0)]
    small_flat = jnp.concatenate([s.reshape(-1) for s in small])
    n_small = small_flat.shape[0]
    small_rows = -(-n_small // (N_DEV * SUBLANES * LANES)) * SUBLANES
    small_flat = jnp.pad(small_flat, (0, N_DEV * small_rows * LANES - n_small))
    stacks[SMALL] = jax.ShapeDtypeStruct((1, N_DEV, small_rows, LANES), F32)
    pending.append((small_flat.reshape(N_DEV, small_rows, LANES), SMALL, 0))
    comm, fams = exchange_pending()
    stacks.update(zip(fams, _run_comm(comm, name="exchange_last")))
    r_wqkv, r_wo, r_win, r_wout, r_w1, r_w2, r_small = [stacks[f] for f in range(SMALL + 1)]

    u_wqkv = _sum_adamw(r_wqkv, sb_wqkv, m_sb_wqkv, v_sb_wqkv, name="adamw_wqkv")
    u_wo = _sum_adamw(r_wo, sb_wo, m_sb_wo, v_sb_wo, name="adamw_wo")
    u_win = _sum_adamw(r_win, sgu_win, m_sgu_win, v_sgu_win, name="adamw_win")
    u_wout = _sum_adamw(r_wout, sgu_wout, m_sgu_wout, v_sgu_wout, name="adamw_wout")
    u_w1 = _sum_adamw(r_w1, mlp_w1, m_mlp_w1, v_mlp_w1, name="adamw_w1")
    u_w2 = _sum_adamw(r_w2, mlp_w2, m_mlp_w2, v_mlp_w2, name="adamw_w2")

    small_sum = _sum_parts(r_small[0], name="sum_small")
    g_small = _all_gather([small_sum[None]], name="gather_small")[0].reshape(-1)[:n_small]

    shapes = [s.shape for s in small]
    sizes = [s.size for s in small]
    offs = [sum(sizes[:k]) for k in range(len(sizes))]
    me = 4 * lax.axis_index("x") + 2 * lax.axis_index("y") + lax.axis_index("c")
    shard_w = SGU_FFN // N_DEV

    def pack(arrs):
        flat = jnp.concatenate([a_.reshape(-1) for a_ in arrs])
        return jnp.pad(flat, (0, N_DEV * small_rows * LANES - n_small)).reshape(-1, LANES)

    def full_gain(gshard):
        return lax.dynamic_update_slice(jnp.zeros((2, SGU_FFN), F32), gshard, (0, me * shard_w))

    w_small = pack([norm_mix, norm_mlp, final_norm, full_gain(sgu_gain), sgu_bs, sgu_ws])
    m_small = pack([m_norm_mix, m_norm_mlp, m_final_norm, full_gain(m_sgu_gain), m_sgu_bs, m_sgu_ws])
    v_small = pack([v_norm_mix, v_norm_mlp, v_final_norm, full_gain(v_sgu_gain), v_sgu_bs, v_sgu_ws])
    g_pack = jnp.pad(g_small, (0, N_DEV * small_rows * LANES - n_small)).reshape(-1, LANES)
    sm = [g_pack] + list(_adamw_small(g_pack, w_small, m_small, v_small, name="adamw_small"))

    def unpack(flat2d):
        flat = flat2d.reshape(-1)
        out = [flat[offs[k]:offs[k] + sizes[k]].reshape(shapes[k]) for k in range(len(sizes))]
        out[2] = out[2].reshape(D)
        out[3] = lax.dynamic_slice(out[3], (0, me * shard_w), (2, shard_w))
        return out

    outs = []
    for k, big_u in enumerate(zip(u_wqkv, u_wo, u_win, u_wout, u_w1, u_w2)):
        s_nm, s_nl, s_fn, s_gain, s_bs, s_ws = unpack(sm[k])
        b_wqkv, b_wo, b_win, b_wout, b_w1, b_w2 = big_u
        outs += [s_nm, s_nl, b_wqkv, b_wo, b_win, s_gain, s_ws, s_bs, b_wout, b_w1, b_w2, s_fn]
    return (loss, grad_x, *outs)
```

```python
import functools

import jax
import jax.numpy as jnp
from jax import lax
from jax.experimental import pallas as pl
from jax.experimental.pallas import tpu as pltpu

F32 = jnp.float32
BF16 = jnp.bfloat16

N_DEV = 8
D_MODEL = 1024
SEQ = 2048
DEPTH = 4
SB_HEAD_DIM = 64
SGU_CHUNK = 128
SGU_FFN = 2 * D_MODEL
SGU_GROUPS = 8
SGU_GROUP_W = SGU_FFN // SGU_GROUPS
EPS = 1e-6

ADAM_LR = 0.001
ADAM_B1 = 0.9
ADAM_B2 = 0.999
ADAM_EPS = 1e-08
ADAM_WD = 0.01
ADAM_STEP = 10

MXU_TILE = 256
LANES = 128
SUBLANES = 8
VMEM_LIMIT = 56 * 1024 * 1024
EXP_ZERO_BELOW = -104.0
SB_TILE = 256
SB_STRAIGHT = 2
SB_PAIRS = 4

MESH = pl.DeviceIdType.MESH
ANY = pl.BlockSpec(memory_space=pl.ANY)


def _params(*sem):
    return pltpu.CompilerParams(dimension_semantics=sem, vmem_limit_bytes=VMEM_LIMIT)


def _dot(a, b):
    return jnp.dot(a, b, preferred_element_type=F32)


def _dot_nt(a, b):
    return lax.dot_general(a, b, (((1,), (1,)), ((), ())), preferred_element_type=F32)


def _dot_tn(a, b):
    return lax.dot_general(a, b, (((0,), (0,)), ((), ())), preferred_element_type=F32)


def _split_bf16(x):
    hi = x.astype(BF16)
    lo = (x - hi.astype(F32)).astype(BF16)
    return hi, lo


def _relu2(av):
    t = jnp.maximum(av, jnp.zeros_like(av))
    return t * t


def _mm_nn(a, w, l, *, out_dtype, name, res=None, a_act=None, gain=None, gelu=False, comm=None,
           tm=512, kc=1024):
    M, K = a.shape
    _, P, K2, n = w.shape
    assert K2 == K
    tm, kc = min(tm, M), min(kc, K)
    assert M % tm == 0 and K % kc == 0
    join = 2 if n % MXU_TILE and P % 2 == 0 else 1

    n_in = 2 + (gain is not None) + (res is not None)
    n_out = 1 + (gain is not None) + (gelu is True)

    def body(*all_refs):
        ins, outs, _ = carried.split(all_refs, n_in, n_out, 0)
        carried.emit(all_refs)
        refs = list(ins) + list(outs)
        a_ref, w_ref, o_ref = refs[0], refs[1], refs[n_in]
        if gain is not None:
            xv = a_ref[...]
            r = lax.rsqrt(jnp.mean(xv * xv, axis=-1, keepdims=True) + EPS)
            h = (xv * r * refs[2][...]).astype(BF16)
            refs[n_in + 1][...] = h
        for p in range(0, P, join):
            sl = slice(p * n, (p + join) * n)
            acc = None
            for k0 in range(0, K, kc):
                if gain is not None:
                    av = h[:, k0:k0 + kc]
                else:
                    av = a_ref[:, k0:k0 + kc]
                    av = _relu2(av) if a_act == "relu2" else av.astype(BF16)
                wv = [w_ref[0, p + s, k0:k0 + kc, :] for s in range(join)]
                d = _dot(av, wv[0] if join == 1 else jnp.concatenate(wv, axis=1))
                acc = d if acc is None else acc + d
            if res is not None:
                acc = acc + refs[n_in - 1][:, sl]
            if gelu:
                acc, dact = _gelu(acc)
                refs[-1][:, sl] = dact.astype(BF16)
            o_ref[:, sl] = acc.astype(out_dtype)

    row = lambda width: pl.BlockSpec((tm, width), lambda i: (i, 0))
    in_specs = [row(K), pl.BlockSpec((1, P, K, n), lambda i: (l, 0, 0, 0), pipeline_mode=pl.Buffered(1))]
    args = [a, w]
    if gain is not None:
        in_specs.append(pl.BlockSpec((1, K), lambda i: (0, 0)))
        args.append(gain)
    if res is not None:
        in_specs.append(row(P * n))
        args.append(res)
    out_specs, out_shape = [row(P * n)], [jax.ShapeDtypeStruct((M, P * n), out_dtype)]
    if gain is not None:
        out_specs.append(row(K))
        out_shape.append(jax.ShapeDtypeStruct((M, K), BF16))
    if gelu:
        out_specs.append(row(P * n))
        out_shape.append(jax.ShapeDtypeStruct((M, P * n), BF16))
    carried = _Carried(comm, (M // tm,), last="parallel")
    outs, comm_results = carried.pallas_call(
        body, in_specs=in_specs, out_specs=out_specs, out_shape=out_shape, scratch_shapes=[],
        args=args, name=name)
    outs = tuple(outs) if n_out > 1 else outs[0]
    return outs if comm is None else (outs, comm_results)


def _mm_nt(a, w, l, *, out_dtype, name, act_src=None, tm=512, tn=1024):
    M, K = a.shape
    _, P, Nout, kc = w.shape
    assert P == 1 and K == kc
    tm, tn = min(tm, M), min(tn, Nout)
    assert M % tm == 0 and Nout % tn == 0

    def body(*refs):
        a_ref, w_ref, o_ref = refs[0], refs[1], refs[-1]
        av = a_ref[...].astype(BF16)
        for n0 in range(0, Nout, tn):
            r = _dot_nt(av, w_ref[0, 0, n0:n0 + tn, :])
            if act_src is not None:
                r = r * (2.0 * jnp.maximum(refs[2][:, n0:n0 + tn].astype(F32), 0.0))
            o_ref[:, n0:n0 + tn] = r.astype(out_dtype)

    row = lambda width: pl.BlockSpec((tm, width), lambda i: (i, 0))
    in_specs = [row(K), pl.BlockSpec((1, 1, Nout, K), lambda i: (l, 0, 0, 0), pipeline_mode=pl.Buffered(1))]
    args = [a, w]
    if act_src is not None:
        in_specs.append(row(Nout))
        args.append(act_src)
    return pl.pallas_call(
        body, grid=(M // tm,), in_specs=in_specs, out_specs=row(Nout),
        out_shape=jax.ShapeDtypeStruct((M, Nout), out_dtype),
        compiler_params=_params("parallel"), name=name)(*args)


def _mm_tn(a, b, *, shards, name, a_act=None, tm=2048, tk=1024, pb=1):
    b_parts = list(b) if isinstance(b, (list, tuple)) else [b]
    M, K = a.shape
    N = sum(part.shape[1] for part in b_parts)
    assert all(part.shape[0] == M for part in b_parts)
    n = N // shards
    tm, tk = min(tm, M), min(tk, K)
    assert M % tm == 0 and K % tk == 0 and shards % pb == 0
    width, nm = pb * n, M // tm
    assert len(b_parts) == 1 or width == N

    def body(*refs):
        a_ref, b_refs, o_ref, acc = refs[0], refs[1:-2], refs[-2], refs[-1]
        m = pl.program_id(2)
        av = a_ref[...]
        if a_act == "relu2":
            av = _relu2(av)
        bv = [r[...].astype(BF16) for r in b_refs]

        @pl.when(m == 0)
        def _():
            acc[...] = jnp.zeros_like(acc)

        acc[...] += _dot_tn(av.astype(BF16), bv[0] if len(bv) == 1 else jnp.concatenate(bv, axis=1))

        @pl.when(m == nm - 1)
        def _():
            for p in range(pb):
                o_ref[p] = acc[:, p * n:(p + 1) * n].astype(BF16)

    if len(b_parts) == 1:
        b_specs = [pl.BlockSpec((tm, width), lambda i, j, m: (m, j))]
    else:
        b_specs = [pl.BlockSpec((tm, part.shape[1]), lambda i, j, m: (m, 0)) for part in b_parts]
    return pl.pallas_call(
        body, grid=(K // tk, N // width, nm),
        in_specs=[pl.BlockSpec((tm, tk), lambda i, j, m: (m, i))] + b_specs,
        out_specs=pl.BlockSpec((pb, tk, n), lambda i, j, m: (j, i, 0)),
        out_shape=jax.ShapeDtypeStruct((shards, K, n), BF16),
        scratch_shapes=[pltpu.VMEM((tk, width), F32)],
        compiler_params=_params("parallel", "parallel", "arbitrary"), name=name)(a, *b_parts)


def _mm_nt_rms_bwd(a, w, l, x, gain, dres, *, name, comm=None, tm=512):
    a_parts = list(a) if isinstance(a, (list, tuple)) else [a]
    na = len(a_parts)
    M, K = a_parts[0].shape[0], sum(part.shape[1] for part in a_parts)
    _, P, D, kc = w.shape
    assert K == P * kc and x.shape == (M, D)
    tm = min(tm, M)
    nr = M // tm
    join = 2 if kc % MXU_TILE and P % 2 == 0 else 1

    def body(*refs):
        ins, (dx_ref, dxb_ref, dg_ref), (acc,) = carried.split(refs, na + 4, 3, 1)
        a_refs, (w_ref, x_ref, g_ref, dres_ref) = ins[:na], ins[na:]
        i = pl.program_id(0)
        carried.emit(refs)
        av = a_refs[0][...] if na == 1 else jnp.concatenate([r[...] for r in a_refs], axis=1)
        dhv = None
        for p in range(0, P, join):
            wv = [w_ref[0, p + s] for s in range(join)]
            d = _dot_nt(av[:, p * kc:(p + join) * kc], wv[0] if join == 1 else jnp.concatenate(wv, axis=1))
            dhv = d if dhv is None else dhv + d
        xv = x_ref[...]
        r = lax.rsqrt(jnp.mean(xv * xv, axis=-1, keepdims=True) + EPS)
        xhat = xv * r
        dxhat = dhv * g_ref[...]
        dx = dres_ref[...] + r * (dxhat - xhat * jnp.mean(dxhat * xhat, axis=-1, keepdims=True))
        dx_ref[...] = dx
        dxb_ref[...] = dx.astype(BF16)
        part = jnp.sum((dhv * xhat).reshape(tm // SUBLANES, SUBLANES, D), axis=0)

        @pl.when(i == 0)
        def _():
            acc[...] = jnp.zeros_like(acc)

        acc[...] += part

        @pl.when(i == nr - 1)
        def _():
            dg_ref[...] = jnp.sum(acc[...], axis=0, keepdims=True)

    row = pl.BlockSpec((tm, D), lambda i: (i, 0))
    vec = pl.BlockSpec((1, D), lambda i: (0, 0))
    carried = _Carried(comm, (nr,))
    return carried.pallas_call(
        body,
        in_specs=[pl.BlockSpec((tm, part.shape[1]), lambda i: (i, 0)) for part in a_parts] + [
            pl.BlockSpec((1, P, D, kc), lambda i: (l, 0, 0, 0), pipeline_mode=pl.Buffered(1)),
            row, vec, row],
        out_specs=[row, row, vec],
        out_shape=[jax.ShapeDtypeStruct((M, D), F32), jax.ShapeDtypeStruct((M, D), BF16),
                   jax.ShapeDtypeStruct((1, D), F32)],
        scratch_shapes=[pltpu.VMEM((SUBLANES, D), F32)],
        args=a_parts + [w, x, gain, dres], name=name)


def _loss_head(x, gain, target, *, name, tr=512):
    T, D = x.shape
    nr = T // tr

    def body(x_ref, g_ref, t_ref, dx_ref, dxb_ref, sq_ref, dg_ref, sq_acc, dg_acc):
        i = pl.program_id(0)
        xv = x_ref[...]
        g = g_ref[...]
        r = lax.rsqrt(jnp.mean(xv * xv, axis=-1, keepdims=True) + EPS)
        xhat = xv * r
        err = xhat * g - t_ref[...]
        dy = err * (1.0 / D)
        dxhat = dy * g
        dx = r * (dxhat - xhat * jnp.mean(dxhat * xhat, axis=-1, keepdims=True))
        dx_ref[...] = dx
        dxb_ref[...] = dx.astype(BF16)
        sq = jnp.sum((err * err).reshape(tr // SUBLANES, SUBLANES, D), axis=0)
        dg = jnp.sum((dy * xhat).reshape(tr // SUBLANES, SUBLANES, D), axis=0)

        @pl.when(i == 0)
        def _():
            sq_acc[...] = sq
            dg_acc[...] = dg

        @pl.when(i > 0)
        def _():
            sq_acc[...] += sq
            dg_acc[...] += dg

        @pl.when(i == nr - 1)
        def _():
            sq_ref[...] = sq_acc[...]
            dg_ref[...] = jnp.sum(dg_acc[...], axis=0, keepdims=True)

    row = pl.BlockSpec((tr, D), lambda i: (i, 0))
    vec = pl.BlockSpec((1, D), lambda i: (0, 0))
    part = pl.BlockSpec((SUBLANES, D), lambda i: (0, 0))
    return pl.pallas_call(
        body, grid=(nr,), in_specs=[row, vec, row], out_specs=[row, row, part, vec],
        out_shape=[jax.ShapeDtypeStruct((T, D), F32), jax.ShapeDtypeStruct((T, D), BF16),
                   jax.ShapeDtypeStruct((SUBLANES, D), F32), jax.ShapeDtypeStruct((1, D), F32)],
        scratch_shapes=[pltpu.VMEM((SUBLANES, D), F32), pltpu.VMEM((SUBLANES, D), F32)],
        compiler_params=_params("arbitrary"), name=name)(x, gain, target)


def _head0_lanes():
    return lax.broadcasted_iota(jnp.int32, (1, LANES), 1) < SB_HEAD_DIM


def _stack_heads(x):
    zero = jnp.zeros_like(x)
    h0 = _head0_lanes()
    return jnp.concatenate([jnp.where(h0, x, zero), jnp.where(h0, zero, x)], axis=0)


def _unstack_heads(y, tq):
    return jnp.where(_head0_lanes(), y[:tq], y[tq:])


def _past_mask(tq, tk):
    row = lax.broadcasted_iota(jnp.int32, (2 * tq, tk), 0) & (tq - 1)
    col = lax.broadcasted_iota(jnp.int32, (2 * tq, tk), 1)
    return col < row


def _max_of(arrays):
    return functools.reduce(jnp.maximum, [jnp.max(a) for a in arrays])


def _sb_logs(z, past):
    minus_abs = lax.bitcast_convert_type(
        lax.bitcast_convert_type(z, jnp.uint32) | jnp.uint32(0x80000000), F32)
    log_beta = jnp.minimum(z, 0.0) - jnp.log(1.0 + jnp.exp(minus_abs))
    l = log_beta - z
    if past is not None:
        l = jnp.where(past, l, 0.0)
    return log_beta, l


def _suffix_matrix(tk):
    j = lax.broadcasted_iota(jnp.int32, (2 * tk, tk), 0) & (tk - 1)
    s = lax.broadcasted_iota(jnp.int32, (2 * tk, tk), 1)
    return (j > s).astype(BF16)


def _suffix_sum(x, u2, exact=True):
    if not exact:
        return _dot(x.astype(BF16), u2[:x.shape[1]])
    hi, lo = _split_bf16(x)
    return _dot(jnp.concatenate([hi, lo], axis=1), u2)


class _Carried:
    def __init__(self, comm, grid, last="arbitrary"):
        self.comm, self.grid, self.last = comm, grid, last
        self.n_op = len(comm.operands) if comm else 0
        self.n_tgt = len(comm.targets) if comm else 0

    def split(self, refs, n_in, n_out, n_scratch):
        self.n_in, self.n_out, self.n_scratch = n_in, n_out, n_scratch
        a = n_in + self.n_op
        b = a + n_out + self.n_tgt
        return refs[:n_in], refs[a:a + n_out], refs[b:b + n_scratch]

    def emit(self, refs):
        if self.comm is None:
            return
        step, n_steps = 0, 1
        for d, size in enumerate(self.grid):
            step = step * size + pl.program_id(d)
            n_steps *= size
        a = self.n_in + self.n_op + self.n_out
        self.comm.emit(step, n_steps, refs[self.n_in:self.n_in + len(self.comm.sources)],
                       refs[a:a + self.n_tgt], refs[a + self.n_tgt + self.n_scratch:])

    def pallas_call(self, body, *, in_specs, out_specs, out_shape, scratch_shapes, args, name):
        comm = self.comm
        n_in, n_out = len(in_specs), len(out_specs)
        aliases = {}
        if comm is not None:
            in_specs = in_specs + [ANY] * self.n_op
            out_specs = out_specs + [ANY] * self.n_tgt
            out_shape = out_shape + comm.out_shapes
            scratch_shapes = scratch_shapes + comm.sems
            args = args + comm.operands
            aliases = comm.aliases(n_in, n_out)
        sem = ("parallel",) * (len(self.grid) - 1) + (self.last,) if comm is None else ("arbitrary",) * len(self.grid)
        results = pl.pallas_call(
            body, grid=self.grid, in_specs=in_specs, out_specs=out_specs, out_shape=out_shape,
            scratch_shapes=scratch_shapes, input_output_aliases=aliases,
            compiler_params=_params(*sem), name=name)(*args)
        return results[:n_out], results[n_out:]


def _sb_fwd(qkv, *, batch, seq, name, comm=None, tq=SB_TILE, n_pre=SB_STRAIGHT, pairs=SB_PAIRS):
    T, D3 = qkv.shape
    D = D3 // 3
    nhp = D // LANES
    tk = tq
    nq = seq // tq
    scale = SB_HEAD_DIM ** -0.5
    assert 1 <= n_pre <= nq and nhp % pairs == 0

    def body(*refs):
        (q_ref, k_ref, v_ref), (o_ref,), (acc,) = carried.split(refs, 3, 1, 1)
        qi = pl.program_id(2)
        carried.emit(refs)
        qs = [_stack_heads(q_ref[:, lanes]) * scale for lanes in cols]
        past = _past_mask(tq, tk)
        u = _suffix_matrix(tk)

        def block(kb, g, c, diag):
            ks = pl.multiple_of(kb * tk, tk)
            z = _dot_nt(qs[g], k_ref[pl.ds(ks, tk), cols[g]])
            log_beta, l = _sb_logs(z, past if diag else None)
            arg = log_beta + _suffix_sum(l, u, exact=False)
            a = jnp.exp(arg if c is None else arg + c)
            if diag:
                a = jnp.where(past, a, 0.0)
            return _dot(a.astype(BF16), v_ref[pl.ds(ks, tk), cols[g]]), jnp.sum(l, axis=1, keepdims=True)

        def straight(n):
            o_sum, c = [None] * pairs, [None] * pairs
            for b in range(n):
                for g in range(pairs):
                    o_b, c_b = block(qi - b, g, c[g], b == 0)
                    o_sum[g] = o_b if b == 0 else o_sum[g] + o_b
                    c[g] = c_b if b == 0 else c[g] + c_b
            return o_sum, c

        def finish(o_sum):
            for g in range(pairs):
                o_ref[:, cols[g]] = _unstack_heads(o_sum[g], tq)

        for n in range(1, n_pre):
            @pl.when(qi == n - 1)
            def _(n=n):
                finish(straight(n)[0])

        @pl.when(qi >= n_pre - 1)
        def _():
            o_sum, c = straight(n_pre)
            for g in range(pairs):
                acc[g] = o_sum[g]

            def cond(st):
                kb, c = st
                return jnp.logical_and(kb >= 0, _max_of(c) > EXP_ZERO_BELOW)

            def step(st):
                kb, c = st
                new_c = []
                for g in range(pairs):
                    o_n, c_n = block(kb, g, c[g], False)
                    acc[g] += o_n
                    new_c.append(c[g] + c_n)
                return kb - 1, tuple(new_c)

            lax.while_loop(cond, step, (qi - n_pre, tuple(c)))
            finish([acc[g] for g in range(pairs)])

    cols = [slice(g * LANES, (g + 1) * LANES) for g in range(pairs)]
    width = pairs * LANES
    carried = _Carried(comm, (batch, nhp // pairs, nq))
    (o,), comm_results = carried.pallas_call(
        body,
        in_specs=[pl.BlockSpec((tq, width), lambda b, p, i: (b * nq + i, p)),
                  pl.BlockSpec((seq, width), lambda b, p, i: (b, nhp // pairs + p)),
                  pl.BlockSpec((seq, width), lambda b, p, i: (b, 2 * (nhp // pairs) + p))],
        out_specs=[pl.BlockSpec((tq, width), lambda b, p, i: (b * nq + i, p))],
        out_shape=[jax.ShapeDtypeStruct((T, D), F32)],
        scratch_shapes=[pltpu.VMEM((pairs, 2 * tq, LANES), F32)],
        args=[qkv, qkv, qkv], name=name)
    return o, comm_results


def _sb_bwd(qkv, o, do, *, batch, seq, name, comm=None, tq=SB_TILE, n_pre=SB_STRAIGHT, pairs=SB_PAIRS):
    T, D3 = qkv.shape
    D = D3 // 3
    nhp = D // LANES
    tk = tq
    nq = seq // tq
    scale = SB_HEAD_DIM ** -0.5

    def body(*refs):
        ins, outs, scratch = carried.split(refs, 5, 3, 3)
        q_ref, k_ref, v_ref, o_ref, do_ref = ins
        dq_ref, dk_ref, dv_ref = outs
        dq_acc, dk_acc, dv_acc = scratch
        qi = pl.program_id(2)
        carried.emit(refs)

        @pl.when(qi == 0)
        def _():
            dk_acc[...] = jnp.zeros_like(dk_acc)
            dv_acc[...] = jnp.zeros_like(dv_acc)

        qs = [_stack_heads(q_ref[:, lanes]) * scale for lanes in cols]
        dos = [_stack_heads(do_ref[:, lanes]) for lanes in cols]
        dsum = [jnp.sum(_stack_heads(do_ref[:, lanes].astype(F32) * o_ref[:, lanes]), axis=1, keepdims=True)
                for lanes in cols]
        past = _past_mask(tq, tk)
        u = _suffix_matrix(tk)

        def block(kb, p, c, gc, diag):
            ks = pl.multiple_of(kb * tk, tk)
            kblk = k_ref[pl.ds(ks, tk), cols[p]]
            vblk = v_ref[pl.ds(ks, tk), cols[p]]
            z = _dot_nt(qs[p], kblk)
            log_beta, l = _sb_logs(z, past if diag else None)
            arg = log_beta + _suffix_sum(l, u, exact=False)
            a = jnp.exp(arg if c is None else arg + c)
            if diag:
                a = jnp.where(past, a, 0.0)
            a = a.astype(BF16)
            beta = 1.0 - jnp.exp(l)
            g = a.astype(F32) * _dot_nt(dos[p], vblk)
            gs = _suffix_sum(g, u)
            dz = g - beta * (dsum[p] - (gs if gc is None else gs + gc))
            if diag:
                dz = jnp.where(past, dz, 0.0)
            dzb = dz.astype(BF16)
            dk_acc[pl.ds(ks, tk), cols[p]] += _dot_tn(dzb, qs[p])
            dv_acc[pl.ds(ks, tk), cols[p]] += _dot_tn(a, dos[p])
            return (_dot(dzb, kblk), jnp.sum(l, axis=1, keepdims=True),
                    jnp.sum(g, axis=1, keepdims=True))

        def finish(dq_sum):
            for p in range(pairs):
                dq_ref[:, cols[p]] = (_unstack_heads(dq_sum[p], tq) * scale).astype(BF16)

        def straight(n):
            dq_sum, c, gc = [None] * pairs, [None] * pairs, [None] * pairs
            for b in range(n):
                for p in range(pairs):
                    dq_b, c_b, g_b = block(qi - b, p, c[p], gc[p], b == 0)
                    dq_sum[p] = dq_b if b == 0 else dq_sum[p] + dq_b
                    c[p] = c_b if b == 0 else c[p] + c_b
                    gc[p] = g_b if b == 0 else gc[p] + g_b
            return dq_sum, c, gc

        for n in range(1, n_pre):
            @pl.when(qi == n - 1)
            def _(n=n):
                finish(straight(n)[0])

        @pl.when(qi >= n_pre - 1)
        def _():
            dq_sum, c, gc = straight(n_pre)
            for p in range(pairs):
                dq_acc[p] = dq_sum[p]

            def cond(st):
                kb, c, gc = st
                return jnp.logical_and(kb >= 0, _max_of(c) > EXP_ZERO_BELOW)

            def step(st):
                kb, c, gc = st
                new_c, new_gc = [], []
                for p in range(pairs):
                    dq_n, c_n, g_n = block(kb, p, c[p], gc[p], False)
                    dq_acc[p] += dq_n
                    new_c.append(c[p] + c_n)
                    new_gc.append(gc[p] + g_n)
                return kb - 1, tuple(new_c), tuple(new_gc)

            lax.while_loop(cond, step, (qi - n_pre, tuple(c), tuple(gc)))
            finish([dq_acc[p] for p in range(pairs)])

        @pl.when(qi == nq - 1)
        def _():
            dk_ref[...] = dk_acc[...].astype(BF16)
            dv_ref[...] = dv_acc[...].astype(BF16)

    cols = [slice(p * LANES, (p + 1) * LANES) for p in range(pairs)]
    width, ncb = pairs * LANES, nhp // pairs
    qspec = pl.BlockSpec((tq, width), lambda b, p, i: (b * nq + i, p))
    sspec = pl.BlockSpec((seq, width), lambda b, p, i: (b, p))
    out = jax.ShapeDtypeStruct((T, D), BF16)
    carried = _Carried(comm, (batch, ncb, nq))
    return carried.pallas_call(
        body,
        in_specs=[qspec,
                  pl.BlockSpec((seq, width), lambda b, p, i: (b, ncb + p)),
                  pl.BlockSpec((seq, width), lambda b, p, i: (b, 2 * ncb + p)),
                  qspec, qspec],
        out_specs=[qspec, sspec, sspec], out_shape=[out, out, out],
        scratch_shapes=[pltpu.VMEM((pairs, 2 * tq, LANES), F32), pltpu.VMEM((seq, width), F32),
                        pltpu.VMEM((seq, width), F32)],
        args=[qkv, qkv, qkv, o, do], name=name)


_GELU_C = 0.7978845608028654
_GELU_A = 0.044715


def _gelu(x):
    xx = x * x
    a1 = 1.0 + jnp.tanh(x * (_GELU_C + (_GELU_C * _GELU_A) * xx))
    hx = 0.5 * x
    grad = a1 * (0.5 + (hx * (2.0 - a1)) * (_GELU_C + (3.0 * _GELU_C * _GELU_A) * xx))
    return hx * a1, grad


def _causal_ws(ws_ref, g):
    t = lax.broadcasted_iota(jnp.int32, (SGU_CHUNK, SGU_CHUNK), 0)
    s = lax.broadcasted_iota(jnp.int32, (SGU_CHUNK, SGU_CHUNK), 1)
    return jnp.where(s <= t, ws_ref[g], 0.0)


def _sgu_fwd(uv, gain, ws, bst, *, name):
    T, F2 = uv.shape
    F = F2 // 2
    C, G, W = SGU_CHUNK, SGU_GROUPS, SGU_GROUP_W

    def body(uv_ref, g_ref, ws_ref, bs_ref, y_ref):
        u, v = uv_ref[:, :F].astype(F32), uv_ref[:, F:].astype(F32)
        r = lax.rsqrt(jnp.mean(v * v, axis=-1, keepdims=True) + EPS)
        vn = (v * r * g_ref[...]).astype(BF16)
        for g in range(G):
            sl = slice(g * W, (g + 1) * W)
            mixed = _dot(_causal_ws(ws_ref, g).astype(BF16), vn[:, sl]) + bs_ref[:, g:g + 1]
            y_ref[:, sl] = (u[:, sl] * mixed).astype(BF16)

    return pl.pallas_call(
        body, grid=(T // C,),
        in_specs=[pl.BlockSpec((C, F2), lambda i: (i, 0)), pl.BlockSpec((1, F), lambda i: (0, 0)),
                  pl.BlockSpec((G, C, C), lambda i: (0, 0, 0)), pl.BlockSpec((C, G), lambda i: (0, 0))],
        out_specs=pl.BlockSpec((C, F), lambda i: (i, 0)),
        out_shape=jax.ShapeDtypeStruct((T, F), BF16),
        compiler_params=_params("parallel"), name=name)(uv, gain, ws, bst)


def _sgu_bwd(uv, dgelu, dy, gain, ws, bst, *, name):
    T, F2 = uv.shape
    F = F2 // 2
    C, G, W = SGU_CHUNK, SGU_GROUPS, SGU_GROUP_W
    nc = T // C

    def body(uv_ref, dgelu_ref, dy_ref, g_ref, ws_ref, bs_ref, duv_ref, dg_ref, dws_ref, dbs_ref,
             dg_acc, dws_acc, dbs_acc):
        i = pl.program_id(0)

        @pl.when(i == 0)
        def _():
            dg_acc[...] = jnp.zeros_like(dg_acc)
            dws_acc[...] = jnp.zeros_like(dws_acc)
            dbs_acc[...] = jnp.zeros_like(dbs_acc)

        u, v = uv_ref[:, :F].astype(F32), uv_ref[:, F:].astype(F32)
        dgelu = dgelu_ref[...].astype(F32)
        r = lax.rsqrt(jnp.mean(v * v, axis=-1, keepdims=True) + EPS)
        vhat = v * r
        gain_v = g_ref[...]
        vn = (vhat * gain_v).astype(BF16)
        dyv = dy_ref[...].astype(F32)
        lane8 = lax.broadcasted_iota(jnp.int32, (1, G), 1)
        dvn_parts = []
        dbs_new = jnp.zeros((C, G), F32)
        for g in range(G):
            sl = slice(g * W, (g + 1) * W)
            wsg = _causal_ws(ws_ref, g)
            mixed = _dot(wsg.astype(BF16), vn[:, sl]) + bs_ref[:, g:g + 1]
            duv_ref[:, sl] = (dyv[:, sl] * mixed * dgelu[:, sl]).astype(BF16)
            dmix = dyv[:, sl] * u[:, sl]
            dbs_new = dbs_new + jnp.where(lane8 == g, jnp.sum(dmix, axis=1, keepdims=True), 0.0)
            dmix_b = dmix.astype(BF16)
            dws_acc[g] += _dot_nt(dmix_b, vn[:, sl])
            dvn_parts.append(_dot(wsg.T.astype(BF16), dmix_b))
        dbs_acc[...] += dbs_new
        dvn = jnp.concatenate(dvn_parts, axis=1)
        dg_acc[...] += jnp.sum((dvn * vhat).reshape(C // SUBLANES, SUBLANES, F), axis=0)
        dvhat = dvn * gain_v
        dv = r * (dvhat - vhat * jnp.mean(dvhat * vhat, axis=-1, keepdims=True))
        duv_ref[:, F:] = (dv * dgelu[:, F:]).astype(BF16)

        @pl.when(i == nc - 1)
        def _():
            dg_ref[...] = jnp.sum(dg_acc[...], axis=0, keepdims=True)
            t = lax.broadcasted_iota(jnp.int32, (G, C, C), 1)
            s = lax.broadcasted_iota(jnp.int32, (G, C, C), 2)
            dws_ref[...] = jnp.where(s <= t, dws_acc[...], 0.0)
            dbs_ref[...] = dbs_acc[...]

    return pl.pallas_call(
        body, grid=(nc,),
        in_specs=[pl.BlockSpec((C, F2), lambda i: (i, 0)), pl.BlockSpec((C, F2), lambda i: (i, 0)),
                  pl.BlockSpec((C, F), lambda i: (i, 0)),
                  pl.BlockSpec((1, F), lambda i: (0, 0)), pl.BlockSpec((G, C, C), lambda i: (0, 0, 0)),
                  pl.BlockSpec((C, G), lambda i: (0, 0))],
        out_specs=[pl.BlockSpec((C, F2), lambda i: (i, 0)), pl.BlockSpec((1, F), lambda i: (0, 0)),
                   pl.BlockSpec((G, C, C), lambda i: (0, 0, 0)), pl.BlockSpec((C, G), lambda i: (0, 0))],
        out_shape=[jax.ShapeDtypeStruct((T, F2), BF16), jax.ShapeDtypeStruct((1, F), F32),
                   jax.ShapeDtypeStruct((G, C, C), F32), jax.ShapeDtypeStruct((C, G), F32)],
        scratch_shapes=[pltpu.VMEM((SUBLANES, F), F32), pltpu.VMEM((G, C, C), F32), pltpu.VMEM((C, G), F32)],
        compiler_params=_params("arbitrary"), name=name)(uv, dgelu, dy, gain, ws, bst)


def _my_place():
    return lax.axis_index("x"), lax.axis_index("y"), lax.axis_index("c")


def _all_gather(shards, *, name):
    nf = len(shards)
    items = [(lambda ins, f=f: ins[f], lambda outs, p, f=f: outs[f].at[:, p]) for f in range(nf)]
    targets = [jax.ShapeDtypeStruct((s.shape[0], N_DEV) + s.shape[1:], s.dtype) for s in shards]
    return _run_comm(_Comm(shards, targets, len(items), _gather_emit(items, [list(range(nf))])), name=name)


class _Comm:
    def __init__(self, sources, targets, n_items, emit):
        self.sources, self.targets, self.n_items, self.emit = list(sources), list(targets), n_items, emit
        self.filled = [t for t in self.targets if not isinstance(t, jax.ShapeDtypeStruct)]
        self.operands = self.sources + self.filled
        self.out_shapes = [jax.ShapeDtypeStruct(t.shape, t.dtype) for t in self.targets]
        self.sems = [pltpu.SemaphoreType.DMA((n_items, 7)), pltpu.SemaphoreType.DMA((n_items, 7)),
                     pltpu.SemaphoreType.DMA((n_items,))]

    def aliases(self, first_operand, first_result):
        pos = {id(t): k for k, t in enumerate(self.targets)}
        return {first_operand + len(self.sources) + a: first_result + pos[id(t)]
                for a, t in enumerate(self.filled)}


def _run_comm(comm, *, name):
    n_op, n_out = len(comm.operands), len(comm.targets)

    def body(*refs):
        comm.emit(0, 1, refs[:len(comm.sources)], refs[n_op:n_op + n_out], refs[n_op + n_out:])

    return pl.pallas_call(
        body, in_specs=[ANY] * n_op, out_specs=[ANY] * n_out, out_shape=comm.out_shapes,
        scratch_shapes=comm.sems, input_output_aliases=comm.aliases(0, 0), name=name)(*comm.operands)


def _at_steps(step, n_steps, phases):
    if n_steps == 1:
        for _, fn in phases:
            fn()
        return
    marks = {}
    for frac, fn in phases:
        marks.setdefault(min(int(frac * n_steps), n_steps - 1), []).append(fn)
    for mark, fns in sorted(marks.items()):
        @pl.when(step == mark)
        def _(fns=fns):
            for fn in fns:
                fn()


def _gather_emit(items, groups, fractions=None):
    if fractions is None:
        fractions = [(g + 1) / len(groups) for g in range(len(groups))]
    def emit(step, n_steps, ins, outs, sems):
        send_sems, recv_sems, local_sems = sems
        x, y, c = _my_place()
        me, sibling = (x, y, c), (x, y, 1 - c)
        chips = [(1 - x, y), (x, 1 - y), (1 - x, 1 - y)]

        def copy(i, k, block, to, own=False):
            src_of, dst_of = items[i]
            dst = dst_of(outs, 4 * block[0] + 2 * block[1] + block[2])
            return pltpu.make_async_remote_copy(
                src_ref=src_of(ins) if own else dst, dst_ref=dst,
                send_sem=send_sems.at[i, k], recv_sem=recv_sems.at[i, k],
                device_id=to, device_id_type=MESH)

        def local(i):
            src_of, dst_of = items[i]
            return pltpu.make_async_copy(src_of(ins), dst_of(outs, 4 * x + 2 * y + c), local_sems.at[i])

        def first(i):
            return [copy(i, 0, me, sibling, own=True)] + [
                copy(i, 1 + j, me, (*chip, c), own=True) for j, chip in enumerate(chips)]

        def start():
            for i in range(len(items)):
                local(i).start()
                for cp in first(i):
                    cp.start()

        def forward(group):
            for j, chip in enumerate(chips):
                for i in group:
                    copy(i, 1 + j, (*chip, c), me).wait_recv()
                    copy(i, 4 + j, (*chip, c), sibling).start()

        def finish():
            for i in range(len(items)):
                copy(i, 0, sibling, me).wait_recv()
                for j, chip in enumerate(chips):
                    copy(i, 4 + j, (*chip, 1 - c), me).wait_recv()
            for i in range(len(items)):
                for cp in first(i) + [copy(i, 4 + j, (*chip, c), sibling) for j, chip in enumerate(chips)]:
                    cp.wait_send()
                local(i).wait()

        phases = [(0.0, start)]
        for frac, group in zip(fractions, groups):
            phases.append((frac, functools.partial(forward, group)))
        phases.append((1.0, finish))
        _at_steps(step, n_steps, phases)

    return emit


def _exchange_emit(items):
    def emit(step, n_steps, ins, outs, sems):
        send_sems, recv_sems, local_sems = sems
        x, y, c = _my_place()
        me = 4 * x + 2 * y + c

        def peer_of(k):
            return x ^ (k >> 2), y ^ ((k >> 1) & 1), c ^ (k & 1)

        def copy(i, k):
            src_of, dst_of = items[i]
            px, py, pc = peer_of(k)
            return pltpu.make_async_remote_copy(
                src_ref=src_of(ins, 4 * px + 2 * py + pc), dst_ref=dst_of(outs, me),
                send_sem=send_sems.at[i, k - 1], recv_sem=recv_sems.at[i, k - 1],
                device_id=(px, py, pc), device_id_type=MESH)

        def arrival(i, k):
            src_of, dst_of = items[i]
            px, py, pc = peer_of(k)
            peer = 4 * px + 2 * py + pc
            return pltpu.make_async_remote_copy(
                src_ref=src_of(ins, peer), dst_ref=dst_of(outs, peer),
                send_sem=send_sems.at[i, k - 1], recv_sem=recv_sems.at[i, k - 1],
                device_id=(x, y, c), device_id_type=MESH)

        def local(i):
            src_of, dst_of = items[i]
            return pltpu.make_async_copy(src_of(ins, me), dst_of(outs, me), local_sems.at[i])

        def start():
            for i in range(len(items)):
                local(i).start()
            for k in range(1, N_DEV):
                for i in range(len(items)):
                    copy(i, k).start()

        def finish():
            for k in range(1, N_DEV):
                for i in range(len(items)):
                    arrival(i, k).wait_recv()
            for k in range(1, N_DEV):
                for i in range(len(items)):
                    copy(i, k).wait_send()
            for i in range(len(items)):
                local(i).wait()

        _at_steps(step, n_steps, [(0.0, start), (1.0, finish)])

    return emit


def _adam_math(g, w, m, v):
    m = ADAM_B1 * m + (1.0 - ADAM_B1) * g
    v = ADAM_B2 * v + (1.0 - ADAM_B2) * (g * g)
    m_hat = m / (1.0 - ADAM_B1 ** ADAM_STEP)
    v_hat = v / (1.0 - ADAM_B2 ** ADAM_STEP)
    delta = -ADAM_LR * (m_hat / (jnp.sqrt(v_hat) + ADAM_EPS) + ADAM_WD * w)
    return delta, m, v


def _sum_adamw(parts, w, m, v, *, name, tr=256):
    L, nd, R, C = parts.shape
    tr = min(tr, R)
    assert R % tr == 0

    def body(p_ref, w_ref, m_ref, v_ref, g_ref, d_ref, nm_ref, nv_ref):
        g = p_ref[0, 0].astype(F32)
        for q in range(1, nd):
            g = g + p_ref[0, q].astype(F32)
        d, nm, nv = _adam_math(g, w_ref[0], m_ref[0], v_ref[0])
        g_ref[0] = g
        d_ref[0] = d
        nm_ref[0] = nm
        nv_ref[0] = nv

    blk = pl.BlockSpec((1, tr, C), lambda l, i: (l, i, 0))
    out = jax.ShapeDtypeStruct((L, R, C), F32)
    return pl.pallas_call(
        body, grid=(L, R // tr),
        in_specs=[pl.BlockSpec((1, nd, tr, C), lambda l, i: (l, 0, i, 0)), blk, blk, blk],
        out_specs=[blk] * 4, out_shape=[out] * 4,
        compiler_params=_params("parallel", "parallel"), name=name)(parts, w, m, v)


def _sum_parts(parts, *, name):
    nd, R, C = parts.shape

    def body(p_ref, o_ref):
        g = p_ref[0]
        for q in range(1, nd):
            g = g + p_ref[q]
        o_ref[...] = g

    return pl.pallas_call(
        body, out_shape=jax.ShapeDtypeStruct((R, C), F32),
        in_specs=[pl.BlockSpec(memory_space=pltpu.VMEM)],
        out_specs=pl.BlockSpec(memory_space=pltpu.VMEM), name=name)(parts)


def _adamw_small(g, w, m, v, *, name):
    def body(g_ref, w_ref, m_ref, v_ref, d_ref, nm_ref, nv_ref):
        d, nm, nv = _adam_math(g_ref[...], w_ref[...], m_ref[...], v_ref[...])
        d_ref[...] = d
        nm_ref[...] = nm
        nv_ref[...] = nv

    vm = pl.BlockSpec(memory_space=pltpu.VMEM)
    out = jax.ShapeDtypeStruct(g.shape, F32)
    return pl.pallas_call(body, out_shape=[out] * 3, in_specs=[vm] * 4, out_specs=[vm] * 3,
                          name=name)(g, w, m, v)


def kernel(x, norm_mix, norm_mlp, sb_wqkv, sb_wo, sgu_win, sgu_gain, sgu_ws, sgu_bs, sgu_wout, mlp_w1, mlp_w2, final_norm, loss_target, m_norm_mix, m_norm_mlp, m_sb_wqkv, m_sb_wo, m_sgu_win, m_sgu_gain, m_sgu_ws, m_sgu_bs, m_sgu_wout, m_mlp_w1, m_mlp_w2, m_final_norm, v_norm_mix, v_norm_mlp, v_sb_wqkv, v_sb_wo, v_sgu_win, v_sgu_gain, v_sgu_ws, v_sgu_bs, v_sgu_wout, v_mlp_w1, v_mlp_w2, v_final_norm):
    batch, seq, D = x.shape
    T = batch * seq
    x0 = x.reshape(T, D)
    target = loss_target.reshape(T, D)

    WQKV, WO, WIN, WOUT, W1, W2, GAIN = range(7)
    big = [sb_wqkv, sb_wo, sgu_win, sgu_wout, mlp_w1, mlp_w2]
    shards = [w.astype(BF16) for w in big] + [sgu_gain[:, None, :]]
    rides = {
        "qkv0": [[(WO, 0), (W1, 0)]],
        "sb_fwd0": [[(W2, 0)], [(WIN, 0), (WOUT, 0), (W1, 1), (W2, 1)], [(WQKV, 1)]],
        "qkv2": [[(WO, 1), (W1, 2)]],
        "sb_fwd2": [[(W2, 2)], [(WIN, 1), (WOUT, 1), (W1, 3), (W2, 3)]],
    }

    first = [(lambda ins: ins[WQKV].at[0], lambda outs, p: outs[WQKV].at[0, p]),
             (lambda ins: ins[GAIN], lambda outs, p: outs[GAIN].at[:, p])]
    targets = [jax.ShapeDtypeStruct((s.shape[0], N_DEV) + s.shape[1:], s.dtype) for s in shards]
    gathered0 = _run_comm(_Comm(shards, targets, len(first), _gather_emit(first, [[0, 1]])), name="gather_first")
    gain_sgu = gathered0[GAIN].reshape(-1, 1, SGU_FFN)
    gw = dict(enumerate(gathered0[:GAIN]))

    def gather_ride(call):
        wanted = [fl for group in rides[call] for fl in group]
        fams = sorted({f for f, _ in wanted})
        items = [(lambda ins, f=f, l=l: ins[f].at[l], lambda outs, p, t=fams.index(f), l=l: outs[t].at[l, p])
                 for f, l in wanted]
        groups, sent, k = [], [], 0
        for group in rides[call]:
            groups.append(list(range(k, k + len(group))))
            sent.append(sum(shards[f].shape[1] * shards[f].shape[2] for f, _ in group) + (sent[-1] if sent else 0))
            k += len(group)
        fractions = [s / sent[-1] for s in sent]
        return _Comm(shards[:GAIN], [gw[f] for f in fams], len(items), _gather_emit(items, groups, fractions)), fams

    def weight(f):
        g = gw[f]
        return g if f in (WQKV, WIN, W1) else g.reshape(g.shape[0], 1, N_DEV * g.shape[2], g.shape[3])

    saved = []
    xs = x0
    for i in range(DEPTH):
        j = i // 2
        if i % 2 == 0:
            comm, fams = gather_ride(f"qkv{i}")
            (qkv, h), results = _mm_nn(xs, weight(WQKV), j, out_dtype=BF16, gain=norm_mix[i:i + 1],
                                       name=f"qkv{i}", comm=comm, tm=1024)
            gw.update(zip(fams, results))
            comm, fams = gather_ride(f"sb_fwd{i}")
            o, results = _sb_fwd(qkv, batch=batch, seq=seq, name=f"sb_fwd{i}", comm=comm)
            gw.update(zip(fams, results))
            x_mid = _mm_nn(o, weight(WO), j, out_dtype=F32, res=xs, name=f"wo{i}", tm=1024)
            mix = (h, qkv, o)
        else:
            gain_j = gain_sgu[j]
            bst = sgu_bs[j].T
            uv, h, dgelu = _mm_nn(xs, weight(WIN), j, out_dtype=BF16, gain=norm_mix[i:i + 1], gelu=True,
                                  name=f"win{i}")
            yv = _sgu_fwd(uv, gain_j, sgu_ws[j], bst, name=f"sgu_fwd{i}")
            x_mid = _mm_nn(yv, weight(WOUT), j, out_dtype=F32, res=xs, name=f"wout{i}", tm=1024)
            mix = (h, uv, dgelu, yv, gain_j, bst)
        a, h2 = _mm_nn(x_mid, weight(W1), i, out_dtype=BF16, gain=norm_mlp[i:i + 1], name=f"w1_{i}", tm=1024)
        x_out = _mm_nn(a, weight(W2), i, out_dtype=F32, res=x_mid, a_act="relu2", name=f"w2_{i}")
        saved.append((xs, mix, x_mid, h2, a))
        xs = x_out
    g_wqkv, g_wo, g_win, g_wout, g_w1, g_w2 = [weight(f) for f in range(GAIN)]

    dx, dxb, sq, d_final = _loss_head(xs, final_norm.reshape(1, D), target, name="loss_head")
    loss = lax.psum(0.5 * jnp.sum(sq) / D, ("x", "y", "c"))

    SMALL = GAIN
    stacks = {f: jax.ShapeDtypeStruct((w.shape[0], N_DEV) + w.shape[1:], BF16) for f, w in enumerate(big)}
    pending = []

    def row_shards(p):
        return p.reshape(N_DEV, p.shape[1] // N_DEV, p.shape[2])

    def exchange_pending(newest_only=False):
        going = [pending.pop()] if newest_only else [pending.pop(0) for _ in range(len(pending))]
        fams = sorted({f for _, f, _ in going})
        items = [(lambda ins, p, a=a: ins[a].at[p], lambda outs, q, t=fams.index(f), l=l: outs[t].at[l, q])
                 for a, (_, f, l) in enumerate(going)]
        return _Comm([part for part, _, _ in going], [stacks[f] for f in fams], len(items), _exchange_emit(items)), fams

    d_norm_mix, d_norm_mlp = [None] * DEPTH, [None] * DEPTH
    d_gain, d_ws, d_bs = [None] * 2, [None] * 2, [None] * 2
    for i in reversed(range(DEPTH)):
        j = i // 2
        xs, mix, x_mid, h2, a = saved[i]
        da = _mm_nt(dxb, g_w2, i, out_dtype=BF16, act_src=a, name=f"d_a{i}")
        pending.append((row_shards(_mm_tn(a, dxb, shards=1, a_act="relu2", name=f"d_w2_{i}")), W2, i))
        pending.append((_mm_tn(h2, da, shards=N_DEV, name=f"d_w1_{i}", pb=4, tm=1024), W1, i))
        (dx, dxb, d_norm_mlp[i]), _ = _mm_nt_rms_bwd(da, g_w1, i, x_mid, norm_mlp[i:i + 1], dx, name=f"d_h2_{i}")
        if i % 2 == 0:
            h, qkv, o = mix
            do = _mm_nt(dxb, g_wo, j, out_dtype=BF16, name=f"d_o{i}", tm=1024)
            pending.append((row_shards(_mm_tn(o, dxb, shards=1, name=f"d_wo{i}")), WO, j))
            comm, fams = exchange_pending()
            (dq, dk, dv), results = _sb_bwd(qkv, o, do, batch=batch, seq=seq, name=f"sb_bwd{i}", comm=comm)
            stacks.update(zip(fams, results))
            dqkv = [dq, dk, dv]
            pending.append((_mm_tn(h, dqkv, shards=N_DEV, name=f"d_wqkv{i}", pb=N_DEV, tm=1024, tk=512), WQKV, j))
            mixer_in, w_in = dqkv, g_wqkv
        else:
            h, uv, dgelu, yv, gain_j, bst = mix
            dy = _mm_nt(dxb, g_wout, j, out_dtype=BF16, name=f"d_y{i}", tm=1024)
            pending.append((row_shards(_mm_tn(yv, dxb, shards=1, name=f"d_wout{i}")), WOUT, j))
            duv, d_gain[j], d_ws[j], dbst = _sgu_bwd(uv, dgelu, dy, gain_j, sgu_ws[j], bst, name=f"sgu_bwd{i}")
            d_bs[j] = dbst.T
            pending.append((_mm_tn(h, duv, shards=N_DEV, name=f"d_win{i}", pb=4, tm=1024), WIN, j))
            mixer_in, w_in = duv, g_win
        comm, fams = exchange_pending(newest_only=True)
        (dx, dxb, d_norm_mix[i]), results = _mm_nt_rms_bwd(
            mixer_in, w_in, j, xs, norm_mix[i:i + 1], dx, name=f"d_h_mix{i}", comm=comm)
        stacks.update(zip(fams, results))
    grad_x = dx.reshape(batch, seq, D)

    small = [jnp.concatenate(d_norm_mix, 0), jnp.concatenate(d_norm_mlp, 0), d_final,
             jnp.concatenate(d_gain, 0), jnp.stack(d_bs, 0), jnp.stack(d_ws, 0)]
    small_flat = jnp.concatenate([s.reshape(-1) for s in small])
    n_small = small_flat.shape[0]
    small_rows = -(-n_small // (N_DEV * SUBLANES * LANES)) * SUBLANES
    small_flat = jnp.pad(small_flat, (0, N_DEV * small_rows * LANES - n_small))
    stacks[SMALL] = jax.ShapeDtypeStruct((1, N_DEV, small_rows, LANES), F32)
    pending.append((small_flat.reshape(N_DEV, small_rows, LANES), SMALL, 0))
    comm, fams = exchange_pending()
    stacks.update(zip(fams, _run_comm(comm, name="exchange_last")))
    r_wqkv, r_wo, r_win, r_wout, r_w1, r_w2, r_small = [stacks[f] for f in range(SMALL + 1)]

    u_wqkv = _sum_adamw(r_wqkv, sb_wqkv, m_sb_wqkv, v_sb_wqkv, name="adamw_wqkv")
    u_wo = _sum_adamw(r_wo, sb_wo, m_sb_wo, v_sb_wo, name="adamw_wo")
    u_win = _sum_adamw(r_win, sgu_win, m_sgu_win, v_sgu_win, name="adamw_win")
    u_wout = _sum_adamw(r_wout, sgu_wout, m_sgu_wout, v_sgu_wout, name="adamw_wout")
    u_w1 = _sum_adamw(r_w1, mlp_w1, m_mlp_w1, v_mlp_w1, name="adamw_w1")
    u_w2 = _sum_adamw(r_w2, mlp_w2, m_mlp_w2, v_mlp_w2, name="adamw_w2")

    small_sum = _sum_parts(r_small[0], name="sum_small")
    g_small = _all_gather([small_sum[None]], name="gather_small")[0].reshape(-1)[:n_small]

    shapes = [s.shape for s in small]
    sizes = [s.size for s in small]
    offs = [sum(sizes[:k]) for k in range(len(sizes))]
    me = 4 * lax.axis_index("x") + 2 * lax.axis_index("y") + lax.axis_index("c")
    shard_w = SGU_FFN // N_DEV

    def pack(arrs):
        flat = jnp.concatenate([a_.reshape(-1) for a_ in arrs])
        return jnp.pad(flat, (0, N_DEV * small_rows * LANES - n_small)).reshape(-1, LANES)

    def full_gain(gshard):
        return lax.dynamic_update_slice(jnp.zeros((2, SGU_FFN), F32), gshard, (0, me * shard_w))

    w_small = pack([norm_mix, norm_mlp, final_norm, full_gain(sgu_gain), sgu_bs, sgu_ws])
    m_small = pack([m_norm_mix, m_norm_mlp, m_final_norm, full_gain(m_sgu_gain), m_sgu_bs, m_sgu_ws])
    v_small = pack([v_norm_mix, v_norm_mlp, v_final_norm, full_gain(v_sgu_gain), v_sgu_bs, v_sgu_ws])
    g_pack = jnp.pad(g_small, (0, N_DEV * small_rows * LANES - n_small)).reshape(-1, LANES)
    sm = [g_pack] + list(_adamw_small(g_pack, w_small, m_small, v_small, name="adamw_small"))

    def unpack(flat2d):
        flat = flat2d.reshape(-1)
        out = [flat[offs[k]:offs[k] + sizes[k]].reshape(shapes[k]) for k in range(len(sizes))]
        out[2] = out[2].reshape(D)
        out[3] = lax.dynamic_slice(out[3], (0, me * shard_w), (2, shard_w))
        return out

    outs = []
    for k, big_u in enumerate(zip(u_wqkv, u_wo, u_win, u_wout, u_w1, u_w2)):
        s_nm, s_nl, s_fn, s_gain, s_bs, s_ws = unpack(sm[k])
        b_wqkv, b_wo, b_win, b_wout, b_w1, b_w2 = big_u
        outs += [s_nm, s_nl, b_wqkv, b_wo, b_win, s_gain, s_ws, s_bs, b_wout, b_w1, b_w2, s_fn]
    return (loss, grad_x, *outs)
```

```python
import functools

import jax
import jax.numpy as jnp
from jax import lax
from jax.experimental import pallas as pl
from jax.experimental.pallas import tpu as pltpu

F32 = jnp.float32
BF16 = jnp.bfloat16

N_DEV = 8
D_MODEL = 1024
SEQ = 2048
DEPTH = 4
SB_HEAD_DIM = 64
SGU_CHUNK = 128
SGU_FFN = 2 * D_MODEL
SGU_GROUPS = 8
SGU_GROUP_W = SGU_FFN // SGU_GROUPS
EPS = 1e-6

ADAM_LR = 0.001
ADAM_B1 = 0.9
ADAM_B2 = 0.999
ADAM_EPS = 1e-08
ADAM_WD = 0.01
ADAM_STEP = 10

MXU_TILE = 256
LANES = 128
SUBLANES = 8
VMEM_LIMIT = 56 * 1024 * 1024
EXP_ZERO_BELOW = -104.0
SB_TILE = 256
SB_STRAIGHT = 2
SB_PAIRS = 4

MESH = pl.DeviceIdType.MESH
ANY = pl.BlockSpec(memory_space=pl.ANY)


def _params(*sem):
    return pltpu.CompilerParams(dimension_semantics=sem, vmem_limit_bytes=VMEM_LIMIT)


def _dot(a, b):
    return jnp.dot(a, b, preferred_element_type=F32)


def _dot_nt(a, b):
    return lax.dot_general(a, b, (((1,), (1,)), ((), ())), preferred_element_type=F32)


def _dot_tn(a, b):
    return lax.dot_general(a, b, (((0,), (0,)), ((), ())), preferred_element_type=F32)


def _split_bf16(x):
    hi = x.astype(BF16)
    lo = (x - hi.astype(F32)).astype(BF16)
    return hi, lo


def _relu2(av):
    t = jnp.maximum(av, jnp.zeros_like(av))
    return t * t


def _mm_nn(a, w, l, *, out_dtype, name, res=None, a_act=None, gain=None, gelu=False, comm=None,
           tm=512, kc=1024):
    M, K = a.shape
    _, P, K2, n = w.shape
    assert K2 == K
    tm, kc = min(tm, M), min(kc, K)
    assert M % tm == 0 and K % kc == 0
    join = 2 if n % MXU_TILE and P % 2 == 0 else 1

    n_in = 2 + (gain is not None) + (res is not None)
    n_out = 1 + (gain is not None) + (gelu is True)

    def body(*all_refs):
        ins, outs, _ = carried.split(all_refs, n_in, n_out, 0)
        carried.emit(all_refs)
        refs = list(ins) + list(outs)
        a_ref, w_ref, o_ref = refs[0], refs[1], refs[n_in]
        if gain is not None:
            xv = a_ref[...]
            r = lax.rsqrt(jnp.mean(xv * xv, axis=-1, keepdims=True) + EPS)
            h = (xv * r * refs[2][...]).astype(BF16)
            refs[n_in + 1][...] = h
        for p in range(0, P, join):
            sl = slice(p * n, (p + join) * n)
            acc = None
            for k0 in range(0, K, kc):
                if gain is not None:
                    av = h[:, k0:k0 + kc]
                else:
                    av = a_ref[:, k0:k0 + kc]
                    av = _relu2(av) if a_act == "relu2" else av.astype(BF16)
                wv = [w_ref[0, p + s, k0:k0 + kc, :] for s in range(join)]
                d = _dot(av, wv[0] if join == 1 else jnp.concatenate(wv, axis=1))
                acc = d if acc is None else acc + d
            if res is not None:
                acc = acc + refs[n_in - 1][:, sl]
            if gelu:
                acc, dact = _gelu(acc)
                refs[-1][:, sl] = dact.astype(BF16)
            o_ref[:, sl] = acc.astype(out_dtype)

    row = lambda width: pl.BlockSpec((tm, width), lambda i: (i, 0))
    in_specs = [row(K), pl.BlockSpec((1, P, K, n), lambda i: (l, 0, 0, 0), pipeline_mode=pl.Buffered(1))]
    args = [a, w]
    if gain is not None:
        in_specs.append(pl.BlockSpec((1, K), lambda i: (0, 0)))
        args.append(gain)
    if res is not None:
        in_specs.append(row(P * n))
        args.append(res)
    out_specs, out_shape = [row(P * n)], [jax.ShapeDtypeStruct((M, P * n), out_dtype)]
    if gain is not None:
        out_specs.append(row(K))
        out_shape.append(jax.ShapeDtypeStruct((M, K), BF16))
    if gelu:
        out_specs.append(row(P * n))
        out_shape.append(jax.ShapeDtypeStruct((M, P * n), BF16))
    carried = _Carried(comm, (M // tm,), last="parallel")
    outs, comm_results = carried.pallas_call(
        body, in_specs=in_specs, out_specs=out_specs, out_shape=out_shape, scratch_shapes=[],
        args=args, name=name)
    outs = tuple(outs) if n_out > 1 else outs[0]
    return outs if comm is None else (outs, comm_results)


def _mm_nt(a, w, l, *, out_dtype, name, act_src=None, tm=512, tn=1024):
    M, K = a.shape
    _, P, Nout, kc = w.shape
    assert P == 1 and K == kc
    tm, tn = min(tm, M), min(tn, Nout)
    assert M % tm == 0 and Nout % tn == 0

    def body(*refs):
        a_ref, w_ref, o_ref = refs[0], refs[1], refs[-1]
        av = a_ref[...].astype(BF16)
        for n0 in range(0, Nout, tn):
            r = _dot_nt(av, w_ref[0, 0, n0:n0 + tn, :])
            if act_src is not None:
                r = r * (2.0 * jnp.maximum(refs[2][:, n0:n0 + tn].astype(F32), 0.0))
            o_ref[:, n0:n0 + tn] = r.astype(out_dtype)

    row = lambda width: pl.BlockSpec((tm, width), lambda i: (i, 0))
    in_specs = [row(K), pl.BlockSpec((1, 1, Nout, K), lambda i: (l, 0, 0, 0), pipeline_mode=pl.Buffered(1))]
    args = [a, w]
    if act_src is not None:
        in_specs.append(row(Nout))
        args.append(act_src)
    return pl.pallas_call(
        body, grid=(M // tm,), in_specs=in_specs, out_specs=row(Nout),
        out_shape=jax.ShapeDtypeStruct((M, Nout), out_dtype),
        compiler_params=_params("parallel"), name=name)(*args)


def _mm_tn(a, b, *, shards, name, a_act=None, tm=2048, tk=1024, pb=1):
    b_parts = list(b) if isinstance(b, (list, tuple)) else [b]
    M, K = a.shape
    N = sum(part.shape[1] for part in b_parts)
    assert all(part.shape[0] == M for part in b_parts)
    n = N // shards
    tm, tk = min(tm, M), min(tk, K)
    assert M % tm == 0 and K % tk == 0 and shards % pb == 0
    width, nm = pb * n, M // tm
    assert len(b_parts) == 1 or width == N

    def body(*refs):
        a_ref, b_refs, o_ref, acc = refs[0], refs[1:-2], refs[-2], refs[-1]
        m = pl.program_id(2)
        av = a_ref[...]
        if a_act == "relu2":
            av = _relu2(av)
        bv = [r[...].astype(BF16) for r in b_refs]

        @pl.when(m == 0)
        def _():
            acc[...] = jnp.zeros_like(acc)

        acc[...] += _dot_tn(av.astype(BF16), bv[0] if len(bv) == 1 else jnp.concatenate(bv, axis=1))

        @pl.when(m == nm - 1)
        def _():
            for p in range(pb):
                o_ref[p] = acc[:, p * n:(p + 1) * n].astype(BF16)

    if len(b_parts) == 1:
        b_specs = [pl.BlockSpec((tm, width), lambda i, j, m: (m, j))]
    else:
        b_specs = [pl.BlockSpec((tm, part.shape[1]), lambda i, j, m: (m, 0)) for part in b_parts]
    return pl.pallas_call(
        body, grid=(K // tk, N // width, nm),
        in_specs=[pl.BlockSpec((tm, tk), lambda i, j, m: (m, i))] + b_specs,
        out_specs=pl.BlockSpec((pb, tk, n), lambda i, j, m: (j, i, 0)),
        out_shape=jax.ShapeDtypeStruct((shards, K, n), BF16),
        scratch_shapes=[pltpu.VMEM((tk, width), F32)],
        compiler_params=_params("parallel", "parallel", "arbitrary"), name=name)(a, *b_parts)


def _mm_nt_rms_bwd(a, w, l, x, gain, dres, *, name, comm=None, tm=512):
    a_parts = list(a) if isinstance(a, (list, tuple)) else [a]
    na = len(a_parts)
    M, K = a_parts[0].shape[0], sum(part.shape[1] for part in a_parts)
    _, P, D, kc = w.shape
    assert K == P * kc and x.shape == (M, D)
    tm = min(tm, M)
    nr = M // tm
    join = 2 if kc % MXU_TILE and P % 2 == 0 else 1

    def body(*refs):
        ins, (dx_ref, dxb_ref, dg_ref), (acc,) = carried.split(refs, na + 4, 3, 1)
        a_refs, (w_ref, x_ref, g_ref, dres_ref) = ins[:na], ins[na:]
        i = pl.program_id(0)
        carried.emit(refs)
        av = a_refs[0][...] if na == 1 else jnp.concatenate([r[...] for r in a_refs], axis=1)
        dhv = None
        for p in range(0, P, join):
            wv = [w_ref[0, p + s] for s in range(join)]
            d = _dot_nt(av[:, p * kc:(p + join) * kc], wv[0] if join == 1 else jnp.concatenate(wv, axis=1))
            dhv = d if dhv is None else dhv + d
        xv = x_ref[...]
        r = lax.rsqrt(jnp.mean(xv * xv, axis=-1, keepdims=True) + EPS)
        xhat = xv * r
        dxhat = dhv * g_ref[...]
        dx = dres_ref[...] + r * (dxhat - xhat * jnp.mean(dxhat * xhat, axis=-1, keepdims=True))
        dx_ref[...] = dx
        dxb_ref[...] = dx.astype(BF16)
        part = jnp.sum((dhv * xhat).reshape(tm // SUBLANES, SUBLANES, D), axis=0)

        @pl.when(i == 0)
        def _():
            acc[...] = jnp.zeros_like(acc)

        acc[...] += part

        @pl.when(i == nr - 1)
        def _():
            dg_ref[...] = jnp.sum(acc[...], axis=0, keepdims=True)

    row = pl.BlockSpec((tm, D), lambda i: (i, 0))
    vec = pl.BlockSpec((1, D), lambda i: (0, 0))
    carried = _Carried(comm, (nr,))
    return carried.pallas_call(
        body,
        in_specs=[pl.BlockSpec((tm, part.shape[1]), lambda i: (i, 0)) for part in a_parts] + [
            pl.BlockSpec((1, P, D, kc), lambda i: (l, 0, 0, 0), pipeline_mode=pl.Buffered(1)),
            row, vec, row],
        out_specs=[row, row, vec],
        out_shape=[jax.ShapeDtypeStruct((M, D), F32), jax.ShapeDtypeStruct((M, D), BF16),
                   jax.ShapeDtypeStruct((1, D), F32)],
        scratch_shapes=[pltpu.VMEM((SUBLANES, D), F32)],
        args=a_parts + [w, x, gain, dres], name=name)


def _loss_head(x, gain, target, *, name, tr=512):
    T, D = x.shape
    nr = T // tr

    def body(x_ref, g_ref, t_ref, dx_ref, dxb_ref, sq_ref, dg_ref, sq_acc, dg_acc):
        i = pl.program_id(0)
        xv = x_ref[...]
        g = g_ref[...]
        r = lax.rsqrt(jnp.mean(xv * xv, axis=-1, keepdims=True) + EPS)
        xhat = xv * r
        err = xhat * g - t_ref[...]
        dy = err * (1.0 / D)
        dxhat = dy * g
        dx = r * (dxhat - xhat * jnp.mean(dxhat * xhat, axis=-1, keepdims=True))
        dx_ref[...] = dx
        dxb_ref[...] = dx.astype(BF16)
        sq = jnp.sum((err * err).reshape(tr // SUBLANES, SUBLANES, D), axis=0)
        dg = jnp.sum((dy * xhat).reshape(tr // SUBLANES, SUBLANES, D), axis=0)

        @pl.when(i == 0)
        def _():
            sq_acc[...] = sq
            dg_acc[...] = dg

        @pl.when(i > 0)
        def _():
            sq_acc[...] += sq
            dg_acc[...] += dg

        @pl.when(i == nr - 1)
        def _():
            sq_ref[...] = sq_acc[...]
            dg_ref[...] = jnp.sum(dg_acc[...], axis=0, keepdims=True)

    row = pl.BlockSpec((tr, D), lambda i: (i, 0))
    vec = pl.BlockSpec((1, D), lambda i: (0, 0))
    part = pl.BlockSpec((SUBLANES, D), lambda i: (0, 0))
    return pl.pallas_call(
        body, grid=(nr,), in_specs=[row, vec, row], out_specs=[row, row, part, vec],
        out_shape=[jax.ShapeDtypeStruct((T, D), F32), jax.ShapeDtypeStruct((T, D), BF16),
                   jax.ShapeDtypeStruct((SUBLANES, D), F32), jax.ShapeDtypeStruct((1, D), F32)],
        scratch_shapes=[pltpu.VMEM((SUBLANES, D), F32), pltpu.VMEM((SUBLANES, D), F32)],
        compiler_params=_params("arbitrary"), name=name)(x, gain, target)


def _head0_lanes():
    return lax.broadcasted_iota(jnp.int32, (1, LANES), 1) < SB_HEAD_DIM


def _stack_heads(x):
    zero = jnp.zeros_like(x)
    h0 = _head0_lanes()
    return jnp.concatenate([jnp.where(h0, x, zero), jnp.where(h0, zero, x)], axis=0)


def _unstack_heads(y, tq):
    return jnp.where(_head0_lanes(), y[:tq], y[tq:])


def _past_mask(tq, tk):
    row = lax.broadcasted_iota(jnp.int32, (2 * tq, tk), 0) & (tq - 1)
    col = lax.broadcasted_iota(jnp.int32, (2 * tq, tk), 1)
    return col < row


def _max_of(arrays):
    return functools.reduce(jnp.maximum, [jnp.max(a) for a in arrays])


def _sb_logs(z, past):
    minus_abs = lax.bitcast_convert_type(
        lax.bitcast_convert_type(z, jnp.uint32) | jnp.uint32(0x80000000), F32)
    log_beta = jnp.minimum(z, 0.0) - jnp.log(1.0 + jnp.exp(minus_abs))
    l = log_beta - z
    if past is not None:
        l = jnp.where(past, l, 0.0)
    return log_beta, l


def _suffix_matrix(tk):
    j = lax.broadcasted_iota(jnp.int32, (2 * tk, tk), 0) & (tk - 1)
    s = lax.broadcasted_iota(jnp.int32, (2 * tk, tk), 1)
    return (j > s).astype(BF16)


def _suffix_sum(x, u2, exact=True):
    if not exact:
        return _dot(x.astype(BF16), u2[:x.shape[1]])
    hi, lo = _split_bf16(x)
    return _dot(jnp.concatenate([hi, lo], axis=1), u2)


class _Carried:
    def __init__(self, comm, grid, last="arbitrary"):
        self.comm, self.grid, self.last = comm, grid, last
        self.n_op = len(comm.operands) if comm else 0
        self.n_tgt = len(comm.targets) if comm else 0

    def split(self, refs, n_in, n_out, n_scratch):
        self.n_in, self.n_out, self.n_scratch = n_in, n_out, n_scratch
        a = n_in + self.n_op
        b = a + n_out + self.n_tgt
        return refs[:n_in], refs[a:a + n_out], refs[b:b + n_scratch]

    def emit(self, refs):
        if self.comm is None:
            return
        step, n_steps = 0, 1
        for d, size in enumerate(self.grid):
            step = step * size + pl.program_id(d)
            n_steps *= size
        a = self.n_in + self.n_op + self.n_out
        self.comm.emit(step, n_steps, refs[self.n_in:self.n_in + len(self.comm.sources)],
                       refs[a:a + self.n_tgt], refs[a + self.n_tgt + self.n_scratch:])

    def pallas_call(self, body, *, in_specs, out_specs, out_shape, scratch_shapes, args, name):
        comm = self.comm
        n_in, n_out = len(in_specs), len(out_specs)
        aliases = {}
        if comm is not None:
            in_specs = in_specs + [ANY] * self.n_op
            out_specs = out_specs + [ANY] * self.n_tgt
            out_shape = out_shape + comm.out_shapes
            scratch_shapes = scratch_shapes + comm.sems
            args = args + comm.operands
            aliases = comm.aliases(n_in, n_out)
        sem = ("parallel",) * (len(self.grid) - 1) + (self.last,) if comm is None else ("arbitrary",) * len(self.grid)
        results = pl.pallas_call(
            body, grid=self.grid, in_specs=in_specs, out_specs=out_specs, out_shape=out_shape,
            scratch_shapes=scratch_shapes, input_output_aliases=aliases,
            compiler_params=_params(*sem), name=name)(*args)
        return results[:n_out], results[n_out:]


def _sb_fwd(qkv, *, batch, seq, name, comm=None, tq=SB_TILE, n_pre=SB_STRAIGHT, pairs=SB_PAIRS):
    T, D3 = qkv.shape
    D = D3 // 3
    nhp = D // LANES
    tk = tq
    nq = seq // tq
    scale = SB_HEAD_DIM ** -0.5
    assert 1 <= n_pre <= nq and nhp % pairs == 0

    def body(*refs):
        (q_ref, k_ref, v_ref), (o_ref,), (acc,) = carried.split(refs, 3, 1, 1)
        qi = pl.program_id(2)
        carried.emit(refs)
        qs = [_stack_heads(q_ref[:, lanes]) * scale for lanes in cols]
        past = _past_mask(tq, tk)
        u = _suffix_matrix(tk)

        def block(kb, g, c, diag):
            ks = pl.multiple_of(kb * tk, tk)
            z = _dot_nt(qs[g], k_ref[pl.ds(ks, tk), cols[g]])
            log_beta, l = _sb_logs(z, past if diag else None)
            arg = log_beta + _suffix_sum(l, u, exact=False)
            a = jnp.exp(arg if c is None else arg + c)
            if diag:
                a = jnp.where(past, a, 0.0)
            return _dot(a.astype(BF16), v_ref[pl.ds(ks, tk), cols[g]]), jnp.sum(l, axis=1, keepdims=True)

        def straight(n):
            o_sum, c = [None] * pairs, [None] * pairs
            for b in range(n):
                for g in range(pairs):
                    o_b, c_b = block(qi - b, g, c[g], b == 0)
                    o_sum[g] = o_b if b == 0 else o_sum[g] + o_b
                    c[g] = c_b if b == 0 else c[g] + c_b
            return o_sum, c

        def finish(o_sum):
            for g in range(pairs):
                o_ref[:, cols[g]] = _unstack_heads(o_sum[g], tq)

        for n in range(1, n_pre):
            @pl.when(qi == n - 1)
            def _(n=n):
                finish(straight(n)[0])

        @pl.when(qi >= n_pre - 1)
        def _():
            o_sum, c = straight(n_pre)
            for g in range(pairs):
                acc[g] = o_sum[g]

            def cond(st):
                kb, c = st
                return jnp.logical_and(kb >= 0, _max_of(c) > EXP_ZERO_BELOW)

            def step(st):
                kb, c = st
                new_c = []
                for g in range(pairs):
                    o_n, c_n = block(kb, g, c[g], False)
                    acc[g] += o_n
                    new_c.append(c[g] + c_n)
                return kb - 1, tuple(new_c)

            lax.while_loop(cond, step, (qi - n_pre, tuple(c)))
            finish([acc[g] for g in range(pairs)])

    cols = [slice(g * LANES, (g + 1) * LANES) for g in range(pairs)]
    width = pairs * LANES
    carried = _Carried(comm, (batch, nhp // pairs, nq))
    (o,), comm_results = carried.pallas_call(
        body,
        in_specs=[pl.BlockSpec((tq, width), lambda b, p, i: (b * nq + i, p)),
                  pl.BlockSpec((seq, width), lambda b, p, i: (b, nhp // pairs + p)),
                  pl.BlockSpec((seq, width), lambda b, p, i: (b, 2 * (nhp // pairs) + p))],
        out_specs=[pl.BlockSpec((tq, width), lambda b, p, i: (b * nq + i, p))],
        out_shape=[jax.ShapeDtypeStruct((T, D), F32)],
        scratch_shapes=[pltpu.VMEM((pairs, 2 * tq, LANES), F32)],
        args=[qkv, qkv, qkv], name=name)
    return o, comm_results


def _sb_bwd(qkv, o, do, *, batch, seq, name, comm=None, tq=SB_TILE, n_pre=SB_STRAIGHT, pairs=SB_PAIRS):
    T, D3 = qkv.shape
    D = D3 // 3
    nhp = D // LANES
    tk = tq
    nq = seq // tq
    scale = SB_HEAD_DIM ** -0.5

    def body(*refs):
        ins, outs, scratch = carried.split(refs, 5, 3, 3)
        q_ref, k_ref, v_ref, o_ref, do_ref = ins
        dq_ref, dk_ref, dv_ref = outs
        dq_acc, dk_acc, dv_acc = scratch
        qi = pl.program_id(2)
        carried.emit(refs)

        @pl.when(qi == 0)
        def _():
            dk_acc[...] = jnp.zeros_like(dk_acc)
            dv_acc[...] = jnp.zeros_like(dv_acc)

        qs = [_stack_heads(q_ref[:, lanes]) * scale for lanes in cols]
        dos = [_stack_heads(do_ref[:, lanes]) for lanes in cols]
        dsum = [jnp.sum(_stack_heads(do_ref[:, lanes].astype(F32) * o_ref[:, lanes]), axis=1, keepdims=True)
                for lanes in cols]
        past = _past_mask(tq, tk)
        u = _suffix_matrix(tk)

        def block(kb, p, c, gc, diag):
            ks = pl.multiple_of(kb * tk, tk)
            kblk = k_ref[pl.ds(ks, tk), cols[p]]
            vblk = v_ref[pl.ds(ks, tk), cols[p]]
            z = _dot_nt(qs[p], kblk)
            log_beta, l = _sb_logs(z, past if diag else None)
            arg = log_beta + _suffix_sum(l, u, exact=False)
            a = jnp.exp(arg if c is None else arg + c)
            if diag:
                a = jnp.where(past, a, 0.0)
            a = a.astype(BF16)
            beta = 1.0 - jnp.exp(l)
            g = a.astype(F32) * _dot_nt(dos[p], vblk)
            gs = _suffix_sum(g, u)
            dz = g - beta * (dsum[p] - (gs if gc is None else gs + gc))
            if diag:
                dz = jnp.where(past, dz, 0.0)
            dzb = dz.astype(BF16)
            dk_acc[pl.ds(ks, tk), cols[p]] += _dot_tn(dzb, qs[p])
            dv_acc[pl.ds(ks, tk), cols[p]] += _dot_tn(a, dos[p])
            return (_dot(dzb, kblk), jnp.sum(l, axis=1, keepdims=True),
                    jnp.sum(g, axis=1, keepdims=True))

        def finish(dq_sum):
            for p in range(pairs):
                dq_ref[:, cols[p]] = (_unstack_heads(dq_sum[p], tq) * scale).astype(BF16)

        def straight(n):
            dq_sum, c, gc = [None] * pairs, [None] * pairs, [None] * pairs
            for b in range(n):
                for p in range(pairs):
                    dq_b, c_b, g_b = block(qi - b, p, c[p], gc[p], b == 0)
                    dq_sum[p] = dq_b if b == 0 else dq_sum[p] + dq_b
                    c[p] = c_b if b == 0 else c[p] + c_b
                    gc[p] = g_b if b == 0 else gc[p] + g_b
            return dq_sum, c, gc

        for n in range(1, n_pre):
            @pl.when(qi == n - 1)
            def _(n=n):
                finish(straight(n)[0])

        @pl.when(qi >= n_pre - 1)
        def _():
            dq_sum, c, gc = straight(n_pre)
            for p in range(pairs):
                dq_acc[p] = dq_sum[p]

            def cond(st):
                kb, c, gc = st
                return jnp.logical_and(kb >= 0, _max_of(c) > EXP_ZERO_BELOW)

            def step(st):
                kb, c, gc = st
                new_c, new_gc = [], []
                for p in range(pairs):
                    dq_n, c_n, g_n = block(kb, p, c[p], gc[p], False)
                    dq_acc[p] += dq_n
                    new_c.append(c[p] + c_n)
                    new_gc.append(gc[p] + g_n)
                return kb - 1, tuple(new_c), tuple(new_gc)

            lax.while_loop(cond, step, (qi - n_pre, tuple(c), tuple(gc)))
            finish([dq_acc[p] for p in range(pairs)])

        @pl.when(qi == nq - 1)
        def _():
            dk_ref[...] = dk_acc[...].astype(BF16)
            dv_ref[...] = dv_acc[...].astype(BF16)

    cols = [slice(p * LANES, (p + 1) * LANES) for p in range(pairs)]
    width, ncb = pairs * LANES, nhp // pairs
    qspec = pl.BlockSpec((tq, width), lambda b, p, i: (b * nq + i, p))
    sspec = pl.BlockSpec((seq, width), lambda b, p, i: (b, p))
    out = jax.ShapeDtypeStruct((T, D), BF16)
    carried = _Carried(comm, (batch, ncb, nq))
    return carried.pallas_call(
        body,
        in_specs=[qspec,
                  pl.BlockSpec((seq, width), lambda b, p, i: (b, ncb + p)),
                  pl.BlockSpec((seq, width), lambda b, p, i: (b, 2 * ncb + p)),
                  qspec, qspec],
        out_specs=[qspec, sspec, sspec], out_shape=[out, out, out],
        scratch_shapes=[pltpu.VMEM((pairs, 2 * tq, LANES), F32), pltpu.VMEM((seq, width), F32),
                        pltpu.VMEM((seq, width), F32)],
        args=[qkv, qkv, qkv, o, do], name=name)


_GELU_C = 0.7978845608028654
_GELU_A = 0.044715


def _gelu(x):
    x = x.astype(BF16)
    xx = x * x
    a1 = 1.0 + jnp.tanh(x * (_GELU_C + (_GELU_C * _GELU_A) * xx))
    hx = 0.5 * x
    grad = a1 * (0.5 + (hx * (2.0 - a1)) * (_GELU_C + (3.0 * _GELU_C * _GELU_A) * xx))
    return hx * a1, grad


def _causal_ws(ws_ref, g):
    t = lax.broadcasted_iota(jnp.int32, (SGU_CHUNK, SGU_CHUNK), 0)
    s = lax.broadcasted_iota(jnp.int32, (SGU_CHUNK, SGU_CHUNK), 1)
    return jnp.where(s <= t, ws_ref[g], 0.0)


def _sgu_fwd(uv, gain, ws, bst, *, name):
    T, F2 = uv.shape
    F = F2 // 2
    C, G, W = SGU_CHUNK, SGU_GROUPS, SGU_GROUP_W

    def body(uv_ref, g_ref, ws_ref, bs_ref, y_ref):
        u, v = uv_ref[:, :F].astype(F32), uv_ref[:, F:].astype(F32)
        r = lax.rsqrt(jnp.mean(v * v, axis=-1, keepdims=True) + EPS)
        vn = (v * r * g_ref[...]).astype(BF16)
        for g in range(G):
            sl = slice(g * W, (g + 1) * W)
            mixed = _dot(_causal_ws(ws_ref, g).astype(BF16), vn[:, sl]) + bs_ref[:, g:g + 1]
            y_ref[:, sl] = (u[:, sl] * mixed).astype(BF16)

    return pl.pallas_call(
        body, grid=(T // C,),
        in_specs=[pl.BlockSpec((C, F2), lambda i: (i, 0)), pl.BlockSpec((1, F), lambda i: (0, 0)),
                  pl.BlockSpec((G, C, C), lambda i: (0, 0, 0)), pl.BlockSpec((C, G), lambda i: (0, 0))],
        out_specs=pl.BlockSpec((C, F), lambda i: (i, 0)),
        out_shape=jax.ShapeDtypeStruct((T, F), BF16),
        compiler_params=_params("parallel"), name=name)(uv, gain, ws, bst)


def _sgu_bwd(uv, dgelu, dy, gain, ws, bst, *, name):
    T, F2 = uv.shape
    F = F2 // 2
    C, G, W = SGU_CHUNK, SGU_GROUPS, SGU_GROUP_W
    nc = T // C

    def body(uv_ref, dgelu_ref, dy_ref, g_ref, ws_ref, bs_ref, duv_ref, dg_ref, dws_ref, dbs_ref,
             dg_acc, dws_acc, dbs_acc):
        i = pl.program_id(0)

        @pl.when(i == 0)
        def _():
            dg_acc[...] = jnp.zeros_like(dg_acc)
            dws_acc[...] = jnp.zeros_like(dws_acc)
            dbs_acc[...] = jnp.zeros_like(dbs_acc)

        u, v = uv_ref[:, :F].astype(F32), uv_ref[:, F:].astype(F32)
        dgelu = dgelu_ref[...].astype(F32)
        r = lax.rsqrt(jnp.mean(v * v, axis=-1, keepdims=True) + EPS)
        vhat = v * r
        gain_v = g_ref[...]
        vn = (vhat * gain_v).astype(BF16)
        dyv = dy_ref[...].astype(F32)
        lane8 = lax.broadcasted_iota(jnp.int32, (1, G), 1)
        dvn_parts = []
        dbs_new = jnp.zeros((C, G), F32)
        for g in range(G):
            sl = slice(g * W, (g + 1) * W)
            wsg = _causal_ws(ws_ref, g)
            mixed = _dot(wsg.astype(BF16), vn[:, sl]) + bs_ref[:, g:g + 1]
            duv_ref[:, sl] = (dyv[:, sl] * mixed * dgelu[:, sl]).astype(BF16)
            dmix = dyv[:, sl] * u[:, sl]
            dbs_new = dbs_new + jnp.where(lane8 == g, jnp.sum(dmix, axis=1, keepdims=True), 0.0)
            dmix_b = dmix.astype(BF16)
            dws_acc[g] += _dot_nt(dmix_b, vn[:, sl])
            dvn_parts.append(_dot(wsg.T.astype(BF16), dmix_b))
        dbs_acc[...] += dbs_new
        dvn = jnp.concatenate(dvn_parts, axis=1)
        dg_acc[...] += jnp.sum((dvn * vhat).reshape(C // SUBLANES, SUBLANES, F), axis=0)
        dvhat = dvn * gain_v
        dv = r * (dvhat - vhat * jnp.mean(dvhat * vhat, axis=-1, keepdims=True))
        duv_ref[:, F:] = (dv * dgelu[:, F:]).astype(BF16)

        @pl.when(i == nc - 1)
        def _():
            dg_ref[...] = jnp.sum(dg_acc[...], axis=0, keepdims=True)
            t = lax.broadcasted_iota(jnp.int32, (G, C, C), 1)
            s = lax.broadcasted_iota(jnp.int32, (G, C, C), 2)
            dws_ref[...] = jnp.where(s <= t, dws_acc[...], 0.0)
            dbs_ref[...] = dbs_acc[...]

    return pl.pallas_call(
        body, grid=(nc,),
        in_specs=[pl.BlockSpec((C, F2), lambda i: (i, 0)), pl.BlockSpec((C, F2), lambda i: (i, 0)),
                  pl.BlockSpec((C, F), lambda i: (i, 0)),
                  pl.BlockSpec((1, F), lambda i: (0, 0)), pl.BlockSpec((G, C, C), lambda i: (0, 0, 0)),
                  pl.BlockSpec((C, G), lambda i: (0, 0))],
        out_specs=[pl.BlockSpec((C, F2), lambda i: (i, 0)), pl.BlockSpec((1, F), lambda i: (0, 0)),
                   pl.BlockSpec((G, C, C), lambda i: (0, 0, 0)), pl.BlockSpec((C, G), lambda i: (0, 0))],
        out_shape=[jax.ShapeDtypeStruct((T, F2), BF16), jax.ShapeDtypeStruct((1, F), F32),
                   jax.ShapeDtypeStruct((G, C, C), F32), jax.ShapeDtypeStruct((C, G), F32)],
        scratch_shapes=[pltpu.VMEM((SUBLANES, F), F32), pltpu.VMEM((G, C, C), F32), pltpu.VMEM((C, G), F32)],
        compiler_params=_params("arbitrary"), name=name)(uv, dgelu, dy, gain, ws, bst)


def _my_place():
    return lax.axis_index("x"), lax.axis_index("y"), lax.axis_index("c")


def _all_gather(shards, *, name):
    nf = len(shards)
    items = [(lambda ins, f=f: ins[f], lambda outs, p, f=f: outs[f].at[:, p]) for f in range(nf)]
    targets = [jax.ShapeDtypeStruct((s.shape[0], N_DEV) + s.shape[1:], s.dtype) for s in shards]
    return _run_comm(_Comm(shards, targets, len(items), _gather_emit(items, [list(range(nf))])), name=name)


class _Comm:
    def __init__(self, sources, targets, n_items, emit):
        self.sources, self.targets, self.n_items, self.emit = list(sources), list(targets), n_items, emit
        self.filled = [t for t in self.targets if not isinstance(t, jax.ShapeDtypeStruct)]
        self.operands = self.sources + self.filled
        self.out_shapes = [jax.ShapeDtypeStruct(t.shape, t.dtype) for t in self.targets]
        self.sems = [pltpu.SemaphoreType.DMA((n_items, 7)), pltpu.SemaphoreType.DMA((n_items, 7)),
                     pltpu.SemaphoreType.DMA((n_items,))]

    def aliases(self, first_operand, first_result):
        pos = {id(t): k for k, t in enumerate(self.targets)}
        return {first_operand + len(self.sources) + a: first_result + pos[id(t)]
                for a, t in enumerate(self.filled)}


def _run_comm(comm, *, name):
    n_op, n_out = len(comm.operands), len(comm.targets)

    def body(*refs):
        comm.emit(0, 1, refs[:len(comm.sources)], refs[n_op:n_op + n_out], refs[n_op + n_out:])

    return pl.pallas_call(
        body, in_specs=[ANY] * n_op, out_specs=[ANY] * n_out, out_shape=comm.out_shapes,
        scratch_shapes=comm.sems, input_output_aliases=comm.aliases(0, 0), name=name)(*comm.operands)


def _at_steps(step, n_steps, phases):
    if n_steps == 1:
        for _, fn in phases:
            fn()
        return
    marks = {}
    for frac, fn in phases:
        marks.setdefault(min(int(frac * n_steps), n_steps - 1), []).append(fn)
    for mark, fns in sorted(marks.items()):
        @pl.when(step == mark)
        def _(fns=fns):
            for fn in fns:
                fn()


def _gather_emit(items, groups, fractions=None):
    if fractions is None:
        fractions = [(g + 1) / len(groups) for g in range(len(groups))]
    def emit(step, n_steps, ins, outs, sems):
        send_sems, recv_sems, local_sems = sems
        x, y, c = _my_place()
        me, sibling = (x, y, c), (x, y, 1 - c)
        chips = [(1 - x, y), (x, 1 - y), (1 - x, 1 - y)]

        def copy(i, k, block, to, own=False):
            src_of, dst_of = items[i]
            dst = dst_of(outs, 4 * block[0] + 2 * block[1] + block[2])
            return pltpu.make_async_remote_copy(
                src_ref=src_of(ins) if own else dst, dst_ref=dst,
                send_sem=send_sems.at[i, k], recv_sem=recv_sems.at[i, k],
                device_id=to, device_id_type=MESH)

        def local(i):
            src_of, dst_of = items[i]
            return pltpu.make_async_copy(src_of(ins), dst_of(outs, 4 * x + 2 * y + c), local_sems.at[i])

        def first(i):
            return [copy(i, 0, me, sibling, own=True)] + [
                copy(i, 1 + j, me, (*chip, c), own=True) for j, chip in enumerate(chips)]

        def start():
            for i in range(len(items)):
                local(i).start()
                for cp in first(i):
                    cp.start()

        def forward(group):
            for j, chip in enumerate(chips):
                for i in group:
                    copy(i, 1 + j, (*chip, c), me).wait_recv()
                    copy(i, 4 + j, (*chip, c), sibling).start()

        def finish():
            for i in range(len(items)):
                copy(i, 0, sibling, me).wait_recv()
                for j, chip in enumerate(chips):
                    copy(i, 4 + j, (*chip, 1 - c), me).wait_recv()
            for i in range(len(items)):
                for cp in first(i) + [copy(i, 4 + j, (*chip, c), sibling) for j, chip in enumerate(chips)]:
                    cp.wait_send()
                local(i).wait()

        phases = [(0.0, start)]
        for frac, group in zip(fractions, groups):
            phases.append((frac, functools.partial(forward, group)))
        phases.append((1.0, finish))
        _at_steps(step, n_steps, phases)

    return emit


def _exchange_emit(items):
    def emit(step, n_steps, ins, outs, sems):
        send_sems, recv_sems, local_sems = sems
        x, y, c = _my_place()
        me = 4 * x + 2 * y + c

        def peer_of(k):
            return x ^ (k >> 2), y ^ ((k >> 1) & 1), c ^ (k & 1)

        def copy(i, k):
            src_of, dst_of = items[i]
            px, py, pc = peer_of(k)
            return pltpu.make_async_remote_copy(
                src_ref=src_of(ins, 4 * px + 2 * py + pc), dst_ref=dst_of(outs, me),
                send_sem=send_sems.at[i, k - 1], recv_sem=recv_sems.at[i, k - 1],
                device_id=(px, py, pc), device_id_type=MESH)

        def arrival(i, k):
            src_of, dst_of = items[i]
            px, py, pc = peer_of(k)
            peer = 4 * px + 2 * py + pc
            return pltpu.make_async_remote_copy(
                src_ref=src_of(ins, peer), dst_ref=dst_of(outs, peer),
                send_sem=send_sems.at[i, k - 1], recv_sem=recv_sems.at[i, k - 1],
                device_id=(x, y, c), device_id_type=MESH)

        def local(i):
            src_of, dst_of = items[i]
            return pltpu.make_async_copy(src_of(ins, me), dst_of(outs, me), local_sems.at[i])

        def start():
            for i in range(len(items)):
                local(i).start()
            for k in range(1, N_DEV):
                for i in range(len(items)):
                    copy(i, k).start()

        def finish():
            for k in range(1, N_DEV):
                for i in range(len(items)):
                    arrival(i, k).wait_recv()
            for k in range(1, N_DEV):
                for i in range(len(items)):
                    copy(i, k).wait_send()
            for i in range(len(items)):
                local(i).wait()

        _at_steps(step, n_steps, [(0.0, start), (1.0, finish)])

    return emit


def _adam_math(g, w, m, v):
    m = ADAM_B1 * m + (1.0 - ADAM_B1) * g
    v = ADAM_B2 * v + (1.0 - ADAM_B2) * (g * g)
    m_hat = m / (1.0 - ADAM_B1 ** ADAM_STEP)
    v_hat = v / (1.0 - ADAM_B2 ** ADAM_STEP)
    delta = -ADAM_LR * (m_hat / (jnp.sqrt(v_hat) + ADAM_EPS) + ADAM_WD * w)
    return delta, m, v


def _sum_adamw(parts, w, m, v, *, name, tr=256):
    L, nd, R, C = parts.shape
    tr = min(tr, R)
    assert R % tr == 0

    def body(p_ref, w_ref, m_ref, v_ref, g_ref, d_ref, nm_ref, nv_ref):
        g = p_ref[0, 0].astype(F32)
        for q in range(1, nd):
            g = g + p_ref[0, q].astype(F32)
        d, nm, nv = _adam_math(g, w_ref[0], m_ref[0], v_ref[0])
        g_ref[0] = g
        d_ref[0] = d
        nm_ref[0] = nm
        nv_ref[0] = nv

    blk = pl.BlockSpec((1, tr, C), lambda l, i: (l, i, 0))
    out = jax.ShapeDtypeStruct((L, R, C), F32)
    return pl.pallas_call(
        body, grid=(L, R // tr),
        in_specs=[pl.BlockSpec((1, nd, tr, C), lambda l, i: (l, 0, i, 0)), blk, blk, blk],
        out_specs=[blk] * 4, out_shape=[out] * 4,
        compiler_params=_params("parallel", "parallel"), name=name)(parts, w, m, v)


def _sum_parts(parts, *, name):
    nd, R, C = parts.shape

    def body(p_ref, o_ref):
        g = p_ref[0]
        for q in range(1, nd):
            g = g + p_ref[q]
        o_ref[...] = g

    return pl.pallas_call(
        body, out_shape=jax.ShapeDtypeStruct((R, C), F32),
        in_specs=[pl.BlockSpec(memory_space=pltpu.VMEM)],
        out_specs=pl.BlockSpec(memory_space=pltpu.VMEM), name=name)(parts)


def _adamw_small(g, w, m, v, *, name):
    def body(g_ref, w_ref, m_ref, v_ref, d_ref, nm_ref, nv_ref):
        d, nm, nv = _adam_math(g_ref[...], w_ref[...], m_ref[...], v_ref[...])
        d_ref[...] = d
        nm_ref[...] = nm
        nv_ref[...] = nv

    vm = pl.BlockSpec(memory_space=pltpu.VMEM)
    out = jax.ShapeDtypeStruct(g.shape, F32)
    return pl.pallas_call(body, out_shape=[out] * 3, in_specs=[vm] * 4, out_specs=[vm] * 3,
                          name=name)(g, w, m, v)


def kernel(x, norm_mix, norm_mlp, sb_wqkv, sb_wo, sgu_win, sgu_gain, sgu_ws, sgu_bs, sgu_wout, mlp_w1, mlp_w2, final_norm, loss_target, m_norm_mix, m_norm_mlp, m_sb_wqkv, m_sb_wo, m_sgu_win, m_sgu_gain, m_sgu_ws, m_sgu_bs, m_sgu_wout, m_mlp_w1, m_mlp_w2, m_final_norm, v_norm_mix, v_norm_mlp, v_sb_wqkv, v_sb_wo, v_sgu_win, v_sgu_gain, v_sgu_ws, v_sgu_bs, v_sgu_wout, v_mlp_w1, v_mlp_w2, v_final_norm):
    batch, seq, D = x.shape
    T = batch * seq
    x0 = x.reshape(T, D)
    target = loss_target.reshape(T, D)

    WQKV, WO, WIN, WOUT, W1, W2, GAIN = range(7)
    big = [sb_wqkv, sb_wo, sgu_win, sgu_wout, mlp_w1, mlp_w2]
    shards = [w.astype(BF16) for w in big] + [sgu_gain[:, None, :]]
    rides = {
        "qkv0": [[(WO, 0), (W1, 0)]],
        "sb_fwd0": [[(W2, 0)], [(WIN, 0), (WOUT, 0), (W1, 1), (W2, 1)], [(WQKV, 1)]],
        "qkv2": [[(WO, 1), (W1, 2)]],
        "sb_fwd2": [[(W2, 2)], [(WIN, 1), (WOUT, 1), (W1, 3), (W2, 3)]],
    }

    first = [(lambda ins: ins[WQKV].at[0], lambda outs, p: outs[WQKV].at[0, p]),
             (lambda ins: ins[GAIN], lambda outs, p: outs[GAIN].at[:, p])]
    targets = [jax.ShapeDtypeStruct((s.shape[0], N_DEV) + s.shape[1:], s.dtype) for s in shards]
    gathered0 = _run_comm(_Comm(shards, targets, len(first), _gather_emit(first, [[0, 1]])), name="gather_first")
    gain_sgu = gathered0[GAIN].reshape(-1, 1, SGU_FFN)
    gw = dict(enumerate(gathered0[:GAIN]))

    def gather_ride(call):
        wanted = [fl for group in rides[call] for fl in group]
        fams = sorted({f for f, _ in wanted})
        items = [(lambda ins, f=f, l=l: ins[f].at[l], lambda outs, p, t=fams.index(f), l=l: outs[t].at[l, p])
                 for f, l in wanted]
        groups, sent, k = [], [], 0
        for group in rides[call]:
            groups.append(list(range(k, k + len(group))))
            sent.append(sum(shards[f].shape[1] * shards[f].shape[2] for f, _ in group) + (sent[-1] if sent else 0))
            k += len(group)
        fractions = [s / sent[-1] for s in sent]
        return _Comm(shards[:GAIN], [gw[f] for f in fams], len(items), _gather_emit(items, groups, fractions)), fams

    def weight(f):
        g = gw[f]
        return g if f in (WQKV, WIN, W1) else g.reshape(g.shape[0], 1, N_DEV * g.shape[2], g.shape[3])

    saved = []
    xs = x0
    for i in range(DEPTH):
        j = i // 2
        if i % 2 == 0:
            comm, fams = gather_ride(f"qkv{i}")
            (qkv, h), results = _mm_nn(xs, weight(WQKV), j, out_dtype=BF16, gain=norm_mix[i:i + 1],
                                       name=f"qkv{i}", comm=comm, tm=1024)
            gw.update(zip(fams, results))
            comm, fams = gather_ride(f"sb_fwd{i}")
            o, results = _sb_fwd(qkv, batch=batch, seq=seq, name=f"sb_fwd{i}", comm=comm)
            gw.update(zip(fams, results))
            x_mid = _mm_nn(o, weight(WO), j, out_dtype=F32, res=xs, name=f"wo{i}", tm=1024)
            mix = (h, qkv, o)
        else:
            gain_j = gain_sgu[j]
            bst = sgu_bs[j].T
            uv, h, dgelu = _mm_nn(xs, weight(WIN), j, out_dtype=BF16, gain=norm_mix[i:i + 1], gelu=True,
                                  name=f"win{i}")
            yv = _sgu_fwd(uv, gain_j, sgu_ws[j], bst, name=f"sgu_fwd{i}")
            x_mid = _mm_nn(yv, weight(WOUT), j, out_dtype=F32, res=xs, name=f"wout{i}", tm=1024)
            mix = (h, uv, dgelu, yv, gain_j, bst)
        a, h2 = _mm_nn(x_mid, weight(W1), i, out_dtype=BF16, gain=norm_mlp[i:i + 1], name=f"w1_{i}", tm=1024)
        x_out = _mm_nn(a, weight(W2), i, out_dtype=F32, res=x_mid, a_act="relu2", name=f"w2_{i}")
        saved.append((xs, mix, x_mid, h2, a))
        xs = x_out
    g_wqkv, g_wo, g_win, g_wout, g_w1, g_w2 = [weight(f) for f in range(GAIN)]

    dx, dxb, sq, d_final = _loss_head(xs, final_norm.reshape(1, D), target, name="loss_head")
    loss = lax.psum(0.5 * jnp.sum(sq) / D, ("x", "y", "c"))

    SMALL = GAIN
    stacks = {f: jax.ShapeDtypeStruct((w.shape[0], N_DEV) + w.shape[1:], BF16) for f, w in enumerate(big)}
    pending = []

    def row_shards(p):
        return p.reshape(N_DEV, p.shape[1] // N_DEV, p.shape[2])

    def exchange_pending(newest_only=False):
        going = [pending.pop()] if newest_only else [pending.pop(0) for _ in range(len(pending))]
        fams = sorted({f for _, f, _ in going})
        items = [(lambda ins, p, a=a: ins[a].at[p], lambda outs, q, t=fams.index(f), l=l: outs[t].at[l, q])
                 for a, (_, f, l) in enumerate(going)]
        return _Comm([part for part, _, _ in going], [stacks[f] for f in fams], len(items), _exchange_emit(items)), fams

    d_norm_mix, d_norm_mlp = [None] * DEPTH, [None] * DEPTH
    d_gain, d_ws, d_bs = [None] * 2, [None] * 2, [None] * 2
    for i in reversed(range(DEPTH)):
        j = i // 2
        xs, mix, x_mid, h2, a = saved[i]
        da = _mm_nt(dxb, g_w2, i, out_dtype=BF16, act_src=a, name=f"d_a{i}")
        pending.append((row_shards(_mm_tn(a, dxb, shards=1, a_act="relu2", name=f"d_w2_{i}")), W2, i))
        pending.append((_mm_tn(h2, da, shards=N_DEV, name=f"d_w1_{i}", pb=4), W1, i))
        (dx, dxb, d_norm_mlp[i]), _ = _mm_nt_rms_bwd(da, g_w1, i, x_mid, norm_mlp[i:i + 1], dx, name=f"d_h2_{i}")
        if i % 2 == 0:
            h, qkv, o = mix
            do = _mm_nt(dxb, g_wo, j, out_dtype=BF16, name=f"d_o{i}", tm=1024)
            pending.append((row_shards(_mm_tn(o, dxb, shards=1, name=f"d_wo{i}")), WO, j))
            comm, fams = exchange_pending()
            (dq, dk, dv), results = _sb_bwd(qkv, o, do, batch=batch, seq=seq, name=f"sb_bwd{i}", comm=comm)
            stacks.update(zip(fams, results))
            dqkv = [dq, dk, dv]
            pending.append((_mm_tn(h, dqkv, shards=N_DEV, name=f"d_wqkv{i}", pb=N_DEV, tm=1024, tk=512), WQKV, j))
            mixer_in, w_in = dqkv, g_wqkv
        else:
            h, uv, dgelu, yv, gain_j, bst = mix
            dy = _mm_nt(dxb, g_wout, j, out_dtype=BF16, name=f"d_y{i}", tm=1024)
            pending.append((row_shards(_mm_tn(yv, dxb, shards=1, name=f"d_wout{i}")), WOUT, j))
            duv, d_gain[j], d_ws[j], dbst = _sgu_bwd(uv, dgelu, dy, gain_j, sgu_ws[j], bst, name=f"sgu_bwd{i}")
            d_bs[j] = dbst.T
            pending.append((_mm_tn(h, duv, shards=N_DEV, name=f"d_win{i}", pb=4), WIN, j))
            mixer_in, w_in = duv, g_win
        comm, fams = exchange_pending(newest_only=True)
        (dx, dxb, d_norm_mix[i]), results = _mm_nt_rms_bwd(
            mixer_in, w_in, j, xs, norm_mix[i:i + 1], dx, name=f"d_h_mix{i}", comm=comm)
        stacks.update(zip(fams, results))
    grad_x = dx.reshape(batch, seq, D)

    small = [jnp.concatenate(d_norm_mix, 0), jnp.concatenate(d_norm_mlp, 0), d_final,
             jnp.concatenate(d_gain, 0), jnp.stack(d_bs, 0), jnp.stack(d_ws, 0)]
    small_flat = jnp.concatenate([s.reshape(-1) for s in small])
    n_small = small_flat.shape[0]
    small_rows = -(-n_small // (N_DEV * SUBLANES * LANES)) * SUBLANES
    small_flat = jnp.pad(small_flat, (0, N_DEV * small_rows * LANES - n_small))
    stacks[SMALL] = jax.ShapeDtypeStruct((1, N_DEV, small_rows, LANES), F32)
    pending.append((small_flat.reshape(N_DEV, small_rows, LANES), SMALL, 0))
    comm, fams = exchange_pending()
    stacks.update(zip(fams, _run_comm(comm, name="exchange_last")))
    r_wqkv, r_wo, r_win, r_wout, r_w1, r_w2, r_small = [stacks[f] for f in range(SMALL + 1)]

    u_wqkv = _sum_adamw(r_wqkv, sb_wqkv, m_sb_wqkv, v_sb_wqkv, name="adamw_wqkv")
    u_wo = _sum_adamw(r_wo, sb_wo, m_sb_wo, v_sb_wo, name="adamw_wo")
    u_win = _sum_adamw(r_win, sgu_win, m_sgu_win, v_sgu_win, name="adamw_win")
    u_wout = _sum_adamw(r_wout, sgu_wout, m_sgu_wout, v_sgu_wout, name="adamw_wout")
    u_w1 = _sum_adamw(r_w1, mlp_w1, m_mlp_w1, v_mlp_w1, name="adamw_w1")
    u_w2 = _sum_adamw(r_w2, mlp_w2, m_mlp_w2, v_mlp_w2, name="adamw_w2")

    small_sum = _sum_parts(r_small[0], name="sum_small")
    g_small = _all_gather([small_sum[None]], name="gather_small")[0].reshape(-1)[:n_small]

    shapes = [s.shape for s in small]
    sizes = [s.size for s in small]
    offs = [sum(sizes[:k]) for k in range(len(sizes))]
    me = 4 * lax.axis_index("x") + 2 * lax.axis_index("y") + lax.axis_index("c")
    shard_w = SGU_FFN // N_DEV

    def pack(arrs):
        flat = jnp.concatenate([a_.reshape(-1) for a_ in arrs])
        return jnp.pad(flat, (0, N_DEV * small_rows * LANES - n_small)).reshape(-1, LANES)

    def full_gain(gshard):
        return lax.dynamic_update_slice(jnp.zeros((2, SGU_FFN), F32), gshard, (0, me * shard_w))

    w_small = pack([norm_mix, norm_mlp, final_norm, full_gain(sgu_gain), sgu_bs, sgu_ws])
    m_small = pack([m_norm_mix, m_norm_mlp, m_final_norm, full_gain(m_sgu_gain), m_sgu_bs, m_sgu_ws])
    v_small = pack([v_norm_mix, v_norm_mlp, v_final_norm, full_gain(v_sgu_gain), v_sgu_bs, v_sgu_ws])
    g_pack = jnp.pad(g_small, (0, N_DEV * small_rows * LANES - n_small)).reshape(-1, LANES)
    sm = [g_pack] + list(_adamw_small(g_pack, w_small, m_small, v_small, name="adamw_small"))

    def unpack(flat2d):
        flat = flat2d.reshape(-1)
        out = [flat[offs[k]:offs[k] + sizes[k]].reshape(shapes[k]) for k in range(len(sizes))]
        out[2] = out[2].reshape(D)
        out[3] = lax.dynamic_slice(out[3], (0, me * shard_w), (2, shard_w))
        return out

    outs = []
    for k, big_u in enumerate(zip(u_wqkv, u_wo, u_win, u_wout, u_w1, u_w2)):
        s_nm, s_nl, s_fn, s_gain, s_bs, s_ws = unpack(sm[k])
        b_wqkv, b_wo, b_win, b_wout, b_w1, b_w2 = big_u
        outs += [s_nm, s_nl, b_wqkv, b_wo, b_win, s_gain, s_ws, s_bs, b_wout, b_w1, b_w2, s_fn]
    return (loss, grad_x, *outs)
```

```python
import functools

import jax
import jax.numpy as jnp
from jax import lax
from jax.experimental import pallas as pl
from jax.experimental.pallas import tpu as pltpu

F32 = jnp.float32
BF16 = jnp.bfloat16

N_DEV = 8
D_MODEL = 1024
SEQ = 2048
DEPTH = 4
SB_HEAD_DIM = 64
SGU_CHUNK = 128
SGU_FFN = 2 * D_MODEL
SGU_GROUPS = 8
SGU_GROUP_W = SGU_FFN // SGU_GROUPS
EPS = 1e-6

ADAM_LR = 0.001
ADAM_B1 = 0.9
ADAM_B2 = 0.999
ADAM_EPS = 1e-08
ADAM_WD = 0.01
ADAM_STEP = 10

MXU_TILE = 256
LANES = 128
SUBLANES = 8
VMEM_LIMIT = 56 * 1024 * 1024
EXP_ZERO_BELOW = -104.0
SB_TILE = 256
SB_STRAIGHT = 2
SB_PAIRS = 4

MESH = pl.DeviceIdType.MESH
ANY = pl.BlockSpec(memory_space=pl.ANY)


def _params(*sem):
    return pltpu.CompilerParams(dimension_semantics=sem, vmem_limit_bytes=VMEM_LIMIT)


def _dot(a, b):
    return jnp.dot(a, b, preferred_element_type=F32)


def _dot_nt(a, b):
    return lax.dot_general(a, b, (((1,), (1,)), ((), ())), preferred_element_type=F32)


def _dot_tn(a, b):
    return lax.dot_general(a, b, (((0,), (0,)), ((), ())), preferred_element_type=F32)


def _split_bf16(x):
    hi = x.astype(BF16)
    lo = (x - hi.astype(F32)).astype(BF16)
    return hi, lo


def _relu2(av):
    t = jnp.maximum(av, jnp.zeros_like(av))
    return t * t


def _mm_nn(a, w, l, *, out_dtype, name, res=None, a_act=None, gain=None, sgu=None, comm=None,
           tm=512, kc=1024):
    M, K = a.shape
    _, P, K2, n = w.shape
    assert K2 == K
    tm, kc = min(tm, M), min(kc, K)
    assert M % tm == 0 and K % kc == 0
    join = 2 if n % MXU_TILE and P % 2 == 0 else 1

    n_main = 2 + (gain is not None) + (res is not None)
    n_in = n_main + (3 if sgu is not None else 0)
    n_out = 1 + (gain is not None) + (2 if sgu is not None else 0)
    gelu = sgu is not None

    def body(*all_refs):
        ins, outs, _ = carried.split(all_refs, n_in, n_out, 0)
        carried.emit(all_refs)
        refs = list(ins) + list(outs)
        a_ref, w_ref, o_ref = refs[0], refs[1], refs[n_in]
        if gain is not None:
            xv = a_ref[...]
            r = lax.rsqrt(jnp.mean(xv * xv, axis=-1, keepdims=True) + EPS)
            h = (xv * r * refs[2][...]).astype(BF16)
            refs[n_in + 1][...] = h
        for p in range(0, P, join):
            sl = slice(p * n, (p + join) * n)
            acc = None
            for k0 in range(0, K, kc):
                if gain is not None:
                    av = h[:, k0:k0 + kc]
                else:
                    av = a_ref[:, k0:k0 + kc]
                    av = _relu2(av) if a_act == "relu2" else av.astype(BF16)
                wv = [w_ref[0, p + s, k0:k0 + kc, :] for s in range(join)]
                d = _dot(av, wv[0] if join == 1 else jnp.concatenate(wv, axis=1))
                acc = d if acc is None else acc + d
            if res is not None:
                acc = acc + refs[n_main - 1][:, sl]
            if gelu:
                acc, dact = _gelu(acc)
                refs[-2][:, sl] = dact.astype(BF16)
            o_ref[:, sl] = acc.astype(out_dtype)
        if sgu is not None:
            half = P * n // 2
            for r0 in range(0, tm, SGU_CHUNK):
                rows = slice(r0, r0 + SGU_CHUNK)
                refs[-1][rows, :] = _sgu_gate(o_ref[rows, :half].astype(F32), o_ref[rows, half:].astype(F32),
                                              *refs[n_main:n_in])

    row = lambda width: pl.BlockSpec((tm, width), lambda i: (i, 0))
    whole = lambda arr: pl.BlockSpec(arr.shape, lambda i: (0,) * arr.ndim)
    in_specs = [row(K), pl.BlockSpec((1, P, K, n), lambda i: (l, 0, 0, 0), pipeline_mode=pl.Buffered(1))]
    args = [a, w]
    if gain is not None:
        in_specs.append(whole(gain))
        args.append(gain)
    if res is not None:
        in_specs.append(row(P * n))
        args.append(res)
    out_specs, out_shape = [row(P * n)], [jax.ShapeDtypeStruct((M, P * n), out_dtype)]
    if gain is not None:
        out_specs.append(row(K))
        out_shape.append(jax.ShapeDtypeStruct((M, K), BF16))
    if sgu is not None:
        in_specs += [whole(s) for s in sgu]
        args += list(sgu)
        out_specs += [row(P * n), row(P * n // 2)]
        out_shape += [jax.ShapeDtypeStruct((M, P * n), BF16), jax.ShapeDtypeStruct((M, P * n // 2), BF16)]
    carried = _Carried(comm, (M // tm,), last="parallel")
    outs, comm_results = carried.pallas_call(
        body, in_specs=in_specs, out_specs=out_specs, out_shape=out_shape, scratch_shapes=[],
        args=args, name=name)
    outs = tuple(outs) if n_out > 1 else outs[0]
    return outs if comm is None else (outs, comm_results)


def _mm_nt(a, w, l, *, out_dtype, name, act_src=None, tm=512, tn=1024):
    M, K = a.shape
    _, P, Nout, kc = w.shape
    assert P == 1 and K == kc
    tm, tn = min(tm, M), min(tn, Nout)
    assert M % tm == 0 and Nout % tn == 0

    def body(*refs):
        a_ref, w_ref, o_ref = refs[0], refs[1], refs[-1]
        av = a_ref[...].astype(BF16)
        for n0 in range(0, Nout, tn):
            r = _dot_nt(av, w_ref[0, 0, n0:n0 + tn, :])
            if act_src is not None:
                r = r * (2.0 * jnp.maximum(refs[2][:, n0:n0 + tn].astype(F32), 0.0))
            o_ref[:, n0:n0 + tn] = r.astype(out_dtype)

    row = lambda width: pl.BlockSpec((tm, width), lambda i: (i, 0))
    in_specs = [row(K), pl.BlockSpec((1, 1, Nout, K), lambda i: (l, 0, 0, 0), pipeline_mode=pl.Buffered(1))]
    args = [a, w]
    if act_src is not None:
        in_specs.append(row(Nout))
        args.append(act_src)
    return pl.pallas_call(
        body, grid=(M // tm,), in_specs=in_specs, out_specs=row(Nout),
        out_shape=jax.ShapeDtypeStruct((M, Nout), out_dtype),
        compiler_params=_params("parallel"), name=name)(*args)


def _mm_tn(a, b, *, shards, name, a_act=None, tm=2048, tk=1024, pb=1):
    b_parts = list(b) if isinstance(b, (list, tuple)) else [b]
    M, K = a.shape
    N = sum(part.shape[1] for part in b_parts)
    assert all(part.shape[0] == M for part in b_parts)
    n = N // shards
    tm, tk = min(tm, M), min(tk, K)
    assert M % tm == 0 and K % tk == 0 and shards % pb == 0
    width, nm = pb * n, M // tm
    assert len(b_parts) == 1 or width == N

    def body(*refs):
        a_ref, b_refs, o_ref, acc = refs[0], refs[1:-2], refs[-2], refs[-1]
        m = pl.program_id(2)
        av = a_ref[...]
        if a_act == "relu2":
            av = _relu2(av)
        bv = [r[...].astype(BF16) for r in b_refs]

        @pl.when(m == 0)
        def _():
            acc[...] = jnp.zeros_like(acc)

        acc[...] += _dot_tn(av.astype(BF16), bv[0] if len(bv) == 1 else jnp.concatenate(bv, axis=1))

        @pl.when(m == nm - 1)
        def _():
            for p in range(pb):
                o_ref[p] = acc[:, p * n:(p + 1) * n].astype(BF16)

    if len(b_parts) == 1:
        b_specs = [pl.BlockSpec((tm, width), lambda i, j, m: (m, j))]
    else:
        b_specs = [pl.BlockSpec((tm, part.shape[1]), lambda i, j, m: (m, 0)) for part in b_parts]
    return pl.pallas_call(
        body, grid=(K // tk, N // width, nm),
        in_specs=[pl.BlockSpec((tm, tk), lambda i, j, m: (m, i))] + b_specs,
        out_specs=pl.BlockSpec((pb, tk, n), lambda i, j, m: (j, i, 0)),
        out_shape=jax.ShapeDtypeStruct((shards, K, n), BF16),
        scratch_shapes=[pltpu.VMEM((tk, width), F32)],
        compiler_params=_params("parallel", "parallel", "arbitrary"), name=name)(a, *b_parts)


def _mm_nt_rms_bwd(a, w, l, x, gain, dres, *, name, comm=None, tm=512):
    a_parts = list(a) if isinstance(a, (list, tuple)) else [a]
    na = len(a_parts)
    M, K = a_parts[0].shape[0], sum(part.shape[1] for part in a_parts)
    _, P, D, kc = w.shape
    assert K == P * kc and x.shape == (M, D)
    tm = min(tm, M)
    nr = M // tm
    join = 2 if kc % MXU_TILE and P % 2 == 0 else 1

    def body(*refs):
        ins, (dx_ref, dxb_ref, dg_ref), (acc,) = carried.split(refs, na + 4, 3, 1)
        a_refs, (w_ref, x_ref, g_ref, dres_ref) = ins[:na], ins[na:]
        i = pl.program_id(0)
        carried.emit(refs)
        av = a_refs[0][...] if na == 1 else jnp.concatenate([r[...] for r in a_refs], axis=1)
        dhv = None
        for p in range(0, P, join):
            wv = [w_ref[0, p + s] for s in range(join)]
            d = _dot_nt(av[:, p * kc:(p + join) * kc], wv[0] if join == 1 else jnp.concatenate(wv, axis=1))
            dhv = d if dhv is None else dhv + d
        xv = x_ref[...]
        r = lax.rsqrt(jnp.mean(xv * xv, axis=-1, keepdims=True) + EPS)
        xhat = xv * r
        dxhat = dhv * g_ref[...]
        dx = dres_ref[...] + r * (dxhat - xhat * jnp.mean(dxhat * xhat, axis=-1, keepdims=True))
        dx_ref[...] = dx
        dxb_ref[...] = dx.astype(BF16)
        part = jnp.sum((dhv * xhat).reshape(tm // SUBLANES, SUBLANES, D), axis=0)

        @pl.when(i == 0)
        def _():
            acc[...] = jnp.zeros_like(acc)

        acc[...] += part

        @pl.when(i == nr - 1)
        def _():
            dg_ref[...] = jnp.sum(acc[...], axis=0, keepdims=True)

    row = pl.BlockSpec((tm, D), lambda i: (i, 0))
    vec = pl.BlockSpec((1, D), lambda i: (0, 0))
    carried = _Carried(comm, (nr,))
    return carried.pallas_call(
        body,
        in_specs=[pl.BlockSpec((tm, part.shape[1]), lambda i: (i, 0)) for part in a_parts] + [
            pl.BlockSpec((1, P, D, kc), lambda i: (l, 0, 0, 0), pipeline_mode=pl.Buffered(1)),
            row, vec, row],
        out_specs=[row, row, vec],
        out_shape=[jax.ShapeDtypeStruct((M, D), F32), jax.ShapeDtypeStruct((M, D), BF16),
                   jax.ShapeDtypeStruct((1, D), F32)],
        scratch_shapes=[pltpu.VMEM((SUBLANES, D), F32)],
        args=a_parts + [w, x, gain, dres], name=name)


def _mlp_out_loss_head(a, w, l, res, gain, target, *, name, tr=512, kc=1024):
    T, K = a.shape
    D = w.shape[3]
    nr = T // tr

    def body(a_ref, w_ref, res_ref, g_ref, t_ref, dx_ref, dxb_ref, sq_ref, dg_ref, sq_acc, dg_acc):
        i = pl.program_id(0)
        xv = res_ref[...]
        for k0 in range(0, K, kc):
            xv = xv + _dot(_relu2(a_ref[:, k0:k0 + kc]), w_ref[0, 0, k0:k0 + kc, :])
        g = g_ref[...]
        r = lax.rsqrt(jnp.mean(xv * xv, axis=-1, keepdims=True) + EPS)
        xhat = xv * r
        err = xhat * g - t_ref[...]
        dy = err * (1.0 / D)
        dxhat = dy * g
        dx = r * (dxhat - xhat * jnp.mean(dxhat * xhat, axis=-1, keepdims=True))
        dx_ref[...] = dx
        dxb_ref[...] = dx.astype(BF16)
        sq = jnp.sum((err * err).reshape(tr // SUBLANES, SUBLANES, D), axis=0)
        dg = jnp.sum((dy * xhat).reshape(tr // SUBLANES, SUBLANES, D), axis=0)

        @pl.when(i == 0)
        def _():
            sq_acc[...] = sq
            dg_acc[...] = dg

        @pl.when(i > 0)
        def _():
            sq_acc[...] += sq
            dg_acc[...] += dg

        @pl.when(i == nr - 1)
        def _():
            sq_ref[...] = sq_acc[...]
            dg_ref[...] = jnp.sum(dg_acc[...], axis=0, keepdims=True)

    row = pl.BlockSpec((tr, D), lambda i: (i, 0))
    vec = pl.BlockSpec((1, D), lambda i: (0, 0))
    part = pl.BlockSpec((SUBLANES, D), lambda i: (0, 0))
    return pl.pallas_call(
        body, grid=(nr,),
        in_specs=[pl.BlockSpec((tr, K), lambda i: (i, 0)),
                  pl.BlockSpec((1, 1, K, D), lambda i: (l, 0, 0, 0), pipeline_mode=pl.Buffered(1)),
                  row, vec, row],
        out_specs=[row, row, part, vec],
        out_shape=[jax.ShapeDtypeStruct((T, D), F32), jax.ShapeDtypeStruct((T, D), BF16),
                   jax.ShapeDtypeStruct((SUBLANES, D), F32), jax.ShapeDtypeStruct((1, D), F32)],
        scratch_shapes=[pltpu.VMEM((SUBLANES, D), F32), pltpu.VMEM((SUBLANES, D), F32)],
        compiler_params=_params("arbitrary"), name=name)(a, w, res, gain, target)


def _head0_lanes():
    return lax.broadcasted_iota(jnp.int32, (1, LANES), 1) < SB_HEAD_DIM


def _stack_heads(x):
    zero = jnp.zeros_like(x)
    h0 = _head0_lanes()
    return jnp.concatenate([jnp.where(h0, x, zero), jnp.where(h0, zero, x)], axis=0)


def _unstack_heads(y, tq):
    return jnp.where(_head0_lanes(), y[:tq], y[tq:])


def _past_mask(tq, tk):
    row = lax.broadcasted_iota(jnp.int32, (2 * tq, tk), 0) & (tq - 1)
    col = lax.broadcasted_iota(jnp.int32, (2 * tq, tk), 1)
    return col < row


def _max_of(arrays):
    return functools.reduce(jnp.maximum, [jnp.max(a) for a in arrays])


def _sb_logs(z, past):
    minus_abs = lax.bitcast_convert_type(
        lax.bitcast_convert_type(z, jnp.uint32) | jnp.uint32(0x80000000), F32)
    log_beta = jnp.minimum(z, 0.0) - jnp.log(1.0 + jnp.exp(minus_abs))
    l = log_beta - z
    if past is not None:
        l = jnp.where(past, l, 0.0)
    return log_beta, l


def _suffix_matrix(tk):
    j = lax.broadcasted_iota(jnp.int32, (2 * tk, tk), 0) & (tk - 1)
    s = lax.broadcasted_iota(jnp.int32, (2 * tk, tk), 1)
    return (j > s).astype(BF16)


def _suffix_sum(x, u2, exact=True):
    if not exact:
        return _dot(x.astype(BF16), u2[:x.shape[1]])
    hi, lo = _split_bf16(x)
    return _dot(jnp.concatenate([hi, lo], axis=1), u2)


class _Carried:
    def __init__(self, comm, grid, last="arbitrary"):
        self.comm, self.grid, self.last = comm, grid, last
        self.n_op = len(comm.operands) if comm else 0
        self.n_tgt = len(comm.targets) if comm else 0

    def split(self, refs, n_in, n_out, n_scratch):
        self.n_in, self.n_out, self.n_scratch = n_in, n_out, n_scratch
        a = n_in + self.n_op
        b = a + n_out + self.n_tgt
        return refs[:n_in], refs[a:a + n_out], refs[b:b + n_scratch]

    def emit(self, refs):
        if self.comm is None:
            return
        step, n_steps = 0, 1
        for d, size in enumerate(self.grid):
            step = step * size + pl.program_id(d)
            n_steps *= size
        a = self.n_in + self.n_op + self.n_out
        self.comm.emit(step, n_steps, refs[self.n_in:self.n_in + len(self.comm.sources)],
                       refs[a:a + self.n_tgt], refs[a + self.n_tgt + self.n_scratch:])

    def pallas_call(self, body, *, in_specs, out_specs, out_shape, scratch_shapes, args, name):
        comm = self.comm
        n_in, n_out = len(in_specs), len(out_specs)
        aliases = {}
        if comm is not None:
            in_specs = in_specs + [ANY] * self.n_op
            out_specs = out_specs + [ANY] * self.n_tgt
            out_shape = out_shape + comm.out_shapes
            scratch_shapes = scratch_shapes + comm.sems
            args = args + comm.operands
            aliases = comm.aliases(n_in, n_out)
        sem = ("parallel",) * (len(self.grid) - 1) + (self.last,) if comm is None else ("arbitrary",) * len(self.grid)
        results = pl.pallas_call(
            body, grid=self.grid, in_specs=in_specs, out_specs=out_specs, out_shape=out_shape,
            scratch_shapes=scratch_shapes, input_output_aliases=aliases,
            compiler_params=_params(*sem), name=name)(*args)
        return results[:n_out], results[n_out:]


def _sb_fwd(qkv, *, batch, seq, name, comm=None, tq=SB_TILE, n_pre=SB_STRAIGHT, pairs=SB_PAIRS):
    T, D3 = qkv.shape
    D = D3 // 3
    nhp = D // LANES
    tk = tq
    nq = seq // tq
    scale = SB_HEAD_DIM ** -0.5
    assert 1 <= n_pre <= nq and nhp % pairs == 0

    def body(*refs):
        (q_ref, k_ref, v_ref), (o_ref,), (acc,) = carried.split(refs, 3, 1, 1)
        qi = pl.program_id(2)
        carried.emit(refs)
        qs = [_stack_heads(q_ref[:, lanes]) * scale for lanes in cols]
        past = _past_mask(tq, tk)
        u = _suffix_matrix(tk)

        def block(kb, g, c, diag):
            ks = pl.multiple_of(kb * tk, tk)
            z = _dot_nt(qs[g], k_ref[pl.ds(ks, tk), cols[g]])
            log_beta, l = _sb_logs(z, past if diag else None)
            arg = log_beta + _suffix_sum(l, u, exact=False)
            a = jnp.exp(arg if c is None else arg + c)
            if diag:
                a = jnp.where(past, a, 0.0)
            return _dot(a.astype(BF16), v_ref[pl.ds(ks, tk), cols[g]]), jnp.sum(l, axis=1, keepdims=True)

        def straight(n):
            o_sum, c = [None] * pairs, [None] * pairs
            for b in range(n):
                for g in range(pairs):
                    o_b, c_b = block(qi - b, g, c[g], b == 0)
                    o_sum[g] = o_b if b == 0 else o_sum[g] + o_b
                    c[g] = c_b if b == 0 else c[g] + c_b
            return o_sum, c

        def finish(o_sum):
            for g in range(pairs):
                o_ref[:, cols[g]] = _unstack_heads(o_sum[g], tq)

        for n in range(1, n_pre):
            @pl.when(qi == n - 1)
            def _(n=n):
                finish(straight(n)[0])

        @pl.when(qi >= n_pre - 1)
        def _():
            o_sum, c = straight(n_pre)
            for g in range(pairs):
                acc[g] = o_sum[g]

            def cond(st):
                kb, c = st
                return jnp.logical_and(kb >= 0, _max_of(c) > EXP_ZERO_BELOW)

            def step(st):
                kb, c = st
                new_c = []
                for g in range(pairs):
                    o_n, c_n = block(kb, g, c[g], False)
                    acc[g] += o_n
                    new_c.append(c[g] + c_n)
                return kb - 1, tuple(new_c)

            lax.while_loop(cond, step, (qi - n_pre, tuple(c)))
            finish([acc[g] for g in range(pairs)])

    cols = [slice(g * LANES, (g + 1) * LANES) for g in range(pairs)]
    width = pairs * LANES
    carried = _Carried(comm, (batch, nhp // pairs, nq))
    (o,), comm_results = carried.pallas_call(
        body,
        in_specs=[pl.BlockSpec((tq, width), lambda b, p, i: (b * nq + i, p)),
                  pl.BlockSpec((seq, width), lambda b, p, i: (b, nhp // pairs + p)),
                  pl.BlockSpec((seq, width), lambda b, p, i: (b, 2 * (nhp // pairs) + p))],
        out_specs=[pl.BlockSpec((tq, width), lambda b, p, i: (b * nq + i, p))],
        out_shape=[jax.ShapeDtypeStruct((T, D), F32)],
        scratch_shapes=[pltpu.VMEM((pairs, 2 * tq, LANES), F32)],
        args=[qkv, qkv, qkv], name=name)
    return o, comm_results


def _sb_bwd(qkv, o, do, *, batch, seq, name, comm=None, tq=SB_TILE, n_pre=SB_STRAIGHT, pairs=SB_PAIRS):
    T, D3 = qkv.shape
    D = D3 // 3
    nhp = D // LANES
    tk = tq
    nq = seq // tq
    scale = SB_HEAD_DIM ** -0.5

    def body(*refs):
        ins, outs, scratch = carried.split(refs, 5, 3, 3)
        q_ref, k_ref, v_ref, o_ref, do_ref = ins
        dq_ref, dk_ref, dv_ref = outs
        dq_acc, dk_acc, dv_acc = scratch
        qi = pl.program_id(2)
        carried.emit(refs)

        @pl.when(qi == 0)
        def _():
            dk_acc[...] = jnp.zeros_like(dk_acc)
            dv_acc[...] = jnp.zeros_like(dv_acc)

        qs = [_stack_heads(q_ref[:, lanes]) * scale for lanes in cols]
        dos = [_stack_heads(do_ref[:, lanes]) for lanes in cols]
        dsum = [jnp.sum(_stack_heads(do_ref[:, lanes].astype(F32) * o_ref[:, lanes]), axis=1, keepdims=True)
                for lanes in cols]
        past = _past_mask(tq, tk)
        u = _suffix_matrix(tk)

        def block(kb, p, c, gc, diag):
            ks = pl.multiple_of(kb * tk, tk)
            kblk = k_ref[pl.ds(ks, tk), cols[p]]
            vblk = v_ref[pl.ds(ks, tk), cols[p]]
            z = _dot_nt(qs[p], kblk)
            log_beta, l = _sb_logs(z, past if diag else None)
            arg = log_beta + _suffix_sum(l, u, exact=False)
            a = jnp.exp(arg if c is None else arg + c)
            if diag:
                a = jnp.where(past, a, 0.0)
            a = a.astype(BF16)
            beta = 1.0 - jnp.exp(l)
            g = a.astype(F32) * _dot_nt(dos[p], vblk)
            gs = _suffix_sum(g, u)
            dz = g - beta * (dsum[p] - (gs if gc is None else gs + gc))
            if diag:
                dz = jnp.where(past, dz, 0.0)
            dzb = dz.astype(BF16)
            dk_acc[pl.ds(ks, tk), cols[p]] += _dot_tn(dzb, qs[p])
            dv_acc[pl.ds(ks, tk), cols[p]] += _dot_tn(a, dos[p])
            return (_dot(dzb, kblk), jnp.sum(l, axis=1, keepdims=True),
                    jnp.sum(g, axis=1, keepdims=True))

        def finish(dq_sum):
            for p in range(pairs):
                dq_ref[:, cols[p]] = (_unstack_heads(dq_sum[p], tq) * scale).astype(BF16)

        def straight(n):
            dq_sum, c, gc = [None] * pairs, [None] * pairs, [None] * pairs
            for b in range(n):
                for p in range(pairs):
                    dq_b, c_b, g_b = block(qi - b, p, c[p], gc[p], b == 0)
                    dq_sum[p] = dq_b if b == 0 else dq_sum[p] + dq_b
                    c[p] = c_b if b == 0 else c[p] + c_b
                    gc[p] = g_b if b == 0 else gc[p] + g_b
            return dq_sum, c, gc

        for n in range(1, n_pre):
            @pl.when(qi == n - 1)
            def _(n=n):
                finish(straight(n)[0])

        @pl.when(qi >= n_pre - 1)
        def _():
            dq_sum, c, gc = straight(n_pre)
            for p in range(pairs):
                dq_acc[p] = dq_sum[p]

            def cond(st):
                kb, c, gc = st
                return jnp.logical_and(kb >= 0, _max_of(c) > EXP_ZERO_BELOW)

            def step(st):
                kb, c, gc = st
                new_c, new_gc = [], []
                for p in range(pairs):
                    dq_n, c_n, g_n = block(kb, p, c[p], gc[p], False)
                    dq_acc[p] += dq_n
                    new_c.append(c[p] + c_n)
                    new_gc.append(gc[p] + g_n)
                return kb - 1, tuple(new_c), tuple(new_gc)

            lax.while_loop(cond, step, (qi - n_pre, tuple(c), tuple(gc)))
            finish([dq_acc[p] for p in range(pairs)])

        @pl.when(qi == nq - 1)
        def _():
            dk_ref[...] = dk_acc[...].astype(BF16)
            dv_ref[...] = dv_acc[...].astype(BF16)

    cols = [slice(p * LANES, (p + 1) * LANES) for p in range(pairs)]
    width, ncb = pairs * LANES, nhp // pairs
    qspec = pl.BlockSpec((tq, width), lambda b, p, i: (b * nq + i, p))
    sspec = pl.BlockSpec((seq, width), lambda b, p, i: (b, p))
    out = jax.ShapeDtypeStruct((T, D), BF16)
    carried = _Carried(comm, (batch, ncb, nq))
    return carried.pallas_call(
        body,
        in_specs=[qspec,
                  pl.BlockSpec((seq, width), lambda b, p, i: (b, ncb + p)),
                  pl.BlockSpec((seq, width), lambda b, p, i: (b, 2 * ncb + p)),
                  qspec, qspec],
        out_specs=[qspec, sspec, sspec], out_shape=[out, out, out],
        scratch_shapes=[pltpu.VMEM((pairs, 2 * tq, LANES), F32), pltpu.VMEM((seq, width), F32),
                        pltpu.VMEM((seq, width), F32)],
        args=[qkv, qkv, qkv, o, do], name=name)


_GELU_C = 0.7978845608028654
_GELU_A = 0.044715


def _gelu(x):
    x = x.astype(BF16)
    xx = x * x
    a1 = 1.0 + jnp.tanh(x * (_GELU_C + (_GELU_C * _GELU_A) * xx))
    hx = 0.5 * x
    grad = a1 * (0.5 + (hx * (2.0 - a1)) * (_GELU_C + (3.0 * _GELU_C * _GELU_A) * xx))
    return hx * a1, grad


def _causal_ws(ws_ref, g):
    t = lax.broadcasted_iota(jnp.int32, (SGU_CHUNK, SGU_CHUNK), 0)
    s = lax.broadcasted_iota(jnp.int32, (SGU_CHUNK, SGU_CHUNK), 1)
    return jnp.where(s <= t, ws_ref[g], 0.0)


def _sgu_gate(u, v, g_ref, ws_ref, bs_ref):
    G, W = SGU_GROUPS, SGU_GROUP_W
    r = lax.rsqrt(jnp.mean(v * v, axis=-1, keepdims=True) + EPS)
    vn = (v * r * g_ref[...]).astype(BF16)
    y = []
    for g in range(G):
        sl = slice(g * W, (g + 1) * W)
        mixed = _dot(_causal_ws(ws_ref, g).astype(BF16), vn[:, sl]) + bs_ref[:, g:g + 1]
        y.append((u[:, sl] * mixed).astype(BF16))
    return jnp.concatenate(y, axis=1)


def _sgu_bwd(uv, dgelu, dy, gain, ws, bst, *, name):
    T, F2 = uv.shape
    F = F2 // 2
    C, G, W = SGU_CHUNK, SGU_GROUPS, SGU_GROUP_W
    nc = T // C

    def body(uv_ref, dgelu_ref, dy_ref, g_ref, ws_ref, bs_ref, duv_ref, dg_ref, dws_ref, dbs_ref,
             dg_acc, dws_acc, dbs_acc):
        i = pl.program_id(0)

        @pl.when(i == 0)
        def _():
            dg_acc[...] = jnp.zeros_like(dg_acc)
            dws_acc[...] = jnp.zeros_like(dws_acc)
            dbs_acc[...] = jnp.zeros_like(dbs_acc)

        u, v = uv_ref[:, :F].astype(F32), uv_ref[:, F:].astype(F32)
        dgelu = dgelu_ref[...].astype(F32)
        r = lax.rsqrt(jnp.mean(v * v, axis=-1, keepdims=True) + EPS)
        vhat = v * r
        gain_v = g_ref[...]
        vn = (vhat * gain_v).astype(BF16)
        dyv = dy_ref[...].astype(F32)
        lane8 = lax.broadcasted_iota(jnp.int32, (1, G), 1)
        dvn_parts = []
        dbs_new = jnp.zeros((C, G), F32)
        for g in range(G):
            sl = slice(g * W, (g + 1) * W)
            wsg = _causal_ws(ws_ref, g)
            mixed = _dot(wsg.astype(BF16), vn[:, sl]) + bs_ref[:, g:g + 1]
            duv_ref[:, sl] = (dyv[:, sl] * mixed * dgelu[:, sl]).astype(BF16)
            dmix = dyv[:, sl] * u[:, sl]
            dbs_new = dbs_new + jnp.where(lane8 == g, jnp.sum(dmix, axis=1, keepdims=True), 0.0)
            dmix_b = dmix.astype(BF16)
            dws_acc[g] += _dot_nt(dmix_b, vn[:, sl])
            dvn_parts.append(_dot(wsg.T.astype(BF16), dmix_b))
        dbs_acc[...] += dbs_new
        dvn = jnp.concatenate(dvn_parts, axis=1)
        dg_acc[...] += jnp.sum((dvn * vhat).reshape(C // SUBLANES, SUBLANES, F), axis=0)
        dvhat = dvn * gain_v
        dv = r * (dvhat - vhat * jnp.mean(dvhat * vhat, axis=-1, keepdims=True))
        duv_ref[:, F:] = (dv * dgelu[:, F:]).astype(BF16)

        @pl.when(i == nc - 1)
        def _():
            dg_ref[...] = jnp.sum(dg_acc[...], axis=0, keepdims=True)
            t = lax.broadcasted_iota(jnp.int32, (G, C, C), 1)
            s = lax.broadcasted_iota(jnp.int32, (G, C, C), 2)
            dws_ref[...] = jnp.where(s <= t, dws_acc[...], 0.0)
            dbs_ref[...] = dbs_acc[...]

    return pl.pallas_call(
        body, grid=(nc,),
        in_specs=[pl.BlockSpec((C, F2), lambda i: (i, 0)), pl.BlockSpec((C, F2), lambda i: (i, 0)),
                  pl.BlockSpec((C, F), lambda i: (i, 0)),
                  pl.BlockSpec((1, F), lambda i: (0, 0)), pl.BlockSpec((G, C, C), lambda i: (0, 0, 0)),
                  pl.BlockSpec((C, G), lambda i: (0, 0))],
        out_specs=[pl.BlockSpec((C, F2), lambda i: (i, 0)), pl.BlockSpec((1, F), lambda i: (0, 0)),
                   pl.BlockSpec((G, C, C), lambda i: (0, 0, 0)), pl.BlockSpec((C, G), lambda i: (0, 0))],
        out_shape=[jax.ShapeDtypeStruct((T, F2), BF16), jax.ShapeDtypeStruct((1, F), F32),
                   jax.ShapeDtypeStruct((G, C, C), F32), jax.ShapeDtypeStruct((C, G), F32)],
        scratch_shapes=[pltpu.VMEM((SUBLANES, F), F32), pltpu.VMEM((G, C, C), F32), pltpu.VMEM((C, G), F32)],
        compiler_params=_params("arbitrary"), name=name)(uv, dgelu, dy, gain, ws, bst)


def _my_place():
    return lax.axis_index("x"), lax.axis_index("y"), lax.axis_index("c")


def _all_gather(shards, *, name):
    nf = len(shards)
    items = [(lambda ins, f=f: ins[f], lambda outs, p, f=f: outs[f].at[:, p]) for f in range(nf)]
    targets = [jax.ShapeDtypeStruct((s.shape[0], N_DEV) + s.shape[1:], s.dtype) for s in shards]
    return _run_comm(_Comm(shards, targets, len(items), _gather_emit(items, [list(range(nf))])), name=name)


class _Comm:
    def __init__(self, sources, targets, n_items, emit):
        self.sources, self.targets, self.n_items, self.emit = list(sources), list(targets), n_items, emit
        self.filled = [t for t in self.targets if not isinstance(t, jax.ShapeDtypeStruct)]
        self.operands = self.sources + self.filled
        self.out_shapes = [jax.ShapeDtypeStruct(t.shape, t.dtype) for t in self.targets]
        self.sems = [pltpu.SemaphoreType.DMA((n_items, 7)), pltpu.SemaphoreType.DMA((n_items, 7)),
                     pltpu.SemaphoreType.DMA((n_items,))]

    def aliases(self, first_operand, first_result):
        pos = {id(t): k for k, t in enumerate(self.targets)}
        return {first_operand + len(self.sources) + a: first_result + pos[id(t)]
                for a, t in enumerate(self.filled)}


def _run_comm(comm, *, name):
    n_op, n_out = len(comm.operands), len(comm.targets)

    def body(*refs):
        comm.emit(0, 1, refs[:len(comm.sources)], refs[n_op:n_op + n_out], refs[n_op + n_out:])

    return pl.pallas_call(
        body, in_specs=[ANY] * n_op, out_specs=[ANY] * n_out, out_shape=comm.out_shapes,
        scratch_shapes=comm.sems, input_output_aliases=comm.aliases(0, 0), name=name)(*comm.operands)


def _at_steps(step, n_steps, phases):
    if n_steps == 1:
        for _, fn in phases:
            fn()
        return
    marks = {}
    for frac, fn in phases:
        marks.setdefault(min(int(frac * n_steps), n_steps - 1), []).append(fn)
    for mark, fns in sorted(marks.items()):
        @pl.when(step == mark)
        def _(fns=fns):
            for fn in fns:
                fn()


def _gather_emit(items, groups, fractions=None):
    if fractions is None:
        fractions = [(g + 1) / len(groups) for g in range(len(groups))]
    def emit(step, n_steps, ins, outs, sems):
        send_sems, recv_sems, local_sems = sems
        x, y, c = _my_place()
        me, sibling = (x, y, c), (x, y, 1 - c)
        chips = [(1 - x, y), (x, 1 - y), (1 - x, 1 - y)]

        def copy(i, k, block, to, own=False):
            src_of, dst_of = items[i]
            dst = dst_of(outs, 4 * block[0] + 2 * block[1] + block[2])
            return pltpu.make_async_remote_copy(
                src_ref=src_of(ins) if own else dst, dst_ref=dst,
                send_sem=send_sems.at[i, k], recv_sem=recv_sems.at[i, k],
                device_id=to, device_id_type=MESH)

        def local(i):
            src_of, dst_of = items[i]
            return pltpu.make_async_copy(src_of(ins), dst_of(outs, 4 * x + 2 * y + c), local_sems.at[i])

        def first(i):
            return [copy(i, 0, me, sibling, own=True)] + [
                copy(i, 1 + j, me, (*chip, c), own=True) for j, chip in enumerate(chips)]

        def start():
            for i in range(len(items)):
                local(i).start()
                for cp in first(i):
                    cp.start()

        def forward(group):
            for j, chip in enumerate(chips):
                for i in group:
                    copy(i, 1 + j, (*chip, c), me).wait_recv()
                    copy(i, 4 + j, (*chip, c), sibling).start()

        def finish():
            for i in range(len(items)):
                copy(i, 0, sibling, me).wait_recv()
                for j, chip in enumerate(chips):
                    copy(i, 4 + j, (*chip, 1 - c), me).wait_recv()
            for i in range(len(items)):
                for cp in first(i) + [copy(i, 4 + j, (*chip, c), sibling) for j, chip in enumerate(chips)]:
                    cp.wait_send()
                local(i).wait()

        phases = [(0.0, start)]
        for frac, group in zip(fractions, groups):
            phases.append((frac, functools.partial(forward, group)))
        phases.append((1.0, finish))
        _at_steps(step, n_steps, phases)

    return emit


def _exchange_emit(items):
    def emit(step, n_steps, ins, outs, sems):
        send_sems, recv_sems, local_sems = sems
        x, y, c = _my_place()
        me = 4 * x + 2 * y + c

        def peer_of(k):
            return x ^ (k >> 2), y ^ ((k >> 1) & 1), c ^ (k & 1)

        def copy(i, k):
            src_of, dst_of = items[i]
            px, py, pc = peer_of(k)
            return pltpu.make_async_remote_copy(
                src_ref=src_of(ins, 4 * px + 2 * py + pc), dst_ref=dst_of(outs, me),
                send_sem=send_sems.at[i, k - 1], recv_sem=recv_sems.at[i, k - 1],
                device_id=(px, py, pc), device_id_type=MESH)

        def arrival(i, k):
            src_of, dst_of = items[i]
            px, py, pc = peer_of(k)
            peer = 4 * px + 2 * py + pc
            return pltpu.make_async_remote_copy(
                src_ref=src_of(ins, peer), dst_ref=dst_of(outs, peer),
                send_sem=send_sems.at[i, k - 1], recv_sem=recv_sems.at[i, k - 1],
                device_id=(x, y, c), device_id_type=MESH)

        def local(i):
            src_of, dst_of = items[i]
            return pltpu.make_async_copy(src_of(ins, me), dst_of(outs, me), local_sems.at[i])

        def start():
            for i in range(len(items)):
                local(i).start()
            for k in range(1, N_DEV):
                for i in range(len(items)):
                    copy(i, k).start()

        def finish():
            for k in range(1, N_DEV):
                for i in range(len(items)):
                    arrival(i, k).wait_recv()
            for k in range(1, N_DEV):
                for i in range(len(items)):
                    copy(i, k).wait_send()
            for i in range(len(items)):
                local(i).wait()

        _at_steps(step, n_steps, [(0.0, start), (1.0, finish)])

    return emit


def _adam_math(g, w, m, v):
    m = ADAM_B1 * m + (1.0 - ADAM_B1) * g
    v = ADAM_B2 * v + (1.0 - ADAM_B2) * (g * g)
    m_hat = m / (1.0 - ADAM_B1 ** ADAM_STEP)
    v_hat = v / (1.0 - ADAM_B2 ** ADAM_STEP)
    delta = -ADAM_LR * (m_hat / (jnp.sqrt(v_hat) + ADAM_EPS) + ADAM_WD * w)
    return delta, m, v


def _sum_adamw(parts, w, m, v, *, name, tr=256):
    L, nd, R, C = parts.shape
    tr = min(tr, R)
    assert R % tr == 0

    def body(p_ref, w_ref, m_ref, v_ref, g_ref, d_ref, nm_ref, nv_ref):
        g = p_ref[0, 0].astype(F32)
        for q in range(1, nd):
            g = g + p_ref[0, q].astype(F32)
        d, nm, nv = _adam_math(g, w_ref[0], m_ref[0], v_ref[0])
        g_ref[0] = g
        d_ref[0] = d
        nm_ref[0] = nm
        nv_ref[0] = nv

    blk = pl.BlockSpec((1, tr, C), lambda l, i: (l, i, 0))
    out = jax.ShapeDtypeStruct((L, R, C), F32)
    return pl.pallas_call(
        body, grid=(L, R // tr),
        in_specs=[pl.BlockSpec((1, nd, tr, C), lambda l, i: (l, 0, i, 0)), blk, blk, blk],
        out_specs=[blk] * 4, out_shape=[out] * 4,
        compiler_params=_params("parallel", "parallel"), name=name)(parts, w, m, v)


def _sum_parts(parts, *, name):
    nd, R, C = parts.shape

    def body(p_ref, o_ref):
        g = p_ref[0]
        for q in range(1, nd):
            g = g + p_ref[q]
        o_ref[...] = g

    return pl.pallas_call(
        body, out_shape=jax.ShapeDtypeStruct((R, C), F32),
        in_specs=[pl.BlockSpec(memory_space=pltpu.VMEM)],
        out_specs=pl.BlockSpec(memory_space=pltpu.VMEM), name=name)(parts)


def _adamw_small(g, w, m, v, *, name):
    def body(g_ref, w_ref, m_ref, v_ref, d_ref, nm_ref, nv_ref):
        d, nm, nv = _adam_math(g_ref[...], w_ref[...], m_ref[...], v_ref[...])
        d_ref[...] = d
        nm_ref[...] = nm
        nv_ref[...] = nv

    vm = pl.BlockSpec(memory_space=pltpu.VMEM)
    out = jax.ShapeDtypeStruct(g.shape, F32)
    return pl.pallas_call(body, out_shape=[out] * 3, in_specs=[vm] * 4, out_specs=[vm] * 3,
                          name=name)(g, w, m, v)


def kernel(x, norm_mix, norm_mlp, sb_wqkv, sb_wo, sgu_win, sgu_gain, sgu_ws, sgu_bs, sgu_wout, mlp_w1, mlp_w2, final_norm, loss_target, m_norm_mix, m_norm_mlp, m_sb_wqkv, m_sb_wo, m_sgu_win, m_sgu_gain, m_sgu_ws, m_sgu_bs, m_sgu_wout, m_mlp_w1, m_mlp_w2, m_final_norm, v_norm_mix, v_norm_mlp, v_sb_wqkv, v_sb_wo, v_sgu_win, v_sgu_gain, v_sgu_ws, v_sgu_bs, v_sgu_wout, v_mlp_w1, v_mlp_w2, v_final_norm):
    batch, seq, D = x.shape
    T = batch * seq
    x0 = x.reshape(T, D)
    target = loss_target.reshape(T, D)

    WQKV, WO, WIN, WOUT, W1, W2, GAIN = range(7)
    big = [sb_wqkv, sb_wo, sgu_win, sgu_wout, mlp_w1, mlp_w2]
    shards = [w.astype(BF16) for w in big] + [sgu_gain[:, None, :]]
    rides = {
        "qkv0": [[(WO, 0), (W1, 0)]],
        "sb_fwd0": [[(W2, 0)], [(WIN, 0), (WOUT, 0), (W1, 1), (W2, 1)], [(WQKV, 1)]],
        "qkv2": [[(WO, 1), (W1, 2)]],
        "sb_fwd2": [[(W2, 2)], [(WIN, 1), (WOUT, 1), (W1, 3), (W2, 3)]],
    }

    first = [(lambda ins: ins[WQKV].at[0], lambda outs, p: outs[WQKV].at[0, p]),
             (lambda ins: ins[GAIN], lambda outs, p: outs[GAIN].at[:, p])]
    targets = [jax.ShapeDtypeStruct((s.shape[0], N_DEV) + s.shape[1:], s.dtype) for s in shards]
    gathered0 = _run_comm(_Comm(shards, targets, len(first), _gather_emit(first, [[0, 1]])), name="gather_first")
    gain_sgu = gathered0[GAIN].reshape(-1, 1, SGU_FFN)
    gw = dict(enumerate(gathered0[:GAIN]))

    def gather_ride(call):
        wanted = [fl for group in rides[call] for fl in group]
        fams = sorted({f for f, _ in wanted})
        items = [(lambda ins, f=f, l=l: ins[f].at[l], lambda outs, p, t=fams.index(f), l=l: outs[t].at[l, p])
                 for f, l in wanted]
        groups, sent, k = [], [], 0
        for group in rides[call]:
            groups.append(list(range(k, k + len(group))))
            sent.append(sum(shards[f].shape[1] * shards[f].shape[2] for f, _ in group) + (sent[-1] if sent else 0))
            k += len(group)
        fractions = [s / sent[-1] for s in sent]
        return _Comm(shards[:GAIN], [gw[f] for f in fams], len(items), _gather_emit(items, groups, fractions)), fams

    def weight(f):
        g = gw[f]
        return g if f in (WQKV, WIN, W1) else g.reshape(g.shape[0], 1, N_DEV * g.shape[2], g.shape[3])

    saved = []
    xs = x0
    for i in range(DEPTH):
        j = i // 2
        if i % 2 == 0:
            comm, fams = gather_ride(f"qkv{i}")
            (qkv, h), results = _mm_nn(xs, weight(WQKV), j, out_dtype=BF16, gain=norm_mix[i:i + 1],
                                       name=f"qkv{i}", comm=comm, tm=1024)
            gw.update(zip(fams, results))
            comm, fams = gather_ride(f"sb_fwd{i}")
            o, results = _sb_fwd(qkv, batch=batch, seq=seq, name=f"sb_fwd{i}", comm=comm)
            gw.update(zip(fams, results))
            x_mid = _mm_nn(o, weight(WO), j, out_dtype=F32, res=xs, name=f"wo{i}", tm=1024)
            mix = (h, qkv, o)
        else:
            gain_j = gain_sgu[j]
            bst = sgu_bs[j].T
            uv, h, dgelu, yv = _mm_nn(xs, weight(WIN), j, out_dtype=BF16, gain=norm_mix[i:i + 1],
                                      sgu=(gain_j, sgu_ws[j], bst), name=f"win_sgu{i}")
            x_mid = _mm_nn(yv, weight(WOUT), j, out_dtype=F32, res=xs, name=f"wout{i}", tm=1024)
            mix = (h, uv, dgelu, yv, gain_j, bst)
        a, h2 = _mm_nn(x_mid, weight(W1), i, out_dtype=BF16, gain=norm_mlp[i:i + 1], name=f"w1_{i}", tm=1024)
        saved.append((xs, mix, x_mid, h2, a))
        if i < DEPTH - 1:
            xs = _mm_nn(a, weight(W2), i, out_dtype=F32, res=x_mid, a_act="relu2", name=f"w2_{i}")
    g_wqkv, g_wo, g_win, g_wout, g_w1, g_w2 = [weight(f) for f in range(GAIN)]

    dx, dxb, sq, d_final = _mlp_out_loss_head(a, g_w2, DEPTH - 1, x_mid, final_norm.reshape(1, D), target,
                                              name="w2_loss_head")
    loss = lax.psum(0.5 * jnp.sum(sq) / D, ("x", "y", "c"))

    SMALL = GAIN
    stacks = {f: jax.ShapeDtypeStruct((w.shape[0], N_DEV) + w.shape[1:], BF16) for f, w in enumerate(big)}
    pending = []

    def row_shards(p):
        return p.reshape(N_DEV, p.shape[1] // N_DEV, p.shape[2])

    def exchange_pending(newest_only=False):
        going = [pending.pop()] if newest_only else [pending.pop(0) for _ in range(len(pending))]
        fams = sorted({f for _, f, _ in going})
        items = [(lambda ins, p, a=a: ins[a].at[p], lambda outs, q, t=fams.index(f), l=l: outs[t].at[l, q])
                 for a, (_, f, l) in enumerate(going)]
        return _Comm([part for part, _, _ in going], [stacks[f] for f in fams], len(items), _exchange_emit(items)), fams

    d_norm_mix, d_norm_mlp = [None] * DEPTH, [None] * DEPTH
    d_gain, d_ws, d_bs = [None] * 2, [None] * 2, [None] * 2
    for i in reversed(range(DEPTH)):
        j = i // 2
        xs, mix, x_mid, h2, a = saved[i]
        da = _mm_nt(dxb, g_w2, i, out_dtype=BF16, act_src=a, name=f"d_a{i}")
        pending.append((row_shards(_mm_tn(a, dxb, shards=1, a_act="relu2", name=f"d_w2_{i}")), W2, i))
        pending.append((_mm_tn(h2, da, shards=N_DEV, name=f"d_w1_{i}", pb=4, tm=1024), W1, i))
        (dx, dxb, d_norm_mlp[i]), _ = _mm_nt_rms_bwd(da, g_w1, i, x_mid, norm_mlp[i:i + 1], dx, name=f"d_h2_{i}")
        if i % 2 == 0:
            h, qkv, o = mix
            do = _mm_nt(dxb, g_wo, j, out_dtype=BF16, name=f"d_o{i}", tm=1024)
            pending.append((row_shards(_mm_tn(o, dxb, shards=1, name=f"d_wo{i}")), WO, j))
            comm, fams = exchange_pending()
            (dq, dk, dv), results = _sb_bwd(qkv, o, do, batch=batch, seq=seq, name=f"sb_bwd{i}", comm=comm)
            stacks.update(zip(fams, results))
            dqkv = [dq, dk, dv]
            pending.append((_mm_tn(h, dqkv, shards=N_DEV, name=f"d_wqkv{i}", pb=N_DEV, tm=1024, tk=512), WQKV, j))
            mixer_in, w_in = dqkv, g_wqkv
        else:
            h, uv, dgelu, yv, gain_j, bst = mix
            dy = _mm_nt(dxb, g_wout, j, out_dtype=BF16, name=f"d_y{i}", tm=1024)
            pending.append((row_shards(_mm_tn(yv, dxb, shards=1, name=f"d_wout{i}")), WOUT, j))
            duv, d_gain[j], d_ws[j], dbst = _sgu_bwd(uv, dgelu, dy, gain_j, sgu_ws[j], bst, name=f"sgu_bwd{i}")
            d_bs[j] = dbst.T
            pending.append((_mm_tn(h, duv, shards=N_DEV, name=f"d_win{i}", pb=4, tm=1024), WIN, j))
            mixer_in, w_in = duv, g_win
        comm, fams = exchange_pending(newest_only=True)
        (dx, dxb, d_norm_mix[i]), results = _mm_nt_rms_bwd(
            mixer_in, w_in, j, xs, norm_mix[i:i + 1], dx, name=f"d_h_mix{i}", comm=comm)
        stacks.update(zip(fams, results))
    grad_x = dx.reshape(batch, seq, D)

    small = [jnp.concatenate(d_norm_mix, 0), jnp.concatenate(d_norm_mlp, 0), d_final,
             jnp.concatenate(d_gain, 0), jnp.stack(d_bs, 0), jnp.stack(d_ws, 0)]
    small_flat = jnp.concatenate([s.reshape(-1) for s in small])
    n_small = small_flat.shape[0]
    small_rows = -(-n_small // (N_DEV * SUBLANES * LANES)) * SUBLANES
    small_flat = jnp.pad(small_flat, (0, N_DEV * small_rows * LANES - n_small))
    stacks[SMALL] = jax.ShapeDtypeStruct((1, N_DEV, small_rows, LANES), F32)
    pending.append((small_flat.reshape(N_DEV, small_rows, LANES), SMALL, 0))
    comm, fams = exchange_pending()
    stacks.update(zip(fams, _run_comm(comm, name="exchange_last")))
    r_wqkv, r_wo, r_win, r_wout, r_w1, r_w2, r_small = [stacks[f] for f in range(SMALL + 1)]

    u_wqkv = _sum_adamw(r_wqkv, sb_wqkv, m_sb_wqkv, v_sb_wqkv, name="adamw_wqkv")
    u_wo = _sum_adamw(r_wo, sb_wo, m_sb_wo, v_sb_wo, name="adamw_wo")
    u_win = _sum_adamw(r_win, sgu_win, m_sgu_win, v_sgu_win, name="adamw_win")
    u_wout = _sum_adamw(r_wout, sgu_wout, m_sgu_wout, v_sgu_wout, name="adamw_wout")
    u_w1 = _sum_adamw(r_w1, mlp_w1, m_mlp_w1, v_mlp_w1, name="adamw_w1")
    u_w2 = _sum_adamw(r_w2, mlp_w2, m_mlp_w2, v_mlp_w2, name="adamw_w2")

    small_sum = _sum_parts(r_small[0], name="sum_small")
    g_small = _all_gather([small_sum[None]], name="gather_small")[0].reshape(-1)[:n_small]

    shapes = [s.shape for s in small]
    sizes = [s.size for s in small]
    offs = [sum(sizes[:k]) for k in range(len(sizes))]
    me = 4 * lax.axis_index("x") + 2 * lax.axis_index("y") + lax.axis_index("c")
    shard_w = SGU_FFN // N_DEV

    def pack(arrs):
        flat = jnp.concatenate([a_.reshape(-1) for a_ in arrs])
        return jnp.pad(flat, (0, N_DEV * small_rows * LANES - n_small)).reshape(-1, LANES)

    def full_gain(gshard):
        return lax.dynamic_update_slice(jnp.zeros((2, SGU_FFN), F32), gshard, (0, me * shard_w))

    w_small = pack([norm_mix, norm_mlp, final_norm, full_gain(sgu_gain), sgu_bs, sgu_ws])
    m_small = pack([m_norm_mix, m_norm_mlp, m_final_norm, full_gain(m_sgu_gain), m_sgu_bs, m_sgu_ws])
    v_small = pack([v_norm_mix, v_norm_mlp, v_final_norm, full_gain(v_sgu_gain), v_sgu_bs, v_sgu_ws])
    g_pack = jnp.pad(g_small, (0, N_DEV * small_rows * LANES - n_small)).reshape(-1, LANES)
    sm = [g_pack] + list(_adamw_small(g_pack, w_small, m_small, v_small, name="adamw_small"))

    def unpack(flat2d):
        flat = flat2d.reshape(-1)
        out = [flat[offs[k]:offs[k] + sizes[k]].reshape(shapes[k]) for k in range(len(sizes))]
        out[2] = out[2].reshape(D)
        out[3] = lax.dynamic_slice(out[3], (0, me * shard_w), (2, shard_w))
        return out

    outs = []
    for k, big_u in enumerate(zip(u_wqkv, u_wo, u_win, u_wout, u_w1, u_w2)):
        s_nm, s_nl, s_fn, s_gain, s_bs, s_ws = unpack(sm[k])
        b_wqkv, b_wo, b_win, b_wout, b_w1, b_w2 = big_u
        outs += [s_nm, s_nl, b_wqkv, b_wo, b_win, s_gain, s_ws, s_bs, b_wout, b_w1, b_w2, s_fn]
    return (loss, grad_x, *outs)
```

```python
import functools

import jax
import jax.numpy as jnp
from jax import lax
from jax.experimental import pallas as pl
from jax.experimental.pallas import tpu as pltpu

F32 = jnp.float32
BF16 = jnp.bfloat16

N_DEV = 8
D_MODEL = 1024
SEQ = 2048
DEPTH = 4
SB_HEAD_DIM = 64
SGU_CHUNK = 128
SGU_FFN = 2 * D_MODEL
SGU_GROUPS = 8
SGU_GROUP_W = SGU_FFN // SGU_GROUPS
EPS = 1e-6

ADAM_LR = 0.001
ADAM_B1 = 0.9
ADAM_B2 = 0.999
ADAM_EPS = 1e-08
ADAM_WD = 0.01
ADAM_STEP = 10

MXU_TILE = 256
LANES = 128
SUBLANES = 8
VMEM_LIMIT = 56 * 1024 * 1024
EXP_ZERO_BELOW = -104.0
SB_TILE = 256
SB_STRAIGHT = 2
SB_PAIRS = 4

MESH = pl.DeviceIdType.MESH
ANY = pl.BlockSpec(memory_space=pl.ANY)


def _params(*sem):
    return pltpu.CompilerParams(dimension_semantics=sem, vmem_limit_bytes=VMEM_LIMIT)


def _dot(a, b):
    return jnp.dot(a, b, preferred_element_type=F32)


def _dot_nt(a, b):
    return lax.dot_general(a, b, (((1,), (1,)), ((), ())), preferred_element_type=F32)


def _dot_tn(a, b):
    return lax.dot_general(a, b, (((0,), (0,)), ((), ())), preferred_element_type=F32)


def _split_bf16(x):
    hi = x.astype(BF16)
    lo = (x - hi.astype(F32)).astype(BF16)
    return hi, lo


def _relu2(av):
    t = jnp.maximum(av, jnp.zeros_like(av))
    return t * t


def _mm_nn(a, w, l, *, out_dtype, name, res=None, a_act=None, gain=None, sgu=None, comm=None,
           tm=512, kc=1024):
    M, K = a.shape
    _, P, K2, n = w.shape
    assert K2 == K
    tm, kc = min(tm, M), min(kc, K)
    assert M % tm == 0 and K % kc == 0
    join = 2 if n % MXU_TILE and P % 2 == 0 else 1

    n_main = 2 + (gain is not None) + (res is not None)
    n_in = n_main + (3 if sgu is not None else 0)
    n_out = 1 + (gain is not None) + (2 if sgu is not None else 0)
    gelu = sgu is not None

    def body(*all_refs):
        ins, outs, _ = carried.split(all_refs, n_in, n_out, 0)
        carried.emit(all_refs)
        refs = list(ins) + list(outs)
        a_ref, w_ref, o_ref = refs[0], refs[1], refs[n_in]
        if gain is not None:
            xv = a_ref[...]
            r = lax.rsqrt(jnp.mean(xv * xv, axis=-1, keepdims=True) + EPS)
            h = (xv * r * refs[2][...]).astype(BF16)
            refs[n_in + 1][...] = h
        for p in range(0, P, join):
            sl = slice(p * n, (p + join) * n)
            acc = None
            for k0 in range(0, K, kc):
                if gain is not None:
                    av = h[:, k0:k0 + kc]
                else:
                    av = a_ref[:, k0:k0 + kc]
                    av = _relu2(av) if a_act == "relu2" else av.astype(BF16)
                wv = [w_ref[0, p + s, k0:k0 + kc, :] for s in range(join)]
                d = _dot(av, wv[0] if join == 1 else jnp.concatenate(wv, axis=1))
                acc = d if acc is None else acc + d
            if res is not None:
                acc = acc + refs[n_main - 1][:, sl]
            if gelu:
                acc, dact = _gelu(acc)
                refs[-2][:, sl] = dact.astype(BF16)
            o_ref[:, sl] = acc.astype(out_dtype)
        if sgu is not None:
            half = P * n // 2
            for r0 in range(0, tm, SGU_CHUNK):
                rows = slice(r0, r0 + SGU_CHUNK)
                refs[-1][rows, :] = _sgu_gate(o_ref[rows, :half].astype(F32), o_ref[rows, half:].astype(F32),
                                              *refs[n_main:n_in])

    row = lambda width: pl.BlockSpec((tm, width), lambda i: (i, 0))
    whole = lambda arr: pl.BlockSpec(arr.shape, lambda i: (0,) * arr.ndim)
    in_specs = [row(K), pl.BlockSpec((1, P, K, n), lambda i: (l, 0, 0, 0), pipeline_mode=pl.Buffered(1))]
    args = [a, w]
    if gain is not None:
        in_specs.append(whole(gain))
        args.append(gain)
    if res is not None:
        in_specs.append(row(P * n))
        args.append(res)
    out_specs, out_shape = [row(P * n)], [jax.ShapeDtypeStruct((M, P * n), out_dtype)]
    if gain is not None:
        out_specs.append(row(K))
        out_shape.append(jax.ShapeDtypeStruct((M, K), BF16))
    if sgu is not None:
        in_specs += [whole(s) for s in sgu]
        args += list(sgu)
        out_specs += [row(P * n), row(P * n // 2)]
        out_shape += [jax.ShapeDtypeStruct((M, P * n), BF16), jax.ShapeDtypeStruct((M, P * n // 2), BF16)]
    carried = _Carried(comm, (M // tm,), last="parallel")
    outs, comm_results = carried.pallas_call(
        body, in_specs=in_specs, out_specs=out_specs, out_shape=out_shape, scratch_shapes=[],
        args=args, name=name)
    outs = tuple(outs) if n_out > 1 else outs[0]
    return outs if comm is None else (outs, comm_results)


def _mm_nt(a, w, l, *, out_dtype, name, act_src=None, tm=512, tn=1024):
    M, K = a.shape
    _, P, Nout, kc = w.shape
    assert P == 1 and K == kc
    tm, tn = min(tm, M), min(tn, Nout)
    assert M % tm == 0 and Nout % tn == 0

    def body(*refs):
        a_ref, w_ref, o_ref = refs[0], refs[1], refs[-1]
        av = a_ref[...].astype(BF16)
        for n0 in range(0, Nout, tn):
            r = _dot_nt(av, w_ref[0, 0, n0:n0 + tn, :])
            if act_src is not None:
                r = r * (2.0 * jnp.maximum(refs[2][:, n0:n0 + tn].astype(F32), 0.0))
            o_ref[:, n0:n0 + tn] = r.astype(out_dtype)

    row = lambda width: pl.BlockSpec((tm, width), lambda i: (i, 0))
    in_specs = [row(K), pl.BlockSpec((1, 1, Nout, K), lambda i: (l, 0, 0, 0), pipeline_mode=pl.Buffered(1))]
    args = [a, w]
    if act_src is not None:
        in_specs.append(row(Nout))
        args.append(act_src)
    return pl.pallas_call(
        body, grid=(M // tm,), in_specs=in_specs, out_specs=row(Nout),
        out_shape=jax.ShapeDtypeStruct((M, Nout), out_dtype),
        compiler_params=_params("parallel"), name=name)(*args)


def _mm_tn(a, b, *, shards, name, a_act=None, tm=2048, tk=1024, pb=1):
    b_parts = list(b) if isinstance(b, (list, tuple)) else [b]
    M, K = a.shape
    N = sum(part.shape[1] for part in b_parts)
    assert all(part.shape[0] == M for part in b_parts)
    n = N // shards
    tm, tk = min(tm, M), min(tk, K)
    assert M % tm == 0 and K % tk == 0 and shards % pb == 0
    width, nm = pb * n, M // tm
    assert len(b_parts) == 1 or width == N

    def body(*refs):
        a_ref, b_refs, o_ref, acc = refs[0], refs[1:-2], refs[-2], refs[-1]
        m = pl.program_id(2)
        av = a_ref[...]
        if a_act == "relu2":
            av = _relu2(av)
        bv = [r[...].astype(BF16) for r in b_refs]

        @pl.when(m == 0)
        def _():
            acc[...] = jnp.zeros_like(acc)

        acc[...] += _dot_tn(av.astype(BF16), bv[0] if len(bv) == 1 else jnp.concatenate(bv, axis=1))

        @pl.when(m == nm - 1)
        def _():
            for p in range(pb):
                o_ref[p] = acc[:, p * n:(p + 1) * n].astype(BF16)

    if len(b_parts) == 1:
        b_specs = [pl.BlockSpec((tm, width), lambda i, j, m: (m, j))]
    else:
        b_specs = [pl.BlockSpec((tm, part.shape[1]), lambda i, j, m: (m, 0)) for part in b_parts]
    return pl.pallas_call(
        body, grid=(K // tk, N // width, nm),
        in_specs=[pl.BlockSpec((tm, tk), lambda i, j, m: (m, i))] + b_specs,
        out_specs=pl.BlockSpec((pb, tk, n), lambda i, j, m: (j, i, 0)),
        out_shape=jax.ShapeDtypeStruct((shards, K, n), BF16),
        scratch_shapes=[pltpu.VMEM((tk, width), F32)],
        compiler_params=_params("parallel", "parallel", "arbitrary"), name=name)(a, *b_parts)


def _mm_nt_rms_bwd(a, w, l, x, gain, dres, *, name, comm=None, tm=512):
    a_parts = list(a) if isinstance(a, (list, tuple)) else [a]
    na = len(a_parts)
    M, K = a_parts[0].shape[0], sum(part.shape[1] for part in a_parts)
    _, P, D, kc = w.shape
    assert K == P * kc and x.shape == (M, D)
    tm = min(tm, M)
    nr = M // tm
    join = 2 if kc % MXU_TILE and P % 2 == 0 else 1

    def body(*refs):
        ins, (dx_ref, dxb_ref, dg_ref), (acc,) = carried.split(refs, na + 4, 3, 1)
        a_refs, (w_ref, x_ref, g_ref, dres_ref) = ins[:na], ins[na:]
        i = pl.program_id(0)
        carried.emit(refs)
        av = a_refs[0][...] if na == 1 else jnp.concatenate([r[...] for r in a_refs], axis=1)
        dhv = None
        for p in range(0, P, join):
            wv = [w_ref[0, p + s] for s in range(join)]
            d = _dot_nt(av[:, p * kc:(p + join) * kc], wv[0] if join == 1 else jnp.concatenate(wv, axis=1))
            dhv = d if dhv is None else dhv + d
        xv = x_ref[...]
        r = lax.rsqrt(jnp.mean(xv * xv, axis=-1, keepdims=True) + EPS)
        xhat = xv * r
        dxhat = dhv * g_ref[...]
        dx = dres_ref[...] + r * (dxhat - xhat * jnp.mean(dxhat * xhat, axis=-1, keepdims=True))
        dx_ref[...] = dx
        dxb_ref[...] = dx.astype(BF16)
        part = jnp.sum((dhv * xhat).reshape(tm // SUBLANES, SUBLANES, D), axis=0)

        @pl.when(i == 0)
        def _():
            acc[...] = jnp.zeros_like(acc)

        acc[...] += part

        @pl.when(i == nr - 1)
        def _():
            dg_ref[...] = jnp.sum(acc[...], axis=0, keepdims=True)

    row = pl.BlockSpec((tm, D), lambda i: (i, 0))
    vec = pl.BlockSpec((1, D), lambda i: (0, 0))
    carried = _Carried(comm, (nr,))
    return carried.pallas_call(
        body,
        in_specs=[pl.BlockSpec((tm, part.shape[1]), lambda i: (i, 0)) for part in a_parts] + [
            pl.BlockSpec((1, P, D, kc), lambda i: (l, 0, 0, 0), pipeline_mode=pl.Buffered(1)),
            row, vec, row],
        out_specs=[row, row, vec],
        out_shape=[jax.ShapeDtypeStruct((M, D), F32), jax.ShapeDtypeStruct((M, D), BF16),
                   jax.ShapeDtypeStruct((1, D), F32)],
        scratch_shapes=[pltpu.VMEM((SUBLANES, D), F32)],
        args=a_parts + [w, x, gain, dres], name=name)


def _mlp_out_loss_head(a, w, l, res, gain, target, *, name, tr=512, kc=1024):
    T, K = a.shape
    D = w.shape[3]
    nr = T // tr

    def body(a_ref, w_ref, res_ref, g_ref, t_ref, dx_ref, dxb_ref, sq_ref, dg_ref, sq_acc, dg_acc):
        i = pl.program_id(0)
        xv = res_ref[...]
        for k0 in range(0, K, kc):
            xv = xv + _dot(_relu2(a_ref[:, k0:k0 + kc]), w_ref[0, 0, k0:k0 + kc, :])
        g = g_ref[...]
        r = lax.rsqrt(jnp.mean(xv * xv, axis=-1, keepdims=True) + EPS)
        xhat = xv * r
        err = xhat * g - t_ref[...]
        dy = err * (1.0 / D)
        dxhat = dy * g
        dx = r * (dxhat - xhat * jnp.mean(dxhat * xhat, axis=-1, keepdims=True))
        dx_ref[...] = dx
        dxb_ref[...] = dx.astype(BF16)
        sq = jnp.sum((err * err).reshape(tr // SUBLANES, SUBLANES, D), axis=0)
        dg = jnp.sum((dy * xhat).reshape(tr // SUBLANES, SUBLANES, D), axis=0)

        @pl.when(i == 0)
        def _():
            sq_acc[...] = sq
            dg_acc[...] = dg

        @pl.when(i > 0)
        def _():
            sq_acc[...] += sq
            dg_acc[...] += dg

        @pl.when(i == nr - 1)
        def _():
            sq_ref[...] = sq_acc[...]
            dg_ref[...] = jnp.sum(dg_acc[...], axis=0, keepdims=True)

    row = pl.BlockSpec((tr, D), lambda i: (i, 0))
    vec = pl.BlockSpec((1, D), lambda i: (0, 0))
    part = pl.BlockSpec((SUBLANES, D), lambda i: (0, 0))
    return pl.pallas_call(
        body, grid=(nr,),
        in_specs=[pl.BlockSpec((tr, K), lambda i: (i, 0)),
                  pl.BlockSpec((1, 1, K, D), lambda i: (l, 0, 0, 0), pipeline_mode=pl.Buffered(1)),
                  row, vec, row],
        out_specs=[row, row, part, vec],
        out_shape=[jax.ShapeDtypeStruct((T, D), F32), jax.ShapeDtypeStruct((T, D), BF16),
                   jax.ShapeDtypeStruct((SUBLANES, D), F32), jax.ShapeDtypeStruct((1, D), F32)],
        scratch_shapes=[pltpu.VMEM((SUBLANES, D), F32), pltpu.VMEM((SUBLANES, D), F32)],
        compiler_params=_params("arbitrary"), name=name)(a, w, res, gain, target)


def _head0_lanes():
    return lax.broadcasted_iota(jnp.int32, (1, LANES), 1) < SB_HEAD_DIM


def _stack_heads(x):
    zero = jnp.zeros_like(x)
    h0 = _head0_lanes()
    return jnp.concatenate([jnp.where(h0, x, zero), jnp.where(h0, zero, x)], axis=0)


def _unstack_heads(y, tq):
    return jnp.where(_head0_lanes(), y[:tq], y[tq:])


def _past_mask(tq, tk):
    row = lax.broadcasted_iota(jnp.int32, (2 * tq, tk), 0) & (tq - 1)
    col = lax.broadcasted_iota(jnp.int32, (2 * tq, tk), 1)
    return col < row


def _max_of(arrays):
    return functools.reduce(jnp.maximum, [jnp.max(a) for a in arrays])


def _sb_logs(z, past):
    minus_abs = lax.bitcast_convert_type(
        lax.bitcast_convert_type(z, jnp.uint32) | jnp.uint32(0x80000000), F32)
    log_beta = jnp.minimum(z, 0.0) - jnp.log(1.0 + jnp.exp(minus_abs))
    l = log_beta - z
    if past is not None:
        l = jnp.where(past, l, 0.0)
    return log_beta, l


def _suffix_matrix(tk):
    j = lax.broadcasted_iota(jnp.int32, (2 * tk, tk), 0) & (tk - 1)
    s = lax.broadcasted_iota(jnp.int32, (2 * tk, tk), 1)
    return (j > s).astype(BF16)


def _suffix_sum(x, u2, exact=True):
    if not exact:
        return _dot(x.astype(BF16), u2[:x.shape[1]])
    hi, lo = _split_bf16(x)
    return _dot(jnp.concatenate([hi, lo], axis=1), u2)


class _Carried:
    def __init__(self, comm, grid, last="arbitrary"):
        self.comm, self.grid, self.last = comm, grid, last
        self.n_op = len(comm.operands) if comm else 0
        self.n_tgt = len(comm.targets) if comm else 0

    def split(self, refs, n_in, n_out, n_scratch):
        self.n_in, self.n_out, self.n_scratch = n_in, n_out, n_scratch
        a = n_in + self.n_op
        b = a + n_out + self.n_tgt
        return refs[:n_in], refs[a:a + n_out], refs[b:b + n_scratch]

    def emit(self, refs):
        if self.comm is None:
            return
        step, n_steps = 0, 1
        for d, size in enumerate(self.grid):
            step = step * size + pl.program_id(d)
            n_steps *= size
        a = self.n_in + self.n_op + self.n_out
        self.comm.emit(step, n_steps, refs[self.n_in:self.n_in + len(self.comm.sources)],
                       refs[a:a + self.n_tgt], refs[a + self.n_tgt + self.n_scratch:])

    def pallas_call(self, body, *, in_specs, out_specs, out_shape, scratch_shapes, args, name):
        comm = self.comm
        n_in, n_out = len(in_specs), len(out_specs)
        aliases = {}
        if comm is not None:
            in_specs = in_specs + [ANY] * self.n_op
            out_specs = out_specs + [ANY] * self.n_tgt
            out_shape = out_shape + comm.out_shapes
            scratch_shapes = scratch_shapes + comm.sems
            args = args + comm.operands
            aliases = comm.aliases(n_in, n_out)
        sem = ("parallel",) * (len(self.grid) - 1) + (self.last,) if comm is None else ("arbitrary",) * len(self.grid)
        results = pl.pallas_call(
            body, grid=self.grid, in_specs=in_specs, out_specs=out_specs, out_shape=out_shape,
            scratch_shapes=scratch_shapes, input_output_aliases=aliases,
            compiler_params=_params(*sem), name=name)(*args)
        return results[:n_out], results[n_out:]


def _sb_fwd(qkv, *, batch, seq, name, comm=None, tq=SB_TILE, n_pre=SB_STRAIGHT, pairs=SB_PAIRS):
    T, D3 = qkv.shape
    D = D3 // 3
    nhp = D // LANES
    tk = tq
    nq = seq // tq
    scale = SB_HEAD_DIM ** -0.5
    assert 1 <= n_pre <= nq and nhp % pairs == 0

    def body(*refs):
        (q_ref, k_ref, v_ref), (o_ref,), (acc,) = carried.split(refs, 3, 1, 1)
        qi = pl.program_id(2)
        carried.emit(refs)
        qs = [_stack_heads(q_ref[:, lanes]) * scale for lanes in cols]
        past = _past_mask(tq, tk)
        u = _suffix_matrix(tk)

        def block(kb, g, c, diag):
            ks = pl.multiple_of(kb * tk, tk)
            z = _dot_nt(qs[g], k_ref[pl.ds(ks, tk), cols[g]])
            log_beta, l = _sb_logs(z, past if diag else None)
            arg = log_beta + _suffix_sum(l, u, exact=False)
            a = jnp.exp(arg if c is None else arg + c)
            if diag:
                a = jnp.where(past, a, 0.0)
            return _dot(a.astype(BF16), v_ref[pl.ds(ks, tk), cols[g]]), jnp.sum(l, axis=1, keepdims=True)

        def straight(n):
            o_sum, c = [None] * pairs, [None] * pairs
            for b in range(n):
                for g in range(pairs):
                    o_b, c_b = block(qi - b, g, c[g], b == 0)
                    o_sum[g] = o_b if b == 0 else o_sum[g] + o_b
                    c[g] = c_b if b == 0 else c[g] + c_b
            return o_sum, c

        def finish(o_sum):
            for g in range(pairs):
                o_ref[:, cols[g]] = _unstack_heads(o_sum[g], tq)

        for n in range(1, n_pre):
            @pl.when(qi == n - 1)
            def _(n=n):
                finish(straight(n)[0])

        @pl.when(qi >= n_pre - 1)
        def _():
            o_sum, c = straight(n_pre)
            for g in range(pairs):
                acc[g] = o_sum[g]

            def cond(st):
                kb, c = st
                return jnp.logical_and(kb >= 0, _max_of(c) > EXP_ZERO_BELOW)

            def step(st):
                kb, c = st
                new_c = []
                for g in range(pairs):
                    o_n, c_n = block(kb, g, c[g], False)
                    acc[g] += o_n
                    new_c.append(c[g] + c_n)
                return kb - 1, tuple(new_c)

            lax.while_loop(cond, step, (qi - n_pre, tuple(c)))
            finish([acc[g] for g in range(pairs)])

    cols = [slice(g * LANES, (g + 1) * LANES) for g in range(pairs)]
    width = pairs * LANES
    carried = _Carried(comm, (batch, nhp // pairs, nq))
    (o,), comm_results = carried.pallas_call(
        body,
        in_specs=[pl.BlockSpec((tq, width), lambda b, p, i: (b * nq + i, p)),
                  pl.BlockSpec((seq, width), lambda b, p, i: (b, nhp // pairs + p)),
                  pl.BlockSpec((seq, width), lambda b, p, i: (b, 2 * (nhp // pairs) + p))],
        out_specs=[pl.BlockSpec((tq, width), lambda b, p, i: (b * nq + i, p))],
        out_shape=[jax.ShapeDtypeStruct((T, D), F32)],
        scratch_shapes=[pltpu.VMEM((pairs, 2 * tq, LANES), F32)],
        args=[qkv, qkv, qkv], name=name)
    return o, comm_results


def _sb_bwd(qkv, o, do, *, batch, seq, name, comm=None, tq=SB_TILE, n_pre=SB_STRAIGHT, pairs=SB_PAIRS):
    T, D3 = qkv.shape
    D = D3 // 3
    nhp = D // LANES
    tk = tq
    nq = seq // tq
    scale = SB_HEAD_DIM ** -0.5

    def body(*refs):
        ins, outs, scratch = carried.split(refs, 5, 3, 3)
        q_ref, k_ref, v_ref, o_ref, do_ref = ins
        dq_ref, dk_ref, dv_ref = outs
        dq_acc, dk_acc, dv_acc = scratch
        qi = pl.program_id(2)
        carried.emit(refs)

        @pl.when(qi == 0)
        def _():
            dk_acc[...] = jnp.zeros_like(dk_acc)
            dv_acc[...] = jnp.zeros_like(dv_acc)

        qs = [_stack_heads(q_ref[:, lanes]) * scale for lanes in cols]
        dos = [_stack_heads(do_ref[:, lanes]) for lanes in cols]
        dsum = [jnp.sum(_stack_heads(do_ref[:, lanes].astype(F32) * o_ref[:, lanes]), axis=1, keepdims=True)
                for lanes in cols]
        past = _past_mask(tq, tk)
        u = _suffix_matrix(tk)

        def block(kb, p, c, gc, diag):
            ks = pl.multiple_of(kb * tk, tk)
            kblk = k_ref[pl.ds(ks, tk), cols[p]]
            vblk = v_ref[pl.ds(ks, tk), cols[p]]
            z = _dot_nt(qs[p], kblk)
            log_beta, l = _sb_logs(z, past if diag else None)
            arg = log_beta + _suffix_sum(l, u, exact=False)
            a = jnp.exp(arg if c is None else arg + c)
            if diag:
                a = jnp.where(past, a, 0.0)
            a = a.astype(BF16)
            beta = 1.0 - jnp.exp(l)
            g = a.astype(F32) * _dot_nt(dos[p], vblk)
            gs = _suffix_sum(g, u)
            dz = g - beta * (dsum[p] - (gs if gc is None else gs + gc))
            if diag:
                dz = jnp.where(past, dz, 0.0)
            dzb = dz.astype(BF16)
            dk_acc[pl.ds(ks, tk), cols[p]] += _dot_tn(dzb, qs[p])
            dv_acc[pl.ds(ks, tk), cols[p]] += _dot_tn(a, dos[p])
            return (_dot(dzb, kblk), jnp.sum(l, axis=1, keepdims=True),
                    jnp.sum(g, axis=1, keepdims=True))

        def finish(dq_sum):
            for p in range(pairs):
                dq_ref[:, cols[p]] = (_unstack_heads(dq_sum[p], tq) * scale).astype(BF16)

        def straight(n):
            dq_sum, c, gc = [None] * pairs, [None] * pairs, [None] * pairs
            for b in range(n):
                for p in range(pairs):
                    dq_b, c_b, g_b = block(qi - b, p, c[p], gc[p], b == 0)
                    dq_sum[p] = dq_b if b == 0 else dq_sum[p] + dq_b
                    c[p] = c_b if b == 0 else c[p] + c_b
                    gc[p] = g_b if b == 0 else gc[p] + g_b
            return dq_sum, c, gc

        for n in range(1, n_pre):
            @pl.when(qi == n - 1)
            def _(n=n):
                finish(straight(n)[0])

        @pl.when(qi >= n_pre - 1)
        def _():
            dq_sum, c, gc = straight(n_pre)
            for p in range(pairs):
                dq_acc[p] = dq_sum[p]

            def cond(st):
                kb, c, gc = st
                return jnp.logical_and(kb >= 0, _max_of(c) > EXP_ZERO_BELOW)

            def step(st):
                kb, c, gc = st
                new_c, new_gc = [], []
                for p in range(pairs):
                    dq_n, c_n, g_n = block(kb, p, c[p], gc[p], False)
                    dq_acc[p] += dq_n
                    new_c.append(c[p] + c_n)
                    new_gc.append(gc[p] + g_n)
                return kb - 1, tuple(new_c), tuple(new_gc)

            lax.while_loop(cond, step, (qi - n_pre, tuple(c), tuple(gc)))
            finish([dq_acc[p] for p in range(pairs)])

        @pl.when(qi == nq - 1)
        def _():
            dk_ref[...] = dk_acc[...].astype(BF16)
            dv_ref[...] = dv_acc[...].astype(BF16)

    cols = [slice(p * LANES, (p + 1) * LANES) for p in range(pairs)]
    width, ncb = pairs * LANES, nhp // pairs
    qspec = pl.BlockSpec((tq, width), lambda b, p, i: (b * nq + i, p))
    sspec = pl.BlockSpec((seq, width), lambda b, p, i: (b, p))
    out = jax.ShapeDtypeStruct((T, D), BF16)
    carried = _Carried(comm, (batch, ncb, nq))
    return carried.pallas_call(
        body,
        in_specs=[qspec,
                  pl.BlockSpec((seq, width), lambda b, p, i: (b, ncb + p)),
                  pl.BlockSpec((seq, width), lambda b, p, i: (b, 2 * ncb + p)),
                  qspec, qspec],
        out_specs=[qspec, sspec, sspec], out_shape=[out, out, out],
        scratch_shapes=[pltpu.VMEM((pairs, 2 * tq, LANES), F32), pltpu.VMEM((seq, width), F32),
                        pltpu.VMEM((seq, width), F32)],
        args=[qkv, qkv, qkv, o, do], name=name)


_GELU_C = 0.7978845608028654
_GELU_A = 0.044715


def _gelu(x):
    x = x.astype(BF16)
    xx = x * x
    a1 = 1.0 + jnp.tanh(x * (_GELU_C + (_GELU_C * _GELU_A) * xx))
    hx = 0.5 * x
    grad = a1 * (0.5 + (hx * (2.0 - a1)) * (_GELU_C + (3.0 * _GELU_C * _GELU_A) * xx))
    return hx * a1, grad


def _causal_ws(ws_ref, g):
    t = lax.broadcasted_iota(jnp.int32, (SGU_CHUNK, SGU_CHUNK), 0)
    s = lax.broadcasted_iota(jnp.int32, (SGU_CHUNK, SGU_CHUNK), 1)
    return jnp.where(s <= t, ws_ref[g], 0.0)


def _sgu_gate(u, v, g_ref, ws_ref, bs_ref):
    G, W = SGU_GROUPS, SGU_GROUP_W
    r = lax.rsqrt(jnp.mean(v * v, axis=-1, keepdims=True) + EPS)
    vn = (v * r * g_ref[...]).astype(BF16)
    y = []
    for g in range(G):
        sl = slice(g * W, (g + 1) * W)
        mixed = _dot(_causal_ws(ws_ref, g).astype(BF16), vn[:, sl]) + bs_ref[:, g:g + 1]
        y.append((u[:, sl] * mixed).astype(BF16))
    return jnp.concatenate(y, axis=1)


def _sgu_bwd(uv, dgelu, dy, gain, ws, bst, *, name):
    T, F2 = uv.shape
    F = F2 // 2
    C, G, W = SGU_CHUNK, SGU_GROUPS, SGU_GROUP_W
    nc = T // C

    def body(uv_ref, dgelu_ref, dy_ref, g_ref, ws_ref, bs_ref, duv_ref, dg_ref, dws_ref, dbs_ref,
             dg_acc, dws_acc, dbs_acc):
        i = pl.program_id(0)

        @pl.when(i == 0)
        def _():
            dg_acc[...] = jnp.zeros_like(dg_acc)
            dws_acc[...] = jnp.zeros_like(dws_acc)
            dbs_acc[...] = jnp.zeros_like(dbs_acc)

        u, v = uv_ref[:, :F].astype(F32), uv_ref[:, F:].astype(F32)
        dgelu = dgelu_ref[...].astype(F32)
        r = lax.rsqrt(jnp.mean(v * v, axis=-1, keepdims=True) + EPS)
        vhat = v * r
        gain_v = g_ref[...]
        vn = (vhat * gain_v).astype(BF16)
        dyv = dy_ref[...].astype(F32)
        lane8 = lax.broadcasted_iota(jnp.int32, (1, G), 1)
        dvn_parts = []
        dbs_new = jnp.zeros((C, G), F32)
        for g in range(G):
            sl = slice(g * W, (g + 1) * W)
            wsg = _causal_ws(ws_ref, g)
            mixed = _dot(wsg.astype(BF16), vn[:, sl]) + bs_ref[:, g:g + 1]
            duv_ref[:, sl] = (dyv[:, sl] * mixed * dgelu[:, sl]).astype(BF16)
            dmix = dyv[:, sl] * u[:, sl]
            dbs_new = dbs_new + jnp.where(lane8 == g, jnp.sum(dmix, axis=1, keepdims=True), 0.0)
            dmix_b = dmix.astype(BF16)
            dws_acc[g] += _dot_nt(dmix_b, vn[:, sl])
            dvn_parts.append(_dot(wsg.T.astype(BF16), dmix_b))
        dbs_acc[...] += dbs_new
        dvn = jnp.concatenate(dvn_parts, axis=1)
        dg_acc[...] += jnp.sum((dvn * vhat).reshape(C // SUBLANES, SUBLANES, F), axis=0)
        dvhat = dvn * gain_v
        dv = r * (dvhat - vhat * jnp.mean(dvhat * vhat, axis=-1, keepdims=True))
        duv_ref[:, F:] = (dv * dgelu[:, F:]).astype(BF16)

        @pl.when(i == nc - 1)
        def _():
            dg_ref[...] = jnp.sum(dg_acc[...], axis=0, keepdims=True)
            t = lax.broadcasted_iota(jnp.int32, (G, C, C), 1)
            s = lax.broadcasted_iota(jnp.int32, (G, C, C), 2)
            dws_ref[...] = jnp.where(s <= t, dws_acc[...], 0.0)
            dbs_ref[...] = dbs_acc[...]

    return pl.pallas_call(
        body, grid=(nc,),
        in_specs=[pl.BlockSpec((C, F2), lambda i: (i, 0)), pl.BlockSpec((C, F2), lambda i: (i, 0)),
                  pl.BlockSpec((C, F), lambda i: (i, 0)),
                  pl.BlockSpec((1, F), lambda i: (0, 0)), pl.BlockSpec((G, C, C), lambda i: (0, 0, 0)),
                  pl.BlockSpec((C, G), lambda i: (0, 0))],
        out_specs=[pl.BlockSpec((C, F2), lambda i: (i, 0)), pl.BlockSpec((1, F), lambda i: (0, 0)),
                   pl.BlockSpec((G, C, C), lambda i: (0, 0, 0)), pl.BlockSpec((C, G), lambda i: (0, 0))],
        out_shape=[jax.ShapeDtypeStruct((T, F2), BF16), jax.ShapeDtypeStruct((1, F), F32),
                   jax.ShapeDtypeStruct((G, C, C), F32), jax.ShapeDtypeStruct((C, G), F32)],
        scratch_shapes=[pltpu.VMEM((SUBLANES, F), F32), pltpu.VMEM((G, C, C), F32), pltpu.VMEM((C, G), F32)],
        compiler_params=_params("arbitrary"), name=name)(uv, dgelu, dy, gain, ws, bst)


def _my_place():
    return lax.axis_index("x"), lax.axis_index("y"), lax.axis_index("c")


def _all_gather(shards, *, name):
    nf = len(shards)
    items = [(lambda ins, f=f: ins[f], lambda outs, p, f=f: outs[f].at[:, p]) for f in range(nf)]
    targets = [jax.ShapeDtypeStruct((s.shape[0], N_DEV) + s.shape[1:], s.dtype) for s in shards]
    return _run_comm(_Comm(shards, targets, len(items), _gather_emit(items, [list(range(nf))])), name=name)


class _Comm:
    def __init__(self, sources, targets, n_items, emit):
        self.sources, self.targets, self.n_items, self.emit = list(sources), list(targets), n_items, emit
        self.filled = [t for t in self.targets if not isinstance(t, jax.ShapeDtypeStruct)]
        self.operands = self.sources + self.filled
        self.out_shapes = [jax.ShapeDtypeStruct(t.shape, t.dtype) for t in self.targets]
        self.sems = [pltpu.SemaphoreType.DMA((n_items, 7)), pltpu.SemaphoreType.DMA((n_items, 7)),
                     pltpu.SemaphoreType.DMA((n_items,))]

    def aliases(self, first_operand, first_result):
        pos = {id(t): k for k, t in enumerate(self.targets)}
        return {first_operand + len(self.sources) + a: first_result + pos[id(t)]
                for a, t in enumerate(self.filled)}


def _run_comm(comm, *, name):
    n_op, n_out = len(comm.operands), len(comm.targets)

    def body(*refs):
        comm.emit(0, 1, refs[:len(comm.sources)], refs[n_op:n_op + n_out], refs[n_op + n_out:])

    return pl.pallas_call(
        body, in_specs=[ANY] * n_op, out_specs=[ANY] * n_out, out_shape=comm.out_shapes,
        scratch_shapes=comm.sems, input_output_aliases=comm.aliases(0, 0), name=name)(*comm.operands)


def _at_steps(step, n_steps, phases):
    if n_steps == 1:
        for _, fn in phases:
            fn()
        return
    marks = {}
    for frac, fn in phases:
        marks.setdefault(min(int(frac * n_steps), n_steps - 1), []).append(fn)
    for mark, fns in sorted(marks.items()):
        @pl.when(step == mark)
        def _(fns=fns):
            for fn in fns:
                fn()


def _gather_emit(items, groups, fractions=None):
    if fractions is None:
        fractions = [(g + 1) / len(groups) for g in range(len(groups))]
    def emit(step, n_steps, ins, outs, sems):
        send_sems, recv_sems, local_sems = sems
        x, y, c = _my_place()
        me, sibling = (x, y, c), (x, y, 1 - c)
        chips = [(1 - x, y), (x, 1 - y), (1 - x, 1 - y)]

        def copy(i, k, block, to, own=False):
            src_of, dst_of = items[i]
            dst = dst_of(outs, 4 * block[0] + 2 * block[1] + block[2])
            return pltpu.make_async_remote_copy(
                src_ref=src_of(ins) if own else dst, dst_ref=dst,
                send_sem=send_sems.at[i, k], recv_sem=recv_sems.at[i, k],
                device_id=to, device_id_type=MESH)

        def local(i):
            src_of, dst_of = items[i]
            return pltpu.make_async_copy(src_of(ins), dst_of(outs, 4 * x + 2 * y + c), local_sems.at[i])

        def first(i):
            return [copy(i, 0, me, sibling, own=True)] + [
                copy(i, 1 + j, me, (*chip, c), own=True) for j, chip in enumerate(chips)]

        def start():
            for i in range(len(items)):
                local(i).start()
                for cp in first(i):
                    cp.start()

        def forward(group):
            for j, chip in enumerate(chips):
                for i in group:
                    copy(i, 1 + j, (*chip, c), me).wait_recv()
                    copy(i, 4 + j, (*chip, c), sibling).start()

        def finish():
            for i in range(len(items)):
                copy(i, 0, sibling, me).wait_recv()
                for j, chip in enumerate(chips):
                    copy(i, 4 + j, (*chip, 1 - c), me).wait_recv()
            for i in range(len(items)):
                for cp in first(i) + [copy(i, 4 + j, (*chip, c), sibling) for j, chip in enumerate(chips)]:
                    cp.wait_send()
                local(i).wait()

        phases = [(0.0, start)]
        for frac, group in zip(fractions, groups):
            phases.append((frac, functools.partial(forward, group)))
        phases.append((1.0, finish))
        _at_steps(step, n_steps, phases)

    return emit


def _exchange_emit(items):
    def emit(step, n_steps, ins, outs, sems):
        send_sems, recv_sems, local_sems = sems
        x, y, c = _my_place()
        me = 4 * x + 2 * y + c

        def peer_of(k):
            return x ^ (k >> 2), y ^ ((k >> 1) & 1), c ^ (k & 1)

        def copy(i, k):
            src_of, dst_of = items[i]
            px, py, pc = peer_of(k)
            return pltpu.make_async_remote_copy(
                src_ref=src_of(ins, 4 * px + 2 * py + pc), dst_ref=dst_of(outs, me),
                send_sem=send_sems.at[i, k - 1], recv_sem=recv_sems.at[i, k - 1],
                device_id=(px, py, pc), device_id_type=MESH)

        def arrival(i, k):
            src_of, dst_of = items[i]
            px, py, pc = peer_of(k)
            peer = 4 * px + 2 * py + pc
            return pltpu.make_async_remote_copy(
                src_ref=src_of(ins, peer), dst_ref=dst_of(outs, peer),
                send_sem=send_sems.at[i, k - 1], recv_sem=recv_sems.at[i, k - 1],
                device_id=(x, y, c), device_id_type=MESH)

        def local(i):
            src_of, dst_of = items[i]
            return pltpu.make_async_copy(src_of(ins, me), dst_of(outs, me), local_sems.at[i])

        def start():
            for i in range(len(items)):
                local(i).start()
            for k in range(1, N_DEV):
                for i in range(len(items)):
                    copy(i, k).start()

        def finish():
            for k in range(1, N_DEV):
                for i in range(len(items)):
                    arrival(i, k).wait_recv()
            for k in range(1, N_DEV):
                for i in range(len(items)):
                    copy(i, k).wait_send()
            for i in range(len(items)):
                local(i).wait()

        _at_steps(step, n_steps, [(0.0, start), (1.0, finish)])

    return emit


def _adam_math(g, w, m, v):
    m = ADAM_B1 * m + (1.0 - ADAM_B1) * g
    v = ADAM_B2 * v + (1.0 - ADAM_B2) * (g * g)
    m_hat = m / (1.0 - ADAM_B1 ** ADAM_STEP)
    v_hat = v / (1.0 - ADAM_B2 ** ADAM_STEP)
    delta = -ADAM_LR * (m_hat / (jnp.sqrt(v_hat) + ADAM_EPS) + ADAM_WD * w)
    return delta, m, v


def _sum_adamw(parts, w, m, v, *, name, tr=256):
    L, nd, R, C = parts.shape
    tr = min(tr, R)
    assert R % tr == 0

    def body(p_ref, w_ref, m_ref, v_ref, g_ref, d_ref, nm_ref, nv_ref):
        g = p_ref[0, 0].astype(F32)
        for q in range(1, nd):
            g = g + p_ref[0, q].astype(F32)
        d, nm, nv = _adam_math(g, w_ref[0], m_ref[0], v_ref[0])
        g_ref[0] = g
        d_ref[0] = d
        nm_ref[0] = nm
        nv_ref[0] = nv

    blk = pl.BlockSpec((1, tr, C), lambda l, i: (l, i, 0))
    out = jax.ShapeDtypeStruct((L, R, C), F32)
    return pl.pallas_call(
        body, grid=(L, R // tr),
        in_specs=[pl.BlockSpec((1, nd, tr, C), lambda l, i: (l, 0, i, 0)), blk, blk, blk],
        out_specs=[blk] * 4, out_shape=[out] * 4,
        compiler_params=_params("parallel", "parallel"), name=name)(parts, w, m, v)


def _sum_parts(parts, *, name):
    nd, R, C = parts.shape

    def body(p_ref, o_ref):
        g = p_ref[0]
        for q in range(1, nd):
            g = g + p_ref[q]
        o_ref[...] = g

    return pl.pallas_call(
        body, out_shape=jax.ShapeDtypeStruct((R, C), F32),
        in_specs=[pl.BlockSpec(memory_space=pltpu.VMEM)],
        out_specs=pl.BlockSpec(memory_space=pltpu.VMEM), name=name)(parts)


def _adamw_small(g, w, m, v, *, name):
    def body(g_ref, w_ref, m_ref, v_ref, d_ref, nm_ref, nv_ref):
        d, nm, nv = _adam_math(g_ref[...], w_ref[...], m_ref[...], v_ref[...])
        d_ref[...] = d
        nm_ref[...] = nm
        nv_ref[...] = nv

    vm = pl.BlockSpec(memory_space=pltpu.VMEM)
    out = jax.ShapeDtypeStruct(g.shape, F32)
    return pl.pallas_call(body, out_shape=[out] * 3, in_specs=[vm] * 4, out_specs=[vm] * 3,
                          name=name)(g, w, m, v)


def kernel(x, norm_mix, norm_mlp, sb_wqkv, sb_wo, sgu_win, sgu_gain, sgu_ws, sgu_bs, sgu_wout, mlp_w1, mlp_w2, final_norm, loss_target, m_norm_mix, m_norm_mlp, m_sb_wqkv, m_sb_wo, m_sgu_win, m_sgu_gain, m_sgu_ws, m_sgu_bs, m_sgu_wout, m_mlp_w1, m_mlp_w2, m_final_norm, v_norm_mix, v_norm_mlp, v_sb_wqkv, v_sb_wo, v_sgu_win, v_sgu_gain, v_sgu_ws, v_sgu_bs, v_sgu_wout, v_mlp_w1, v_mlp_w2, v_final_norm):
    batch, seq, D = x.shape
    T = batch * seq
    x0 = x.reshape(T, D)
    target = loss_target.reshape(T, D)

    WQKV, WO, WIN, WOUT, W1, W2, GAIN = range(7)
    big = [sb_wqkv, sb_wo, sgu_win, sgu_wout, mlp_w1, mlp_w2]
    shards = [w.astype(BF16) for w in big] + [sgu_gain[:, None, :]]
    rides = {
        "qkv0": [[(WO, 0), (W1, 0)]],
        "sb_fwd0": [[(WIN, 0), (WOUT, 0), (W1, 1), (W2, 1)], [(WQKV, 1)]],
        "w1_0": [[(W2, 0)]],
        "qkv2": [[(WO, 1), (W1, 2)]],
        "sb_fwd2": [[(W2, 2)], [(WIN, 1), (WOUT, 1), (W1, 3), (W2, 3)]],
    }

    first = [(lambda ins: ins[WQKV].at[0], lambda outs, p: outs[WQKV].at[0, p]),
             (lambda ins: ins[GAIN], lambda outs, p: outs[GAIN].at[:, p])]
    targets = [jax.ShapeDtypeStruct((s.shape[0], N_DEV) + s.shape[1:], s.dtype) for s in shards]
    gathered0 = _run_comm(_Comm(shards, targets, len(first), _gather_emit(first, [[0, 1]])), name="gather_first")
    gain_sgu = gathered0[GAIN].reshape(-1, 1, SGU_FFN)
    gw = dict(enumerate(gathered0[:GAIN]))

    def gather_ride(call):
        wanted = [fl for group in rides[call] for fl in group]
        fams = sorted({f for f, _ in wanted})
        items = [(lambda ins, f=f, l=l: ins[f].at[l], lambda outs, p, t=fams.index(f), l=l: outs[t].at[l, p])
                 for f, l in wanted]
        groups, sent, k = [], [], 0
        for group in rides[call]:
            groups.append(list(range(k, k + len(group))))
            sent.append(sum(shards[f].shape[1] * shards[f].shape[2] for f, _ in group) + (sent[-1] if sent else 0))
            k += len(group)
        fractions = [s / sent[-1] for s in sent]
        return _Comm(shards[:GAIN], [gw[f] for f in fams], len(items), _gather_emit(items, groups, fractions)), fams

    def weight(f):
        g = gw[f]
        return g if f in (WQKV, WIN, W1) else g.reshape(g.shape[0], 1, N_DEV * g.shape[2], g.shape[3])

    saved = []
    xs = x0
    for i in range(DEPTH):
        j = i // 2
        if i % 2 == 0:
            comm, fams = gather_ride(f"qkv{i}")
            (qkv, h), results = _mm_nn(xs, weight(WQKV), j, out_dtype=BF16, gain=norm_mix[i:i + 1],
                                       name=f"qkv{i}", comm=comm, tm=1024)
            gw.update(zip(fams, results))
            comm, fams = gather_ride(f"sb_fwd{i}")
            o, results = _sb_fwd(qkv, batch=batch, seq=seq, name=f"sb_fwd{i}", comm=comm)
            gw.update(zip(fams, results))
            x_mid = _mm_nn(o, weight(WO), j, out_dtype=F32, res=xs, name=f"wo{i}", tm=1024)
            mix = (h, qkv, o)
        else:
            gain_j = gain_sgu[j]
            bst = sgu_bs[j].T
            uv, h, dgelu, yv = _mm_nn(xs, weight(WIN), j, out_dtype=BF16, gain=norm_mix[i:i + 1],
                                      sgu=(gain_j, sgu_ws[j], bst), name=f"win_sgu{i}")
            x_mid = _mm_nn(yv, weight(WOUT), j, out_dtype=F32, res=xs, name=f"wout{i}", tm=1024)
            mix = (h, uv, dgelu, yv, gain_j, bst)
        if f"w1_{i}" in rides:
            comm, fams = gather_ride(f"w1_{i}")
            (a, h2), results = _mm_nn(x_mid, weight(W1), i, out_dtype=BF16, gain=norm_mlp[i:i + 1],
                                      name=f"w1_{i}", comm=comm, tm=1024)
            gw.update(zip(fams, results))
        else:
            a, h2 = _mm_nn(x_mid, weight(W1), i, out_dtype=BF16, gain=norm_mlp[i:i + 1], name=f"w1_{i}", tm=1024)
        saved.append((xs, mix, x_mid, h2, a))
        if i < DEPTH - 1:
            xs = _mm_nn(a, weight(W2), i, out_dtype=F32, res=x_mid, a_act="relu2", name=f"w2_{i}")
    g_wqkv, g_wo, g_win, g_wout, g_w1, g_w2 = [weight(f) for f in range(GAIN)]

    dx, dxb, sq, d_final = _mlp_out_loss_head(a, g_w2, DEPTH - 1, x_mid, final_norm.reshape(1, D), target,
                                              name="w2_loss_head")
    loss = lax.psum(0.5 * jnp.sum(sq) / D, ("x", "y", "c"))

    SMALL = GAIN
    stacks = {f: jax.ShapeDtypeStruct((w.shape[0], N_DEV) + w.shape[1:], BF16) for f, w in enumerate(big)}
    pending = []
    second_half = []

    def row_shards(p):
        return p.reshape(N_DEV, p.shape[1] // N_DEV, p.shape[2])

    def exchange(going):
        fams = sorted({f for _, f, _, _ in going})

        def item(a, f, l, rows):
            rows = slice(None) if rows is None else slice(*rows)
            return (lambda ins, p: ins[a].at[p, rows], lambda outs, q: outs[fams.index(f)].at[l, q, rows])

        items = [item(a, f, l, rows) for a, (_, f, l, rows) in enumerate(going)]
        return _Comm([g[0] for g in going], [stacks[f] for f in fams], len(items), _exchange_emit(items)), fams

    def take(entries):
        going = list(entries)
        entries.clear()
        return going

    d_norm_mix, d_norm_mlp = [None] * DEPTH, [None] * DEPTH
    d_gain, d_ws, d_bs = [None] * 2, [None] * 2, [None] * 2
    for i in reversed(range(DEPTH)):
        j = i // 2
        xs, mix, x_mid, h2, a = saved[i]
        da = _mm_nt(dxb, g_w2, i, out_dtype=BF16, act_src=a, name=f"d_a{i}")
        pending.append((row_shards(_mm_tn(a, dxb, shards=1, a_act="relu2", name=f"d_w2_{i}")), W2, i, None))
        pending.append((_mm_tn(h2, da, shards=N_DEV, name=f"d_w1_{i}", pb=4, tm=1024), W1, i, None))
        comm, fams = exchange(take(second_half)) if second_half else (None, [])
        (dx, dxb, d_norm_mlp[i]), results = _mm_nt_rms_bwd(
            da, g_w1, i, x_mid, norm_mlp[i:i + 1], dx, name=f"d_h2_{i}", comm=comm)
        stacks.update(zip(fams, results))
        if i % 2 == 0:
            h, qkv, o = mix
            do = _mm_nt(dxb, g_wo, j, out_dtype=BF16, name=f"d_o{i}", tm=1024)
            pending.append((row_shards(_mm_tn(o, dxb, shards=1, name=f"d_wo{i}")), WO, j, None))
            comm, fams = exchange(take(pending))
            (dq, dk, dv), results = _sb_bwd(qkv, o, do, batch=batch, seq=seq, name=f"sb_bwd{i}", comm=comm)
            stacks.update(zip(fams, results))
            mixer_in, w_in, f_in = [dq, dk, dv], g_wqkv, WQKV
            d_w_in = _mm_tn(h, mixer_in, shards=N_DEV, name=f"d_wqkv{i}", pb=N_DEV, tm=1024, tk=512)
        else:
            h, uv, dgelu, yv, gain_j, bst = mix
            dy = _mm_nt(dxb, g_wout, j, out_dtype=BF16, name=f"d_y{i}", tm=1024)
            pending.append((row_shards(_mm_tn(yv, dxb, shards=1, name=f"d_wout{i}")), WOUT, j, None))
            mixer_in, d_gain[j], d_ws[j], dbst = _sgu_bwd(uv, dgelu, dy, gain_j, sgu_ws[j], bst, name=f"sgu_bwd{i}")
            d_bs[j] = dbst.T
            w_in, f_in = g_win, WIN
            d_w_in = _mm_tn(h, mixer_in, shards=N_DEV, name=f"d_win{i}", pb=4, tm=1024)
        rows = d_w_in.shape[1]
        if i > 0:
            second_half.append((d_w_in, f_in, j, (rows // 2, rows)))
        comm, fams = exchange([(d_w_in, f_in, j, (0, rows // 2) if i > 0 else None)])
        (dx, dxb, d_norm_mix[i]), results = _mm_nt_rms_bwd(
            mixer_in, w_in, j, xs, norm_mix[i:i + 1], dx, name=f"d_h_mix{i}", comm=comm)
        stacks.update(zip(fams, results))
    grad_x = dx.reshape(batch, seq, D)

    small = [jnp.concatenate(d_norm_mix, 0), jnp.concatenate(d_norm_mlp, 0), d_final,
             jnp.concatenate(d_gain, 0), jnp.stack(d_bs, 0), jnp.stack(d_ws, 0)]
    small_flat = jnp.concatenate([s.reshape(-1) for s in small])
    n_small = small_flat.shape[0]
    small_rows = -(-n_small // (N_DEV * SUBLANES * LANES)) * SUBLANES
    small_flat = jnp.pad(small_flat, (0, N_DEV * small_rows * LANES - n_small))
    stacks[SMALL] = jax.ShapeDtypeStruct((1, N_DEV, small_rows, LANES), F32)
    pending.append((small_flat.reshape(N_DEV, small_rows, LANES), SMALL, 0, None))
    comm, fams = exchange(take(pending))
    stacks.update(zip(fams, _run_comm(comm, name="exchange_last")))
    r_wqkv, r_wo, r_win, r_wout, r_w1, r_w2, r_small = [stacks[f] for f in range(SMALL + 1)]

    u_wqkv = _sum_adamw(r_wqkv, sb_wqkv, m_sb_wqkv, v_sb_wqkv, name="adamw_wqkv")
    u_wo = _sum_adamw(r_wo, sb_wo, m_sb_wo, v_sb_wo, name="adamw_wo")
    u_win = _sum_adamw(r_win, sgu_win, m_sgu_win, v_sgu_win, name="adamw_win")
    u_wout = _sum_adamw(r_wout, sgu_wout, m_sgu_wout, v_sgu_wout, name="adamw_wout")
    u_w1 = _sum_adamw(r_w1, mlp_w1, m_mlp_w1, v_mlp_w1, name="adamw_w1")
    u_w2 = _sum_adamw(r_w2, mlp_w2, m_mlp_w2, v_mlp_w2, name="adamw_w2")

    small_sum = _sum_parts(r_small[0], name="sum_small")
    g_small = _all_gather([small_sum[None]], name="gather_small")[0].reshape(-1)[:n_small]

    shapes = [s.shape for s in small]
    sizes = [s.size for s in small]
    offs = [sum(sizes[:k]) for k in range(len(sizes))]
    me = 4 * lax.axis_index("x") + 2 * lax.axis_index("y") + lax.axis_index("c")
    shard_w = SGU_FFN // N_DEV

    def pack(arrs):
        flat = jnp.concatenate([a_.reshape(-1) for a_ in arrs])
        return jnp.pad(flat, (0, N_DEV * small_rows * LANES - n_small)).reshape(-1, LANES)

    def full_gain(gshard):
        return lax.dynamic_update_slice(jnp.zeros((2, SGU_FFN), F32), gshard, (0, me * shard_w))

    w_small = pack([norm_mix, norm_mlp, final_norm, full_gain(sgu_gain), sgu_bs, sgu_ws])
    m_small = pack([m_norm_mix, m_norm_mlp, m_final_norm, full_gain(m_sgu_gain), m_sgu_bs, m_sgu_ws])
    v_small = pack([v_norm_mix, v_norm_mlp, v_final_norm, full_gain(v_sgu_gain), v_sgu_bs, v_sgu_ws])
    g_pack = jnp.pad(g_small, (0, N_DEV * small_rows * LANES - n_small)).reshape(-1, LANES)
    sm = [g_pack] + list(_adamw_small(g_pack, w_small, m_small, v_small, name="adamw_small"))

    def unpack(flat2d):
        flat = flat2d.reshape(-1)
        out = [flat[offs[k]:offs[k] + sizes[k]].reshape(shapes[k]) for k in range(len(sizes))]
        out[2] = out[2].reshape(D)
        out[3] = lax.dynamic_slice(out[3], (0, me * shard_w), (2, shard_w))
        return out

    outs = []
    for k, big_u in enumerate(zip(u_wqkv, u_wo, u_win, u_wout, u_w1, u_w2)):
        s_nm, s_nl, s_fn, s_gain, s_bs, s_ws = unpack(sm[k])
        b_wqkv, b_wo, b_win, b_wout, b_w1, b_w2 = big_u
        outs += [s_nm, s_nl, b_wqkv, b_wo, b_win, s_gain, s_ws, s_bs, b_wout, b_w1, b_w2, s_fn]
    return (loss, grad_x, *outs)
```

```python
import functools

import jax
import jax.numpy as jnp
from jax import lax
from jax.experimental import pallas as pl
from jax.experimental.pallas import tpu as pltpu

F32 = jnp.float32
BF16 = jnp.bfloat16

N_DEV = 8
D_MODEL = 1024
SEQ = 2048
DEPTH = 4
SB_HEAD_DIM = 64
SGU_CHUNK = 128
SGU_FFN = 2 * D_MODEL
SGU_GROUPS = 8
SGU_GROUP_W = SGU_FFN // SGU_GROUPS
EPS = 1e-6

ADAM_LR = 0.001
ADAM_B1 = 0.9
ADAM_B2 = 0.999
ADAM_EPS = 1e-08
ADAM_WD = 0.01
ADAM_STEP = 10

MXU_TILE = 256
LANES = 128
SUBLANES = 8
VMEM_LIMIT = 56 * 1024 * 1024
EXP_ZERO_BELOW = -104.0
SB_TILE = 256
SB_STRAIGHT = 2
SB_PAIRS = 4

MESH = pl.DeviceIdType.MESH
ANY = pl.BlockSpec(memory_space=pl.ANY)


def _params(*sem):
    return pltpu.CompilerParams(dimension_semantics=sem, vmem_limit_bytes=VMEM_LIMIT)


def _dot(a, b):
    return jnp.dot(a, b, preferred_element_type=F32)


def _dot_nt(a, b):
    return lax.dot_general(a, b, (((1,), (1,)), ((), ())), preferred_element_type=F32)


def _dot_tn(a, b):
    return lax.dot_general(a, b, (((0,), (0,)), ((), ())), preferred_element_type=F32)


def _split_bf16(x):
    hi = x.astype(BF16)
    lo = (x - hi.astype(F32)).astype(BF16)
    return hi, lo


def _relu2(av):
    t = jnp.maximum(av, jnp.zeros_like(av))
    return t * t


def _mm_nn(a, w, l, *, out_dtype, name, res=None, a_act=None, gain=None, sgu=None, comm=None,
           tm=512, kc=1024):
    M, K = a.shape
    _, P, K2, n = w.shape
    assert K2 == K
    tm, kc = min(tm, M), min(kc, K)
    assert M % tm == 0 and K % kc == 0
    join = 2 if n % MXU_TILE and P % 2 == 0 else 1

    n_main = 2 + (gain is not None) + (res is not None)
    n_in = n_main + (3 if sgu is not None else 0)
    n_out = 1 + (gain is not None) + (2 if sgu is not None else 0)
    gelu = sgu is not None

    def body(*all_refs):
        ins, outs, _ = carried.split(all_refs, n_in, n_out, 0)
        carried.emit(all_refs)
        refs = list(ins) + list(outs)
        a_ref, w_ref, o_ref = refs[0], refs[1], refs[n_in]
        if gain is not None:
            xv = a_ref[...]
            r = lax.rsqrt(jnp.mean(xv * xv, axis=-1, keepdims=True) + EPS)
            h = (xv * r * refs[2][...]).astype(BF16)
            refs[n_in + 1][...] = h
        for p in range(0, P, join):
            sl = slice(p * n, (p + join) * n)
            acc = None
            for k0 in range(0, K, kc):
                if gain is not None:
                    av = h[:, k0:k0 + kc]
                else:
                    av = a_ref[:, k0:k0 + kc]
                    av = _relu2(av) if a_act == "relu2" else av.astype(BF16)
                wv = [w_ref[0, p + s, k0:k0 + kc, :] for s in range(join)]
                d = _dot(av, wv[0] if join == 1 else jnp.concatenate(wv, axis=1))
                acc = d if acc is None else acc + d
            if res is not None:
                acc = acc + refs[n_main - 1][:, sl]
            if gelu:
                acc, dact = _gelu(acc)
                refs[-2][:, sl] = dact.astype(BF16)
            o_ref[:, sl] = acc.astype(out_dtype)
        if sgu is not None:
            half = P * n // 2
            for r0 in range(0, tm, SGU_CHUNK):
                rows = slice(r0, r0 + SGU_CHUNK)
                refs[-1][rows, :] = _sgu_gate(o_ref[rows, :half].astype(F32), o_ref[rows, half:].astype(F32),
                                              *refs[n_main:n_in])

    row = lambda width: pl.BlockSpec((tm, width), lambda i: (i, 0))
    whole = lambda arr: pl.BlockSpec(arr.shape, lambda i: (0,) * arr.ndim)
    in_specs = [row(K), pl.BlockSpec((1, P, K, n), lambda i: (l, 0, 0, 0), pipeline_mode=pl.Buffered(1))]
    args = [a, w]
    if gain is not None:
        in_specs.append(whole(gain))
        args.append(gain)
    if res is not None:
        in_specs.append(row(P * n))
        args.append(res)
    out_specs, out_shape = [row(P * n)], [jax.ShapeDtypeStruct((M, P * n), out_dtype)]
    if gain is not None:
        out_specs.append(row(K))
        out_shape.append(jax.ShapeDtypeStruct((M, K), BF16))
    if sgu is not None:
        in_specs += [whole(s) for s in sgu]
        args += list(sgu)
        out_specs += [row(P * n), row(P * n // 2)]
        out_shape += [jax.ShapeDtypeStruct((M, P * n), BF16), jax.ShapeDtypeStruct((M, P * n // 2), BF16)]
    carried = _Carried(comm, (M // tm,), last="parallel")
    outs, comm_results = carried.pallas_call(
        body, in_specs=in_specs, out_specs=out_specs, out_shape=out_shape, scratch_shapes=[],
        args=args, name=name)
    outs = tuple(outs) if n_out > 1 else outs[0]
    return outs if comm is None else (outs, comm_results)


def _mm_nt(a, w, l, *, out_dtype, name, act_src=None, tm=512, tn=1024):
    M, K = a.shape
    _, P, Nout, kc = w.shape
    assert P == 1 and K == kc
    tm, tn = min(tm, M), min(tn, Nout)
    assert M % tm == 0 and Nout % tn == 0

    def body(*refs):
        a_ref, w_ref, o_ref = refs[0], refs[1], refs[-1]
        av = a_ref[...].astype(BF16)
        for n0 in range(0, Nout, tn):
            r = _dot_nt(av, w_ref[0, 0, n0:n0 + tn, :])
            if act_src is not None:
                r = r * (2.0 * jnp.maximum(refs[2][:, n0:n0 + tn].astype(F32), 0.0))
            o_ref[:, n0:n0 + tn] = r.astype(out_dtype)

    row = lambda width: pl.BlockSpec((tm, width), lambda i: (i, 0))
    in_specs = [row(K), pl.BlockSpec((1, 1, Nout, K), lambda i: (l, 0, 0, 0), pipeline_mode=pl.Buffered(1))]
    args = [a, w]
    if act_src is not None:
        in_specs.append(row(Nout))
        args.append(act_src)
    return pl.pallas_call(
        body, grid=(M // tm,), in_specs=in_specs, out_specs=row(Nout),
        out_shape=jax.ShapeDtypeStruct((M, Nout), out_dtype),
        compiler_params=_params("parallel"), name=name)(*args)


def _mm_tn(a, b, *, shards, name, a_act=None, tm=2048, tk=1024, pb=1):
    b_parts = list(b) if isinstance(b, (list, tuple)) else [b]
    M, K = a.shape
    N = sum(part.shape[1] for part in b_parts)
    assert all(part.shape[0] == M for part in b_parts)
    n = N // shards
    tm, tk = min(tm, M), min(tk, K)
    assert M % tm == 0 and K % tk == 0 and shards % pb == 0
    width, nm = pb * n, M // tm
    assert len(b_parts) == 1 or width == N

    def body(*refs):
        a_ref, b_refs, o_ref, acc = refs[0], refs[1:-2], refs[-2], refs[-1]
        m = pl.program_id(2)
        av = a_ref[...]
        if a_act == "relu2":
            av = _relu2(av)
        bv = [r[...].astype(BF16) for r in b_refs]

        @pl.when(m == 0)
        def _():
            acc[...] = jnp.zeros_like(acc)

        acc[...] += _dot_tn(av.astype(BF16), bv[0] if len(bv) == 1 else jnp.concatenate(bv, axis=1))

        @pl.when(m == nm - 1)
        def _():
            for p in range(pb):
                o_ref[p] = acc[:, p * n:(p + 1) * n].astype(BF16)

    if len(b_parts) == 1:
        b_specs = [pl.BlockSpec((tm, width), lambda i, j, m: (m, j))]
    else:
        b_specs = [pl.BlockSpec((tm, part.shape[1]), lambda i, j, m: (m, 0)) for part in b_parts]
    return pl.pallas_call(
        body, grid=(K // tk, N // width, nm),
        in_specs=[pl.BlockSpec((tm, tk), lambda i, j, m: (m, i))] + b_specs,
        out_specs=pl.BlockSpec((pb, tk, n), lambda i, j, m: (j, i, 0)),
        out_shape=jax.ShapeDtypeStruct((shards, K, n), BF16),
        scratch_shapes=[pltpu.VMEM((tk, width), F32)],
        compiler_params=_params("parallel", "parallel", "arbitrary"), name=name)(a, *b_parts)


def _mm_nt_rms_bwd(a, w, l, x, gain, dres, *, name, comm=None, tm=512):
    a_parts = list(a) if isinstance(a, (list, tuple)) else [a]
    na = len(a_parts)
    M, K = a_parts[0].shape[0], sum(part.shape[1] for part in a_parts)
    _, P, D, kc = w.shape
    assert K == P * kc and x.shape == (M, D)
    tm = min(tm, M)
    nr = M // tm
    join = 2 if kc % MXU_TILE and P % 2 == 0 else 1

    def body(*refs):
        ins, (dx_ref, dxb_ref, dg_ref), (acc,) = carried.split(refs, na + 4, 3, 1)
        a_refs, (w_ref, x_ref, g_ref, dres_ref) = ins[:na], ins[na:]
        i = pl.program_id(0)
        carried.emit(refs)
        av = a_refs[0][...] if na == 1 else jnp.concatenate([r[...] for r in a_refs], axis=1)
        dhv = None
        for p in range(0, P, join):
            wv = [w_ref[0, p + s] for s in range(join)]
            d = _dot_nt(av[:, p * kc:(p + join) * kc], wv[0] if join == 1 else jnp.concatenate(wv, axis=1))
            dhv = d if dhv is None else dhv + d
        xv = x_ref[...]
        r = lax.rsqrt(jnp.mean(xv * xv, axis=-1, keepdims=True) + EPS)
        xhat = xv * r
        dxhat = dhv * g_ref[...]
        dx = dres_ref[...] + r * (dxhat - xhat * jnp.mean(dxhat * xhat, axis=-1, keepdims=True))
        dx_ref[...] = dx
        dxb_ref[...] = dx.astype(BF16)
        part = jnp.sum((dhv * xhat).reshape(tm // SUBLANES, SUBLANES, D), axis=0)

        @pl.when(i == 0)
        def _():
            acc[...] = jnp.zeros_like(acc)

        acc[...] += part

        @pl.when(i == nr - 1)
        def _():
            dg_ref[...] = jnp.sum(acc[...], axis=0, keepdims=True)

    row = pl.BlockSpec((tm, D), lambda i: (i, 0))
    vec = pl.BlockSpec((1, D), lambda i: (0, 0))
    carried = _Carried(comm, (nr,))
    return carried.pallas_call(
        body,
        in_specs=[pl.BlockSpec((tm, part.shape[1]), lambda i: (i, 0)) for part in a_parts] + [
            pl.BlockSpec((1, P, D, kc), lambda i: (l, 0, 0, 0), pipeline_mode=pl.Buffered(1)),
            row, vec, row],
        out_specs=[row, row, vec],
        out_shape=[jax.ShapeDtypeStruct((M, D), F32), jax.ShapeDtypeStruct((M, D), BF16),
                   jax.ShapeDtypeStruct((1, D), F32)],
        scratch_shapes=[pltpu.VMEM((SUBLANES, D), F32)],
        args=a_parts + [w, x, gain, dres], name=name)


def _mlp_out_loss_head(a, w, l, res, gain, target, *, name, tr=512, kc=1024):
    T, K = a.shape
    D = w.shape[3]
    nr = T // tr

    def body(a_ref, w_ref, res_ref, g_ref, t_ref, dx_ref, dxb_ref, sq_ref, dg_ref, sq_acc, dg_acc):
        i = pl.program_id(0)
        xv = res_ref[...]
        for k0 in range(0, K, kc):
            xv = xv + _dot(_relu2(a_ref[:, k0:k0 + kc]), w_ref[0, 0, k0:k0 + kc, :])
        g = g_ref[...]
        r = lax.rsqrt(jnp.mean(xv * xv, axis=-1, keepdims=True) + EPS)
        xhat = xv * r
        err = xhat * g - t_ref[...]
        dy = err * (1.0 / D)
        dxhat = dy * g
        dx = r * (dxhat - xhat * jnp.mean(dxhat * xhat, axis=-1, keepdims=True))
        dx_ref[...] = dx
        dxb_ref[...] = dx.astype(BF16)
        sq = jnp.sum((err * err).reshape(tr // SUBLANES, SUBLANES, D), axis=0)
        dg = jnp.sum((dy * xhat).reshape(tr // SUBLANES, SUBLANES, D), axis=0)

        @pl.when(i == 0)
        def _():
            sq_acc[...] = sq
            dg_acc[...] = dg

        @pl.when(i > 0)
        def _():
            sq_acc[...] += sq
            dg_acc[...] += dg

        @pl.when(i == nr - 1)
        def _():
            sq_ref[...] = sq_acc[...]
            dg_ref[...] = jnp.sum(dg_acc[...], axis=0, keepdims=True)

    row = pl.BlockSpec((tr, D), lambda i: (i, 0))
    vec = pl.BlockSpec((1, D), lambda i: (0, 0))
    part = pl.BlockSpec((SUBLANES, D), lambda i: (0, 0))
    return pl.pallas_call(
        body, grid=(nr,),
        in_specs=[pl.BlockSpec((tr, K), lambda i: (i, 0)),
                  pl.BlockSpec((1, 1, K, D), lambda i: (l, 0, 0, 0), pipeline_mode=pl.Buffered(1)),
                  row, vec, row],
        out_specs=[row, row, part, vec],
        out_shape=[jax.ShapeDtypeStruct((T, D), F32), jax.ShapeDtypeStruct((T, D), BF16),
                   jax.ShapeDtypeStruct((SUBLANES, D), F32), jax.ShapeDtypeStruct((1, D), F32)],
        scratch_shapes=[pltpu.VMEM((SUBLANES, D), F32), pltpu.VMEM((SUBLANES, D), F32)],
        compiler_params=_params("arbitrary"), name=name)(a, w, res, gain, target)


def _head0_lanes():
    return lax.broadcasted_iota(jnp.int32, (1, LANES), 1) < SB_HEAD_DIM


def _stack_heads(x):
    zero = jnp.zeros_like(x)
    h0 = _head0_lanes()
    return jnp.concatenate([jnp.where(h0, x, zero), jnp.where(h0, zero, x)], axis=0)


def _unstack_heads(y, tq):
    return jnp.where(_head0_lanes(), y[:tq], y[tq:])


def _past_mask(tq, tk):
    row = lax.broadcasted_iota(jnp.int32, (2 * tq, tk), 0) & (tq - 1)
    col = lax.broadcasted_iota(jnp.int32, (2 * tq, tk), 1)
    return col < row


def _max_of(arrays):
    return functools.reduce(jnp.maximum, [jnp.max(a) for a in arrays])


def _sb_logs(z, past):
    minus_abs = lax.bitcast_convert_type(
        lax.bitcast_convert_type(z, jnp.uint32) | jnp.uint32(0x80000000), F32)
    log_beta = jnp.minimum(z, 0.0) - jnp.log(1.0 + jnp.exp(minus_abs))
    l = log_beta - z
    if past is not None:
        l = jnp.where(past, l, 0.0)
    return log_beta, l


def _suffix_matrix(tk):
    j = lax.broadcasted_iota(jnp.int32, (2 * tk, tk), 0) & (tk - 1)
    s = lax.broadcasted_iota(jnp.int32, (2 * tk, tk), 1)
    return (j > s).astype(BF16)


def _suffix_sum(x, u2, exact=True):
    if not exact:
        return _dot(x.astype(BF16), u2[:x.shape[1]])
    hi, lo = _split_bf16(x)
    return _dot(jnp.concatenate([hi, lo], axis=1), u2)


class _Carried:
    def __init__(self, comm, grid, last="arbitrary"):
        self.comm, self.grid, self.last = comm, grid, last
        self.n_op = len(comm.operands) if comm else 0
        self.n_tgt = len(comm.targets) if comm else 0

    def split(self, refs, n_in, n_out, n_scratch):
        self.n_in, self.n_out, self.n_scratch = n_in, n_out, n_scratch
        a = n_in + self.n_op
        b = a + n_out + self.n_tgt
        return refs[:n_in], refs[a:a + n_out], refs[b:b + n_scratch]

    def emit(self, refs):
        if self.comm is None:
            return
        step, n_steps = 0, 1
        for d, size in enumerate(self.grid):
            step = step * size + pl.program_id(d)
            n_steps *= size
        a = self.n_in + self.n_op + self.n_out
        self.comm.emit(step, n_steps, refs[self.n_in:self.n_in + len(self.comm.sources)],
                       refs[a:a + self.n_tgt], refs[a + self.n_tgt + self.n_scratch:])

    def pallas_call(self, body, *, in_specs, out_specs, out_shape, scratch_shapes, args, name):
        comm = self.comm
        n_in, n_out = len(in_specs), len(out_specs)
        aliases = {}
        if comm is not None:
            in_specs = in_specs + [ANY] * self.n_op
            out_specs = out_specs + [ANY] * self.n_tgt
            out_shape = out_shape + comm.out_shapes
            scratch_shapes = scratch_shapes + comm.sems
            args = args + comm.operands
            aliases = comm.aliases(n_in, n_out)
        sem = ("parallel",) * (len(self.grid) - 1) + (self.last,) if comm is None else ("arbitrary",) * len(self.grid)
        results = pl.pallas_call(
            body, grid=self.grid, in_specs=in_specs, out_specs=out_specs, out_shape=out_shape,
            scratch_shapes=scratch_shapes, input_output_aliases=aliases,
            compiler_params=_params(*sem), name=name)(*args)
        return results[:n_out], results[n_out:]


def _sb_fwd(qkv, *, batch, seq, name, comm=None, tq=SB_TILE, n_pre=SB_STRAIGHT, pairs=SB_PAIRS):
    T, D3 = qkv.shape
    D = D3 // 3
    nhp = D // LANES
    tk = tq
    nq = seq // tq
    scale = SB_HEAD_DIM ** -0.5
    assert 1 <= n_pre <= nq and nhp % pairs == 0

    def body(*refs):
        (q_ref, k_ref, v_ref), (o_ref,), (acc,) = carried.split(refs, 3, 1, 1)
        qi = pl.program_id(2)
        carried.emit(refs)
        qs = [_stack_heads(q_ref[:, lanes]) * scale for lanes in cols]
        past = _past_mask(tq, tk)
        u = _suffix_matrix(tk)

        def block(kb, g, c, diag):
            ks = pl.multiple_of(kb * tk, tk)
            z = _dot_nt(qs[g], k_ref[pl.ds(ks, tk), cols[g]])
            log_beta, l = _sb_logs(z, past if diag else None)
            arg = log_beta + _suffix_sum(l, u, exact=False)
            a = jnp.exp(arg if c is None else arg + c)
            if diag:
                a = jnp.where(past, a, 0.0)
            return _dot(a.astype(BF16), v_ref[pl.ds(ks, tk), cols[g]]), jnp.sum(l, axis=1, keepdims=True)

        def straight(n):
            o_sum, c = [None] * pairs, [None] * pairs
            for b in range(n):
                for g in range(pairs):
                    o_b, c_b = block(qi - b, g, c[g], b == 0)
                    o_sum[g] = o_b if b == 0 else o_sum[g] + o_b
                    c[g] = c_b if b == 0 else c[g] + c_b
            return o_sum, c

        def finish(o_sum):
            for g in range(pairs):
                o_ref[:, cols[g]] = _unstack_heads(o_sum[g], tq)

        for n in range(1, n_pre):
            @pl.when(qi == n - 1)
            def _(n=n):
                finish(straight(n)[0])

        @pl.when(qi >= n_pre - 1)
        def _():
            o_sum, c = straight(n_pre)
            for g in range(pairs):
                acc[g] = o_sum[g]

            def cond(st):
                kb, c = st
                return jnp.logical_and(kb >= 0, _max_of(c) > EXP_ZERO_BELOW)

            def step(st):
                kb, c = st
                new_c = []
                for g in range(pairs):
                    o_n, c_n = block(kb, g, c[g], False)
                    acc[g] += o_n
                    new_c.append(c[g] + c_n)
                return kb - 1, tuple(new_c)

            lax.while_loop(cond, step, (qi - n_pre, tuple(c)))
            finish([acc[g] for g in range(pairs)])

    cols = [slice(g * LANES, (g + 1) * LANES) for g in range(pairs)]
    width = pairs * LANES
    carried = _Carried(comm, (batch, nhp // pairs, nq))
    (o,), comm_results = carried.pallas_call(
        body,
        in_specs=[pl.BlockSpec((tq, width), lambda b, p, i: (b * nq + i, p)),
                  pl.BlockSpec((seq, width), lambda b, p, i: (b, nhp // pairs + p)),
                  pl.BlockSpec((seq, width), lambda b, p, i: (b, 2 * (nhp // pairs) + p))],
        out_specs=[pl.BlockSpec((tq, width), lambda b, p, i: (b * nq + i, p))],
        out_shape=[jax.ShapeDtypeStruct((T, D), F32)],
        scratch_shapes=[pltpu.VMEM((pairs, 2 * tq, LANES), F32)],
        args=[qkv, qkv, qkv], name=name)
    return o, comm_results


def _sb_bwd(qkv, o, do, *, batch, seq, name, comm=None, tq=SB_TILE, n_pre=SB_STRAIGHT, pairs=SB_PAIRS):
    T, D3 = qkv.shape
    D = D3 // 3
    nhp = D // LANES
    tk = tq
    nq = seq // tq
    scale = SB_HEAD_DIM ** -0.5

    def body(*refs):
        ins, outs, scratch = carried.split(refs, 5, 3, 3)
        q_ref, k_ref, v_ref, o_ref, do_ref = ins
        dq_ref, dk_ref, dv_ref = outs
        dq_acc, dk_acc, dv_acc = scratch
        qi = pl.program_id(2)
        carried.emit(refs)

        @pl.when(qi == 0)
        def _():
            dk_acc[...] = jnp.zeros_like(dk_acc)
            dv_acc[...] = jnp.zeros_like(dv_acc)

        qs = [_stack_heads(q_ref[:, lanes]) * scale for lanes in cols]
        dos = [_stack_heads(do_ref[:, lanes]) for lanes in cols]
        dsum = [jnp.sum(_stack_heads(do_ref[:, lanes].astype(F32) * o_ref[:, lanes]), axis=1, keepdims=True)
                for lanes in cols]
        past = _past_mask(tq, tk)
        u = _suffix_matrix(tk)

        def block(kb, p, c, gc, diag):
            ks = pl.multiple_of(kb * tk, tk)
            kblk = k_ref[pl.ds(ks, tk), cols[p]]
            vblk = v_ref[pl.ds(ks, tk), cols[p]]
            z = _dot_nt(qs[p], kblk)
            log_beta, l = _sb_logs(z, past if diag else None)
            arg = log_beta + _suffix_sum(l, u, exact=False)
            a = jnp.exp(arg if c is None else arg + c)
            if diag:
                a = jnp.where(past, a, 0.0)
            a = a.astype(BF16)
            beta = 1.0 - jnp.exp(l)
            g = a.astype(F32) * _dot_nt(dos[p], vblk)
            gs = _suffix_sum(g, u)
            dz = g - beta * (dsum[p] - (gs if gc is None else gs + gc))
            if diag:
                dz = jnp.where(past, dz, 0.0)
            dzb = dz.astype(BF16)
            dk_acc[pl.ds(ks, tk), cols[p]] += _dot_tn(dzb, qs[p])
            dv_acc[pl.ds(ks, tk), cols[p]] += _dot_tn(a, dos[p])
            return (_dot(dzb, kblk), jnp.sum(l, axis=1, keepdims=True),
                    jnp.sum(g, axis=1, keepdims=True))

        def finish(dq_sum):
            for p in range(pairs):
                dq_ref[:, cols[p]] = (_unstack_heads(dq_sum[p], tq) * scale).astype(BF16)

        def straight(n):
            dq_sum, c, gc = [None] * pairs, [None] * pairs, [None] * pairs
            for b in range(n):
                for p in range(pairs):
                    dq_b, c_b, g_b = block(qi - b, p, c[p], gc[p], b == 0)
                    dq_sum[p] = dq_b if b == 0 else dq_sum[p] + dq_b
                    c[p] = c_b if b == 0 else c[p] + c_b
                    gc[p] = g_b if b == 0 else gc[p] + g_b
            return dq_sum, c, gc

        for n in range(1, n_pre):
            @pl.when(qi == n - 1)
            def _(n=n):
                finish(straight(n)[0])

        @pl.when(qi >= n_pre - 1)
        def _():
            dq_sum, c, gc = straight(n_pre)
            for p in range(pairs):
                dq_acc[p] = dq_sum[p]

            def cond(st):
                kb, c, gc = st
                return jnp.logical_and(kb >= 0, _max_of(c) > EXP_ZERO_BELOW)

            def step(st):
                kb, c, gc = st
                new_c, new_gc = [], []
                for p in range(pairs):
                    dq_n, c_n, g_n = block(kb, p, c[p], gc[p], False)
                    dq_acc[p] += dq_n
                    new_c.append(c[p] + c_n)
                    new_gc.append(gc[p] + g_n)
                return kb - 1, tuple(new_c), tuple(new_gc)

            lax.while_loop(cond, step, (qi - n_pre, tuple(c), tuple(gc)))
            finish([dq_acc[p] for p in range(pairs)])

        @pl.when(qi == nq - 1)
        def _():
            dk_ref[...] = dk_acc[...].astype(BF16)
            dv_ref[...] = dv_acc[...].astype(BF16)

    cols = [slice(p * LANES, (p + 1) * LANES) for p in range(pairs)]
    width, ncb = pairs * LANES, nhp // pairs
    qspec = pl.BlockSpec((tq, width), lambda b, p, i: (b * nq + i, p))
    sspec = pl.BlockSpec((seq, width), lambda b, p, i: (b, p))
    out = jax.ShapeDtypeStruct((T, D), BF16)
    carried = _Carried(comm, (batch, ncb, nq))
    return carried.pallas_call(
        body,
        in_specs=[qspec,
                  pl.BlockSpec((seq, width), lambda b, p, i: (b, ncb + p)),
                  pl.BlockSpec((seq, width), lambda b, p, i: (b, 2 * ncb + p)),
                  qspec, qspec],
        out_specs=[qspec, sspec, sspec], out_shape=[out, out, out],
        scratch_shapes=[pltpu.VMEM((pairs, 2 * tq, LANES), F32), pltpu.VMEM((seq, width), F32),
                        pltpu.VMEM((seq, width), F32)],
        args=[qkv, qkv, qkv, o, do], name=name)


_GELU_C = 0.7978845608028654
_GELU_A = 0.044715


def _gelu(x):
    x = x.astype(BF16)
    xx = x * x
    a1 = 1.0 + jnp.tanh(x * (_GELU_C + (_GELU_C * _GELU_A) * xx))
    hx = 0.5 * x
    grad = a1 * (0.5 + (hx * (2.0 - a1)) * (_GELU_C + (3.0 * _GELU_C * _GELU_A) * xx))
    return hx * a1, grad


def _causal_ws(ws_ref, g):
    t = lax.broadcasted_iota(jnp.int32, (SGU_CHUNK, SGU_CHUNK), 0)
    s = lax.broadcasted_iota(jnp.int32, (SGU_CHUNK, SGU_CHUNK), 1)
    return jnp.where(s <= t, ws_ref[g], 0.0)


def _sgu_gate(u, v, g_ref, ws_ref, bs_ref):
    G, W = SGU_GROUPS, SGU_GROUP_W
    r = lax.rsqrt(jnp.mean(v * v, axis=-1, keepdims=True) + EPS)
    vn = (v * r * g_ref[...]).astype(BF16)
    y = []
    for g in range(G):
        sl = slice(g * W, (g + 1) * W)
        mixed = _dot(_causal_ws(ws_ref, g).astype(BF16), vn[:, sl]) + bs_ref[:, g:g + 1]
        y.append((u[:, sl] * mixed).astype(BF16))
    return jnp.concatenate(y, axis=1)


def _sgu_bwd(uv, dgelu, dy, gain, ws, bst, *, name):
    T, F2 = uv.shape
    F = F2 // 2
    C, G, W = SGU_CHUNK, SGU_GROUPS, SGU_GROUP_W
    nc = T // C

    def body(uv_ref, dgelu_ref, dy_ref, g_ref, ws_ref, bs_ref, duv_ref, dg_ref, dws_ref, dbs_ref,
             dg_acc, dws_acc, dbs_acc):
        i = pl.program_id(0)

        @pl.when(i == 0)
        def _():
            dg_acc[...] = jnp.zeros_like(dg_acc)
            dws_acc[...] = jnp.zeros_like(dws_acc)
            dbs_acc[...] = jnp.zeros_like(dbs_acc)

        u, v = uv_ref[:, :F].astype(F32), uv_ref[:, F:].astype(F32)
        dgelu = dgelu_ref[...].astype(F32)
        r = lax.rsqrt(jnp.mean(v * v, axis=-1, keepdims=True) + EPS)
        vhat = v * r
        gain_v = g_ref[...]
        vn = (vhat * gain_v).astype(BF16)
        dyv = dy_ref[...].astype(F32)
        lane8 = lax.broadcasted_iota(jnp.int32, (1, G), 1)
        dvn_parts = []
        dbs_new = jnp.zeros((C, G), F32)
        for g in range(G):
            sl = slice(g * W, (g + 1) * W)
            wsg = _causal_ws(ws_ref, g)
            mixed = _dot(wsg.astype(BF16), vn[:, sl]) + bs_ref[:, g:g + 1]
            duv_ref[:, sl] = (dyv[:, sl] * mixed * dgelu[:, sl]).astype(BF16)
            dmix = dyv[:, sl] * u[:, sl]
            dbs_new = dbs_new + jnp.where(lane8 == g, jnp.sum(dmix, axis=1, keepdims=True), 0.0)
            dmix_b = dmix.astype(BF16)
            dws_acc[g] += _dot_nt(dmix_b, vn[:, sl])
            dvn_parts.append(_dot(wsg.T.astype(BF16), dmix_b))
        dbs_acc[...] += dbs_new
        dvn = jnp.concatenate(dvn_parts, axis=1)
        dg_acc[...] += jnp.sum((dvn * vhat).reshape(C // SUBLANES, SUBLANES, F), axis=0)
        dvhat = dvn * gain_v
        dv = r * (dvhat - vhat * jnp.mean(dvhat * vhat, axis=-1, keepdims=True))
        duv_ref[:, F:] = (dv * dgelu[:, F:]).astype(BF16)

        @pl.when(i == nc - 1)
        def _():
            dg_ref[...] = jnp.sum(dg_acc[...], axis=0, keepdims=True)
            t = lax.broadcasted_iota(jnp.int32, (G, C, C), 1)
            s = lax.broadcasted_iota(jnp.int32, (G, C, C), 2)
            dws_ref[...] = jnp.where(s <= t, dws_acc[...], 0.0)
            dbs_ref[...] = dbs_acc[...]

    return pl.pallas_call(
        body, grid=(nc,),
        in_specs=[pl.BlockSpec((C, F2), lambda i: (i, 0)), pl.BlockSpec((C, F2), lambda i: (i, 0)),
                  pl.BlockSpec((C, F), lambda i: (i, 0)),
                  pl.BlockSpec((1, F), lambda i: (0, 0)), pl.BlockSpec((G, C, C), lambda i: (0, 0, 0)),
                  pl.BlockSpec((C, G), lambda i: (0, 0))],
        out_specs=[pl.BlockSpec((C, F2), lambda i: (i, 0)), pl.BlockSpec((1, F), lambda i: (0, 0)),
                   pl.BlockSpec((G, C, C), lambda i: (0, 0, 0)), pl.BlockSpec((C, G), lambda i: (0, 0))],
        out_shape=[jax.ShapeDtypeStruct((T, F2), BF16), jax.ShapeDtypeStruct((1, F), F32),
                   jax.ShapeDtypeStruct((G, C, C), F32), jax.ShapeDtypeStruct((C, G), F32)],
        scratch_shapes=[pltpu.VMEM((SUBLANES, F), F32), pltpu.VMEM((G, C, C), F32), pltpu.VMEM((C, G), F32)],
        compiler_params=_params("arbitrary"), name=name)(uv, dgelu, dy, gain, ws, bst)


def _my_place():
    return lax.axis_index("x"), lax.axis_index("y"), lax.axis_index("c")


def _all_gather(shards, *, name):
    nf = len(shards)
    items = [(lambda ins, f=f: ins[f], lambda outs, p, f=f: outs[f].at[:, p]) for f in range(nf)]
    targets = [jax.ShapeDtypeStruct((s.shape[0], N_DEV) + s.shape[1:], s.dtype) for s in shards]
    return _run_comm(_Comm(shards, targets, len(items), _gather_emit(items, [list(range(nf))])), name=name)


class _Comm:
    def __init__(self, sources, targets, n_items, emit):
        self.sources, self.targets, self.n_items, self.emit = list(sources), list(targets), n_items, emit
        self.filled = [t for t in self.targets if not isinstance(t, jax.ShapeDtypeStruct)]
        self.operands = self.sources + self.filled
        self.out_shapes = [jax.ShapeDtypeStruct(t.shape, t.dtype) for t in self.targets]
        self.sems = [pltpu.SemaphoreType.DMA((n_items, 7)), pltpu.SemaphoreType.DMA((n_items, 7)),
                     pltpu.SemaphoreType.DMA((n_items,))]

    def aliases(self, first_operand, first_result):
        pos = {id(t): k for k, t in enumerate(self.targets)}
        return {first_operand + len(self.sources) + a: first_result + pos[id(t)]
                for a, t in enumerate(self.filled)}


def _run_comm(comm, *, name):
    n_op, n_out = len(comm.operands), len(comm.targets)

    def body(*refs):
        comm.emit(0, 1, refs[:len(comm.sources)], refs[n_op:n_op + n_out], refs[n_op + n_out:])

    return pl.pallas_call(
        body, in_specs=[ANY] * n_op, out_specs=[ANY] * n_out, out_shape=comm.out_shapes,
        scratch_shapes=comm.sems, input_output_aliases=comm.aliases(0, 0), name=name)(*comm.operands)


def _at_steps(step, n_steps, phases):
    if n_steps == 1:
        for _, fn in phases:
            fn()
        return
    marks = {}
    for frac, fn in phases:
        marks.setdefault(min(int(frac * n_steps), n_steps - 1), []).append(fn)
    for mark, fns in sorted(marks.items()):
        @pl.when(step == mark)
        def _(fns=fns):
            for fn in fns:
                fn()


def _gather_emit(items, groups, fractions=None):
    if fractions is None:
        fractions = [(g + 1) / len(groups) for g in range(len(groups))]
    def emit(step, n_steps, ins, outs, sems):
        send_sems, recv_sems, local_sems = sems
        x, y, c = _my_place()
        me, sibling = (x, y, c), (x, y, 1 - c)
        chips = [(1 - x, y), (x, 1 - y), (1 - x, 1 - y)]

        def copy(i, k, block, to, own=False):
            src_of, dst_of = items[i]
            dst = dst_of(outs, 4 * block[0] + 2 * block[1] + block[2])
            return pltpu.make_async_remote_copy(
                src_ref=src_of(ins) if own else dst, dst_ref=dst,
                send_sem=send_sems.at[i, k], recv_sem=recv_sems.at[i, k],
                device_id=to, device_id_type=MESH)

        def local(i):
            src_of, dst_of = items[i]
            return pltpu.make_async_copy(src_of(ins), dst_of(outs, 4 * x + 2 * y + c), local_sems.at[i])

        def first(i):
            return [copy(i, 0, me, sibling, own=True)] + [
                copy(i, 1 + j, me, (*chip, c), own=True) for j, chip in enumerate(chips)]

        def start():
            for i in range(len(items)):
                local(i).start()
                for cp in first(i):
                    cp.start()

        def forward(group):
            for j, chip in enumerate(chips):
                for i in group:
                    copy(i, 1 + j, (*chip, c), me).wait_recv()
                    copy(i, 4 + j, (*chip, c), sibling).start()

        def finish():
            for i in range(len(items)):
                copy(i, 0, sibling, me).wait_recv()
                for j, chip in enumerate(chips):
                    copy(i, 4 + j, (*chip, 1 - c), me).wait_recv()
            for i in range(len(items)):
                for cp in first(i) + [copy(i, 4 + j, (*chip, c), sibling) for j, chip in enumerate(chips)]:
                    cp.wait_send()
                local(i).wait()

        phases = [(0.0, start)]
        for frac, group in zip(fractions, groups):
            phases.append((frac, functools.partial(forward, group)))
        phases.append((1.0, finish))
        _at_steps(step, n_steps, phases)

    return emit


def _exchange_emit(items):
    def emit(step, n_steps, ins, outs, sems):
        send_sems, recv_sems, local_sems = sems
        x, y, c = _my_place()
        me = 4 * x + 2 * y + c

        def peer_of(k):
            return x ^ (k >> 2), y ^ ((k >> 1) & 1), c ^ (k & 1)

        def copy(i, k):
            src_of, dst_of = items[i]
            px, py, pc = peer_of(k)
            return pltpu.make_async_remote_copy(
                src_ref=src_of(ins, 4 * px + 2 * py + pc), dst_ref=dst_of(outs, me),
                send_sem=send_sems.at[i, k - 1], recv_sem=recv_sems.at[i, k - 1],
                device_id=(px, py, pc), device_id_type=MESH)

        def arrival(i, k):
            src_of, dst_of = items[i]
            px, py, pc = peer_of(k)
            peer = 4 * px + 2 * py + pc
            return pltpu.make_async_remote_copy(
                src_ref=src_of(ins, peer), dst_ref=dst_of(outs, peer),
                send_sem=send_sems.at[i, k - 1], recv_sem=recv_sems.at[i, k - 1],
                device_id=(x, y, c), device_id_type=MESH)

        def local(i):
            src_of, dst_of = items[i]
            return pltpu.make_async_copy(src_of(ins, me), dst_of(outs, me), local_sems.at[i])

        def start():
            for i in range(len(items)):
                local(i).start()
            for k in range(1, N_DEV):
                for i in range(len(items)):
                    copy(i, k).start()

        def finish():
            for k in range(1, N_DEV):
                for i in range(len(items)):
                    arrival(i, k).wait_recv()
            for k in range(1, N_DEV):
                for i in range(len(items)):
                    copy(i, k).wait_send()
            for i in range(len(items)):
                local(i).wait()

        _at_steps(step, n_steps, [(0.0, start), (1.0, finish)])

    return emit


def _adam_math(g, w, m, v):
    m = ADAM_B1 * m + (1.0 - ADAM_B1) * g
    v = ADAM_B2 * v + (1.0 - ADAM_B2) * (g * g)
    m_hat = m / (1.0 - ADAM_B1 ** ADAM_STEP)
    v_hat = v / (1.0 - ADAM_B2 ** ADAM_STEP)
    delta = -ADAM_LR * (m_hat / (jnp.sqrt(v_hat) + ADAM_EPS) + ADAM_WD * w)
    return delta, m, v


def _sum_adamw(parts, w, m, v, *, name, tr=256):
    L, nd, R, C = parts.shape
    tr = min(tr, R)
    assert R % tr == 0

    def body(p_ref, w_ref, m_ref, v_ref, g_ref, d_ref, nm_ref, nv_ref):
        g = p_ref[0, 0].astype(F32)
        for q in range(1, nd):
            g = g + p_ref[0, q].astype(F32)
        d, nm, nv = _adam_math(g, w_ref[0], m_ref[0], v_ref[0])
        g_ref[0] = g
        d_ref[0] = d
        nm_ref[0] = nm
        nv_ref[0] = nv

    blk = pl.BlockSpec((1, tr, C), lambda l, i: (l, i, 0))
    out = jax.ShapeDtypeStruct((L, R, C), F32)
    return pl.pallas_call(
        body, grid=(L, R // tr),
        in_specs=[pl.BlockSpec((1, nd, tr, C), lambda l, i: (l, 0, i, 0)), blk, blk, blk],
        out_specs=[blk] * 4, out_shape=[out] * 4,
        compiler_params=_params("parallel", "parallel"), name=name)(parts, w, m, v)


def _sum_parts(parts, *, name):
    nd, R, C = parts.shape

    def body(p_ref, o_ref):
        g = p_ref[0]
        for q in range(1, nd):
            g = g + p_ref[q]
        o_ref[...] = g

    return pl.pallas_call(
        body, out_shape=jax.ShapeDtypeStruct((R, C), F32),
        in_specs=[pl.BlockSpec(memory_space=pltpu.VMEM)],
        out_specs=pl.BlockSpec(memory_space=pltpu.VMEM), name=name)(parts)


def _adamw_small(g, w, m, v, *, name):
    def body(g_ref, w_ref, m_ref, v_ref, d_ref, nm_ref, nv_ref):
        d, nm, nv = _adam_math(g_ref[...], w_ref[...], m_ref[...], v_ref[...])
        d_ref[...] = d
        nm_ref[...] = nm
        nv_ref[...] = nv

    vm = pl.BlockSpec(memory_space=pltpu.VMEM)
    out = jax.ShapeDtypeStruct(g.shape, F32)
    return pl.pallas_call(body, out_shape=[out] * 3, in_specs=[vm] * 4, out_specs=[vm] * 3,
                          name=name)(g, w, m, v)


def kernel(x, norm_mix, norm_mlp, sb_wqkv, sb_wo, sgu_win, sgu_gain, sgu_ws, sgu_bs, sgu_wout, mlp_w1, mlp_w2, final_norm, loss_target, m_norm_mix, m_norm_mlp, m_sb_wqkv, m_sb_wo, m_sgu_win, m_sgu_gain, m_sgu_ws, m_sgu_bs, m_sgu_wout, m_mlp_w1, m_mlp_w2, m_final_norm, v_norm_mix, v_norm_mlp, v_sb_wqkv, v_sb_wo, v_sgu_win, v_sgu_gain, v_sgu_ws, v_sgu_bs, v_sgu_wout, v_mlp_w1, v_mlp_w2, v_final_norm):
    batch, seq, D = x.shape
    T = batch * seq
    x0 = x.reshape(T, D)
    target = loss_target.reshape(T, D)

    WQKV, WO, WIN, WOUT, W1, W2, GAIN = range(7)
    big = [sb_wqkv, sb_wo, sgu_win, sgu_wout, mlp_w1, mlp_w2]
    shards = [w.astype(BF16) for w in big] + [sgu_gain[:, None, :]]
    rides = {
        "qkv0": [[(WO, 0), (W1, 0)]],
        "sb_fwd0": [[(WIN, 0), (WOUT, 0), (W1, 1), (W2, 1)], [(WQKV, 1)]],
        "w1_0": [[(W2, 0)]],
        "qkv2": [[(WO, 1), (W1, 2)]],
        "sb_fwd2": [[(W2, 2)], [(WIN, 1), (WOUT, 1), (W1, 3), (W2, 3)]],
    }

    first = [(lambda ins: ins[WQKV].at[0], lambda outs, p: outs[WQKV].at[0, p]),
             (lambda ins: ins[GAIN], lambda outs, p: outs[GAIN].at[:, p])]
    targets = [jax.ShapeDtypeStruct((s.shape[0], N_DEV) + s.shape[1:], s.dtype) for s in shards]
    gathered0 = _run_comm(_Comm(shards, targets, len(first), _gather_emit(first, [[0, 1]])), name="gather_first")
    gain_sgu = gathered0[GAIN].reshape(-1, 1, SGU_FFN)
    gw = dict(enumerate(gathered0[:GAIN]))

    def gather_ride(call):
        wanted = [fl for group in rides[call] for fl in group]
        fams = sorted({f for f, _ in wanted})
        items = [(lambda ins, f=f, l=l: ins[f].at[l], lambda outs, p, t=fams.index(f), l=l: outs[t].at[l, p])
                 for f, l in wanted]
        groups, sent, k = [], [], 0
        for group in rides[call]:
            groups.append(list(range(k, k + len(group))))
            sent.append(sum(shards[f].shape[1] * shards[f].shape[2] for f, _ in group) + (sent[-1] if sent else 0))
            k += len(group)
        fractions = [s / sent[-1] for s in sent]
        return _Comm(shards[:GAIN], [gw[f] for f in fams], len(items), _gather_emit(items, groups, fractions)), fams

    def weight(f):
        g = gw[f]
        return g if f in (WQKV, WIN, W1) else g.reshape(g.shape[0], 1, N_DEV * g.shape[2], g.shape[3])

    saved = []
    xs = x0
    for i in range(DEPTH):
        j = i // 2
        if i % 2 == 0:
            comm, fams = gather_ride(f"qkv{i}")
            (qkv, h), results = _mm_nn(xs, weight(WQKV), j, out_dtype=BF16, gain=norm_mix[i:i + 1],
                                       name=f"qkv{i}", comm=comm, tm=1024)
            gw.update(zip(fams, results))
            comm, fams = gather_ride(f"sb_fwd{i}")
            o, results = _sb_fwd(qkv, batch=batch, seq=seq, name=f"sb_fwd{i}", comm=comm)
            gw.update(zip(fams, results))
            x_mid = _mm_nn(o, weight(WO), j, out_dtype=F32, res=xs, name=f"wo{i}", tm=1024)
            mix = (h, qkv, o)
        else:
            gain_j = gain_sgu[j]
            bst = sgu_bs[j].T
            uv, h, dgelu, yv = _mm_nn(xs, weight(WIN), j, out_dtype=BF16, gain=norm_mix[i:i + 1],
                                      sgu=(gain_j, sgu_ws[j], bst), name=f"win_sgu{i}")
            x_mid = _mm_nn(yv, weight(WOUT), j, out_dtype=F32, res=xs, name=f"wout{i}", tm=1024)
            mix = (h, uv, dgelu, yv, gain_j, bst)
        if f"w1_{i}" in rides:
            comm, fams = gather_ride(f"w1_{i}")
            (a, h2), results = _mm_nn(x_mid, weight(W1), i, out_dtype=BF16, gain=norm_mlp[i:i + 1],
                                      name=f"w1_{i}", comm=comm, tm=1024)
            gw.update(zip(fams, results))
        else:
            a, h2 = _mm_nn(x_mid, weight(W1), i, out_dtype=BF16, gain=norm_mlp[i:i + 1], name=f"w1_{i}", tm=1024)
        saved.append((xs, mix, x_mid, h2, a))
        if i < DEPTH - 1:
            xs = _mm_nn(a, weight(W2), i, out_dtype=F32, res=x_mid, a_act="relu2", name=f"w2_{i}")
    g_wqkv, g_wo, g_win, g_wout, g_w1, g_w2 = [weight(f) for f in range(GAIN)]

    dx, dxb, sq, d_final = _mlp_out_loss_head(a, g_w2, DEPTH - 1, x_mid, final_norm.reshape(1, D), target,
                                              name="w2_loss_head")
    loss = lax.psum(0.5 * jnp.sum(sq) / D, ("x", "y", "c"))

    SMALL = GAIN
    stacks = {f: jax.ShapeDtypeStruct((w.shape[0], N_DEV) + w.shape[1:], BF16) for f, w in enumerate(big)}
    pending = []
    second_half = []

    def row_shards(p):
        return p.reshape(N_DEV, p.shape[1] // N_DEV, p.shape[2])

    def exchange(going):
        fams = sorted({f for _, f, _, _ in going})

        def item(a, f, l, rows):
            rows = slice(None) if rows is None else slice(*rows)
            return (lambda ins, p: ins[a].at[p, rows], lambda outs, q: outs[fams.index(f)].at[l, q, rows])

        items = [item(a, f, l, rows) for a, (_, f, l, rows) in enumerate(going)]
        return _Comm([g[0] for g in going], [stacks[f] for f in fams], len(items), _exchange_emit(items)), fams

    def take(entries):
        going = list(entries)
        entries.clear()
        return going

    d_norm_mix, d_norm_mlp = [None] * DEPTH, [None] * DEPTH
    d_gain, d_ws, d_bs = [None] * 2, [None] * 2, [None] * 2
    for i in reversed(range(DEPTH)):
        j = i // 2
        xs, mix, x_mid, h2, a = saved[i]
        da = _mm_nt(dxb, g_w2, i, out_dtype=BF16, act_src=a, name=f"d_a{i}")
        pending.append((row_shards(_mm_tn(a, dxb, shards=1, a_act="relu2", name=f"d_w2_{i}")), W2, i, None))
        pending.append((_mm_tn(h2, da, shards=N_DEV, name=f"d_w1_{i}", pb=4, tm=1024), W1, i, None))
        comm, fams = exchange(take(second_half)) if second_half else (None, [])
        (dx, dxb, d_norm_mlp[i]), results = _mm_nt_rms_bwd(
            da, g_w1, i, x_mid, norm_mlp[i:i + 1], dx, name=f"d_h2_{i}", comm=comm)
        stacks.update(zip(fams, results))
        if i % 2 == 0:
            h, qkv, o = mix
            do = _mm_nt(dxb, g_wo, j, out_dtype=BF16, name=f"d_o{i}", tm=1024)
            pending.append((row_shards(_mm_tn(o, dxb, shards=1, name=f"d_wo{i}")), WO, j, None))
            comm, fams = exchange(take(pending))
            (dq, dk, dv), results = _sb_bwd(qkv, o, do, batch=batch, seq=seq, name=f"sb_bwd{i}", comm=comm)
            stacks.update(zip(fams, results))
            mixer_in, w_in, f_in = [dq, dk, dv], g_wqkv, WQKV
            d_w_in = _mm_tn(h, mixer_in, shards=N_DEV, name=f"d_wqkv{i}", pb=N_DEV, tm=1024, tk=512)
        else:
            h, uv, dgelu, yv, gain_j, bst = mix
            dy = _mm_nt(dxb, g_wout, j, out_dtype=BF16, name=f"d_y{i}", tm=1024)
            pending.append((row_shards(_mm_tn(yv, dxb, shards=1, name=f"d_wout{i}")), WOUT, j, None))
            mixer_in, d_gain[j], d_ws[j], dbst = _sgu_bwd(uv, dgelu, dy, gain_j, sgu_ws[j], bst, name=f"sgu_bwd{i}")
            d_bs[j] = dbst.T
            w_in, f_in = g_win, WIN
            d_w_in = _mm_tn(h, mixer_in, shards=N_DEV, name=f"d_win{i}", pb=4, tm=1024)
        rows = d_w_in.shape[1]
        if i >= 2:
            pending.append((d_w_in, f_in, j, None))
            comm, fams = None, []
        elif i == 1:
            second_half.append((d_w_in, f_in, j, (rows // 2, rows)))
            comm, fams = exchange([(d_w_in, f_in, j, (0, rows // 2))])
        else:
            comm, fams = exchange([(d_w_in, f_in, j, None)])
        (dx, dxb, d_norm_mix[i]), results = _mm_nt_rms_bwd(
            mixer_in, w_in, j, xs, norm_mix[i:i + 1], dx, name=f"d_h_mix{i}", comm=comm)
        stacks.update(zip(fams, results))
    grad_x = dx.reshape(batch, seq, D)

    small = [jnp.concatenate(d_norm_mix, 0), jnp.concatenate(d_norm_mlp, 0), d_final,
             jnp.concatenate(d_gain, 0), jnp.stack(d_bs, 0), jnp.stack(d_ws, 0)]
    small_flat = jnp.concatenate([s.reshape(-1) for s in small])
    n_small = small_flat.shape[0]
    small_rows = -(-n_small // (N_DEV * SUBLANES * LANES)) * SUBLANES
    small_flat = jnp.pad(small_flat, (0, N_DEV * small_rows * LANES - n_small))
    stacks[SMALL] = jax.ShapeDtypeStruct((1, N_DEV, small_rows, LANES), F32)
    pending.append((small_flat.reshape(N_DEV, small_rows, LANES), SMALL, 0, None))
    comm, fams = exchange(take(pending))
    stacks.update(zip(fams, _run_comm(comm, name="exchange_last")))
    r_wqkv, r_wo, r_win, r_wout, r_w1, r_w2, r_small = [stacks[f] for f in range(SMALL + 1)]

    u_wqkv = _sum_adamw(r_wqkv, sb_wqkv, m_sb_wqkv, v_sb_wqkv, name="adamw_wqkv")
    u_wo = _sum_adamw(r_wo, sb_wo, m_sb_wo, v_sb_wo, name="adamw_wo")
    u_win = _sum_adamw(r_win, sgu_win, m_sgu_win, v_sgu_win, name="adamw_win")
    u_wout = _sum_adamw(r_wout, sgu_wout, m_sgu_wout, v_sgu_wout, name="adamw_wout")
    u_w1 = _sum_adamw(r_w1, mlp_w1, m_mlp_w1, v_mlp_w1, name="adamw_w1")
    u_w2 = _sum_adamw(r_w2, mlp_w2, m_mlp_w2, v_mlp_w2, name="adamw_w2")

    small_sum = _sum_parts(r_small[0], name="sum_small")
    g_small = _all_gather([small_sum[None]], name="gather_small")[0].reshape(-1)[:n_small]

    shapes = [s.shape for s in small]
    sizes = [s.size for s in small]
    offs = [sum(sizes[:k]) for k in range(len(sizes))]
    me = 4 * lax.axis_index("x") + 2 * lax.axis_index("y") + lax.axis_index("c")
    shard_w = SGU_FFN // N_DEV

    def pack(arrs):
        flat = jnp.concatenate([a_.reshape(-1) for a_ in arrs])
        return jnp.pad(flat, (0, N_DEV * small_rows * LANES - n_small)).reshape(-1, LANES)

    def full_gain(gshard):
        return lax.dynamic_update_slice(jnp.zeros((2, SGU_FFN), F32), gshard, (0, me * shard_w))

    w_small = pack([norm_mix, norm_mlp, final_norm, full_gain(sgu_gain), sgu_bs, sgu_ws])
    m_small = pack([m_norm_mix, m_norm_mlp, m_final_norm, full_gain(m_sgu_gain), m_sgu_bs, m_sgu_ws])
    v_small = pack([v_norm_mix, v_norm_mlp, v_final_norm, full_gain(v_sgu_gain), v_sgu_bs, v_sgu_ws])
    g_pack = jnp.pad(g_small, (0, N_DEV * small_rows * LANES - n_small)).reshape(-1, LANES)
    sm = [g_pack] + list(_adamw_small(g_pack, w_small, m_small, v_small, name="adamw_small"))

    def unpack(flat2d):
        flat = flat2d.reshape(-1)
        out = [flat[offs[k]:offs[k] + sizes[k]].reshape(shapes[k]) for k in range(len(sizes))]
        out[2] = out[2].reshape(D)
        out[3] = lax.dynamic_slice(out[3], (0, me * shard_w), (2, shard_w))
        return out

    outs = []
    for k, big_u in enumerate(zip(u_wqkv, u_wo, u_win, u_wout, u_w1, u_w2)):
        s_nm, s_nl, s_fn, s_gain, s_bs, s_ws = unpack(sm[k])
        b_wqkv, b_wo, b_win, b_wout, b_w1, b_w2 = big_u
        outs += [s_nm, s_nl, b_wqkv, b_wo, b_win, s_gain, s_ws, s_bs, b_wout, b_w1, b_w2, s_fn]
    return (loss, grad_x, *outs)
```

```python
import functools

import jax
import jax.numpy as jnp
from jax import lax
from jax.experimental import pallas as pl
from jax.experimental.pallas import tpu as pltpu

F32 = jnp.float32
BF16 = jnp.bfloat16

N_DEV = 8
D_MODEL = 1024
SEQ = 2048
DEPTH = 4
SB_HEAD_DIM = 64
SGU_CHUNK = 128
SGU_FFN = 2 * D_MODEL
SGU_GROUPS = 8
SGU_GROUP_W = SGU_FFN // SGU_GROUPS
EPS = 1e-6

ADAM_LR = 0.001
ADAM_B1 = 0.9
ADAM_B2 = 0.999
ADAM_EPS = 1e-08
ADAM_WD = 0.01
ADAM_STEP = 10

MXU_TILE = 256
LANES = 128
SUBLANES = 8
VMEM_LIMIT = 56 * 1024 * 1024
EXP_ZERO_BELOW = -104.0
SB_TILE = 256
SB_STRAIGHT = 2
SB_PAIRS = 4

MESH = pl.DeviceIdType.MESH
ANY = pl.BlockSpec(memory_space=pl.ANY)


def _params(*sem):
    return pltpu.CompilerParams(dimension_semantics=sem, vmem_limit_bytes=VMEM_LIMIT)


def _dot(a, b):
    return jnp.dot(a, b, preferred_element_type=F32)


def _dot_nt(a, b):
    return lax.dot_general(a, b, (((1,), (1,)), ((), ())), preferred_element_type=F32)


def _dot_tn(a, b):
    return lax.dot_general(a, b, (((0,), (0,)), ((), ())), preferred_element_type=F32)


def _split_bf16(x):
    hi = x.astype(BF16)
    lo = (x - hi.astype(F32)).astype(BF16)
    return hi, lo


def _relu2(av):
    t = jnp.maximum(av, jnp.zeros_like(av))
    return t * t


def _mm_nn(a, w, l, *, out_dtype, name, res=None, a_act=None, gain=None, sgu=None, comm=None,
           tm=512, kc=1024):
    M, K = a.shape
    _, P, K2, n = w.shape
    assert K2 == K
    tm, kc = min(tm, M), min(kc, K)
    assert M % tm == 0 and K % kc == 0
    join = 2 if n % MXU_TILE and P % 2 == 0 else 1

    n_main = 2 + (gain is not None) + (res is not None)
    n_in = n_main + (3 if sgu is not None else 0)
    n_out = 1 + (gain is not None) + (2 if sgu is not None else 0)
    gelu = sgu is not None

    def body(*all_refs):
        ins, outs, _ = carried.split(all_refs, n_in, n_out, 0)
        carried.emit(all_refs)
        refs = list(ins) + list(outs)
        a_ref, w_ref, o_ref = refs[0], refs[1], refs[n_in]
        if gain is not None:
            xv = a_ref[...]
            r = lax.rsqrt(jnp.mean(xv * xv, axis=-1, keepdims=True) + EPS)
            h = (xv * r * refs[2][...]).astype(BF16)
            refs[n_in + 1][...] = h
        for p in range(0, P, join):
            sl = slice(p * n, (p + join) * n)
            acc = None
            for k0 in range(0, K, kc):
                if gain is not None:
                    av = h[:, k0:k0 + kc]
                else:
                    av = a_ref[:, k0:k0 + kc]
                    av = _relu2(av) if a_act == "relu2" else av.astype(BF16)
                wv = [w_ref[0, p + s, k0:k0 + kc, :] for s in range(join)]
                d = _dot(av, wv[0] if join == 1 else jnp.concatenate(wv, axis=1))
                acc = d if acc is None else acc + d
            if res is not None:
                acc = acc + refs[n_main - 1][:, sl]
            if gelu:
                acc, dact = _gelu(acc)
                refs[-2][:, sl] = dact.astype(BF16)
            o_ref[:, sl] = acc.astype(out_dtype)
        if sgu is not None:
            half = P * n // 2
            for r0 in range(0, tm, SGU_CHUNK):
                rows = slice(r0, r0 + SGU_CHUNK)
                refs[-1][rows, :] = _sgu_gate(o_ref[rows, :half].astype(F32), o_ref[rows, half:].astype(F32),
                                              *refs[n_main:n_in])

    row = lambda width: pl.BlockSpec((tm, width), lambda i: (i, 0))
    whole = lambda arr: pl.BlockSpec(arr.shape, lambda i: (0,) * arr.ndim)
    in_specs = [row(K), pl.BlockSpec((1, P, K, n), lambda i: (l, 0, 0, 0), pipeline_mode=pl.Buffered(1))]
    args = [a, w]
    if gain is not None:
        in_specs.append(whole(gain))
        args.append(gain)
    if res is not None:
        in_specs.append(row(P * n))
        args.append(res)
    out_specs, out_shape = [row(P * n)], [jax.ShapeDtypeStruct((M, P * n), out_dtype)]
    if gain is not None:
        out_specs.append(row(K))
        out_shape.append(jax.ShapeDtypeStruct((M, K), BF16))
    if sgu is not None:
        in_specs += [whole(s) for s in sgu]
        args += list(sgu)
        out_specs += [row(P * n), row(P * n // 2)]
        out_shape += [jax.ShapeDtypeStruct((M, P * n), BF16), jax.ShapeDtypeStruct((M, P * n // 2), BF16)]
    carried = _Carried(comm, (M // tm,), last="parallel")
    outs, comm_results = carried.pallas_call(
        body, in_specs=in_specs, out_specs=out_specs, out_shape=out_shape, scratch_shapes=[],
        args=args, name=name)
    outs = tuple(outs) if n_out > 1 else outs[0]
    return outs if comm is None else (outs, comm_results)


def _mm_nt(a, w, l, *, out_dtype, name, act_src=None, tm=512, tn=1024):
    M, K = a.shape
    _, P, Nout, kc = w.shape
    assert P == 1 and K == kc
    tm, tn = min(tm, M), min(tn, Nout)
    assert M % tm == 0 and Nout % tn == 0

    def body(*refs):
        a_ref, w_ref, o_ref = refs[0], refs[1], refs[-1]
        av = a_ref[...].astype(BF16)
        for n0 in range(0, Nout, tn):
            r = _dot_nt(av, w_ref[0, 0, n0:n0 + tn, :])
            if act_src is not None:
                r = r * (2.0 * jnp.maximum(refs[2][:, n0:n0 + tn].astype(F32), 0.0))
            o_ref[:, n0:n0 + tn] = r.astype(out_dtype)

    row = lambda width: pl.BlockSpec((tm, width), lambda i: (i, 0))
    in_specs = [row(K), pl.BlockSpec((1, 1, Nout, K), lambda i: (l, 0, 0, 0), pipeline_mode=pl.Buffered(1))]
    args = [a, w]
    if act_src is not None:
        in_specs.append(row(Nout))
        args.append(act_src)
    return pl.pallas_call(
        body, grid=(M // tm,), in_specs=in_specs, out_specs=row(Nout),
        out_shape=jax.ShapeDtypeStruct((M, Nout), out_dtype),
        compiler_params=_params("parallel"), name=name)(*args)


def _mm_tn(a, b, *, shards, name, a_act=None, tm=2048, tk=1024, pb=1):
    b_parts = list(b) if isinstance(b, (list, tuple)) else [b]
    M, K = a.shape
    N = sum(part.shape[1] for part in b_parts)
    assert all(part.shape[0] == M for part in b_parts)
    n = N // shards
    tm, tk = min(tm, M), min(tk, K)
    assert M % tm == 0 and K % tk == 0 and shards % pb == 0
    width, nm = pb * n, M // tm
    assert len(b_parts) == 1 or width == N

    def body(*refs):
        a_ref, b_refs, o_ref, acc = refs[0], refs[1:-2], refs[-2], refs[-1]
        m = pl.program_id(2)
        av = a_ref[...]
        if a_act == "relu2":
            av = _relu2(av)
        bv = [r[...].astype(BF16) for r in b_refs]

        @pl.when(m == 0)
        def _():
            acc[...] = jnp.zeros_like(acc)

        acc[...] += _dot_tn(av.astype(BF16), bv[0] if len(bv) == 1 else jnp.concatenate(bv, axis=1))

        @pl.when(m == nm - 1)
        def _():
            for p in range(pb):
                o_ref[p] = acc[:, p * n:(p + 1) * n].astype(BF16)

    if len(b_parts) == 1:
        b_specs = [pl.BlockSpec((tm, width), lambda i, j, m: (m, j))]
    else:
        b_specs = [pl.BlockSpec((tm, part.shape[1]), lambda i, j, m: (m, 0)) for part in b_parts]
    return pl.pallas_call(
        body, grid=(K // tk, N // width, nm),
        in_specs=[pl.BlockSpec((tm, tk), lambda i, j, m: (m, i))] + b_specs,
        out_specs=pl.BlockSpec((pb, tk, n), lambda i, j, m: (j, i, 0)),
        out_shape=jax.ShapeDtypeStruct((shards, K, n), BF16),
        scratch_shapes=[pltpu.VMEM((tk, width), F32)],
        compiler_params=_params("parallel", "parallel", "arbitrary"), name=name)(a, *b_parts)


def _mm_nt_rms_bwd(a, w, l, x, gain, dres, *, name, comm=None, tm=512):
    a_parts = list(a) if isinstance(a, (list, tuple)) else [a]
    na = len(a_parts)
    M, K = a_parts[0].shape[0], sum(part.shape[1] for part in a_parts)
    _, P, D, kc = w.shape
    assert K == P * kc and x.shape == (M, D)
    tm = min(tm, M)
    nr = M // tm
    join = 2 if kc % MXU_TILE and P % 2 == 0 else 1

    def body(*refs):
        ins, (dx_ref, dxb_ref, dg_ref), (acc,) = carried.split(refs, na + 4, 3, 1)
        a_refs, (w_ref, x_ref, g_ref, dres_ref) = ins[:na], ins[na:]
        i = pl.program_id(0)
        carried.emit(refs)
        av = a_refs[0][...] if na == 1 else jnp.concatenate([r[...] for r in a_refs], axis=1)
        dhv = None
        for p in range(0, P, join):
            wv = [w_ref[0, p + s] for s in range(join)]
            d = _dot_nt(av[:, p * kc:(p + join) * kc], wv[0] if join == 1 else jnp.concatenate(wv, axis=1))
            dhv = d if dhv is None else dhv + d
        xv = x_ref[...]
        r = lax.rsqrt(jnp.mean(xv * xv, axis=-1, keepdims=True) + EPS)
        xhat = xv * r
        dxhat = dhv * g_ref[...]
        dx = dres_ref[...] + r * (dxhat - xhat * jnp.mean(dxhat * xhat, axis=-1, keepdims=True))
        dx_ref[...] = dx
        dxb_ref[...] = dx.astype(BF16)
        part = jnp.sum((dhv * xhat).reshape(tm // SUBLANES, SUBLANES, D), axis=0)

        @pl.when(i == 0)
        def _():
            acc[...] = jnp.zeros_like(acc)

        acc[...] += part

        @pl.when(i == nr - 1)
        def _():
            dg_ref[...] = jnp.sum(acc[...], axis=0, keepdims=True)

    row = pl.BlockSpec((tm, D), lambda i: (i, 0))
    vec = pl.BlockSpec((1, D), lambda i: (0, 0))
    carried = _Carried(comm, (nr,))
    return carried.pallas_call(
        body,
        in_specs=[pl.BlockSpec((tm, part.shape[1]), lambda i: (i, 0)) for part in a_parts] + [
            pl.BlockSpec((1, P, D, kc), lambda i: (l, 0, 0, 0), pipeline_mode=pl.Buffered(1)),
            row, vec, row],
        out_specs=[row, row, vec],
        out_shape=[jax.ShapeDtypeStruct((M, D), F32), jax.ShapeDtypeStruct((M, D), BF16),
                   jax.ShapeDtypeStruct((1, D), F32)],
        scratch_shapes=[pltpu.VMEM((SUBLANES, D), F32)],
        args=a_parts + [w, x, gain, dres], name=name)


def _mlp_out_loss_head(a, w, l, res, gain, target, *, name, tr=512, kc=1024):
    T, K = a.shape
    D = w.shape[3]
    nr = T // tr

    def body(a_ref, w_ref, res_ref, g_ref, t_ref, dx_ref, dxb_ref, sq_ref, dg_ref, sq_acc, dg_acc):
        i = pl.program_id(0)
        xv = res_ref[...]
        for k0 in range(0, K, kc):
            xv = xv + _dot(_relu2(a_ref[:, k0:k0 + kc]), w_ref[0, 0, k0:k0 + kc, :])
        g = g_ref[...]
        r = lax.rsqrt(jnp.mean(xv * xv, axis=-1, keepdims=True) + EPS)
        xhat = xv * r
        err = xhat * g - t_ref[...]
        dy = err * (1.0 / D)
        dxhat = dy * g
        dx = r * (dxhat - xhat * jnp.mean(dxhat * xhat, axis=-1, keepdims=True))
        dx_ref[...] = dx
        dxb_ref[...] = dx.astype(BF16)
        sq = jnp.sum((err * err).reshape(tr // SUBLANES, SUBLANES, D), axis=0)
        dg = jnp.sum((dy * xhat).reshape(tr // SUBLANES, SUBLANES, D), axis=0)

        @pl.when(i == 0)
        def _():
            sq_acc[...] = sq
            dg_acc[...] = dg

        @pl.when(i > 0)
        def _():
            sq_acc[...] += sq
            dg_acc[...] += dg

        @pl.when(i == nr - 1)
        def _():
            sq_ref[...] = sq_acc[...]
            dg_ref[...] = jnp.sum(dg_acc[...], axis=0, keepdims=True)

    row = pl.BlockSpec((tr, D), lambda i: (i, 0))
    vec = pl.BlockSpec((1, D), lambda i: (0, 0))
    part = pl.BlockSpec((SUBLANES, D), lambda i: (0, 0))
    return pl.pallas_call(
        body, grid=(nr,),
        in_specs=[pl.BlockSpec((tr, K), lambda i: (i, 0)),
                  pl.BlockSpec((1, 1, K, D), lambda i: (l, 0, 0, 0), pipeline_mode=pl.Buffered(1)),
                  row, vec, row],
        out_specs=[row, row, part, vec],
        out_shape=[jax.ShapeDtypeStruct((T, D), F32), jax.ShapeDtypeStruct((T, D), BF16),
                   jax.ShapeDtypeStruct((SUBLANES, D), F32), jax.ShapeDtypeStruct((1, D), F32)],
        scratch_shapes=[pltpu.VMEM((SUBLANES, D), F32), pltpu.VMEM((SUBLANES, D), F32)],
        compiler_params=_params("arbitrary"), name=name)(a, w, res, gain, target)


def _head0_lanes():
    return lax.broadcasted_iota(jnp.int32, (1, LANES), 1) < SB_HEAD_DIM


def _stack_heads(x):
    zero = jnp.zeros_like(x)
    h0 = _head0_lanes()
    return jnp.concatenate([jnp.where(h0, x, zero), jnp.where(h0, zero, x)], axis=0)


def _unstack_heads(y, tq):
    return jnp.where(_head0_lanes(), y[:tq], y[tq:])


def _past_mask(tq, tk):
    row = lax.broadcasted_iota(jnp.int32, (2 * tq, tk), 0) & (tq - 1)
    col = lax.broadcasted_iota(jnp.int32, (2 * tq, tk), 1)
    return col < row


def _max_of(arrays):
    return functools.reduce(jnp.maximum, [jnp.max(a) for a in arrays])


def _sb_logs(z, past):
    minus_abs = lax.bitcast_convert_type(
        lax.bitcast_convert_type(z, jnp.uint32) | jnp.uint32(0x80000000), F32)
    log_beta = jnp.minimum(z, 0.0) - jnp.log(1.0 + jnp.exp(minus_abs))
    l = log_beta - z
    if past is not None:
        l = jnp.where(past, l, 0.0)
    return log_beta, l


def _suffix_matrix(tk):
    j = lax.broadcasted_iota(jnp.int32, (2 * tk, tk), 0) & (tk - 1)
    s = lax.broadcasted_iota(jnp.int32, (2 * tk, tk), 1)
    return (j > s).astype(BF16)


def _suffix_sum(x, u2, exact=True):
    if not exact:
        return _dot(x.astype(BF16), u2[:x.shape[1]])
    hi, lo = _split_bf16(x)
    return _dot(jnp.concatenate([hi, lo], axis=1), u2)


class _Carried:
    def __init__(self, comm, grid, last="arbitrary"):
        self.comm, self.grid, self.last = comm, grid, last
        self.n_op = len(comm.operands) if comm else 0
        self.n_tgt = len(comm.targets) if comm else 0

    def split(self, refs, n_in, n_out, n_scratch):
        self.n_in, self.n_out, self.n_scratch = n_in, n_out, n_scratch
        a = n_in + self.n_op
        b = a + n_out + self.n_tgt
        return refs[:n_in], refs[a:a + n_out], refs[b:b + n_scratch]

    def emit(self, refs):
        if self.comm is None:
            return
        step, n_steps = 0, 1
        for d, size in enumerate(self.grid):
            step = step * size + pl.program_id(d)
            n_steps *= size
        a = self.n_in + self.n_op + self.n_out
        self.comm.emit(step, n_steps, refs[self.n_in:self.n_in + len(self.comm.sources)],
                       refs[a:a + self.n_tgt], refs[a + self.n_tgt + self.n_scratch:])

    def pallas_call(self, body, *, in_specs, out_specs, out_shape, scratch_shapes, args, name):
        comm = self.comm
        n_in, n_out = len(in_specs), len(out_specs)
        aliases = {}
        if comm is not None:
            in_specs = in_specs + [ANY] * self.n_op
            out_specs = out_specs + [ANY] * self.n_tgt
            out_shape = out_shape + comm.out_shapes
            scratch_shapes = scratch_shapes + comm.sems
            args = args + comm.operands
            aliases = comm.aliases(n_in, n_out)
        sem = ("parallel",) * (len(self.grid) - 1) + (self.last,) if comm is None else ("arbitrary",) * len(self.grid)
        results = pl.pallas_call(
            body, grid=self.grid, in_specs=in_specs, out_specs=out_specs, out_shape=out_shape,
            scratch_shapes=scratch_shapes, input_output_aliases=aliases,
            compiler_params=_params(*sem), name=name)(*args)
        return results[:n_out], results[n_out:]


def _sb_fwd(qkv, *, batch, seq, name, comm=None, tq=SB_TILE, n_pre=SB_STRAIGHT, pairs=SB_PAIRS):
    T, D3 = qkv.shape
    D = D3 // 3
    nhp = D // LANES
    tk = tq
    nq = seq // tq
    scale = SB_HEAD_DIM ** -0.5
    assert 1 <= n_pre <= nq and nhp % pairs == 0

    def body(*refs):
        (q_ref, k_ref, v_ref), (o_ref,), (acc,) = carried.split(refs, 3, 1, 1)
        qi = pl.program_id(2)
        carried.emit(refs)
        qs = [_stack_heads(q_ref[:, lanes]) * scale for lanes in cols]
        past = _past_mask(tq, tk)
        u = _suffix_matrix(tk)

        def block(kb, g, c, diag):
            ks = pl.multiple_of(kb * tk, tk)
            z = _dot_nt(qs[g], k_ref[pl.ds(ks, tk), cols[g]])
            log_beta, l = _sb_logs(z, past if diag else None)
            arg = log_beta + _suffix_sum(l, u, exact=False)
            a = jnp.exp(arg if c is None else arg + c)
            if diag:
                a = jnp.where(past, a, 0.0)
            return _dot(a.astype(BF16), v_ref[pl.ds(ks, tk), cols[g]]), jnp.sum(l, axis=1, keepdims=True)

        def straight(n):
            o_sum, c = [None] * pairs, [None] * pairs
            for b in range(n):
                for g in range(pairs):
                    o_b, c_b = block(qi - b, g, c[g], b == 0)
                    o_sum[g] = o_b if b == 0 else o_sum[g] + o_b
                    c[g] = c_b if b == 0 else c[g] + c_b
            return o_sum, c

        def finish(o_sum):
            for g in range(pairs):
                o_ref[:, cols[g]] = _unstack_heads(o_sum[g], tq)

        for n in range(1, n_pre):
            @pl.when(qi == n - 1)
            def _(n=n):
                finish(straight(n)[0])

        @pl.when(qi >= n_pre - 1)
        def _():
            o_sum, c = straight(n_pre)
            for g in range(pairs):
                acc[g] = o_sum[g]

            def cond(st):
                kb, c = st
                return jnp.logical_and(kb >= 0, _max_of(c) > EXP_ZERO_BELOW)

            def step(st):
                kb, c = st
                new_c = []
                for g in range(pairs):
                    o_n, c_n = block(kb, g, c[g], False)
                    acc[g] += o_n
                    new_c.append(c[g] + c_n)
                return kb - 1, tuple(new_c)

            lax.while_loop(cond, step, (qi - n_pre, tuple(c)))
            finish([acc[g] for g in range(pairs)])

    cols = [slice(g * LANES, (g + 1) * LANES) for g in range(pairs)]
    width = pairs * LANES
    carried = _Carried(comm, (batch, nhp // pairs, nq))
    (o,), comm_results = carried.pallas_call(
        body,
        in_specs=[pl.BlockSpec((tq, width), lambda b, p, i: (b * nq + i, p)),
                  pl.BlockSpec((seq, width), lambda b, p, i: (b, nhp // pairs + p)),
                  pl.BlockSpec((seq, width), lambda b, p, i: (b, 2 * (nhp // pairs) + p))],
        out_specs=[pl.BlockSpec((tq, width), lambda b, p, i: (b * nq + i, p))],
        out_shape=[jax.ShapeDtypeStruct((T, D), F32)],
        scratch_shapes=[pltpu.VMEM((pairs, 2 * tq, LANES), F32)],
        args=[qkv, qkv, qkv], name=name)
    return o, comm_results


def _sb_bwd(qkv, o, do, *, batch, seq, name, comm=None, tq=SB_TILE, n_pre=SB_STRAIGHT, pairs=SB_PAIRS):
    T, D3 = qkv.shape
    D = D3 // 3
    nhp = D // LANES
    tk = tq
    nq = seq // tq
    scale = SB_HEAD_DIM ** -0.5

    def body(*refs):
        ins, outs, scratch = carried.split(refs, 5, 3, 3)
        q_ref, k_ref, v_ref, o_ref, do_ref = ins
        dq_ref, dk_ref, dv_ref = outs
        dq_acc, dk_acc, dv_acc = scratch
        qi = pl.program_id(2)
        carried.emit(refs)

        @pl.when(qi == 0)
        def _():
            dk_acc[...] = jnp.zeros_like(dk_acc)
            dv_acc[...] = jnp.zeros_like(dv_acc)

        qs = [_stack_heads(q_ref[:, lanes]) * scale for lanes in cols]
        dos = [_stack_heads(do_ref[:, lanes]) for lanes in cols]
        dsum = [jnp.sum(_stack_heads(do_ref[:, lanes].astype(F32) * o_ref[:, lanes]), axis=1, keepdims=True)
                for lanes in cols]
        past = _past_mask(tq, tk)
        u = _suffix_matrix(tk)

        def weights(kb, p, c, diag):
            ks = pl.multiple_of(kb * tk, tk)
            kblk = k_ref[pl.ds(ks, tk), cols[p]]
            z = _dot_nt(qs[p], kblk)
            log_beta, l = _sb_logs(z, past if diag else None)
            arg = log_beta + _suffix_sum(l, u, exact=False)
            a = jnp.exp(arg if c is None else arg + c)
            if diag:
                a = jnp.where(past, a, 0.0)
            a = a.astype(BF16)
            g = a.astype(F32) * _dot_nt(dos[p], v_ref[pl.ds(ks, tk), cols[p]])
            return ks, kblk, a, 1.0 - jnp.exp(l), g, jnp.sum(l, axis=1, keepdims=True)

        def scores_grad(p, ks, kblk, a, beta, g, prefix, diag):
            dz = g - beta * prefix
            if diag:
                dz = jnp.where(past, dz, 0.0)
            dzb = dz.astype(BF16)
            dk_acc[pl.ds(ks, tk), cols[p]] += _dot_tn(dzb, qs[p])
            dv_acc[pl.ds(ks, tk), cols[p]] += _dot_tn(a, dos[p])
            return _dot(dzb, kblk)

        def block(kb, p, c, gc):
            ks, kblk, a, beta, g, l_sum = weights(kb, p, c, False)
            prefix = dsum[p] - (_suffix_sum(g, u) + gc)
            return scores_grad(p, ks, kblk, a, beta, g, prefix, False), l_sum, jnp.sum(g, axis=1, keepdims=True)

        def finish(dq_sum):
            for p in range(pairs):
                dq_ref[:, cols[p]] = (_unstack_heads(dq_sum[p], tq) * scale).astype(BF16)

        def straight(n):
            dq_sum, c_all, g_all = [], [], []
            for p in range(pairs):
                blocks, c = [], None
                for b in range(n):
                    blk = weights(qi - b, p, c, b == 0)
                    c = blk[5] if b == 0 else c + blk[5]
                    g_bf = blk[4].astype(BF16)
                    suffix = _suffix_sum(g_bf, u, exact=False)
                    rounded_sum = suffix[:, :1] + g_bf[:, :1].astype(F32)
                    blocks.append((blk, suffix, rounded_sum, jnp.sum(blk[4], axis=1, keepdims=True)))
                exact_sum = functools.reduce(lambda x, y: x + y, [e for _, _, _, e in blocks])
                total = dsum[p] - exact_sum + functools.reduce(lambda x, y: x + y, [r for _, _, r, _ in blocks])
                dq, right = None, None
                for b, ((ks, kblk, a, beta, g, _), suffix, rounded_sum, _) in enumerate(blocks):
                    prefix = total - (suffix if right is None else suffix + right)
                    d = scores_grad(p, ks, kblk, a, beta, g, prefix, b == 0)
                    dq = d if dq is None else dq + d
                    right = rounded_sum if right is None else right + rounded_sum
                dq_sum.append(dq)
                c_all.append(c)
                g_all.append(exact_sum)
            return dq_sum, c_all, g_all

        for n in range(1, n_pre):
            @pl.when(qi == n - 1)
            def _(n=n):
                finish(straight(n)[0])

        @pl.when(qi >= n_pre - 1)
        def _():
            dq_sum, c, gc = straight(n_pre)
            for p in range(pairs):
                dq_acc[p] = dq_sum[p]

            def cond(st):
                kb, c, gc = st
                return jnp.logical_and(kb >= 0, _max_of(c) > EXP_ZERO_BELOW)

            def step(st):
                kb, c, gc = st
                new_c, new_gc = [], []
                for p in range(pairs):
                    dq_n, c_n, g_n = block(kb, p, c[p], gc[p])
                    dq_acc[p] += dq_n
                    new_c.append(c[p] + c_n)
                    new_gc.append(gc[p] + g_n)
                return kb - 1, tuple(new_c), tuple(new_gc)

            lax.while_loop(cond, step, (qi - n_pre, tuple(c), tuple(gc)))
            finish([dq_acc[p] for p in range(pairs)])

        @pl.when(qi == nq - 1)
        def _():
            dk_ref[...] = dk_acc[...].astype(BF16)
            dv_ref[...] = dv_acc[...].astype(BF16)

    cols = [slice(p * LANES, (p + 1) * LANES) for p in range(pairs)]
    width, ncb = pairs * LANES, nhp // pairs
    qspec = pl.BlockSpec((tq, width), lambda b, p, i: (b * nq + i, p))
    sspec = pl.BlockSpec((seq, width), lambda b, p, i: (b, p))
    out = jax.ShapeDtypeStruct((T, D), BF16)
    carried = _Carried(comm, (batch, ncb, nq))
    return carried.pallas_call(
        body,
        in_specs=[qspec,
                  pl.BlockSpec((seq, width), lambda b, p, i: (b, ncb + p)),
                  pl.BlockSpec((seq, width), lambda b, p, i: (b, 2 * ncb + p)),
                  qspec, qspec],
        out_specs=[qspec, sspec, sspec], out_shape=[out, out, out],
        scratch_shapes=[pltpu.VMEM((pairs, 2 * tq, LANES), F32), pltpu.VMEM((seq, width), F32),
                        pltpu.VMEM((seq, width), F32)],
        args=[qkv, qkv, qkv, o, do], name=name)


_GELU_C = 0.7978845608028654
_GELU_A = 0.044715


def _gelu(x):
    x = x.astype(BF16)
    xx = x * x
    a1 = 1.0 + jnp.tanh(x * (_GELU_C + (_GELU_C * _GELU_A) * xx))
    hx = 0.5 * x
    grad = a1 * (0.5 + (hx * (2.0 - a1)) * (_GELU_C + (3.0 * _GELU_C * _GELU_A) * xx))
    return hx * a1, grad


def _causal_ws(ws_ref, g):
    t = lax.broadcasted_iota(jnp.int32, (SGU_CHUNK, SGU_CHUNK), 0)
    s = lax.broadcasted_iota(jnp.int32, (SGU_CHUNK, SGU_CHUNK), 1)
    return jnp.where(s <= t, ws_ref[g], 0.0)


def _sgu_gate(u, v, g_ref, ws_ref, bs_ref):
    G, W = SGU_GROUPS, SGU_GROUP_W
    r = lax.rsqrt(jnp.mean(v * v, axis=-1, keepdims=True) + EPS)
    vn = (v * r * g_ref[...]).astype(BF16)
    y = []
    for g in range(G):
        sl = slice(g * W, (g + 1) * W)
        mixed = _dot(_causal_ws(ws_ref, g).astype(BF16), vn[:, sl]) + bs_ref[:, g:g + 1]
        y.append((u[:, sl] * mixed).astype(BF16))
    return jnp.concatenate(y, axis=1)


def _sgu_bwd(uv, dgelu, dy, gain, ws, bst, *, name):
    T, F2 = uv.shape
    F = F2 // 2
    C, G, W = SGU_CHUNK, SGU_GROUPS, SGU_GROUP_W
    nc = T // C

    def body(uv_ref, dgelu_ref, dy_ref, g_ref, ws_ref, bs_ref, duv_ref, dg_ref, dws_ref, dbs_ref,
             dg_acc, dws_acc, dbs_acc):
        i = pl.program_id(0)

        @pl.when(i == 0)
        def _():
            dg_acc[...] = jnp.zeros_like(dg_acc)
            dws_acc[...] = jnp.zeros_like(dws_acc)
            dbs_acc[...] = jnp.zeros_like(dbs_acc)

        u, v = uv_ref[:, :F].astype(F32), uv_ref[:, F:].astype(F32)
        dgelu = dgelu_ref[...].astype(F32)
        r = lax.rsqrt(jnp.mean(v * v, axis=-1, keepdims=True) + EPS)
        vhat = v * r
        gain_v = g_ref[...]
        vn = (vhat * gain_v).astype(BF16)
        dyv = dy_ref[...].astype(F32)
        lane8 = lax.broadcasted_iota(jnp.int32, (1, G), 1)
        dvn_parts = []
        dbs_new = jnp.zeros((C, G), F32)
        for g in range(G):
            sl = slice(g * W, (g + 1) * W)
            wsg = _causal_ws(ws_ref, g)
            mixed = _dot(wsg.astype(BF16), vn[:, sl]) + bs_ref[:, g:g + 1]
            duv_ref[:, sl] = (dyv[:, sl] * mixed * dgelu[:, sl]).astype(BF16)
            dmix = dyv[:, sl] * u[:, sl]
            dbs_new = dbs_new + jnp.where(lane8 == g, jnp.sum(dmix, axis=1, keepdims=True), 0.0)
            dmix_b = dmix.astype(BF16)
            dws_acc[g] += _dot_nt(dmix_b, vn[:, sl])
            dvn_parts.append(_dot(wsg.T.astype(BF16), dmix_b))
        dbs_acc[...] += dbs_new
        dvn = jnp.concatenate(dvn_parts, axis=1)
        dg_acc[...] += jnp.sum((dvn * vhat).reshape(C // SUBLANES, SUBLANES, F), axis=0)
        dvhat = dvn * gain_v
        dv = r * (dvhat - vhat * jnp.mean(dvhat * vhat, axis=-1, keepdims=True))
        duv_ref[:, F:] = (dv * dgelu[:, F:]).astype(BF16)

        @pl.when(i == nc - 1)
        def _():
            dg_ref[...] = jnp.sum(dg_acc[...], axis=0, keepdims=True)
            t = lax.broadcasted_iota(jnp.int32, (G, C, C), 1)
            s = lax.broadcasted_iota(jnp.int32, (G, C, C), 2)
            dws_ref[...] = jnp.where(s <= t, dws_acc[...], 0.0)
            dbs_ref[...] = dbs_acc[...]

    return pl.pallas_call(
        body, grid=(nc,),
        in_specs=[pl.BlockSpec((C, F2), lambda i: (i, 0)), pl.BlockSpec((C, F2), lambda i: (i, 0)),
                  pl.BlockSpec((C, F), lambda i: (i, 0)),
                  pl.BlockSpec((1, F), lambda i: (0, 0)), pl.BlockSpec((G, C, C), lambda i: (0, 0, 0)),
                  pl.BlockSpec((C, G), lambda i: (0, 0))],
        out_specs=[pl.BlockSpec((C, F2), lambda i: (i, 0)), pl.BlockSpec((1, F), lambda i: (0, 0)),
                   pl.BlockSpec((G, C, C), lambda i: (0, 0, 0)), pl.BlockSpec((C, G), lambda i: (0, 0))],
        out_shape=[jax.ShapeDtypeStruct((T, F2), BF16), jax.ShapeDtypeStruct((1, F), F32),
                   jax.ShapeDtypeStruct((G, C, C), F32), jax.ShapeDtypeStruct((C, G), F32)],
        scratch_shapes=[pltpu.VMEM((SUBLANES, F), F32), pltpu.VMEM((G, C, C), F32), pltpu.VMEM((C, G), F32)],
        compiler_params=_params("arbitrary"), name=name)(uv, dgelu, dy, gain, ws, bst)


def _my_place():
    return lax.axis_index("x"), lax.axis_index("y"), lax.axis_index("c")


def _all_gather(shards, *, name):
    nf = len(shards)
    items = [(lambda ins, f=f: ins[f], lambda outs, p, f=f: outs[f].at[:, p]) for f in range(nf)]
    targets = [jax.ShapeDtypeStruct((s.shape[0], N_DEV) + s.shape[1:], s.dtype) for s in shards]
    return _run_comm(_Comm(shards, targets, len(items), _gather_emit(items, [list(range(nf))])), name=name)


class _Comm:
    def __init__(self, sources, targets, n_items, emit):
        self.sources, self.targets, self.n_items, self.emit = list(sources), list(targets), n_items, emit
        self.filled = [t for t in self.targets if not isinstance(t, jax.ShapeDtypeStruct)]
        self.operands = self.sources + self.filled
        self.out_shapes = [jax.ShapeDtypeStruct(t.shape, t.dtype) for t in self.targets]
        self.sems = [pltpu.SemaphoreType.DMA((n_items, 7)), pltpu.SemaphoreType.DMA((n_items, 7)),
                     pltpu.SemaphoreType.DMA((n_items,))]

    def aliases(self, first_operand, first_result):
        pos = {id(t): k for k, t in enumerate(self.targets)}
        return {first_operand + len(self.sources) + a: first_result + pos[id(t)]
                for a, t in enumerate(self.filled)}


def _run_comm(comm, *, name):
    n_op, n_out = len(comm.operands), len(comm.targets)

    def body(*refs):
        comm.emit(0, 1, refs[:len(comm.sources)], refs[n_op:n_op + n_out], refs[n_op + n_out:])

    return pl.pallas_call(
        body, in_specs=[ANY] * n_op, out_specs=[ANY] * n_out, out_shape=comm.out_shapes,
        scratch_shapes=comm.sems, input_output_aliases=comm.aliases(0, 0), name=name)(*comm.operands)


def _at_steps(step, n_steps, phases):
    if n_steps == 1:
        for _, fn in phases:
            fn()
        return
    marks = {}
    for frac, fn in phases:
        marks.setdefault(min(int(frac * n_steps), n_steps - 1), []).append(fn)
    for mark, fns in sorted(marks.items()):
        @pl.when(step == mark)
        def _(fns=fns):
            for fn in fns:
                fn()


def _gather_emit(items, groups, fractions=None):
    if fractions is None:
        fractions = [(g + 1) / len(groups) for g in range(len(groups))]
    def emit(step, n_steps, ins, outs, sems):
        send_sems, recv_sems, local_sems = sems
        x, y, c = _my_place()
        me, sibling = (x, y, c), (x, y, 1 - c)
        chips = [(1 - x, y), (x, 1 - y), (1 - x, 1 - y)]

        def copy(i, k, block, to, own=False):
            src_of, dst_of = items[i]
            dst = dst_of(outs, 4 * block[0] + 2 * block[1] + block[2])
            return pltpu.make_async_remote_copy(
                src_ref=src_of(ins) if own else dst, dst_ref=dst,
                send_sem=send_sems.at[i, k], recv_sem=recv_sems.at[i, k],
                device_id=to, device_id_type=MESH)

        def local(i):
            src_of, dst_of = items[i]
            return pltpu.make_async_copy(src_of(ins), dst_of(outs, 4 * x + 2 * y + c), local_sems.at[i])

        def first(i):
            return [copy(i, 0, me, sibling, own=True)] + [
                copy(i, 1 + j, me, (*chip, c), own=True) for j, chip in enumerate(chips)]

        def start():
            for i in range(len(items)):
                local(i).start()
                for cp in first(i):
                    cp.start()

        def forward(group):
            for j, chip in enumerate(chips):
                for i in group:
                    copy(i, 1 + j, (*chip, c), me).wait_recv()
                    copy(i, 4 + j, (*chip, c), sibling).start()

        def finish():
            for i in range(len(items)):
                copy(i, 0, sibling, me).wait_recv()
                for j, chip in enumerate(chips):
                    copy(i, 4 + j, (*chip, 1 - c), me).wait_recv()
            for i in range(len(items)):
                for cp in first(i) + [copy(i, 4 + j, (*chip, c), sibling) for j, chip in enumerate(chips)]:
                    cp.wait_send()
                local(i).wait()

        phases = [(0.0, start)]
        for frac, group in zip(fractions, groups):
            phases.append((frac, functools.partial(forward, group)))
        phases.append((1.0, finish))
        _at_steps(step, n_steps, phases)

    return emit


def _exchange_emit(items):
    def emit(step, n_steps, ins, outs, sems):
        send_sems, recv_sems, local_sems = sems
        x, y, c = _my_place()
        me = 4 * x + 2 * y + c

        def peer_of(k):
            return x ^ (k >> 2), y ^ ((k >> 1) & 1), c ^ (k & 1)

        def copy(i, k):
            src_of, dst_of = items[i]
            px, py, pc = peer_of(k)
            return pltpu.make_async_remote_copy(
                src_ref=src_of(ins, 4 * px + 2 * py + pc), dst_ref=dst_of(outs, me),
                send_sem=send_sems.at[i, k - 1], recv_sem=recv_sems.at[i, k - 1],
                device_id=(px, py, pc), device_id_type=MESH)

        def arrival(i, k):
            src_of, dst_of = items[i]
            px, py, pc = peer_of(k)
            peer = 4 * px + 2 * py + pc
            return pltpu.make_async_remote_copy(
                src_ref=src_of(ins, peer), dst_ref=dst_of(outs, peer),
                send_sem=send_sems.at[i, k - 1], recv_sem=recv_sems.at[i, k - 1],
                device_id=(x, y, c), device_id_type=MESH)

        def local(i):
            src_of, dst_of = items[i]
            return pltpu.make_async_copy(src_of(ins, me), dst_of(outs, me), local_sems.at[i])

        def start():
            for i in range(len(items)):
                local(i).start()
            for k in range(1, N_DEV):
                for i in range(len(items)):
                    copy(i, k).start()

        def finish():
            for k in range(1, N_DEV):
                for i in range(len(items)):
                    arrival(i, k).wait_recv()
            for k in range(1, N_DEV):
                for i in range(len(items)):
                    copy(i, k).wait_send()
            for i in range(len(items)):
                local(i).wait()

        _at_steps(step, n_steps, [(0.0, start), (1.0, finish)])

    return emit


def _adam_math(g, w, m, v):
    m = ADAM_B1 * m + (1.0 - ADAM_B1) * g
    v = ADAM_B2 * v + (1.0 - ADAM_B2) * (g * g)
    m_hat = m / (1.0 - ADAM_B1 ** ADAM_STEP)
    v_hat = v / (1.0 - ADAM_B2 ** ADAM_STEP)
    delta = -ADAM_LR * (m_hat / (jnp.sqrt(v_hat) + ADAM_EPS) + ADAM_WD * w)
    return delta, m, v


def _sum_adamw(parts, w, m, v, *, name, tr=256):
    L, nd, R, C = parts.shape
    tr = min(tr, R)
    assert R % tr == 0

    def body(p_ref, w_ref, m_ref, v_ref, g_ref, d_ref, nm_ref, nv_ref):
        g = p_ref[0, 0].astype(F32)
        for q in range(1, nd):
            g = g + p_ref[0, q].astype(F32)
        d, nm, nv = _adam_math(g, w_ref[0], m_ref[0], v_ref[0])
        g_ref[0] = g
        d_ref[0] = d
        nm_ref[0] = nm
        nv_ref[0] = nv

    blk = pl.BlockSpec((1, tr, C), lambda l, i: (l, i, 0))
    out = jax.ShapeDtypeStruct((L, R, C), F32)
    return pl.pallas_call(
        body, grid=(L, R // tr),
        in_specs=[pl.BlockSpec((1, nd, tr, C), lambda l, i: (l, 0, i, 0)), blk, blk, blk],
        out_specs=[blk] * 4, out_shape=[out] * 4,
        compiler_params=_params("parallel", "parallel"), name=name)(parts, w, m, v)


def _sum_parts(parts, *, name):
    nd, R, C = parts.shape

    def body(p_ref, o_ref):
        g = p_ref[0]
        for q in range(1, nd):
            g = g + p_ref[q]
        o_ref[...] = g

    return pl.pallas_call(
        body, out_shape=jax.ShapeDtypeStruct((R, C), F32),
        in_specs=[pl.BlockSpec(memory_space=pltpu.VMEM)],
        out_specs=pl.BlockSpec(memory_space=pltpu.VMEM), name=name)(parts)


def _adamw_small(g, w, m, v, *, name):
    def body(g_ref, w_ref, m_ref, v_ref, d_ref, nm_ref, nv_ref):
        d, nm, nv = _adam_math(g_ref[...], w_ref[...], m_ref[...], v_ref[...])
        d_ref[...] = d
        nm_ref[...] = nm
        nv_ref[...] = nv

    vm = pl.BlockSpec(memory_space=pltpu.VMEM)
    out = jax.ShapeDtypeStruct(g.shape, F32)
    return pl.pallas_call(body, out_shape=[out] * 3, in_specs=[vm] * 4, out_specs=[vm] * 3,
                          name=name)(g, w, m, v)


def kernel(x, norm_mix, norm_mlp, sb_wqkv, sb_wo, sgu_win, sgu_gain, sgu_ws, sgu_bs, sgu_wout, mlp_w1, mlp_w2, final_norm, loss_target, m_norm_mix, m_norm_mlp, m_sb_wqkv, m_sb_wo, m_sgu_win, m_sgu_gain, m_sgu_ws, m_sgu_bs, m_sgu_wout, m_mlp_w1, m_mlp_w2, m_final_norm, v_norm_mix, v_norm_mlp, v_sb_wqkv, v_sb_wo, v_sgu_win, v_sgu_gain, v_sgu_ws, v_sgu_bs, v_sgu_wout, v_mlp_w1, v_mlp_w2, v_final_norm):
    batch, seq, D = x.shape
    T = batch * seq
    x0 = x.reshape(T, D)
    target = loss_target.reshape(T, D)

    WQKV, WO, WIN, WOUT, W1, W2, GAIN = range(7)
    big = [sb_wqkv, sb_wo, sgu_win, sgu_wout, mlp_w1, mlp_w2]
    shards = [w.astype(BF16) for w in big] + [sgu_gain[:, None, :]]
    rides = {
        "qkv0": [[(WO, 0), (W1, 0)]],
        "sb_fwd0": [[(WIN, 0), (WOUT, 0), (W1, 1), (W2, 1)], [(WQKV, 1)]],
        "w1_0": [[(W2, 0)]],
        "qkv2": [[(WO, 1), (W1, 2)]],
        "sb_fwd2": [[(W2, 2)], [(WIN, 1), (WOUT, 1), (W1, 3), (W2, 3)]],
    }

    first = [(lambda ins: ins[WQKV].at[0], lambda outs, p: outs[WQKV].at[0, p]),
             (lambda ins: ins[GAIN], lambda outs, p: outs[GAIN].at[:, p])]
    targets = [jax.ShapeDtypeStruct((s.shape[0], N_DEV) + s.shape[1:], s.dtype) for s in shards]
    gathered0 = _run_comm(_Comm(shards, targets, len(first), _gather_emit(first, [[0, 1]])), name="gather_first")
    gain_sgu = gathered0[GAIN].reshape(-1, 1, SGU_FFN)
    gw = dict(enumerate(gathered0[:GAIN]))

    def gather_ride(call):
        wanted = [fl for group in rides[call] for fl in group]
        fams = sorted({f for f, _ in wanted})
        items = [(lambda ins, f=f, l=l: ins[f].at[l], lambda outs, p, t=fams.index(f), l=l: outs[t].at[l, p])
                 for f, l in wanted]
        groups, sent, k = [], [], 0
        for group in rides[call]:
            groups.append(list(range(k, k + len(group))))
            sent.append(sum(shards[f].shape[1] * shards[f].shape[2] for f, _ in group) + (sent[-1] if sent else 0))
            k += len(group)
        fractions = [s / sent[-1] for s in sent]
        return _Comm(shards[:GAIN], [gw[f] for f in fams], len(items), _gather_emit(items, groups, fractions)), fams

    def weight(f):
        g = gw[f]
        return g if f in (WQKV, WIN, W1) else g.reshape(g.shape[0], 1, N_DEV * g.shape[2], g.shape[3])

    saved = []
    xs = x0
    for i in range(DEPTH):
        j = i // 2
        if i % 2 == 0:
            comm, fams = gather_ride(f"qkv{i}")
            (qkv, h), results = _mm_nn(xs, weight(WQKV), j, out_dtype=BF16, gain=norm_mix[i:i + 1],
                                       name=f"qkv{i}", comm=comm, tm=1024)
            gw.update(zip(fams, results))
            comm, fams = gather_ride(f"sb_fwd{i}")
            o, results = _sb_fwd(qkv, batch=batch, seq=seq, name=f"sb_fwd{i}", comm=comm)
            gw.update(zip(fams, results))
            x_mid = _mm_nn(o, weight(WO), j, out_dtype=F32, res=xs, name=f"wo{i}", tm=1024)
            mix = (h, qkv, o)
        else:
            gain_j = gain_sgu[j]
            bst = sgu_bs[j].T
            uv, h, dgelu, yv = _mm_nn(xs, weight(WIN), j, out_dtype=BF16, gain=norm_mix[i:i + 1],
                                      sgu=(gain_j, sgu_ws[j], bst), name=f"win_sgu{i}")
            x_mid = _mm_nn(yv, weight(WOUT), j, out_dtype=F32, res=xs, name=f"wout{i}", tm=1024)
            mix = (h, uv, dgelu, yv, gain_j, bst)
        if f"w1_{i}" in rides:
            comm, fams = gather_ride(f"w1_{i}")
            (a, h2), results = _mm_nn(x_mid, weight(W1), i, out_dtype=BF16, gain=norm_mlp[i:i + 1],
                                      name=f"w1_{i}", comm=comm, tm=1024)
            gw.update(zip(fams, results))
        else:
            a, h2 = _mm_nn(x_mid, weight(W1), i, out_dtype=BF16, gain=norm_mlp[i:i + 1], name=f"w1_{i}", tm=1024)
        saved.append((xs, mix, x_mid, h2, a))
        if i < DEPTH - 1:
            xs = _mm_nn(a, weight(W2), i, out_dtype=F32, res=x_mid, a_act="relu2", name=f"w2_{i}")
    g_wqkv, g_wo, g_win, g_wout, g_w1, g_w2 = [weight(f) for f in range(GAIN)]

    dx, dxb, sq, d_final = _mlp_out_loss_head(a, g_w2, DEPTH - 1, x_mid, final_norm.reshape(1, D), target,
                                              name="w2_loss_head")
    loss = lax.psum(0.5 * jnp.sum(sq) / D, ("x", "y", "c"))

    SMALL = GAIN
    stacks = {f: jax.ShapeDtypeStruct((w.shape[0], N_DEV) + w.shape[1:], BF16) for f, w in enumerate(big)}
    pending = []
    second_half = []

    def row_shards(p):
        return p.reshape(N_DEV, p.shape[1] // N_DEV, p.shape[2])

    def exchange(going):
        fams = sorted({f for _, f, _, _ in going})

        def item(a, f, l, rows):
            rows = slice(None) if rows is None else slice(*rows)
            return (lambda ins, p: ins[a].at[p, rows], lambda outs, q: outs[fams.index(f)].at[l, q, rows])

        items = [item(a, f, l, rows) for a, (_, f, l, rows) in enumerate(going)]
        return _Comm([g[0] for g in going], [stacks[f] for f in fams], len(items), _exchange_emit(items)), fams

    def take(entries):
        going = list(entries)
        entries.clear()
        return going

    d_norm_mix, d_norm_mlp = [None] * DEPTH, [None] * DEPTH
    d_gain, d_ws, d_bs = [None] * 2, [None] * 2, [None] * 2
    for i in reversed(range(DEPTH)):
        j = i // 2
        xs, mix, x_mid, h2, a = saved[i]
        da = _mm_nt(dxb, g_w2, i, out_dtype=BF16, act_src=a, name=f"d_a{i}")
        pending.append((row_shards(_mm_tn(a, dxb, shards=1, a_act="relu2", name=f"d_w2_{i}")), W2, i, None))
        pending.append((_mm_tn(h2, da, shards=N_DEV, name=f"d_w1_{i}", pb=4, tm=1024), W1, i, None))
        comm, fams = exchange(take(second_half)) if second_half else (None, [])
        (dx, dxb, d_norm_mlp[i]), results = _mm_nt_rms_bwd(
            da, g_w1, i, x_mid, norm_mlp[i:i + 1], dx, name=f"d_h2_{i}", comm=comm)
        stacks.update(zip(fams, results))
        if i % 2 == 0:
            h, qkv, o = mix
            do = _mm_nt(dxb, g_wo, j, out_dtype=BF16, name=f"d_o{i}", tm=1024)
            pending.append((row_shards(_mm_tn(o, dxb, shards=1, name=f"d_wo{i}")), WO, j, None))
            comm, fams = exchange(take(pending))
            (dq, dk, dv), results = _sb_bwd(qkv, o, do, batch=batch, seq=seq, name=f"sb_bwd{i}", comm=comm)
            stacks.update(zip(fams, results))
            mixer_in, w_in, f_in = [dq, dk, dv], g_wqkv, WQKV
            d_w_in = _mm_tn(h, mixer_in, shards=N_DEV, name=f"d_wqkv{i}", pb=N_DEV, tm=1024, tk=512)
        else:
            h, uv, dgelu, yv, gain_j, bst = mix
            dy = _mm_nt(dxb, g_wout, j, out_dtype=BF16, name=f"d_y{i}", tm=1024)
            pending.append((row_shards(_mm_tn(yv, dxb, shards=1, name=f"d_wout{i}")), WOUT, j, None))
            mixer_in, d_gain[j], d_ws[j], dbst = _sgu_bwd(uv, dgelu, dy, gain_j, sgu_ws[j], bst, name=f"sgu_bwd{i}")
            d_bs[j] = dbst.T
            w_in, f_in = g_win, WIN
            d_w_in = _mm_tn(h, mixer_in, shards=N_DEV, name=f"d_win{i}", pb=4, tm=1024)
        rows = d_w_in.shape[1]
        if i >= 2:
            pending.append((d_w_in, f_in, j, None))
            comm, fams = None, []
        elif i == 1:
            second_half.append((d_w_in, f_in, j, (rows // 2, rows)))
            comm, fams = exchange([(d_w_in, f_in, j, (0, rows // 2))])
        else:
            comm, fams = exchange([(d_w_in, f_in, j, None)])
        (dx, dxb, d_norm_mix[i]), results = _mm_nt_rms_bwd(
            mixer_in, w_in, j, xs, norm_mix[i:i + 1], dx, name=f"d_h_mix{i}", comm=comm)
        stacks.update(zip(fams, results))
    grad_x = dx.reshape(batch, seq, D)

    small = [jnp.concatenate(d_norm_mix, 0), jnp.concatenate(d_norm_mlp, 0), d_final,
             jnp.concatenate(d_gain, 0), jnp.stack(d_bs, 0), jnp.stack(d_ws, 0)]
    small_flat = jnp.concatenate([s.reshape(-1) for s in small])
    n_small = small_flat.shape[0]
    small_rows = -(-n_small // (N_DEV * SUBLANES * LANES)) * SUBLANES
    small_flat = jnp.pad(small_flat, (0, N_DEV * small_rows * LANES - n_small))
    stacks[SMALL] = jax.ShapeDtypeStruct((1, N_DEV, small_rows, LANES), F32)
    pending.append((small_flat.reshape(N_DEV, small_rows, LANES), SMALL, 0, None))
    comm, fams = exchange(take(pending))
    stacks.update(zip(fams, _run_comm(comm, name="exchange_last")))
    r_wqkv, r_wo, r_win, r_wout, r_w1, r_w2, r_small = [stacks[f] for f in range(SMALL + 1)]

    u_wqkv = _sum_adamw(r_wqkv, sb_wqkv, m_sb_wqkv, v_sb_wqkv, name="adamw_wqkv")
    u_wo = _sum_adamw(r_wo, sb_wo, m_sb_wo, v_sb_wo, name="adamw_wo")
    u_win = _sum_adamw(r_win, sgu_win, m_sgu_win, v_sgu_win, name="adamw_win")
    u_wout = _sum_adamw(r_wout, sgu_wout, m_sgu_wout, v_sgu_wout, name="adamw_wout")
    u_w1 = _sum_adamw(r_w1, mlp_w1, m_mlp_w1, v_mlp_w1, name="adamw_w1")
    u_w2 = _sum_adamw(r_w2, mlp_w2, m_mlp_w2, v_mlp_w2, name="adamw_w2")

    small_sum = _sum_parts(r_small[0], name="sum_small")
    g_small = _all_gather([small_sum[None]], name="gather_small")[0].reshape(-1)[:n_small]

    shapes = [s.shape for s in small]
    sizes = [s.size for s in small]
    offs = [sum(sizes[:k]) for k in range(len(sizes))]
    me = 4 * lax.axis_index("x") + 2 * lax.axis_index("y") + lax.axis_index("c")
    shard_w = SGU_FFN // N_DEV

    def pack(arrs):
        flat = jnp.concatenate([a_.reshape(-1) for a_ in arrs])
        return jnp.pad(flat, (0, N_DEV * small_rows * LANES - n_small)).reshape(-1, LANES)

    def full_gain(gshard):
        return lax.dynamic_update_slice(jnp.zeros((2, SGU_FFN), F32), gshard, (0, me * shard_w))

    w_small = pack([norm_mix, norm_mlp, final_norm, full_gain(sgu_gain), sgu_bs, sgu_ws])
    m_small = pack([m_norm_mix, m_norm_mlp, m_final_norm, full_gain(m_sgu_gain), m_sgu_bs, m_sgu_ws])
    v_small = pack([v_norm_mix, v_norm_mlp, v_final_norm, full_gain(v_sgu_gain), v_sgu_bs, v_sgu_ws])
    g_pack = jnp.pad(g_small, (0, N_DEV * small_rows * LANES - n_small)).reshape(-1, LANES)
    sm = [g_pack] + list(_adamw_small(g_pack, w_small, m_small, v_small, name="adamw_small"))

    def unpack(flat2d):
        flat = flat2d.reshape(-1)
        out = [flat[offs[k]:offs[k] + sizes[k]].reshape(shapes[k]) for k in range(len(sizes))]
        out[2] = out[2].reshape(D)
        out[3] = lax.dynamic_slice(out[3], (0, me * shard_w), (2, shard_w))
        return out

    outs = []
    for k, big_u in enumerate(zip(u_wqkv, u_wo, u_win, u_wout, u_w1, u_w2)):
        s_nm, s_nl, s_fn, s_gain, s_bs, s_ws = unpack(sm[k])
        b_wqkv, b_wo, b_win, b_wout, b_w1, b_w2 = big_u
        outs += [s_nm, s_nl, b_wqkv, b_wo, b_win, s_gain, s_ws, s_bs, b_wout, b_w1, b_w2, s_fn]
    return (loss, grad_x, *outs)
```

```python
import functools

import jax
import jax.numpy as jnp
from jax import lax
from jax.experimental import pallas as pl
from jax.experimental.pallas import tpu as pltpu

F32 = jnp.float32
BF16 = jnp.bfloat16

N_DEV = 8
D_MODEL = 1024
SEQ = 2048
DEPTH = 4
SB_HEAD_DIM = 64
SGU_CHUNK = 128
SGU_FFN = 2 * D_MODEL
SGU_GROUPS = 8
SGU_GROUP_W = SGU_FFN // SGU_GROUPS
EPS = 1e-6

ADAM_LR = 0.001
ADAM_B1 = 0.9
ADAM_B2 = 0.999
ADAM_EPS = 1e-08
ADAM_WD = 0.01
ADAM_STEP = 10

MXU_TILE = 256
LANES = 128
SUBLANES = 8
VMEM_LIMIT = 56 * 1024 * 1024
EXP_ZERO_BELOW = -104.0
SB_TILE = 256
SB_STRAIGHT = 2
SB_PAIRS = 4

MESH = pl.DeviceIdType.MESH
ANY = pl.BlockSpec(memory_space=pl.ANY)


def _params(*sem):
    return pltpu.CompilerParams(dimension_semantics=sem, vmem_limit_bytes=VMEM_LIMIT)


def _dot(a, b):
    return jnp.dot(a, b, preferred_element_type=F32)


def _dot_nt(a, b):
    return lax.dot_general(a, b, (((1,), (1,)), ((), ())), preferred_element_type=F32)


def _dot_tn(a, b):
    return lax.dot_general(a, b, (((0,), (0,)), ((), ())), preferred_element_type=F32)


def _split_bf16(x):
    hi = x.astype(BF16)
    lo = (x - hi.astype(F32)).astype(BF16)
    return hi, lo


def _relu2(av):
    t = jnp.maximum(av, jnp.zeros_like(av))
    return t * t


def _mm_nn(a, w, l, *, out_dtype, name, res=None, a_act=None, gain=None, sgu=None, comm=None,
           tm=512, kc=1024):
    M, K = a.shape
    _, P, K2, n = w.shape
    assert K2 == K
    tm, kc = min(tm, M), min(kc, K)
    assert M % tm == 0 and K % kc == 0
    join = 2 if n % MXU_TILE and P % 2 == 0 else 1

    n_main = 2 + (gain is not None) + (res is not None)
    n_in = n_main + (3 if sgu is not None else 0)
    n_out = 1 + (gain is not None) + (2 if sgu is not None else 0)
    gelu = sgu is not None

    def body(*all_refs):
        ins, outs, _ = carried.split(all_refs, n_in, n_out, 0)
        carried.emit(all_refs)
        refs = list(ins) + list(outs)
        a_ref, w_ref, o_ref = refs[0], refs[1], refs[n_in]
        if gain is not None:
            xv = a_ref[...]
            r = lax.rsqrt(jnp.mean(xv * xv, axis=-1, keepdims=True) + EPS)
            h = (xv * r * refs[2][...]).astype(BF16)
            refs[n_in + 1][...] = h
        for p in range(0, P, join):
            sl = slice(p * n, (p + join) * n)
            acc = None
            for k0 in range(0, K, kc):
                if gain is not None:
                    av = h[:, k0:k0 + kc]
                else:
                    av = a_ref[:, k0:k0 + kc]
                    av = _relu2(av) if a_act == "relu2" else av.astype(BF16)
                wv = [w_ref[0, p + s, k0:k0 + kc, :] for s in range(join)]
                d = _dot(av, wv[0] if join == 1 else jnp.concatenate(wv, axis=1))
                acc = d if acc is None else acc + d
            if res is not None:
                acc = acc + refs[n_main - 1][:, sl]
            if gelu:
                acc, dact = _gelu(acc)
                refs[-2][:, sl] = dact.astype(BF16)
            o_ref[:, sl] = acc.astype(out_dtype)
        if sgu is not None:
            half = P * n // 2
            for r0 in range(0, tm, SGU_CHUNK):
                rows = slice(r0, r0 + SGU_CHUNK)
                refs[-1][rows, :] = _sgu_gate(o_ref[rows, :half].astype(F32), o_ref[rows, half:].astype(F32),
                                              *refs[n_main:n_in])

    row = lambda width: pl.BlockSpec((tm, width), lambda i: (i, 0))
    whole = lambda arr: pl.BlockSpec(arr.shape, lambda i: (0,) * arr.ndim)
    in_specs = [row(K), pl.BlockSpec((1, P, K, n), lambda i: (l, 0, 0, 0), pipeline_mode=pl.Buffered(1))]
    args = [a, w]
    if gain is not None:
        in_specs.append(whole(gain))
        args.append(gain)
    if res is not None:
        in_specs.append(row(P * n))
        args.append(res)
    out_specs, out_shape = [row(P * n)], [jax.ShapeDtypeStruct((M, P * n), out_dtype)]
    if gain is not None:
        out_specs.append(row(K))
        out_shape.append(jax.ShapeDtypeStruct((M, K), BF16))
    if sgu is not None:
        in_specs += [whole(s) for s in sgu]
        args += list(sgu)
        out_specs += [row(P * n), row(P * n // 2)]
        out_shape += [jax.ShapeDtypeStruct((M, P * n), BF16), jax.ShapeDtypeStruct((M, P * n // 2), BF16)]
    carried = _Carried(comm, (M // tm,), last="parallel")
    outs, comm_results = carried.pallas_call(
        body, in_specs=in_specs, out_specs=out_specs, out_shape=out_shape, scratch_shapes=[],
        args=args, name=name)
    outs = tuple(outs) if n_out > 1 else outs[0]
    return outs if comm is None else (outs, comm_results)


def _mm_nt(a, w, l, *, out_dtype, name, act_src=None, tm=512, tn=1024):
    M, K = a.shape
    _, P, Nout, kc = w.shape
    assert P == 1 and K == kc
    tm, tn = min(tm, M), min(tn, Nout)
    assert M % tm == 0 and Nout % tn == 0

    def body(*refs):
        a_ref, w_ref, o_ref = refs[0], refs[1], refs[-1]
        av = a_ref[...].astype(BF16)
        for n0 in range(0, Nout, tn):
            r = _dot_nt(av, w_ref[0, 0, n0:n0 + tn, :])
            if act_src is not None:
                r = r * (2.0 * jnp.maximum(refs[2][:, n0:n0 + tn].astype(F32), 0.0))
            o_ref[:, n0:n0 + tn] = r.astype(out_dtype)

    row = lambda width: pl.BlockSpec((tm, width), lambda i: (i, 0))
    in_specs = [row(K), pl.BlockSpec((1, 1, Nout, K), lambda i: (l, 0, 0, 0), pipeline_mode=pl.Buffered(1))]
    args = [a, w]
    if act_src is not None:
        in_specs.append(row(Nout))
        args.append(act_src)
    return pl.pallas_call(
        body, grid=(M // tm,), in_specs=in_specs, out_specs=row(Nout),
        out_shape=jax.ShapeDtypeStruct((M, Nout), out_dtype),
        compiler_params=_params("parallel"), name=name)(*args)


def _mm_tn(a, b, *, shards, name, a_act=None, tm=2048, tk=1024, pb=1):
    b_parts = list(b) if isinstance(b, (list, tuple)) else [b]
    M, K = a.shape
    N = sum(part.shape[1] for part in b_parts)
    assert all(part.shape[0] == M for part in b_parts)
    n = N // shards
    tm, tk = min(tm, M), min(tk, K)
    assert M % tm == 0 and K % tk == 0 and shards % pb == 0
    width, nm = pb * n, M // tm
    assert len(b_parts) == 1 or width == N

    def body(*refs):
        a_ref, b_refs, o_ref, acc = refs[0], refs[1:-2], refs[-2], refs[-1]
        m = pl.program_id(2)
        av = a_ref[...]
        if a_act == "relu2":
            av = _relu2(av)
        bv = [r[...].astype(BF16) for r in b_refs]

        @pl.when(m == 0)
        def _():
            acc[...] = jnp.zeros_like(acc)

        acc[...] += _dot_tn(av.astype(BF16), bv[0] if len(bv) == 1 else jnp.concatenate(bv, axis=1))

        @pl.when(m == nm - 1)
        def _():
            for p in range(pb):
                o_ref[p] = acc[:, p * n:(p + 1) * n].astype(BF16)

    if len(b_parts) == 1:
        b_specs = [pl.BlockSpec((tm, width), lambda i, j, m: (m, j))]
    else:
        b_specs = [pl.BlockSpec((tm, part.shape[1]), lambda i, j, m: (m, 0)) for part in b_parts]
    return pl.pallas_call(
        body, grid=(K // tk, N // width, nm),
        in_specs=[pl.BlockSpec((tm, tk), lambda i, j, m: (m, i))] + b_specs,
        out_specs=pl.BlockSpec((pb, tk, n), lambda i, j, m: (j, i, 0)),
        out_shape=jax.ShapeDtypeStruct((shards, K, n), BF16),
        scratch_shapes=[pltpu.VMEM((tk, width), F32)],
        compiler_params=_params("parallel", "parallel", "arbitrary"), name=name)(a, *b_parts)


def _mm_nt_rms_bwd(a, w, l, x, gain, dres, *, name, comm=None, tm=512):
    a_parts = list(a) if isinstance(a, (list, tuple)) else [a]
    na = len(a_parts)
    M, K = a_parts[0].shape[0], sum(part.shape[1] for part in a_parts)
    _, P, D, kc = w.shape
    assert K == P * kc and x.shape == (M, D)
    tm = min(tm, M)
    nr = M // tm
    join = 2 if kc % MXU_TILE and P % 2 == 0 else 1

    def body(*refs):
        ins, (dx_ref, dxb_ref, dg_ref), (acc,) = carried.split(refs, na + 4, 3, 1)
        a_refs, (w_ref, x_ref, g_ref, dres_ref) = ins[:na], ins[na:]
        i = pl.program_id(0)
        carried.emit(refs)
        av = a_refs[0][...] if na == 1 else jnp.concatenate([r[...] for r in a_refs], axis=1)
        dhv = None
        for p in range(0, P, join):
            wv = [w_ref[0, p + s] for s in range(join)]
            d = _dot_nt(av[:, p * kc:(p + join) * kc], wv[0] if join == 1 else jnp.concatenate(wv, axis=1))
            dhv = d if dhv is None else dhv + d
        xv = x_ref[...]
        r = lax.rsqrt(jnp.mean(xv * xv, axis=-1, keepdims=True) + EPS)
        xhat = xv * r
        dxhat = dhv * g_ref[...]
        dx = dres_ref[...] + r * (dxhat - xhat * jnp.mean(dxhat * xhat, axis=-1, keepdims=True))
        dx_ref[...] = dx
        dxb_ref[...] = dx.astype(BF16)
        part = jnp.sum((dhv * xhat).reshape(tm // SUBLANES, SUBLANES, D), axis=0)

        @pl.when(i == 0)
        def _():
            acc[...] = jnp.zeros_like(acc)

        acc[...] += part

        @pl.when(i == nr - 1)
        def _():
            dg_ref[...] = jnp.sum(acc[...], axis=0, keepdims=True)

    row = pl.BlockSpec((tm, D), lambda i: (i, 0))
    vec = pl.BlockSpec((1, D), lambda i: (0, 0))
    carried = _Carried(comm, (nr,))
    return carried.pallas_call(
        body,
        in_specs=[pl.BlockSpec((tm, part.shape[1]), lambda i: (i, 0)) for part in a_parts] + [
            pl.BlockSpec((1, P, D, kc), lambda i: (l, 0, 0, 0), pipeline_mode=pl.Buffered(1)),
            row, vec, row],
        out_specs=[row, row, vec],
        out_shape=[jax.ShapeDtypeStruct((M, D), F32), jax.ShapeDtypeStruct((M, D), BF16),
                   jax.ShapeDtypeStruct((1, D), F32)],
        scratch_shapes=[pltpu.VMEM((SUBLANES, D), F32)],
        args=a_parts + [w, x, gain, dres], name=name)


def _mlp_out_loss_head(a, w, l, res, gain, target, *, name, tr=512, kc=1024):
    T, K = a.shape
    D = w.shape[3]
    nr = T // tr

    def body(a_ref, w_ref, res_ref, g_ref, t_ref, dx_ref, dxb_ref, sq_ref, dg_ref, sq_acc, dg_acc):
        i = pl.program_id(0)
        xv = res_ref[...]
        for k0 in range(0, K, kc):
            xv = xv + _dot(_relu2(a_ref[:, k0:k0 + kc]), w_ref[0, 0, k0:k0 + kc, :])
        g = g_ref[...]
        r = lax.rsqrt(jnp.mean(xv * xv, axis=-1, keepdims=True) + EPS)
        xhat = xv * r
        err = xhat * g - t_ref[...]
        dy = err * (1.0 / D)
        dxhat = dy * g
        dx = r * (dxhat - xhat * jnp.mean(dxhat * xhat, axis=-1, keepdims=True))
        dx_ref[...] = dx
        dxb_ref[...] = dx.astype(BF16)
        sq = jnp.sum((err * err).reshape(tr // SUBLANES, SUBLANES, D), axis=0)
        dg = jnp.sum((dy * xhat).reshape(tr // SUBLANES, SUBLANES, D), axis=0)

        @pl.when(i == 0)
        def _():
            sq_acc[...] = sq
            dg_acc[...] = dg

        @pl.when(i > 0)
        def _():
            sq_acc[...] += sq
            dg_acc[...] += dg

        @pl.when(i == nr - 1)
        def _():
            sq_ref[...] = sq_acc[...]
            dg_ref[...] = jnp.sum(dg_acc[...], axis=0, keepdims=True)

    row = pl.BlockSpec((tr, D), lambda i: (i, 0))
    vec = pl.BlockSpec((1, D), lambda i: (0, 0))
    part = pl.BlockSpec((SUBLANES, D), lambda i: (0, 0))
    return pl.pallas_call(
        body, grid=(nr,),
        in_specs=[pl.BlockSpec((tr, K), lambda i: (i, 0)),
                  pl.BlockSpec((1, 1, K, D), lambda i: (l, 0, 0, 0), pipeline_mode=pl.Buffered(1)),
                  row, vec, row],
        out_specs=[row, row, part, vec],
        out_shape=[jax.ShapeDtypeStruct((T, D), F32), jax.ShapeDtypeStruct((T, D), BF16),
                   jax.ShapeDtypeStruct((SUBLANES, D), F32), jax.ShapeDtypeStruct((1, D), F32)],
        scratch_shapes=[pltpu.VMEM((SUBLANES, D), F32), pltpu.VMEM((SUBLANES, D), F32)],
        compiler_params=_params("arbitrary"), name=name)(a, w, res, gain, target)


def _head0_lanes():
    return lax.broadcasted_iota(jnp.int32, (1, LANES), 1) < SB_HEAD_DIM


def _stack_heads(x):
    zero = jnp.zeros_like(x)
    h0 = _head0_lanes()
    return jnp.concatenate([jnp.where(h0, x, zero), jnp.where(h0, zero, x)], axis=0)


def _unstack_heads(y, tq):
    return jnp.where(_head0_lanes(), y[:tq], y[tq:])


def _past_mask(tq, tk):
    row = lax.broadcasted_iota(jnp.int32, (2 * tq, tk), 0) & (tq - 1)
    col = lax.broadcasted_iota(jnp.int32, (2 * tq, tk), 1)
    return col < row


def _max_of(arrays):
    return functools.reduce(jnp.maximum, [jnp.max(a) for a in arrays])


def _sb_logs(z, past):
    minus_abs = lax.bitcast_convert_type(
        lax.bitcast_convert_type(z, jnp.uint32) | jnp.uint32(0x80000000), F32)
    log_beta = jnp.minimum(z, 0.0) - jnp.log(1.0 + jnp.exp(minus_abs))
    l = log_beta - z
    if past is not None:
        l = jnp.where(past, l, 0.0)
    return log_beta, l


def _suffix_matrix(tk):
    j = lax.broadcasted_iota(jnp.int32, (2 * tk, tk), 0) & (tk - 1)
    s = lax.broadcasted_iota(jnp.int32, (2 * tk, tk), 1)
    return (j > s).astype(BF16)


def _suffix_sum(x, u2, exact=True):
    if not exact:
        return _dot(x.astype(BF16), u2[:x.shape[1]])
    hi, lo = _split_bf16(x)
    return _dot(jnp.concatenate([hi, lo], axis=1), u2)


class _Carried:
    def __init__(self, comm, grid, last="arbitrary"):
        self.comm, self.grid, self.last = comm, grid, last
        self.n_op = len(comm.operands) if comm else 0
        self.n_tgt = len(comm.targets) if comm else 0

    def split(self, refs, n_in, n_out, n_scratch):
        self.n_in, self.n_out, self.n_scratch = n_in, n_out, n_scratch
        a = n_in + self.n_op
        b = a + n_out + self.n_tgt
        return refs[:n_in], refs[a:a + n_out], refs[b:b + n_scratch]

    def emit(self, refs):
        if self.comm is None:
            return
        step, n_steps = 0, 1
        for d, size in enumerate(self.grid):
            step = step * size + pl.program_id(d)
            n_steps *= size
        a = self.n_in + self.n_op + self.n_out
        self.comm.emit(step, n_steps, refs[self.n_in:self.n_in + len(self.comm.sources)],
                       refs[a:a + self.n_tgt], refs[a + self.n_tgt + self.n_scratch:])

    def pallas_call(self, body, *, in_specs, out_specs, out_shape, scratch_shapes, args, name):
        comm = self.comm
        n_in, n_out = len(in_specs), len(out_specs)
        aliases = {}
        if comm is not None:
            in_specs = in_specs + [ANY] * self.n_op
            out_specs = out_specs + [ANY] * self.n_tgt
            out_shape = out_shape + comm.out_shapes
            scratch_shapes = scratch_shapes + comm.sems
            args = args + comm.operands
            aliases = comm.aliases(n_in, n_out)
        sem = ("parallel",) * (len(self.grid) - 1) + (self.last,) if comm is None else ("arbitrary",) * len(self.grid)
        results = pl.pallas_call(
            body, grid=self.grid, in_specs=in_specs, out_specs=out_specs, out_shape=out_shape,
            scratch_shapes=scratch_shapes, input_output_aliases=aliases,
            compiler_params=_params(*sem), name=name)(*args)
        return results[:n_out], results[n_out:]


def _sb_fwd(qkv, *, batch, seq, name, comm=None, tq=SB_TILE, n_pre=SB_STRAIGHT, pairs=SB_PAIRS):
    T, D3 = qkv.shape
    D = D3 // 3
    nhp = D // LANES
    tk = tq
    nq = seq // tq
    scale = SB_HEAD_DIM ** -0.5
    assert 1 <= n_pre <= nq and nhp % pairs == 0

    def body(*refs):
        (q_ref, k_ref, v_ref), (o_ref,), (acc,) = carried.split(refs, 3, 1, 1)
        qi = pl.program_id(2)
        carried.emit(refs)
        qs = [_stack_heads(q_ref[:, lanes]) * scale for lanes in cols]
        past = _past_mask(tq, tk)
        u = _suffix_matrix(tk)

        def block(kb, g, c, diag):
            ks = pl.multiple_of(kb * tk, tk)
            z = _dot_nt(qs[g], k_ref[pl.ds(ks, tk), cols[g]])
            log_beta, l = _sb_logs(z, past if diag else None)
            arg = log_beta + _suffix_sum(l, u, exact=False)
            a = jnp.exp(arg if c is None else arg + c)
            if diag:
                a = jnp.where(past, a, 0.0)
            return _dot(a.astype(BF16), v_ref[pl.ds(ks, tk), cols[g]]), jnp.sum(l, axis=1, keepdims=True)

        def straight(n):
            o_sum, c = [None] * pairs, [None] * pairs
            for b in range(n):
                for g in range(pairs):
                    o_b, c_b = block(qi - b, g, c[g], b == 0)
                    o_sum[g] = o_b if b == 0 else o_sum[g] + o_b
                    c[g] = c_b if b == 0 else c[g] + c_b
            return o_sum, c

        def finish(o_sum):
            for g in range(pairs):
                o_ref[:, cols[g]] = _unstack_heads(o_sum[g], tq)

        for n in range(1, n_pre):
            @pl.when(qi == n - 1)
            def _(n=n):
                finish(straight(n)[0])

        @pl.when(qi >= n_pre - 1)
        def _():
            o_sum, c = straight(n_pre)
            for g in range(pairs):
                acc[g] = o_sum[g]

            def cond(st):
                kb, c = st
                return jnp.logical_and(kb >= 0, _max_of(c) > EXP_ZERO_BELOW)

            def step(st):
                kb, c = st
                new_c = []
                for g in range(pairs):
                    o_n, c_n = block(kb, g, c[g], False)
                    acc[g] += o_n
                    new_c.append(c[g] + c_n)
                return kb - 1, tuple(new_c)

            lax.while_loop(cond, step, (qi - n_pre, tuple(c)))
            finish([acc[g] for g in range(pairs)])

    cols = [slice(g * LANES, (g + 1) * LANES) for g in range(pairs)]
    width = pairs * LANES
    carried = _Carried(comm, (batch, nhp // pairs, nq))
    (o,), comm_results = carried.pallas_call(
        body,
        in_specs=[pl.BlockSpec((tq, width), lambda b, p, i: (b * nq + i, p)),
                  pl.BlockSpec((seq, width), lambda b, p, i: (b, nhp // pairs + p)),
                  pl.BlockSpec((seq, width), lambda b, p, i: (b, 2 * (nhp // pairs) + p))],
        out_specs=[pl.BlockSpec((tq, width), lambda b, p, i: (b * nq + i, p))],
        out_shape=[jax.ShapeDtypeStruct((T, D), F32)],
        scratch_shapes=[pltpu.VMEM((pairs, 2 * tq, LANES), F32)],
        args=[qkv, qkv, qkv], name=name)
    return o, comm_results


def _sb_bwd(qkv, o, do, *, batch, seq, name, comm=None, tq=SB_TILE, n_pre=SB_STRAIGHT, pairs=SB_PAIRS):
    T, D3 = qkv.shape
    D = D3 // 3
    nhp = D // LANES
    tk = tq
    nq = seq // tq
    scale = SB_HEAD_DIM ** -0.5

    def body(*refs):
        ins, outs, scratch = carried.split(refs, 5, 3, 3)
        q_ref, k_ref, v_ref, o_ref, do_ref = ins
        dq_ref, dk_ref, dv_ref = outs
        dq_acc, dk_acc, dv_acc = scratch
        qi = pl.program_id(2)
        carried.emit(refs)

        @pl.when(qi == 0)
        def _():
            dk_acc[...] = jnp.zeros_like(dk_acc)
            dv_acc[...] = jnp.zeros_like(dv_acc)

        qs = [_stack_heads(q_ref[:, lanes]) * scale for lanes in cols]
        dos = [_stack_heads(do_ref[:, lanes]) for lanes in cols]
        dsum = [jnp.sum(_stack_heads(do_ref[:, lanes].astype(F32) * o_ref[:, lanes]), axis=1, keepdims=True)
                for lanes in cols]
        past = _past_mask(tq, tk)
        u = _suffix_matrix(tk)

        def weights(kb, p, c, diag):
            ks = pl.multiple_of(kb * tk, tk)
            kblk = k_ref[pl.ds(ks, tk), cols[p]]
            z = _dot_nt(qs[p], kblk)
            log_beta, l = _sb_logs(z, past if diag else None)
            arg = log_beta + _suffix_sum(l, u, exact=False)
            a = jnp.exp(arg if c is None else arg + c)
            if diag:
                a = jnp.where(past, a, 0.0)
            a = a.astype(BF16)
            g = a.astype(F32) * _dot_nt(dos[p], v_ref[pl.ds(ks, tk), cols[p]])
            return ks, kblk, a, 1.0 - jnp.exp(l), g, jnp.sum(l, axis=1, keepdims=True)

        def scores_grad(p, ks, kblk, a, beta, g, prefix, diag):
            dz = g - beta * prefix
            if diag:
                dz = jnp.where(past, dz, 0.0)
            dzb = dz.astype(BF16)
            dk_acc[pl.ds(ks, tk), cols[p]] += _dot_tn(dzb, qs[p])
            dv_acc[pl.ds(ks, tk), cols[p]] += _dot_tn(a, dos[p])
            return _dot(dzb, kblk)

        def block(kb, p, c, gc):
            ks, kblk, a, beta, g, l_sum = weights(kb, p, c, False)
            prefix = dsum[p] - (_suffix_sum(g, u) + gc)
            return scores_grad(p, ks, kblk, a, beta, g, prefix, False), l_sum, jnp.sum(g, axis=1, keepdims=True)

        def finish(dq_sum):
            for p in range(pairs):
                dq_ref[:, cols[p]] = (_unstack_heads(dq_sum[p], tq) * scale).astype(BF16)

        def straight(n):
            dq_sum, c_all, g_all = [], [], []
            for p in range(pairs):
                blocks, c = [], None
                for b in range(n):
                    blk = weights(qi - b, p, c, b == 0)
                    c = blk[5] if b == 0 else c + blk[5]
                    g_bf = blk[4].astype(BF16)
                    suffix = _suffix_sum(g_bf, u, exact=False)
                    rounded_sum = suffix[:, :1] + g_bf[:, :1].astype(F32)
                    blocks.append((blk, suffix, rounded_sum, jnp.sum(blk[4], axis=1, keepdims=True)))
                exact_sum = functools.reduce(lambda x, y: x + y, [e for _, _, _, e in blocks])
                total = dsum[p] - exact_sum + functools.reduce(lambda x, y: x + y, [r for _, _, r, _ in blocks])
                dq, right = None, None
                for b, ((ks, kblk, a, beta, g, _), suffix, rounded_sum, _) in enumerate(blocks):
                    prefix = total - (suffix if right is None else suffix + right)
                    d = scores_grad(p, ks, kblk, a, beta, g, prefix, b == 0)
                    dq = d if dq is None else dq + d
                    right = rounded_sum if right is None else right + rounded_sum
                dq_sum.append(dq)
                c_all.append(c)
                g_all.append(exact_sum)
            return dq_sum, c_all, g_all

        for n in range(1, n_pre):
            @pl.when(qi == n - 1)
            def _(n=n):
                finish(straight(n)[0])

        @pl.when(qi >= n_pre - 1)
        def _():
            dq_sum, c, gc = straight(n_pre)
            for p in range(pairs):
                dq_acc[p] = dq_sum[p]

            def cond(st):
                kb, c, gc = st
                return jnp.logical_and(kb >= 0, _max_of(c) > EXP_ZERO_BELOW)

            def step(st):
                kb, c, gc = st
                new_c, new_gc = [], []
                for p in range(pairs):
                    dq_n, c_n, g_n = block(kb, p, c[p], gc[p])
                    dq_acc[p] += dq_n
                    new_c.append(c[p] + c_n)
                    new_gc.append(gc[p] + g_n)
                return kb - 1, tuple(new_c), tuple(new_gc)

            lax.while_loop(cond, step, (qi - n_pre, tuple(c), tuple(gc)))
            finish([dq_acc[p] for p in range(pairs)])

        @pl.when(qi == nq - 1)
        def _():
            dk_ref[...] = dk_acc[...].astype(BF16)
            dv_ref[...] = dv_acc[...].astype(BF16)

    cols = [slice(p * LANES, (p + 1) * LANES) for p in range(pairs)]
    width, ncb = pairs * LANES, nhp // pairs
    qspec = pl.BlockSpec((tq, width), lambda b, p, i: (b * nq + i, p))
    sspec = pl.BlockSpec((seq, width), lambda b, p, i: (b, p))
    out = jax.ShapeDtypeStruct((T, D), BF16)
    carried = _Carried(comm, (batch, ncb, nq))
    return carried.pallas_call(
        body,
        in_specs=[qspec,
                  pl.BlockSpec((seq, width), lambda b, p, i: (b, ncb + p)),
                  pl.BlockSpec((seq, width), lambda b, p, i: (b, 2 * ncb + p)),
                  qspec, qspec],
        out_specs=[qspec, sspec, sspec], out_shape=[out, out, out],
        scratch_shapes=[pltpu.VMEM((pairs, 2 * tq, LANES), F32), pltpu.VMEM((seq, width), F32),
                        pltpu.VMEM((seq, width), F32)],
        args=[qkv, qkv, qkv, o, do], name=name)


_GELU_C = 0.7978845608028654
_GELU_A = 0.044715


def _gelu(x):
    x = x.astype(BF16)
    xx = x * x
    a1 = 1.0 + jnp.tanh(x * (_GELU_C + (_GELU_C * _GELU_A) * xx))
    hx = 0.5 * x
    grad = a1 * (0.5 + (hx * (2.0 - a1)) * (_GELU_C + (3.0 * _GELU_C * _GELU_A) * xx))
    return hx * a1, grad


def _causal_ws(ws_ref, g):
    t = lax.broadcasted_iota(jnp.int32, (SGU_CHUNK, SGU_CHUNK), 0)
    s = lax.broadcasted_iota(jnp.int32, (SGU_CHUNK, SGU_CHUNK), 1)
    return jnp.where(s <= t, ws_ref[g], 0.0)


def _sgu_gate(u, v, g_ref, ws_ref, bs_ref):
    G, W = SGU_GROUPS, SGU_GROUP_W
    r = lax.rsqrt(jnp.mean(v * v, axis=-1, keepdims=True) + EPS)
    vn = (v * r * g_ref[...]).astype(BF16)
    y = []
    for g in range(G):
        sl = slice(g * W, (g + 1) * W)
        mixed = _dot(_causal_ws(ws_ref, g).astype(BF16), vn[:, sl]) + bs_ref[:, g:g + 1]
        y.append((u[:, sl] * mixed).astype(BF16))
    return jnp.concatenate(y, axis=1)


def _sgu_bwd(uv, dgelu, dy, gain, ws, bst, *, name):
    T, F2 = uv.shape
    F = F2 // 2
    C, G, W = SGU_CHUNK, SGU_GROUPS, SGU_GROUP_W
    nc = T // C

    def body(uv_ref, dgelu_ref, dy_ref, g_ref, ws_ref, bs_ref, duv_ref, dg_ref, dws_ref, dbs_ref,
             dg_acc, dws_acc, dbs_acc):
        i = pl.program_id(0)

        @pl.when(i == 0)
        def _():
            dg_acc[...] = jnp.zeros_like(dg_acc)
            dws_acc[...] = jnp.zeros_like(dws_acc)
            dbs_acc[...] = jnp.zeros_like(dbs_acc)

        u, v = uv_ref[:, :F].astype(F32), uv_ref[:, F:].astype(F32)
        dgelu = dgelu_ref[...].astype(F32)
        r = lax.rsqrt(jnp.mean(v * v, axis=-1, keepdims=True) + EPS)
        vhat = v * r
        gain_v = g_ref[...]
        vn = (vhat * gain_v).astype(BF16)
        dyv = dy_ref[...].astype(F32)
        lane8 = lax.broadcasted_iota(jnp.int32, (1, G), 1)
        dvn_parts = []
        dbs_new = jnp.zeros((C, G), F32)
        for g in range(G):
            sl = slice(g * W, (g + 1) * W)
            wsg = _causal_ws(ws_ref, g)
            mixed = _dot(wsg.astype(BF16), vn[:, sl]) + bs_ref[:, g:g + 1]
            duv_ref[:, sl] = (dyv[:, sl] * mixed * dgelu[:, sl]).astype(BF16)
            dmix = dyv[:, sl] * u[:, sl]
            dbs_new = dbs_new + jnp.where(lane8 == g, jnp.sum(dmix, axis=1, keepdims=True), 0.0)
            dmix_b = dmix.astype(BF16)
            dws_acc[g] += _dot_nt(dmix_b, vn[:, sl])
            dvn_parts.append(_dot(wsg.T.astype(BF16), dmix_b))
        dbs_acc[...] += dbs_new
        dvn = jnp.concatenate(dvn_parts, axis=1)
        dg_acc[...] += jnp.sum((dvn * vhat).reshape(C // SUBLANES, SUBLANES, F), axis=0)
        dvhat = dvn * gain_v
        dv = r * (dvhat - vhat * jnp.mean(dvhat * vhat, axis=-1, keepdims=True))
        duv_ref[:, F:] = (dv * dgelu[:, F:]).astype(BF16)

        @pl.when(i == nc - 1)
        def _():
            dg_ref[...] = jnp.sum(dg_acc[...], axis=0, keepdims=True)
            t = lax.broadcasted_iota(jnp.int32, (G, C, C), 1)
            s = lax.broadcasted_iota(jnp.int32, (G, C, C), 2)
            dws_ref[...] = jnp.where(s <= t, dws_acc[...], 0.0)
            dbs_ref[...] = dbs_acc[...]

    return pl.pallas_call(
        body, grid=(nc,),
        in_specs=[pl.BlockSpec((C, F2), lambda i: (i, 0)), pl.BlockSpec((C, F2), lambda i: (i, 0)),
                  pl.BlockSpec((C, F), lambda i: (i, 0)),
                  pl.BlockSpec((1, F), lambda i: (0, 0)), pl.BlockSpec((G, C, C), lambda i: (0, 0, 0)),
                  pl.BlockSpec((C, G), lambda i: (0, 0))],
        out_specs=[pl.BlockSpec((C, F2), lambda i: (i, 0)), pl.BlockSpec((1, F), lambda i: (0, 0)),
                   pl.BlockSpec((G, C, C), lambda i: (0, 0, 0)), pl.BlockSpec((C, G), lambda i: (0, 0))],
        out_shape=[jax.ShapeDtypeStruct((T, F2), BF16), jax.ShapeDtypeStruct((1, F), F32),
                   jax.ShapeDtypeStruct((G, C, C), F32), jax.ShapeDtypeStruct((C, G), F32)],
        scratch_shapes=[pltpu.VMEM((SUBLANES, F), F32), pltpu.VMEM((G, C, C), F32), pltpu.VMEM((C, G), F32)],
        compiler_params=_params("arbitrary"), name=name)(uv, dgelu, dy, gain, ws, bst)


def _my_place():
    return lax.axis_index("x"), lax.axis_index("y"), lax.axis_index("c")


def _all_gather(shards, *, name):
    nf = len(shards)
    items = [(lambda ins, f=f: ins[f], lambda outs, p, f=f: outs[f].at[:, p]) for f in range(nf)]
    targets = [jax.ShapeDtypeStruct((s.shape[0], N_DEV) + s.shape[1:], s.dtype) for s in shards]
    return _run_comm(_Comm(shards, targets, len(items), _gather_emit(items, [list(range(nf))])), name=name)


class _Comm:
    def __init__(self, sources, targets, n_items, emit):
        self.sources, self.targets, self.n_items, self.emit = list(sources), list(targets), n_items, emit
        self.filled = [t for t in self.targets if not isinstance(t, jax.ShapeDtypeStruct)]
        self.operands = self.sources + self.filled
        self.out_shapes = [jax.ShapeDtypeStruct(t.shape, t.dtype) for t in self.targets]
        self.sems = [pltpu.SemaphoreType.DMA((n_items, 7)), pltpu.SemaphoreType.DMA((n_items, 7)),
                     pltpu.SemaphoreType.DMA((n_items,))]

    def aliases(self, first_operand, first_result):
        pos = {id(t): k for k, t in enumerate(self.targets)}
        return {first_operand + len(self.sources) + a: first_result + pos[id(t)]
                for a, t in enumerate(self.filled)}


def _run_comm(comm, *, name):
    n_op, n_out = len(comm.operands), len(comm.targets)

    def body(*refs):
        comm.emit(0, 1, refs[:len(comm.sources)], refs[n_op:n_op + n_out], refs[n_op + n_out:])

    return pl.pallas_call(
        body, in_specs=[ANY] * n_op, out_specs=[ANY] * n_out, out_shape=comm.out_shapes,
        scratch_shapes=comm.sems, input_output_aliases=comm.aliases(0, 0), name=name)(*comm.operands)


def _at_steps(step, n_steps, phases):
    if n_steps == 1:
        for _, fn in phases:
            fn()
        return
    marks = {}
    for frac, fn in phases:
        marks.setdefault(min(int(frac * n_steps), n_steps - 1), []).append(fn)
    for mark, fns in sorted(marks.items()):
        @pl.when(step == mark)
        def _(fns=fns):
            for fn in fns:
                fn()


def _gather_emit(items, groups, fractions=None):
    if fractions is None:
        fractions = [(g + 1) / len(groups) for g in range(len(groups))]
    def emit(step, n_steps, ins, outs, sems):
        send_sems, recv_sems, local_sems = sems
        x, y, c = _my_place()
        me, sibling = (x, y, c), (x, y, 1 - c)
        chips = [(1 - x, y), (x, 1 - y), (1 - x, 1 - y)]

        def copy(i, k, block, to, own=False):
            src_of, dst_of = items[i]
            dst = dst_of(outs, 4 * block[0] + 2 * block[1] + block[2])
            return pltpu.make_async_remote_copy(
                src_ref=src_of(ins) if own else dst, dst_ref=dst,
                send_sem=send_sems.at[i, k], recv_sem=recv_sems.at[i, k],
                device_id=to, device_id_type=MESH)

        def local(i):
            src_of, dst_of = items[i]
            return pltpu.make_async_copy(src_of(ins), dst_of(outs, 4 * x + 2 * y + c), local_sems.at[i])

        def first(i):
            return [copy(i, 0, me, sibling, own=True)] + [
                copy(i, 1 + j, me, (*chip, c), own=True) for j, chip in enumerate(chips)]

        def start():
            for i in range(len(items)):
                local(i).start()
                for cp in first(i):
                    cp.start()

        def forward(group):
            for j, chip in enumerate(chips):
                for i in group:
                    copy(i, 1 + j, (*chip, c), me).wait_recv()
                    copy(i, 4 + j, (*chip, c), sibling).start()

        def finish():
            for i in range(len(items)):
                copy(i, 0, sibling, me).wait_recv()
                for j, chip in enumerate(chips):
                    copy(i, 4 + j, (*chip, 1 - c), me).wait_recv()
            for i in range(len(items)):
                for cp in first(i) + [copy(i, 4 + j, (*chip, c), sibling) for j, chip in enumerate(chips)]:
                    cp.wait_send()
                local(i).wait()

        phases = [(0.0, start)]
        for frac, group in zip(fractions, groups):
            phases.append((frac, functools.partial(forward, group)))
        phases.append((1.0, finish))
        _at_steps(step, n_steps, phases)

    return emit


def _exchange_emit(items):
    def emit(step, n_steps, ins, outs, sems):
        send_sems, recv_sems, local_sems = sems
        x, y, c = _my_place()
        me = 4 * x + 2 * y + c

        def peer_of(k):
            return x ^ (k >> 2), y ^ ((k >> 1) & 1), c ^ (k & 1)

        def copy(i, k):
            src_of, dst_of = items[i]
            px, py, pc = peer_of(k)
            return pltpu.make_async_remote_copy(
                src_ref=src_of(ins, 4 * px + 2 * py + pc), dst_ref=dst_of(outs, me),
                send_sem=send_sems.at[i, k - 1], recv_sem=recv_sems.at[i, k - 1],
                device_id=(px, py, pc), device_id_type=MESH)

        def arrival(i, k):
            src_of, dst_of = items[i]
            px, py, pc = peer_of(k)
            peer = 4 * px + 2 * py + pc
            return pltpu.make_async_remote_copy(
                src_ref=src_of(ins, peer), dst_ref=dst_of(outs, peer),
                send_sem=send_sems.at[i, k - 1], recv_sem=recv_sems.at[i, k - 1],
                device_id=(x, y, c), device_id_type=MESH)

        def local(i):
            src_of, dst_of = items[i]
            return pltpu.make_async_copy(src_of(ins, me), dst_of(outs, me), local_sems.at[i])

        def start():
            for i in range(len(items)):
                local(i).start()
            for k in range(1, N_DEV):
                for i in range(len(items)):
                    copy(i, k).start()

        def finish():
            for k in range(1, N_DEV):
                for i in range(len(items)):
                    arrival(i, k).wait_recv()
            for k in range(1, N_DEV):
                for i in range(len(items)):
                    copy(i, k).wait_send()
            for i in range(len(items)):
                local(i).wait()

        _at_steps(step, n_steps, [(0.0, start), (1.0, finish)])

    return emit


def _adam_math(g, w, m, v):
    m = ADAM_B1 * m + (1.0 - ADAM_B1) * g
    v = ADAM_B2 * v + (1.0 - ADAM_B2) * (g * g)
    m_hat = m / (1.0 - ADAM_B1 ** ADAM_STEP)
    v_hat = v / (1.0 - ADAM_B2 ** ADAM_STEP)
    delta = -ADAM_LR * (m_hat / (jnp.sqrt(v_hat) + ADAM_EPS) + ADAM_WD * w)
    return delta, m, v


def _sum_adamw(parts, w, m, v, *, name, tr=256):
    L, nd, R, C = parts.shape
    tr = min(tr, R)
    assert R % tr == 0

    def body(p_ref, w_ref, m_ref, v_ref, g_ref, d_ref, nm_ref, nv_ref):
        g = p_ref[0, 0].astype(F32)
        for q in range(1, nd):
            g = g + p_ref[0, q].astype(F32)
        d, nm, nv = _adam_math(g, w_ref[0], m_ref[0], v_ref[0])
        g_ref[0] = g
        d_ref[0] = d
        nm_ref[0] = nm
        nv_ref[0] = nv

    blk = pl.BlockSpec((1, tr, C), lambda l, i: (l, i, 0))
    out = jax.ShapeDtypeStruct((L, R, C), F32)
    return pl.pallas_call(
        body, grid=(L, R // tr),
        in_specs=[pl.BlockSpec((1, nd, tr, C), lambda l, i: (l, 0, i, 0)), blk, blk, blk],
        out_specs=[blk] * 4, out_shape=[out] * 4,
        compiler_params=_params("parallel", "parallel"), name=name)(parts, w, m, v)


def _sum_parts(parts, *, name):
    nd, R, C = parts.shape

    def body(p_ref, o_ref):
        g = p_ref[0]
        for q in range(1, nd):
            g = g + p_ref[q]
        o_ref[...] = g

    return pl.pallas_call(
        body, out_shape=jax.ShapeDtypeStruct((R, C), F32),
        in_specs=[pl.BlockSpec(memory_space=pltpu.VMEM)],
        out_specs=pl.BlockSpec(memory_space=pltpu.VMEM), name=name)(parts)


def _adamw_small(g, w, m, v, *, name):
    def body(g_ref, w_ref, m_ref, v_ref, d_ref, nm_ref, nv_ref):
        d, nm, nv = _adam_math(g_ref[...], w_ref[...], m_ref[...], v_ref[...])
        d_ref[...] = d
        nm_ref[...] = nm
        nv_ref[...] = nv

    vm = pl.BlockSpec(memory_space=pltpu.VMEM)
    out = jax.ShapeDtypeStruct(g.shape, F32)
    return pl.pallas_call(body, out_shape=[out] * 3, in_specs=[vm] * 4, out_specs=[vm] * 3,
                          name=name)(g, w, m, v)


def kernel(x, norm_mix, norm_mlp, sb_wqkv, sb_wo, sgu_win, sgu_gain, sgu_ws, sgu_bs, sgu_wout, mlp_w1, mlp_w2, final_norm, loss_target, m_norm_mix, m_norm_mlp, m_sb_wqkv, m_sb_wo, m_sgu_win, m_sgu_gain, m_sgu_ws, m_sgu_bs, m_sgu_wout, m_mlp_w1, m_mlp_w2, m_final_norm, v_norm_mix, v_norm_mlp, v_sb_wqkv, v_sb_wo, v_sgu_win, v_sgu_gain, v_sgu_ws, v_sgu_bs, v_sgu_wout, v_mlp_w1, v_mlp_w2, v_final_norm):
    batch, seq, D = x.shape
    T = batch * seq
    x0 = x.reshape(T, D)
    target = loss_target.reshape(T, D)

    WQKV, WO, WIN, WOUT, W1, W2, GAIN = range(7)
    big = [sb_wqkv, sb_wo, sgu_win, sgu_wout, mlp_w1, mlp_w2]
    shards = [w.astype(BF16) for w in big] + [sgu_gain[:, None, :]]
    rides = {
        "qkv0": [[(WO, 0), (W1, 0)]],
        "sb_fwd0": [[(WIN, 0), (WOUT, 0), (W1, 1), (W2, 1)], [(WQKV, 1)]],
        "w1_0": [[(W2, 0)]],
        "qkv2": [[(WO, 1), (W1, 2)]],
        "sb_fwd2": [[(W2, 2)], [(WIN, 1), (WOUT, 1), (W1, 3), (W2, 3)]],
    }

    first = [(lambda ins: ins[WQKV].at[0], lambda outs, p: outs[WQKV].at[0, p]),
             (lambda ins: ins[GAIN], lambda outs, p: outs[GAIN].at[:, p])]
    targets = [jax.ShapeDtypeStruct((s.shape[0], N_DEV) + s.shape[1:], s.dtype) for s in shards]
    gathered0 = _run_comm(_Comm(shards, targets, len(first), _gather_emit(first, [[0, 1]])), name="gather_first")
    gain_sgu = gathered0[GAIN].reshape(-1, 1, SGU_FFN)
    gw = dict(enumerate(gathered0[:GAIN]))

    def gather_ride(call):
        wanted = [fl for group in rides[call] for fl in group]
        fams = sorted({f for f, _ in wanted})
        items = [(lambda ins, f=f, l=l: ins[f].at[l], lambda outs, p, t=fams.index(f), l=l: outs[t].at[l, p])
                 for f, l in wanted]
        groups, sent, k = [], [], 0
        for group in rides[call]:
            groups.append(list(range(k, k + len(group))))
            sent.append(sum(shards[f].shape[1] * shards[f].shape[2] for f, _ in group) + (sent[-1] if sent else 0))
            k += len(group)
        fractions = [s / sent[-1] for s in sent]
        return _Comm(shards[:GAIN], [gw[f] for f in fams], len(items), _gather_emit(items, groups, fractions)), fams

    def weight(f):
        g = gw[f]
        return g if f in (WQKV, WIN, W1) else g.reshape(g.shape[0], 1, N_DEV * g.shape[2], g.shape[3])

    saved = []
    xs = x0
    for i in range(DEPTH):
        j = i // 2
        if i % 2 == 0:
            comm, fams = gather_ride(f"qkv{i}")
            (qkv, h), results = _mm_nn(xs, weight(WQKV), j, out_dtype=BF16, gain=norm_mix[i:i + 1],
                                       name=f"qkv{i}", comm=comm, tm=1024)
            gw.update(zip(fams, results))
            comm, fams = gather_ride(f"sb_fwd{i}")
            o, results = _sb_fwd(qkv, batch=batch, seq=seq, name=f"sb_fwd{i}", comm=comm)
            gw.update(zip(fams, results))
            x_mid = _mm_nn(o, weight(WO), j, out_dtype=F32, res=xs, name=f"wo{i}", tm=1024)
            mix = (h, qkv, o)
        else:
            gain_j = gain_sgu[j]
            bst = sgu_bs[j].T
            uv, h, dgelu, yv = _mm_nn(xs, weight(WIN), j, out_dtype=BF16, gain=norm_mix[i:i + 1],
                                      sgu=(gain_j, sgu_ws[j], bst), name=f"win_sgu{i}")
            x_mid = _mm_nn(yv, weight(WOUT), j, out_dtype=F32, res=xs, name=f"wout{i}", tm=1024)
            mix = (h, uv, dgelu, yv, gain_j, bst)
        if f"w1_{i}" in rides:
            comm, fams = gather_ride(f"w1_{i}")
            (a, h2), results = _mm_nn(x_mid, weight(W1), i, out_dtype=BF16, gain=norm_mlp[i:i + 1],
                                      name=f"w1_{i}", comm=comm, tm=1024)
            gw.update(zip(fams, results))
        else:
            a, h2 = _mm_nn(x_mid, weight(W1), i, out_dtype=BF16, gain=norm_mlp[i:i + 1], name=f"w1_{i}", tm=1024)
        saved.append((xs, mix, x_mid, h2, a))
        if i < DEPTH - 1:
            xs = _mm_nn(a, weight(W2), i, out_dtype=F32, res=x_mid, a_act="relu2", name=f"w2_{i}")
    g_wqkv, g_wo, g_win, g_wout, g_w1, g_w2 = [weight(f) for f in range(GAIN)]

    dx, dxb, sq, d_final = _mlp_out_loss_head(a, g_w2, DEPTH - 1, x_mid, final_norm.reshape(1, D), target,
                                              name="w2_loss_head")
    loss = lax.psum(0.5 * jnp.sum(sq) / D, ("x", "y", "c"))

    SMALL = GAIN
    stacks = {f: jax.ShapeDtypeStruct((w.shape[0], N_DEV) + w.shape[1:], BF16) for f, w in enumerate(big)}
    pending = []
    second_half = []

    def row_shards(p):
        return p.reshape(N_DEV, p.shape[1] // N_DEV, p.shape[2])

    def exchange(going):
        fams = sorted({f for _, f, _, _ in going})

        def item(a, f, l, rows):
            rows = slice(None) if rows is None else slice(*rows)
            return (lambda ins, p: ins[a].at[p, rows], lambda outs, q: outs[fams.index(f)].at[l, q, rows])

        items = [item(a, f, l, rows) for a, (_, f, l, rows) in enumerate(going)]
        return _Comm([g[0] for g in going], [stacks[f] for f in fams], len(items), _exchange_emit(items)), fams

    def take(entries):
        going = list(entries)
        entries.clear()
        return going

    d_norm_mix, d_norm_mlp = [None] * DEPTH, [None] * DEPTH
    d_gain, d_ws, d_bs = [None] * 2, [None] * 2, [None] * 2
    for i in reversed(range(DEPTH)):
        j = i // 2
        xs, mix, x_mid, h2, a = saved[i]
        da = _mm_nt(dxb, g_w2, i, out_dtype=BF16, act_src=a, name=f"d_a{i}")
        pending.append((row_shards(_mm_tn(a, dxb, shards=1, a_act="relu2", name=f"d_w2_{i}")), W2, i, None))
        pending.append((_mm_tn(h2, da, shards=N_DEV, name=f"d_w1_{i}", pb=4, tm=1024), W1, i, None))
        comm, fams = exchange(take(second_half)) if second_half else (None, [])
        (dx, dxb, d_norm_mlp[i]), results = _mm_nt_rms_bwd(
            da, g_w1, i, x_mid, norm_mlp[i:i + 1], dx, name=f"d_h2_{i}", comm=comm)
        stacks.update(zip(fams, results))
        if i % 2 == 0:
            h, qkv, o = mix
            do = _mm_nt(dxb, g_wo, j, out_dtype=BF16, name=f"d_o{i}", tm=1024)
            pending.append((row_shards(_mm_tn(o, dxb, shards=1, name=f"d_wo{i}")), WO, j, None))
            comm, fams = exchange(take(pending))
            (dq, dk, dv), results = _sb_bwd(qkv, o, do, batch=batch, seq=seq, name=f"sb_bwd{i}", comm=comm)
            stacks.update(zip(fams, results))
            mixer_in, w_in, f_in = [dq, dk, dv], g_wqkv, WQKV
            d_w_in = _mm_tn(h, mixer_in, shards=N_DEV, name=f"d_wqkv{i}", pb=N_DEV, tm=1024, tk=512)
        else:
            h, uv, dgelu, yv, gain_j, bst = mix
            dy = _mm_nt(dxb, g_wout, j, out_dtype=BF16, name=f"d_y{i}", tm=1024)
            pending.append((row_shards(_mm_tn(yv, dxb, shards=1, name=f"d_wout{i}")), WOUT, j, None))
            mixer_in, d_gain[j], d_ws[j], dbst = _sgu_bwd(uv, dgelu, dy, gain_j, sgu_ws[j], bst, name=f"sgu_bwd{i}")
            d_bs[j] = dbst.T
            w_in, f_in = g_win, WIN
            d_w_in = _mm_tn(h, mixer_in, shards=N_DEV, name=f"d_win{i}", pb=4, tm=1024)
        rows = d_w_in.shape[1]
        if i > 0:
            second_half.append((d_w_in, f_in, j, (rows // 2, rows)))
            comm, fams = exchange([(d_w_in, f_in, j, (0, rows // 2))])
        else:
            comm, fams = exchange([(d_w_in, f_in, j, None)])
        (dx, dxb, d_norm_mix[i]), results = _mm_nt_rms_bwd(
            mixer_in, w_in, j, xs, norm_mix[i:i + 1], dx, name=f"d_h_mix{i}", comm=comm)
        stacks.update(zip(fams, results))
    grad_x = dx.reshape(batch, seq, D)

    small = [jnp.concatenate(d_norm_mix, 0), jnp.concatenate(d_norm_mlp, 0), d_final,
             jnp.concatenate(d_gain, 0), jnp.stack(d_bs, 0), jnp.stack(d_ws, 0)]
    small_flat = jnp.concatenate([s.reshape(-1) for s in small])
    n_small = small_flat.shape[0]
    small_rows = -(-n_small // (N_DEV * SUBLANES * LANES)) * SUBLANES
    small_flat = jnp.pad(small_flat, (0, N_DEV * small_rows * LANES - n_small))
    stacks[SMALL] = jax.ShapeDtypeStruct((1, N_DEV, small_rows, LANES), F32)
    pending.append((small_flat.reshape(N_DEV, small_rows, LANES), SMALL, 0, None))
    comm, fams = exchange(take(pending))
    stacks.update(zip(fams, _run_comm(comm, name="exchange_last")))
    r_wqkv, r_wo, r_win, r_wout, r_w1, r_w2, r_small = [stacks[f] for f in range(SMALL + 1)]

    u_wqkv = _sum_adamw(r_wqkv, sb_wqkv, m_sb_wqkv, v_sb_wqkv, name="adamw_wqkv")
    u_wo = _sum_adamw(r_wo, sb_wo, m_sb_wo, v_sb_wo, name="adamw_wo")
    u_win = _sum_adamw(r_win, sgu_win, m_sgu_win, v_sgu_win, name="adamw_win")
    u_wout = _sum_adamw(r_wout, sgu_wout, m_sgu_wout, v_sgu_wout, name="adamw_wout")
    u_w1 = _sum_adamw(r_w1, mlp_w1, m_mlp_w1, v_mlp_w1, name="adamw_w1")
    u_w2 = _sum_adamw(r_w2, mlp_w2, m_mlp_w2, v_mlp_w2, name="adamw_w2")

    small_sum = _sum_parts(r_small[0], name="sum_small")
    g_small = _all_gather([small_sum[None]], name="gather_small")[0].reshape(-1)[:n_small]

    shapes = [s.shape for s in small]
    sizes = [s.size for s in small]
    offs = [sum(sizes[:k]) for k in range(len(sizes))]
    me = 4 * lax.axis_index("x") + 2 * lax.axis_index("y") + lax.axis_index("c")
    shard_w = SGU_FFN // N_DEV

    def pack(arrs):
        flat = jnp.concatenate([a_.reshape(-1) for a_ in arrs])
        return jnp.pad(flat, (0, N_DEV * small_rows * LANES - n_small)).reshape(-1, LANES)

    def full_gain(gshard):
        return lax.dynamic_update_slice(jnp.zeros((2, SGU_FFN), F32), gshard, (0, me * shard_w))

    w_small = pack([norm_mix, norm_mlp, final_norm, full_gain(sgu_gain), sgu_bs, sgu_ws])
    m_small = pack([m_norm_mix, m_norm_mlp, m_final_norm, full_gain(m_sgu_gain), m_sgu_bs, m_sgu_ws])
    v_small = pack([v_norm_mix, v_norm_mlp, v_final_norm, full_gain(v_sgu_gain), v_sgu_bs, v_sgu_ws])
    g_pack = jnp.pad(g_small, (0, N_DEV * small_rows * LANES - n_small)).reshape(-1, LANES)
    sm = [g_pack] + list(_adamw_small(g_pack, w_small, m_small, v_small, name="adamw_small"))

    def unpack(flat2d):
        flat = flat2d.reshape(-1)
        out = [flat[offs[k]:offs[k] + sizes[k]].reshape(shapes[k]) for k in range(len(sizes))]
        out[2] = out[2].reshape(D)
        out[3] = lax.dynamic_slice(out[3], (0, me * shard_w), (2, shard_w))
        return out

    outs = []
    for k, big_u in enumerate(zip(u_wqkv, u_wo, u_win, u_wout, u_w1, u_w2)):
        s_nm, s_nl, s_fn, s_gain, s_bs, s_ws = unpack(sm[k])
        b_wqkv, b_wo, b_win, b_wout, b_w1, b_w2 = big_u
        outs += [s_nm, s_nl, b_wqkv, b_wo, b_win, s_gain, s_ws, s_bs, b_wout, b_w1, b_w2, s_fn]
    return (loss, grad_x, *outs)
```

```python
import functools

import jax
import jax.numpy as jnp
from jax import lax
from jax.experimental import pallas as pl
from jax.experimental.pallas import tpu as pltpu

F32 = jnp.float32
BF16 = jnp.bfloat16

N_DEV = 8
D_MODEL = 1024
SEQ = 2048
DEPTH = 4
SB_HEAD_DIM = 64
SGU_CHUNK = 128
SGU_FFN = 2 * D_MODEL
SGU_GROUPS = 8
SGU_GROUP_W = SGU_FFN // SGU_GROUPS
EPS = 1e-6

ADAM_LR = 0.001
ADAM_B1 = 0.9
ADAM_B2 = 0.999
ADAM_EPS = 1e-08
ADAM_WD = 0.01
ADAM_STEP = 10

MXU_TILE = 256
LANES = 128
SUBLANES = 8
VMEM_LIMIT = 56 * 1024 * 1024
EXP_ZERO_BELOW = -104.0
SB_TILE = 256
SB_STRAIGHT = 2
SB_PAIRS = 4

MESH = pl.DeviceIdType.MESH
ANY = pl.BlockSpec(memory_space=pl.ANY)


def _params(*sem):
    return pltpu.CompilerParams(dimension_semantics=sem, vmem_limit_bytes=VMEM_LIMIT)


def _dot(a, b):
    return jnp.dot(a, b, preferred_element_type=F32)


def _dot_nt(a, b):
    return lax.dot_general(a, b, (((1,), (1,)), ((), ())), preferred_element_type=F32)


def _dot_tn(a, b):
    return lax.dot_general(a, b, (((0,), (0,)), ((), ())), preferred_element_type=F32)


def _split_bf16(x):
    hi = x.astype(BF16)
    lo = (x - hi.astype(F32)).astype(BF16)
    return hi, lo


def _relu2(av):
    t = jnp.maximum(av, jnp.zeros_like(av))
    return t * t


def _mm_nn(a, w, l, *, out_dtype, name, res=None, a_act=None, gain=None, sgu=None, comm=None,
           tm=512, kc=1024):
    M, K = a.shape
    _, P, K2, n = w.shape
    assert K2 == K
    tm, kc = min(tm, M), min(kc, K)
    assert M % tm == 0 and K % kc == 0
    join = 2 if n % MXU_TILE and P % 2 == 0 else 1

    n_main = 2 + (gain is not None) + (res is not None)
    n_in = n_main + (3 if sgu is not None else 0)
    n_out = 1 + (gain is not None) + (2 if sgu is not None else 0)
    gelu = sgu is not None

    def body(*all_refs):
        ins, outs, _ = carried.split(all_refs, n_in, n_out, 0)
        carried.emit(all_refs)
        refs = list(ins) + list(outs)
        a_ref, w_ref, o_ref = refs[0], refs[1], refs[n_in]
        if gain is not None:
            xv = a_ref[...]
            r = lax.rsqrt(jnp.mean(xv * xv, axis=-1, keepdims=True) + EPS)
            h = (xv * r * refs[2][...]).astype(BF16)
            refs[n_in + 1][...] = h
        for p in range(0, P, join):
            sl = slice(p * n, (p + join) * n)
            acc = None
            for k0 in range(0, K, kc):
                if gain is not None:
                    av = h[:, k0:k0 + kc]
                else:
                    av = a_ref[:, k0:k0 + kc]
                    av = _relu2(av) if a_act == "relu2" else av.astype(BF16)
                wv = [w_ref[0, p + s, k0:k0 + kc, :] for s in range(join)]
                d = _dot(av, wv[0] if join == 1 else jnp.concatenate(wv, axis=1))
                acc = d if acc is None else acc + d
            if res is not None:
                acc = acc + refs[n_main - 1][:, sl]
            if gelu:
                acc, dact = _gelu(acc)
                refs[-2][:, sl] = dact.astype(BF16)
            o_ref[:, sl] = acc.astype(out_dtype)
        if sgu is not None:
            half = P * n // 2
            for r0 in range(0, tm, SGU_CHUNK):
                rows = slice(r0, r0 + SGU_CHUNK)
                refs[-1][rows, :] = _sgu_gate(o_ref[rows, :half].astype(F32), o_ref[rows, half:].astype(F32),
                                              *refs[n_main:n_in])

    row = lambda width: pl.BlockSpec((tm, width), lambda i: (i, 0))
    whole = lambda arr: pl.BlockSpec(arr.shape, lambda i: (0,) * arr.ndim)
    in_specs = [row(K), pl.BlockSpec((1, P, K, n), lambda i: (l, 0, 0, 0), pipeline_mode=pl.Buffered(1))]
    args = [a, w]
    if gain is not None:
        in_specs.append(whole(gain))
        args.append(gain)
    if res is not None:
        in_specs.append(row(P * n))
        args.append(res)
    out_specs, out_shape = [row(P * n)], [jax.ShapeDtypeStruct((M, P * n), out_dtype)]
    if gain is not None:
        out_specs.append(row(K))
        out_shape.append(jax.ShapeDtypeStruct((M, K), BF16))
    if sgu is not None:
        in_specs += [whole(s) for s in sgu]
        args += list(sgu)
        out_specs += [row(P * n), row(P * n // 2)]
        out_shape += [jax.ShapeDtypeStruct((M, P * n), BF16), jax.ShapeDtypeStruct((M, P * n // 2), BF16)]
    carried = _Carried(comm, (M // tm,), last="parallel")
    outs, comm_results = carried.pallas_call(
        body, in_specs=in_specs, out_specs=out_specs, out_shape=out_shape, scratch_shapes=[],
        args=args, name=name)
    outs = tuple(outs) if n_out > 1 else outs[0]
    return outs if comm is None else (outs, comm_results)


def _mm_nt(a, w, l, *, out_dtype, name, act_src=None, tm=512, tn=1024):
    M, K = a.shape
    _, P, Nout, kc = w.shape
    assert P == 1 and K == kc
    tm, tn = min(tm, M), min(tn, Nout)
    assert M % tm == 0 and Nout % tn == 0

    def body(*refs):
        a_ref, w_ref, o_ref = refs[0], refs[1], refs[-1]
        av = a_ref[...].astype(BF16)
        for n0 in range(0, Nout, tn):
            r = _dot_nt(av, w_ref[0, 0, n0:n0 + tn, :])
            if act_src is not None:
                r = r * (2.0 * jnp.maximum(refs[2][:, n0:n0 + tn].astype(F32), 0.0))
            o_ref[:, n0:n0 + tn] = r.astype(out_dtype)

    row = lambda width: pl.BlockSpec((tm, width), lambda i: (i, 0))
    in_specs = [row(K), pl.BlockSpec((1, 1, Nout, K), lambda i: (l, 0, 0, 0), pipeline_mode=pl.Buffered(1))]
    args = [a, w]
    if act_src is not None:
        in_specs.append(row(Nout))
        args.append(act_src)
    return pl.pallas_call(
        body, grid=(M // tm,), in_specs=in_specs, out_specs=row(Nout),
        out_shape=jax.ShapeDtypeStruct((M, Nout), out_dtype),
        compiler_params=_params("parallel"), name=name)(*args)


def _mm_tn(a, b, *, shards, name, a_act=None, tm=2048, tk=1024, pb=1):
    b_parts = list(b) if isinstance(b, (list, tuple)) else [b]
    M, K = a.shape
    N = sum(part.shape[1] for part in b_parts)
    assert all(part.shape[0] == M for part in b_parts)
    n = N // shards
    tm, tk = min(tm, M), min(tk, K)
    assert M % tm == 0 and K % tk == 0 and shards % pb == 0
    width, nm = pb * n, M // tm
    assert len(b_parts) == 1 or width == N

    def body(*refs):
        a_ref, b_refs, o_ref, acc = refs[0], refs[1:-2], refs[-2], refs[-1]
        m = pl.program_id(2)
        av = a_ref[...]
        if a_act == "relu2":
            av = _relu2(av)
        bv = [r[...].astype(BF16) for r in b_refs]

        @pl.when(m == 0)
        def _():
            acc[...] = jnp.zeros_like(acc)

        acc[...] += _dot_tn(av.astype(BF16), bv[0] if len(bv) == 1 else jnp.concatenate(bv, axis=1))

        @pl.when(m == nm - 1)
        def _():
            for p in range(pb):
                o_ref[p] = acc[:, p * n:(p + 1) * n].astype(BF16)

    if len(b_parts) == 1:
        b_specs = [pl.BlockSpec((tm, width), lambda i, j, m: (m, j))]
    else:
        b_specs = [pl.BlockSpec((tm, part.shape[1]), lambda i, j, m: (m, 0)) for part in b_parts]
    return pl.pallas_call(
        body, grid=(K // tk, N // width, nm),
        in_specs=[pl.BlockSpec((tm, tk), lambda i, j, m: (m, i))] + b_specs,
        out_specs=pl.BlockSpec((pb, tk, n), lambda i, j, m: (j, i, 0)),
        out_shape=jax.ShapeDtypeStruct((shards, K, n), BF16),
        scratch_shapes=[pltpu.VMEM((tk, width), F32)],
        compiler_params=_params("parallel", "parallel", "arbitrary"), name=name)(a, *b_parts)


def _mm_nt_rms_bwd(a, w, l, x, gain, dres, *, name, comm=None, tm=512):
    a_parts = list(a) if isinstance(a, (list, tuple)) else [a]
    na = len(a_parts)
    M, K = a_parts[0].shape[0], sum(part.shape[1] for part in a_parts)
    _, P, D, kc = w.shape
    assert K == P * kc and x.shape == (M, D)
    tm = min(tm, M)
    nr = M // tm
    join = 2 if kc % MXU_TILE and P % 2 == 0 else 1

    def body(*refs):
        ins, (dx_ref, dxb_ref, dg_ref), (acc,) = carried.split(refs, na + 4, 3, 1)
        a_refs, (w_ref, x_ref, g_ref, dres_ref) = ins[:na], ins[na:]
        i = pl.program_id(0)
        carried.emit(refs)
        av = a_refs[0][...] if na == 1 else jnp.concatenate([r[...] for r in a_refs], axis=1)
        dhv = None
        for p in range(0, P, join):
            wv = [w_ref[0, p + s] for s in range(join)]
            d = _dot_nt(av[:, p * kc:(p + join) * kc], wv[0] if join == 1 else jnp.concatenate(wv, axis=1))
            dhv = d if dhv is None else dhv + d
        xv = x_ref[...]
        r = lax.rsqrt(jnp.mean(xv * xv, axis=-1, keepdims=True) + EPS)
        xhat = xv * r
        dxhat = dhv * g_ref[...]
        dx = dres_ref[...] + r * (dxhat - xhat * jnp.mean(dxhat * xhat, axis=-1, keepdims=True))
        dx_ref[...] = dx
        dxb_ref[...] = dx.astype(BF16)
        part = jnp.sum((dhv * xhat).reshape(tm // SUBLANES, SUBLANES, D), axis=0)

        @pl.when(i == 0)
        def _():
            acc[...] = jnp.zeros_like(acc)

        acc[...] += part

        @pl.when(i == nr - 1)
        def _():
            dg_ref[...] = jnp.sum(acc[...], axis=0, keepdims=True)

    row = pl.BlockSpec((tm, D), lambda i: (i, 0))
    vec = pl.BlockSpec((1, D), lambda i: (0, 0))
    carried = _Carried(comm, (nr,))
    return carried.pallas_call(
        body,
        in_specs=[pl.BlockSpec((tm, part.shape[1]), lambda i: (i, 0)) for part in a_parts] + [
            pl.BlockSpec((1, P, D, kc), lambda i: (l, 0, 0, 0), pipeline_mode=pl.Buffered(1)),
            row, vec, row],
        out_specs=[row, row, vec],
        out_shape=[jax.ShapeDtypeStruct((M, D), F32), jax.ShapeDtypeStruct((M, D), BF16),
                   jax.ShapeDtypeStruct((1, D), F32)],
        scratch_shapes=[pltpu.VMEM((SUBLANES, D), F32)],
        args=a_parts + [w, x, gain, dres], name=name)


def _mlp_out_loss_head(a, w, l, res, gain, target, *, name, tr=512, kc=1024):
    T, K = a.shape
    D = w.shape[3]
    nr = T // tr

    def body(a_ref, w_ref, res_ref, g_ref, t_ref, dx_ref, dxb_ref, sq_ref, dg_ref, sq_acc, dg_acc):
        i = pl.program_id(0)
        xv = res_ref[...]
        for k0 in range(0, K, kc):
            xv = xv + _dot(_relu2(a_ref[:, k0:k0 + kc]), w_ref[0, 0, k0:k0 + kc, :])
        g = g_ref[...]
        r = lax.rsqrt(jnp.mean(xv * xv, axis=-1, keepdims=True) + EPS)
        xhat = xv * r
        err = xhat * g - t_ref[...]
        dy = err * (1.0 / D)
        dxhat = dy * g
        dx = r * (dxhat - xhat * jnp.mean(dxhat * xhat, axis=-1, keepdims=True))
        dx_ref[...] = dx
        dxb_ref[...] = dx.astype(BF16)
        sq = jnp.sum((err * err).reshape(tr // SUBLANES, SUBLANES, D), axis=0)
        dg = jnp.sum((dy * xhat).reshape(tr // SUBLANES, SUBLANES, D), axis=0)

        @pl.when(i == 0)
        def _():
            sq_acc[...] = sq
            dg_acc[...] = dg

        @pl.when(i > 0)
        def _():
            sq_acc[...] += sq
            dg_acc[...] += dg

        @pl.when(i == nr - 1)
        def _():
            sq_ref[...] = sq_acc[...]
            dg_ref[...] = jnp.sum(dg_acc[...], axis=0, keepdims=True)

    row = pl.BlockSpec((tr, D), lambda i: (i, 0))
    vec = pl.BlockSpec((1, D), lambda i: (0, 0))
    part = pl.BlockSpec((SUBLANES, D), lambda i: (0, 0))
    return pl.pallas_call(
        body, grid=(nr,),
        in_specs=[pl.BlockSpec((tr, K), lambda i: (i, 0)),
                  pl.BlockSpec((1, 1, K, D), lambda i: (l, 0, 0, 0), pipeline_mode=pl.Buffered(1)),
                  row, vec, row],
        out_specs=[row, row, part, vec],
        out_shape=[jax.ShapeDtypeStruct((T, D), F32), jax.ShapeDtypeStruct((T, D), BF16),
                   jax.ShapeDtypeStruct((SUBLANES, D), F32), jax.ShapeDtypeStruct((1, D), F32)],
        scratch_shapes=[pltpu.VMEM((SUBLANES, D), F32), pltpu.VMEM((SUBLANES, D), F32)],
        compiler_params=_params("arbitrary"), name=name)(a, w, res, gain, target)


def _head0_lanes():
    return lax.broadcasted_iota(jnp.int32, (1, LANES), 1) < SB_HEAD_DIM


def _stack_heads(x):
    zero = jnp.zeros_like(x)
    h0 = _head0_lanes()
    return jnp.concatenate([jnp.where(h0, x, zero), jnp.where(h0, zero, x)], axis=0)


def _unstack_heads(y, tq):
    return jnp.where(_head0_lanes(), y[:tq], y[tq:])


def _past_mask(tq, tk):
    row = lax.broadcasted_iota(jnp.int32, (2 * tq, tk), 0) & (tq - 1)
    col = lax.broadcasted_iota(jnp.int32, (2 * tq, tk), 1)
    return col < row


def _max_of(arrays):
    return functools.reduce(jnp.maximum, [jnp.max(a) for a in arrays])


def _sb_logs(z, past):
    minus_abs = lax.bitcast_convert_type(
        lax.bitcast_convert_type(z, jnp.uint32) | jnp.uint32(0x80000000), F32)
    log_beta = jnp.minimum(z, 0.0) - jnp.log(1.0 + jnp.exp(minus_abs))
    l = log_beta - z
    if past is not None:
        l = jnp.where(past, l, 0.0)
    return log_beta, l


def _suffix_matrix(tk):
    j = lax.broadcasted_iota(jnp.int32, (2 * tk, tk), 0) & (tk - 1)
    s = lax.broadcasted_iota(jnp.int32, (2 * tk, tk), 1)
    return (j > s).astype(BF16)


def _suffix_sum(x, u2, exact=True):
    if not exact:
        return _dot(x.astype(BF16), u2[:x.shape[1]])
    hi, lo = _split_bf16(x)
    return _dot(jnp.concatenate([hi, lo], axis=1), u2)


class _Carried:
    def __init__(self, comm, grid, last="arbitrary"):
        self.comm, self.grid, self.last = comm, grid, last
        self.n_op = len(comm.operands) if comm else 0
        self.n_tgt = len(comm.targets) if comm else 0

    def split(self, refs, n_in, n_out, n_scratch):
        self.n_in, self.n_out, self.n_scratch = n_in, n_out, n_scratch
        a = n_in + self.n_op
        b = a + n_out + self.n_tgt
        return refs[:n_in], refs[a:a + n_out], refs[b:b + n_scratch]

    def emit(self, refs):
        if self.comm is None:
            return
        step, n_steps = 0, 1
        for d, size in enumerate(self.grid):
            step = step * size + pl.program_id(d)
            n_steps *= size
        a = self.n_in + self.n_op + self.n_out
        self.comm.emit(step, n_steps, refs[self.n_in:self.n_in + len(self.comm.sources)],
                       refs[a:a + self.n_tgt], refs[a + self.n_tgt + self.n_scratch:])

    def pallas_call(self, body, *, in_specs, out_specs, out_shape, scratch_shapes, args, name):
        comm = self.comm
        n_in, n_out = len(in_specs), len(out_specs)
        aliases = {}
        if comm is not None:
            in_specs = in_specs + [ANY] * self.n_op
            out_specs = out_specs + [ANY] * self.n_tgt
            out_shape = out_shape + comm.out_shapes
            scratch_shapes = scratch_shapes + comm.sems
            args = args + comm.operands
            aliases = comm.aliases(n_in, n_out)
        sem = ("parallel",) * (len(self.grid) - 1) + (self.last,) if comm is None else ("arbitrary",) * len(self.grid)
        results = pl.pallas_call(
            body, grid=self.grid, in_specs=in_specs, out_specs=out_specs, out_shape=out_shape,
            scratch_shapes=scratch_shapes, input_output_aliases=aliases,
            compiler_params=_params(*sem), name=name)(*args)
        return results[:n_out], results[n_out:]


def _sb_fwd(qkv, *, batch, seq, name, comm=None, tq=SB_TILE, n_pre=SB_STRAIGHT, pairs=SB_PAIRS):
    T, D3 = qkv.shape
    D = D3 // 3
    nhp = D // LANES
    tk = tq
    nq = seq // tq
    scale = SB_HEAD_DIM ** -0.5
    assert 1 <= n_pre <= nq and nhp % pairs == 0

    def body(*refs):
        (q_ref, k_ref, v_ref), (o_ref,), (acc,) = carried.split(refs, 3, 1, 1)
        qi = pl.program_id(2)
        carried.emit(refs)
        qs = [_stack_heads(q_ref[:, lanes]) * scale for lanes in cols]
        past = _past_mask(tq, tk)
        u = _suffix_matrix(tk)

        def block(kb, g, c, diag):
            ks = pl.multiple_of(kb * tk, tk)
            z = _dot_nt(qs[g], k_ref[pl.ds(ks, tk), cols[g]])
            log_beta, l = _sb_logs(z, past if diag else None)
            arg = log_beta + _suffix_sum(l, u, exact=False)
            a = jnp.exp(arg if c is None else arg + c)
            if diag:
                a = jnp.where(past, a, 0.0)
            return _dot(a.astype(BF16), v_ref[pl.ds(ks, tk), cols[g]]), jnp.sum(l, axis=1, keepdims=True)

        def straight(n):
            o_sum, c = [None] * pairs, [None] * pairs
            for b in range(n):
                for g in range(pairs):
                    o_b, c_b = block(qi - b, g, c[g], b == 0)
                    o_sum[g] = o_b if b == 0 else o_sum[g] + o_b
                    c[g] = c_b if b == 0 else c[g] + c_b
            return o_sum, c

        def finish(o_sum):
            for g in range(pairs):
                o_ref[:, cols[g]] = _unstack_heads(o_sum[g], tq)

        for n in range(1, n_pre):
            @pl.when(qi == n - 1)
            def _(n=n):
                finish(straight(n)[0])

        @pl.when(qi >= n_pre - 1)
        def _():
            o_sum, c = straight(n_pre)
            for g in range(pairs):
                acc[g] = o_sum[g]

            def cond(st):
                kb, c = st
                return jnp.logical_and(kb >= 0, _max_of(c) > EXP_ZERO_BELOW)

            def step(st):
                kb, c = st
                new_c = []
                for g in range(pairs):
                    o_n, c_n = block(kb, g, c[g], False)
                    acc[g] += o_n
                    new_c.append(c[g] + c_n)
                return kb - 1, tuple(new_c)

            lax.while_loop(cond, step, (qi - n_pre, tuple(c)))
            finish([acc[g] for g in range(pairs)])

    cols = [slice(g * LANES, (g + 1) * LANES) for g in range(pairs)]
    width = pairs * LANES
    carried = _Carried(comm, (batch, nhp // pairs, nq))
    (o,), comm_results = carried.pallas_call(
        body,
        in_specs=[pl.BlockSpec((tq, width), lambda b, p, i: (b * nq + i, p)),
                  pl.BlockSpec((seq, width), lambda b, p, i: (b, nhp // pairs + p)),
                  pl.BlockSpec((seq, width), lambda b, p, i: (b, 2 * (nhp // pairs) + p))],
        out_specs=[pl.BlockSpec((tq, width), lambda b, p, i: (b * nq + i, p))],
        out_shape=[jax.ShapeDtypeStruct((T, D), F32)],
        scratch_shapes=[pltpu.VMEM((pairs, 2 * tq, LANES), F32)],
        args=[qkv, qkv, qkv], name=name)
    return o, comm_results


def _sb_bwd(qkv, o, do, *, batch, seq, name, comm=None, tq=SB_TILE, n_pre=SB_STRAIGHT, pairs=SB_PAIRS):
    T, D3 = qkv.shape
    D = D3 // 3
    nhp = D // LANES
    tk = tq
    nq = seq // tq
    scale = SB_HEAD_DIM ** -0.5

    def body(*refs):
        ins, outs, scratch = carried.split(refs, 5, 3, 3)
        q_ref, k_ref, v_ref, o_ref, do_ref = ins
        dq_ref, dk_ref, dv_ref = outs
        dq_acc, dk_acc, dv_acc = scratch
        qi = pl.program_id(2)
        carried.emit(refs)

        @pl.when(qi == 0)
        def _():
            dk_acc[...] = jnp.zeros_like(dk_acc)
            dv_acc[...] = jnp.zeros_like(dv_acc)

        qs = [_stack_heads(q_ref[:, lanes]) * scale for lanes in cols]
        dos = [_stack_heads(do_ref[:, lanes]) for lanes in cols]
        dsum = [jnp.sum(_stack_heads(do_ref[:, lanes].astype(F32) * o_ref[:, lanes]), axis=1, keepdims=True)
                for lanes in cols]
        past = _past_mask(tq, tk)
        u = _suffix_matrix(tk)

        def weights(kb, p, c, diag):
            ks = pl.multiple_of(kb * tk, tk)
            kblk = k_ref[pl.ds(ks, tk), cols[p]]
            z = _dot_nt(qs[p], kblk)
            log_beta, l = _sb_logs(z, past if diag else None)
            arg = log_beta + _suffix_sum(l, u, exact=False)
            a = jnp.exp(arg if c is None else arg + c)
            if diag:
                a = jnp.where(past, a, 0.0)
            a = a.astype(BF16)
            g = a.astype(F32) * _dot_nt(dos[p], v_ref[pl.ds(ks, tk), cols[p]])
            return ks, kblk, a, 1.0 - jnp.exp(l), g, jnp.sum(l, axis=1, keepdims=True)

        def scores_grad(p, ks, kblk, a, beta, g, prefix, diag):
            dz = g - beta * prefix
            if diag:
                dz = jnp.where(past, dz, 0.0)
            dzb = dz.astype(BF16)
            dk_acc[pl.ds(ks, tk), cols[p]] += _dot_tn(dzb, qs[p])
            dv_acc[pl.ds(ks, tk), cols[p]] += _dot_tn(a, dos[p])
            return _dot(dzb, kblk)

        def block(kb, p, c, gc):
            ks, kblk, a, beta, g, l_sum = weights(kb, p, c, False)
            prefix = dsum[p] - (_suffix_sum(g, u) + gc)
            return scores_grad(p, ks, kblk, a, beta, g, prefix, False), l_sum, jnp.sum(g, axis=1, keepdims=True)

        def finish(dq_sum):
            for p in range(pairs):
                dq_ref[:, cols[p]] = (_unstack_heads(dq_sum[p], tq) * scale).astype(BF16)

        def straight(n):
            dq_sum, c_all, g_all = [], [], []
            for p in range(pairs):
                blocks, c = [], None
                for b in range(n):
                    blk = weights(qi - b, p, c, b == 0)
                    c = blk[5] if b == 0 else c + blk[5]
                    g_bf = blk[4].astype(BF16)
                    suffix = _suffix_sum(g_bf, u, exact=False)
                    rounded_sum = suffix[:, :1] + g_bf[:, :1].astype(F32)
                    blocks.append((blk, suffix, rounded_sum, jnp.sum(blk[4], axis=1, keepdims=True)))
                exact_sum = functools.reduce(lambda x, y: x + y, [e for _, _, _, e in blocks])
                total = dsum[p] - exact_sum + functools.reduce(lambda x, y: x + y, [r for _, _, r, _ in blocks])
                dq, right = None, None
                for b, ((ks, kblk, a, beta, g, _), suffix, rounded_sum, _) in enumerate(blocks):
                    prefix = total - (suffix if right is None else suffix + right)
                    d = scores_grad(p, ks, kblk, a, beta, g, prefix, b == 0)
                    dq = d if dq is None else dq + d
                    right = rounded_sum if right is None else right + rounded_sum
                dq_sum.append(dq)
                c_all.append(c)
                g_all.append(exact_sum)
            return dq_sum, c_all, g_all

        for n in range(1, n_pre):
            @pl.when(qi == n - 1)
            def _(n=n):
                finish(straight(n)[0])

        @pl.when(qi >= n_pre - 1)
        def _():
            dq_sum, c, gc = straight(n_pre)
            for p in range(pairs):
                dq_acc[p] = dq_sum[p]

            def cond(st):
                kb, c, gc = st
                return jnp.logical_and(kb >= 0, _max_of(c) > EXP_ZERO_BELOW)

            def step(st):
                kb, c, gc = st
                new_c, new_gc = [], []
                for p in range(pairs):
                    dq_n, c_n, g_n = block(kb, p, c[p], gc[p])
                    dq_acc[p] += dq_n
                    new_c.append(c[p] + c_n)
                    new_gc.append(gc[p] + g_n)
                return kb - 1, tuple(new_c), tuple(new_gc)

            lax.while_loop(cond, step, (qi - n_pre, tuple(c), tuple(gc)))
            finish([dq_acc[p] for p in range(pairs)])

        @pl.when(qi == nq - 1)
        def _():
            dk_ref[...] = dk_acc[...].astype(BF16)
            dv_ref[...] = dv_acc[...].astype(BF16)

    cols = [slice(p * LANES, (p + 1) * LANES) for p in range(pairs)]
    width, ncb = pairs * LANES, nhp // pairs
    qspec = pl.BlockSpec((tq, width), lambda b, p, i: (b * nq + i, p))
    sspec = pl.BlockSpec((seq, width), lambda b, p, i: (b, p))
    out = jax.ShapeDtypeStruct((T, D), BF16)
    carried = _Carried(comm, (batch, ncb, nq))
    return carried.pallas_call(
        body,
        in_specs=[qspec,
                  pl.BlockSpec((seq, width), lambda b, p, i: (b, ncb + p)),
                  pl.BlockSpec((seq, width), lambda b, p, i: (b, 2 * ncb + p)),
                  qspec, qspec],
        out_specs=[qspec, sspec, sspec], out_shape=[out, out, out],
        scratch_shapes=[pltpu.VMEM((pairs, 2 * tq, LANES), F32), pltpu.VMEM((seq, width), F32),
                        pltpu.VMEM((seq, width), F32)],
        args=[qkv, qkv, qkv, o, do], name=name)


_GELU_C = 0.7978845608028654
_GELU_A = 0.044715


def _gelu(x):
    x = x.astype(BF16)
    xx = x * x
    a1 = 1.0 + jnp.tanh(x * (_GELU_C + (_GELU_C * _GELU_A) * xx))
    hx = 0.5 * x
    grad = a1 * (0.5 + (hx * (2.0 - a1)) * (_GELU_C + (3.0 * _GELU_C * _GELU_A) * xx))
    return hx * a1, grad


def _causal_ws(ws_ref, g):
    t = lax.broadcasted_iota(jnp.int32, (SGU_CHUNK, SGU_CHUNK), 0)
    s = lax.broadcasted_iota(jnp.int32, (SGU_CHUNK, SGU_CHUNK), 1)
    return jnp.where(s <= t, ws_ref[g], 0.0)


def _sgu_gate(u, v, g_ref, ws_ref, bs_ref):
    G, W = SGU_GROUPS, SGU_GROUP_W
    r = lax.rsqrt(jnp.mean(v * v, axis=-1, keepdims=True) + EPS)
    vn = (v * r * g_ref[...]).astype(BF16)
    y = []
    for g in range(G):
        sl = slice(g * W, (g + 1) * W)
        mixed = _dot(_causal_ws(ws_ref, g).astype(BF16), vn[:, sl]) + bs_ref[:, g:g + 1]
        y.append((u[:, sl] * mixed).astype(BF16))
    return jnp.concatenate(y, axis=1)


def _sgu_bwd(uv, dgelu, dy, gain, ws, bst, *, name):
    T, F2 = uv.shape
    F = F2 // 2
    C, G, W = SGU_CHUNK, SGU_GROUPS, SGU_GROUP_W
    nc = T // C

    def body(uv_ref, dgelu_ref, dy_ref, g_ref, ws_ref, bs_ref, duv_ref, dg_ref, dws_ref, dbs_ref,
             dg_acc, dws_acc, dbs_acc):
        i = pl.program_id(0)

        @pl.when(i == 0)
        def _():
            dg_acc[...] = jnp.zeros_like(dg_acc)
            dws_acc[...] = jnp.zeros_like(dws_acc)
            dbs_acc[...] = jnp.zeros_like(dbs_acc)

        u, v = uv_ref[:, :F].astype(F32), uv_ref[:, F:].astype(F32)
        dgelu = dgelu_ref[...].astype(F32)
        r = lax.rsqrt(jnp.mean(v * v, axis=-1, keepdims=True) + EPS)
        vhat = v * r
        gain_v = g_ref[...]
        vn = (vhat * gain_v).astype(BF16)
        dyv = dy_ref[...].astype(F32)
        lane8 = lax.broadcasted_iota(jnp.int32, (1, G), 1)
        dvn_parts = []
        dbs_new = jnp.zeros((C, G), F32)
        for g in range(G):
            sl = slice(g * W, (g + 1) * W)
            wsg = _causal_ws(ws_ref, g)
            mixed = _dot(wsg.astype(BF16), vn[:, sl]) + bs_ref[:, g:g + 1]
            duv_ref[:, sl] = (dyv[:, sl] * mixed * dgelu[:, sl]).astype(BF16)
            dmix = dyv[:, sl] * u[:, sl]
            dbs_new = dbs_new + jnp.where(lane8 == g, jnp.sum(dmix, axis=1, keepdims=True), 0.0)
            dmix_b = dmix.astype(BF16)
            dws_acc[g] += _dot_nt(dmix_b, vn[:, sl])
            dvn_parts.append(_dot(wsg.T.astype(BF16), dmix_b))
        dbs_acc[...] += dbs_new
        dvn = jnp.concatenate(dvn_parts, axis=1)
        dg_acc[...] += jnp.sum((dvn * vhat).reshape(C // SUBLANES, SUBLANES, F), axis=0)
        dvhat = dvn * gain_v
        dv = r * (dvhat - vhat * jnp.mean(dvhat * vhat, axis=-1, keepdims=True))
        duv_ref[:, F:] = (dv * dgelu[:, F:]).astype(BF16)

        @pl.when(i == nc - 1)
        def _():
            dg_ref[...] = jnp.sum(dg_acc[...], axis=0, keepdims=True)
            t = lax.broadcasted_iota(jnp.int32, (G, C, C), 1)
            s = lax.broadcasted_iota(jnp.int32, (G, C, C), 2)
            dws_ref[...] = jnp.where(s <= t, dws_acc[...], 0.0)
            dbs_ref[...] = dbs_acc[...]

    return pl.pallas_call(
        body, grid=(nc,),
        in_specs=[pl.BlockSpec((C, F2), lambda i: (i, 0)), pl.BlockSpec((C, F2), lambda i: (i, 0)),
                  pl.BlockSpec((C, F), lambda i: (i, 0)),
                  pl.BlockSpec((1, F), lambda i: (0, 0)), pl.BlockSpec((G, C, C), lambda i: (0, 0, 0)),
                  pl.BlockSpec((C, G), lambda i: (0, 0))],
        out_specs=[pl.BlockSpec((C, F2), lambda i: (i, 0)), pl.BlockSpec((1, F), lambda i: (0, 0)),
                   pl.BlockSpec((G, C, C), lambda i: (0, 0, 0)), pl.BlockSpec((C, G), lambda i: (0, 0))],
        out_shape=[jax.ShapeDtypeStruct((T, F2), BF16), jax.ShapeDtypeStruct((1, F), F32),
                   jax.ShapeDtypeStruct((G, C, C), F32), jax.ShapeDtypeStruct((C, G), F32)],
        scratch_shapes=[pltpu.VMEM((SUBLANES, F), F32), pltpu.VMEM((G, C, C), F32), pltpu.VMEM((C, G), F32)],
        compiler_params=_params("arbitrary"), name=name)(uv, dgelu, dy, gain, ws, bst)


def _my_place():
    return lax.axis_index("x"), lax.axis_index("y"), lax.axis_index("c")


def _all_gather(shards, *, name):
    nf = len(shards)
    items = [(lambda ins, f=f: ins[f], lambda outs, p, f=f: outs[f].at[:, p]) for f in range(nf)]
    targets = [jax.ShapeDtypeStruct((s.shape[0], N_DEV) + s.shape[1:], s.dtype) for s in shards]
    return _run_comm(_Comm(shards, targets, len(items), _gather_emit(items, [list(range(nf))])), name=name)


class _Comm:
    def __init__(self, sources, targets, n_items, emit):
        self.sources, self.targets, self.n_items, self.emit = list(sources), list(targets), n_items, emit
        self.filled = [t for t in self.targets if not isinstance(t, jax.ShapeDtypeStruct)]
        self.operands = self.sources + self.filled
        self.out_shapes = [jax.ShapeDtypeStruct(t.shape, t.dtype) for t in self.targets]
        self.sems = [pltpu.SemaphoreType.DMA((n_items, 7)), pltpu.SemaphoreType.DMA((n_items, 7)),
                     pltpu.SemaphoreType.DMA((n_items,))]

    def aliases(self, first_operand, first_result):
        pos = {id(t): k for k, t in enumerate(self.targets)}
        return {first_operand + len(self.sources) + a: first_result + pos[id(t)]
                for a, t in enumerate(self.filled)}


def _run_comm(comm, *, name):
    n_op, n_out = len(comm.operands), len(comm.targets)

    def body(*refs):
        comm.emit(0, 1, refs[:len(comm.sources)], refs[n_op:n_op + n_out], refs[n_op + n_out:])

    return pl.pallas_call(
        body, in_specs=[ANY] * n_op, out_specs=[ANY] * n_out, out_shape=comm.out_shapes,
        scratch_shapes=comm.sems, input_output_aliases=comm.aliases(0, 0), name=name)(*comm.operands)


def _at_steps(step, n_steps, phases):
    if n_steps == 1:
        for _, fn in phases:
            fn()
        return
    marks = {}
    for frac, fn in phases:
        marks.setdefault(min(int(frac * n_steps), n_steps - 1), []).append(fn)
    for mark, fns in sorted(marks.items()):
        @pl.when(step == mark)
        def _(fns=fns):
            for fn in fns:
                fn()


def _gather_emit(items, groups, fractions=None):
    if fractions is None:
        fractions = [(g + 1) / len(groups) for g in range(len(groups))]
    def emit(step, n_steps, ins, outs, sems):
        send_sems, recv_sems, local_sems = sems
        x, y, c = _my_place()
        me, sibling = (x, y, c), (x, y, 1 - c)
        chips = [(1 - x, y), (x, 1 - y), (1 - x, 1 - y)]

        def copy(i, k, block, to, own=False):
            src_of, dst_of = items[i]
            dst = dst_of(outs, 4 * block[0] + 2 * block[1] + block[2])
            return pltpu.make_async_remote_copy(
                src_ref=src_of(ins) if own else dst, dst_ref=dst,
                send_sem=send_sems.at[i, k], recv_sem=recv_sems.at[i, k],
                device_id=to, device_id_type=MESH)

        def local(i):
            src_of, dst_of = items[i]
            return pltpu.make_async_copy(src_of(ins), dst_of(outs, 4 * x + 2 * y + c), local_sems.at[i])

        def first(i):
            return [copy(i, 0, me, sibling, own=True)] + [
                copy(i, 1 + j, me, (*chip, c), own=True) for j, chip in enumerate(chips)]

        def start():
            for i in range(len(items)):
                local(i).start()
                for cp in first(i):
                    cp.start()

        def forward(group):
            for j, chip in enumerate(chips):
                for i in group:
                    copy(i, 1 + j, (*chip, c), me).wait_recv()
                    copy(i, 4 + j, (*chip, c), sibling).start()

        def finish():
            for i in range(len(items)):
                copy(i, 0, sibling, me).wait_recv()
                for j, chip in enumerate(chips):
                    copy(i, 4 + j, (*chip, 1 - c), me).wait_recv()
            for i in range(len(items)):
                for cp in first(i) + [copy(i, 4 + j, (*chip, c), sibling) for j, chip in enumerate(chips)]:
                    cp.wait_send()
                local(i).wait()

        phases = [(0.0, start)]
        for frac, group in zip(fractions, groups):
            phases.append((frac, functools.partial(forward, group)))
        phases.append((1.0, finish))
        _at_steps(step, n_steps, phases)

    return emit


def _exchange_emit(items):
    def emit(step, n_steps, ins, outs, sems):
        send_sems, recv_sems, local_sems = sems
        x, y, c = _my_place()
        me = 4 * x + 2 * y + c

        def peer_of(k):
            return x ^ (k >> 2), y ^ ((k >> 1) & 1), c ^ (k & 1)

        def copy(i, k):
            src_of, dst_of = items[i]
            px, py, pc = peer_of(k)
            return pltpu.make_async_remote_copy(
                src_ref=src_of(ins, 4 * px + 2 * py + pc), dst_ref=dst_of(outs, me),
                send_sem=send_sems.at[i, k - 1], recv_sem=recv_sems.at[i, k - 1],
                device_id=(px, py, pc), device_id_type=MESH)

        def arrival(i, k):
            src_of, dst_of = items[i]
            px, py, pc = peer_of(k)
            peer = 4 * px + 2 * py + pc
            return pltpu.make_async_remote_copy(
                src_ref=src_of(ins, peer), dst_ref=dst_of(outs, peer),
                send_sem=send_sems.at[i, k - 1], recv_sem=recv_sems.at[i, k - 1],
                device_id=(x, y, c), device_id_type=MESH)

        def local(i):
            src_of, dst_of = items[i]
            return pltpu.make_async_copy(src_of(ins, me), dst_of(outs, me), local_sems.at[i])

        def start():
            for i in range(len(items)):
                local(i).start()
            for k in range(1, N_DEV):
                for i in range(len(items)):
                    copy(i, k).start()

        def finish():
            for k in range(1, N_DEV):
                for i in range(len(items)):
                    arrival(i, k).wait_recv()
            for k in range(1, N_DEV):
                for i in range(len(items)):
                    copy(i, k).wait_send()
            for i in range(len(items)):
                local(i).wait()

        _at_steps(step, n_steps, [(0.0, start), (1.0, finish)])

    return emit


def _adam_math(g, w, m, v):
    m = ADAM_B1 * m + (1.0 - ADAM_B1) * g
    v = ADAM_B2 * v + (1.0 - ADAM_B2) * (g * g)
    m_hat = m / (1.0 - ADAM_B1 ** ADAM_STEP)
    v_hat = v / (1.0 - ADAM_B2 ** ADAM_STEP)
    delta = -ADAM_LR * (m_hat / (jnp.sqrt(v_hat) + ADAM_EPS) + ADAM_WD * w)
    return delta, m, v


def _sum_adamw(parts, w, m, v, *, name, tr=256):
    L, nd, R, C = parts.shape
    tr = min(tr, R)
    assert R % tr == 0

    def body(p_ref, w_ref, m_ref, v_ref, g_ref, d_ref, nm_ref, nv_ref):
        g = p_ref[0, 0].astype(F32)
        for q in range(1, nd):
            g = g + p_ref[0, q].astype(F32)
        d, nm, nv = _adam_math(g, w_ref[0], m_ref[0], v_ref[0])
        g_ref[0] = g
        d_ref[0] = d
        nm_ref[0] = nm
        nv_ref[0] = nv

    blk = pl.BlockSpec((1, tr, C), lambda l, i: (l, i, 0))
    out = jax.ShapeDtypeStruct((L, R, C), F32)
    return pl.pallas_call(
        body, grid=(L, R // tr),
        in_specs=[pl.BlockSpec((1, nd, tr, C), lambda l, i: (l, 0, i, 0)), blk, blk, blk],
        out_specs=[blk] * 4, out_shape=[out] * 4,
        compiler_params=_params("parallel", "parallel"), name=name)(parts, w, m, v)


def _sum_parts(parts, *, name):
    nd, R, C = parts.shape

    def body(p_ref, o_ref):
        g = p_ref[0]
        for q in range(1, nd):
            g = g + p_ref[q]
        o_ref[...] = g

    return pl.pallas_call(
        body, out_shape=jax.ShapeDtypeStruct((R, C), F32),
        in_specs=[pl.BlockSpec(memory_space=pltpu.VMEM)],
        out_specs=pl.BlockSpec(memory_space=pltpu.VMEM), name=name)(parts)


def _adamw_small(g, w, m, v, *, name):
    def body(g_ref, w_ref, m_ref, v_ref, d_ref, nm_ref, nv_ref):
        d, nm, nv = _adam_math(g_ref[...], w_ref[...], m_ref[...], v_ref[...])
        d_ref[...] = d
        nm_ref[...] = nm
        nv_ref[...] = nv

    vm = pl.BlockSpec(memory_space=pltpu.VMEM)
    out = jax.ShapeDtypeStruct(g.shape, F32)
    return pl.pallas_call(body, out_shape=[out] * 3, in_specs=[vm] * 4, out_specs=[vm] * 3,
                          name=name)(g, w, m, v)


def kernel(x, norm_mix, norm_mlp, sb_wqkv, sb_wo, sgu_win, sgu_gain, sgu_ws, sgu_bs, sgu_wout, mlp_w1, mlp_w2, final_norm, loss_target, m_norm_mix, m_norm_mlp, m_sb_wqkv, m_sb_wo, m_sgu_win, m_sgu_gain, m_sgu_ws, m_sgu_bs, m_sgu_wout, m_mlp_w1, m_mlp_w2, m_final_norm, v_norm_mix, v_norm_mlp, v_sb_wqkv, v_sb_wo, v_sgu_win, v_sgu_gain, v_sgu_ws, v_sgu_bs, v_sgu_wout, v_mlp_w1, v_mlp_w2, v_final_norm):
    batch, seq, D = x.shape
    T = batch * seq
    x0 = x.reshape(T, D)
    target = loss_target.reshape(T, D)

    WQKV, WO, WIN, WOUT, W1, W2, GAIN = range(7)
    big = [sb_wqkv, sb_wo, sgu_win, sgu_wout, mlp_w1, mlp_w2]
    shards = [w.astype(BF16) for w in big] + [sgu_gain[:, None, :]]
    rides = {
        "qkv0": [[(WO, 0), (W1, 0)]],
        "sb_fwd0": [[(WIN, 0), (WOUT, 0), (W1, 1), (W2, 1)], [(WQKV, 1)]],
        "w1_0": [[(W2, 0)]],
        "qkv2": [[(WO, 1), (W1, 2)]],
        "sb_fwd2": [[(W2, 2)], [(WIN, 1), (WOUT, 1), (W1, 3), (W2, 3)]],
    }

    first = [(lambda ins: ins[WQKV].at[0], lambda outs, p: outs[WQKV].at[0, p]),
             (lambda ins: ins[GAIN], lambda outs, p: outs[GAIN].at[:, p])]
    targets = [jax.ShapeDtypeStruct((s.shape[0], N_DEV) + s.shape[1:], s.dtype) for s in shards]
    gathered0 = _run_comm(_Comm(shards, targets, len(first), _gather_emit(first, [[0, 1]])), name="gather_first")
    gain_sgu = gathered0[GAIN].reshape(-1, 1, SGU_FFN)
    gw = dict(enumerate(gathered0[:GAIN]))

    def gather_ride(call):
        wanted = [fl for group in rides[call] for fl in group]
        fams = sorted({f for f, _ in wanted})
        items = [(lambda ins, f=f, l=l: ins[f].at[l], lambda outs, p, t=fams.index(f), l=l: outs[t].at[l, p])
                 for f, l in wanted]
        groups, sent, k = [], [], 0
        for group in rides[call]:
            groups.append(list(range(k, k + len(group))))
            sent.append(sum(shards[f].shape[1] * shards[f].shape[2] for f, _ in group) + (sent[-1] if sent else 0))
            k += len(group)
        fractions = [s / sent[-1] for s in sent]
        return _Comm(shards[:GAIN], [gw[f] for f in fams], len(items), _gather_emit(items, groups, fractions)), fams

    def weight(f):
        g = gw[f]
        return g if f in (WQKV, WIN, W1) else g.reshape(g.shape[0], 1, N_DEV * g.shape[2], g.shape[3])

    saved = []
    xs = x0
    for i in range(DEPTH):
        j = i // 2
        if i % 2 == 0:
            comm, fams = gather_ride(f"qkv{i}")
            (qkv, h), results = _mm_nn(xs, weight(WQKV), j, out_dtype=BF16, gain=norm_mix[i:i + 1],
                                       name=f"qkv{i}", comm=comm, tm=1024)
            gw.update(zip(fams, results))
            comm, fams = gather_ride(f"sb_fwd{i}")
            o, results = _sb_fwd(qkv, batch=batch, seq=seq, name=f"sb_fwd{i}", comm=comm)
            gw.update(zip(fams, results))
            x_mid = _mm_nn(o, weight(WO), j, out_dtype=F32, res=xs, name=f"wo{i}", tm=1024)
            mix = (h, qkv, o)
        else:
            gain_j = gain_sgu[j]
            bst = sgu_bs[j].T
            uv, h, dgelu, yv = _mm_nn(xs, weight(WIN), j, out_dtype=BF16, gain=norm_mix[i:i + 1],
                                      sgu=(gain_j, sgu_ws[j], bst), name=f"win_sgu{i}")
            x_mid = _mm_nn(yv, weight(WOUT), j, out_dtype=F32, res=xs, name=f"wout{i}", tm=1024)
            mix = (h, uv, dgelu, yv, gain_j, bst)
        if f"w1_{i}" in rides:
            comm, fams = gather_ride(f"w1_{i}")
            (a, h2), results = _mm_nn(x_mid, weight(W1), i, out_dtype=BF16, gain=norm_mlp[i:i + 1],
                                      name=f"w1_{i}", comm=comm, tm=1024)
            gw.update(zip(fams, results))
        else:
            a, h2 = _mm_nn(x_mid, weight(W1), i, out_dtype=BF16, gain=norm_mlp[i:i + 1], name=f"w1_{i}", tm=1024)
        saved.append((xs, mix, x_mid, h2, a))
        if i < DEPTH - 1:
            xs = _mm_nn(a, weight(W2), i, out_dtype=F32, res=x_mid, a_act="relu2", name=f"w2_{i}", tm=1024)
    g_wqkv, g_wo, g_win, g_wout, g_w1, g_w2 = [weight(f) for f in range(GAIN)]

    dx, dxb, sq, d_final = _mlp_out_loss_head(a, g_w2, DEPTH - 1, x_mid, final_norm.reshape(1, D), target,
                                              name="w2_loss_head")

    SMALL = GAIN
    stacks = {f: jax.ShapeDtypeStruct((w.shape[0], N_DEV) + w.shape[1:], BF16) for f, w in enumerate(big)}
    pending = []
    second_half = []

    def row_shards(p):
        return p.reshape(N_DEV, p.shape[1] // N_DEV, p.shape[2])

    def exchange(going):
        fams = sorted({f for _, f, _, _ in going})

        def item(a, f, l, rows):
            rows = slice(None) if rows is None else slice(*rows)
            return (lambda ins, p: ins[a].at[p, rows], lambda outs, q: outs[fams.index(f)].at[l, q, rows])

        items = [item(a, f, l, rows) for a, (_, f, l, rows) in enumerate(going)]
        return _Comm([g[0] for g in going], [stacks[f] for f in fams], len(items), _exchange_emit(items)), fams

    def take(entries):
        going = list(entries)
        entries.clear()
        return going

    d_norm_mix, d_norm_mlp = [None] * DEPTH, [None] * DEPTH
    d_gain, d_ws, d_bs = [None] * 2, [None] * 2, [None] * 2
    for i in reversed(range(DEPTH)):
        j = i // 2
        xs, mix, x_mid, h2, a = saved[i]
        da = _mm_nt(dxb, g_w2, i, out_dtype=BF16, act_src=a, name=f"d_a{i}")
        pending.append((row_shards(_mm_tn(a, dxb, shards=1, a_act="relu2", name=f"d_w2_{i}")), W2, i, None))
        pending.append((_mm_tn(h2, da, shards=N_DEV, name=f"d_w1_{i}", pb=4, tm=1024), W1, i, None))
        comm, fams = exchange(take(second_half)) if second_half else (None, [])
        (dx, dxb, d_norm_mlp[i]), results = _mm_nt_rms_bwd(
            da, g_w1, i, x_mid, norm_mlp[i:i + 1], dx, name=f"d_h2_{i}", comm=comm)
        stacks.update(zip(fams, results))
        if i % 2 == 0:
            h, qkv, o = mix
            do = _mm_nt(dxb, g_wo, j, out_dtype=BF16, name=f"d_o{i}", tm=1024)
            pending.append((row_shards(_mm_tn(o, dxb, shards=1, name=f"d_wo{i}")), WO, j, None))
            comm, fams = exchange(take(pending))
            (dq, dk, dv), results = _sb_bwd(qkv, o, do, batch=batch, seq=seq, name=f"sb_bwd{i}", comm=comm)
            stacks.update(zip(fams, results))
            mixer_in, w_in, f_in = [dq, dk, dv], g_wqkv, WQKV
            d_w_in = _mm_tn(h, mixer_in, shards=N_DEV, name=f"d_wqkv{i}", pb=N_DEV, tm=1024, tk=512)
        else:
            h, uv, dgelu, yv, gain_j, bst = mix
            dy = _mm_nt(dxb, g_wout, j, out_dtype=BF16, name=f"d_y{i}", tm=1024)
            pending.append((row_shards(_mm_tn(yv, dxb, shards=1, name=f"d_wout{i}")), WOUT, j, None))
            mixer_in, d_gain[j], d_ws[j], dbst = _sgu_bwd(uv, dgelu, dy, gain_j, sgu_ws[j], bst, name=f"sgu_bwd{i}")
            d_bs[j] = dbst.T
            w_in, f_in = g_win, WIN
            d_w_in = _mm_tn(h, mixer_in, shards=N_DEV, name=f"d_win{i}", pb=4, tm=1024)
        rows = d_w_in.shape[1]
        if i > 0:
            second_half.append((d_w_in, f_in, j, (rows // 2, rows)))
            comm, fams = exchange([(d_w_in, f_in, j, (0, rows // 2))])
        else:
            comm, fams = exchange([(d_w_in, f_in, j, None)])
        (dx, dxb, d_norm_mix[i]), results = _mm_nt_rms_bwd(
            mixer_in, w_in, j, xs, norm_mix[i:i + 1], dx, name=f"d_h_mix{i}", comm=comm)
        stacks.update(zip(fams, results))
    grad_x = dx.reshape(batch, seq, D)

    small = [jnp.concatenate(d_norm_mix, 0), jnp.concatenate(d_norm_mlp, 0), d_final,
             jnp.concatenate(d_gain, 0), jnp.stack(d_bs, 0), jnp.stack(d_ws, 0)]
    small_flat = jnp.concatenate([s.reshape(-1) for s in small] + [(0.5 * jnp.sum(sq) / D).reshape(1)])
    n_small = small_flat.shape[0]
    small_rows = -(-n_small // (N_DEV * SUBLANES * LANES)) * SUBLANES
    small_size = N_DEV * small_rows * LANES
    small_flat = jnp.pad(small_flat, (0, small_size - n_small))
    stacks[SMALL] = jax.ShapeDtypeStruct((1, N_DEV, small_rows, LANES), F32)
    pending.append((small_flat.reshape(N_DEV, small_rows, LANES), SMALL, 0, None))
    comm, fams = exchange(take(pending))
    stacks.update(zip(fams, _run_comm(comm, name="exchange_last")))
    r_wqkv, r_wo, r_win, r_wout, r_w1, r_w2, r_small = [stacks[f] for f in range(SMALL + 1)]

    u_wqkv = _sum_adamw(r_wqkv, sb_wqkv, m_sb_wqkv, v_sb_wqkv, name="adamw_wqkv")
    u_wo = _sum_adamw(r_wo, sb_wo, m_sb_wo, v_sb_wo, name="adamw_wo")
    u_win = _sum_adamw(r_win, sgu_win, m_sgu_win, v_sgu_win, name="adamw_win")
    u_wout = _sum_adamw(r_wout, sgu_wout, m_sgu_wout, v_sgu_wout, name="adamw_wout")
    u_w1 = _sum_adamw(r_w1, mlp_w1, m_mlp_w1, v_mlp_w1, name="adamw_w1")
    u_w2 = _sum_adamw(r_w2, mlp_w2, m_mlp_w2, v_mlp_w2, name="adamw_w2")

    small_sum = _sum_parts(r_small[0], name="sum_small")
    g_small = _all_gather([small_sum[None]], name="gather_small")[0].reshape(-1)
    loss = g_small[n_small - 1]

    shapes = [s.shape for s in small]
    sizes = [s.size for s in small]
    offs = [sum(sizes[:k]) for k in range(len(sizes))]
    me = 4 * lax.axis_index("x") + 2 * lax.axis_index("y") + lax.axis_index("c")
    shard_w = SGU_FFN // N_DEV

    def pack(arrs):
        flat = jnp.concatenate([a_.reshape(-1) for a_ in arrs])
        return jnp.pad(flat, (0, small_size - flat.shape[0])).reshape(-1, LANES)

    def full_gain(gshard):
        return lax.dynamic_update_slice(jnp.zeros((2, SGU_FFN), F32), gshard, (0, me * shard_w))

    w_small = pack([norm_mix, norm_mlp, final_norm, full_gain(sgu_gain), sgu_bs, sgu_ws])
    m_small = pack([m_norm_mix, m_norm_mlp, m_final_norm, full_gain(m_sgu_gain), m_sgu_bs, m_sgu_ws])
    v_small = pack([v_norm_mix, v_norm_mlp, v_final_norm, full_gain(v_sgu_gain), v_sgu_bs, v_sgu_ws])
    g_pack = g_small.reshape(-1, LANES)
    sm = [g_pack] + list(_adamw_small(g_pack, w_small, m_small, v_small, name="adamw_small"))

    def unpack(flat2d):
        flat = flat2d.reshape(-1)
        out = [flat[offs[k]:offs[k] + sizes[k]].reshape(shapes[k]) for k in range(len(sizes))]
        out[2] = out[2].reshape(D)
        out[3] = lax.dynamic_slice(out[3], (0, me * shard_w), (2, shard_w))
        return out

    outs = []
    for k, big_u in enumerate(zip(u_wqkv, u_wo, u_win, u_wout, u_w1, u_w2)):
        s_nm, s_nl, s_fn, s_gain, s_bs, s_ws = unpack(sm[k])
        b_wqkv, b_wo, b_win, b_wout, b_w1, b_w2 = big_u
        outs += [s_nm, s_nl, b_wqkv, b_wo, b_win, s_gain, s_ws, s_bs, b_wout, b_w1, b_w2, s_fn]
    return (loss, grad_x, *outs)
```

```python
import functools

import jax
import jax.numpy as jnp
from jax import lax
from jax.experimental import pallas as pl
from jax.experimental.pallas import tpu as pltpu

F32 = jnp.float32
BF16 = jnp.bfloat16

N_DEV = 8
D_MODEL = 1024
SEQ = 2048
DEPTH = 4
SB_HEAD_DIM = 64
SGU_CHUNK = 128
SGU_FFN = 2 * D_MODEL
SGU_GROUPS = 8
SGU_GROUP_W = SGU_FFN // SGU_GROUPS
EPS = 1e-6

ADAM_LR = 0.001
ADAM_B1 = 0.9
ADAM_B2 = 0.999
ADAM_EPS = 1e-08
ADAM_WD = 0.01
ADAM_STEP = 10

MXU_TILE = 256
LANES = 128
SUBLANES = 8
VMEM_LIMIT = 56 * 1024 * 1024
EXP_ZERO_BELOW = -104.0
SB_TILE = 256
SB_STRAIGHT = 2
SB_PAIRS = 4

MESH = pl.DeviceIdType.MESH
ANY = pl.BlockSpec(memory_space=pl.ANY)


def _params(*sem):
    return pltpu.CompilerParams(dimension_semantics=sem, vmem_limit_bytes=VMEM_LIMIT)


def _dot(a, b):
    return jnp.dot(a, b, preferred_element_type=F32)


def _dot_nt(a, b):
    return lax.dot_general(a, b, (((1,), (1,)), ((), ())), preferred_element_type=F32)


def _dot_tn(a, b):
    return lax.dot_general(a, b, (((0,), (0,)), ((), ())), preferred_element_type=F32)


def _split_bf16(x):
    hi = x.astype(BF16)
    lo = (x - hi.astype(F32)).astype(BF16)
    return hi, lo


def _relu2(av):
    t = jnp.maximum(av, jnp.zeros_like(av))
    return t * t


def _mm_nn(a, w, l, *, out_dtype, name, res=None, a_act=None, gain=None, sgu=None, comm=None,
           tm=512, kc=1024):
    M, K = a.shape
    _, P, K2, n = w.shape
    assert K2 == K
    tm, kc = min(tm, M), min(kc, K)
    assert M % tm == 0 and K % kc == 0
    join = 2 if n % MXU_TILE and P % 2 == 0 else 1

    n_main = 2 + (gain is not None) + (res is not None)
    n_in = n_main + (3 if sgu is not None else 0)
    n_out = 1 + (gain is not None) + (2 if sgu is not None else 0)
    gelu = sgu is not None

    def body(*all_refs):
        ins, outs, _ = carried.split(all_refs, n_in, n_out, 0)
        carried.emit(all_refs)
        refs = list(ins) + list(outs)
        a_ref, w_ref, o_ref = refs[0], refs[1], refs[n_in]
        if gain is not None:
            xv = a_ref[...]
            r = lax.rsqrt(jnp.mean(xv * xv, axis=-1, keepdims=True) + EPS)
            h = (xv * r * refs[2][...]).astype(BF16)
            refs[n_in + 1][...] = h
        for p in range(0, P, join):
            sl = slice(p * n, (p + join) * n)
            acc = None
            for k0 in range(0, K, kc):
                if gain is not None:
                    av = h[:, k0:k0 + kc]
                else:
                    av = a_ref[:, k0:k0 + kc]
                    av = _relu2(av) if a_act == "relu2" else av.astype(BF16)
                wv = [w_ref[0, p + s, k0:k0 + kc, :] for s in range(join)]
                d = _dot(av, wv[0] if join == 1 else jnp.concatenate(wv, axis=1))
                acc = d if acc is None else acc + d
            if res is not None:
                acc = acc + refs[n_main - 1][:, sl]
            if gelu:
                acc, dact = _gelu(acc)
                refs[-2][:, sl] = dact.astype(BF16)
            o_ref[:, sl] = acc.astype(out_dtype)
        if sgu is not None:
            half = P * n // 2
            for r0 in range(0, tm, SGU_CHUNK):
                rows = slice(r0, r0 + SGU_CHUNK)
                refs[-1][rows, :] = _sgu_gate(o_ref[rows, :half].astype(F32), o_ref[rows, half:].astype(F32),
                                              *refs[n_main:n_in])

    row = lambda width: pl.BlockSpec((tm, width), lambda i: (i, 0))
    whole = lambda arr: pl.BlockSpec(arr.shape, lambda i: (0,) * arr.ndim)
    in_specs = [row(K), pl.BlockSpec((1, P, K, n), lambda i: (l, 0, 0, 0), pipeline_mode=pl.Buffered(1))]
    args = [a, w]
    if gain is not None:
        in_specs.append(whole(gain))
        args.append(gain)
    if res is not None:
        in_specs.append(row(P * n))
        args.append(res)
    out_specs, out_shape = [row(P * n)], [jax.ShapeDtypeStruct((M, P * n), out_dtype)]
    if gain is not None:
        out_specs.append(row(K))
        out_shape.append(jax.ShapeDtypeStruct((M, K), BF16))
    if sgu is not None:
        in_specs += [whole(s) for s in sgu]
        args += list(sgu)
        out_specs += [row(P * n), row(P * n // 2)]
        out_shape += [jax.ShapeDtypeStruct((M, P * n), BF16), jax.ShapeDtypeStruct((M, P * n // 2), BF16)]
    carried = _Carried(comm, (M // tm,), last="parallel")
    outs, comm_results = carried.pallas_call(
        body, in_specs=in_specs, out_specs=out_specs, out_shape=out_shape, scratch_shapes=[],
        args=args, name=name)
    outs = tuple(outs) if n_out > 1 else outs[0]
    return outs if comm is None else (outs, comm_results)


def _mm_nt(a, w, l, *, out_dtype, name, act_src=None, tm=512, tn=1024):
    M, K = a.shape
    _, P, Nout, kc = w.shape
    assert P == 1 and K == kc
    tm, tn = min(tm, M), min(tn, Nout)
    assert M % tm == 0 and Nout % tn == 0

    def body(*refs):
        a_ref, w_ref, o_ref = refs[0], refs[1], refs[-1]
        av = a_ref[...].astype(BF16)
        for n0 in range(0, Nout, tn):
            r = _dot_nt(av, w_ref[0, 0, n0:n0 + tn, :])
            if act_src is not None:
                r = r * (2.0 * jnp.maximum(refs[2][:, n0:n0 + tn].astype(F32), 0.0))
            o_ref[:, n0:n0 + tn] = r.astype(out_dtype)

    row = lambda width: pl.BlockSpec((tm, width), lambda i: (i, 0))
    in_specs = [row(K), pl.BlockSpec((1, 1, Nout, K), lambda i: (l, 0, 0, 0), pipeline_mode=pl.Buffered(1))]
    args = [a, w]
    if act_src is not None:
        in_specs.append(row(Nout))
        args.append(act_src)
    return pl.pallas_call(
        body, grid=(M // tm,), in_specs=in_specs, out_specs=row(Nout),
        out_shape=jax.ShapeDtypeStruct((M, Nout), out_dtype),
        compiler_params=_params("parallel"), name=name)(*args)


def _mm_tn(a, b, *, shards, name, a_act=None, tm=2048, tk=1024, pb=1):
    b_parts = list(b) if isinstance(b, (list, tuple)) else [b]
    M, K = a.shape
    N = sum(part.shape[1] for part in b_parts)
    assert all(part.shape[0] == M for part in b_parts)
    n = N // shards
    tm, tk = min(tm, M), min(tk, K)
    assert M % tm == 0 and K % tk == 0 and shards % pb == 0
    width, nm = pb * n, M // tm
    assert len(b_parts) == 1 or width == N

    def body(*refs):
        a_ref, b_refs, o_ref, acc = refs[0], refs[1:-2], refs[-2], refs[-1]
        m = pl.program_id(2)
        av = a_ref[...]
        if a_act == "relu2":
            av = _relu2(av)
        bv = [r[...].astype(BF16) for r in b_refs]

        @pl.when(m == 0)
        def _():
            acc[...] = jnp.zeros_like(acc)

        acc[...] += _dot_tn(av.astype(BF16), bv[0] if len(bv) == 1 else jnp.concatenate(bv, axis=1))

        @pl.when(m == nm - 1)
        def _():
            for p in range(pb):
                o_ref[p] = acc[:, p * n:(p + 1) * n].astype(BF16)

    if len(b_parts) == 1:
        b_specs = [pl.BlockSpec((tm, width), lambda i, j, m: (m, j))]
    else:
        b_specs = [pl.BlockSpec((tm, part.shape[1]), lambda i, j, m: (m, 0)) for part in b_parts]
    return pl.pallas_call(
        body, grid=(K // tk, N // width, nm),
        in_specs=[pl.BlockSpec((tm, tk), lambda i, j, m: (m, i))] + b_specs,
        out_specs=pl.BlockSpec((pb, tk, n), lambda i, j, m: (j, i, 0)),
        out_shape=jax.ShapeDtypeStruct((shards, K, n), BF16),
        scratch_shapes=[pltpu.VMEM((tk, width), F32)],
        compiler_params=_params("parallel", "parallel", "arbitrary"), name=name)(a, *b_parts)


def _mm_nt_rms_bwd(a, w, l, x, gain, dres, *, name, comm=None, tm=512):
    a_parts = list(a) if isinstance(a, (list, tuple)) else [a]
    na = len(a_parts)
    M, K = a_parts[0].shape[0], sum(part.shape[1] for part in a_parts)
    _, P, D, kc = w.shape
    assert K == P * kc and x.shape == (M, D)
    tm = min(tm, M)
    nr = M // tm
    join = 2 if kc % MXU_TILE and P % 2 == 0 else 1

    def body(*refs):
        ins, (dx_ref, dxb_ref, dg_ref), (acc,) = carried.split(refs, na + 4, 3, 1)
        a_refs, (w_ref, x_ref, g_ref, dres_ref) = ins[:na], ins[na:]
        i = pl.program_id(0)
        carried.emit(refs)
        av = a_refs[0][...] if na == 1 else jnp.concatenate([r[...] for r in a_refs], axis=1)
        dhv = None
        for p in range(0, P, join):
            wv = [w_ref[0, p + s] for s in range(join)]
            d = _dot_nt(av[:, p * kc:(p + join) * kc], wv[0] if join == 1 else jnp.concatenate(wv, axis=1))
            dhv = d if dhv is None else dhv + d
        xv = x_ref[...]
        r = lax.rsqrt(jnp.mean(xv * xv, axis=-1, keepdims=True) + EPS)
        xhat = xv * r
        dxhat = dhv * g_ref[...]
        dx = dres_ref[...] + r * (dxhat - xhat * jnp.mean(dxhat * xhat, axis=-1, keepdims=True))
        dx_ref[...] = dx
        dxb_ref[...] = dx.astype(BF16)
        part = jnp.sum((dhv * xhat).reshape(tm // SUBLANES, SUBLANES, D), axis=0)

        @pl.when(i == 0)
        def _():
            acc[...] = jnp.zeros_like(acc)

        acc[...] += part

        @pl.when(i == nr - 1)
        def _():
            dg_ref[...] = jnp.sum(acc[...], axis=0, keepdims=True)

    row = pl.BlockSpec((tm, D), lambda i: (i, 0))
    vec = pl.BlockSpec((1, D), lambda i: (0, 0))
    carried = _Carried(comm, (nr,))
    return carried.pallas_call(
        body,
        in_specs=[pl.BlockSpec((tm, part.shape[1]), lambda i: (i, 0)) for part in a_parts] + [
            pl.BlockSpec((1, P, D, kc), lambda i: (l, 0, 0, 0), pipeline_mode=pl.Buffered(1)),
            row, vec, row],
        out_specs=[row, row, vec],
        out_shape=[jax.ShapeDtypeStruct((M, D), F32), jax.ShapeDtypeStruct((M, D), BF16),
                   jax.ShapeDtypeStruct((1, D), F32)],
        scratch_shapes=[pltpu.VMEM((SUBLANES, D), F32)],
        args=a_parts + [w, x, gain, dres], name=name)


def _mlp_out_loss_head(a, w, l, res, gain, target, *, name, tr=512, kc=1024):
    T, K = a.shape
    D = w.shape[3]
    nr = T // tr

    def body(a_ref, w_ref, res_ref, g_ref, t_ref, dx_ref, dxb_ref, sq_ref, dg_ref, sq_acc, dg_acc):
        i = pl.program_id(0)
        xv = res_ref[...]
        for k0 in range(0, K, kc):
            xv = xv + _dot(_relu2(a_ref[:, k0:k0 + kc]), w_ref[0, 0, k0:k0 + kc, :])
        g = g_ref[...]
        r = lax.rsqrt(jnp.mean(xv * xv, axis=-1, keepdims=True) + EPS)
        xhat = xv * r
        err = xhat * g - t_ref[...]
        dy = err * (1.0 / D)
        dxhat = dy * g
        dx = r * (dxhat - xhat * jnp.mean(dxhat * xhat, axis=-1, keepdims=True))
        dx_ref[...] = dx
        dxb_ref[...] = dx.astype(BF16)
        sq = jnp.sum((err * err).reshape(tr // SUBLANES, SUBLANES, D), axis=0)
        dg = jnp.sum((dy * xhat).reshape(tr // SUBLANES, SUBLANES, D), axis=0)

        @pl.when(i == 0)
        def _():
            sq_acc[...] = sq
            dg_acc[...] = dg

        @pl.when(i > 0)
        def _():
            sq_acc[...] += sq
            dg_acc[...] += dg

        @pl.when(i == nr - 1)
        def _():
            sq_ref[...] = sq_acc[...]
            dg_ref[...] = jnp.sum(dg_acc[...], axis=0, keepdims=True)

    row = pl.BlockSpec((tr, D), lambda i: (i, 0))
    vec = pl.BlockSpec((1, D), lambda i: (0, 0))
    part = pl.BlockSpec((SUBLANES, D), lambda i: (0, 0))
    return pl.pallas_call(
        body, grid=(nr,),
        in_specs=[pl.BlockSpec((tr, K), lambda i: (i, 0)),
                  pl.BlockSpec((1, 1, K, D), lambda i: (l, 0, 0, 0), pipeline_mode=pl.Buffered(1)),
                  row, vec, row],
        out_specs=[row, row, part, vec],
        out_shape=[jax.ShapeDtypeStruct((T, D), F32), jax.ShapeDtypeStruct((T, D), BF16),
                   jax.ShapeDtypeStruct((SUBLANES, D), F32), jax.ShapeDtypeStruct((1, D), F32)],
        scratch_shapes=[pltpu.VMEM((SUBLANES, D), F32), pltpu.VMEM((SUBLANES, D), F32)],
        compiler_params=_params("arbitrary"), name=name)(a, w, res, gain, target)


def _head0_lanes():
    return lax.broadcasted_iota(jnp.int32, (1, LANES), 1) < SB_HEAD_DIM


def _stack_heads(x):
    zero = jnp.zeros_like(x)
    h0 = _head0_lanes()
    return jnp.concatenate([jnp.where(h0, x, zero), jnp.where(h0, zero, x)], axis=0)


def _unstack_heads(y, tq):
    return jnp.where(_head0_lanes(), y[:tq], y[tq:])


def _past_mask(tq, tk):
    row = lax.broadcasted_iota(jnp.int32, (2 * tq, tk), 0) & (tq - 1)
    col = lax.broadcasted_iota(jnp.int32, (2 * tq, tk), 1)
    return col < row


def _max_of(arrays):
    return functools.reduce(jnp.maximum, [jnp.max(a) for a in arrays])


def _sb_logs(z, past):
    minus_abs = lax.bitcast_convert_type(
        lax.bitcast_convert_type(z, jnp.uint32) | jnp.uint32(0x80000000), F32)
    log_beta = jnp.minimum(z, 0.0) - jnp.log(1.0 + jnp.exp(minus_abs))
    l = log_beta - z
    if past is not None:
        l = jnp.where(past, l, 0.0)
    return log_beta, l


def _suffix_matrix(tk):
    j = lax.broadcasted_iota(jnp.int32, (2 * tk, tk), 0) & (tk - 1)
    s = lax.broadcasted_iota(jnp.int32, (2 * tk, tk), 1)
    return (j > s).astype(BF16)


def _suffix_sum(x, u2, exact=True):
    if not exact:
        return _dot(x.astype(BF16), u2[:x.shape[1]])
    hi, lo = _split_bf16(x)
    return _dot(jnp.concatenate([hi, lo], axis=1), u2)


class _Carried:
    def __init__(self, comm, grid, last="arbitrary"):
        self.comm, self.grid, self.last = comm, grid, last
        self.n_op = len(comm.operands) if comm else 0
        self.n_tgt = len(comm.targets) if comm else 0

    def split(self, refs, n_in, n_out, n_scratch):
        self.n_in, self.n_out, self.n_scratch = n_in, n_out, n_scratch
        a = n_in + self.n_op
        b = a + n_out + self.n_tgt
        return refs[:n_in], refs[a:a + n_out], refs[b:b + n_scratch]

    def emit(self, refs):
        if self.comm is None:
            return
        step, n_steps = 0, 1
        for d, size in enumerate(self.grid):
            step = step * size + pl.program_id(d)
            n_steps *= size
        a = self.n_in + self.n_op + self.n_out
        self.comm.emit(step, n_steps, refs[self.n_in:self.n_in + len(self.comm.sources)],
                       refs[a:a + self.n_tgt], refs[a + self.n_tgt + self.n_scratch:])

    def pallas_call(self, body, *, in_specs, out_specs, out_shape, scratch_shapes, args, name):
        comm = self.comm
        n_in, n_out = len(in_specs), len(out_specs)
        aliases = {}
        if comm is not None:
            in_specs = in_specs + [ANY] * self.n_op
            out_specs = out_specs + [ANY] * self.n_tgt
            out_shape = out_shape + comm.out_shapes
            scratch_shapes = scratch_shapes + comm.sems
            args = args + comm.operands
            aliases = comm.aliases(n_in, n_out)
        sem = ("parallel",) * (len(self.grid) - 1) + (self.last,) if comm is None else ("arbitrary",) * len(self.grid)
        results = pl.pallas_call(
            body, grid=self.grid, in_specs=in_specs, out_specs=out_specs, out_shape=out_shape,
            scratch_shapes=scratch_shapes, input_output_aliases=aliases,
            compiler_params=_params(*sem), name=name)(*args)
        return results[:n_out], results[n_out:]


def _sb_fwd(qkv, *, batch, seq, name, comm=None, tq=SB_TILE, n_pre=SB_STRAIGHT, pairs=SB_PAIRS):
    T, D3 = qkv.shape
    D = D3 // 3
    nhp = D // LANES
    tk = tq
    nq = seq // tq
    scale = SB_HEAD_DIM ** -0.5
    assert 1 <= n_pre <= nq and nhp % pairs == 0

    def body(*refs):
        (q_ref, k_ref, v_ref), (o_ref,), (acc,) = carried.split(refs, 3, 1, 1)
        qi = pl.program_id(2)
        carried.emit(refs)
        qs = [_stack_heads(q_ref[:, lanes]) * scale for lanes in cols]
        past = _past_mask(tq, tk)
        u = _suffix_matrix(tk)

        def block(kb, g, c, diag):
            ks = pl.multiple_of(kb * tk, tk)
            z = _dot_nt(qs[g], k_ref[pl.ds(ks, tk), cols[g]])
            log_beta, l = _sb_logs(z, past if diag else None)
            arg = log_beta + _suffix_sum(l, u, exact=False)
            a = jnp.exp(arg if c is None else arg + c)
            if diag:
                a = jnp.where(past, a, 0.0)
            return _dot(a.astype(BF16), v_ref[pl.ds(ks, tk), cols[g]]), jnp.sum(l, axis=1, keepdims=True)

        def straight(n):
            o_sum, c = [None] * pairs, [None] * pairs
            for b in range(n):
                for g in range(pairs):
                    o_b, c_b = block(qi - b, g, c[g], b == 0)
                    o_sum[g] = o_b if b == 0 else o_sum[g] + o_b
                    c[g] = c_b if b == 0 else c[g] + c_b
            return o_sum, c

        def finish(o_sum):
            for g in range(pairs):
                o_ref[:, cols[g]] = _unstack_heads(o_sum[g], tq)

        for n in range(1, n_pre):
            @pl.when(qi == n - 1)
            def _(n=n):
                finish(straight(n)[0])

        @pl.when(qi >= n_pre - 1)
        def _():
            o_sum, c = straight(n_pre)
            for g in range(pairs):
                acc[g] = o_sum[g]

            def cond(st):
                kb, c = st
                return jnp.logical_and(kb >= 0, _max_of(c) > EXP_ZERO_BELOW)

            def step(st):
                kb, c = st
                new_c = []
                for g in range(pairs):
                    o_n, c_n = block(kb, g, c[g], False)
                    acc[g] += o_n
                    new_c.append(c[g] + c_n)
                return kb - 1, tuple(new_c)

            lax.while_loop(cond, step, (qi - n_pre, tuple(c)))
            finish([acc[g] for g in range(pairs)])

    cols = [slice(g * LANES, (g + 1) * LANES) for g in range(pairs)]
    width = pairs * LANES
    carried = _Carried(comm, (batch, nhp // pairs, nq))
    (o,), comm_results = carried.pallas_call(
        body,
        in_specs=[pl.BlockSpec((tq, width), lambda b, p, i: (b * nq + i, p)),
                  pl.BlockSpec((seq, width), lambda b, p, i: (b, nhp // pairs + p)),
                  pl.BlockSpec((seq, width), lambda b, p, i: (b, 2 * (nhp // pairs) + p))],
        out_specs=[pl.BlockSpec((tq, width), lambda b, p, i: (b * nq + i, p))],
        out_shape=[jax.ShapeDtypeStruct((T, D), F32)],
        scratch_shapes=[pltpu.VMEM((pairs, 2 * tq, LANES), F32)],
        args=[qkv, qkv, qkv], name=name)
    return o, comm_results


def _sb_bwd(qkv, o, do, *, batch, seq, name, comm=None, tq=SB_TILE, n_pre=SB_STRAIGHT, pairs=SB_PAIRS):
    T, D3 = qkv.shape
    D = D3 // 3
    nhp = D // LANES
    tk = tq
    nq = seq // tq
    scale = SB_HEAD_DIM ** -0.5

    def body(*refs):
        ins, outs, scratch = carried.split(refs, 5, 3, 3)
        q_ref, k_ref, v_ref, o_ref, do_ref = ins
        dq_ref, dk_ref, dv_ref = outs
        dq_acc, dk_acc, dv_acc = scratch
        qi = pl.program_id(2)
        carried.emit(refs)

        @pl.when(qi == 0)
        def _():
            dk_acc[...] = jnp.zeros_like(dk_acc)
            dv_acc[...] = jnp.zeros_like(dv_acc)

        qs = [_stack_heads(q_ref[:, lanes]) * scale for lanes in cols]
        dos = [_stack_heads(do_ref[:, lanes]) for lanes in cols]
        dsum = [jnp.sum(_stack_heads(do_ref[:, lanes].astype(F32) * o_ref[:, lanes]), axis=1, keepdims=True)
                for lanes in cols]
        past = _past_mask(tq, tk)
        u = _suffix_matrix(tk)

        def weights(kb, p, c, diag):
            ks = pl.multiple_of(kb * tk, tk)
            kblk = k_ref[pl.ds(ks, tk), cols[p]]
            z = _dot_nt(qs[p], kblk)
            log_beta, l = _sb_logs(z, past if diag else None)
            arg = log_beta + _suffix_sum(l, u, exact=False)
            a = jnp.exp(arg if c is None else arg + c)
            if diag:
                a = jnp.where(past, a, 0.0)
            a = a.astype(BF16)
            g = a.astype(F32) * _dot_nt(dos[p], v_ref[pl.ds(ks, tk), cols[p]])
            return ks, kblk, a, 1.0 - jnp.exp(l), g, jnp.sum(l, axis=1, keepdims=True)

        def scores_grad(p, ks, kblk, a, beta, g, prefix, diag):
            dz = g - beta * prefix
            if diag:
                dz = jnp.where(past, dz, 0.0)
            dzb = dz.astype(BF16)
            dk_acc[pl.ds(ks, tk), cols[p]] += _dot_tn(dzb, qs[p])
            dv_acc[pl.ds(ks, tk), cols[p]] += _dot_tn(a, dos[p])
            return _dot(dzb, kblk)

        def block(kb, p, c, gc):
            ks, kblk, a, beta, g, l_sum = weights(kb, p, c, False)
            prefix = dsum[p] - (_suffix_sum(g, u) + gc)
            return scores_grad(p, ks, kblk, a, beta, g, prefix, False), l_sum, jnp.sum(g, axis=1, keepdims=True)

        def finish(dq_sum):
            for p in range(pairs):
                dq_ref[:, cols[p]] = (_unstack_heads(dq_sum[p], tq) * scale).astype(BF16)

        def straight(n):
            dq_sum, c_all, g_all = [], [], []
            for p in range(pairs):
                blocks, c = [], None
                for b in range(n):
                    blk = weights(qi - b, p, c, b == 0)
                    c = blk[5] if b == 0 else c + blk[5]
                    g_bf = blk[4].astype(BF16)
                    suffix = _suffix_sum(g_bf, u, exact=False)
                    rounded_sum = suffix[:, :1] + g_bf[:, :1].astype(F32)
                    blocks.append((blk, suffix, rounded_sum, jnp.sum(blk[4], axis=1, keepdims=True)))
                exact_sum = functools.reduce(lambda x, y: x + y, [e for _, _, _, e in blocks])
                total = dsum[p] - exact_sum + functools.reduce(lambda x, y: x + y, [r for _, _, r, _ in blocks])
                dq, right = None, None
                for b, ((ks, kblk, a, beta, g, _), suffix, rounded_sum, _) in enumerate(blocks):
                    prefix = total - (suffix if right is None else suffix + right)
                    d = scores_grad(p, ks, kblk, a, beta, g, prefix, b == 0)
                    dq = d if dq is None else dq + d
                    right = rounded_sum if right is None else right + rounded_sum
                dq_sum.append(dq)
                c_all.append(c)
                g_all.append(exact_sum)
            return dq_sum, c_all, g_all

        for n in range(1, n_pre):
            @pl.when(qi == n - 1)
            def _(n=n):
                finish(straight(n)[0])

        @pl.when(qi >= n_pre - 1)
        def _():
            dq_sum, c, gc = straight(n_pre)
            for p in range(pairs):
                dq_acc[p] = dq_sum[p]

            def cond(st):
                kb, c, gc = st
                return jnp.logical_and(kb >= 0, _max_of(c) > EXP_ZERO_BELOW)

            def step(st):
                kb, c, gc = st
                new_c, new_gc = [], []
                for p in range(pairs):
                    dq_n, c_n, g_n = block(kb, p, c[p], gc[p])
                    dq_acc[p] += dq_n
                    new_c.append(c[p] + c_n)
                    new_gc.append(gc[p] + g_n)
                return kb - 1, tuple(new_c), tuple(new_gc)

            lax.while_loop(cond, step, (qi - n_pre, tuple(c), tuple(gc)))
            finish([dq_acc[p] for p in range(pairs)])

        @pl.when(qi == nq - 1)
        def _():
            dk_ref[...] = dk_acc[...].astype(BF16)
            dv_ref[...] = dv_acc[...].astype(BF16)

    cols = [slice(p * LANES, (p + 1) * LANES) for p in range(pairs)]
    width, ncb = pairs * LANES, nhp // pairs
    qspec = pl.BlockSpec((tq, width), lambda b, p, i: (b * nq + i, p))
    sspec = pl.BlockSpec((seq, width), lambda b, p, i: (b, p))
    out = jax.ShapeDtypeStruct((T, D), BF16)
    carried = _Carried(comm, (batch, ncb, nq))
    return carried.pallas_call(
        body,
        in_specs=[qspec,
                  pl.BlockSpec((seq, width), lambda b, p, i: (b, ncb + p)),
                  pl.BlockSpec((seq, width), lambda b, p, i: (b, 2 * ncb + p)),
                  qspec, qspec],
        out_specs=[qspec, sspec, sspec], out_shape=[out, out, out],
        scratch_shapes=[pltpu.VMEM((pairs, 2 * tq, LANES), F32), pltpu.VMEM((seq, width), F32),
                        pltpu.VMEM((seq, width), F32)],
        args=[qkv, qkv, qkv, o, do], name=name)


_GELU_C = 0.7978845608028654
_GELU_A = 0.044715


def _gelu(x):
    x = x.astype(BF16)
    xx = x * x
    a1 = 1.0 + jnp.tanh(x * (_GELU_C + (_GELU_C * _GELU_A) * xx))
    hx = 0.5 * x
    grad = a1 * (0.5 + (hx * (2.0 - a1)) * (_GELU_C + (3.0 * _GELU_C * _GELU_A) * xx))
    return hx * a1, grad


def _causal_ws(ws_ref, g):
    t = lax.broadcasted_iota(jnp.int32, (SGU_CHUNK, SGU_CHUNK), 0)
    s = lax.broadcasted_iota(jnp.int32, (SGU_CHUNK, SGU_CHUNK), 1)
    return jnp.where(s <= t, ws_ref[g], 0.0)


def _sgu_gate(u, v, g_ref, ws_ref, bs_ref):
    G, W = SGU_GROUPS, SGU_GROUP_W
    r = lax.rsqrt(jnp.mean(v * v, axis=-1, keepdims=True) + EPS)
    vn = (v * r * g_ref[...]).astype(BF16)
    y = []
    for g in range(G):
        sl = slice(g * W, (g + 1) * W)
        mixed = _dot(_causal_ws(ws_ref, g).astype(BF16), vn[:, sl]) + bs_ref[:, g:g + 1]
        y.append((u[:, sl] * mixed).astype(BF16))
    return jnp.concatenate(y, axis=1)


def _sgu_bwd(uv, dgelu, dy, gain, ws, bst, *, name):
    T, F2 = uv.shape
    F = F2 // 2
    C, G, W = SGU_CHUNK, SGU_GROUPS, SGU_GROUP_W
    nc = T // C

    def body(uv_ref, dgelu_ref, dy_ref, g_ref, ws_ref, bs_ref, duv_ref, dg_ref, dws_ref, dbs_ref,
             dg_acc, dws_acc, dbs_acc):
        i = pl.program_id(0)

        @pl.when(i == 0)
        def _():
            dg_acc[...] = jnp.zeros_like(dg_acc)
            dws_acc[...] = jnp.zeros_like(dws_acc)
            dbs_acc[...] = jnp.zeros_like(dbs_acc)

        u, v = uv_ref[:, :F].astype(F32), uv_ref[:, F:].astype(F32)
        dgelu = dgelu_ref[...].astype(F32)
        r = lax.rsqrt(jnp.mean(v * v, axis=-1, keepdims=True) + EPS)
        vhat = v * r
        gain_v = g_ref[...]
        vn = (vhat * gain_v).astype(BF16)
        dyv = dy_ref[...].astype(F32)
        lane8 = lax.broadcasted_iota(jnp.int32, (1, G), 1)
        dvn_parts = []
        dbs_new = jnp.zeros((C, G), F32)
        for g in range(G):
            sl = slice(g * W, (g + 1) * W)
            wsg = _causal_ws(ws_ref, g)
            mixed = _dot(wsg.astype(BF16), vn[:, sl]) + bs_ref[:, g:g + 1]
            duv_ref[:, sl] = (dyv[:, sl] * mixed * dgelu[:, sl]).astype(BF16)
            dmix = dyv[:, sl] * u[:, sl]
            dbs_new = dbs_new + jnp.where(lane8 == g, jnp.sum(dmix, axis=1, keepdims=True), 0.0)
            dmix_b = dmix.astype(BF16)
            dws_acc[g] += _dot_nt(dmix_b, vn[:, sl])
            dvn_parts.append(_dot(wsg.T.astype(BF16), dmix_b))
        dbs_acc[...] += dbs_new
        dvn = jnp.concatenate(dvn_parts, axis=1)
        dg_acc[...] += jnp.sum((dvn * vhat).reshape(C // SUBLANES, SUBLANES, F), axis=0)
        dvhat = dvn * gain_v
        dv = r * (dvhat - vhat * jnp.mean(dvhat * vhat, axis=-1, keepdims=True))
        duv_ref[:, F:] = (dv * dgelu[:, F:]).astype(BF16)

        @pl.when(i == nc - 1)
        def _():
            dg_ref[...] = jnp.sum(dg_acc[...], axis=0, keepdims=True)
            t = lax.broadcasted_iota(jnp.int32, (G, C, C), 1)
            s = lax.broadcasted_iota(jnp.int32, (G, C, C), 2)
            dws_ref[...] = jnp.where(s <= t, dws_acc[...], 0.0)
            dbs_ref[...] = dbs_acc[...]

    return pl.pallas_call(
        body, grid=(nc,),
        in_specs=[pl.BlockSpec((C, F2), lambda i: (i, 0)), pl.BlockSpec((C, F2), lambda i: (i, 0)),
                  pl.BlockSpec((C, F), lambda i: (i, 0)),
                  pl.BlockSpec((1, F), lambda i: (0, 0)), pl.BlockSpec((G, C, C), lambda i: (0, 0, 0)),
                  pl.BlockSpec((C, G), lambda i: (0, 0))],
        out_specs=[pl.BlockSpec((C, F2), lambda i: (i, 0)), pl.BlockSpec((1, F), lambda i: (0, 0)),
                   pl.BlockSpec((G, C, C), lambda i: (0, 0, 0)), pl.BlockSpec((C, G), lambda i: (0, 0))],
        out_shape=[jax.ShapeDtypeStruct((T, F2), BF16), jax.ShapeDtypeStruct((1, F), F32),
                   jax.ShapeDtypeStruct((G, C, C), F32), jax.ShapeDtypeStruct((C, G), F32)],
        scratch_shapes=[pltpu.VMEM((SUBLANES, F), F32), pltpu.VMEM((G, C, C), F32), pltpu.VMEM((C, G), F32)],
        compiler_params=_params("arbitrary"), name=name)(uv, dgelu, dy, gain, ws, bst)


def _my_place():
    return lax.axis_index("x"), lax.axis_index("y"), lax.axis_index("c")


def _all_gather(shards, *, name):
    nf = len(shards)
    items = [(lambda ins, f=f: ins[f], lambda outs, p, f=f: outs[f].at[:, p]) for f in range(nf)]
    targets = [jax.ShapeDtypeStruct((s.shape[0], N_DEV) + s.shape[1:], s.dtype) for s in shards]
    return _run_comm(_Comm(shards, targets, len(items), _gather_emit(items, [list(range(nf))])), name=name)


class _Comm:
    def __init__(self, sources, targets, n_items, emit):
        self.sources, self.targets, self.n_items, self.emit = list(sources), list(targets), n_items, emit
        self.filled = [t for t in self.targets if not isinstance(t, jax.ShapeDtypeStruct)]
        self.operands = self.sources + self.filled
        self.out_shapes = [jax.ShapeDtypeStruct(t.shape, t.dtype) for t in self.targets]
        self.sems = [pltpu.SemaphoreType.DMA((n_items, 7)), pltpu.SemaphoreType.DMA((n_items, 7)),
                     pltpu.SemaphoreType.DMA((n_items,))]

    def aliases(self, first_operand, first_result):
        pos = {id(t): k for k, t in enumerate(self.targets)}
        return {first_operand + len(self.sources) + a: first_result + pos[id(t)]
                for a, t in enumerate(self.filled)}


def _run_comm(comm, *, name):
    n_op, n_out = len(comm.operands), len(comm.targets)

    def body(*refs):
        comm.emit(0, 1, refs[:len(comm.sources)], refs[n_op:n_op + n_out], refs[n_op + n_out:])

    return pl.pallas_call(
        body, in_specs=[ANY] * n_op, out_specs=[ANY] * n_out, out_shape=comm.out_shapes,
        scratch_shapes=comm.sems, input_output_aliases=comm.aliases(0, 0), name=name)(*comm.operands)


def _at_steps(step, n_steps, phases):
    if n_steps == 1:
        for _, fn in phases:
            fn()
        return
    marks = {}
    for frac, fn in phases:
        marks.setdefault(min(int(frac * n_steps), n_steps - 1), []).append(fn)
    for mark, fns in sorted(marks.items()):
        @pl.when(step == mark)
        def _(fns=fns):
            for fn in fns:
                fn()


def _gather_emit(items, groups, fractions=None):
    if fractions is None:
        fractions = [(g + 1) / len(groups) for g in range(len(groups))]
    def emit(step, n_steps, ins, outs, sems):
        send_sems, recv_sems, local_sems = sems
        x, y, c = _my_place()
        me, sibling = (x, y, c), (x, y, 1 - c)
        chips = [(1 - x, y), (x, 1 - y), (1 - x, 1 - y)]

        def copy(i, k, block, to, own=False):
            src_of, dst_of = items[i]
            dst = dst_of(outs, 4 * block[0] + 2 * block[1] + block[2])
            return pltpu.make_async_remote_copy(
                src_ref=src_of(ins) if own else dst, dst_ref=dst,
                send_sem=send_sems.at[i, k], recv_sem=recv_sems.at[i, k],
                device_id=to, device_id_type=MESH)

        def local(i):
            src_of, dst_of = items[i]
            return pltpu.make_async_copy(src_of(ins), dst_of(outs, 4 * x + 2 * y + c), local_sems.at[i])

        def first(i):
            return [copy(i, 0, me, sibling, own=True)] + [
                copy(i, 1 + j, me, (*chip, c), own=True) for j, chip in enumerate(chips)]

        def start():
            for i in range(len(items)):
                local(i).start()
                for cp in first(i):
                    cp.start()

        def forward(group):
            for j, chip in enumerate(chips):
                for i in group:
                    copy(i, 1 + j, (*chip, c), me).wait_recv()
                    copy(i, 4 + j, (*chip, c), sibling).start()

        def finish():
            for i in range(len(items)):
                copy(i, 0, sibling, me).wait_recv()
                for j, chip in enumerate(chips):
                    copy(i, 4 + j, (*chip, 1 - c), me).wait_recv()
            for i in range(len(items)):
                for cp in first(i) + [copy(i, 4 + j, (*chip, c), sibling) for j, chip in enumerate(chips)]:
                    cp.wait_send()
                local(i).wait()

        phases = [(0.0, start)]
        for frac, group in zip(fractions, groups):
            phases.append((frac, functools.partial(forward, group)))
        phases.append((1.0, finish))
        _at_steps(step, n_steps, phases)

    return emit


def _exchange_emit(items):
    def emit(step, n_steps, ins, outs, sems):
        send_sems, recv_sems, local_sems = sems
        x, y, c = _my_place()
        me = 4 * x + 2 * y + c

        def peer_of(k):
            return x ^ (k >> 2), y ^ ((k >> 1) & 1), c ^ (k & 1)

        def copy(i, k):
            src_of, dst_of = items[i]
            px, py, pc = peer_of(k)
            return pltpu.make_async_remote_copy(
                src_ref=src_of(ins, 4 * px + 2 * py + pc), dst_ref=dst_of(outs, me),
                send_sem=send_sems.at[i, k - 1], recv_sem=recv_sems.at[i, k - 1],
                device_id=(px, py, pc), device_id_type=MESH)

        def arrival(i, k):
            src_of, dst_of = items[i]
            px, py, pc = peer_of(k)
            peer = 4 * px + 2 * py + pc
            return pltpu.make_async_remote_copy(
                src_ref=src_of(ins, peer), dst_ref=dst_of(outs, peer),
                send_sem=send_sems.at[i, k - 1], recv_sem=recv_sems.at[i, k - 1],
                device_id=(x, y, c), device_id_type=MESH)

        def local(i):
            src_of, dst_of = items[i]
            return pltpu.make_async_copy(src_of(ins, me), dst_of(outs, me), local_sems.at[i])

        def start():
            for i in range(len(items)):
                local(i).start()
            for k in range(1, N_DEV):
                for i in range(len(items)):
                    copy(i, k).start()

        def finish():
            for k in range(1, N_DEV):
                for i in range(len(items)):
                    arrival(i, k).wait_recv()
            for k in range(1, N_DEV):
                for i in range(len(items)):
                    copy(i, k).wait_send()
            for i in range(len(items)):
                local(i).wait()

        _at_steps(step, n_steps, [(0.0, start), (1.0, finish)])

    return emit


def _adam_math(g, w, m, v):
    m = ADAM_B1 * m + (1.0 - ADAM_B1) * g
    v = ADAM_B2 * v + (1.0 - ADAM_B2) * (g * g)
    m_hat = m / (1.0 - ADAM_B1 ** ADAM_STEP)
    v_hat = v / (1.0 - ADAM_B2 ** ADAM_STEP)
    delta = -ADAM_LR * (m_hat / (jnp.sqrt(v_hat) + ADAM_EPS) + ADAM_WD * w)
    return delta, m, v


def _sum_adamw(parts, w, m, v, *, name, tr=256):
    L, nd, R, C = parts.shape
    tr = min(tr, R)
    assert R % tr == 0

    def body(p_ref, w_ref, m_ref, v_ref, g_ref, d_ref, nm_ref, nv_ref):
        g = p_ref[0, 0].astype(F32)
        for q in range(1, nd):
            g = g + p_ref[0, q].astype(F32)
        d, nm, nv = _adam_math(g, w_ref[0], m_ref[0], v_ref[0])
        g_ref[0] = g
        d_ref[0] = d
        nm_ref[0] = nm
        nv_ref[0] = nv

    blk = pl.BlockSpec((1, tr, C), lambda l, i: (l, i, 0))
    out = jax.ShapeDtypeStruct((L, R, C), F32)
    return pl.pallas_call(
        body, grid=(L, R // tr),
        in_specs=[pl.BlockSpec((1, nd, tr, C), lambda l, i: (l, 0, i, 0)), blk, blk, blk],
        out_specs=[blk] * 4, out_shape=[out] * 4,
        compiler_params=_params("parallel", "parallel"), name=name)(parts, w, m, v)


def _sum_parts(parts, *, name):
    nd, R, C = parts.shape

    def body(p_ref, o_ref):
        g = p_ref[0]
        for q in range(1, nd):
            g = g + p_ref[q]
        o_ref[...] = g

    return pl.pallas_call(
        body, out_shape=jax.ShapeDtypeStruct((R, C), F32),
        in_specs=[pl.BlockSpec(memory_space=pltpu.VMEM)],
        out_specs=pl.BlockSpec(memory_space=pltpu.VMEM), name=name)(parts)


def _adamw_small(g, w, m, v, *, name):
    def body(g_ref, w_ref, m_ref, v_ref, d_ref, nm_ref, nv_ref):
        d, nm, nv = _adam_math(g_ref[...], w_ref[...], m_ref[...], v_ref[...])
        d_ref[...] = d
        nm_ref[...] = nm
        nv_ref[...] = nv

    vm = pl.BlockSpec(memory_space=pltpu.VMEM)
    out = jax.ShapeDtypeStruct(g.shape, F32)
    return pl.pallas_call(body, out_shape=[out] * 3, in_specs=[vm] * 4, out_specs=[vm] * 3,
                          name=name)(g, w, m, v)


def kernel(x, norm_mix, norm_mlp, sb_wqkv, sb_wo, sgu_win, sgu_gain, sgu_ws, sgu_bs, sgu_wout, mlp_w1, mlp_w2, final_norm, loss_target, m_norm_mix, m_norm_mlp, m_sb_wqkv, m_sb_wo, m_sgu_win, m_sgu_gain, m_sgu_ws, m_sgu_bs, m_sgu_wout, m_mlp_w1, m_mlp_w2, m_final_norm, v_norm_mix, v_norm_mlp, v_sb_wqkv, v_sb_wo, v_sgu_win, v_sgu_gain, v_sgu_ws, v_sgu_bs, v_sgu_wout, v_mlp_w1, v_mlp_w2, v_final_norm):
    batch, seq, D = x.shape
    T = batch * seq
    x0 = x.reshape(T, D)
    target = loss_target.reshape(T, D)

    WQKV, WO, WIN, WOUT, W1, W2, GAIN = range(7)
    big = [sb_wqkv, sb_wo, sgu_win, sgu_wout, mlp_w1, mlp_w2]
    shards = [w.astype(BF16) for w in big] + [sgu_gain[:, None, :]]
    rides = {
        "qkv0": [[(WO, 0), (W1, 0)]],
        "sb_fwd0": [[(WIN, 0), (WOUT, 0), (W1, 1), (W2, 1)], [(WQKV, 1)]],
        "w1_0": [[(W2, 0)]],
        "qkv2": [[(WO, 1), (W1, 2)]],
        "sb_fwd2": [[(W2, 2)], [(WIN, 1), (WOUT, 1), (W1, 3), (W2, 3)]],
    }

    first = [(lambda ins: ins[WQKV].at[0], lambda outs, p: outs[WQKV].at[0, p]),
             (lambda ins: ins[GAIN], lambda outs, p: outs[GAIN].at[:, p])]
    targets = [jax.ShapeDtypeStruct((s.shape[0], N_DEV) + s.shape[1:], s.dtype) for s in shards]
    gathered0 = _run_comm(_Comm(shards, targets, len(first), _gather_emit(first, [[0, 1]])), name="gather_first")
    gain_sgu = gathered0[GAIN].reshape(-1, 1, SGU_FFN)
    gw = dict(enumerate(gathered0[:GAIN]))

    def gather_ride(call):
        wanted = [fl for group in rides[call] for fl in group]
        fams = sorted({f for f, _ in wanted})
        items = [(lambda ins, f=f, l=l: ins[f].at[l], lambda outs, p, t=fams.index(f), l=l: outs[t].at[l, p])
                 for f, l in wanted]
        groups, sent, k = [], [], 0
        for group in rides[call]:
            groups.append(list(range(k, k + len(group))))
            sent.append(sum(shards[f].shape[1] * shards[f].shape[2] for f, _ in group) + (sent[-1] if sent else 0))
            k += len(group)
        fractions = [s / sent[-1] for s in sent]
        return _Comm(shards[:GAIN], [gw[f] for f in fams], len(items), _gather_emit(items, groups, fractions)), fams

    def weight(f):
        g = gw[f]
        return g if f in (WQKV, WIN, W1) else g.reshape(g.shape[0], 1, N_DEV * g.shape[2], g.shape[3])

    saved = []
    xs = x0
    for i in range(DEPTH):
        j = i // 2
        if i % 2 == 0:
            comm, fams = gather_ride(f"qkv{i}")
            (qkv, h), results = _mm_nn(xs, weight(WQKV), j, out_dtype=BF16, gain=norm_mix[i:i + 1],
                                       name=f"qkv{i}", comm=comm, tm=1024)
            gw.update(zip(fams, results))
            comm, fams = gather_ride(f"sb_fwd{i}")
            o, results = _sb_fwd(qkv, batch=batch, seq=seq, name=f"sb_fwd{i}", comm=comm)
            gw.update(zip(fams, results))
            x_mid = _mm_nn(o, weight(WO), j, out_dtype=F32, res=xs, name=f"wo{i}", tm=1024)
            mix = (h, qkv, o)
        else:
            gain_j = gain_sgu[j]
            bst = sgu_bs[j].T
            uv, h, dgelu, yv = _mm_nn(xs, weight(WIN), j, out_dtype=BF16, gain=norm_mix[i:i + 1],
                                      sgu=(gain_j, sgu_ws[j], bst), name=f"win_sgu{i}")
            x_mid = _mm_nn(yv, weight(WOUT), j, out_dtype=F32, res=xs, name=f"wout{i}", tm=1024)
            mix = (h, uv, dgelu, yv, gain_j, bst)
        if f"w1_{i}" in rides:
            comm, fams = gather_ride(f"w1_{i}")
            (a, h2), results = _mm_nn(x_mid, weight(W1), i, out_dtype=BF16, gain=norm_mlp[i:i + 1],
                                      name=f"w1_{i}", comm=comm, tm=1024)
            gw.update(zip(fams, results))
        else:
            a, h2 = _mm_nn(x_mid, weight(W1), i, out_dtype=BF16, gain=norm_mlp[i:i + 1], name=f"w1_{i}", tm=1024)
        saved.append((xs, mix, x_mid, h2, a))
        if i < DEPTH - 1:
            xs = _mm_nn(a, weight(W2), i, out_dtype=F32, res=x_mid, a_act="relu2", name=f"w2_{i}")
    g_wqkv, g_wo, g_win, g_wout, g_w1, g_w2 = [weight(f) for f in range(GAIN)]

    dx, dxb, sq, d_final = _mlp_out_loss_head(a, g_w2, DEPTH - 1, x_mid, final_norm.reshape(1, D), target,
                                              name="w2_loss_head")

    SMALL = GAIN
    stacks = {f: jax.ShapeDtypeStruct((w.shape[0], N_DEV) + w.shape[1:], BF16) for f, w in enumerate(big)}
    pending = []
    second_half = []

    def row_shards(p):
        return p.reshape(N_DEV, p.shape[1] // N_DEV, p.shape[2])

    def exchange(going):
        fams = sorted({f for _, f, _, _ in going})

        def item(a, f, l, rows):
            rows = slice(None) if rows is None else slice(*rows)
            return (lambda ins, p: ins[a].at[p, rows], lambda outs, q: outs[fams.index(f)].at[l, q, rows])

        items = [item(a, f, l, rows) for a, (_, f, l, rows) in enumerate(going)]
        return _Comm([g[0] for g in going], [stacks[f] for f in fams], len(items), _exchange_emit(items)), fams

    def take(entries):
        going = list(entries)
        entries.clear()
        return going

    d_norm_mix, d_norm_mlp = [None] * DEPTH, [None] * DEPTH
    d_gain, d_ws, d_bs = [None] * 2, [None] * 2, [None] * 2
    for i in reversed(range(DEPTH)):
        j = i // 2
        xs, mix, x_mid, h2, a = saved[i]
        da = _mm_nt(dxb, g_w2, i, out_dtype=BF16, act_src=a, name=f"d_a{i}")
        pending.append((row_shards(_mm_tn(a, dxb, shards=1, a_act="relu2", name=f"d_w2_{i}")), W2, i, None))
        pending.append((_mm_tn(h2, da, shards=N_DEV, name=f"d_w1_{i}", pb=4, tm=1024), W1, i, None))
        comm, fams = exchange(take(second_half)) if second_half else (None, [])
        (dx, dxb, d_norm_mlp[i]), results = _mm_nt_rms_bwd(
            da, g_w1, i, x_mid, norm_mlp[i:i + 1], dx, name=f"d_h2_{i}", comm=comm)
        stacks.update(zip(fams, results))
        if i % 2 == 0:
            h, qkv, o = mix
            do = _mm_nt(dxb, g_wo, j, out_dtype=BF16, name=f"d_o{i}", tm=1024)
            pending.append((row_shards(_mm_tn(o, dxb, shards=1, name=f"d_wo{i}")), WO, j, None))
            comm, fams = exchange(take(pending))
            (dq, dk, dv), results = _sb_bwd(qkv, o, do, batch=batch, seq=seq, name=f"sb_bwd{i}", comm=comm)
            stacks.update(zip(fams, results))
            mixer_in, w_in, f_in = [dq, dk, dv], g_wqkv, WQKV
            d_w_in = _mm_tn(h, mixer_in, shards=N_DEV, name=f"d_wqkv{i}", pb=N_DEV, tm=1024, tk=512)
        else:
            h, uv, dgelu, yv, gain_j, bst = mix
            dy = _mm_nt(dxb, g_wout, j, out_dtype=BF16, name=f"d_y{i}", tm=1024)
            pending.append((row_shards(_mm_tn(yv, dxb, shards=1, name=f"d_wout{i}")), WOUT, j, None))
            mixer_in, d_gain[j], d_ws[j], dbst = _sgu_bwd(uv, dgelu, dy, gain_j, sgu_ws[j], bst, name=f"sgu_bwd{i}")
            d_bs[j] = dbst.T
            w_in, f_in = g_win, WIN
            d_w_in = _mm_tn(h, mixer_in, shards=N_DEV, name=f"d_win{i}", pb=4, tm=1024)
        rows = d_w_in.shape[1]
        if i > 0:
            second_half.append((d_w_in, f_in, j, (rows // 2, rows)))
            comm, fams = exchange([(d_w_in, f_in, j, (0, rows // 2))])
        else:
            comm, fams = exchange([(d_w_in, f_in, j, None)])
        (dx, dxb, d_norm_mix[i]), results = _mm_nt_rms_bwd(
            mixer_in, w_in, j, xs, norm_mix[i:i + 1], dx, name=f"d_h_mix{i}", comm=comm)
        stacks.update(zip(fams, results))
    grad_x = dx.reshape(batch, seq, D)

    small = [jnp.concatenate(d_norm_mix, 0), jnp.concatenate(d_norm_mlp, 0), d_final,
             jnp.concatenate(d_gain, 0), jnp.stack(d_bs, 0), jnp.stack(d_ws, 0)]
    small_flat = jnp.concatenate([s.reshape(-1) for s in small] + [(0.5 * jnp.sum(sq) / D).reshape(1)])
    n_small = small_flat.shape[0]
    small_rows = -(-n_small // (N_DEV * SUBLANES * LANES)) * SUBLANES
    small_size = N_DEV * small_rows * LANES
    small_flat = jnp.pad(small_flat, (0, small_size - n_small))
    stacks[SMALL] = jax.ShapeDtypeStruct((1, N_DEV, small_rows, LANES), F32)
    pending.append((small_flat.reshape(N_DEV, small_rows, LANES), SMALL, 0, None))
    comm, fams = exchange(take(pending))
    stacks.update(zip(fams, _run_comm(comm, name="exchange_last")))
    r_wqkv, r_wo, r_win, r_wout, r_w1, r_w2, r_small = [stacks[f] for f in range(SMALL + 1)]

    u_wqkv = _sum_adamw(r_wqkv, sb_wqkv, m_sb_wqkv, v_sb_wqkv, name="adamw_wqkv")
    u_wo = _sum_adamw(r_wo, sb_wo, m_sb_wo, v_sb_wo, name="adamw_wo")
    u_win = _sum_adamw(r_win, sgu_win, m_sgu_win, v_sgu_win, name="adamw_win")
    u_wout = _sum_adamw(r_wout, sgu_wout, m_sgu_wout, v_sgu_wout, name="adamw_wout")
    u_w1 = _sum_adamw(r_w1, mlp_w1, m_mlp_w1, v_mlp_w1, name="adamw_w1")
    u_w2 = _sum_adamw(r_w2, mlp_w2, m_mlp_w2, v_mlp_w2, name="adamw_w2")

    small_sum = _sum_parts(r_small[0], name="sum_small")
    g_small = _all_gather([small_sum[None]], name="gather_small")[0].reshape(-1)
    loss = g_small[n_small - 1]

    shapes = [s.shape for s in small]
    sizes = [s.size for s in small]
    offs = [sum(sizes[:k]) for k in range(len(sizes))]
    me = 4 * lax.axis_index("x") + 2 * lax.axis_index("y") + lax.axis_index("c")
    shard_w = SGU_FFN // N_DEV

    def pack(arrs):
        flat = jnp.concatenate([a_.reshape(-1) for a_ in arrs])
        return jnp.pad(flat, (0, small_size - flat.shape[0])).reshape(-1, LANES)

    def full_gain(gshard):
        return lax.dynamic_update_slice(jnp.zeros((2, SGU_FFN), F32), gshard, (0, me * shard_w))

    w_small = pack([norm_mix, norm_mlp, final_norm, full_gain(sgu_gain), sgu_bs, sgu_ws])
    m_small = pack([m_norm_mix, m_norm_mlp, m_final_norm, full_gain(m_sgu_gain), m_sgu_bs, m_sgu_ws])
    v_small = pack([v_norm_mix, v_norm_mlp, v_final_norm, full_gain(v_sgu_gain), v_sgu_bs, v_sgu_ws])
    g_pack = g_small.reshape(-1, LANES)
    sm = [g_pack] + list(_adamw_small(g_pack, w_small, m_small, v_small, name="adamw_small"))

    def unpack(flat2d):
        flat = flat2d.reshape(-1)
        out = [flat[offs[k]:offs[k] + sizes[k]].reshape(shapes[k]) for k in range(len(sizes))]
        out[2] = out[2].reshape(D)
        out[3] = lax.dynamic_slice(out[3], (0, me * shard_w), (2, shard_w))
        return out

    outs = []
    for k, big_u in enumerate(zip(u_wqkv, u_wo, u_win, u_wout, u_w1, u_w2)):
        s_nm, s_nl, s_fn, s_gain, s_bs, s_ws = unpack(sm[k])
        b_wqkv, b_wo, b_win, b_wout, b_w1, b_w2 = big_u
        outs += [s_nm, s_nl, b_wqkv, b_wo, b_win, s_gain, s_ws, s_bs, b_wout, b_w1, b_w2, s_fn]
    return (loss, grad_x, *outs)
```

```python
import functools

import jax
import jax.numpy as jnp
from jax import lax
from jax.experimental import pallas as pl
from jax.experimental.pallas import tpu as pltpu

F32 = jnp.float32
BF16 = jnp.bfloat16

N_DEV = 8
D_MODEL = 1024
SEQ = 2048
DEPTH = 4
SB_HEAD_DIM = 64
SGU_CHUNK = 128
SGU_FFN = 2 * D_MODEL
SGU_GROUPS = 8
SGU_GROUP_W = SGU_FFN // SGU_GROUPS
EPS = 1e-6

ADAM_LR = 0.001
ADAM_B1 = 0.9
ADAM_B2 = 0.999
ADAM_EPS = 1e-08
ADAM_WD = 0.01
ADAM_STEP = 10

MXU_TILE = 256
LANES = 128
SUBLANES = 8
VMEM_LIMIT = 56 * 1024 * 1024
EXP_ZERO_BELOW = -104.0
SB_TILE = 256
SB_STRAIGHT = 2
SB_PAIRS = 4

MESH = pl.DeviceIdType.MESH
ANY = pl.BlockSpec(memory_space=pl.ANY)


def _params(*sem):
    return pltpu.CompilerParams(dimension_semantics=sem, vmem_limit_bytes=VMEM_LIMIT)


def _dot(a, b):
    return jnp.dot(a, b, preferred_element_type=F32)


def _dot_nt(a, b):
    return lax.dot_general(a, b, (((1,), (1,)), ((), ())), preferred_element_type=F32)


def _dot_tn(a, b):
    return lax.dot_general(a, b, (((0,), (0,)), ((), ())), preferred_element_type=F32)


def _split_bf16(x):
    hi = x.astype(BF16)
    lo = (x - hi.astype(F32)).astype(BF16)
    return hi, lo


def _relu2(av):
    t = jnp.maximum(av, jnp.zeros_like(av))
    return t * t


def _mm_nn(a, w, l, *, out_dtype, name, res=None, a_act=None, gain=None, sgu=None, comm=None,
           tm=512, kc=1024):
    M, K = a.shape
    _, P, K2, n = w.shape
    assert K2 == K
    tm, kc = min(tm, M), min(kc, K)
    assert M % tm == 0 and K % kc == 0
    join = 2 if n % MXU_TILE and P % 2 == 0 else 1

    n_main = 2 + (gain is not None) + (res is not None)
    n_in = n_main + (3 if sgu is not None else 0)
    n_out = 1 + (gain is not None) + (2 if sgu is not None else 0)
    gelu = sgu is not None

    def body(*all_refs):
        ins, outs, _ = carried.split(all_refs, n_in, n_out, 0)
        carried.emit(all_refs)
        refs = list(ins) + list(outs)
        a_ref, w_ref, o_ref = refs[0], refs[1], refs[n_in]
        if gain is not None:
            xv = a_ref[...]
            r = lax.rsqrt(jnp.mean(xv * xv, axis=-1, keepdims=True) + EPS)
            h = (xv * r * refs[2][...]).astype(BF16)
            refs[n_in + 1][...] = h
        for p in range(0, P, join):
            sl = slice(p * n, (p + join) * n)
            acc = None
            for k0 in range(0, K, kc):
                if gain is not None:
                    av = h[:, k0:k0 + kc]
                else:
                    av = a_ref[:, k0:k0 + kc]
                    av = _relu2(av) if a_act == "relu2" else av.astype(BF16)
                wv = [w_ref[0, p + s, k0:k0 + kc, :] for s in range(join)]
                d = _dot(av, wv[0] if join == 1 else jnp.concatenate(wv, axis=1))
                acc = d if acc is None else acc + d
            if res is not None:
                acc = acc + refs[n_main - 1][:, sl]
            if gelu:
                acc, dact = _gelu(acc)
                refs[-2][:, sl] = dact.astype(BF16)
            o_ref[:, sl] = acc.astype(out_dtype)
        if sgu is not None:
            half = P * n // 2
            for r0 in range(0, tm, SGU_CHUNK):
                rows = slice(r0, r0 + SGU_CHUNK)
                refs[-1][rows, :] = _sgu_gate(o_ref[rows, :half].astype(F32), o_ref[rows, half:].astype(F32),
                                              *refs[n_main:n_in])

    row = lambda width: pl.BlockSpec((tm, width), lambda i: (i, 0))
    whole = lambda arr: pl.BlockSpec(arr.shape, lambda i: (0,) * arr.ndim)
    in_specs = [row(K), pl.BlockSpec((1, P, K, n), lambda i: (l, 0, 0, 0), pipeline_mode=pl.Buffered(1))]
    args = [a, w]
    if gain is not None:
        in_specs.append(whole(gain))
        args.append(gain)
    if res is not None:
        in_specs.append(row(P * n))
        args.append(res)
    out_specs, out_shape = [row(P * n)], [jax.ShapeDtypeStruct((M, P * n), out_dtype)]
    if gain is not None:
        out_specs.append(row(K))
        out_shape.append(jax.ShapeDtypeStruct((M, K), BF16))
    if sgu is not None:
        in_specs += [whole(s) for s in sgu]
        args += list(sgu)
        out_specs += [row(P * n), row(P * n // 2)]
        out_shape += [jax.ShapeDtypeStruct((M, P * n), BF16), jax.ShapeDtypeStruct((M, P * n // 2), BF16)]
    carried = _Carried(comm, (M // tm,), last="parallel")
    outs, comm_results = carried.pallas_call(
        body, in_specs=in_specs, out_specs=out_specs, out_shape=out_shape, scratch_shapes=[],
        args=args, name=name)
    outs = tuple(outs) if n_out > 1 else outs[0]
    return outs if comm is None else (outs, comm_results)


def _mm_nt(a, w, l, *, out_dtype, name, act_src=None, tm=512, tn=1024):
    M, K = a.shape
    _, P, Nout, kc = w.shape
    assert P == 1 and K == kc
    tm, tn = min(tm, M), min(tn, Nout)
    assert M % tm == 0 and Nout % tn == 0

    def body(*refs):
        a_ref, w_ref, o_ref = refs[0], refs[1], refs[-1]
        av = a_ref[...].astype(BF16)
        for n0 in range(0, Nout, tn):
            r = _dot_nt(av, w_ref[0, 0, n0:n0 + tn, :])
            if act_src is not None:
                r = r * (2.0 * jnp.maximum(refs[2][:, n0:n0 + tn].astype(F32), 0.0))
            o_ref[:, n0:n0 + tn] = r.astype(out_dtype)

    row = lambda width: pl.BlockSpec((tm, width), lambda i: (i, 0))
    in_specs = [row(K), pl.BlockSpec((1, 1, Nout, K), lambda i: (l, 0, 0, 0), pipeline_mode=pl.Buffered(1))]
    args = [a, w]
    if act_src is not None:
        in_specs.append(row(Nout))
        args.append(act_src)
    return pl.pallas_call(
        body, grid=(M // tm,), in_specs=in_specs, out_specs=row(Nout),
        out_shape=jax.ShapeDtypeStruct((M, Nout), out_dtype),
        compiler_params=_params("parallel"), name=name)(*args)


def _mm_tn(a, b, *, shards, name, a_act=None, tm=2048, tk=1024, pb=1):
    b_parts = list(b) if isinstance(b, (list, tuple)) else [b]
    M, K = a.shape
    N = sum(part.shape[1] for part in b_parts)
    assert all(part.shape[0] == M for part in b_parts)
    n = N // shards
    tm, tk = min(tm, M), min(tk, K)
    assert M % tm == 0 and K % tk == 0 and shards % pb == 0
    width, nm = pb * n, M // tm
    assert len(b_parts) == 1 or width == N

    def body(*refs):
        a_ref, b_refs, o_ref, acc = refs[0], refs[1:-2], refs[-2], refs[-1]
        m = pl.program_id(2)
        av = a_ref[...]
        if a_act == "relu2":
            av = _relu2(av)
        bv = [r[...].astype(BF16) for r in b_refs]

        @pl.when(m == 0)
        def _():
            acc[...] = jnp.zeros_like(acc)

        acc[...] += _dot_tn(av.astype(BF16), bv[0] if len(bv) == 1 else jnp.concatenate(bv, axis=1))

        @pl.when(m == nm - 1)
        def _():
            for p in range(pb):
                o_ref[p] = acc[:, p * n:(p + 1) * n].astype(BF16)

    if len(b_parts) == 1:
        b_specs = [pl.BlockSpec((tm, width), lambda i, j, m: (m, j))]
    else:
        b_specs = [pl.BlockSpec((tm, part.shape[1]), lambda i, j, m: (m, 0)) for part in b_parts]
    return pl.pallas_call(
        body, grid=(K // tk, N // width, nm),
        in_specs=[pl.BlockSpec((tm, tk), lambda i, j, m: (m, i))] + b_specs,
        out_specs=pl.BlockSpec((pb, tk, n), lambda i, j, m: (j, i, 0)),
        out_shape=jax.ShapeDtypeStruct((shards, K, n), BF16),
        scratch_shapes=[pltpu.VMEM((tk, width), F32)],
        compiler_params=_params("parallel", "parallel", "arbitrary"), name=name)(a, *b_parts)


def _mm_nt_rms_bwd(a, w, l, x, gain, dres, *, name, comm=None, tm=512):
    a_parts = list(a) if isinstance(a, (list, tuple)) else [a]
    na = len(a_parts)
    M, K = a_parts[0].shape[0], sum(part.shape[1] for part in a_parts)
    _, P, D, kc = w.shape
    assert K == P * kc and x.shape == (M, D)
    tm = min(tm, M)
    nr = M // tm
    join = 2 if kc % MXU_TILE and P % 2 == 0 else 1

    def body(*refs):
        ins, (dx_ref, dxb_ref, dg_ref), (acc,) = carried.split(refs, na + 4, 3, 1)
        a_refs, (w_ref, x_ref, g_ref, dres_ref) = ins[:na], ins[na:]
        i = pl.program_id(0)
        carried.emit(refs)
        av = a_refs[0][...] if na == 1 else jnp.concatenate([r[...] for r in a_refs], axis=1)
        dhv = None
        for p in range(0, P, join):
            wv = [w_ref[0, p + s] for s in range(join)]
            d = _dot_nt(av[:, p * kc:(p + join) * kc], wv[0] if join == 1 else jnp.concatenate(wv, axis=1))
            dhv = d if dhv is None else dhv + d
        xv = x_ref[...]
        r = lax.rsqrt(jnp.mean(xv * xv, axis=-1, keepdims=True) + EPS)
        xhat = xv * r
        dxhat = dhv * g_ref[...]
        dx = dres_ref[...] + r * (dxhat - xhat * jnp.mean(dxhat * xhat, axis=-1, keepdims=True))
        dx_ref[...] = dx
        dxb_ref[...] = dx.astype(BF16)
        part = jnp.sum((dhv * xhat).reshape(tm // SUBLANES, SUBLANES, D), axis=0)

        @pl.when(i == 0)
        def _():
            acc[...] = jnp.zeros_like(acc)

        acc[...] += part

        @pl.when(i == nr - 1)
        def _():
            dg_ref[...] = jnp.sum(acc[...], axis=0, keepdims=True)

    row = pl.BlockSpec((tm, D), lambda i: (i, 0))
    vec = pl.BlockSpec((1, D), lambda i: (0, 0))
    carried = _Carried(comm, (nr,))
    return carried.pallas_call(
        body,
        in_specs=[pl.BlockSpec((tm, part.shape[1]), lambda i: (i, 0)) for part in a_parts] + [
            pl.BlockSpec((1, P, D, kc), lambda i: (l, 0, 0, 0), pipeline_mode=pl.Buffered(1)),
            row, vec, row],
        out_specs=[row, row, vec],
        out_shape=[jax.ShapeDtypeStruct((M, D), F32), jax.ShapeDtypeStruct((M, D), BF16),
                   jax.ShapeDtypeStruct((1, D), F32)],
        scratch_shapes=[pltpu.VMEM((SUBLANES, D), F32)],
        args=a_parts + [w, x, gain, dres], name=name)


def _mlp_out_loss_head(a, w, l, res, gain, target, *, name, tr=512, kc=1024):
    T, K = a.shape
    D = w.shape[3]
    nr = T // tr

    def body(a_ref, w_ref, res_ref, g_ref, t_ref, dx_ref, dxb_ref, sq_ref, dg_ref, sq_acc, dg_acc):
        i = pl.program_id(0)
        xv = res_ref[...]
        for k0 in range(0, K, kc):
            xv = xv + _dot(_relu2(a_ref[:, k0:k0 + kc]), w_ref[0, 0, k0:k0 + kc, :])
        g = g_ref[...]
        r = lax.rsqrt(jnp.mean(xv * xv, axis=-1, keepdims=True) + EPS)
        xhat = xv * r
        err = xhat * g - t_ref[...]
        dy = err * (1.0 / D)
        dxhat = dy * g
        dx = r * (dxhat - xhat * jnp.mean(dxhat * xhat, axis=-1, keepdims=True))
        dx_ref[...] = dx
        dxb_ref[...] = dx.astype(BF16)
        sq = jnp.sum((err * err).reshape(tr // SUBLANES, SUBLANES, D), axis=0)
        dg = jnp.sum((dy * xhat).reshape(tr // SUBLANES, SUBLANES, D), axis=0)

        @pl.when(i == 0)
        def _():
            sq_acc[...] = sq
            dg_acc[...] = dg

        @pl.when(i > 0)
        def _():
            sq_acc[...] += sq
            dg_acc[...] += dg

        @pl.when(i == nr - 1)
        def _():
            sq_ref[...] = sq_acc[...]
            dg_ref[...] = jnp.sum(dg_acc[...], axis=0, keepdims=True)

    row = pl.BlockSpec((tr, D), lambda i: (i, 0))
    vec = pl.BlockSpec((1, D), lambda i: (0, 0))
    part = pl.BlockSpec((SUBLANES, D), lambda i: (0, 0))
    return pl.pallas_call(
        body, grid=(nr,),
        in_specs=[pl.BlockSpec((tr, K), lambda i: (i, 0)),
                  pl.BlockSpec((1, 1, K, D), lambda i: (l, 0, 0, 0), pipeline_mode=pl.Buffered(1)),
                  row, vec, row],
        out_specs=[row, row, part, vec],
        out_shape=[jax.ShapeDtypeStruct((T, D), F32), jax.ShapeDtypeStruct((T, D), BF16),
                   jax.ShapeDtypeStruct((SUBLANES, D), F32), jax.ShapeDtypeStruct((1, D), F32)],
        scratch_shapes=[pltpu.VMEM((SUBLANES, D), F32), pltpu.VMEM((SUBLANES, D), F32)],
        compiler_params=_params("arbitrary"), name=name)(a, w, res, gain, target)


def _head0_lanes():
    return lax.broadcasted_iota(jnp.int32, (1, LANES), 1) < SB_HEAD_DIM


def _stack_heads(x):
    zero = jnp.zeros_like(x)
    h0 = _head0_lanes()
    return jnp.concatenate([jnp.where(h0, x, zero), jnp.where(h0, zero, x)], axis=0)


def _unstack_heads(y, tq):
    return jnp.where(_head0_lanes(), y[:tq], y[tq:])


def _past_mask(tq, tk):
    row = lax.broadcasted_iota(jnp.int32, (2 * tq, tk), 0) & (tq - 1)
    col = lax.broadcasted_iota(jnp.int32, (2 * tq, tk), 1)
    return col < row


def _max_of(arrays):
    return functools.reduce(jnp.maximum, [jnp.max(a) for a in arrays])


def _sb_logs(z, past):
    minus_abs = lax.bitcast_convert_type(
        lax.bitcast_convert_type(z, jnp.uint32) | jnp.uint32(0x80000000), F32)
    log_beta = jnp.minimum(z, 0.0) - jnp.log(1.0 + jnp.exp(minus_abs))
    l = log_beta - z
    if past is not None:
        l = jnp.where(past, l, 0.0)
    return log_beta, l


def _suffix_matrix(tk):
    j = lax.broadcasted_iota(jnp.int32, (2 * tk, tk), 0) & (tk - 1)
    s = lax.broadcasted_iota(jnp.int32, (2 * tk, tk), 1)
    return (j > s).astype(BF16)


def _suffix_sum(x, u2, exact=True):
    if not exact:
        return _dot(x.astype(BF16), u2[:x.shape[1]])
    hi, lo = _split_bf16(x)
    return _dot(jnp.concatenate([hi, lo], axis=1), u2)


class _Carried:
    def __init__(self, comm, grid, last="arbitrary"):
        self.comm, self.grid, self.last = comm, grid, last
        self.n_op = len(comm.operands) if comm else 0
        self.n_tgt = len(comm.targets) if comm else 0

    def split(self, refs, n_in, n_out, n_scratch):
        self.n_in, self.n_out, self.n_scratch = n_in, n_out, n_scratch
        a = n_in + self.n_op
        b = a + n_out + self.n_tgt
        return refs[:n_in], refs[a:a + n_out], refs[b:b + n_scratch]

    def emit(self, refs):
        if self.comm is None:
            return
        step, n_steps = 0, 1
        for d, size in enumerate(self.grid):
            step = step * size + pl.program_id(d)
            n_steps *= size
        a = self.n_in + self.n_op + self.n_out
        self.comm.emit(step, n_steps, refs[self.n_in:self.n_in + len(self.comm.sources)],
                       refs[a:a + self.n_tgt], refs[a + self.n_tgt + self.n_scratch:])

    def pallas_call(self, body, *, in_specs, out_specs, out_shape, scratch_shapes, args, name):
        comm = self.comm
        n_in, n_out = len(in_specs), len(out_specs)
        aliases = {}
        if comm is not None:
            in_specs = in_specs + [ANY] * self.n_op
            out_specs = out_specs + [ANY] * self.n_tgt
            out_shape = out_shape + comm.out_shapes
            scratch_shapes = scratch_shapes + comm.sems
            args = args + comm.operands
            aliases = comm.aliases(n_in, n_out)
        sem = ("parallel",) * (len(self.grid) - 1) + (self.last,) if comm is None else ("arbitrary",) * len(self.grid)
        results = pl.pallas_call(
            body, grid=self.grid, in_specs=in_specs, out_specs=out_specs, out_shape=out_shape,
            scratch_shapes=scratch_shapes, input_output_aliases=aliases,
            compiler_params=_params(*sem), name=name)(*args)
        return results[:n_out], results[n_out:]


def _sb_fwd(qkv, *, batch, seq, name, comm=None, tq=SB_TILE, n_pre=SB_STRAIGHT, pairs=SB_PAIRS):
    T, D3 = qkv.shape
    D = D3 // 3
    nhp = D // LANES
    tk = tq
    nq = seq // tq
    scale = SB_HEAD_DIM ** -0.5
    assert 1 <= n_pre <= nq and nhp % pairs == 0

    def body(*refs):
        (q_ref, k_ref, v_ref), (o_ref,), (acc,) = carried.split(refs, 3, 1, 1)
        qi = pl.program_id(2)
        carried.emit(refs)
        qs = [_stack_heads(q_ref[:, lanes]) * scale for lanes in cols]
        past = _past_mask(tq, tk)
        u = _suffix_matrix(tk)

        def block(kb, g, c, diag):
            ks = pl.multiple_of(kb * tk, tk)
            z = _dot_nt(qs[g], k_ref[pl.ds(ks, tk), cols[g]])
            log_beta, l = _sb_logs(z, past if diag else None)
            arg = log_beta + _suffix_sum(l, u, exact=False)
            a = jnp.exp(arg if c is None else arg + c)
            if diag:
                a = jnp.where(past, a, 0.0)
            return _dot(a.astype(BF16), v_ref[pl.ds(ks, tk), cols[g]]), jnp.sum(l, axis=1, keepdims=True)

        def straight(n):
            o_sum, c = [None] * pairs, [None] * pairs
            for b in range(n):
                for g in range(pairs):
                    o_b, c_b = block(qi - b, g, c[g], b == 0)
                    o_sum[g] = o_b if b == 0 else o_sum[g] + o_b
                    c[g] = c_b if b == 0 else c[g] + c_b
            return o_sum, c

        def finish(o_sum):
            for g in range(pairs):
                o_ref[:, cols[g]] = _unstack_heads(o_sum[g], tq)

        for n in range(1, n_pre):
            @pl.when(qi == n - 1)
            def _(n=n):
                finish(straight(n)[0])

        @pl.when(qi >= n_pre - 1)
        def _():
            o_sum, c = straight(n_pre)
            for g in range(pairs):
                acc[g] = o_sum[g]

            def cond(st):
                kb, c = st
                return jnp.logical_and(kb >= 0, _max_of(c) > EXP_ZERO_BELOW)

            def step(st):
                kb, c = st
                new_c = []
                for g in range(pairs):
                    o_n, c_n = block(kb, g, c[g], False)
                    acc[g] += o_n
                    new_c.append(c[g] + c_n)
                return kb - 1, tuple(new_c)

            lax.while_loop(cond, step, (qi - n_pre, tuple(c)))
            finish([acc[g] for g in range(pairs)])

    cols = [slice(g * LANES, (g + 1) * LANES) for g in range(pairs)]
    width = pairs * LANES
    carried = _Carried(comm, (batch, nhp // pairs, nq))
    (o,), comm_results = carried.pallas_call(
        body,
        in_specs=[pl.BlockSpec((tq, width), lambda b, p, i: (b * nq + i, p)),
                  pl.BlockSpec((seq, width), lambda b, p, i: (b, nhp // pairs + p)),
                  pl.BlockSpec((seq, width), lambda b, p, i: (b, 2 * (nhp // pairs) + p))],
        out_specs=[pl.BlockSpec((tq, width), lambda b, p, i: (b * nq + i, p))],
        out_shape=[jax.ShapeDtypeStruct((T, D), F32)],
        scratch_shapes=[pltpu.VMEM((pairs, 2 * tq, LANES), F32)],
        args=[qkv, qkv, qkv], name=name)
    return o, comm_results


def _sb_bwd(qkv, o, do, *, batch, seq, name, comm=None, tq=SB_TILE, n_pre=SB_STRAIGHT, pairs=SB_PAIRS):
    T, D3 = qkv.shape
    D = D3 // 3
    nhp = D // LANES
    tk = tq
    nq = seq // tq
    scale = SB_HEAD_DIM ** -0.5

    def body(*refs):
        ins, outs, scratch = carried.split(refs, 5, 3, 3)
        q_ref, k_ref, v_ref, o_ref, do_ref = ins
        dq_ref, dk_ref, dv_ref = outs
        dq_acc, dk_acc, dv_acc = scratch
        qi = pl.program_id(2)
        carried.emit(refs)

        @pl.when(qi == 0)
        def _():
            dk_acc[...] = jnp.zeros_like(dk_acc)
            dv_acc[...] = jnp.zeros_like(dv_acc)

        qs = [_stack_heads(q_ref[:, lanes]) * scale for lanes in cols]
        dos = [_stack_heads(do_ref[:, lanes]) for lanes in cols]
        dsum = [jnp.sum(_stack_heads(do_ref[:, lanes].astype(F32) * o_ref[:, lanes]), axis=1, keepdims=True)
                for lanes in cols]
        past = _past_mask(tq, tk)
        u = _suffix_matrix(tk)

        def weights(kb, p, c, diag):
            ks = pl.multiple_of(kb * tk, tk)
            kblk = k_ref[pl.ds(ks, tk), cols[p]]
            z = _dot_nt(qs[p], kblk)
            log_beta, l = _sb_logs(z, past if diag else None)
            arg = log_beta + _suffix_sum(l, u, exact=False)
            a = jnp.exp(arg if c is None else arg + c)
            if diag:
                a = jnp.where(past, a, 0.0)
            a = a.astype(BF16)
            g = a.astype(F32) * _dot_nt(dos[p], v_ref[pl.ds(ks, tk), cols[p]])
            return ks, kblk, a, 1.0 - jnp.exp(l), g, jnp.sum(l, axis=1, keepdims=True)

        def scores_grad(p, ks, kblk, a, beta, g, prefix, diag):
            dz = g - beta * prefix
            if diag:
                dz = jnp.where(past, dz, 0.0)
            dzb = dz.astype(BF16)
            dk_acc[pl.ds(ks, tk), cols[p]] += _dot_tn(dzb, qs[p])
            dv_acc[pl.ds(ks, tk), cols[p]] += _dot_tn(a, dos[p])
            return _dot(dzb, kblk)

        def block(kb, p, c, gc):
            ks, kblk, a, beta, g, l_sum = weights(kb, p, c, False)
            prefix = dsum[p] - (_suffix_sum(g, u) + gc)
            return scores_grad(p, ks, kblk, a, beta, g, prefix, False), l_sum, jnp.sum(g, axis=1, keepdims=True)

        def finish(dq_sum):
            for p in range(pairs):
                dq_ref[:, cols[p]] = (_unstack_heads(dq_sum[p], tq) * scale).astype(BF16)

        def straight(n):
            dq_sum, c_all, g_all = [], [], []
            for p in range(pairs):
                blocks, c = [], None
                for b in range(n):
                    blk = weights(qi - b, p, c, b == 0)
                    c = blk[5] if b == 0 else c + blk[5]
                    g_bf = blk[4].astype(BF16)
                    suffix = _suffix_sum(g_bf, u, exact=False)
                    rounded_sum = suffix[:, :1] + g_bf[:, :1].astype(F32)
                    blocks.append((blk, suffix, rounded_sum, jnp.sum(blk[4], axis=1, keepdims=True)))
                exact_sum = functools.reduce(lambda x, y: x + y, [e for _, _, _, e in blocks])
                total = dsum[p] - exact_sum + functools.reduce(lambda x, y: x + y, [r for _, _, r, _ in blocks])
                dq, right = None, None
                for b, ((ks, kblk, a, beta, g, _), suffix, rounded_sum, _) in enumerate(blocks):
                    prefix = total - (suffix if right is None else suffix + right)
                    d = scores_grad(p, ks, kblk, a, beta, g, prefix, b == 0)
                    dq = d if dq is None else dq + d
                    right = rounded_sum if right is None else right + rounded_sum
                dq_sum.append(dq)
                c_all.append(c)
                g_all.append(exact_sum)
            return dq_sum, c_all, g_all

        for n in range(1, n_pre):
            @pl.when(qi == n - 1)
            def _(n=n):
                finish(straight(n)[0])

        @pl.when(qi >= n_pre - 1)
        def _():
            dq_sum, c, gc = straight(n_pre)
            for p in range(pairs):
                dq_acc[p] = dq_sum[p]

            def cond(st):
                kb, c, gc = st
                return jnp.logical_and(kb >= 0, _max_of(c) > EXP_ZERO_BELOW)

            def step(st):
                kb, c, gc = st
                new_c, new_gc = [], []
                for p in range(pairs):
                    dq_n, c_n, g_n = block(kb, p, c[p], gc[p])
                    dq_acc[p] += dq_n
                    new_c.append(c[p] + c_n)
                    new_gc.append(gc[p] + g_n)
                return kb - 1, tuple(new_c), tuple(new_gc)

            lax.while_loop(cond, step, (qi - n_pre, tuple(c), tuple(gc)))
            finish([dq_acc[p] for p in range(pairs)])

        @pl.when(qi == nq - 1)
        def _():
            dk_ref[...] = dk_acc[...].astype(BF16)
            dv_ref[...] = dv_acc[...].astype(BF16)

    cols = [slice(p * LANES, (p + 1) * LANES) for p in range(pairs)]
    width, ncb = pairs * LANES, nhp // pairs
    qspec = pl.BlockSpec((tq, width), lambda b, p, i: (b * nq + i, p))
    sspec = pl.BlockSpec((seq, width), lambda b, p, i: (b, p))
    out = jax.ShapeDtypeStruct((T, D), BF16)
    carried = _Carried(comm, (batch, ncb, nq))
    return carried.pallas_call(
        body,
        in_specs=[qspec,
                  pl.BlockSpec((seq, width), lambda b, p, i: (b, ncb + p)),
                  pl.BlockSpec((seq, width), lambda b, p, i: (b, 2 * ncb + p)),
                  qspec, qspec],
        out_specs=[qspec, sspec, sspec], out_shape=[out, out, out],
        scratch_shapes=[pltpu.VMEM((pairs, 2 * tq, LANES), F32), pltpu.VMEM((seq, width), F32),
                        pltpu.VMEM((seq, width), F32)],
        args=[qkv, qkv, qkv, o, do], name=name)


_GELU_C = 0.7978845608028654
_GELU_A = 0.044715


def _gelu(x):
    x = x.astype(BF16)
    xx = x * x
    a1 = 1.0 + jnp.tanh(x * (_GELU_C + (_GELU_C * _GELU_A) * xx))
    hx = 0.5 * x
    grad = a1 * (0.5 + (hx * (2.0 - a1)) * (_GELU_C + (3.0 * _GELU_C * _GELU_A) * xx))
    return hx * a1, grad


def _causal_ws(ws_ref, g):
    t = lax.broadcasted_iota(jnp.int32, (SGU_CHUNK, SGU_CHUNK), 0)
    s = lax.broadcasted_iota(jnp.int32, (SGU_CHUNK, SGU_CHUNK), 1)
    return jnp.where(s <= t, ws_ref[g], 0.0)


def _sgu_gate(u, v, g_ref, ws_ref, bs_ref):
    G, W = SGU_GROUPS, SGU_GROUP_W
    r = lax.rsqrt(jnp.mean(v * v, axis=-1, keepdims=True) + EPS)
    vn = (v * r * g_ref[...]).astype(BF16)
    y = []
    for g in range(G):
        sl = slice(g * W, (g + 1) * W)
        mixed = _dot(_causal_ws(ws_ref, g).astype(BF16), vn[:, sl]) + bs_ref[:, g:g + 1]
        y.append((u[:, sl] * mixed).astype(BF16))
    return jnp.concatenate(y, axis=1)


def _sgu_bwd(uv, dgelu, dy, gain, ws, bst, *, name):
    T, F2 = uv.shape
    F = F2 // 2
    C, G, W = SGU_CHUNK, SGU_GROUPS, SGU_GROUP_W
    nc = T // C

    def body(uv_ref, dgelu_ref, dy_ref, g_ref, ws_ref, bs_ref, duv_ref, dg_ref, dws_ref, dbs_ref,
             dg_acc, dws_acc, dbs_acc):
        i = pl.program_id(0)

        @pl.when(i == 0)
        def _():
            dg_acc[...] = jnp.zeros_like(dg_acc)
            dws_acc[...] = jnp.zeros_like(dws_acc)
            dbs_acc[...] = jnp.zeros_like(dbs_acc)

        u, v = uv_ref[:, :F].astype(F32), uv_ref[:, F:].astype(F32)
        dgelu = dgelu_ref[...].astype(F32)
        r = lax.rsqrt(jnp.mean(v * v, axis=-1, keepdims=True) + EPS)
        vhat = v * r
        gain_v = g_ref[...]
        vn = (vhat * gain_v).astype(BF16)
        dyv = dy_ref[...].astype(F32)
        lane8 = lax.broadcasted_iota(jnp.int32, (1, G), 1)
        dvn_parts = []
        dbs_new = jnp.zeros((C, G), F32)
        for g in range(G):
            sl = slice(g * W, (g + 1) * W)
            wsg = _causal_ws(ws_ref, g)
            mixed = _dot(wsg.astype(BF16), vn[:, sl]) + bs_ref[:, g:g + 1]
            duv_ref[:, sl] = (dyv[:, sl] * mixed * dgelu[:, sl]).astype(BF16)
            dmix = dyv[:, sl] * u[:, sl]
            dbs_new = dbs_new + jnp.where(lane8 == g, jnp.sum(dmix, axis=1, keepdims=True), 0.0)
            dmix_b = dmix.astype(BF16)
            dws_acc[g] += _dot_nt(dmix_b, vn[:, sl])
            dvn_parts.append(_dot(wsg.T.astype(BF16), dmix_b))
        dbs_acc[...] += dbs_new
        dvn = jnp.concatenate(dvn_parts, axis=1)
        dg_acc[...] += jnp.sum((dvn * vhat).reshape(C // SUBLANES, SUBLANES, F), axis=0)
        dvhat = dvn * gain_v
        dv = r * (dvhat - vhat * jnp.mean(dvhat * vhat, axis=-1, keepdims=True))
        duv_ref[:, F:] = (dv * dgelu[:, F:]).astype(BF16)

        @pl.when(i == nc - 1)
        def _():
            dg_ref[...] = jnp.sum(dg_acc[...], axis=0, keepdims=True)
            t = lax.broadcasted_iota(jnp.int32, (G, C, C), 1)
            s = lax.broadcasted_iota(jnp.int32, (G, C, C), 2)
            dws_ref[...] = jnp.where(s <= t, dws_acc[...], 0.0)
            dbs_ref[...] = dbs_acc[...]

    return pl.pallas_call(
        body, grid=(nc,),
        in_specs=[pl.BlockSpec((C, F2), lambda i: (i, 0)), pl.BlockSpec((C, F2), lambda i: (i, 0)),
                  pl.BlockSpec((C, F), lambda i: (i, 0)),
                  pl.BlockSpec((1, F), lambda i: (0, 0)), pl.BlockSpec((G, C, C), lambda i: (0, 0, 0)),
                  pl.BlockSpec((C, G), lambda i: (0, 0))],
        out_specs=[pl.BlockSpec((C, F2), lambda i: (i, 0)), pl.BlockSpec((1, F), lambda i: (0, 0)),
                   pl.BlockSpec((G, C, C), lambda i: (0, 0, 0)), pl.BlockSpec((C, G), lambda i: (0, 0))],
        out_shape=[jax.ShapeDtypeStruct((T, F2), BF16), jax.ShapeDtypeStruct((1, F), F32),
                   jax.ShapeDtypeStruct((G, C, C), F32), jax.ShapeDtypeStruct((C, G), F32)],
        scratch_shapes=[pltpu.VMEM((SUBLANES, F), F32), pltpu.VMEM((G, C, C), F32), pltpu.VMEM((C, G), F32)],
        compiler_params=_params("arbitrary"), name=name)(uv, dgelu, dy, gain, ws, bst)


def _my_place():
    return lax.axis_index("x"), lax.axis_index("y"), lax.axis_index("c")


def _all_gather(shards, *, name):
    nf = len(shards)
    items = [(lambda ins, f=f: ins[f], lambda outs, p, f=f: outs[f].at[:, p]) for f in range(nf)]
    targets = [jax.ShapeDtypeStruct((s.shape[0], N_DEV) + s.shape[1:], s.dtype) for s in shards]
    return _run_comm(_Comm(shards, targets, len(items), _gather_emit(items, [list(range(nf))])), name=name)


class _Comm:
    def __init__(self, sources, targets, n_items, emit):
        self.sources, self.targets, self.n_items, self.emit = list(sources), list(targets), n_items, emit
        self.filled = [t for t in self.targets if not isinstance(t, jax.ShapeDtypeStruct)]
        self.operands = self.sources + self.filled
        self.out_shapes = [jax.ShapeDtypeStruct(t.shape, t.dtype) for t in self.targets]
        self.sems = [pltpu.SemaphoreType.DMA((n_items, 7)), pltpu.SemaphoreType.DMA((n_items, 7)),
                     pltpu.SemaphoreType.DMA((n_items,))]

    def aliases(self, first_operand, first_result):
        pos = {id(t): k for k, t in enumerate(self.targets)}
        return {first_operand + len(self.sources) + a: first_result + pos[id(t)]
                for a, t in enumerate(self.filled)}


def _run_comm(comm, *, name):
    n_op, n_out = len(comm.operands), len(comm.targets)

    def body(*refs):
        comm.emit(0, 1, refs[:len(comm.sources)], refs[n_op:n_op + n_out], refs[n_op + n_out:])

    return pl.pallas_call(
        body, in_specs=[ANY] * n_op, out_specs=[ANY] * n_out, out_shape=comm.out_shapes,
        scratch_shapes=comm.sems, input_output_aliases=comm.aliases(0, 0), name=name)(*comm.operands)


def _at_steps(step, n_steps, phases):
    if n_steps == 1:
        for _, fn in phases:
            fn()
        return
    marks = {}
    for frac, fn in phases:
        marks.setdefault(min(int(frac * n_steps), n_steps - 1), []).append(fn)
    for mark, fns in sorted(marks.items()):
        @pl.when(step == mark)
        def _(fns=fns):
            for fn in fns:
                fn()


def _gather_emit(items, groups, fractions=None):
    if fractions is None:
        fractions = [(g + 1) / len(groups) for g in range(len(groups))]
    def emit(step, n_steps, ins, outs, sems):
        send_sems, recv_sems, local_sems = sems
        x, y, c = _my_place()
        me, sibling = (x, y, c), (x, y, 1 - c)
        chips = [(1 - x, y), (x, 1 - y), (1 - x, 1 - y)]

        def copy(i, k, block, to, own=False):
            src_of, dst_of = items[i]
            dst = dst_of(outs, 4 * block[0] + 2 * block[1] + block[2])
            return pltpu.make_async_remote_copy(
                src_ref=src_of(ins) if own else dst, dst_ref=dst,
                send_sem=send_sems.at[i, k], recv_sem=recv_sems.at[i, k],
                device_id=to, device_id_type=MESH)

        def local(i):
            src_of, dst_of = items[i]
            return pltpu.make_async_copy(src_of(ins), dst_of(outs, 4 * x + 2 * y + c), local_sems.at[i])

        def first(i):
            return [copy(i, 0, me, sibling, own=True)] + [
                copy(i, 1 + j, me, (*chip, c), own=True) for j, chip in enumerate(chips)]

        def start():
            for i in range(len(items)):
                local(i).start()
                for cp in first(i):
                    cp.start()

        def forward(group):
            for j, chip in enumerate(chips):
                for i in group:
                    copy(i, 1 + j, (*chip, c), me).wait_recv()
                    copy(i, 4 + j, (*chip, c), sibling).start()

        def finish():
            for i in range(len(items)):
                copy(i, 0, sibling, me).wait_recv()
                for j, chip in enumerate(chips):
                    copy(i, 4 + j, (*chip, 1 - c), me).wait_recv()
            for i in range(len(items)):
                for cp in first(i) + [copy(i, 4 + j, (*chip, c), sibling) for j, chip in enumerate(chips)]:
                    cp.wait_send()
                local(i).wait()

        phases = [(0.0, start)]
        for frac, group in zip(fractions, groups):
            phases.append((frac, functools.partial(forward, group)))
        phases.append((1.0, finish))
        _at_steps(step, n_steps, phases)

    return emit


def _exchange_emit(items):
    def emit(step, n_steps, ins, outs, sems):
        send_sems, recv_sems, local_sems = sems
        x, y, c = _my_place()
        me = 4 * x + 2 * y + c

        def peer_of(k):
            return x ^ (k >> 2), y ^ ((k >> 1) & 1), c ^ (k & 1)

        def copy(i, k):
            src_of, dst_of = items[i]
            px, py, pc = peer_of(k)
            return pltpu.make_async_remote_copy(
                src_ref=src_of(ins, 4 * px + 2 * py + pc), dst_ref=dst_of(outs, me),
                send_sem=send_sems.at[i, k - 1], recv_sem=recv_sems.at[i, k - 1],
                device_id=(px, py, pc), device_id_type=MESH)

        def arrival(i, k):
            src_of, dst_of = items[i]
            px, py, pc = peer_of(k)
            peer = 4 * px + 2 * py + pc
            return pltpu.make_async_remote_copy(
                src_ref=src_of(ins, peer), dst_ref=dst_of(outs, peer),
                send_sem=send_sems.at[i, k - 1], recv_sem=recv_sems.at[i, k - 1],
                device_id=(x, y, c), device_id_type=MESH)

        def local(i):
            src_of, dst_of = items[i]
            return pltpu.make_async_copy(src_of(ins, me), dst_of(outs, me), local_sems.at[i])

        def start():
            for i in range(len(items)):
                local(i).start()
            for k in range(1, N_DEV):
                for i in range(len(items)):
                    copy(i, k).start()

        def finish():
            for k in range(1, N_DEV):
                for i in range(len(items)):
                    arrival(i, k).wait_recv()
            for k in range(1, N_DEV):
                for i in range(len(items)):
                    copy(i, k).wait_send()
            for i in range(len(items)):
                local(i).wait()

        _at_steps(step, n_steps, [(0.0, start), (1.0, finish)])

    return emit


def _adam_math(g, w, m, v):
    m = ADAM_B1 * m + (1.0 - ADAM_B1) * g
    v = ADAM_B2 * v + (1.0 - ADAM_B2) * (g * g)
    m_hat = m / (1.0 - ADAM_B1 ** ADAM_STEP)
    v_hat = v / (1.0 - ADAM_B2 ** ADAM_STEP)
    delta = -ADAM_LR * (m_hat / (jnp.sqrt(v_hat) + ADAM_EPS) + ADAM_WD * w)
    return delta, m, v


def _sum_adamw(parts, w, m, v, *, name, tr=256):
    L, nd, R, C = parts.shape
    tr = min(tr, R)
    assert R % tr == 0

    def body(p_ref, w_ref, m_ref, v_ref, g_ref, d_ref, nm_ref, nv_ref):
        g = p_ref[0, 0].astype(F32)
        for q in range(1, nd):
            g = g + p_ref[0, q].astype(F32)
        d, nm, nv = _adam_math(g, w_ref[0], m_ref[0], v_ref[0])
        g_ref[0] = g
        d_ref[0] = d
        nm_ref[0] = nm
        nv_ref[0] = nv

    blk = pl.BlockSpec((1, tr, C), lambda l, i: (l, i, 0))
    out = jax.ShapeDtypeStruct((L, R, C), F32)
    return pl.pallas_call(
        body, grid=(L, R // tr),
        in_specs=[pl.BlockSpec((1, nd, tr, C), lambda l, i: (l, 0, i, 0)), blk, blk, blk],
        out_specs=[blk] * 4, out_shape=[out] * 4,
        compiler_params=_params("parallel", "parallel"), name=name)(parts, w, m, v)


def _sum_parts(parts, *, name):
    nd, R, C = parts.shape

    def body(p_ref, o_ref):
        g = p_ref[0]
        for q in range(1, nd):
            g = g + p_ref[q]
        o_ref[...] = g

    return pl.pallas_call(
        body, out_shape=jax.ShapeDtypeStruct((R, C), F32),
        in_specs=[pl.BlockSpec(memory_space=pltpu.VMEM)],
        out_specs=pl.BlockSpec(memory_space=pltpu.VMEM), name=name)(parts)


def _adamw_small(g, w, m, v, *, name):
    def body(g_ref, w_ref, m_ref, v_ref, d_ref, nm_ref, nv_ref):
        d, nm, nv = _adam_math(g_ref[...], w_ref[...], m_ref[...], v_ref[...])
        d_ref[...] = d
        nm_ref[...] = nm
        nv_ref[...] = nv

    vm = pl.BlockSpec(memory_space=pltpu.VMEM)
    out = jax.ShapeDtypeStruct(g.shape, F32)
    return pl.pallas_call(body, out_shape=[out] * 3, in_specs=[vm] * 4, out_specs=[vm] * 3,
                          name=name)(g, w, m, v)


def kernel(x, norm_mix, norm_mlp, sb_wqkv, sb_wo, sgu_win, sgu_gain, sgu_ws, sgu_bs, sgu_wout, mlp_w1, mlp_w2, final_norm, loss_target, m_norm_mix, m_norm_mlp, m_sb_wqkv, m_sb_wo, m_sgu_win, m_sgu_gain, m_sgu_ws, m_sgu_bs, m_sgu_wout, m_mlp_w1, m_mlp_w2, m_final_norm, v_norm_mix, v_norm_mlp, v_sb_wqkv, v_sb_wo, v_sgu_win, v_sgu_gain, v_sgu_ws, v_sgu_bs, v_sgu_wout, v_mlp_w1, v_mlp_w2, v_final_norm):
    batch, seq, D = x.shape
    T = batch * seq
    x0 = x.reshape(T, D)
    target = loss_target.reshape(T, D)

    WQKV, WO, WIN, WOUT, W1, W2, GAIN = range(7)
    big = [sb_wqkv, sb_wo, sgu_win, sgu_wout, mlp_w1, mlp_w2]
    shards = [w.astype(BF16) for w in big] + [sgu_gain[:, None, :]]
    rides = {
        "qkv0": [[(WO, 0), (W1, 0)]],
        "sb_fwd0": [[(WIN, 0), (WOUT, 0), (W1, 1), (W2, 1)], [(WQKV, 1)]],
        "w1_0": [[(W2, 0)]],
        "qkv2": [[(WO, 1), (W1, 2)]],
        "sb_fwd2": [[(W2, 2)], [(WIN, 1), (WOUT, 1), (W1, 3), (W2, 3)]],
    }

    first = [(lambda ins: ins[WQKV].at[0], lambda outs, p: outs[WQKV].at[0, p]),
             (lambda ins: ins[GAIN], lambda outs, p: outs[GAIN].at[:, p])]
    targets = [jax.ShapeDtypeStruct((s.shape[0], N_DEV) + s.shape[1:], s.dtype) for s in shards]
    gathered0 = _run_comm(_Comm(shards, targets, len(first), _gather_emit(first, [[0, 1]])), name="gather_first")
    gain_sgu = gathered0[GAIN].reshape(-1, 1, SGU_FFN)
    gw = dict(enumerate(gathered0[:GAIN]))

    def gather_ride(call):
        wanted = [fl for group in rides[call] for fl in group]
        fams = sorted({f for f, _ in wanted})
        items = [(lambda ins, f=f, l=l: ins[f].at[l], lambda outs, p, t=fams.index(f), l=l: outs[t].at[l, p])
                 for f, l in wanted]
        groups, sent, k = [], [], 0
        for group in rides[call]:
            groups.append(list(range(k, k + len(group))))
            sent.append(sum(shards[f].shape[1] * shards[f].shape[2] for f, _ in group) + (sent[-1] if sent else 0))
            k += len(group)
        fractions = [s / sent[-1] for s in sent]
        return _Comm(shards[:GAIN], [gw[f] for f in fams], len(items), _gather_emit(items, groups, fractions)), fams

    def weight(f):
        g = gw[f]
        return g if f in (WQKV, WIN, W1) else g.reshape(g.shape[0], 1, N_DEV * g.shape[2], g.shape[3])

    saved = []
    xs = x0
    for i in range(DEPTH):
        j = i // 2
        if i % 2 == 0:
            comm, fams = gather_ride(f"qkv{i}")
            (qkv, h), results = _mm_nn(xs, weight(WQKV), j, out_dtype=BF16, gain=norm_mix[i:i + 1],
                                       name=f"qkv{i}", comm=comm, tm=1024)
            gw.update(zip(fams, results))
            comm, fams = gather_ride(f"sb_fwd{i}")
            o, results = _sb_fwd(qkv, batch=batch, seq=seq, name=f"sb_fwd{i}", comm=comm)
            gw.update(zip(fams, results))
            x_mid = _mm_nn(o, weight(WO), j, out_dtype=F32, res=xs, name=f"wo{i}", tm=1024)
            mix = (h, qkv, o)
        else:
            gain_j = gain_sgu[j]
            bst = sgu_bs[j].T
            uv, h, dgelu, yv = _mm_nn(xs, weight(WIN), j, out_dtype=BF16, gain=norm_mix[i:i + 1],
                                      sgu=(gain_j, sgu_ws[j], bst), name=f"win_sgu{i}")
            x_mid = _mm_nn(yv, weight(WOUT), j, out_dtype=F32, res=xs, name=f"wout{i}", tm=1024)
            mix = (h, uv, dgelu, yv, gain_j, bst)
        if f"w1_{i}" in rides:
            comm, fams = gather_ride(f"w1_{i}")
            (a, h2), results = _mm_nn(x_mid, weight(W1), i, out_dtype=BF16, gain=norm_mlp[i:i + 1],
                                      name=f"w1_{i}", comm=comm, tm=1024)
            gw.update(zip(fams, results))
        else:
            a, h2 = _mm_nn(x_mid, weight(W1), i, out_dtype=BF16, gain=norm_mlp[i:i + 1], name=f"w1_{i}", tm=1024)
        saved.append((xs, mix, x_mid, h2, a))
        if i < DEPTH - 1:
            xs = _mm_nn(a, weight(W2), i, out_dtype=F32, res=x_mid, a_act="relu2", name=f"w2_{i}")
    g_wqkv, g_wo, g_win, g_wout, g_w1, g_w2 = [weight(f) for f in range(GAIN)]

    dx, dxb, sq, d_final = _mlp_out_loss_head(a, g_w2, DEPTH - 1, x_mid, final_norm.reshape(1, D), target,
                                              name="w2_loss_head")

    SMALL = GAIN
    stacks = {f: jax.ShapeDtypeStruct((w.shape[0], N_DEV) + w.shape[1:], BF16) for f, w in enumerate(big)}
    pending = []
    second_half = []

    def row_shards(p):
        return p.reshape(N_DEV, p.shape[1] // N_DEV, p.shape[2])

    def exchange(going):
        fams = sorted({f for _, f, _, _ in going})

        def item(a, f, l, rows):
            rows = slice(None) if rows is None else slice(*rows)
            return (lambda ins, p: ins[a].at[p, rows], lambda outs, q: outs[fams.index(f)].at[l, q, rows])

        items = [item(a, f, l, rows) for a, (_, f, l, rows) in enumerate(going)]
        return _Comm([g[0] for g in going], [stacks[f] for f in fams], len(items), _exchange_emit(items)), fams

    def take(entries):
        going = list(entries)
        entries.clear()
        return going

    d_norm_mix, d_norm_mlp = [None] * DEPTH, [None] * DEPTH
    d_gain, d_ws, d_bs = [None] * 2, [None] * 2, [None] * 2
    for i in reversed(range(DEPTH)):
        j = i // 2
        xs, mix, x_mid, h2, a = saved[i]
        da = _mm_nt(dxb, g_w2, i, out_dtype=BF16, act_src=a, name=f"d_a{i}")
        pending.append((row_shards(_mm_tn(a, dxb, shards=1, a_act="relu2", name=f"d_w2_{i}", tm=1024)), W2, i, None))
        pending.append((_mm_tn(h2, da, shards=N_DEV, name=f"d_w1_{i}", pb=4, tm=1024), W1, i, None))
        comm, fams = exchange(take(second_half)) if second_half else (None, [])
        (dx, dxb, d_norm_mlp[i]), results = _mm_nt_rms_bwd(
            da, g_w1, i, x_mid, norm_mlp[i:i + 1], dx, name=f"d_h2_{i}", comm=comm)
        stacks.update(zip(fams, results))
        if i % 2 == 0:
            h, qkv, o = mix
            do = _mm_nt(dxb, g_wo, j, out_dtype=BF16, name=f"d_o{i}", tm=1024)
            pending.append((row_shards(_mm_tn(o, dxb, shards=1, name=f"d_wo{i}")), WO, j, None))
            comm, fams = exchange(take(pending))
            (dq, dk, dv), results = _sb_bwd(qkv, o, do, batch=batch, seq=seq, name=f"sb_bwd{i}", comm=comm)
            stacks.update(zip(fams, results))
            mixer_in, w_in, f_in = [dq, dk, dv], g_wqkv, WQKV
            d_w_in = _mm_tn(h, mixer_in, shards=N_DEV, name=f"d_wqkv{i}", pb=N_DEV, tm=1024, tk=512)
        else:
            h, uv, dgelu, yv, gain_j, bst = mix
            dy = _mm_nt(dxb, g_wout, j, out_dtype=BF16, name=f"d_y{i}", tm=1024)
            pending.append((row_shards(_mm_tn(yv, dxb, shards=1, name=f"d_wout{i}")), WOUT, j, None))
            mixer_in, d_gain[j], d_ws[j], dbst = _sgu_bwd(uv, dgelu, dy, gain_j, sgu_ws[j], bst, name=f"sgu_bwd{i}")
            d_bs[j] = dbst.T
            w_in, f_in = g_win, WIN
            d_w_in = _mm_tn(h, mixer_in, shards=N_DEV, name=f"d_win{i}", pb=4, tm=1024)
        rows = d_w_in.shape[1]
        if i > 0:
            second_half.append((d_w_in, f_in, j, (rows // 2, rows)))
            comm, fams = exchange([(d_w_in, f_in, j, (0, rows // 2))])
        else:
            comm, fams = exchange([(d_w_in, f_in, j, None)])
        (dx, dxb, d_norm_mix[i]), results = _mm_nt_rms_bwd(
            mixer_in, w_in, j, xs, norm_mix[i:i + 1], dx, name=f"d_h_mix{i}", comm=comm)
        stacks.update(zip(fams, results))
    grad_x = dx.reshape(batch, seq, D)

    small = [jnp.concatenate(d_norm_mix, 0), jnp.concatenate(d_norm_mlp, 0), d_final,
             jnp.concatenate(d_gain, 0), jnp.stack(d_bs, 0), jnp.stack(d_ws, 0)]
    small_flat = jnp.concatenate([s.reshape(-1) for s in small] + [(0.5 * jnp.sum(sq) / D).reshape(1)])
    n_small = small_flat.shape[0]
    small_rows = -(-n_small // (N_DEV * SUBLANES * LANES)) * SUBLANES
    small_size = N_DEV * small_rows * LANES
    small_flat = jnp.pad(small_flat, (0, small_size - n_small))
    stacks[SMALL] = jax.ShapeDtypeStruct((1, N_DEV, small_rows, LANES), F32)
    pending.append((small_flat.reshape(N_DEV, small_rows, LANES), SMALL, 0, None))
    comm, fams = exchange(take(pending))
    stacks.update(zip(fams, _run_comm(comm, name="exchange_last")))
    r_wqkv, r_wo, r_win, r_wout, r_w1, r_w2, r_small = [stacks[f] for f in range(SMALL + 1)]

    u_wqkv = _sum_adamw(r_wqkv, sb_wqkv, m_sb_wqkv, v_sb_wqkv, name="adamw_wqkv")
    u_wo = _sum_adamw(r_wo, sb_wo, m_sb_wo, v_sb_wo, name="adamw_wo")
    u_win = _sum_adamw(r_win, sgu_win, m_sgu_win, v_sgu_win, name="adamw_win")
    u_wout = _sum_adamw(r_wout, sgu_wout, m_sgu_wout, v_sgu_wout, name="adamw_wout")
    u_w1 = _sum_adamw(r_w1, mlp_w1, m_mlp_w1, v_mlp_w1, name="adamw_w1")
    u_w2 = _sum_adamw(r_w2, mlp_w2, m_mlp_w2, v_mlp_w2, name="adamw_w2")

    small_sum = _sum_parts(r_small[0], name="sum_small")
    g_small = _all_gather([small_sum[None]], name="gather_small")[0].reshape(-1)
    loss = g_small[n_small - 1]

    shapes = [s.shape for s in small]
    sizes = [s.size for s in small]
    offs = [sum(sizes[:k]) for k in range(len(sizes))]
    me = 4 * lax.axis_index("x") + 2 * lax.axis_index("y") + lax.axis_index("c")
    shard_w = SGU_FFN // N_DEV

    def pack(arrs):
        flat = jnp.concatenate([a_.reshape(-1) for a_ in arrs])
        return jnp.pad(flat, (0, small_size - flat.shape[0])).reshape(-1, LANES)

    def full_gain(gshard):
        return lax.dynamic_update_slice(jnp.zeros((2, SGU_FFN), F32), gshard, (0, me * shard_w))

    w_small = pack([norm_mix, norm_mlp, final_norm, full_gain(sgu_gain), sgu_bs, sgu_ws])
    m_small = pack([m_norm_mix, m_norm_mlp, m_final_norm, full_gain(m_sgu_gain), m_sgu_bs, m_sgu_ws])
    v_small = pack([v_norm_mix, v_norm_mlp, v_final_norm, full_gain(v_sgu_gain), v_sgu_bs, v_sgu_ws])
    g_pack = g_small.reshape(-1, LANES)
    sm = [g_pack] + list(_adamw_small(g_pack, w_small, m_small, v_small, name="adamw_small"))

    def unpack(flat2d):
        flat = flat2d.reshape(-1)
        out = [flat[offs[k]:offs[k] + sizes[k]].reshape(shapes[k]) for k in range(len(sizes))]
        out[2] = out[2].reshape(D)
        out[3] = lax.dynamic_slice(out[3], (0, me * shard_w), (2, shard_w))
        return out

    outs = []
    for k, big_u in enumerate(zip(u_wqkv, u_wo, u_win, u_wout, u_w1, u_w2)):
        s_nm, s_nl, s_fn, s_gain, s_bs, s_ws = unpack(sm[k])
        b_wqkv, b_wo, b_win, b_wout, b_w1, b_w2 = big_u
        outs += [s_nm, s_nl, b_wqkv, b_wo, b_win, s_gain, s_ws, s_bs, b_wout, b_w1, b_w2, s_fn]
    return (loss, grad_x, *outs)
```

```python
import functools

import jax
import jax.numpy as jnp
from jax import lax
from jax.experimental import pallas as pl
from jax.experimental.pallas import tpu as pltpu

F32 = jnp.float32
BF16 = jnp.bfloat16

N_DEV = 8
D_MODEL = 1024
SEQ = 2048
DEPTH = 4
SB_HEAD_DIM = 64
SGU_CHUNK = 128
SGU_FFN = 2 * D_MODEL
SGU_GROUPS = 8
SGU_GROUP_W = SGU_FFN // SGU_GROUPS
EPS = 1e-6

ADAM_LR = 0.001
ADAM_B1 = 0.9
ADAM_B2 = 0.999
ADAM_EPS = 1e-08
ADAM_WD = 0.01
ADAM_STEP = 10

MXU_TILE = 256
LANES = 128
SUBLANES = 8
VMEM_LIMIT = 56 * 1024 * 1024
EXP_ZERO_BELOW = -104.0
SB_TILE = 256
SB_STRAIGHT = 2
SB_PAIRS = 4

MESH = pl.DeviceIdType.MESH
ANY = pl.BlockSpec(memory_space=pl.ANY)


def _params(*sem):
    return pltpu.CompilerParams(dimension_semantics=sem, vmem_limit_bytes=VMEM_LIMIT)


def _dot(a, b):
    return jnp.dot(a, b, preferred_element_type=F32)


def _dot_nt(a, b):
    return lax.dot_general(a, b, (((1,), (1,)), ((), ())), preferred_element_type=F32)


def _dot_tn(a, b):
    return lax.dot_general(a, b, (((0,), (0,)), ((), ())), preferred_element_type=F32)


def _split_bf16(x):
    hi = x.astype(BF16)
    lo = (x - hi.astype(F32)).astype(BF16)
    return hi, lo


def _relu2(av):
    t = jnp.maximum(av, jnp.zeros_like(av))
    return t * t


def _mm_nn(a, w, l, *, out_dtype, name, res=None, a_act=None, gain=None, sgu=None, comm=None,
           tm=512, kc=1024):
    M, K = a.shape
    _, P, K2, n = w.shape
    assert K2 == K
    tm, kc = min(tm, M), min(kc, K)
    assert M % tm == 0 and K % kc == 0
    join = 2 if n % MXU_TILE and P % 2 == 0 else 1

    n_main = 2 + (gain is not None) + (res is not None)
    n_in = n_main + (3 if sgu is not None else 0)
    n_out = 1 + (gain is not None) + (2 if sgu is not None else 0)
    gelu = sgu is not None

    def body(*all_refs):
        ins, outs, _ = carried.split(all_refs, n_in, n_out, 0)
        carried.emit(all_refs)
        refs = list(ins) + list(outs)
        a_ref, w_ref, o_ref = refs[0], refs[1], refs[n_in]
        if gain is not None:
            xv = a_ref[...]
            r = lax.rsqrt(jnp.mean(xv * xv, axis=-1, keepdims=True) + EPS)
            h = (xv * r * refs[2][...]).astype(BF16)
            refs[n_in + 1][...] = h
        for p in range(0, P, join):
            sl = slice(p * n, (p + join) * n)
            acc = None
            for k0 in range(0, K, kc):
                if gain is not None:
                    av = h[:, k0:k0 + kc]
                else:
                    av = a_ref[:, k0:k0 + kc]
                    av = _relu2(av) if a_act == "relu2" else av.astype(BF16)
                wv = [w_ref[0, p + s, k0:k0 + kc, :] for s in range(join)]
                d = _dot(av, wv[0] if join == 1 else jnp.concatenate(wv, axis=1))
                acc = d if acc is None else acc + d
            if res is not None:
                acc = acc + refs[n_main - 1][:, sl]
            if gelu:
                acc, dact = _gelu(acc)
                refs[-2][:, sl] = dact.astype(BF16)
            o_ref[:, sl] = acc.astype(out_dtype)
        if sgu is not None:
            half = P * n // 2
            for r0 in range(0, tm, SGU_CHUNK):
                rows = slice(r0, r0 + SGU_CHUNK)
                refs[-1][rows, :] = _sgu_gate(o_ref[rows, :half].astype(F32), o_ref[rows, half:].astype(F32),
                                              *refs[n_main:n_in])

    row = lambda width: pl.BlockSpec((tm, width), lambda i: (i, 0))
    whole = lambda arr: pl.BlockSpec(arr.shape, lambda i: (0,) * arr.ndim)
    in_specs = [row(K), pl.BlockSpec((1, P, K, n), lambda i: (l, 0, 0, 0), pipeline_mode=pl.Buffered(1))]
    args = [a, w]
    if gain is not None:
        in_specs.append(whole(gain))
        args.append(gain)
    if res is not None:
        in_specs.append(row(P * n))
        args.append(res)
    out_specs, out_shape = [row(P * n)], [jax.ShapeDtypeStruct((M, P * n), out_dtype)]
    if gain is not None:
        out_specs.append(row(K))
        out_shape.append(jax.ShapeDtypeStruct((M, K), BF16))
    if sgu is not None:
        in_specs += [whole(s) for s in sgu]
        args += list(sgu)
        out_specs += [row(P * n), row(P * n // 2)]
        out_shape += [jax.ShapeDtypeStruct((M, P * n), BF16), jax.ShapeDtypeStruct((M, P * n // 2), BF16)]
    carried = _Carried(comm, (M // tm,), last="parallel")
    outs, comm_results = carried.pallas_call(
        body, in_specs=in_specs, out_specs=out_specs, out_shape=out_shape, scratch_shapes=[],
        args=args, name=name)
    outs = tuple(outs) if n_out > 1 else outs[0]
    return outs if comm is None else (outs, comm_results)


def _mm_nt(a, w, l, *, out_dtype, name, act_src=None, tm=512, tn=1024):
    M, K = a.shape
    _, P, Nout, kc = w.shape
    assert P == 1 and K == kc
    tm, tn = min(tm, M), min(tn, Nout)
    assert M % tm == 0 and Nout % tn == 0

    def body(*refs):
        a_ref, w_ref, o_ref = refs[0], refs[1], refs[-1]
        av = a_ref[...].astype(BF16)
        for n0 in range(0, Nout, tn):
            r = _dot_nt(av, w_ref[0, 0, n0:n0 + tn, :])
            if act_src is not None:
                r = r * (2.0 * jnp.maximum(refs[2][:, n0:n0 + tn].astype(F32), 0.0))
            o_ref[:, n0:n0 + tn] = r.astype(out_dtype)

    row = lambda width: pl.BlockSpec((tm, width), lambda i: (i, 0))
    in_specs = [row(K), pl.BlockSpec((1, 1, Nout, K), lambda i: (l, 0, 0, 0), pipeline_mode=pl.Buffered(1))]
    args = [a, w]
    if act_src is not None:
        in_specs.append(row(Nout))
        args.append(act_src)
    return pl.pallas_call(
        body, grid=(M // tm,), in_specs=in_specs, out_specs=row(Nout),
        out_shape=jax.ShapeDtypeStruct((M, Nout), out_dtype),
        compiler_params=_params("parallel"), name=name)(*args)


def _mm_tn(a, b, *, shards, name, a_act=None, tm=2048, tk=1024, pb=1):
    b_parts = list(b) if isinstance(b, (list, tuple)) else [b]
    M, K = a.shape
    N = sum(part.shape[1] for part in b_parts)
    assert all(part.shape[0] == M for part in b_parts)
    n = N // shards
    tm, tk = min(tm, M), min(tk, K)
    assert M % tm == 0 and K % tk == 0 and shards % pb == 0
    width, nm = pb * n, M // tm
    assert len(b_parts) == 1 or width == N

    def body(*refs):
        a_ref, b_refs, o_ref, acc = refs[0], refs[1:-2], refs[-2], refs[-1]
        m = pl.program_id(2)

        def product():
            av = a_ref[...]
            if a_act == "relu2":
                av = _relu2(av)
            bv = [r[...].astype(BF16) for r in b_refs]
            return _dot_tn(av.astype(BF16), bv[0] if len(bv) == 1 else jnp.concatenate(bv, axis=1))

        def write(total):
            for p in range(pb):
                o_ref[p] = total[:, p * n:(p + 1) * n].astype(BF16)

        if nm == 1:
            write(product())
            return

        @pl.when(m == 0)
        def _():
            acc[...] = product()

        @pl.when(jnp.logical_and(m > 0, m < nm - 1))
        def _():
            acc[...] += product()

        @pl.when(m == nm - 1)
        def _():
            write(acc[...] + product())

    if len(b_parts) == 1:
        b_specs = [pl.BlockSpec((tm, width), lambda i, j, m: (m, j))]
    else:
        b_specs = [pl.BlockSpec((tm, part.shape[1]), lambda i, j, m: (m, 0)) for part in b_parts]
    return pl.pallas_call(
        body, grid=(K // tk, N // width, nm),
        in_specs=[pl.BlockSpec((tm, tk), lambda i, j, m: (m, i))] + b_specs,
        out_specs=pl.BlockSpec((pb, tk, n), lambda i, j, m: (j, i, 0)),
        out_shape=jax.ShapeDtypeStruct((shards, K, n), BF16),
        scratch_shapes=[pltpu.VMEM((tk, width), F32)],
        compiler_params=_params("parallel", "parallel", "arbitrary"), name=name)(a, *b_parts)


def _mm_nt_rms_bwd(a, w, l, x, gain, dres, *, name, comm=None, tm=512):
    a_parts = list(a) if isinstance(a, (list, tuple)) else [a]
    na = len(a_parts)
    M, K = a_parts[0].shape[0], sum(part.shape[1] for part in a_parts)
    _, P, D, kc = w.shape
    assert K == P * kc and x.shape == (M, D)
    tm = min(tm, M)
    nr = M // tm
    join = 2 if kc % MXU_TILE and P % 2 == 0 else 1

    def body(*refs):
        ins, (dx_ref, dxb_ref, dg_ref), (acc,) = carried.split(refs, na + 4, 3, 1)
        a_refs, (w_ref, x_ref, g_ref, dres_ref) = ins[:na], ins[na:]
        i = pl.program_id(0)
        carried.emit(refs)
        av = a_refs[0][...] if na == 1 else jnp.concatenate([r[...] for r in a_refs], axis=1)
        dhv = None
        for p in range(0, P, join):
            wv = [w_ref[0, p + s] for s in range(join)]
            d = _dot_nt(av[:, p * kc:(p + join) * kc], wv[0] if join == 1 else jnp.concatenate(wv, axis=1))
            dhv = d if dhv is None else dhv + d
        xv = x_ref[...]
        r = lax.rsqrt(jnp.mean(xv * xv, axis=-1, keepdims=True) + EPS)
        xhat = xv * r
        dxhat = dhv * g_ref[...]
        dx = dres_ref[...] + r * (dxhat - xhat * jnp.mean(dxhat * xhat, axis=-1, keepdims=True))
        dx_ref[...] = dx
        dxb_ref[...] = dx.astype(BF16)
        part = jnp.sum((dhv * xhat).reshape(tm // SUBLANES, SUBLANES, D), axis=0)

        @pl.when(i == 0)
        def _():
            acc[...] = jnp.zeros_like(acc)

        acc[...] += part

        @pl.when(i == nr - 1)
        def _():
            dg_ref[...] = jnp.sum(acc[...], axis=0, keepdims=True)

    row = pl.BlockSpec((tm, D), lambda i: (i, 0))
    vec = pl.BlockSpec((1, D), lambda i: (0, 0))
    carried = _Carried(comm, (nr,))
    return carried.pallas_call(
        body,
        in_specs=[pl.BlockSpec((tm, part.shape[1]), lambda i: (i, 0)) for part in a_parts] + [
            pl.BlockSpec((1, P, D, kc), lambda i: (l, 0, 0, 0), pipeline_mode=pl.Buffered(1)),
            row, vec, row],
        out_specs=[row, row, vec],
        out_shape=[jax.ShapeDtypeStruct((M, D), F32), jax.ShapeDtypeStruct((M, D), BF16),
                   jax.ShapeDtypeStruct((1, D), F32)],
        scratch_shapes=[pltpu.VMEM((SUBLANES, D), F32)],
        args=a_parts + [w, x, gain, dres], name=name)


def _mlp_out_loss_head(a, w, l, res, gain, target, *, name, tr=512, kc=1024):
    T, K = a.shape
    D = w.shape[3]
    nr = T // tr

    def body(a_ref, w_ref, res_ref, g_ref, t_ref, dx_ref, dxb_ref, sq_ref, dg_ref, sq_acc, dg_acc):
        i = pl.program_id(0)
        xv = res_ref[...]
        for k0 in range(0, K, kc):
            xv = xv + _dot(_relu2(a_ref[:, k0:k0 + kc]), w_ref[0, 0, k0:k0 + kc, :])
        g = g_ref[...]
        r = lax.rsqrt(jnp.mean(xv * xv, axis=-1, keepdims=True) + EPS)
        xhat = xv * r
        err = xhat * g - t_ref[...]
        dy = err * (1.0 / D)
        dxhat = dy * g
        dx = r * (dxhat - xhat * jnp.mean(dxhat * xhat, axis=-1, keepdims=True))
        dx_ref[...] = dx
        dxb_ref[...] = dx.astype(BF16)
        sq = jnp.sum((err * err).reshape(tr // SUBLANES, SUBLANES, D), axis=0)
        dg = jnp.sum((dy * xhat).reshape(tr // SUBLANES, SUBLANES, D), axis=0)

        @pl.when(i == 0)
        def _():
            sq_acc[...] = sq
            dg_acc[...] = dg

        @pl.when(i > 0)
        def _():
            sq_acc[...] += sq
            dg_acc[...] += dg

        @pl.when(i == nr - 1)
        def _():
            sq_ref[...] = sq_acc[...]
            dg_ref[...] = jnp.sum(dg_acc[...], axis=0, keepdims=True)

    row = pl.BlockSpec((tr, D), lambda i: (i, 0))
    vec = pl.BlockSpec((1, D), lambda i: (0, 0))
    part = pl.BlockSpec((SUBLANES, D), lambda i: (0, 0))
    return pl.pallas_call(
        body, grid=(nr,),
        in_specs=[pl.BlockSpec((tr, K), lambda i: (i, 0)),
                  pl.BlockSpec((1, 1, K, D), lambda i: (l, 0, 0, 0), pipeline_mode=pl.Buffered(1)),
                  row, vec, row],
        out_specs=[row, row, part, vec],
        out_shape=[jax.ShapeDtypeStruct((T, D), F32), jax.ShapeDtypeStruct((T, D), BF16),
                   jax.ShapeDtypeStruct((SUBLANES, D), F32), jax.ShapeDtypeStruct((1, D), F32)],
        scratch_shapes=[pltpu.VMEM((SUBLANES, D), F32), pltpu.VMEM((SUBLANES, D), F32)],
        compiler_params=_params("arbitrary"), name=name)(a, w, res, gain, target)


def _head0_lanes():
    return lax.broadcasted_iota(jnp.int32, (1, LANES), 1) < SB_HEAD_DIM


def _stack_heads(x):
    zero = jnp.zeros_like(x)
    h0 = _head0_lanes()
    return jnp.concatenate([jnp.where(h0, x, zero), jnp.where(h0, zero, x)], axis=0)


def _unstack_heads(y, tq):
    return jnp.where(_head0_lanes(), y[:tq], y[tq:])


def _past_mask(tq, tk):
    row = lax.broadcasted_iota(jnp.int32, (2 * tq, tk), 0) & (tq - 1)
    col = lax.broadcasted_iota(jnp.int32, (2 * tq, tk), 1)
    return col < row


def _max_of(arrays):
    return functools.reduce(jnp.maximum, [jnp.max(a) for a in arrays])


def _sb_logs(z, past):
    minus_abs = lax.bitcast_convert_type(
        lax.bitcast_convert_type(z, jnp.uint32) | jnp.uint32(0x80000000), F32)
    log_beta = jnp.minimum(z, 0.0) - jnp.log(1.0 + jnp.exp(minus_abs))
    l = log_beta - z
    if past is not None:
        l = jnp.where(past, l, 0.0)
    return log_beta, l


def _suffix_matrix(tk):
    j = lax.broadcasted_iota(jnp.int32, (2 * tk, tk), 0) & (tk - 1)
    s = lax.broadcasted_iota(jnp.int32, (2 * tk, tk), 1)
    return (j > s).astype(BF16)


def _suffix_sum(x, u2, exact=True):
    if not exact:
        return _dot(x.astype(BF16), u2[:x.shape[1]])
    hi, lo = _split_bf16(x)
    return _dot(jnp.concatenate([hi, lo], axis=1), u2)


class _Carried:
    def __init__(self, comm, grid, last="arbitrary"):
        self.comm, self.grid, self.last = comm, grid, last
        self.n_op = len(comm.operands) if comm else 0
        self.n_tgt = len(comm.targets) if comm else 0

    def split(self, refs, n_in, n_out, n_scratch):
        self.n_in, self.n_out, self.n_scratch = n_in, n_out, n_scratch
        a = n_in + self.n_op
        b = a + n_out + self.n_tgt
        return refs[:n_in], refs[a:a + n_out], refs[b:b + n_scratch]

    def emit(self, refs):
        if self.comm is None:
            return
        step, n_steps = 0, 1
        for d, size in enumerate(self.grid):
            step = step * size + pl.program_id(d)
            n_steps *= size
        a = self.n_in + self.n_op + self.n_out
        self.comm.emit(step, n_steps, refs[self.n_in:self.n_in + len(self.comm.sources)],
                       refs[a:a + self.n_tgt], refs[a + self.n_tgt + self.n_scratch:])

    def pallas_call(self, body, *, in_specs, out_specs, out_shape, scratch_shapes, args, name):
        comm = self.comm
        n_in, n_out = len(in_specs), len(out_specs)
        aliases = {}
        if comm is not None:
            in_specs = in_specs + [ANY] * self.n_op
            out_specs = out_specs + [ANY] * self.n_tgt
            out_shape = out_shape + comm.out_shapes
            scratch_shapes = scratch_shapes + comm.sems
            args = args + comm.operands
            aliases = comm.aliases(n_in, n_out)
        sem = ("parallel",) * (len(self.grid) - 1) + (self.last,) if comm is None else ("arbitrary",) * len(self.grid)
        results = pl.pallas_call(
            body, grid=self.grid, in_specs=in_specs, out_specs=out_specs, out_shape=out_shape,
            scratch_shapes=scratch_shapes, input_output_aliases=aliases,
            compiler_params=_params(*sem), name=name)(*args)
        return results[:n_out], results[n_out:]


def _sb_fwd(qkv, *, batch, seq, name, comm=None, tq=SB_TILE, n_pre=SB_STRAIGHT, pairs=SB_PAIRS):
    T, D3 = qkv.shape
    D = D3 // 3
    nhp = D // LANES
    tk = tq
    nq = seq // tq
    scale = SB_HEAD_DIM ** -0.5
    assert 1 <= n_pre <= nq and nhp % pairs == 0

    def body(*refs):
        (q_ref, k_ref, v_ref), (o_ref,), (acc,) = carried.split(refs, 3, 1, 1)
        qi = pl.program_id(2)
        carried.emit(refs)
        qs = [_stack_heads(q_ref[:, lanes]) * scale for lanes in cols]
        past = _past_mask(tq, tk)
        u = _suffix_matrix(tk)

        def block(kb, g, c, diag):
            ks = pl.multiple_of(kb * tk, tk)
            z = _dot_nt(qs[g], k_ref[pl.ds(ks, tk), cols[g]])
            log_beta, l = _sb_logs(z, past if diag else None)
            arg = log_beta + _suffix_sum(l, u, exact=False)
            a = jnp.exp(arg if c is None else arg + c)
            if diag:
                a = jnp.where(past, a, 0.0)
            return _dot(a.astype(BF16), v_ref[pl.ds(ks, tk), cols[g]]), jnp.sum(l, axis=1, keepdims=True)

        def straight(n):
            o_sum, c = [None] * pairs, [None] * pairs
            for b in range(n):
                for g in range(pairs):
                    o_b, c_b = block(qi - b, g, c[g], b == 0)
                    o_sum[g] = o_b if b == 0 else o_sum[g] + o_b
                    c[g] = c_b if b == 0 else c[g] + c_b
            return o_sum, c

        def finish(o_sum):
            for g in range(pairs):
                o_ref[:, cols[g]] = _unstack_heads(o_sum[g], tq)

        for n in range(1, n_pre):
            @pl.when(qi == n - 1)
            def _(n=n):
                finish(straight(n)[0])

        @pl.when(qi >= n_pre - 1)
        def _():
            o_sum, c = straight(n_pre)
            for g in range(pairs):
                acc[g] = o_sum[g]

            def cond(st):
                kb, c = st
                return jnp.logical_and(kb >= 0, _max_of(c) > EXP_ZERO_BELOW)

            def step(st):
                kb, c = st
                new_c = []
                for g in range(pairs):
                    o_n, c_n = block(kb, g, c[g], False)
                    acc[g] += o_n
                    new_c.append(c[g] + c_n)
                return kb - 1, tuple(new_c)

            lax.while_loop(cond, step, (qi - n_pre, tuple(c)))
            finish([acc[g] for g in range(pairs)])

    cols = [slice(g * LANES, (g + 1) * LANES) for g in range(pairs)]
    width = pairs * LANES
    carried = _Carried(comm, (batch, nhp // pairs, nq))
    (o,), comm_results = carried.pallas_call(
        body,
        in_specs=[pl.BlockSpec((tq, width), lambda b, p, i: (b * nq + i, p)),
                  pl.BlockSpec((seq, width), lambda b, p, i: (b, nhp // pairs + p)),
                  pl.BlockSpec((seq, width), lambda b, p, i: (b, 2 * (nhp // pairs) + p))],
        out_specs=[pl.BlockSpec((tq, width), lambda b, p, i: (b * nq + i, p))],
        out_shape=[jax.ShapeDtypeStruct((T, D), F32)],
        scratch_shapes=[pltpu.VMEM((pairs, 2 * tq, LANES), F32)],
        args=[qkv, qkv, qkv], name=name)
    return o, comm_results


def _sb_bwd(qkv, o, do, *, batch, seq, name, comm=None, tq=SB_TILE, n_pre=SB_STRAIGHT, pairs=SB_PAIRS):
    T, D3 = qkv.shape
    D = D3 // 3
    nhp = D // LANES
    tk = tq
    nq = seq // tq
    scale = SB_HEAD_DIM ** -0.5

    def body(*refs):
        ins, outs, scratch = carried.split(refs, 5, 3, 3)
        q_ref, k_ref, v_ref, o_ref, do_ref = ins
        dq_ref, dk_ref, dv_ref = outs
        dq_acc, dk_acc, dv_acc = scratch
        qi = pl.program_id(2)
        carried.emit(refs)

        @pl.when(qi == 0)
        def _():
            dk_acc[...] = jnp.zeros_like(dk_acc)
            dv_acc[...] = jnp.zeros_like(dv_acc)

        qs = [_stack_heads(q_ref[:, lanes]) * scale for lanes in cols]
        dos = [_stack_heads(do_ref[:, lanes]) for lanes in cols]
        dsum = [jnp.sum(_stack_heads(do_ref[:, lanes].astype(F32) * o_ref[:, lanes]), axis=1, keepdims=True)
                for lanes in cols]
        past = _past_mask(tq, tk)
        u = _suffix_matrix(tk)

        def weights(kb, p, c, diag):
            ks = pl.multiple_of(kb * tk, tk)
            kblk = k_ref[pl.ds(ks, tk), cols[p]]
            z = _dot_nt(qs[p], kblk)
            log_beta, l = _sb_logs(z, past if diag else None)
            arg = log_beta + _suffix_sum(l, u, exact=False)
            a = jnp.exp(arg if c is None else arg + c)
            if diag:
                a = jnp.where(past, a, 0.0)
            a = a.astype(BF16)
            g = a.astype(F32) * _dot_nt(dos[p], v_ref[pl.ds(ks, tk), cols[p]])
            return ks, kblk, a, 1.0 - jnp.exp(l), g, jnp.sum(l, axis=1, keepdims=True)

        def scores_grad(p, ks, kblk, a, beta, g, prefix, diag):
            dz = g - beta * prefix
            if diag:
                dz = jnp.where(past, dz, 0.0)
            dzb = dz.astype(BF16)
            dk_acc[pl.ds(ks, tk), cols[p]] += _dot_tn(dzb, qs[p])
            dv_acc[pl.ds(ks, tk), cols[p]] += _dot_tn(a, dos[p])
            return _dot(dzb, kblk)

        def block(kb, p, c, gc):
            ks, kblk, a, beta, g, l_sum = weights(kb, p, c, False)
            prefix = dsum[p] - (_suffix_sum(g, u) + gc)
            return scores_grad(p, ks, kblk, a, beta, g, prefix, False), l_sum, jnp.sum(g, axis=1, keepdims=True)

        def finish(dq_sum):
            for p in range(pairs):
                dq_ref[:, cols[p]] = (_unstack_heads(dq_sum[p], tq) * scale).astype(BF16)

        def straight(n):
            dq_sum, c_all, g_all = [], [], []
            for p in range(pairs):
                blocks, c = [], None
                for b in range(n):
                    blk = weights(qi - b, p, c, b == 0)
                    c = blk[5] if b == 0 else c + blk[5]
                    g_bf = blk[4].astype(BF16)
                    suffix = _suffix_sum(g_bf, u, exact=False)
                    rounded_sum = suffix[:, :1] + g_bf[:, :1].astype(F32)
                    blocks.append((blk, suffix, rounded_sum, jnp.sum(blk[4], axis=1, keepdims=True)))
                exact_sum = functools.reduce(lambda x, y: x + y, [e for _, _, _, e in blocks])
                total = dsum[p] - exact_sum + functools.reduce(lambda x, y: x + y, [r for _, _, r, _ in blocks])
                dq, right = None, None
                for b, ((ks, kblk, a, beta, g, _), suffix, rounded_sum, _) in enumerate(blocks):
                    prefix = total - (suffix if right is None else suffix + right)
                    d = scores_grad(p, ks, kblk, a, beta, g, prefix, b == 0)
                    dq = d if dq is None else dq + d
                    right = rounded_sum if right is None else right + rounded_sum
                dq_sum.append(dq)
                c_all.append(c)
                g_all.append(exact_sum)
            return dq_sum, c_all, g_all

        for n in range(1, n_pre):
            @pl.when(qi == n - 1)
            def _(n=n):
                finish(straight(n)[0])

        @pl.when(qi >= n_pre - 1)
        def _():
            dq_sum, c, gc = straight(n_pre)
            for p in range(pairs):
                dq_acc[p] = dq_sum[p]

            def cond(st):
                kb, c, gc = st
                return jnp.logical_and(kb >= 0, _max_of(c) > EXP_ZERO_BELOW)

            def step(st):
                kb, c, gc = st
                new_c, new_gc = [], []
                for p in range(pairs):
                    dq_n, c_n, g_n = block(kb, p, c[p], gc[p])
                    dq_acc[p] += dq_n
                    new_c.append(c[p] + c_n)
                    new_gc.append(gc[p] + g_n)
                return kb - 1, tuple(new_c), tuple(new_gc)

            lax.while_loop(cond, step, (qi - n_pre, tuple(c), tuple(gc)))
            finish([dq_acc[p] for p in range(pairs)])

        @pl.when(qi == nq - 1)
        def _():
            dk_ref[...] = dk_acc[...].astype(BF16)
            dv_ref[...] = dv_acc[...].astype(BF16)

    cols = [slice(p * LANES, (p + 1) * LANES) for p in range(pairs)]
    width, ncb = pairs * LANES, nhp // pairs
    qspec = pl.BlockSpec((tq, width), lambda b, p, i: (b * nq + i, p))
    sspec = pl.BlockSpec((seq, width), lambda b, p, i: (b, p))
    out = jax.ShapeDtypeStruct((T, D), BF16)
    carried = _Carried(comm, (batch, ncb, nq))
    return carried.pallas_call(
        body,
        in_specs=[qspec,
                  pl.BlockSpec((seq, width), lambda b, p, i: (b, ncb + p)),
                  pl.BlockSpec((seq, width), lambda b, p, i: (b, 2 * ncb + p)),
                  qspec, qspec],
        out_specs=[qspec, sspec, sspec], out_shape=[out, out, out],
        scratch_shapes=[pltpu.VMEM((pairs, 2 * tq, LANES), F32), pltpu.VMEM((seq, width), F32),
                        pltpu.VMEM((seq, width), F32)],
        args=[qkv, qkv, qkv, o, do], name=name)


_GELU_C = 0.7978845608028654
_GELU_A = 0.044715


def _gelu(x):
    x = x.astype(BF16)
    xx = x * x
    a1 = 1.0 + jnp.tanh(x * (_GELU_C + (_GELU_C * _GELU_A) * xx))
    hx = 0.5 * x
    grad = a1 * (0.5 + (hx * (2.0 - a1)) * (_GELU_C + (3.0 * _GELU_C * _GELU_A) * xx))
    return hx * a1, grad


def _causal_ws(ws_ref, g):
    t = lax.broadcasted_iota(jnp.int32, (SGU_CHUNK, SGU_CHUNK), 0)
    s = lax.broadcasted_iota(jnp.int32, (SGU_CHUNK, SGU_CHUNK), 1)
    return jnp.where(s <= t, ws_ref[g], 0.0)


def _sgu_gate(u, v, g_ref, ws_ref, bs_ref):
    G, W = SGU_GROUPS, SGU_GROUP_W
    r = lax.rsqrt(jnp.mean(v * v, axis=-1, keepdims=True) + EPS)
    vn = (v * r * g_ref[...]).astype(BF16)
    y = []
    for g in range(G):
        sl = slice(g * W, (g + 1) * W)
        mixed = _dot(_causal_ws(ws_ref, g).astype(BF16), vn[:, sl]) + bs_ref[:, g:g + 1]
        y.append((u[:, sl] * mixed).astype(BF16))
    return jnp.concatenate(y, axis=1)


def _sgu_bwd(uv, dgelu, dy, gain, ws, bst, *, name):
    T, F2 = uv.shape
    F = F2 // 2
    C, G, W = SGU_CHUNK, SGU_GROUPS, SGU_GROUP_W
    nc = T // C

    def body(uv_ref, dgelu_ref, dy_ref, g_ref, ws_ref, bs_ref, duv_ref, dg_ref, dws_ref, dbs_ref,
             dg_acc, dws_acc, dbs_acc):
        i = pl.program_id(0)

        @pl.when(i == 0)
        def _():
            dg_acc[...] = jnp.zeros_like(dg_acc)
            dws_acc[...] = jnp.zeros_like(dws_acc)
            dbs_acc[...] = jnp.zeros_like(dbs_acc)

        u, v = uv_ref[:, :F].astype(F32), uv_ref[:, F:].astype(F32)
        dgelu = dgelu_ref[...].astype(F32)
        r = lax.rsqrt(jnp.mean(v * v, axis=-1, keepdims=True) + EPS)
        vhat = v * r
        gain_v = g_ref[...]
        vn = (vhat * gain_v).astype(BF16)
        dyv = dy_ref[...].astype(F32)
        lane8 = lax.broadcasted_iota(jnp.int32, (1, G), 1)
        dvn_parts = []
        dbs_new = jnp.zeros((C, G), F32)
        for g in range(G):
            sl = slice(g * W, (g + 1) * W)
            wsg = _causal_ws(ws_ref, g)
            mixed = _dot(wsg.astype(BF16), vn[:, sl]) + bs_ref[:, g:g + 1]
            duv_ref[:, sl] = (dyv[:, sl] * mixed * dgelu[:, sl]).astype(BF16)
            dmix = dyv[:, sl] * u[:, sl]
            dbs_new = dbs_new + jnp.where(lane8 == g, jnp.sum(dmix, axis=1, keepdims=True), 0.0)
            dmix_b = dmix.astype(BF16)
            dws_acc[g] += _dot_nt(dmix_b, vn[:, sl])
            dvn_parts.append(_dot(wsg.T.astype(BF16), dmix_b))
        dbs_acc[...] += dbs_new
        dvn = jnp.concatenate(dvn_parts, axis=1)
        dg_acc[...] += jnp.sum((dvn * vhat).reshape(C // SUBLANES, SUBLANES, F), axis=0)
        dvhat = dvn * gain_v
        dv = r * (dvhat - vhat * jnp.mean(dvhat * vhat, axis=-1, keepdims=True))
        duv_ref[:, F:] = (dv * dgelu[:, F:]).astype(BF16)

        @pl.when(i == nc - 1)
        def _():
            dg_ref[...] = jnp.sum(dg_acc[...], axis=0, keepdims=True)
            t = lax.broadcasted_iota(jnp.int32, (G, C, C), 1)
            s = lax.broadcasted_iota(jnp.int32, (G, C, C), 2)
            dws_ref[...] = jnp.where(s <= t, dws_acc[...], 0.0)
            dbs_ref[...] = dbs_acc[...]

    return pl.pallas_call(
        body, grid=(nc,),
        in_specs=[pl.BlockSpec((C, F2), lambda i: (i, 0)), pl.BlockSpec((C, F2), lambda i: (i, 0)),
                  pl.BlockSpec((C, F), lambda i: (i, 0)),
                  pl.BlockSpec((1, F), lambda i: (0, 0)), pl.BlockSpec((G, C, C), lambda i: (0, 0, 0)),
                  pl.BlockSpec((C, G), lambda i: (0, 0))],
        out_specs=[pl.BlockSpec((C, F2), lambda i: (i, 0)), pl.BlockSpec((1, F), lambda i: (0, 0)),
                   pl.BlockSpec((G, C, C), lambda i: (0, 0, 0)), pl.BlockSpec((C, G), lambda i: (0, 0))],
        out_shape=[jax.ShapeDtypeStruct((T, F2), BF16), jax.ShapeDtypeStruct((1, F), F32),
                   jax.ShapeDtypeStruct((G, C, C), F32), jax.ShapeDtypeStruct((C, G), F32)],
        scratch_shapes=[pltpu.VMEM((SUBLANES, F), F32), pltpu.VMEM((G, C, C), F32), pltpu.VMEM((C, G), F32)],
        compiler_params=_params("arbitrary"), name=name)(uv, dgelu, dy, gain, ws, bst)


def _my_place():
    return lax.axis_index("x"), lax.axis_index("y"), lax.axis_index("c")


def _all_gather(shards, *, name):
    nf = len(shards)
    items = [(lambda ins, f=f: ins[f], lambda outs, p, f=f: outs[f].at[:, p]) for f in range(nf)]
    targets = [jax.ShapeDtypeStruct((s.shape[0], N_DEV) + s.shape[1:], s.dtype) for s in shards]
    return _run_comm(_Comm(shards, targets, len(items), _gather_emit(items, [list(range(nf))])), name=name)


class _Comm:
    def __init__(self, sources, targets, n_items, emit):
        self.sources, self.targets, self.n_items, self.emit = list(sources), list(targets), n_items, emit
        self.filled = [t for t in self.targets if not isinstance(t, jax.ShapeDtypeStruct)]
        self.operands = self.sources + self.filled
        self.out_shapes = [jax.ShapeDtypeStruct(t.shape, t.dtype) for t in self.targets]
        self.sems = [pltpu.SemaphoreType.DMA((n_items, 7)), pltpu.SemaphoreType.DMA((n_items, 7)),
                     pltpu.SemaphoreType.DMA((n_items,))]

    def aliases(self, first_operand, first_result):
        pos = {id(t): k for k, t in enumerate(self.targets)}
        return {first_operand + len(self.sources) + a: first_result + pos[id(t)]
                for a, t in enumerate(self.filled)}


def _run_comm(comm, *, name):
    n_op, n_out = len(comm.operands), len(comm.targets)

    def body(*refs):
        comm.emit(0, 1, refs[:len(comm.sources)], refs[n_op:n_op + n_out], refs[n_op + n_out:])

    return pl.pallas_call(
        body, in_specs=[ANY] * n_op, out_specs=[ANY] * n_out, out_shape=comm.out_shapes,
        scratch_shapes=comm.sems, input_output_aliases=comm.aliases(0, 0), name=name)(*comm.operands)


def _at_steps(step, n_steps, phases):
    if n_steps == 1:
        for _, fn in phases:
            fn()
        return
    marks = {}
    for frac, fn in phases:
        marks.setdefault(min(int(frac * n_steps), n_steps - 1), []).append(fn)
    for mark, fns in sorted(marks.items()):
        @pl.when(step == mark)
        def _(fns=fns):
            for fn in fns:
                fn()


def _gather_emit(items, groups, fractions=None):
    if fractions is None:
        fractions = [(g + 1) / len(groups) for g in range(len(groups))]
    def emit(step, n_steps, ins, outs, sems):
        send_sems, recv_sems, local_sems = sems
        x, y, c = _my_place()
        me, sibling = (x, y, c), (x, y, 1 - c)
        chips = [(1 - x, y), (x, 1 - y), (1 - x, 1 - y)]

        def copy(i, k, block, to, own=False):
            src_of, dst_of = items[i]
            dst = dst_of(outs, 4 * block[0] + 2 * block[1] + block[2])
            return pltpu.make_async_remote_copy(
                src_ref=src_of(ins) if own else dst, dst_ref=dst,
                send_sem=send_sems.at[i, k], recv_sem=recv_sems.at[i, k],
                device_id=to, device_id_type=MESH)

        def local(i):
            src_of, dst_of = items[i]
            return pltpu.make_async_copy(src_of(ins), dst_of(outs, 4 * x + 2 * y + c), local_sems.at[i])

        def first(i):
            return [copy(i, 0, me, sibling, own=True)] + [
                copy(i, 1 + j, me, (*chip, c), own=True) for j, chip in enumerate(chips)]

        def start():
            for i in range(len(items)):
                local(i).start()
                for cp in first(i):
                    cp.start()

        def forward(group):
            for j, chip in enumerate(chips):
                for i in group:
                    copy(i, 1 + j, (*chip, c), me).wait_recv()
                    copy(i, 4 + j, (*chip, c), sibling).start()

        def finish():
            for i in range(len(items)):
                copy(i, 0, sibling, me).wait_recv()
                for j, chip in enumerate(chips):
                    copy(i, 4 + j, (*chip, 1 - c), me).wait_recv()
            for i in range(len(items)):
                for cp in first(i) + [copy(i, 4 + j, (*chip, c), sibling) for j, chip in enumerate(chips)]:
                    cp.wait_send()
                local(i).wait()

        phases = [(0.0, start)]
        for frac, group in zip(fractions, groups):
            phases.append((frac, functools.partial(forward, group)))
        phases.append((1.0, finish))
        _at_steps(step, n_steps, phases)

    return emit


def _exchange_emit(items):
    def emit(step, n_steps, ins, outs, sems):
        send_sems, recv_sems, local_sems = sems
        x, y, c = _my_place()
        me = 4 * x + 2 * y + c

        def peer_of(k):
            return x ^ (k >> 2), y ^ ((k >> 1) & 1), c ^ (k & 1)

        def copy(i, k):
            src_of, dst_of = items[i]
            px, py, pc = peer_of(k)
            return pltpu.make_async_remote_copy(
                src_ref=src_of(ins, 4 * px + 2 * py + pc), dst_ref=dst_of(outs, me),
                send_sem=send_sems.at[i, k - 1], recv_sem=recv_sems.at[i, k - 1],
                device_id=(px, py, pc), device_id_type=MESH)

        def arrival(i, k):
            src_of, dst_of = items[i]
            px, py, pc = peer_of(k)
            peer = 4 * px + 2 * py + pc
            return pltpu.make_async_remote_copy(
                src_ref=src_of(ins, peer), dst_ref=dst_of(outs, peer),
                send_sem=send_sems.at[i, k - 1], recv_sem=recv_sems.at[i, k - 1],
                device_id=(x, y, c), device_id_type=MESH)

        def local(i):
            src_of, dst_of = items[i]
            return pltpu.make_async_copy(src_of(ins, me), dst_of(outs, me), local_sems.at[i])

        def start():
            for i in range(len(items)):
                local(i).start()
            for k in range(1, N_DEV):
                for i in range(len(items)):
                    copy(i, k).start()

        def finish():
            for k in range(1, N_DEV):
                for i in range(len(items)):
                    arrival(i, k).wait_recv()
            for k in range(1, N_DEV):
                for i in range(len(items)):
                    copy(i, k).wait_send()
            for i in range(len(items)):
                local(i).wait()

        _at_steps(step, n_steps, [(0.0, start), (1.0, finish)])

    return emit


def _adam_math(g, w, m, v):
    m = ADAM_B1 * m + (1.0 - ADAM_B1) * g
    v = ADAM_B2 * v + (1.0 - ADAM_B2) * (g * g)
    m_hat = m / (1.0 - ADAM_B1 ** ADAM_STEP)
    v_hat = v / (1.0 - ADAM_B2 ** ADAM_STEP)
    delta = -ADAM_LR * (m_hat / (jnp.sqrt(v_hat) + ADAM_EPS) + ADAM_WD * w)
    return delta, m, v


def _sum_adamw(parts, w, m, v, *, name, tr=256):
    L, nd, R, C = parts.shape
    tr = min(tr, R)
    assert R % tr == 0

    def body(p_ref, w_ref, m_ref, v_ref, g_ref, d_ref, nm_ref, nv_ref):
        g = p_ref[0, 0].astype(F32)
        for q in range(1, nd):
            g = g + p_ref[0, q].astype(F32)
        d, nm, nv = _adam_math(g, w_ref[0], m_ref[0], v_ref[0])
        g_ref[0] = g
        d_ref[0] = d
        nm_ref[0] = nm
        nv_ref[0] = nv

    blk = pl.BlockSpec((1, tr, C), lambda l, i: (l, i, 0))
    out = jax.ShapeDtypeStruct((L, R, C), F32)
    return pl.pallas_call(
        body, grid=(L, R // tr),
        in_specs=[pl.BlockSpec((1, nd, tr, C), lambda l, i: (l, 0, i, 0)), blk, blk, blk],
        out_specs=[blk] * 4, out_shape=[out] * 4,
        compiler_params=_params("parallel", "parallel"), name=name)(parts, w, m, v)


def _sum_parts(parts, *, name):
    nd, R, C = parts.shape

    def body(p_ref, o_ref):
        g = p_ref[0]
        for q in range(1, nd):
            g = g + p_ref[q]
        o_ref[...] = g

    return pl.pallas_call(
        body, out_shape=jax.ShapeDtypeStruct((R, C), F32),
        in_specs=[pl.BlockSpec(memory_space=pltpu.VMEM)],
        out_specs=pl.BlockSpec(memory_space=pltpu.VMEM), name=name)(parts)


def _adamw_small(g, w, m, v, *, name):
    def body(g_ref, w_ref, m_ref, v_ref, d_ref, nm_ref, nv_ref):
        d, nm, nv = _adam_math(g_ref[...], w_ref[...], m_ref[...], v_ref[...])
        d_ref[...] = d
        nm_ref[...] = nm
        nv_ref[...] = nv

    vm = pl.BlockSpec(memory_space=pltpu.VMEM)
    out = jax.ShapeDtypeStruct(g.shape, F32)
    return pl.pallas_call(body, out_shape=[out] * 3, in_specs=[vm] * 4, out_specs=[vm] * 3,
                          name=name)(g, w, m, v)


def kernel(x, norm_mix, norm_mlp, sb_wqkv, sb_wo, sgu_win, sgu_gain, sgu_ws, sgu_bs, sgu_wout, mlp_w1, mlp_w2, final_norm, loss_target, m_norm_mix, m_norm_mlp, m_sb_wqkv, m_sb_wo, m_sgu_win, m_sgu_gain, m_sgu_ws, m_sgu_bs, m_sgu_wout, m_mlp_w1, m_mlp_w2, m_final_norm, v_norm_mix, v_norm_mlp, v_sb_wqkv, v_sb_wo, v_sgu_win, v_sgu_gain, v_sgu_ws, v_sgu_bs, v_sgu_wout, v_mlp_w1, v_mlp_w2, v_final_norm):
    batch, seq, D = x.shape
    T = batch * seq
    x0 = x.reshape(T, D)
    target = loss_target.reshape(T, D)

    WQKV, WO, WIN, WOUT, W1, W2, GAIN = range(7)
    big = [sb_wqkv, sb_wo, sgu_win, sgu_wout, mlp_w1, mlp_w2]
    shards = [w.astype(BF16) for w in big] + [sgu_gain[:, None, :]]
    rides = {
        "qkv0": [[(WO, 0), (W1, 0)]],
        "sb_fwd0": [[(WIN, 0), (WOUT, 0), (W1, 1), (W2, 1)], [(WQKV, 1)]],
        "w1_0": [[(W2, 0)]],
        "qkv2": [[(WO, 1), (W1, 2)]],
        "sb_fwd2": [[(W2, 2)], [(WIN, 1), (WOUT, 1), (W1, 3), (W2, 3)]],
    }

    first = [(lambda ins: ins[WQKV].at[0], lambda outs, p: outs[WQKV].at[0, p]),
             (lambda ins: ins[GAIN], lambda outs, p: outs[GAIN].at[:, p])]
    targets = [jax.ShapeDtypeStruct((s.shape[0], N_DEV) + s.shape[1:], s.dtype) for s in shards]
    gathered0 = _run_comm(_Comm(shards, targets, len(first), _gather_emit(first, [[0, 1]])), name="gather_first")
    gain_sgu = gathered0[GAIN].reshape(-1, 1, SGU_FFN)
    gw = dict(enumerate(gathered0[:GAIN]))

    def gather_ride(call):
        wanted = [fl for group in rides[call] for fl in group]
        fams = sorted({f for f, _ in wanted})
        items = [(lambda ins, f=f, l=l: ins[f].at[l], lambda outs, p, t=fams.index(f), l=l: outs[t].at[l, p])
                 for f, l in wanted]
        groups, sent, k = [], [], 0
        for group in rides[call]:
            groups.append(list(range(k, k + len(group))))
            sent.append(sum(shards[f].shape[1] * shards[f].shape[2] for f, _ in group) + (sent[-1] if sent else 0))
            k += len(group)
        fractions = [s / sent[-1] for s in sent]
        return _Comm(shards[:GAIN], [gw[f] for f in fams], len(items), _gather_emit(items, groups, fractions)), fams

    def weight(f):
        g = gw[f]
        return g if f in (WQKV, WIN, W1) else g.reshape(g.shape[0], 1, N_DEV * g.shape[2], g.shape[3])

    saved = []
    xs = x0
    for i in range(DEPTH):
        j = i // 2
        if i % 2 == 0:
            comm, fams = gather_ride(f"qkv{i}")
            (qkv, h), results = _mm_nn(xs, weight(WQKV), j, out_dtype=BF16, gain=norm_mix[i:i + 1],
                                       name=f"qkv{i}", comm=comm, tm=1024)
            gw.update(zip(fams, results))
            comm, fams = gather_ride(f"sb_fwd{i}")
            o, results = _sb_fwd(qkv, batch=batch, seq=seq, name=f"sb_fwd{i}", comm=comm)
            gw.update(zip(fams, results))
            x_mid = _mm_nn(o, weight(WO), j, out_dtype=F32, res=xs, name=f"wo{i}", tm=1024)
            mix = (h, qkv, o)
        else:
            gain_j = gain_sgu[j]
            bst = sgu_bs[j].T
            uv, h, dgelu, yv = _mm_nn(xs, weight(WIN), j, out_dtype=BF16, gain=norm_mix[i:i + 1],
                                      sgu=(gain_j, sgu_ws[j], bst), name=f"win_sgu{i}")
            x_mid = _mm_nn(yv, weight(WOUT), j, out_dtype=F32, res=xs, name=f"wout{i}", tm=1024)
            mix = (h, uv, dgelu, yv, gain_j, bst)
        if f"w1_{i}" in rides:
            comm, fams = gather_ride(f"w1_{i}")
            (a, h2), results = _mm_nn(x_mid, weight(W1), i, out_dtype=BF16, gain=norm_mlp[i:i + 1],
                                      name=f"w1_{i}", comm=comm, tm=1024)
            gw.update(zip(fams, results))
        else:
            a, h2 = _mm_nn(x_mid, weight(W1), i, out_dtype=BF16, gain=norm_mlp[i:i + 1], name=f"w1_{i}", tm=1024)
        saved.append((xs, mix, x_mid, h2, a))
        if i < DEPTH - 1:
            xs = _mm_nn(a, weight(W2), i, out_dtype=F32, res=x_mid, a_act="relu2", name=f"w2_{i}")
    g_wqkv, g_wo, g_win, g_wout, g_w1, g_w2 = [weight(f) for f in range(GAIN)]

    dx, dxb, sq, d_final = _mlp_out_loss_head(a, g_w2, DEPTH - 1, x_mid, final_norm.reshape(1, D), target,
                                              name="w2_loss_head")

    SMALL = GAIN
    stacks = {f: jax.ShapeDtypeStruct((w.shape[0], N_DEV) + w.shape[1:], BF16) for f, w in enumerate(big)}
    pending = []
    second_half = []

    def row_shards(p):
        return p.reshape(N_DEV, p.shape[1] // N_DEV, p.shape[2])

    def exchange(going):
        fams = sorted({f for _, f, _, _ in going})

        def item(a, f, l, rows):
            rows = slice(None) if rows is None else slice(*rows)
            return (lambda ins, p: ins[a].at[p, rows], lambda outs, q: outs[fams.index(f)].at[l, q, rows])

        items = [item(a, f, l, rows) for a, (_, f, l, rows) in enumerate(going)]
        return _Comm([g[0] for g in going], [stacks[f] for f in fams], len(items), _exchange_emit(items)), fams

    def take(entries):
        going = list(entries)
        entries.clear()
        return going

    d_norm_mix, d_norm_mlp = [None] * DEPTH, [None] * DEPTH
    d_gain, d_ws, d_bs = [None] * 2, [None] * 2, [None] * 2
    for i in reversed(range(DEPTH)):
        j = i // 2
        xs, mix, x_mid, h2, a = saved[i]
        da = _mm_nt(dxb, g_w2, i, out_dtype=BF16, act_src=a, name=f"d_a{i}")
        pending.append((row_shards(_mm_tn(a, dxb, shards=1, a_act="relu2", name=f"d_w2_{i}")), W2, i, None))
        pending.append((_mm_tn(h2, da, shards=N_DEV, name=f"d_w1_{i}", pb=4, tm=1024), W1, i, None))
        comm, fams = exchange(take(second_half)) if second_half else (None, [])
        (dx, dxb, d_norm_mlp[i]), results = _mm_nt_rms_bwd(
            da, g_w1, i, x_mid, norm_mlp[i:i + 1], dx, name=f"d_h2_{i}", comm=comm)
        stacks.update(zip(fams, results))
        if i % 2 == 0:
            h, qkv, o = mix
            do = _mm_nt(dxb, g_wo, j, out_dtype=BF16, name=f"d_o{i}", tm=1024)
            pending.append((row_shards(_mm_tn(o, dxb, shards=1, name=f"d_wo{i}")), WO, j, None))
            comm, fams = exchange(take(pending))
            (dq, dk, dv), results = _sb_bwd(qkv, o, do, batch=batch, seq=seq, name=f"sb_bwd{i}", comm=comm)
            stacks.update(zip(fams, results))
            mixer_in, w_in, f_in = [dq, dk, dv], g_wqkv, WQKV
            d_w_in = _mm_tn(h, mixer_in, shards=N_DEV, name=f"d_wqkv{i}", pb=N_DEV, tm=1024, tk=512)
        else:
            h, uv, dgelu, yv, gain_j, bst = mix
            dy = _mm_nt(dxb, g_wout, j, out_dtype=BF16, name=f"d_y{i}", tm=1024)
            pending.append((row_shards(_mm_tn(yv, dxb, shards=1, name=f"d_wout{i}")), WOUT, j, None))
            mixer_in, d_gain[j], d_ws[j], dbst = _sgu_bwd(uv, dgelu, dy, gain_j, sgu_ws[j], bst, name=f"sgu_bwd{i}")
            d_bs[j] = dbst.T
            w_in, f_in = g_win, WIN
            d_w_in = _mm_tn(h, mixer_in, shards=N_DEV, name=f"d_win{i}", pb=4, tm=1024)
        rows = d_w_in.shape[1]
        if i > 0:
            second_half.append((d_w_in, f_in, j, (rows // 2, rows)))
            comm, fams = exchange([(d_w_in, f_in, j, (0, rows // 2))])
        else:
            comm, fams = exchange([(d_w_in, f_in, j, None)])
        (dx, dxb, d_norm_mix[i]), results = _mm_nt_rms_bwd(
            mixer_in, w_in, j, xs, norm_mix[i:i + 1], dx, name=f"d_h_mix{i}", comm=comm)
        stacks.update(zip(fams, results))
    grad_x = dx.reshape(batch, seq, D)

    small = [jnp.concatenate(d_norm_mix, 0), jnp.concatenate(d_norm_mlp, 0), d_final,
             jnp.concatenate(d_gain, 0), jnp.stack(d_bs, 0), jnp.stack(d_ws, 0)]
    small_flat = jnp.concatenate([s.reshape(-1) for s in small] + [(0.5 * jnp.sum(sq) / D).reshape(1)])
    n_small = small_flat.shape[0]
    small_rows = -(-n_small // (N_DEV * SUBLANES * LANES)) * SUBLANES
    small_size = N_DEV * small_rows * LANES
    small_flat = jnp.pad(small_flat, (0, small_size - n_small))
    stacks[SMALL] = jax.ShapeDtypeStruct((1, N_DEV, small_rows, LANES), F32)
    pending.append((small_flat.reshape(N_DEV, small_rows, LANES), SMALL, 0, None))
    comm, fams = exchange(take(pending))
    stacks.update(zip(fams, _run_comm(comm, name="exchange_last")))
    r_wqkv, r_wo, r_win, r_wout, r_w1, r_w2, r_small = [stacks[f] for f in range(SMALL + 1)]

    u_wqkv = _sum_adamw(r_wqkv, sb_wqkv, m_sb_wqkv, v_sb_wqkv, name="adamw_wqkv")
    u_wo = _sum_adamw(r_wo, sb_wo, m_sb_wo, v_sb_wo, name="adamw_wo")
    u_win = _sum_adamw(r_win, sgu_win, m_sgu_win, v_sgu_win, name="adamw_win")
    u_wout = _sum_adamw(r_wout, sgu_wout, m_sgu_wout, v_sgu_wout, name="adamw_wout")
    u_w1 = _sum_adamw(r_w1, mlp_w1, m_mlp_w1, v_mlp_w1, name="adamw_w1")
    u_w2 = _sum_adamw(r_w2, mlp_w2, m_mlp_w2, v_mlp_w2, name="adamw_w2")

    small_sum = _sum_parts(r_small[0], name="sum_small")
    g_small = _all_gather([small_sum[None]], name="gather_small")[0].reshape(-1)
    loss = g_small[n_small - 1]

    shapes = [s.shape for s in small]
    sizes = [s.size for s in small]
    offs = [sum(sizes[:k]) for k in range(len(sizes))]
    me = 4 * lax.axis_index("x") + 2 * lax.axis_index("y") + lax.axis_index("c")
    shard_w = SGU_FFN // N_DEV

    def pack(arrs):
        flat = jnp.concatenate([a_.reshape(-1) for a_ in arrs])
        return jnp.pad(flat, (0, small_size - flat.shape[0])).reshape(-1, LANES)

    def full_gain(gshard):
        return lax.dynamic_update_slice(jnp.zeros((2, SGU_FFN), F32), gshard, (0, me * shard_w))

    w_small = pack([norm_mix, norm_mlp, final_norm, full_gain(sgu_gain), sgu_bs, sgu_ws])
    m_small = pack([m_norm_mix, m_norm_mlp, m_final_norm, full_gain(m_sgu_gain), m_sgu_bs, m_sgu_ws])
    v_small = pack([v_norm_mix, v_norm_mlp, v_final_norm, full_gain(v_sgu_gain), v_sgu_bs, v_sgu_ws])
    g_pack = g_small.reshape(-1, LANES)
    sm = [g_pack] + list(_adamw_small(g_pack, w_small, m_small, v_small, name="adamw_small"))

    def unpack(flat2d):
        flat = flat2d.reshape(-1)
        out = [flat[offs[k]:offs[k] + sizes[k]].reshape(shapes[k]) for k in range(len(sizes))]
        out[2] = out[2].reshape(D)
        out[3] = lax.dynamic_slice(out[3], (0, me * shard_w), (2, shard_w))
        return out

    outs = []
    for k, big_u in enumerate(zip(u_wqkv, u_wo, u_win, u_wout, u_w1, u_w2)):
        s_nm, s_nl, s_fn, s_gain, s_bs, s_ws = unpack(sm[k])
        b_wqkv, b_wo, b_win, b_wout, b_w1, b_w2 = big_u
        outs += [s_nm, s_nl, b_wqkv, b_wo, b_win, s_gain, s_ws, s_bs, b_wout, b_w1, b_w2, s_fn]
    return (loss, grad_x, *outs)
```
